```python
import jax, jax.numpy as jnp
from jax import lax
import numpy as np

D_MODEL = 2048
BATCH = 16
SEQ = 2048
DEPTH = 1

N_META = 16
NORM_EPS = 1e-6
D_FF = ((8 * D_MODEL // 3 + 255) // 256) * 256
RWKV_HEAD = 64
RWKV_HEADS = D_MODEL // RWKV_HEAD
RWKV_WIDTH = RWKV_HEADS * RWKV_HEAD
W_LORA = 96
A_LORA = 96
G_LORA = 256
GN_EPS = RWKV_HEAD * 1e-5
MLA_HEADS = D_MODEL // 128
Q_LORA = 512
KV_LORA = 512
NOPE_DIM = 128
ROPE_DIM = 64
V_DIM = 128
QK_DIM = NOPE_DIM + ROPE_DIM
ROPE_THETA = 10000.0
Q_BLOCK = 128
RWKV_COLS = 3 * RWKV_WIDTH + W_LORA + A_LORA + G_LORA
MLA_COLS = Q_LORA + KV_LORA + ROPE_DIM
GATE_COLS = 2 * D_MODEL
IN_COLS = RWKV_COLS + MLA_COLS + GATE_COLS

kernel_name = "macaron_rwkv7_mla_gated_hybrid"


def _split(x, sizes):
    idx = np.cumsum(sizes)[:-1].tolist()
    return jnp.split(x, idx, axis=-1)


def rms_norm(x, g):
    xf = x.astype(jnp.float32)
    y = xf * lax.rsqrt(jnp.mean(xf * xf, axis=-1, keepdims=True) + NORM_EPS)
    return (y * g.astype(jnp.float32)).astype(x.dtype)


def swiglu(h, w_gate, w_up, w_down):
    return (jax.nn.silu(h @ w_gate) * (h @ w_up)) @ w_down


def rope(x, cos, sin):
    x1, x2 = jnp.split(x.astype(jnp.float32), 2, axis=-1)
    c, s = cos[:, None, :], sin[:, None, :]
    return jnp.concatenate([x1 * c - x2 * s, x1 * s + x2 * c], axis=-1).astype(x.dtype)


def wkv7_scan(r, w, k, v, kk_neg, b):
    bsz, _, h, n = r.shape

    def step(S, inp):
        r_t, w_t, k_t, v_t, kn_t, b_t = inp
        sa = jnp.einsum("bhij,bhj->bhi", S, kn_t)
        S = S * w_t[:, :, None, :] + sa[..., None] * b_t[:, :, None, :] + v_t[..., None] * k_t[:, :, None, :]
        return S, jnp.einsum("bhij,bhj->bhi", S, r_t)

    xs = tuple(jnp.moveaxis(t, 1, 0) for t in (r, w, k, v, kk_neg, b))
    S0 = jnp.zeros((bsz, h, n, n), jnp.float32)
    _, y = lax.scan(step, S0, xs)
    return jnp.moveaxis(y, 0, 1)


def rwkv7_branch(p, mu, w0, w_up, a0, a_up, g_up, k_k, k_a, r_k, gn_w, gn_b):
    bsz, t, _ = p.shape
    prev = jnp.pad(p, ((0, 0), (1, 0), (0, 0)))[:, :-1]
    p = p + mu * (prev - p)
    r, k, v, xw, xa, xg = _split(p, (RWKV_WIDTH, RWKV_WIDTH, RWKV_WIDTH, W_LORA, A_LORA, G_LORA))
    w_pre = -jax.nn.softplus(-(w0 + jnp.tanh(xw) @ w_up)) - 0.5
    decay = jnp.exp(-jnp.exp(w_pre.astype(jnp.float32)))
    a = jax.nn.sigmoid(a0 + xa @ a_up)
    g = jax.nn.sigmoid(xg) @ g_up
    heads = lambda z: z.reshape(bsz, t, RWKV_HEADS, RWKV_HEAD).astype(jnp.float32)
    kk = heads(k * k_k)
    kk = kk * lax.rsqrt(jnp.maximum(jnp.sum(kk * kk, axis=-1, keepdims=True), 1e-24))
    k = k * (1.0 + (a - 1.0) * k_a)
    rh, kh, vh, ah, wh = heads(r), heads(k), heads(v), heads(a), heads(decay)
    y = wkv7_scan(rh, wh, kh, vh, -kk, kk * ah)
    mean = jnp.mean(y, axis=-1, keepdims=True)
    var = jnp.mean(jnp.square(y - mean), axis=-1, keepdims=True)
    y = ((y - mean) * lax.rsqrt(var + GN_EPS)).reshape(bsz, t, RWKV_WIDTH)
    y = y * gn_w.astype(jnp.float32) + gn_b.astype(jnp.float32)
    bonus = jnp.sum(rh * kh * r_k.astype(jnp.float32), axis=-1, keepdims=True) * vh
    y = y + bonus.reshape(bsz, t, RWKV_WIDTH)
    return (y * g.astype(jnp.float32)).astype(p.dtype)


def causal_block_attention(q, k, v):
    bsz, t, h, dq = q.shape
    scale = QK_DIM ** -0.5
    kpos = jnp.arange(t)

    def attend(qb, qpos):
        s = jnp.einsum("bqhd,bkhd->bhqk", qb, k).astype(jnp.float32) * scale
        s = jnp.where(kpos[None, :] <= qpos[:, None], s, -1e30)
        p = jax.nn.softmax(s, axis=-1).astype(v.dtype)
        return jnp.einsum("bhqk,bkhd->bqhd", p, v)

    meta_out = attend(q[:, :N_META], jnp.arange(N_META))
    n_real = t - N_META
    nblk = n_real // Q_BLOCK
    q_real = jnp.moveaxis(q[:, N_META:].reshape(bsz, nblk, Q_BLOCK, h, dq), 1, 0)

    def body(args):
        qb, i = args
        return attend(qb, N_META + i * Q_BLOCK + jnp.arange(Q_BLOCK))

    real = lax.map(body, (q_real, jnp.arange(nblk)))
    real = jnp.moveaxis(real, 0, 1).reshape(bsz, n_real, h, v.shape[-1])
    return jnp.concatenate([meta_out, real], axis=1)


def mla_branch(p, q_norm, w_uq, kv_norm, w_ukv, cos, sin):
    bsz, t, _ = p.shape
    c_q, c_kv, k_pe = _split(p, (Q_LORA, KV_LORA, ROPE_DIM))
    q = (rms_norm(c_q, q_norm) @ w_uq).reshape(bsz, t, MLA_HEADS, QK_DIM)
    q_nope, q_pe = _split(q, (NOPE_DIM, ROPE_DIM))
    kv = (rms_norm(c_kv, kv_norm) @ w_ukv).reshape(bsz, t, MLA_HEADS, NOPE_DIM + V_DIM)
    k_nope, v = _split(kv, (NOPE_DIM, V_DIM))
    q_pe = rope(q_pe, cos, sin)
    k_pe = rope(k_pe[:, :, None, :], cos, sin)
    q = jnp.concatenate([q_nope, q_pe], axis=-1)
    k = jnp.concatenate([k_nope, jnp.broadcast_to(k_pe, (bsz, t, MLA_HEADS, ROPE_DIM))], axis=-1)
    o = causal_block_attention(q, k, v)
    return o.reshape(bsz, t, MLA_HEADS * V_DIM)


def _fwd_setup_inputs(seed: int = 0) -> dict:
    key = jax.random.key(seed)
    ks = jax.random.split(key, 32)
    L, D = DEPTH, D_MODEL
    nrm = lambda k, shape, s: jax.random.normal(k, shape, jnp.float32) * s
    uni = lambda k, shape: jax.random.uniform(k, shape, jnp.float32)
    return {
        "x": nrm(ks[0], (BATCH, SEQ, D), 1.0),
        "meta_tokens": nrm(ks[1], (N_META, D), 1.0),
        "ffn1_norm": 1.0 + nrm(ks[2], (L, D), 0.02),
        "ffn1_w_gate": nrm(ks[3], (L, D, D_FF), D ** -0.5),
        "ffn1_w_up": nrm(ks[4], (L, D, D_FF), D ** -0.5),
        "ffn1_w_down": nrm(ks[5], (L, D_FF, D), D_FF ** -0.5),
        "mix_norm": 1.0 + nrm(ks[6], (L, D), 0.02),
        "w_in": nrm(ks[7], (L, D, IN_COLS), D ** -0.5),
        "tm_mu": uni(ks[8], (L, RWKV_COLS)),
        "w0": -6.5 + 5.0 * uni(ks[9], (L, RWKV_WIDTH)),
        "w_up": nrm(ks[10], (L, W_LORA, RWKV_WIDTH), W_LORA ** -0.5),
        "a0": nrm(ks[11], (L, RWKV_WIDTH), 0.1),
        "a_up": nrm(ks[12], (L, A_LORA, RWKV_WIDTH), A_LORA ** -0.5),
        "g_up": nrm(ks[13], (L, G_LORA, RWKV_WIDTH), G_LORA ** -0.5),
        "k_k": 0.85 + nrm(ks[14], (L, RWKV_WIDTH), 0.02),
        "k_a": 1.0 + nrm(ks[15], (L, RWKV_WIDTH), 0.02),
        "r_k": nrm(ks[16], (L, RWKV_HEADS, RWKV_HEAD), 0.1),
        "gn_w": 1.0 + nrm(ks[17], (L, RWKV_WIDTH), 0.02),
        "gn_b": nrm(ks[18], (L, RWKV_WIDTH), 0.01),
        "q_norm": 1.0 + nrm(ks[19], (L, Q_LORA), 0.02),
        "w_uq": nrm(ks[20], (L, Q_LORA, MLA_HEADS * QK_DIM), Q_LORA ** -0.5),
        "kv_norm": 1.0 + nrm(ks[21], (L, KV_LORA), 0.02),
        "w_ukv": nrm(ks[22], (L, KV_LORA, MLA_HEADS * (NOPE_DIM + V_DIM)), KV_LORA ** -0.5),
        "w_out": nrm(ks[23], (L, D, D), D ** -0.5),
        "ffn2_norm": 1.0 + nrm(ks[24], (L, D), 0.02),
        "ffn2_w_gate": nrm(ks[25], (L, D, D_FF), D ** -0.5),
        "ffn2_w_up": nrm(ks[26], (L, D, D_FF), D ** -0.5),
        "ffn2_w_down": nrm(ks[27], (L, D_FF, D), D_FF ** -0.5),
        "final_norm": 1.0 + nrm(ks[28], (D,), 0.02),
    }


def _fwd_reference(x, meta_tokens, ffn1_norm, ffn1_w_gate, ffn1_w_up, ffn1_w_down, mix_norm, w_in,
              tm_mu, w0, w_up, a0, a_up, g_up, k_k, k_a, r_k, gn_w, gn_b, q_norm, w_uq,
              kv_norm, w_ukv, w_out, ffn2_norm, ffn2_w_gate, ffn2_w_up, ffn2_w_down, final_norm):
    bsz = x.shape[0]
    h = jnp.concatenate([jnp.broadcast_to(meta_tokens.astype(x.dtype)[None], (bsz, N_META, D_MODEL)), x], axis=1)
    t = h.shape[1]
    pos = jnp.arange(t, dtype=jnp.float32)
    inv_freq = 1.0 / (ROPE_THETA ** (jnp.arange(0, ROPE_DIM, 2, dtype=jnp.float32) / ROPE_DIM))
    ang = pos[:, None] * inv_freq[None, :]
    cos, sin = jnp.cos(ang), jnp.sin(ang)

    for l in range(DEPTH):
        h = h + 0.5 * swiglu(rms_norm(h, ffn1_norm[l]), ffn1_w_gate[l], ffn1_w_up[l], ffn1_w_down[l])
        u = rms_norm(h, mix_norm[l])
        proj = u @ w_in[l]
        p_rwkv, p_mla, p_gate = _split(proj, (RWKV_COLS, MLA_COLS, GATE_COLS))
        y_a = rwkv7_branch(p_rwkv, tm_mu[l], w0[l], w_up[l], a0[l], a_up[l], g_up[l],
                           k_k[l], k_a[l], r_k[l], gn_w[l], gn_b[l])
        y_b = mla_branch(p_mla, q_norm[l], w_uq[l], kv_norm[l], w_ukv[l], cos, sin)
        g_a, g_b = jnp.split(jax.nn.sigmoid(p_gate), 2, axis=-1)
        h = h + (g_a * y_a + g_b * y_b) @ w_out[l]
        h = h + 0.5 * swiglu(rms_norm(h, ffn2_norm[l]), ffn2_w_gate[l], ffn2_w_up[l], ffn2_w_down[l])

    y = rms_norm(h, final_norm)[:, N_META:]
    return y


import jax as _jax
import jax.numpy as _jnp

TWIN_FORMAT = 'train_step'
FWD_PARAMS = ['x', 'meta_tokens', 'ffn1_norm', 'ffn1_w_gate', 'ffn1_w_up', 'ffn1_w_down', 'mix_norm', 'w_in', 'tm_mu', 'w0', 'w_up', 'a0', 'a_up', 'g_up', 'k_k', 'k_a', 'r_k', 'gn_w', 'gn_b', 'q_norm', 'w_uq', 'kv_norm', 'w_ukv', 'w_out', 'ffn2_norm', 'ffn2_w_gate', 'ffn2_w_up', 'ffn2_w_down', 'final_norm']
TWIN_WEIGHTS = ['meta_tokens', 'ffn1_norm', 'ffn1_w_gate', 'ffn1_w_up', 'ffn1_w_down', 'mix_norm', 'w_in', 'tm_mu', 'w0', 'w_up', 'a0', 'a_up', 'g_up', 'k_k', 'k_a', 'r_k', 'gn_w', 'gn_b', 'q_norm', 'w_uq', 'kv_norm', 'w_ukv', 'w_out', 'ffn2_norm', 'ffn2_w_gate', 'ffn2_w_up', 'ffn2_w_down', 'final_norm']
TWIN_DIFF_INPUT = 'x'
TWIN_INPUTS = ['x', 'meta_tokens', 'ffn1_norm', 'ffn1_w_gate', 'ffn1_w_up', 'ffn1_w_down', 'mix_norm', 'w_in', 'tm_mu', 'w0', 'w_up', 'a0', 'a_up', 'g_up', 'k_k', 'k_a', 'r_k', 'gn_w', 'gn_b', 'q_norm', 'w_uq', 'kv_norm', 'w_ukv', 'w_out', 'ffn2_norm', 'ffn2_w_gate', 'ffn2_w_up', 'ffn2_w_down', 'final_norm', 'loss_target', 'm_meta_tokens', 'm_ffn1_norm', 'm_ffn1_w_gate', 'm_ffn1_w_up', 'm_ffn1_w_down', 'm_mix_norm', 'm_w_in', 'm_tm_mu', 'm_w0', 'm_w_up', 'm_a0', 'm_a_up', 'm_g_up', 'm_k_k', 'm_k_a', 'm_r_k', 'm_gn_w', 'm_gn_b', 'm_q_norm', 'm_w_uq', 'm_kv_norm', 'm_w_ukv', 'm_w_out', 'm_ffn2_norm', 'm_ffn2_w_gate', 'm_ffn2_w_up', 'm_ffn2_w_down', 'm_final_norm', 'v_meta_tokens', 'v_ffn1_norm', 'v_ffn1_w_gate', 'v_ffn1_w_up', 'v_ffn1_w_down', 'v_mix_norm', 'v_w_in', 'v_tm_mu', 'v_w0', 'v_w_up', 'v_a0', 'v_a_up', 'v_g_up', 'v_k_k', 'v_k_a', 'v_r_k', 'v_gn_w', 'v_gn_b', 'v_q_norm', 'v_w_uq', 'v_kv_norm', 'v_w_ukv', 'v_w_out', 'v_ffn2_norm', 'v_ffn2_w_gate', 'v_ffn2_w_up', 'v_ffn2_w_down', 'v_final_norm']
TWIN_OUTPUTS = ['loss', 'grad_x', 'grad_meta_tokens', 'grad_ffn1_norm', 'grad_ffn1_w_gate', 'grad_ffn1_w_up', 'grad_ffn1_w_down', 'grad_mix_norm', 'grad_w_in', 'grad_tm_mu', 'grad_w0', 'grad_w_up', 'grad_a0', 'grad_a_up', 'grad_g_up', 'grad_k_k', 'grad_k_a', 'grad_r_k', 'grad_gn_w', 'grad_gn_b', 'grad_q_norm', 'grad_w_uq', 'grad_kv_norm', 'grad_w_ukv', 'grad_w_out', 'grad_ffn2_norm', 'grad_ffn2_w_gate', 'grad_ffn2_w_up', 'grad_ffn2_w_down', 'grad_final_norm', 'delta_meta_tokens', 'delta_ffn1_norm', 'delta_ffn1_w_gate', 'delta_ffn1_w_up', 'delta_ffn1_w_down', 'delta_mix_norm', 'delta_w_in', 'delta_tm_mu', 'delta_w0', 'delta_w_up', 'delta_a0', 'delta_a_up', 'delta_g_up', 'delta_k_k', 'delta_k_a', 'delta_r_k', 'delta_gn_w', 'delta_gn_b', 'delta_q_norm', 'delta_w_uq', 'delta_kv_norm', 'delta_w_ukv', 'delta_w_out', 'delta_ffn2_norm', 'delta_ffn2_w_gate', 'delta_ffn2_w_up', 'delta_ffn2_w_down', 'delta_final_norm', 'new_m_meta_tokens', 'new_m_ffn1_norm', 'new_m_ffn1_w_gate', 'new_m_ffn1_w_up', 'new_m_ffn1_w_down', 'new_m_mix_norm', 'new_m_w_in', 'new_m_tm_mu', 'new_m_w0', 'new_m_w_up', 'new_m_a0', 'new_m_a_up', 'new_m_g_up', 'new_m_k_k', 'new_m_k_a', 'new_m_r_k', 'new_m_gn_w', 'new_m_gn_b', 'new_m_q_norm', 'new_m_w_uq', 'new_m_kv_norm', 'new_m_w_ukv', 'new_m_w_out', 'new_m_ffn2_norm', 'new_m_ffn2_w_gate', 'new_m_ffn2_w_up', 'new_m_ffn2_w_down', 'new_m_final_norm', 'new_v_meta_tokens', 'new_v_ffn1_norm', 'new_v_ffn1_w_gate', 'new_v_ffn1_w_up', 'new_v_ffn1_w_down', 'new_v_mix_norm', 'new_v_w_in', 'new_v_tm_mu', 'new_v_w0', 'new_v_w_up', 'new_v_a0', 'new_v_a_up', 'new_v_g_up', 'new_v_k_k', 'new_v_k_a', 'new_v_r_k', 'new_v_gn_w', 'new_v_gn_b', 'new_v_q_norm', 'new_v_w_uq', 'new_v_kv_norm', 'new_v_w_ukv', 'new_v_w_out', 'new_v_ffn2_norm', 'new_v_ffn2_w_gate', 'new_v_ffn2_w_up', 'new_v_ffn2_w_down', 'new_v_final_norm']
TWIN_LEAF_KINDS = {'loss': 'loss', 'grad_x': 'grad_x', 'grad_meta_tokens': 'grad_w', 'grad_ffn1_norm': 'grad_w', 'grad_ffn1_w_gate': 'grad_w', 'grad_ffn1_w_up': 'grad_w', 'grad_ffn1_w_down': 'grad_w', 'grad_mix_norm': 'grad_w', 'grad_w_in': 'grad_w', 'grad_tm_mu': 'grad_w', 'grad_w0': 'grad_w', 'grad_w_up': 'grad_w', 'grad_a0': 'grad_w', 'grad_a_up': 'grad_w', 'grad_g_up': 'grad_w', 'grad_k_k': 'grad_w', 'grad_k_a': 'grad_w', 'grad_r_k': 'grad_w', 'grad_gn_w': 'grad_w', 'grad_gn_b': 'grad_w', 'grad_q_norm': 'grad_w', 'grad_w_uq': 'grad_w', 'grad_kv_norm': 'grad_w', 'grad_w_ukv': 'grad_w', 'grad_w_out': 'grad_w', 'grad_ffn2_norm': 'grad_w', 'grad_ffn2_w_gate': 'grad_w', 'grad_ffn2_w_up': 'grad_w', 'grad_ffn2_w_down': 'grad_w', 'grad_final_norm': 'grad_w', 'delta_meta_tokens': 'delta_w', 'delta_ffn1_norm': 'delta_w', 'delta_ffn1_w_gate': 'delta_w', 'delta_ffn1_w_up': 'delta_w', 'delta_ffn1_w_down': 'delta_w', 'delta_mix_norm': 'delta_w', 'delta_w_in': 'delta_w', 'delta_tm_mu': 'delta_w', 'delta_w0': 'delta_w', 'delta_w_up': 'delta_w', 'delta_a0': 'delta_w', 'delta_a_up': 'delta_w', 'delta_g_up': 'delta_w', 'delta_k_k': 'delta_w', 'delta_k_a': 'delta_w', 'delta_r_k': 'delta_w', 'delta_gn_w': 'delta_w', 'delta_gn_b': 'delta_w', 'delta_q_norm': 'delta_w', 'delta_w_uq': 'delta_w', 'delta_kv_norm': 'delta_w', 'delta_w_ukv': 'delta_w', 'delta_w_out': 'delta_w', 'delta_ffn2_norm': 'delta_w', 'delta_ffn2_w_gate': 'delta_w', 'delta_ffn2_w_up': 'delta_w', 'delta_ffn2_w_down': 'delta_w', 'delta_final_norm': 'delta_w', 'new_m_meta_tokens': 'new_m', 'new_m_ffn1_norm': 'new_m', 'new_m_ffn1_w_gate': 'new_m', 'new_m_ffn1_w_up': 'new_m', 'new_m_ffn1_w_down': 'new_m', 'new_m_mix_norm': 'new_m', 'new_m_w_in': 'new_m', 'new_m_tm_mu': 'new_m', 'new_m_w0': 'new_m', 'new_m_w_up': 'new_m', 'new_m_a0': 'new_m', 'new_m_a_up': 'new_m', 'new_m_g_up': 'new_m', 'new_m_k_k': 'new_m', 'new_m_k_a': 'new_m', 'new_m_r_k': 'new_m', 'new_m_gn_w': 'new_m', 'new_m_gn_b': 'new_m', 'new_m_q_norm': 'new_m', 'new_m_w_uq': 'new_m', 'new_m_kv_norm': 'new_m', 'new_m_w_ukv': 'new_m', 'new_m_w_out': 'new_m', 'new_m_ffn2_norm': 'new_m', 'new_m_ffn2_w_gate': 'new_m', 'new_m_ffn2_w_up': 'new_m', 'new_m_ffn2_w_down': 'new_m', 'new_m_final_norm': 'new_m', 'new_v_meta_tokens': 'new_v', 'new_v_ffn1_norm': 'new_v', 'new_v_ffn1_w_gate': 'new_v', 'new_v_ffn1_w_up': 'new_v', 'new_v_ffn1_w_down': 'new_v', 'new_v_mix_norm': 'new_v', 'new_v_w_in': 'new_v', 'new_v_tm_mu': 'new_v', 'new_v_w0': 'new_v', 'new_v_w_up': 'new_v', 'new_v_a0': 'new_v', 'new_v_a_up': 'new_v', 'new_v_g_up': 'new_v', 'new_v_k_k': 'new_v', 'new_v_k_a': 'new_v', 'new_v_r_k': 'new_v', 'new_v_gn_w': 'new_v', 'new_v_gn_b': 'new_v', 'new_v_q_norm': 'new_v', 'new_v_w_uq': 'new_v', 'new_v_kv_norm': 'new_v', 'new_v_w_ukv': 'new_v', 'new_v_w_out': 'new_v', 'new_v_ffn2_norm': 'new_v', 'new_v_ffn2_w_gate': 'new_v', 'new_v_ffn2_w_up': 'new_v', 'new_v_ffn2_w_down': 'new_v', 'new_v_final_norm': 'new_v'}


def _forward(args):
    return _fwd_reference(*[args[k] for k in FWD_PARAMS])


def _output_shape():
    out = _jax.eval_shape(lambda: _forward(_fwd_setup_inputs(0)))
    return out.shape, out.dtype

N_MICROBATCH = 1
ADAM_LR = 0.001
ADAM_B1 = 0.9
ADAM_B2 = 0.999
ADAM_EPS = 1e-08
ADAM_WD = 0.01
ADAM_STEP = 10
PER_EXAMPLE_BATCH_AXIS = {'x': 0, 'loss_target': 0}
SHARED_INPUTS = []
_WEIGHT_DTYPES = {'meta_tokens': _jnp.float32, 'ffn1_norm': _jnp.float32, 'ffn1_w_gate': _jnp.float32, 'ffn1_w_up': _jnp.float32, 'ffn1_w_down': _jnp.float32, 'mix_norm': _jnp.float32, 'w_in': _jnp.float32, 'tm_mu': _jnp.float32, 'w0': _jnp.float32, 'w_up': _jnp.float32, 'a0': _jnp.float32, 'a_up': _jnp.float32, 'g_up': _jnp.float32, 'k_k': _jnp.float32, 'k_a': _jnp.float32, 'r_k': _jnp.float32, 'gn_w': _jnp.float32, 'gn_b': _jnp.float32, 'q_norm': _jnp.float32, 'w_uq': _jnp.float32, 'kv_norm': _jnp.float32, 'w_ukv': _jnp.float32, 'w_out': _jnp.float32, 'ffn2_norm': _jnp.float32, 'ffn2_w_gate': _jnp.float32, 'ffn2_w_up': _jnp.float32, 'ffn2_w_down': _jnp.float32, 'final_norm': _jnp.float32}
MOMENT_SCALE = {'meta_tokens': 3.083688e-03, 'ffn1_norm': 4.138915e-02, 'ffn1_w_gate': 1.796390e-02, 'ffn1_w_up': 1.739695e-02, 'ffn1_w_down': 2.884479e-02, 'mix_norm': 5.020210e-02, 'w_in': 2.076714e-02, 'tm_mu': 4.231606e-02, 'w0': 9.985438e-03, 'w_up': 1.102326e-03, 'a0': 9.206593e-03, 'a_up': 8.759664e-03, 'g_up': 2.481711e-02, 'k_k': 3.622142e-02, 'k_a': 2.661762e-02, 'r_k': 5.554713e-02, 'gn_w': 2.422410e-02, 'gn_b': 2.426670e-02, 'q_norm': 1.413579e-02, 'w_uq': 5.887462e-03, 'kv_norm': 2.064404e-02, 'w_ukv': 6.872887e-03, 'w_out': 2.611706e-02, 'ffn2_norm': 3.386027e-02, 'ffn2_w_gate': 1.463915e-02, 'ffn2_w_up': 1.418558e-02, 'ffn2_w_down': 2.350244e-02, 'final_norm': 1.598080e+01}


def _to_microbatches(a, axis):
    t = _jnp.moveaxis(a, axis, 0)
    t = t.reshape((N_MICROBATCH, t.shape[0] // N_MICROBATCH) + t.shape[1:])
    return _jnp.moveaxis(t, 1, axis + 1)


def setup_inputs(seed: int = 0) -> dict:
    inp = _fwd_setup_inputs(seed)
    key = _jax.random.fold_in(_jax.random.key(seed), 7919)
    shape, _ = _output_shape()
    out = dict(inp)
    out["loss_target"] = _jax.random.normal(_jax.random.fold_in(key, 0), shape, _jnp.float32)
    for i, name in enumerate(TWIN_WEIGHTS):
        w = inp[name].astype(_jnp.float32)
        if MOMENT_SCALE is None:
            s = _jnp.sqrt(_jnp.mean(_jnp.square(w)) + 1e-30)
        else:
            s = MOMENT_SCALE[name]
        km, kv = _jax.random.split(_jax.random.fold_in(key, i + 1))
        out[name] = w
        out["m_" + name] = s * _jax.random.normal(km, w.shape, _jnp.float32)
        out["v_" + name] = (s * s) * _jax.random.uniform(kv, w.shape, _jnp.float32, 0.5, 1.5)
    if N_MICROBATCH > 1:
        for name, axis in PER_EXAMPLE_BATCH_AXIS.items():
            out[name] = _to_microbatches(out[name], axis)
    return {'x': out['x'], 'meta_tokens': out['meta_tokens'], 'ffn1_norm': out['ffn1_norm'], 'ffn1_w_gate': out['ffn1_w_gate'], 'ffn1_w_up': out['ffn1_w_up'], 'ffn1_w_down': out['ffn1_w_down'], 'mix_norm': out['mix_norm'], 'w_in': out['w_in'], 'tm_mu': out['tm_mu'], 'w0': out['w0'], 'w_up': out['w_up'], 'a0': out['a0'], 'a_up': out['a_up'], 'g_up': out['g_up'], 'k_k': out['k_k'], 'k_a': out['k_a'], 'r_k': out['r_k'], 'gn_w': out['gn_w'], 'gn_b': out['gn_b'], 'q_norm': out['q_norm'], 'w_uq': out['w_uq'], 'kv_norm': out['kv_norm'], 'w_ukv': out['w_ukv'], 'w_out': out['w_out'], 'ffn2_norm': out['ffn2_norm'], 'ffn2_w_gate': out['ffn2_w_gate'], 'ffn2_w_up': out['ffn2_w_up'], 'ffn2_w_down': out['ffn2_w_down'], 'final_norm': out['final_norm'], 'loss_target': out['loss_target'], 'm_meta_tokens': out['m_meta_tokens'], 'm_ffn1_norm': out['m_ffn1_norm'], 'm_ffn1_w_gate': out['m_ffn1_w_gate'], 'm_ffn1_w_up': out['m_ffn1_w_up'], 'm_ffn1_w_down': out['m_ffn1_w_down'], 'm_mix_norm': out['m_mix_norm'], 'm_w_in': out['m_w_in'], 'm_tm_mu': out['m_tm_mu'], 'm_w0': out['m_w0'], 'm_w_up': out['m_w_up'], 'm_a0': out['m_a0'], 'm_a_up': out['m_a_up'], 'm_g_up': out['m_g_up'], 'm_k_k': out['m_k_k'], 'm_k_a': out['m_k_a'], 'm_r_k': out['m_r_k'], 'm_gn_w': out['m_gn_w'], 'm_gn_b': out['m_gn_b'], 'm_q_norm': out['m_q_norm'], 'm_w_uq': out['m_w_uq'], 'm_kv_norm': out['m_kv_norm'], 'm_w_ukv': out['m_w_ukv'], 'm_w_out': out['m_w_out'], 'm_ffn2_norm': out['m_ffn2_norm'], 'm_ffn2_w_gate': out['m_ffn2_w_gate'], 'm_ffn2_w_up': out['m_ffn2_w_up'], 'm_ffn2_w_down': out['m_ffn2_w_down'], 'm_final_norm': out['m_final_norm'], 'v_meta_tokens': out['v_meta_tokens'], 'v_ffn1_norm': out['v_ffn1_norm'], 'v_ffn1_w_gate': out['v_ffn1_w_gate'], 'v_ffn1_w_up': out['v_ffn1_w_up'], 'v_ffn1_w_down': out['v_ffn1_w_down'], 'v_mix_norm': out['v_mix_norm'], 'v_w_in': out['v_w_in'], 'v_tm_mu': out['v_tm_mu'], 'v_w0': out['v_w0'], 'v_w_up': out['v_w_up'], 'v_a0': out['v_a0'], 'v_a_up': out['v_a_up'], 'v_g_up': out['v_g_up'], 'v_k_k': out['v_k_k'], 'v_k_a': out['v_k_a'], 'v_r_k': out['v_r_k'], 'v_gn_w': out['v_gn_w'], 'v_gn_b': out['v_gn_b'], 'v_q_norm': out['v_q_norm'], 'v_w_uq': out['v_w_uq'], 'v_kv_norm': out['v_kv_norm'], 'v_w_ukv': out['v_w_ukv'], 'v_w_out': out['v_w_out'], 'v_ffn2_norm': out['v_ffn2_norm'], 'v_ffn2_w_gate': out['v_ffn2_w_gate'], 'v_ffn2_w_up': out['v_ffn2_w_up'], 'v_ffn2_w_down': out['v_ffn2_w_down'], 'v_final_norm': out['v_final_norm']}


def _loss(weights, diff, rest, loss_target):
    with _jax.named_scope("forward"):
        args = {**rest, TWIN_DIFF_INPUT: diff, **{k: w.astype(_WEIGHT_DTYPES[k]) for k, w in weights.items()}}
        y = _forward(args)
    with _jax.named_scope("loss_head"):
        err = _jnp.square(y.astype(_jnp.float32) - loss_target)
        return 0.5 * _jnp.sum(_jnp.mean(err, axis=-1)) if err.ndim else 0.5 * err


def _adamw(w, g, m, v):
    m = ADAM_B1 * m + (1.0 - ADAM_B1) * g
    v = ADAM_B2 * v + (1.0 - ADAM_B2) * _jnp.square(g)
    m_hat = m / (1.0 - ADAM_B1 ** ADAM_STEP)
    v_hat = v / (1.0 - ADAM_B2 ** ADAM_STEP)
    delta = -ADAM_LR * (m_hat / (_jnp.sqrt(v_hat) + ADAM_EPS) + ADAM_WD * w)
    return delta, m, v


def reference(x, meta_tokens, ffn1_norm, ffn1_w_gate, ffn1_w_up, ffn1_w_down, mix_norm, w_in, tm_mu, w0, w_up, a0, a_up, g_up, k_k, k_a, r_k, gn_w, gn_b, q_norm, w_uq, kv_norm, w_ukv, w_out, ffn2_norm, ffn2_w_gate, ffn2_w_up, ffn2_w_down, final_norm, loss_target, m_meta_tokens, m_ffn1_norm, m_ffn1_w_gate, m_ffn1_w_up, m_ffn1_w_down, m_mix_norm, m_w_in, m_tm_mu, m_w0, m_w_up, m_a0, m_a_up, m_g_up, m_k_k, m_k_a, m_r_k, m_gn_w, m_gn_b, m_q_norm, m_w_uq, m_kv_norm, m_w_ukv, m_w_out, m_ffn2_norm, m_ffn2_w_gate, m_ffn2_w_up, m_ffn2_w_down, m_final_norm, v_meta_tokens, v_ffn1_norm, v_ffn1_w_gate, v_ffn1_w_up, v_ffn1_w_down, v_mix_norm, v_w_in, v_tm_mu, v_w0, v_w_up, v_a0, v_a_up, v_g_up, v_k_k, v_k_a, v_r_k, v_gn_w, v_gn_b, v_q_norm, v_w_uq, v_kv_norm, v_w_ukv, v_w_out, v_ffn2_norm, v_ffn2_w_gate, v_ffn2_w_up, v_ffn2_w_down, v_final_norm):
    given = dict(x=x, meta_tokens=meta_tokens, ffn1_norm=ffn1_norm, ffn1_w_gate=ffn1_w_gate, ffn1_w_up=ffn1_w_up, ffn1_w_down=ffn1_w_down, mix_norm=mix_norm, w_in=w_in, tm_mu=tm_mu, w0=w0, w_up=w_up, a0=a0, a_up=a_up, g_up=g_up, k_k=k_k, k_a=k_a, r_k=r_k, gn_w=gn_w, gn_b=gn_b, q_norm=q_norm, w_uq=w_uq, kv_norm=kv_norm, w_ukv=w_ukv, w_out=w_out, ffn2_norm=ffn2_norm, ffn2_w_gate=ffn2_w_gate, ffn2_w_up=ffn2_w_up, ffn2_w_down=ffn2_w_down, final_norm=final_norm, loss_target=loss_target, m_meta_tokens=m_meta_tokens, m_ffn1_norm=m_ffn1_norm, m_ffn1_w_gate=m_ffn1_w_gate, m_ffn1_w_up=m_ffn1_w_up, m_ffn1_w_down=m_ffn1_w_down, m_mix_norm=m_mix_norm, m_w_in=m_w_in, m_tm_mu=m_tm_mu, m_w0=m_w0, m_w_up=m_w_up, m_a0=m_a0, m_a_up=m_a_up, m_g_up=m_g_up, m_k_k=m_k_k, m_k_a=m_k_a, m_r_k=m_r_k, m_gn_w=m_gn_w, m_gn_b=m_gn_b, m_q_norm=m_q_norm, m_w_uq=m_w_uq, m_kv_norm=m_kv_norm, m_w_ukv=m_w_ukv, m_w_out=m_w_out, m_ffn2_norm=m_ffn2_norm, m_ffn2_w_gate=m_ffn2_w_gate, m_ffn2_w_up=m_ffn2_w_up, m_ffn2_w_down=m_ffn2_w_down, m_final_norm=m_final_norm, v_meta_tokens=v_meta_tokens, v_ffn1_norm=v_ffn1_norm, v_ffn1_w_gate=v_ffn1_w_gate, v_ffn1_w_up=v_ffn1_w_up, v_ffn1_w_down=v_ffn1_w_down, v_mix_norm=v_mix_norm, v_w_in=v_w_in, v_tm_mu=v_tm_mu, v_w0=v_w0, v_w_up=v_w_up, v_a0=v_a0, v_a_up=v_a_up, v_g_up=v_g_up, v_k_k=v_k_k, v_k_a=v_k_a, v_r_k=v_r_k, v_gn_w=v_gn_w, v_gn_b=v_gn_b, v_q_norm=v_q_norm, v_w_uq=v_w_uq, v_kv_norm=v_kv_norm, v_w_ukv=v_w_ukv, v_w_out=v_w_out, v_ffn2_norm=v_ffn2_norm, v_ffn2_w_gate=v_ffn2_w_gate, v_ffn2_w_up=v_ffn2_w_up, v_ffn2_w_down=v_ffn2_w_down, v_final_norm=v_final_norm)
    weights = {n: given[n] for n in TWIN_WEIGHTS}
    shared = {n: given[n] for n in SHARED_INPUTS}
    per_example = {n: given[n] for n in ['x']}
    grad_fn = _jax.value_and_grad(_loss, argnums=(0, 1))

    def one_microbatch(ex, loss_target):
        ex = dict(ex)
        diff = ex.pop(TWIN_DIFF_INPUT)
        return grad_fn(weights, diff, {**shared, **ex}, loss_target)

    if N_MICROBATCH == 1:
        loss, (grad_w, grad_x) = one_microbatch(per_example, given["loss_target"])
    else:
        def body(carry, xs):
            loss_sum, grad_sum = carry
            l_k, (gw_k, gx_k) = one_microbatch(xs[0], xs[1])
            with _jax.named_scope("update"):
                return (loss_sum + l_k, _jax.tree.map(_jnp.add, grad_sum, gw_k)), gx_k

        init = (_jnp.zeros((), _jnp.float32), _jax.tree.map(_jnp.zeros_like, weights))
        (loss, grad_w), grad_x = _jax.lax.scan(body, init, (per_example, given["loss_target"]))
    with _jax.named_scope("update"):
        delta_w, new_m, new_v = {}, {}, {}
        for n in TWIN_WEIGHTS:
            delta_w[n], new_m[n], new_v[n] = _adamw(weights[n], grad_w[n], given["m_" + n], given["v_" + n])
    return (loss, grad_x, *[grad_w[n] for n in TWIN_WEIGHTS], *[delta_w[n] for n in TWIN_WEIGHTS],
            *[new_m[n] for n in TWIN_WEIGHTS], *[new_v[n] for n in TWIN_WEIGHTS])
```

```python
import functools
import math

import jax
import jax.numpy as jnp
import numpy as np
from jax import lax
from jax.experimental import pallas as pl
from jax.experimental.pallas import tpu as pltpu

F32 = jnp.float32
BF16 = jnp.bfloat16
MESH = pl.DeviceIdType.MESH

N_META = 16
NORM_EPS = 1e-6
RWKV_HEAD = 64
GN_EPS = RWKV_HEAD * 1e-5
W_LORA, A_LORA, G_LORA = 96, 96, 256
Q_LORA, KV_LORA = 512, 512
NOPE_DIM, ROPE_DIM, V_DIM = 128, 64, 128
QK_DIM = NOPE_DIM + ROPE_DIM
ROPE_THETA = 10000.0
Q_BLOCK = 128
ADAM_LR, ADAM_B1, ADAM_B2, ADAM_EPS, ADAM_WD, ADAM_STEP = 0.001, 0.9, 0.999, 1e-08, 0.01, 10

LANE = 128
VMEM_LIMIT = 56 * 1024 * 1024


def _pcall(body, **kw):
    return pl.pallas_call(body, **kw)


def _cparams(sem):
    return pltpu.CompilerParams(dimension_semantics=sem, vmem_limit_bytes=VMEM_LIMIT)


def _tile(n, prefs=(512, 256, 128)):
    for p in prefs:
        if n % p == 0:
            return p
    return n


def _rtile(n, pref=512):
    best = None
    for t in range(16, min(n, pref * 2) + 1, 16):
        if n % t == 0 and (best is None or abs(t - pref) < abs(best - pref)):
            best = t
    return best if best is not None else n


def mm(a, b, mode="nn", out_dtype=F32, res=None, alpha=1.0, name="mm"):
    if mode == "nn":
        (M, K), (K2, N) = a.shape, b.shape
    elif mode == "nt":
        (M, K), (N, K2) = a.shape, b.shape
    else:
        (K, M), (K2, N) = a.shape, b.shape
    assert K == K2, (a.shape, b.shape, mode)
    tm = _rtile(M, 512) if mode != "tn" else _tile(M)
    tn = _tile(N)
    tk = _rtile(K, 512) if mode == "tn" else _tile(K)
    nk = K // tk
    dims = {"nn": (((1,), (0,)), ((), ())), "nt": (((1,), (1,)), ((), ())), "tn": (((0,), (0,)), ((), ()))}[mode]

    def body(*refs):
        if res is None:
            a_ref, b_ref, o_ref, acc = refs
        else:
            a_ref, b_ref, r_ref, o_ref, acc = refs
        k = pl.program_id(2)

        @pl.when(k == 0)
        def _():
            acc[...] = jnp.zeros_like(acc)

        acc[...] += lax.dot_general(a_ref[...].astype(BF16), b_ref[...].astype(BF16), dims,
                                    preferred_element_type=F32)

        @pl.when(k == nk - 1)
        def _():
            v = acc[...]
            if res is not None:
                v = r_ref[...].astype(F32) + alpha * v
            elif alpha != 1.0:
                v = alpha * v
            o_ref[...] = v.astype(o_ref.dtype)

    if mode == "nn":
        a_spec = pl.BlockSpec((tm, tk), lambda i, j, k: (i, k))
        b_spec = pl.BlockSpec((tk, tn), lambda i, j, k: (k, j))
    elif mode == "nt":
        a_spec = pl.BlockSpec((tm, tk), lambda i, j, k: (i, k))
        b_spec = pl.BlockSpec((tn, tk), lambda i, j, k: (j, k))
    else:
        a_spec = pl.BlockSpec((tk, tm), lambda i, j, k: (k, i))
        b_spec = pl.BlockSpec((tk, tn), lambda i, j, k: (k, j))
    o_spec = pl.BlockSpec((tm, tn), lambda i, j, k: (i, j))
    in_specs = [a_spec, b_spec] + ([o_spec] if res is not None else [])
    args = (a, b) + ((res,) if res is not None else ())
    return _pcall(
        body, name=name, grid=(M // tm, N // tn, nk), in_specs=in_specs, out_specs=o_spec,
        out_shape=jax.ShapeDtypeStruct((M, N), out_dtype), scratch_shapes=[pltpu.VMEM((tm, tn), F32)],
        compiler_params=_cparams(("parallel", "parallel", "arbitrary")),
    )(*args)


def rowwise(fn, row_ins, par_ins, outs, accs=(), *, tm, ncb=1, name="rowwise"):
    R = row_ins[0][0].shape[0]
    assert R % tm == 0, (R, tm)
    nrb = R // tm
    in_specs, args = [], []
    for spec in row_ins:
        arr, w, base = spec[:3]
        mod = spec[3] if len(spec) > 3 else None
        cstep = 0 if (len(spec) > 4 and spec[4]) else 1
        if mod is None:
            in_specs.append(pl.BlockSpec((tm, w), lambda j, i, base=base, cstep=cstep: (i, base + cstep * j)))
        else:
            in_specs.append(pl.BlockSpec((tm, w), lambda j, i, base=base, mod=mod, cstep=cstep: (i % mod, base + cstep * j)))
        args.append(arr)
    for arr, w, base in par_ins:
        in_specs.append(pl.BlockSpec((arr.shape[0], w), lambda j, i, base=base: (0, base + j)))
        args.append(arr)
    out_specs, out_shape = [], []
    for cols, w, dt in outs:
        out_specs.append(pl.BlockSpec((tm, w), lambda j, i: (i, j)))
        out_shape.append(jax.ShapeDtypeStruct((R, cols), dt))
    for p, cols, w in accs:
        out_specs.append(pl.BlockSpec((p, w), lambda j, i: (0, j)))
        out_shape.append(jax.ShapeDtypeStruct((p, cols), F32))
    nin, nout, nacc = len(args), len(outs), len(accs)

    def body(*refs):
        vals = [r[...] for r in refs[:nin]]
        res = fn(*vals)
        if not isinstance(res, (tuple, list)):
            res = (res,)
        assert len(res) == nout + nacc, (len(res), nout, nacc)
        for o_ref, v in zip(refs[nin:nin + nout], res[:nout]):
            o_ref[...] = v.astype(o_ref.dtype)
        if nacc:
            i = pl.program_id(1)

            @pl.when(i == 0)
            def _():
                for a_ref in refs[nin + nout:]:
                    a_ref[...] = jnp.zeros_like(a_ref)

            for a_ref, v in zip(refs[nin + nout:], res[nout:]):
                a_ref[...] += v.astype(F32)

    r = _pcall(
        body, name=name, grid=(ncb, nrb), in_specs=in_specs, out_specs=out_specs, out_shape=out_shape,
        compiler_params=_cparams(("parallel", "arbitrary")),
    )(*args)
    return r


def vjp_fn(fwd, nprim):
    def f(*vals):
        prim, cts = vals[:nprim], vals[nprim:]
        out, pull = jax.vjp(fwd, *[p.astype(F32) for p in prim])
        if not isinstance(out, (tuple, list)):
            cts = cts[0].astype(F32)
        else:
            cts = tuple(c.astype(F32) for c in cts)
        return pull(cts)
    return f


WKV_TB = 8


def wkv_fwd(w, kn, b, k, r, v):
    T, N, L = w.shape
    NI = v.shape[1]
    tb = WKV_TB
    assert T % tb == 0

    def body(w_ref, kn_ref, b_ref, k_ref, r_ref, v_ref, y_ref, sp_ref, s_ref):
        @pl.when(pl.program_id(0) == 0)
        def _():
            s_ref[...] = jnp.zeros_like(s_ref)

        def step(s, carry):
            W, KN, B, Kk, Rr = w_ref[s], kn_ref[s], b_ref[s], k_ref[s], r_ref[s]
            for i in range(NI):
                S = s_ref[i]
                sp_ref[s, i] = S
                sa = jnp.sum(S * KN, axis=0, keepdims=True)
                vi = v_ref[s, pl.ds(i, 1), :]
                Sn = S * W + sa * B + vi * Kk
                s_ref[i] = Sn
                y_ref[s, pl.ds(i, 1), :] = jnp.sum(Sn * Rr, axis=0, keepdims=True)
            return carry

        lax.fori_loop(0, tb, step, 0)

    jspec = pl.BlockSpec((tb, N, L), lambda t: (t, 0, 0))
    ispec = pl.BlockSpec((tb, NI, L), lambda t: (t, 0, 0))
    return _pcall(
        body, name="wkv_fwd", grid=(T // tb,), in_specs=[jspec] * 5 + [ispec],
        out_specs=[ispec, pl.BlockSpec((tb, NI, N, L), lambda t: (t, 0, 0, 0))],
        out_shape=[jax.ShapeDtypeStruct((T, NI, L), F32), jax.ShapeDtypeStruct((T, NI, N, L), F32)],
        scratch_shapes=[pltpu.VMEM((NI, N, L), F32)],
        compiler_params=_cparams(("arbitrary",)),
    )(w, kn, b, k, r, v)


def wkv_bwd(w, kn, b, k, r, v, dy, sp):
    T, N, L = w.shape
    NI = v.shape[1]
    tb = WKV_TB
    nt = T // tb

    def body(w_ref, kn_ref, b_ref, k_ref, r_ref, v_ref, dy_ref, sp_ref,
             dw_ref, dkn_ref, db_ref, dk_ref, dr_ref, dv_ref, ds_ref):
        @pl.when(pl.program_id(0) == 0)
        def _():
            ds_ref[...] = jnp.zeros_like(ds_ref)

        def step(q, carry):
            s = tb - 1 - q
            W, KN, B, Kk, Rr = w_ref[s], kn_ref[s], b_ref[s], k_ref[s], r_ref[s]
            dW = jnp.zeros((N, L), F32)
            dKN, dB, dK, dR = dW, dW, dW, dW
            for i in range(NI):
                Sp = sp_ref[s, i]
                vi = v_ref[s, pl.ds(i, 1), :]
                dyi = dy_ref[s, pl.ds(i, 1), :]
                sa = jnp.sum(Sp * KN, axis=0, keepdims=True)
                Sn = Sp * W + sa * B + vi * Kk
                dS = ds_ref[i] + dyi * Rr
                dR = dR + Sn * dyi
                dv_ref[s, pl.ds(i, 1), :] = jnp.sum(dS * Kk, axis=0, keepdims=True)
                dK = dK + dS * vi
                dsa = jnp.sum(dS * B, axis=0, keepdims=True)
                dB = dB + dS * sa
                dW = dW + dS * Sp
                dKN = dKN + Sp * dsa
                ds_ref[i] = dS * W + dsa * KN
            half = L // 2
            for ref, val in ((dw_ref, dW), (dkn_ref, dKN), (db_ref, dB), (dk_ref, dK), (dr_ref, dR)):
                ref[s] = val + pltpu.roll(val, half, 1)
            return carry

        lax.fori_loop(0, tb, step, 0)

    jspec = pl.BlockSpec((tb, N, L), lambda t: (nt - 1 - t, 0, 0))
    ispec = pl.BlockSpec((tb, NI, L), lambda t: (nt - 1 - t, 0, 0))
    jshape = jax.ShapeDtypeStruct((T, N, L), F32)
    return _pcall(
        body, name="wkv_bwd", grid=(nt,),
        in_specs=[jspec] * 5 + [ispec, ispec, pl.BlockSpec((tb, NI, N, L), lambda t: (nt - 1 - t, 0, 0, 0))],
        out_specs=[jspec] * 5 + [ispec], out_shape=[jshape] * 5 + [jax.ShapeDtypeStruct((T, NI, L), F32)],
        scratch_shapes=[pltpu.VMEM((NI, N, L), F32)],
        compiler_params=_cparams(("arbitrary",)),
    )(w, kn, b, k, r, v, dy, sp)


_NT = (((1,), (1,)), ((), ()))
_TN = (((0,), (0,)), ((), ()))
ATT_SCALE = QK_DIM ** -0.5


def _att_probs(q1, q2, kn_ref, kp_ref, i, L):
    s = lax.dot_general(q1, kn_ref[0, :L, :], _NT, preferred_element_type=F32)
    s = s + lax.dot_general(q2, kp_ref[0, :L, :], _NT, preferred_element_type=F32)
    s = s * ATT_SCALE
    col = lax.broadcasted_iota(jnp.int32, (Q_BLOCK, L), 1)
    row = lax.broadcasted_iota(jnp.int32, (Q_BLOCK, L), 0)
    valid = (col < N_META) | ((col >= Q_BLOCK) & (col - Q_BLOCK <= Q_BLOCK * i + row))
    s = jnp.where(valid, s, -1e30)
    m = jnp.max(s, axis=-1, keepdims=True)
    p = jnp.exp(s - m)
    return p / jnp.sum(p, axis=-1, keepdims=True)


def attn_fwd(qn, qp, kn, kp, v):
    B, S, HD = qn.shape
    SP = kn.shape[1]
    H = HD // LANE
    nq = S // Q_BLOCK

    def body(qn_ref, qp_ref, kn_ref, kp_ref, v_ref, o_ref):
        for i in range(nq):
            L = Q_BLOCK * (i + 2)
            rows = pl.ds(Q_BLOCK * i, Q_BLOCK)
            p = _att_probs(qn_ref[0, rows, :], qp_ref[0, rows, :], kn_ref, kp_ref, i, L)
            o_ref[0, rows, :] = jnp.dot(p.astype(BF16), v_ref[0, :L, :], preferred_element_type=F32)

    qspec = pl.BlockSpec((1, S, LANE), lambda b, h: (b, 0, h))
    kspec = pl.BlockSpec((1, SP, LANE), lambda b, h: (b, 0, h))
    pspec = pl.BlockSpec((1, SP, LANE), lambda b, h: (b, 0, 0))
    return _pcall(
        body, name="attn_fwd", grid=(B, H), in_specs=[qspec, qspec, kspec, pspec, kspec], out_specs=qspec,
        out_shape=jax.ShapeDtypeStruct((B, S, HD), F32), compiler_params=_cparams(("parallel", "parallel")),
    )(qn, qp, kn, kp, v)


def attn_bwd(qn, qp, kn, kp, v, do):
    B, S, HD = qn.shape
    SP = kn.shape[1]
    H = HD // LANE
    nq = S // Q_BLOCK

    def body(qn_ref, qp_ref, kn_ref, kp_ref, v_ref, do_ref, dqn_ref, dqp_ref, dkn_ref, dv_ref, dkp_ref, dkn_acc, dv_acc):
        @pl.when(pl.program_id(1) == 0)
        def _():
            dkp_ref[...] = jnp.zeros_like(dkp_ref)

        dkn_acc[...] = jnp.zeros_like(dkn_acc)
        dv_acc[...] = jnp.zeros_like(dv_acc)
        for i in range(nq):
            L = Q_BLOCK * (i + 2)
            rows = pl.ds(Q_BLOCK * i, Q_BLOCK)
            q1, q2, do_i = qn_ref[0, rows, :], qp_ref[0, rows, :], do_ref[0, rows, :]
            p = _att_probs(q1, q2, kn_ref, kp_ref, i, L)
            dp = lax.dot_general(do_i, v_ref[0, :L, :], _NT, preferred_element_type=F32)
            ds = (p * (dp - jnp.sum(p * dp, axis=-1, keepdims=True)) * ATT_SCALE).astype(BF16)
            dqn_ref[0, rows, :] = jnp.dot(ds, kn_ref[0, :L, :], preferred_element_type=F32).astype(dqn_ref.dtype)
            dqp_ref[0, rows, :] = jnp.dot(ds, kp_ref[0, :L, :], preferred_element_type=F32)
            dkn_acc[:L, :] += lax.dot_general(ds, q1, _TN, preferred_element_type=F32)
            dkp_ref[0, :L, :] += lax.dot_general(ds, q2, _TN, preferred_element_type=F32)
            dv_acc[:L, :] += lax.dot_general(p.astype(BF16), do_i, _TN, preferred_element_type=F32)
        dkn_ref[0] = dkn_acc[...].astype(dkn_ref.dtype)
        dv_ref[0] = dv_acc[...].astype(dv_ref.dtype)

    qspec = pl.BlockSpec((1, S, LANE), lambda b, h: (b, 0, h))
    kspec = pl.BlockSpec((1, SP, LANE), lambda b, h: (b, 0, h))
    pspec = pl.BlockSpec((1, SP, LANE), lambda b, h: (b, 0, 0))
    return _pcall(
        body, name="attn_bwd", grid=(B, H), in_specs=[qspec, qspec, kspec, pspec, kspec, qspec],
        out_specs=[qspec, qspec, kspec, kspec, pspec],
        out_shape=[jax.ShapeDtypeStruct((B, S, HD), BF16), jax.ShapeDtypeStruct((B, S, HD), F32),
                   jax.ShapeDtypeStruct((B, SP, HD), BF16), jax.ShapeDtypeStruct((B, SP, HD), BF16),
                   jax.ShapeDtypeStruct((B, SP, LANE), F32)],
        scratch_shapes=[pltpu.VMEM((SP, LANE), F32), pltpu.VMEM((SP, LANE), F32)],
        compiler_params=_cparams(("parallel", "arbitrary")),
    )(qn, qp, kn, kp, v, do)


_HBM = pl.BlockSpec(memory_space=pltpu.HBM)


def _place():
    x, y, c = lax.axis_index("x"), lax.axis_index("y"), lax.axis_index("c")
    chips = [(1 - x, y), (x, 1 - y), (1 - x, 1 - y)]
    return x, y, c, chips


def _rcopy(src, dst, ssem, rsem, dev):
    return pltpu.make_async_remote_copy(src_ref=src, dst_ref=dst, send_sem=ssem, recv_sem=rsem,
                                        device_id=dev, device_id_type=MESH)


def ag_weights(wp):
    _, Rh, W = wp.shape

    def body(w_ref, out_ref, ssem, rsem, lsem):
        x, y, c, chips = _place()
        s = 2 * x + y
        mine = pltpu.make_async_copy(w_ref, out_ref.at[s], lsem)
        mine.start()
        first = [_rcopy(w_ref.at[c], out_ref.at[s, c], ssem.at[j], rsem.at[j], (px, py, c))
                 for j, (px, py) in enumerate(chips)]
        for cp in first:
            cp.start()
        passed = []
        for j, (px, py) in enumerate(chips):
            sp = 2 * px + py
            _rcopy(w_ref.at[c], out_ref.at[sp, c], ssem.at[j], rsem.at[j], (px, py, c)).wait_recv()
            fw = _rcopy(out_ref.at[sp, c], out_ref.at[sp, c], ssem.at[3 + j], rsem.at[3 + j], (x, y, 1 - c))
            fw.start()
            passed.append(fw)
        for j, (px, py) in enumerate(chips):
            sp = 2 * px + py
            _rcopy(w_ref.at[c], out_ref.at[sp, 1 - c], ssem.at[3 + j], rsem.at[3 + j], (x, y, 1 - c)).wait_recv()
        for cp in first + passed:
            cp.wait_send()
        mine.wait()

    return _pcall(
        body, name="ag_weights", in_specs=[_HBM], out_specs=_HBM,
        out_shape=jax.ShapeDtypeStruct((4, 2, Rh, W), wp.dtype),
        scratch_shapes=[pltpu.SemaphoreType.DMA((6,)), pltpu.SemaphoreType.DMA((6,)), pltpu.SemaphoreType.DMA],
    )(wp)


def swap_halves(g):
    _, _, Rh, W = g.shape

    def body(g_ref, out_ref, ssem, rsem):
        x, y, c, _ = _place()
        cp = _rcopy(g_ref.at[1 - c], out_ref, ssem, rsem, (x, y, 1 - c))
        cp.start()
        cp.wait()

    return _pcall(
        body, name="swap_halves", in_specs=[_HBM], out_specs=_HBM,
        out_shape=jax.ShapeDtypeStruct((4, Rh, W), g.dtype),
        scratch_shapes=[pltpu.SemaphoreType.DMA, pltpu.SemaphoreType.DMA],
    )(g)


def rs_chips(h):
    _, Rh, W = h.shape

    def body(h_ref, out_ref, ssem, rsem, lsem):
        x, y, c, chips = _place()
        s = 2 * x + y
        mine = pltpu.make_async_copy(h_ref.at[s], out_ref.at[s], lsem)
        mine.start()
        cps = [_rcopy(h_ref.at[2 * px + py], out_ref.at[s], ssem.at[j], rsem.at[j], (px, py, c))
               for j, (px, py) in enumerate(chips)]
        for cp in cps:
            cp.start()
        for j, (px, py) in enumerate(chips):
            _rcopy(h_ref.at[s], out_ref.at[2 * px + py], ssem.at[j], rsem.at[j], (px, py, c)).wait_recv()
        for cp in cps:
            cp.wait_send()
        mine.wait()

    return _pcall(
        body, name="rs_chips", in_specs=[_HBM], out_specs=_HBM,
        out_shape=jax.ShapeDtypeStruct((4, Rh, W), h.dtype),
        scratch_shapes=[pltpu.SemaphoreType.DMA((3,)), pltpu.SemaphoreType.DMA((3,)), pltpu.SemaphoreType.DMA],
    )(h)


def share_sibling(t):
    Rh, W = t.shape

    def body(t_ref, out_ref, ssem, rsem, lsem):
        x, y, c, _ = _place()
        mine = pltpu.make_async_copy(t_ref, out_ref.at[c], lsem)
        mine.start()
        cp = _rcopy(t_ref, out_ref.at[c], ssem, rsem, (x, y, 1 - c))
        cp.start()
        _rcopy(t_ref, out_ref.at[1 - c], ssem, rsem, (x, y, 1 - c)).wait_recv()
        cp.wait_send()
        mine.wait()

    return _pcall(
        body, name="share_sibling", in_specs=[_HBM], out_specs=_HBM,
        out_shape=jax.ShapeDtypeStruct((2, Rh, W), t.dtype),
        scratch_shapes=[pltpu.SemaphoreType.DMA, pltpu.SemaphoreType.DMA, pltpu.SemaphoreType.DMA],
    )(t)


def allreduce8(v, name):
    P, W = v.shape

    def body(v_ref, out_ref, buf, ssem, rsem):
        x, y, c, _ = _place()
        me = 4 * x + 2 * y + c
        buf[me] = v_ref[...]
        cps = []
        for k in range(1, 8):
            px = 1 - x if k & 4 else x
            py = 1 - y if k & 2 else y
            pc = 1 - c if k & 1 else c
            cp = _rcopy(buf.at[me], buf.at[me], ssem.at[k - 1], rsem.at[k - 1], (px, py, pc))
            cp.start()
            cps.append((cp, 4 * px + 2 * py + pc))
        for k, (cp, peer) in enumerate(cps):
            _rcopy(buf.at[me], buf.at[peer], ssem.at[k], rsem.at[k], (x, y, c)).wait_recv()
        for cp, _ in cps:
            cp.wait_send()
        acc = buf[0]
        for d in range(1, 8):
            acc = acc + buf[d]
        out_ref[...] = acc

    return _pcall(
        body, name=name, in_specs=[pl.BlockSpec(memory_space=pltpu.VMEM)],
        out_specs=pl.BlockSpec(memory_space=pltpu.VMEM), out_shape=jax.ShapeDtypeStruct((P, W), F32),
        scratch_shapes=[pltpu.VMEM((8, P, W), F32), pltpu.SemaphoreType.DMA((7,)), pltpu.SemaphoreType.DMA((7,))],
    )(v)


def add_slabs(a, b_full, cidx):
    n, Rh, W = a.shape
    tr = _rtile(Rh, 2048)

    def body(c_ref, a_ref, b_ref, o_ref):
        o_ref[...] = a_ref[...] + b_ref[0]

    return _pcall(
        body, name="add_halves",
        grid_spec=pltpu.PrefetchScalarGridSpec(
            num_scalar_prefetch=1, grid=(n, Rh // tr),
            in_specs=[pl.BlockSpec((1, tr, W), lambda s, i, c: (s, i, 0)),
                      pl.BlockSpec((1, 1, tr, W), lambda s, i, c: (c[0], s, i, 0))],
            out_specs=pl.BlockSpec((1, tr, W), lambda s, i, c: (s, i, 0))),
        out_shape=jax.ShapeDtypeStruct((n, Rh, W), F32), compiler_params=_cparams(("parallel", "parallel")),
    )(cidx, a, b_full)


def sum_chips(r):
    _, Rh, W = r.shape
    tr = _rtile(Rh, 2048)

    def body(r_ref, o_ref):
        o_ref[...] = ((r_ref[0] + r_ref[1]) + r_ref[2]) + r_ref[3]

    return _pcall(
        body, name="sum_chips", grid=(Rh // tr,), in_specs=[pl.BlockSpec((4, tr, W), lambda i: (0, i, 0))],
        out_specs=pl.BlockSpec((tr, W), lambda i: (i, 0)), out_shape=jax.ShapeDtypeStruct((Rh, W), F32),
        compiler_params=_cparams(("parallel",)),
    )(r)


HI = lax.Precision.HIGHEST
BLOCK_BYTES = 3 * 512 * 1024


def _rows_tile(R, w, T=None):
    cands = [t for t in range(16, R + 1, 16) if R % t == 0 and (T is None or T % t == 0)]
    ok = [t for t in cands if t * w * 4 <= BLOCK_BYTES]
    return max(ok) if ok else min(cands)


def _gsum(x):
    r = jnp.right_shift(lax.broadcasted_iota(jnp.int32, (LANE, LANE), 0), 6)
    c = jnp.right_shift(lax.broadcasted_iota(jnp.int32, (LANE, LANE), 1), 6)
    return jnp.dot(x, (r == c).astype(F32), precision=HI, preferred_element_type=F32)


def _rms(x, g):
    return x * lax.rsqrt(jnp.mean(x * x, axis=-1, keepdims=True) + NORM_EPS) * g


def _silu_mul(gate, up):
    return jax.nn.silu(gate) * up


def _shift(p, prev, mu):
    return p + mu * (prev - p)


def _lora_act(p, prev, mu):
    s = _shift(p, prev, mu)
    return jax.nn.sigmoid(s[:, :G_LORA]), jnp.tanh(s[:, G_LORA:G_LORA + LANE]), s[:, G_LORA + LANE:]


def _prep(k, lw, la, w0, a0, kk_w, ka_w):
    wpre = -jax.nn.softplus(-(w0 + lw)) - 0.5
    decay = jnp.exp(-jnp.exp(wpre))
    a = jax.nn.sigmoid(a0 + la)
    kk = k * kk_w
    kk = kk * lax.rsqrt(jnp.maximum(_gsum(kk * kk), 1e-24))
    k2 = k * (1.0 + (a - 1.0) * ka_w)
    return decay, -kk, kk * a, k2


def _post(y, r, k2, v, g, gnw, gnb, rk):
    mean = _gsum(y) * (1.0 / RWKV_HEAD)
    d = y - mean
    var = _gsum(d * d) * (1.0 / RWKV_HEAD)
    yn = d * lax.rsqrt(var + GN_EPS) * gnw + gnb
    bonus = _gsum(r * k2 * rk) * v
    return (yn + bonus) * g


def _gate_mix(ga, gb, ya, o):
    return jax.nn.sigmoid(ga) * ya + jax.nn.sigmoid(gb) * o


def _rope(x, cos, sin):
    return x * cos + pltpu.roll(x, LANE // 2, 1) * sin


def _rope_t(dy, cos, sin):
    return dy * cos + pltpu.roll(dy * sin, LANE // 2, 1)


def _mla_pre(cq, ckv, kpe, cos, sin, qw, kvw):
    return _rms(cq, qw), _rms(ckv, kvw), _rope(kpe, cos, sin)


def _mla_pre_bwd(cq, ckv, dcqn, dckvn, dkr, cos, sin, qw, kvw):
    _, pull = jax.vjp(lambda a, b, c, d: (_rms(a, c), _rms(b, d)), cq, ckv, qw, kvw)
    dcq, dckv, dqw, dkvw = pull((dcqn, dckvn))
    return dcq, dckv, _rope_t(dkr, cos, sin), dqw, dkvw


WEIGHTS = ['meta_tokens', 'ffn1_norm', 'ffn1_w_gate', 'ffn1_w_up', 'ffn1_w_down', 'mix_norm', 'w_in', 'tm_mu', 'w0',
           'w_up', 'a0', 'a_up', 'g_up', 'k_k', 'k_a', 'r_k', 'gn_w', 'gn_b', 'q_norm', 'w_uq', 'kv_norm', 'w_ukv',
           'w_out', 'ffn2_norm', 'ffn2_w_gate', 'ffn2_w_up', 'ffn2_w_down', 'final_norm']
SHARD_AXIS = {'meta_tokens': 1, 'ffn1_w_gate': 1, 'ffn1_w_up': 1, 'ffn1_w_down': 0, 'w_in': 1, 'w_up': 1, 'a_up': 1,
              'g_up': 1, 'w_uq': 1, 'w_ukv': 1, 'w_out': 0, 'ffn2_w_gate': 1, 'ffn2_w_up': 1, 'ffn2_w_down': 0}
GATHERED = [n for n in WEIGHTS if n in SHARD_AXIS and n != 'meta_tokens']
SCATTERED = [n for n in WEIGHTS if n in SHARD_AXIS]
SMALL = [n for n in WEIGHTS if n not in SHARD_AXIS]
PACK_ROWS = 2048


def _to2d(a):
    if a.ndim == 1:
        return a.reshape(1, -1)
    if a.ndim == 3:
        return a.reshape(a.shape[0] * a.shape[1], a.shape[2]) if a.shape[0] == 1 and a.shape[1] > 64 else a.reshape(1, -1)
    return a


def _pack(parts, dtype, groups):
    flat = [jnp.concatenate([p.reshape(-1).astype(dtype) for p in g]) for g in parts]
    n = flat[0].shape[0]
    unit = 2 * PACK_ROWS * LANE
    npad = -(-n // unit) * unit
    buf = jnp.stack([jnp.pad(f, (0, npad - n)) for f in flat])
    return buf.reshape(groups, 2, npad // (2 * LANE), LANE)


def _unpack(flat, shapes):
    out, off = [], 0
    for shp in shapes:
        n = shp[0] * shp[1]
        out.append(flat[off:off + n].reshape(shp))
        off += n
    return out


def _adamw(w, g, m, v):
    m = ADAM_B1 * m + (1.0 - ADAM_B1) * g
    v = ADAM_B2 * v + (1.0 - ADAM_B2) * jnp.square(g)
    m_hat = m / (1.0 - ADAM_B1 ** ADAM_STEP)
    v_hat = v / (1.0 - ADAM_B2 ** ADAM_STEP)
    delta = -ADAM_LR * (m_hat / (jnp.sqrt(v_hat) + ADAM_EPS) + ADAM_WD * w)
    return delta, m, v


def adamw(w, g, m, v, name):
    R, C = w.shape
    if R % 16 == 0 and R > 16:
        tm = _rows_tile(R, C)
    else:
        tm = R
    return rowwise(_adamw, [(w, C, 0), (g, C, 0), (m, C, 0), (v, C, 0)], [], [(C, C, F32)] * 3, tm=tm, name=name)


def _step(a):
    x = a['x']
    Bl, S, D = x.shape
    T = S + N_META
    R, RS = Bl * T, Bl * S
    Hr, Hm = D // RWKV_HEAD, D // LANE
    w2 = {n: _to2d(a[n]) for n in WEIGHTS}
    m2 = {n: _to2d(a['m_' + n]) for n in WEIGHTS}
    v2 = {n: _to2d(a['v_' + n]) for n in WEIGHTS}
    xi, yi, ci = lax.axis_index("x"), lax.axis_index("y"), lax.axis_index("c")
    chip = 2 * xi + yi

    wall = ag_weights(_pack([[w2[n] for n in GATHERED]], BF16, 1)[0])
    wflat = wall.reshape(4, -1)
    per_chip = [_unpack(wflat[s], [w2[n].shape for n in GATHERED]) for s in range(4)]
    full = {n: jnp.concatenate([per_chip[s][i] for s in range(4)], axis=SHARD_AXIS[n]) for i, n in enumerate(GATHERED)}
    mt = w2['meta_tokens']
    mcols = mt.shape[1]
    mt_z = lax.dynamic_update_slice(jnp.zeros((N_META, D), F32), 0.5 * mt, (jnp.zeros((), jnp.int32), (chip * mcols).astype(jnp.int32)))
    meta_full = allreduce8(mt_z.reshape(-1, LANE), "gather_meta").reshape(N_META, D)

    F = full['ffn1_w_gate'].shape[1]
    win = full['w_in']
    o = 3 * D
    c_xw, c_xa, c_xg = win[:, o:o + W_LORA], win[:, o + W_LORA:o + 2 * W_LORA], win[:, o + 2 * W_LORA:o + 2 * W_LORA + G_LORA]
    o += 2 * W_LORA + G_LORA
    c_cq, c_ckv, c_kpe = win[:, o:o + Q_LORA], win[:, o + Q_LORA:o + Q_LORA + KV_LORA], win[:, o + Q_LORA + KV_LORA:o + Q_LORA + KV_LORA + ROPE_DIM]
    o += Q_LORA + KV_LORA + ROPE_DIM
    c_ga, c_gb = win[:, o:o + D], win[:, o + D:o + 2 * D]
    zc = lambda n: jnp.zeros((D, n), BF16)
    half = ROPE_DIM // 2
    NP0 = 5 * D + 512 + Q_LORA + KV_LORA + LANE
    NP = -(-NP0 // 512) * 512
    win_p = jnp.concatenate([win[:, :3 * D], c_ga, c_gb, c_xg, c_xw, zc(LANE - W_LORA), c_xa, zc(LANE - A_LORA), c_cq, c_ckv,
                             c_kpe[:, :half], zc(half), c_kpe[:, half:], zc(half), zc(NP - NP0)], axis=1)
    O_GA, O_GB, O_L, O_CQ, O_CKV, O_KPE = 3 * D, 4 * D, 5 * D, 5 * D + 512, 5 * D + 512 + Q_LORA, 5 * D + 512 + Q_LORA + KV_LORA
    zr = lambda n: jnp.zeros((n, D), BF16)
    w_up_p = jnp.concatenate([full['w_up'], zr(LANE - W_LORA)], axis=0)
    a_up_p = jnp.concatenate([full['a_up'], zr(LANE - A_LORA)], axis=0)
    g_up = full['g_up']
    wuq = full['w_uq'].reshape(Q_LORA, Hm, QK_DIM)
    zq = jnp.zeros((Q_LORA, Hm, half), BF16)
    wqn = wuq[:, :, :NOPE_DIM].reshape(Q_LORA, Hm * LANE)
    wqp = jnp.concatenate([wuq[:, :, NOPE_DIM:NOPE_DIM + half], zq, wuq[:, :, NOPE_DIM + half:], zq], axis=2).reshape(Q_LORA, Hm * LANE)
    wukv = full['w_ukv'].reshape(KV_LORA, Hm, NOPE_DIM + V_DIM)
    wkn = wukv[:, :, :NOPE_DIM].reshape(KV_LORA, Hm * LANE)
    wv = wukv[:, :, NOPE_DIM:].reshape(KV_LORA, Hm * LANE)
    wout = full['w_out']
    tmu = w2['tm_mu']
    mu_a = tmu[:, :3 * D]
    zm = lambda n: jnp.zeros((1, n), F32)
    mu_b = jnp.concatenate([tmu[:, 3 * D + 2 * W_LORA:], tmu[:, 3 * D:3 * D + W_LORA], zm(LANE - W_LORA),
                            tmu[:, 3 * D + W_LORA:3 * D + 2 * W_LORA], zm(LANE - A_LORA)], axis=1)
    pos = jnp.arange(T, dtype=F32)
    inv_freq = 1.0 / (ROPE_THETA ** (jnp.arange(0, ROPE_DIM, 2, dtype=F32) / ROPE_DIM))
    ang = pos[:, None] * inv_freq[None, :]
    zt = jnp.zeros((T, half), F32)
    cos_t = jnp.concatenate([jnp.cos(ang), zt, jnp.cos(ang), zt], axis=1)
    sin_t = jnp.concatenate([-jnp.sin(ang), zt, jnp.sin(ang), zt], axis=1)
    cos_q, sin_q = cos_t[N_META:], sin_t[N_META:]

    t_full = _rows_tile(R, D)
    t_512 = _rows_tile(R, 512, T)
    t_128 = _rows_tile(R, LANE, T)
    tq_full = _rows_tile(RS, D)
    tq_128 = _rows_tile(RS, LANE, S)

    def rows3(z):
        return z.reshape(Bl, T, z.shape[-1])

    def real_rows(z):
        return rows3(z)[:, N_META:].reshape(RS, z.shape[-1])

    def pad_meta(z):
        z3 = z.reshape(Bl, S, z.shape[-1])
        return jnp.concatenate([jnp.zeros((Bl, N_META, z.shape[-1]), z.dtype), z3], axis=1).reshape(R, z.shape[-1])

    def shift_down(z):
        z3 = rows3(z)
        return jnp.concatenate([jnp.zeros((Bl, 1, z.shape[-1]), z.dtype), z3[:, :-1]], axis=1).reshape(R, z.shape[-1])

    def shift_up(z):
        z3 = rows3(z)
        return jnp.concatenate([z3[:, 1:], jnp.zeros((Bl, 1, z.shape[-1]), z.dtype)], axis=1).reshape(R, z.shape[-1])

    def ffn_fwd(h, nw, wg, wu, wd, tag):
        n = rowwise(_rms, [(h, D, 0)], [(nw, D, 0)], [(D, D, BF16)], tm=t_full, name=tag + "_norm")[0]
        gate = mm(n, wg, name=tag + "_gate")
        up = mm(n, wu, name=tag + "_up")
        act = rowwise(_silu_mul, [(gate, 512, 0), (up, 512, 0)], [], [(F, 512, BF16)], tm=t_512, ncb=F // 512,
                      name=tag + "_act")[0]
        return mm(act, wd, res=h, alpha=0.5, name=tag + "_down"), (h, n, gate, up, act)

    def ffn_bwd(dh2, saved, nw, wg, wu, wd, tag):
        h, n, gate, up, act = saved
        dz = rowwise(lambda d: 0.5 * d, [(dh2, 512, 0)], [], [(D, 512, BF16)], tm=t_512, ncb=D // 512, name=tag + "_dz")[0]
        d_wd = mm(act, dz, "tn", name=tag + "_dwd")
        dact = mm(dz, wd, "nt", name=tag + "_dact")
        dgate, dup = rowwise(vjp_fn(_silu_mul, 2), [(gate, 512, 0), (up, 512, 0), (dact, 512, 0)], [],
                             [(F, 512, BF16)] * 2, tm=t_512, ncb=F // 512, name=tag + "_dact2")
        d_wg = mm(n, dgate, "tn", name=tag + "_dwg")
        d_wu = mm(n, dup, "tn", name=tag + "_dwu")
        dn = mm(dgate, wg, "nt", name=tag + "_dn1")
        dn = mm(dup, wu, "nt", res=dn, name=tag + "_dn2")

        def f(h_, dn_, dh_, nw_):
            dh, dnw = vjp_fn(_rms, 2)(h_, nw_, dn_)
            return dh + dh_, dnw

        dh, d_nw = rowwise(f, [(h, D, 0), (dn, D, 0), (dh2, D, 0)], [(nw, D, 0)], [(D, D, F32)], [(1, D, D)],
                           tm=t_full, name=tag + "_dnorm")
        return dh, d_nw, d_wg, d_wu, d_wd

    h0 = jnp.concatenate([jnp.broadcast_to(meta_full[None], (Bl, N_META, D)), x], axis=1).reshape(R, D)
    h1, sv1 = ffn_fwd(h0, w2['ffn1_norm'], full['ffn1_w_gate'], full['ffn1_w_up'], full['ffn1_w_down'], "ffn1")
    u = rowwise(_rms, [(h1, D, 0)], [(w2['mix_norm'], D, 0)], [(D, D, BF16)], tm=t_full, name="mix_norm")[0]
    proj = mm(u, win_p, name="proj")
    prev_a = shift_down(proj[:, :3 * D])
    prev_b = shift_down(proj[:, O_L:O_L + 512])
    ps = rowwise(_shift, [(proj, 512, 0), (prev_a, 512, 0)], [(mu_a, 512, 0)], [(3 * D, 512, F32)], tm=t_512,
                 ncb=3 * D // 512, name="shift_rkv")[0]
    sg, txw, xas = rowwise(_lora_act, [(proj, 512, O_L // 512), (prev_b, 512, 0)], [(mu_b, 512, 0)],
                           [(G_LORA, G_LORA, BF16), (LANE, LANE, BF16), (LANE, LANE, BF16)], tm=t_512, name="lora_act")
    lw = mm(txw, w_up_p, name="lora_w")
    la = mm(xas, a_up_p, name="lora_a")
    g = mm(sg, g_up, name="lora_g")
    hb = D // LANE
    par_d = lambda n: (w2[n], LANE, 0)
    decay, kn, bb, k2 = rowwise(_prep, [(ps, LANE, hb), (lw, LANE, 0), (la, LANE, 0)],
                                [par_d('w0'), par_d('a0'), par_d('k_k'), par_d('k_a')], [(D, LANE, F32)] * 4,
                                tm=t_128, ncb=hb, name="wkv_prep")

    def to_j(z):
        z = z.reshape(Bl, T, Hr, RWKV_HEAD).transpose(1, 3, 0, 2).reshape(T, RWKV_HEAD, Bl * Hr)
        return jnp.concatenate([z, z], axis=-1)

    def to_i(z):
        return z.reshape(Bl, T, Hr, 2, RWKV_HEAD // 2).transpose(1, 4, 3, 0, 2).reshape(T, RWKV_HEAD // 2, 2 * Bl * Hr)

    def from_i(z):
        return z.reshape(T, RWKV_HEAD // 2, 2, Bl, Hr).transpose(3, 0, 4, 2, 1).reshape(R, D)

    def from_j(z):
        return z[:, :, :Bl * Hr].reshape(T, RWKV_HEAD, Bl, Hr).transpose(2, 0, 3, 1).reshape(R, D)

    r_s, v_s = ps[:, :D], ps[:, 2 * D:]
    jw, jkn, jb, jk, jr, iv = to_j(decay), to_j(kn), to_j(bb), to_j(k2), to_j(r_s), to_i(v_s)
    y_i, sp = wkv_fwd(jw, jkn, jb, jk, jr, iv)
    y = from_i(y_i)
    post_rows = [(y, LANE, 0), (ps, LANE, 0), (k2, LANE, 0), (ps, LANE, 2 * hb), (g, LANE, 0)]
    post_pars = [par_d('gn_w'), par_d('gn_b'), par_d('r_k')]
    ya = rowwise(_post, post_rows, post_pars, [(D, LANE, F32)], tm=t_128, ncb=hb, name="wkv_post")[0]

    nt512 = T // t_512
    mla_rows = [(proj, Q_LORA, O_CQ // Q_LORA), (proj, KV_LORA, O_CKV // KV_LORA), (proj, LANE, O_KPE // LANE)]
    tabs = [(cos_t, LANE, 0, nt512, True), (sin_t, LANE, 0, nt512, True)]
    mla_pars = [(w2['q_norm'], Q_LORA, 0), (w2['kv_norm'], KV_LORA, 0)]
    cqn, ckvn, kpr = rowwise(_mla_pre, mla_rows + tabs, mla_pars,
                             [(Q_LORA, Q_LORA, BF16), (KV_LORA, KV_LORA, BF16), (LANE, LANE, BF16)], tm=t_512, name="mla_pre")
    cqn_r = real_rows(cqn)
    qn = mm(cqn_r, wqn, out_dtype=BF16, name="q_nope")
    qp_raw = mm(cqn_r, wqp, name="q_pe")
    ntq = S // tq_128
    qtabs = [(cos_q, LANE, 0, ntq, True), (sin_q, LANE, 0, ntq, True)]
    qp = rowwise(_rope, [(qp_raw, LANE, 0)] + qtabs, [], [(D, LANE, BF16)], tm=tq_128, ncb=Hm, name="q_rope")[0]
    knope = mm(ckvn, wkn, out_dtype=BF16, name="k_nope")
    vv = mm(ckvn, wv, out_dtype=BF16, name="v_proj")

    def pad_keys(z):
        z3 = rows3(z)
        return jnp.concatenate([z3[:, :N_META], jnp.zeros((Bl, Q_BLOCK - N_META, z.shape[-1]), z.dtype), z3[:, N_META:]], axis=1)

    def unpad_keys(z):
        return jnp.concatenate([z[:, :N_META], z[:, Q_BLOCK:]], axis=1).reshape(R, z.shape[-1])

    qn3, qp3 = qn.reshape(Bl, S, D), qp.reshape(Bl, S, D)
    knp, kpp, vp = pad_keys(knope), pad_keys(kpr), pad_keys(vv)
    o_att = pad_meta(attn_fwd(qn3, qp3, knp, kpp, vp).reshape(RS, D))
    mix_rows = [(proj, 512, O_GA // 512), (proj, 512, O_GB // 512), (ya, 512, 0), (o_att, 512, 0)]
    mix = rowwise(_gate_mix, mix_rows, [], [(D, 512, BF16)], tm=t_512, ncb=D // 512, name="gate_mix")[0]
    h2 = mm(mix, wout, res=h1, name="w_out")
    h3, sv2 = ffn_fwd(h2, w2['ffn2_norm'], full['ffn2_w_gate'], full['ffn2_w_up'], full['ffn2_w_down'], "ffn2")

    def loss_fb(h_, tgt, fw):
        yv, pull = jax.vjp(_rms, h_, fw)
        e = yv - tgt
        dh, dfw = pull(e * (1.0 / D))
        return dh, jnp.full((1, LANE), 0.5 / D * jnp.sum(e * e), F32), dfw

    dh3r, lossp, g_final = rowwise(loss_fb, [(real_rows(h3), D, 0), (a['loss_target'].reshape(RS, D), D, 0)],
                                   [(w2['final_norm'], D, 0)], [(D, D, F32)], [(1, LANE, LANE), (1, D, D)],
                                   tm=tq_full, name="loss")
    dh3 = pad_meta(dh3r)

    gr = {'final_norm': g_final}
    dh2, gr['ffn2_norm'], gr['ffn2_w_gate'], gr['ffn2_w_up'], gr['ffn2_w_down'] = ffn_bwd(
        dh3, sv2, w2['ffn2_norm'], full['ffn2_w_gate'], full['ffn2_w_up'], full['ffn2_w_down'], "ffn2")
    dh2b = rowwise(lambda d: d, [(dh2, 512, 0)], [], [(D, 512, BF16)], tm=t_512, ncb=D // 512, name="dh2_cast")[0]
    gr['w_out'] = mm(mix, dh2b, "tn", name="d_wout")
    dmix = mm(dh2b, wout, "nt", name="d_mix")
    dga, dgb, dya, do = rowwise(vjp_fn(_gate_mix, 4), mix_rows + [(dmix, 512, 0)], [],
                                [(D, 512, BF16), (D, 512, BF16), (D, 512, F32), (D, 512, BF16)], tm=t_512, ncb=D // 512,
                                name="d_gate_mix")
    dqn, dqp, dknp, dvp, dkpp = attn_bwd(qn3, qp3, knp, kpp, vp, real_rows(do).reshape(Bl, S, D))
    dqn2 = dqn.reshape(RS, D)
    dqp_raw = rowwise(_rope_t, [(dqp.reshape(RS, D), LANE, 0)] + qtabs, [], [(D, LANE, BF16)], tm=tq_128, ncb=Hm,
                      name="d_q_rope")[0]
    d_wqn = mm(cqn_r, dqn2, "tn", name="d_wqn")
    d_wqp = mm(cqn_r, dqp_raw, "tn", name="d_wqp")
    dcqn = mm(dqn2, wqn, "nt", name="d_cqn1")
    dcqn = pad_meta(mm(dqp_raw, wqp, "nt", res=dcqn, name="d_cqn2"))
    dkn2, dv2, dkp2 = unpad_keys(dknp), unpad_keys(dvp), unpad_keys(dkpp)
    d_wkn = mm(ckvn, dkn2, "tn", name="d_wkn")
    d_wv = mm(ckvn, dv2, "tn", name="d_wv")
    dckvn = mm(dkn2, wkn, "nt", name="d_ckvn1")
    dckvn = mm(dv2, wv, "nt", res=dckvn, name="d_ckvn2")
    dcq, dckv, dkpe, gr['q_norm'], gr['kv_norm'] = rowwise(
        _mla_pre_bwd, mla_rows[:2] + [(dcqn, Q_LORA, 0), (dckvn, KV_LORA, 0), (dkp2, LANE, 0)] + tabs, mla_pars,
        [(Q_LORA, Q_LORA, BF16), (KV_LORA, KV_LORA, BF16), (LANE, LANE, BF16)], [(1, Q_LORA, Q_LORA), (1, KV_LORA, KV_LORA)],
        tm=t_512, name="d_mla_pre")

    def post_bwd(y_, r_, k2_, v_, g_, dya_, gnw, gnb, rk):
        return vjp_fn(_post, 8)(y_, r_, k2_, v_, g_, gnw, gnb, rk, dya_)

    dy, dr_b, dk2_b, dv_b, dg, gr['gn_w'], gr['gn_b'], gr['r_k'] = rowwise(
        post_bwd, post_rows + [(dya, LANE, 0)], post_pars,
        [(D, LANE, F32)] * 4 + [(D, LANE, BF16)], [(1, D, LANE)] * 3, tm=t_128, ncb=hb, name="d_wkv_post")
    jdw, jdkn, jdb, jdk, jdr, idv = wkv_bwd(jw, jkn, jb, jk, jr, iv, to_i(dy), sp)
    ddecay, dkn_w, db_w, dk2_w, dr_w, dv_w = from_j(jdw), from_j(jdkn), from_j(jdb), from_j(jdk), from_j(jdr), from_i(idv)

    def prep_bwd(k_, lw_, la_, dd, dkn_, db_, dk2a, dk2b, w0, a0, kkw, kaw):
        return vjp_fn(_prep, 7)(k_, lw_, la_, w0, a0, kkw, kaw, dd, dkn_, db_, dk2a + dk2b)

    dk_s, dlw, dla, gr['w0'], gr['a0'], gr['k_k'], gr['k_a'] = rowwise(
        prep_bwd, [(ps, LANE, hb), (lw, LANE, 0), (la, LANE, 0), (ddecay, LANE, 0), (dkn_w, LANE, 0), (db_w, LANE, 0),
                   (dk2_w, LANE, 0), (dk2_b, LANE, 0)],
        [par_d('w0'), par_d('a0'), par_d('k_k'), par_d('k_a')], [(D, LANE, F32), (D, LANE, BF16), (D, LANE, BF16)],
        [(1, D, LANE)] * 4, tm=t_128, ncb=hb, name="d_wkv_prep")
    d_wup = mm(txw, dlw, "tn", name="d_wup")
    dtxw = mm(dlw, w_up_p, "nt", name="d_txw")
    d_aup = mm(xas, dla, "tn", name="d_aup")
    dxa = mm(dla, a_up_p, "nt", name="d_xa")
    gr['g_up'] = mm(sg, dg, "tn", name="d_gup")
    dsg = mm(dg, g_up, "nt", name="d_sg")

    def lora_bwd(p_, prev_, dsg_, dt_, dxa_, mu_):
        return vjp_fn(_lora_act, 3)(p_, prev_, mu_, dsg_, dt_, dxa_)

    dpb, dprevb, dmu_b = rowwise(lora_bwd, [(proj, 512, O_L // 512), (prev_b, 512, 0), (dsg, G_LORA, 0), (dtxw, LANE, 0),
                                            (dxa, LANE, 0)], [(mu_b, 512, 0)], [(512, 512, F32)] * 2, [(1, 512, 512)],
                                 tm=t_512, name="d_lora_act")

    def shift_bwd2(p_, prev_, c1, c2, mu_):
        return vjp_fn(_shift, 3)(p_, prev_, mu_, c1 + c2)

    def shift_bwd1(p_, prev_, c1, mu_):
        return vjp_fn(_shift, 3)(p_, prev_, mu_, c1)

    def shift_back(sec, cts):
        nb = D // 512
        f = shift_bwd2 if len(cts) == 2 else shift_bwd1
        return rowwise(f, [(proj, 512, sec * nb), (prev_a, 512, sec * nb)] + [(c, 512, 0) for c in cts],
                       [(mu_a, 512, sec * nb)], [(D, 512, F32)] * 2, [(1, D, 512)], tm=t_512, ncb=nb, name=f"d_shift{sec}")

    dp_r, dprev_r, dmu_r = shift_back(0, [dr_w, dr_b])
    dp_k, dprev_k, dmu_k = shift_back(1, [dk_s])
    dp_v, dprev_v, dmu_v = shift_back(2, [dv_w, dv_b])

    def add_cast(p_, q_):
        return p_ + q_

    def dsec(dp_, dprev_, tag):
        C = dp_.shape[1]
        return rowwise(add_cast, [(dp_, 512, 0), (shift_up(dprev_), 512, 0)], [], [(C, 512, BF16)], tm=t_512, ncb=C // 512,
                       name="d_sec_" + tag)[0]

    zpad = jnp.zeros((R, NP - NP0), BF16)
    dproj = jnp.concatenate([dsec(dp_r, dprev_r, "r"), dsec(dp_k, dprev_k, "k"), dsec(dp_v, dprev_v, "v"), dga, dgb,
                             dsec(dpb, dprevb, "l"), dcq, dckv, dkpe, zpad], axis=1)
    d_win_p = mm(u, dproj, "tn", name="d_win")
    du = mm(dproj, win_p, "nt", name="d_u")

    def norm_bwd(h_, dn_, dh_, nw_):
        dh, dnw = vjp_fn(_rms, 2)(h_, nw_, dn_)
        return dh + dh_, dnw

    dh1, gr['mix_norm'] = rowwise(norm_bwd, [(h1, D, 0), (du, D, 0), (dh2, D, 0)], [(w2['mix_norm'], D, 0)],
                                  [(D, D, F32)], [(1, D, D)], tm=t_full, name="d_mix_norm")
    dh0, gr['ffn1_norm'], gr['ffn1_w_gate'], gr['ffn1_w_up'], gr['ffn1_w_down'] = ffn_bwd(
        dh1, sv1, w2['ffn1_norm'], full['ffn1_w_gate'], full['ffn1_w_up'], full['ffn1_w_down'], "ffn1")
    dh0_3 = rows3(dh0)
    grad_x = dh0_3[:, N_META:]
    gr['meta_tokens'] = jnp.sum(dh0_3[:, :N_META], axis=0)

    gr['w_in'] = jnp.concatenate([
        d_win_p[:, :3 * D], d_win_p[:, O_L + G_LORA:O_L + G_LORA + W_LORA], d_win_p[:, O_L + G_LORA + LANE:O_L + G_LORA + LANE + A_LORA],
        d_win_p[:, O_L:O_L + G_LORA], d_win_p[:, O_CQ:O_CQ + Q_LORA], d_win_p[:, O_CKV:O_CKV + KV_LORA],
        d_win_p[:, O_KPE:O_KPE + half], d_win_p[:, O_KPE + 2 * half:O_KPE + 3 * half], d_win_p[:, O_GA:O_GA + 2 * D]], axis=1)
    gr['tm_mu'] = jnp.concatenate([dmu_r, dmu_k, dmu_v, dmu_b[:, G_LORA:G_LORA + W_LORA],
                                   dmu_b[:, G_LORA + LANE:G_LORA + LANE + A_LORA], dmu_b[:, :G_LORA]], axis=1)
    gr['w_up'], gr['a_up'] = d_wup[:W_LORA], d_aup[:A_LORA]
    dq3n, dq3p = d_wqn.reshape(Q_LORA, Hm, LANE), d_wqp.reshape(Q_LORA, Hm, LANE)
    gr['w_uq'] = jnp.concatenate([dq3n, dq3p[:, :, :half], dq3p[:, :, 2 * half:3 * half]], axis=2).reshape(Q_LORA, Hm * QK_DIM)
    gr['w_ukv'] = jnp.concatenate([d_wkn.reshape(KV_LORA, Hm, LANE), d_wv.reshape(KV_LORA, Hm, LANE)], axis=2).reshape(
        KV_LORA, Hm * (NOPE_DIM + V_DIM))

    parts = [[jnp.split(gr[n], 4, axis=SHARD_AXIS[n])[s] for n in SCATTERED] for s in range(4)]
    gp = _pack(parts, F32, 4).transpose(1, 0, 2, 3)
    got = swap_halves(gp)
    hsum = add_slabs(got, gp, ci.reshape(1).astype(jnp.int32))
    red = share_sibling(sum_chips(rs_chips(hsum)))
    shard_shapes = [w2[n].shape for n in SCATTERED]
    g_shard = dict(zip(SCATTERED, _unpack(red.reshape(-1), shard_shapes)))
    small = jnp.concatenate([gr[n].reshape(-1) for n in SMALL] + [lossp.reshape(-1)])
    ns = small.shape[0]
    nsp = -(-ns // (8 * LANE)) * 8 * LANE
    small_sum = allreduce8(jnp.pad(small, (0, nsp - ns)).reshape(-1, LANE), "allreduce_small").reshape(-1)
    g_small = dict(zip(SMALL, _unpack(small_sum, [w2[n].shape for n in SMALL])))
    loss = small_sum[ns - LANE]

    grads, deltas, new_m, new_v = [], [], [], []
    for n in WEIGHTS:
        gw = g_shard[n] if n in g_shard else g_small[n]
        d_, m_, v_ = adamw(w2[n], gw, m2[n], v2[n], "adamw_" + n)
        shp = a[n].shape
        grads.append(gw.reshape(shp))
        deltas.append(d_.reshape(shp))
        new_m.append(m_.reshape(shp))
        new_v.append(v_.reshape(shp))
    return (loss, grad_x, *grads, *deltas, *new_m, *new_v)


def kernel(x, meta_tokens, ffn1_norm, ffn1_w_gate, ffn1_w_up, ffn1_w_down, mix_norm, w_in, tm_mu, w0, w_up, a0, a_up, g_up, k_k, k_a, r_k, gn_w, gn_b, q_norm, w_uq, kv_norm, w_ukv, w_out, ffn2_norm, ffn2_w_gate, ffn2_w_up, ffn2_w_down, final_norm, loss_target, m_meta_tokens, m_ffn1_norm, m_ffn1_w_gate, m_ffn1_w_up, m_ffn1_w_down, m_mix_norm, m_w_in, m_tm_mu, m_w0, m_w_up, m_a0, m_a_up, m_g_up, m_k_k, m_k_a, m_r_k, m_gn_w, m_gn_b, m_q_norm, m_w_uq, m_kv_norm, m_w_ukv, m_w_out, m_ffn2_norm, m_ffn2_w_gate, m_ffn2_w_up, m_ffn2_w_down, m_final_norm, v_meta_tokens, v_ffn1_norm, v_ffn1_w_gate, v_ffn1_w_up, v_ffn1_w_down, v_mix_norm, v_w_in, v_tm_mu, v_w0, v_w_up, v_a0, v_a_up, v_g_up, v_k_k, v_k_a, v_r_k, v_gn_w, v_gn_b, v_q_norm, v_w_uq, v_kv_norm, v_w_ukv, v_w_out, v_ffn2_norm, v_ffn2_w_gate, v_ffn2_w_up, v_ffn2_w_down, v_final_norm):
    return _step(dict(locals()))
```

```python
import functools
import math

import jax
import jax.numpy as jnp
import numpy as np
from jax import lax
from jax.experimental import pallas as pl
from jax.experimental.pallas import tpu as pltpu

F32 = jnp.float32
BF16 = jnp.bfloat16
MESH = pl.DeviceIdType.MESH

N_META = 16
NORM_EPS = 1e-6
RWKV_HEAD = 64
GN_EPS = RWKV_HEAD * 1e-5
W_LORA, A_LORA, G_LORA = 96, 96, 256
Q_LORA, KV_LORA = 512, 512
NOPE_DIM, ROPE_DIM, V_DIM = 128, 64, 128
QK_DIM = NOPE_DIM + ROPE_DIM
ROPE_THETA = 10000.0
Q_BLOCK = 128
ADAM_LR, ADAM_B1, ADAM_B2, ADAM_EPS, ADAM_WD, ADAM_STEP = 0.001, 0.9, 0.999, 1e-08, 0.01, 10

LANE = 128
VMEM_LIMIT = 56 * 1024 * 1024


def _pcall(body, **kw):
    return pl.pallas_call(body, **kw)


def _cparams(sem):
    return pltpu.CompilerParams(dimension_semantics=sem, vmem_limit_bytes=VMEM_LIMIT)


MM_LANE_TILE = 1536
MM_ROW_TILE = 1408


def _div_tile(n, cap, unit):
    best = None
    for t in range(unit, min(n, cap) + 1, unit):
        if n % t == 0:
            best = t
    return best if best is not None else n


def _rtile(n, pref=512):
    best = None
    for t in range(16, min(n, pref * 2) + 1, 16):
        if n % t == 0 and (best is None or abs(t - pref) < abs(best - pref)):
            best = t
    return best if best is not None else n


def mm(a, b, mode="nn", out_dtype=F32, res=None, alpha=1.0, b_stack=False, out_stack=False, name="mm"):
    cs = b.shape[-1] if b_stack else None
    bs = (b.shape[1], 4 * cs) if b_stack else b.shape
    if mode == "nn":
        (M, K), (K2, N) = a.shape, bs
    elif mode == "nt":
        (M, K), (N, K2) = a.shape, bs
    else:
        (K, M), (K2, N) = a.shape, bs
    assert K == K2, (a.shape, b.shape, mode)
    if mode == "tn":
        tm, tk = _div_tile(M, MM_ROW_TILE, LANE), _div_tile(K, 1024, 16)
    else:
        tm = _div_tile(M, MM_ROW_TILE, 16)
        tk = _div_tile(cs if (b_stack and mode == "nt") else K, MM_LANE_TILE, LANE)
    ncol = N // 4 if out_stack else (cs if (b_stack and mode == "nn") else N)
    tn = _div_tile(ncol, MM_LANE_TILE if mode != "nt" else 1024, LANE)
    nk = K // tk
    dims = {"nn": (((1,), (0,)), ((), ())), "nt": (((1,), (1,)), ((), ())), "tn": (((0,), (0,)), ((), ()))}[mode]
    direct = out_dtype == F32

    def body(*refs):
        a_ref, b_ref = refs[:2]
        r_ref = refs[2] if res is not None else None
        o_ref = refs[3] if res is not None else refs[2]
        acc = o_ref if direct else refs[-1]
        k = pl.program_id(2)

        @pl.when(k == 0)
        def _():
            acc[...] = jnp.zeros_like(acc) if res is None else r_ref[...].astype(F32)

        p = lax.dot_general(a_ref[...].astype(BF16), b_ref[...].astype(BF16), dims, preferred_element_type=F32)
        acc[...] += p if alpha == 1.0 else alpha * p

        if not direct:
            @pl.when(k == nk - 1)
            def _():
                o_ref[...] = acc[...].astype(o_ref.dtype)

    if mode == "tn":
        a_spec = pl.BlockSpec((tk, tm), lambda i, j, k: (k, i))
        b_spec = pl.BlockSpec((tk, tn), lambda i, j, k: (k, j))
    else:
        a_spec = pl.BlockSpec((tm, tk), lambda i, j, k: (i, k))
        if mode == "nn":
            if b_stack:
                nps = cs // tn
                b_spec = pl.BlockSpec((None, tk, tn), lambda i, j, k: (j // nps, k, j % nps))
            else:
                b_spec = pl.BlockSpec((tk, tn), lambda i, j, k: (k, j))
        elif b_stack:
            kps = cs // tk
            b_spec = pl.BlockSpec((None, tn, tk), lambda i, j, k: (k // kps, j, k % kps))
        else:
            b_spec = pl.BlockSpec((tn, tk), lambda i, j, k: (j, k))
    r_spec = pl.BlockSpec((tm, tn), lambda i, j, k: (i, j))
    if out_stack:
        ops = (N // 4) // tn
        o_spec = pl.BlockSpec((None, tm, tn), lambda i, j, k: (j // ops, i, j % ops))
        o_shape = jax.ShapeDtypeStruct((4, M, N // 4), out_dtype)
    else:
        o_spec, o_shape = r_spec, jax.ShapeDtypeStruct((M, N), out_dtype)
    in_specs = [a_spec, b_spec] + ([r_spec] if res is not None else [])
    args = (a, b) + ((res,) if res is not None else ())
    return _pcall(
        body, name=name, grid=(M // tm, N // tn, nk), in_specs=in_specs, out_specs=o_spec, out_shape=o_shape,
        scratch_shapes=[] if direct else [pltpu.VMEM((tm, tn), F32)],
        compiler_params=_cparams(("parallel", "parallel", "arbitrary")),
    )(*args)


def rowwise(fn, row_ins, par_ins, outs, accs=(), *, tm, ncb=1, name="rowwise"):
    R = row_ins[0][0].shape[0]
    assert R % tm == 0, (R, tm)
    nrb = R // tm
    in_specs, args = [], []
    for spec in row_ins:
        arr, w, base = spec[:3]
        mod = spec[3] if len(spec) > 3 else None
        cstep = 0 if (len(spec) > 4 and spec[4]) else 1
        if mod is None:
            in_specs.append(pl.BlockSpec((tm, w), lambda j, i, base=base, cstep=cstep: (i, base + cstep * j)))
        else:
            in_specs.append(pl.BlockSpec((tm, w), lambda j, i, base=base, mod=mod, cstep=cstep: (i % mod, base + cstep * j)))
        args.append(arr)
    for arr, w, base in par_ins:
        in_specs.append(pl.BlockSpec((arr.shape[0], w), lambda j, i, base=base: (0, base + j)))
        args.append(arr)
    out_specs, out_shape = [], []
    for cols, w, dt in outs:
        out_specs.append(pl.BlockSpec((tm, w), lambda j, i: (i, j)))
        out_shape.append(jax.ShapeDtypeStruct((R, cols), dt))
    for p, cols, w in accs:
        out_specs.append(pl.BlockSpec((p, w), lambda j, i: (0, j)))
        out_shape.append(jax.ShapeDtypeStruct((p, cols), F32))
    nin, nout, nacc = len(args), len(outs), len(accs)

    def body(*refs):
        vals = [r[...] for r in refs[:nin]]
        res = fn(*vals)
        if not isinstance(res, (tuple, list)):
            res = (res,)
        assert len(res) == nout + nacc, (len(res), nout, nacc)
        for o_ref, v in zip(refs[nin:nin + nout], res[:nout]):
            o_ref[...] = v.astype(o_ref.dtype)
        if nacc:
            i = pl.program_id(1)

            @pl.when(i == 0)
            def _():
                for a_ref in refs[nin + nout:]:
                    a_ref[...] = jnp.zeros_like(a_ref)

            for a_ref, v in zip(refs[nin + nout:], res[nout:]):
                a_ref[...] += v.astype(F32)

    r = _pcall(
        body, name=name, grid=(ncb, nrb), in_specs=in_specs, out_specs=out_specs, out_shape=out_shape,
        compiler_params=_cparams(("parallel", "arbitrary")),
    )(*args)
    return r


def vjp_fn(fwd, nprim):
    def f(*vals):
        prim, cts = vals[:nprim], vals[nprim:]
        out, pull = jax.vjp(fwd, *[p.astype(F32) for p in prim])
        if not isinstance(out, (tuple, list)):
            cts = cts[0].astype(F32)
        else:
            cts = tuple(c.astype(F32) for c in cts)
        return pull(cts)
    return f


WKV_TB = 8


def wkv_fwd(w, kn, b, k, r, v):
    T, N, L = w.shape
    NI = v.shape[1]
    tb = WKV_TB
    assert T % tb == 0

    def body(w_ref, kn_ref, b_ref, k_ref, r_ref, v_ref, y_ref, sp_ref, s_ref):
        @pl.when(pl.program_id(0) == 0)
        def _():
            s_ref[...] = jnp.zeros_like(s_ref)

        def step(s, carry):
            W, KN, B, Kk, Rr = w_ref[s], kn_ref[s], b_ref[s], k_ref[s], r_ref[s]
            for i in range(NI):
                S = s_ref[i]
                sp_ref[s, i] = S
                sa = jnp.sum(S * KN, axis=0, keepdims=True)
                vi = v_ref[s, pl.ds(i, 1), :]
                Sn = S * W + sa * B + vi * Kk
                s_ref[i] = Sn
                y_ref[s, pl.ds(i, 1), :] = jnp.sum(Sn * Rr, axis=0, keepdims=True)
            return carry

        lax.fori_loop(0, tb, step, 0)

    jspec = pl.BlockSpec((tb, N, L), lambda t: (t, 0, 0))
    ispec = pl.BlockSpec((tb, NI, L), lambda t: (t, 0, 0))
    return _pcall(
        body, name="wkv_fwd", grid=(T // tb,), in_specs=[jspec] * 5 + [ispec],
        out_specs=[ispec, pl.BlockSpec((tb, NI, N, L), lambda t: (t, 0, 0, 0))],
        out_shape=[jax.ShapeDtypeStruct((T, NI, L), F32), jax.ShapeDtypeStruct((T, NI, N, L), F32)],
        scratch_shapes=[pltpu.VMEM((NI, N, L), F32)],
        compiler_params=_cparams(("arbitrary",)),
    )(w, kn, b, k, r, v)


def wkv_bwd(w, kn, b, k, r, v, dy, sp):
    T, N, L = w.shape
    NI = v.shape[1]
    tb = WKV_TB
    nt = T // tb

    def body(w_ref, kn_ref, b_ref, k_ref, r_ref, v_ref, dy_ref, sp_ref,
             dw_ref, dkn_ref, db_ref, dk_ref, dr_ref, dv_ref, ds_ref):
        @pl.when(pl.program_id(0) == 0)
        def _():
            ds_ref[...] = jnp.zeros_like(ds_ref)

        def step(q, carry):
            s = tb - 1 - q
            W, KN, B, Kk, Rr = w_ref[s], kn_ref[s], b_ref[s], k_ref[s], r_ref[s]
            dW = jnp.zeros((N, L), F32)
            dKN, dB, dK, dR = dW, dW, dW, dW
            for i in range(NI):
                Sp = sp_ref[s, i]
                vi = v_ref[s, pl.ds(i, 1), :]
                dyi = dy_ref[s, pl.ds(i, 1), :]
                sa = jnp.sum(Sp * KN, axis=0, keepdims=True)
                Sn = Sp * W + sa * B + vi * Kk
                dS = ds_ref[i] + dyi * Rr
                dR = dR + Sn * dyi
                dv_ref[s, pl.ds(i, 1), :] = jnp.sum(dS * Kk, axis=0, keepdims=True)
                dK = dK + dS * vi
                dsa = jnp.sum(dS * B, axis=0, keepdims=True)
                dB = dB + dS * sa
                dW = dW + dS * Sp
                dKN = dKN + Sp * dsa
                ds_ref[i] = dS * W + dsa * KN
            half = L // 2
            for ref, val in ((dw_ref, dW), (dkn_ref, dKN), (db_ref, dB), (dk_ref, dK), (dr_ref, dR)):
                ref[s] = val + pltpu.roll(val, half, 1)
            return carry

        lax.fori_loop(0, tb, step, 0)

    jspec = pl.BlockSpec((tb, N, L), lambda t: (nt - 1 - t, 0, 0))
    ispec = pl.BlockSpec((tb, NI, L), lambda t: (nt - 1 - t, 0, 0))
    jshape = jax.ShapeDtypeStruct((T, N, L), F32)
    return _pcall(
        body, name="wkv_bwd", grid=(nt,),
        in_specs=[jspec] * 5 + [ispec, ispec, pl.BlockSpec((tb, NI, N, L), lambda t: (nt - 1 - t, 0, 0, 0))],
        out_specs=[jspec] * 5 + [ispec], out_shape=[jshape] * 5 + [jax.ShapeDtypeStruct((T, NI, L), F32)],
        scratch_shapes=[pltpu.VMEM((NI, N, L), F32)],
        compiler_params=_cparams(("arbitrary",)),
    )(w, kn, b, k, r, v, dy, sp)


_NT = (((1,), (1,)), ((), ()))
_TN = (((0,), (0,)), ((), ()))
ATT_SCALE = QK_DIM ** -0.5


def _att_probs(q1, q2, kn_ref, kp_ref, i, L):
    s = lax.dot_general(q1, kn_ref[0, :L, :], _NT, preferred_element_type=F32)
    s = s + lax.dot_general(q2, kp_ref[0, :L, :], _NT, preferred_element_type=F32)
    s = s * ATT_SCALE
    col = lax.broadcasted_iota(jnp.int32, (Q_BLOCK, L), 1)
    row = lax.broadcasted_iota(jnp.int32, (Q_BLOCK, L), 0)
    valid = (col < N_META) | ((col >= Q_BLOCK) & (col - Q_BLOCK <= Q_BLOCK * i + row))
    s = jnp.where(valid, s, -1e30)
    m = jnp.max(s, axis=-1, keepdims=True)
    p = jnp.exp(s - m)
    return p / jnp.sum(p, axis=-1, keepdims=True)


def attn_fwd(qn, qp, kn, kp, v):
    B, S, HD = qn.shape
    SP = kn.shape[1]
    H = HD // LANE
    nq = S // Q_BLOCK

    def body(qn_ref, qp_ref, kn_ref, kp_ref, v_ref, o_ref):
        for i in range(nq):
            L = Q_BLOCK * (i + 2)
            rows = pl.ds(Q_BLOCK * i, Q_BLOCK)
            p = _att_probs(qn_ref[0, rows, :], qp_ref[0, rows, :], kn_ref, kp_ref, i, L)
            o_ref[0, rows, :] = jnp.dot(p.astype(BF16), v_ref[0, :L, :], preferred_element_type=F32)

    qspec = pl.BlockSpec((1, S, LANE), lambda b, h: (b, 0, h))
    kspec = pl.BlockSpec((1, SP, LANE), lambda b, h: (b, 0, h))
    pspec = pl.BlockSpec((1, SP, LANE), lambda b, h: (b, 0, 0))
    return _pcall(
        body, name="attn_fwd", grid=(B, H), in_specs=[qspec, qspec, kspec, pspec, kspec], out_specs=qspec,
        out_shape=jax.ShapeDtypeStruct((B, S, HD), F32), compiler_params=_cparams(("parallel", "parallel")),
    )(qn, qp, kn, kp, v)


def attn_bwd(qn, qp, kn, kp, v, do):
    B, S, HD = qn.shape
    SP = kn.shape[1]
    H = HD // LANE
    nq = S // Q_BLOCK

    def body(qn_ref, qp_ref, kn_ref, kp_ref, v_ref, do_ref, dqn_ref, dqp_ref, dkn_ref, dv_ref, dkp_ref, dkn_acc, dv_acc):
        @pl.when(pl.program_id(1) == 0)
        def _():
            dkp_ref[...] = jnp.zeros_like(dkp_ref)

        dkn_acc[...] = jnp.zeros_like(dkn_acc)
        dv_acc[...] = jnp.zeros_like(dv_acc)
        for i in range(nq):
            L = Q_BLOCK * (i + 2)
            rows = pl.ds(Q_BLOCK * i, Q_BLOCK)
            q1, q2, do_i = qn_ref[0, rows, :], qp_ref[0, rows, :], do_ref[0, rows, :]
            p = _att_probs(q1, q2, kn_ref, kp_ref, i, L)
            dp = lax.dot_general(do_i, v_ref[0, :L, :], _NT, preferred_element_type=F32)
            ds = (p * (dp - jnp.sum(p * dp, axis=-1, keepdims=True)) * ATT_SCALE).astype(BF16)
            dqn_ref[0, rows, :] = jnp.dot(ds, kn_ref[0, :L, :], preferred_element_type=F32).astype(dqn_ref.dtype)
            dqp_ref[0, rows, :] = jnp.dot(ds, kp_ref[0, :L, :], preferred_element_type=F32)
            dkn_acc[:L, :] += lax.dot_general(ds, q1, _TN, preferred_element_type=F32)
            dkp_ref[0, :L, :] += lax.dot_general(ds, q2, _TN, preferred_element_type=F32)
            dv_acc[:L, :] += lax.dot_general(p.astype(BF16), do_i, _TN, preferred_element_type=F32)
        dkn_ref[0] = dkn_acc[...].astype(dkn_ref.dtype)
        dv_ref[0] = dv_acc[...].astype(dv_ref.dtype)

    qspec = pl.BlockSpec((1, S, LANE), lambda b, h: (b, 0, h))
    kspec = pl.BlockSpec((1, SP, LANE), lambda b, h: (b, 0, h))
    pspec = pl.BlockSpec((1, SP, LANE), lambda b, h: (b, 0, 0))
    return _pcall(
        body, name="attn_bwd", grid=(B, H), in_specs=[qspec, qspec, kspec, pspec, kspec, qspec],
        out_specs=[qspec, qspec, kspec, kspec, pspec],
        out_shape=[jax.ShapeDtypeStruct((B, S, HD), BF16), jax.ShapeDtypeStruct((B, S, HD), F32),
                   jax.ShapeDtypeStruct((B, SP, HD), BF16), jax.ShapeDtypeStruct((B, SP, HD), BF16),
                   jax.ShapeDtypeStruct((B, SP, LANE), F32)],
        scratch_shapes=[pltpu.VMEM((SP, LANE), F32), pltpu.VMEM((SP, LANE), F32)],
        compiler_params=_cparams(("parallel", "arbitrary")),
    )(qn, qp, kn, kp, v, do)


_HBM = pl.BlockSpec(memory_space=pltpu.HBM)


def _place():
    x, y, c = lax.axis_index("x"), lax.axis_index("y"), lax.axis_index("c")
    chips = [(1 - x, y), (x, 1 - y), (1 - x, 1 - y)]
    return x, y, c, chips


def _rcopy(src, dst, ssem, rsem, dev):
    return pltpu.make_async_remote_copy(src_ref=src, dst_ref=dst, send_sem=ssem, recv_sem=rsem,
                                        device_id=dev, device_id_type=MESH)


def _half_rows(c, r):
    return pl.ds(pl.multiple_of(c * (r // 2), 16), r // 2)


def ag_weights(shards):
    n = len(shards)

    def body(*refs):
        w, out = refs[:n], refs[n:2 * n]
        ssem, rsem, lsem = refs[2 * n:]
        x, y, c, chips = _place()
        s = 2 * x + y
        rows = [_half_rows(c, w[i].shape[0]) for i in range(n)]
        orows = [_half_rows(1 - c, w[i].shape[0]) for i in range(n)]
        local = [pltpu.make_async_copy(w[i], out[i].at[s], lsem.at[i]) for i in range(n)]
        for cp in local:
            cp.start()
        first = [_rcopy(w[i].at[rows[i]], out[i].at[s, rows[i]], ssem.at[6 * i + j], rsem.at[6 * i + j], (px, py, c))
                 for i in range(n) for j, (px, py) in enumerate(chips)]
        for cp in first:
            cp.start()
        passed = []
        for j, (px, py) in enumerate(chips):
            sp = 2 * px + py
            for i in range(n):
                here = out[i].at[sp, rows[i]]
                _rcopy(w[i].at[rows[i]], here, ssem.at[6 * i + j], rsem.at[6 * i + j], (px, py, c)).wait_recv()
                fw = _rcopy(here, here, ssem.at[6 * i + 3 + j], rsem.at[6 * i + 3 + j], (x, y, 1 - c))
                fw.start()
                passed.append(fw)
        for j, (px, py) in enumerate(chips):
            sp = 2 * px + py
            for i in range(n):
                _rcopy(w[i].at[orows[i]], out[i].at[sp, orows[i]], ssem.at[6 * i + 3 + j], rsem.at[6 * i + 3 + j],
                       (x, y, 1 - c)).wait_recv()
        for cp in first + passed:
            cp.wait_send()
        for cp in local:
            cp.wait()

    return _pcall(
        body, name="ag_weights", in_specs=[_HBM] * n, out_specs=[_HBM] * n,
        out_shape=[jax.ShapeDtypeStruct((4,) + w.shape, w.dtype) for w in shards],
        scratch_shapes=[pltpu.SemaphoreType.DMA((6 * n,)), pltpu.SemaphoreType.DMA((6 * n,)), pltpu.SemaphoreType.DMA((n,))],
    )(*shards)


def swap_halves(gs):
    n = len(gs)

    def body(*refs):
        g, out = refs[:n], refs[n:2 * n]
        ssem, rsem = refs[2 * n:]
        x, y, c, _ = _place()
        cps = [_rcopy(g[i].at[:, _half_rows(1 - c, g[i].shape[1])], out[i], ssem.at[i], rsem.at[i], (x, y, 1 - c))
               for i in range(n)]
        for cp in cps:
            cp.start()
        for cp in cps:
            cp.wait()

    return _pcall(
        body, name="swap_halves", in_specs=[_HBM] * n, out_specs=[_HBM] * n,
        out_shape=[jax.ShapeDtypeStruct((4, g.shape[1] // 2, g.shape[2]), g.dtype) for g in gs],
        scratch_shapes=[pltpu.SemaphoreType.DMA((n,)), pltpu.SemaphoreType.DMA((n,))],
    )(*gs)


def rs_chips(hs):
    n = len(hs)

    def body(*refs):
        h, out = refs[:n], refs[n:2 * n]
        ssem, rsem, lsem = refs[2 * n:]
        x, y, c, chips = _place()
        s = 2 * x + y
        local = [pltpu.make_async_copy(h[i].at[s], out[i].at[s], lsem.at[i]) for i in range(n)]
        for cp in local:
            cp.start()
        cps = [_rcopy(h[i].at[2 * px + py], out[i].at[s], ssem.at[3 * i + j], rsem.at[3 * i + j], (px, py, c))
               for i in range(n) for j, (px, py) in enumerate(chips)]
        for cp in cps:
            cp.start()
        for j, (px, py) in enumerate(chips):
            for i in range(n):
                _rcopy(h[i].at[s], out[i].at[2 * px + py], ssem.at[3 * i + j], rsem.at[3 * i + j], (px, py, c)).wait_recv()
        for cp in cps:
            cp.wait_send()
        for cp in local:
            cp.wait()

    return _pcall(
        body, name="rs_chips", in_specs=[_HBM] * n, out_specs=[_HBM] * n,
        out_shape=[jax.ShapeDtypeStruct(h.shape, h.dtype) for h in hs],
        scratch_shapes=[pltpu.SemaphoreType.DMA((3 * n,)), pltpu.SemaphoreType.DMA((3 * n,)), pltpu.SemaphoreType.DMA((n,))],
    )(*hs)


def share_sibling(ts):
    n = len(ts)

    def body(*refs):
        t, out = refs[:n], refs[n:2 * n]
        ssem, rsem, lsem = refs[2 * n:]
        x, y, c, _ = _place()
        local = [pltpu.make_async_copy(t[i], out[i].at[c], lsem.at[i]) for i in range(n)]
        for cp in local:
            cp.start()
        cps = [_rcopy(t[i], out[i].at[c], ssem.at[i], rsem.at[i], (x, y, 1 - c)) for i in range(n)]
        for cp in cps:
            cp.start()
        for i in range(n):
            _rcopy(t[i], out[i].at[1 - c], ssem.at[i], rsem.at[i], (x, y, 1 - c)).wait_recv()
        for cp in cps:
            cp.wait_send()
        for cp in local:
            cp.wait()

    return _pcall(
        body, name="share_sibling", in_specs=[_HBM] * n, out_specs=[_HBM] * n,
        out_shape=[jax.ShapeDtypeStruct((2,) + t.shape, t.dtype) for t in ts],
        scratch_shapes=[pltpu.SemaphoreType.DMA((n,)), pltpu.SemaphoreType.DMA((n,)), pltpu.SemaphoreType.DMA((n,))],
    )(*ts)


def allreduce8(v, name):
    P, W = v.shape

    def body(v_ref, out_ref, buf, ssem, rsem):
        x, y, c, _ = _place()
        me = 4 * x + 2 * y + c
        buf[me] = v_ref[...]
        cps = []
        for k in range(1, 8):
            px = 1 - x if k & 4 else x
            py = 1 - y if k & 2 else y
            pc = 1 - c if k & 1 else c
            cp = _rcopy(buf.at[me], buf.at[me], ssem.at[k - 1], rsem.at[k - 1], (px, py, pc))
            cp.start()
            cps.append((cp, 4 * px + 2 * py + pc))
        for k, (cp, peer) in enumerate(cps):
            _rcopy(buf.at[me], buf.at[peer], ssem.at[k], rsem.at[k], (x, y, c)).wait_recv()
        for cp, _ in cps:
            cp.wait_send()
        acc = buf[0]
        for d in range(1, 8):
            acc = acc + buf[d]
        out_ref[...] = acc

    return _pcall(
        body, name=name, in_specs=[pl.BlockSpec(memory_space=pltpu.VMEM)],
        out_specs=pl.BlockSpec(memory_space=pltpu.VMEM), out_shape=jax.ShapeDtypeStruct((P, W), F32),
        scratch_shapes=[pltpu.VMEM((8, P, W), F32), pltpu.SemaphoreType.DMA((7,)), pltpu.SemaphoreType.DMA((7,))],
    )(v)


def add_half(g, rcv, cidx, name):
    _, hr, W = rcv.shape
    tr = _rows_tile(hr, W)
    nb = hr // tr

    def body(c_ref, g_ref, r_ref, o_ref):
        o_ref[...] = (g_ref[...] + r_ref[...]).astype(o_ref.dtype)

    return _pcall(
        body, name=name,
        grid_spec=pltpu.PrefetchScalarGridSpec(
            num_scalar_prefetch=1, grid=(4, nb),
            in_specs=[pl.BlockSpec((1, tr, W), lambda s, i, c: (s, c[0] * nb + i, 0)),
                      pl.BlockSpec((1, tr, W), lambda s, i, c: (s, i, 0))],
            out_specs=pl.BlockSpec((1, tr, W), lambda s, i, c: (s, i, 0))),
        out_shape=jax.ShapeDtypeStruct((4, hr, W), BF16), compiler_params=_cparams(("parallel", "parallel")),
    )(cidx, g, rcv)


def sum_chips(p, name):
    _, hr, W = p.shape
    tr = _rows_tile(hr, W)

    def body(p_ref, o_ref):
        f = lambda s: p_ref[s].astype(F32)
        o_ref[...] = ((f(0) + f(1)) + f(2)) + f(3)

    return _pcall(
        body, name=name, grid=(hr // tr,), in_specs=[pl.BlockSpec((4, tr, W), lambda i: (0, i, 0))],
        out_specs=pl.BlockSpec((tr, W), lambda i: (i, 0)), out_shape=jax.ShapeDtypeStruct((hr, W), F32),
        compiler_params=_cparams(("parallel",)),
    )(p)


HI = lax.Precision.HIGHEST
BLOCK_BYTES = 3 * 512 * 1024


def _rows_tile(R, w, T=None):
    cands = [t for t in range(16, R + 1, 16) if R % t == 0 and (T is None or T % t == 0)]
    ok = [t for t in cands if t * w * 4 <= BLOCK_BYTES]
    return max(ok) if ok else min(cands)


def _gsum(x):
    r = jnp.right_shift(lax.broadcasted_iota(jnp.int32, (LANE, LANE), 0), 6)
    c = jnp.right_shift(lax.broadcasted_iota(jnp.int32, (LANE, LANE), 1), 6)
    return jnp.dot(x, (r == c).astype(F32), precision=HI, preferred_element_type=F32)


def _rms(x, g):
    return x * lax.rsqrt(jnp.mean(x * x, axis=-1, keepdims=True) + NORM_EPS) * g


def _silu_mul(gate, up):
    return jax.nn.silu(gate) * up


def _shift(p, prev, mu):
    return p + mu * (prev - p)


def _lora_act(p, prev, mu):
    s = _shift(p, prev, mu)
    return jax.nn.sigmoid(s[:, :G_LORA]), jnp.tanh(s[:, G_LORA:G_LORA + LANE]), s[:, G_LORA + LANE:]


def _prep(k, lw, la, w0, a0, kk_w, ka_w):
    wpre = -jax.nn.softplus(-(w0 + lw)) - 0.5
    decay = jnp.exp(-jnp.exp(wpre))
    a = jax.nn.sigmoid(a0 + la)
    kk = k * kk_w
    kk = kk * lax.rsqrt(jnp.maximum(_gsum(kk * kk), 1e-24))
    k2 = k * (1.0 + (a - 1.0) * ka_w)
    return decay, -kk, kk * a, k2


def _post(y, r, k2, v, g, gnw, gnb, rk):
    mean = _gsum(y) * (1.0 / RWKV_HEAD)
    d = y - mean
    var = _gsum(d * d) * (1.0 / RWKV_HEAD)
    yn = d * lax.rsqrt(var + GN_EPS) * gnw + gnb
    bonus = _gsum(r * k2 * rk) * v
    return (yn + bonus) * g


def _gate_mix(ga, gb, ya, o):
    return jax.nn.sigmoid(ga) * ya + jax.nn.sigmoid(gb) * o


def _rope(x, cos, sin):
    return x * cos + pltpu.roll(x, LANE // 2, 1) * sin


def _rope_t(dy, cos, sin):
    return dy * cos + pltpu.roll(dy * sin, LANE // 2, 1)


def _mla_pre(cq, ckv, kpe, cos, sin, qw, kvw):
    return _rms(cq, qw), _rms(ckv, kvw), _rope(kpe, cos, sin)


def _mla_pre_bwd(cq, ckv, dcqn, dckvn, dkr, cos, sin, qw, kvw):
    _, pull = jax.vjp(lambda a, b, c, d: (_rms(a, c), _rms(b, d)), cq, ckv, qw, kvw)
    dcq, dckv, dqw, dkvw = pull((dcqn, dckvn))
    return dcq, dckv, _rope_t(dkr, cos, sin), dqw, dkvw


WEIGHTS = ['meta_tokens', 'ffn1_norm', 'ffn1_w_gate', 'ffn1_w_up', 'ffn1_w_down', 'mix_norm', 'w_in', 'tm_mu', 'w0',
           'w_up', 'a0', 'a_up', 'g_up', 'k_k', 'k_a', 'r_k', 'gn_w', 'gn_b', 'q_norm', 'w_uq', 'kv_norm', 'w_ukv',
           'w_out', 'ffn2_norm', 'ffn2_w_gate', 'ffn2_w_up', 'ffn2_w_down', 'final_norm']
SHARD_AXIS = {'meta_tokens': 1, 'ffn1_w_gate': 1, 'ffn1_w_up': 1, 'ffn1_w_down': 0, 'w_in': 1, 'w_up': 1, 'a_up': 1,
              'g_up': 1, 'w_uq': 1, 'w_ukv': 1, 'w_out': 0, 'ffn2_w_gate': 1, 'ffn2_w_up': 1, 'ffn2_w_down': 0}
GATHERED = [n for n in WEIGHTS if n in SHARD_AXIS and n != 'meta_tokens']
FFN_IN = ('ffn1_w_gate', 'ffn1_w_up', 'ffn2_w_gate', 'ffn2_w_up')
SMALL = [n for n in WEIGHTS if n not in SHARD_AXIS]


def _to2d(a):
    if a.ndim == 1:
        return a.reshape(1, -1)
    if a.ndim == 3:
        return a.reshape(a.shape[0] * a.shape[1], a.shape[2]) if a.shape[0] == 1 and a.shape[1] > 64 else a.reshape(1, -1)
    return a


def _unpack(flat, shapes):
    out, off = [], 0
    for shp in shapes:
        n = shp[0] * shp[1]
        out.append(flat[off:off + n].reshape(shp))
        off += n
    return out


def _adamw(w, g, m, v):
    m = ADAM_B1 * m + (1.0 - ADAM_B1) * g
    v = ADAM_B2 * v + (1.0 - ADAM_B2) * jnp.square(g)
    m_hat = m / (1.0 - ADAM_B1 ** ADAM_STEP)
    v_hat = v / (1.0 - ADAM_B2 ** ADAM_STEP)
    delta = -ADAM_LR * (m_hat / (jnp.sqrt(v_hat) + ADAM_EPS) + ADAM_WD * w)
    return delta, m, v


def adamw(w, g, m, v, name):
    R, C = w.shape
    if R % 16 == 0 and R > 16:
        tm = _rows_tile(R, C)
    else:
        tm = R
    return rowwise(_adamw, [(w, C, 0), (g, C, 0), (m, C, 0), (v, C, 0)], [], [(C, C, F32)] * 3, tm=tm, name=name)


def _step(a):
    x = a['x']
    Bl, S, D = x.shape
    T = S + N_META
    R, RS = Bl * T, Bl * S
    Hr, Hm = D // RWKV_HEAD, D // LANE
    w2 = {n: _to2d(a[n]) for n in WEIGHTS}
    m2 = {n: _to2d(a['m_' + n]) for n in WEIGHTS}
    v2 = {n: _to2d(a['v_' + n]) for n in WEIGHTS}
    xi, yi, ci = lax.axis_index("x"), lax.axis_index("y"), lax.axis_index("c")
    chip = 2 * xi + yi

    stk = dict(zip(GATHERED, ag_weights([w2[n].astype(BF16) for n in GATHERED])))

    def unstack(z, axis):
        return z.reshape(4 * z.shape[1], z.shape[2]) if axis == 0 else jnp.concatenate([z[s] for s in range(4)], axis=1)

    full = {n: unstack(stk[n], SHARD_AXIS[n]) for n in GATHERED if n not in FFN_IN}
    mt = w2['meta_tokens']
    mcols = mt.shape[1]
    mt_z = lax.dynamic_update_slice(jnp.zeros((N_META, D), F32), 0.5 * mt, (jnp.zeros((), jnp.int32), (chip * mcols).astype(jnp.int32)))
    meta_full = allreduce8(mt_z.reshape(-1, LANE), "gather_meta").reshape(N_META, D)

    F = 4 * stk['ffn1_w_gate'].shape[2]
    win = full['w_in']
    o = 3 * D
    c_xw, c_xa, c_xg = win[:, o:o + W_LORA], win[:, o + W_LORA:o + 2 * W_LORA], win[:, o + 2 * W_LORA:o + 2 * W_LORA + G_LORA]
    o += 2 * W_LORA + G_LORA
    c_cq, c_ckv, c_kpe = win[:, o:o + Q_LORA], win[:, o + Q_LORA:o + Q_LORA + KV_LORA], win[:, o + Q_LORA + KV_LORA:o + Q_LORA + KV_LORA + ROPE_DIM]
    o += Q_LORA + KV_LORA + ROPE_DIM
    c_ga, c_gb = win[:, o:o + D], win[:, o + D:o + 2 * D]
    zc = lambda n: jnp.zeros((D, n), BF16)
    half = ROPE_DIM // 2
    NP0 = 5 * D + 512 + Q_LORA + KV_LORA + LANE
    NP = -(-NP0 // 512) * 512
    win_p = jnp.concatenate([win[:, :3 * D], c_ga, c_gb, c_xg, c_xw, zc(LANE - W_LORA), c_xa, zc(LANE - A_LORA), c_cq, c_ckv,
                             c_kpe[:, :half], zc(half), c_kpe[:, half:], zc(half), zc(NP - NP0)], axis=1)
    O_GA, O_GB, O_L, O_CQ, O_CKV, O_KPE = 3 * D, 4 * D, 5 * D, 5 * D + 512, 5 * D + 512 + Q_LORA, 5 * D + 512 + Q_LORA + KV_LORA
    zr = lambda n: jnp.zeros((n, D), BF16)
    w_up_p = jnp.concatenate([full['w_up'], zr(LANE - W_LORA)], axis=0)
    a_up_p = jnp.concatenate([full['a_up'], zr(LANE - A_LORA)], axis=0)
    g_up = full['g_up']
    wuq = full['w_uq'].reshape(Q_LORA, Hm, QK_DIM)
    zq = jnp.zeros((Q_LORA, Hm, half), BF16)
    wqn = wuq[:, :, :NOPE_DIM].reshape(Q_LORA, Hm * LANE)
    wqp = jnp.concatenate([wuq[:, :, NOPE_DIM:NOPE_DIM + half], zq, wuq[:, :, NOPE_DIM + half:], zq], axis=2).reshape(Q_LORA, Hm * LANE)
    wukv = full['w_ukv'].reshape(KV_LORA, Hm, NOPE_DIM + V_DIM)
    wkn = wukv[:, :, :NOPE_DIM].reshape(KV_LORA, Hm * LANE)
    wv = wukv[:, :, NOPE_DIM:].reshape(KV_LORA, Hm * LANE)
    wout = full['w_out']
    tmu = w2['tm_mu']
    mu_a = tmu[:, :3 * D]
    zm = lambda n: jnp.zeros((1, n), F32)
    mu_b = jnp.concatenate([tmu[:, 3 * D + 2 * W_LORA:], tmu[:, 3 * D:3 * D + W_LORA], zm(LANE - W_LORA),
                            tmu[:, 3 * D + W_LORA:3 * D + 2 * W_LORA], zm(LANE - A_LORA)], axis=1)
    pos = jnp.arange(T, dtype=F32)
    inv_freq = 1.0 / (ROPE_THETA ** (jnp.arange(0, ROPE_DIM, 2, dtype=F32) / ROPE_DIM))
    ang = pos[:, None] * inv_freq[None, :]
    zt = jnp.zeros((T, half), F32)
    cos_t = jnp.concatenate([jnp.cos(ang), zt, jnp.cos(ang), zt], axis=1)
    sin_t = jnp.concatenate([-jnp.sin(ang), zt, jnp.sin(ang), zt], axis=1)
    cos_q, sin_q = cos_t[N_META:], sin_t[N_META:]

    t_full = _rows_tile(R, D)
    t_512 = _rows_tile(R, 512, T)
    t_128 = _rows_tile(R, LANE, T)
    tq_full = _rows_tile(RS, D)
    tq_128 = _rows_tile(RS, LANE, S)

    def rows3(z):
        return z.reshape(Bl, T, z.shape[-1])

    def real_rows(z):
        return rows3(z)[:, N_META:].reshape(RS, z.shape[-1])

    def pad_meta(z):
        z3 = z.reshape(Bl, S, z.shape[-1])
        return jnp.concatenate([jnp.zeros((Bl, N_META, z.shape[-1]), z.dtype), z3], axis=1).reshape(R, z.shape[-1])

    def shift_down(z):
        z3 = rows3(z)
        return jnp.concatenate([jnp.zeros((Bl, 1, z.shape[-1]), z.dtype), z3[:, :-1]], axis=1).reshape(R, z.shape[-1])

    def shift_up(z):
        z3 = rows3(z)
        return jnp.concatenate([z3[:, 1:], jnp.zeros((Bl, 1, z.shape[-1]), z.dtype)], axis=1).reshape(R, z.shape[-1])

    def ffn_fwd(h, nw, wg, wu, wd, tag):
        n = rowwise(_rms, [(h, D, 0)], [(nw, D, 0)], [(D, D, BF16)], tm=t_full, name=tag + "_norm")[0]
        gate = mm(n, wg, b_stack=True, name=tag + "_gate")
        up = mm(n, wu, b_stack=True, name=tag + "_up")
        act = rowwise(_silu_mul, [(gate, 512, 0), (up, 512, 0)], [], [(F, 512, BF16)], tm=t_512, ncb=F // 512,
                      name=tag + "_act")[0]
        return mm(act, wd, res=h, alpha=0.5, name=tag + "_down"), (h, n, gate, up, act)

    def ffn_bwd(dh2, saved, nw, wg, wu, wd, tag):
        h, n, gate, up, act = saved
        dz = rowwise(lambda d: 0.5 * d, [(dh2, 512, 0)], [], [(D, 512, BF16)], tm=t_512, ncb=D // 512, name=tag + "_dz")[0]
        d_wd = mm(act, dz, "tn", name=tag + "_dwd")
        dact = mm(dz, wd, "nt", name=tag + "_dact")
        dgate, dup = rowwise(vjp_fn(_silu_mul, 2), [(gate, 512, 0), (up, 512, 0), (dact, 512, 0)], [],
                             [(F, 512, BF16)] * 2, tm=t_512, ncb=F // 512, name=tag + "_dact2")
        d_wg = mm(n, dgate, "tn", out_stack=True, name=tag + "_dwg")
        d_wu = mm(n, dup, "tn", out_stack=True, name=tag + "_dwu")
        dn = mm(dgate, wg, "nt", b_stack=True, name=tag + "_dn1")
        dn = mm(dup, wu, "nt", res=dn, b_stack=True, name=tag + "_dn2")

        def f(h_, dn_, dh_, nw_):
            dh, dnw = vjp_fn(_rms, 2)(h_, nw_, dn_)
            return dh + dh_, dnw

        dh, d_nw = rowwise(f, [(h, D, 0), (dn, D, 0), (dh2, D, 0)], [(nw, D, 0)], [(D, D, F32)], [(1, D, D)],
                           tm=t_full, name=tag + "_dnorm")
        return dh, d_nw, d_wg, d_wu, d_wd

    h0 = jnp.concatenate([jnp.broadcast_to(meta_full[None], (Bl, N_META, D)), x], axis=1).reshape(R, D)
    h1, sv1 = ffn_fwd(h0, w2['ffn1_norm'], stk['ffn1_w_gate'], stk['ffn1_w_up'], full['ffn1_w_down'], "ffn1")
    u = rowwise(_rms, [(h1, D, 0)], [(w2['mix_norm'], D, 0)], [(D, D, BF16)], tm=t_full, name="mix_norm")[0]
    proj = mm(u, win_p, name="proj")
    prev_a = shift_down(proj[:, :3 * D])
    prev_b = shift_down(proj[:, O_L:O_L + 512])
    ps = rowwise(_shift, [(proj, 512, 0), (prev_a, 512, 0)], [(mu_a, 512, 0)], [(3 * D, 512, F32)], tm=t_512,
                 ncb=3 * D // 512, name="shift_rkv")[0]
    sg, txw, xas = rowwise(_lora_act, [(proj, 512, O_L // 512), (prev_b, 512, 0)], [(mu_b, 512, 0)],
                           [(G_LORA, G_LORA, BF16), (LANE, LANE, BF16), (LANE, LANE, BF16)], tm=t_512, name="lora_act")
    lw = mm(txw, w_up_p, name="lora_w")
    la = mm(xas, a_up_p, name="lora_a")
    g = mm(sg, g_up, name="lora_g")
    hb = D // LANE
    par_d = lambda n: (w2[n], LANE, 0)
    decay, kn, bb, k2 = rowwise(_prep, [(ps, LANE, hb), (lw, LANE, 0), (la, LANE, 0)],
                                [par_d('w0'), par_d('a0'), par_d('k_k'), par_d('k_a')], [(D, LANE, F32)] * 4,
                                tm=t_128, ncb=hb, name="wkv_prep")

    def to_j(z):
        z = z.reshape(Bl, T, Hr, RWKV_HEAD).transpose(1, 3, 0, 2).reshape(T, RWKV_HEAD, Bl * Hr)
        return jnp.concatenate([z, z], axis=-1)

    def to_i(z):
        return z.reshape(Bl, T, Hr, 2, RWKV_HEAD // 2).transpose(1, 4, 3, 0, 2).reshape(T, RWKV_HEAD // 2, 2 * Bl * Hr)

    def from_i(z):
        return z.reshape(T, RWKV_HEAD // 2, 2, Bl, Hr).transpose(3, 0, 4, 2, 1).reshape(R, D)

    def from_j(z):
        return z[:, :, :Bl * Hr].reshape(T, RWKV_HEAD, Bl, Hr).transpose(2, 0, 3, 1).reshape(R, D)

    r_s, v_s = ps[:, :D], ps[:, 2 * D:]
    jw, jkn, jb, jk, jr, iv = to_j(decay), to_j(kn), to_j(bb), to_j(k2), to_j(r_s), to_i(v_s)
    y_i, sp = wkv_fwd(jw, jkn, jb, jk, jr, iv)
    y = from_i(y_i)
    post_rows = [(y, LANE, 0), (ps, LANE, 0), (k2, LANE, 0), (ps, LANE, 2 * hb), (g, LANE, 0)]
    post_pars = [par_d('gn_w'), par_d('gn_b'), par_d('r_k')]
    ya = rowwise(_post, post_rows, post_pars, [(D, LANE, F32)], tm=t_128, ncb=hb, name="wkv_post")[0]

    nt512 = T // t_512
    mla_rows = [(proj, Q_LORA, O_CQ // Q_LORA), (proj, KV_LORA, O_CKV // KV_LORA), (proj, LANE, O_KPE // LANE)]
    tabs = [(cos_t, LANE, 0, nt512, True), (sin_t, LANE, 0, nt512, True)]
    mla_pars = [(w2['q_norm'], Q_LORA, 0), (w2['kv_norm'], KV_LORA, 0)]
    cqn, ckvn, kpr = rowwise(_mla_pre, mla_rows + tabs, mla_pars,
                             [(Q_LORA, Q_LORA, BF16), (KV_LORA, KV_LORA, BF16), (LANE, LANE, BF16)], tm=t_512, name="mla_pre")
    cqn_r = real_rows(cqn)
    qn = mm(cqn_r, wqn, out_dtype=BF16, name="q_nope")
    qp_raw = mm(cqn_r, wqp, name="q_pe")
    ntq = S // tq_128
    qtabs = [(cos_q, LANE, 0, ntq, True), (sin_q, LANE, 0, ntq, True)]
    qp = rowwise(_rope, [(qp_raw, LANE, 0)] + qtabs, [], [(D, LANE, BF16)], tm=tq_128, ncb=Hm, name="q_rope")[0]
    knope = mm(ckvn, wkn, out_dtype=BF16, name="k_nope")
    vv = mm(ckvn, wv, out_dtype=BF16, name="v_proj")

    def pad_keys(z):
        z3 = rows3(z)
        return jnp.concatenate([z3[:, :N_META], jnp.zeros((Bl, Q_BLOCK - N_META, z.shape[-1]), z.dtype), z3[:, N_META:]], axis=1)

    def unpad_keys(z):
        return jnp.concatenate([z[:, :N_META], z[:, Q_BLOCK:]], axis=1).reshape(R, z.shape[-1])

    qn3, qp3 = qn.reshape(Bl, S, D), qp.reshape(Bl, S, D)
    knp, kpp, vp = pad_keys(knope), pad_keys(kpr), pad_keys(vv)
    o_att = pad_meta(attn_fwd(qn3, qp3, knp, kpp, vp).reshape(RS, D))
    mix_rows = [(proj, 512, O_GA // 512), (proj, 512, O_GB // 512), (ya, 512, 0), (o_att, 512, 0)]
    mix = rowwise(_gate_mix, mix_rows, [], [(D, 512, BF16)], tm=t_512, ncb=D // 512, name="gate_mix")[0]
    h2 = mm(mix, wout, res=h1, name="w_out")
    h3, sv2 = ffn_fwd(h2, w2['ffn2_norm'], stk['ffn2_w_gate'], stk['ffn2_w_up'], full['ffn2_w_down'], "ffn2")

    def loss_fb(h_, tgt, fw):
        yv, pull = jax.vjp(_rms, h_, fw)
        e = yv - tgt
        dh, dfw = pull(e * (1.0 / D))
        return dh, jnp.full((1, LANE), 0.5 / D * jnp.sum(e * e), F32), dfw

    dh3r, lossp, g_final = rowwise(loss_fb, [(real_rows(h3), D, 0), (a['loss_target'].reshape(RS, D), D, 0)],
                                   [(w2['final_norm'], D, 0)], [(D, D, F32)], [(1, LANE, LANE), (1, D, D)],
                                   tm=tq_full, name="loss")
    dh3 = pad_meta(dh3r)

    gr = {'final_norm': g_final}
    dh2, gr['ffn2_norm'], gr['ffn2_w_gate'], gr['ffn2_w_up'], gr['ffn2_w_down'] = ffn_bwd(
        dh3, sv2, w2['ffn2_norm'], stk['ffn2_w_gate'], stk['ffn2_w_up'], full['ffn2_w_down'], "ffn2")
    dh2b = rowwise(lambda d: d, [(dh2, 512, 0)], [], [(D, 512, BF16)], tm=t_512, ncb=D // 512, name="dh2_cast")[0]
    gr['w_out'] = mm(mix, dh2b, "tn", name="d_wout")
    dmix = mm(dh2b, wout, "nt", name="d_mix")
    dga, dgb, dya, do = rowwise(vjp_fn(_gate_mix, 4), mix_rows + [(dmix, 512, 0)], [],
                                [(D, 512, BF16), (D, 512, BF16), (D, 512, F32), (D, 512, BF16)], tm=t_512, ncb=D // 512,
                                name="d_gate_mix")
    dqn, dqp, dknp, dvp, dkpp = attn_bwd(qn3, qp3, knp, kpp, vp, real_rows(do).reshape(Bl, S, D))
    dqn2 = dqn.reshape(RS, D)
    dqp_raw = rowwise(_rope_t, [(dqp.reshape(RS, D), LANE, 0)] + qtabs, [], [(D, LANE, BF16)], tm=tq_128, ncb=Hm,
                      name="d_q_rope")[0]
    d_wqn = mm(cqn_r, dqn2, "tn", name="d_wqn")
    d_wqp = mm(cqn_r, dqp_raw, "tn", name="d_wqp")
    dcqn = mm(dqn2, wqn, "nt", name="d_cqn1")
    dcqn = pad_meta(mm(dqp_raw, wqp, "nt", res=dcqn, name="d_cqn2"))
    dkn2, dv2, dkp2 = unpad_keys(dknp), unpad_keys(dvp), unpad_keys(dkpp)
    d_wkn = mm(ckvn, dkn2, "tn", name="d_wkn")
    d_wv = mm(ckvn, dv2, "tn", name="d_wv")
    dckvn = mm(dkn2, wkn, "nt", name="d_ckvn1")
    dckvn = mm(dv2, wv, "nt", res=dckvn, name="d_ckvn2")
    dcq, dckv, dkpe, gr['q_norm'], gr['kv_norm'] = rowwise(
        _mla_pre_bwd, mla_rows[:2] + [(dcqn, Q_LORA, 0), (dckvn, KV_LORA, 0), (dkp2, LANE, 0)] + tabs, mla_pars,
        [(Q_LORA, Q_LORA, BF16), (KV_LORA, KV_LORA, BF16), (LANE, LANE, BF16)], [(1, Q_LORA, Q_LORA), (1, KV_LORA, KV_LORA)],
        tm=t_512, name="d_mla_pre")

    def post_bwd(y_, r_, k2_, v_, g_, dya_, gnw, gnb, rk):
        return vjp_fn(_post, 8)(y_, r_, k2_, v_, g_, gnw, gnb, rk, dya_)

    dy, dr_b, dk2_b, dv_b, dg, gr['gn_w'], gr['gn_b'], gr['r_k'] = rowwise(
        post_bwd, post_rows + [(dya, LANE, 0)], post_pars,
        [(D, LANE, F32)] * 4 + [(D, LANE, BF16)], [(1, D, LANE)] * 3, tm=t_128, ncb=hb, name="d_wkv_post")
    jdw, jdkn, jdb, jdk, jdr, idv = wkv_bwd(jw, jkn, jb, jk, jr, iv, to_i(dy), sp)
    ddecay, dkn_w, db_w, dk2_w, dr_w, dv_w = from_j(jdw), from_j(jdkn), from_j(jdb), from_j(jdk), from_j(jdr), from_i(idv)

    def prep_bwd(k_, lw_, la_, dd, dkn_, db_, dk2a, dk2b, w0, a0, kkw, kaw):
        return vjp_fn(_prep, 7)(k_, lw_, la_, w0, a0, kkw, kaw, dd, dkn_, db_, dk2a + dk2b)

    dk_s, dlw, dla, gr['w0'], gr['a0'], gr['k_k'], gr['k_a'] = rowwise(
        prep_bwd, [(ps, LANE, hb), (lw, LANE, 0), (la, LANE, 0), (ddecay, LANE, 0), (dkn_w, LANE, 0), (db_w, LANE, 0),
                   (dk2_w, LANE, 0), (dk2_b, LANE, 0)],
        [par_d('w0'), par_d('a0'), par_d('k_k'), par_d('k_a')], [(D, LANE, F32), (D, LANE, BF16), (D, LANE, BF16)],
        [(1, D, LANE)] * 4, tm=t_128, ncb=hb, name="d_wkv_prep")
    d_wup = mm(txw, dlw, "tn", name="d_wup")
    dtxw = mm(dlw, w_up_p, "nt", name="d_txw")
    d_aup = mm(xas, dla, "tn", name="d_aup")
    dxa = mm(dla, a_up_p, "nt", name="d_xa")
    gr['g_up'] = mm(sg, dg, "tn", name="d_gup")
    dsg = mm(dg, g_up, "nt", name="d_sg")

    def lora_bwd(p_, prev_, dsg_, dt_, dxa_, mu_):
        return vjp_fn(_lora_act, 3)(p_, prev_, mu_, dsg_, dt_, dxa_)

    dpb, dprevb, dmu_b = rowwise(lora_bwd, [(proj, 512, O_L // 512), (prev_b, 512, 0), (dsg, G_LORA, 0), (dtxw, LANE, 0),
                                            (dxa, LANE, 0)], [(mu_b, 512, 0)], [(512, 512, F32)] * 2, [(1, 512, 512)],
                                 tm=t_512, name="d_lora_act")

    def shift_bwd2(p_, prev_, c1, c2, mu_):
        return vjp_fn(_shift, 3)(p_, prev_, mu_, c1 + c2)

    def shift_bwd1(p_, prev_, c1, mu_):
        return vjp_fn(_shift, 3)(p_, prev_, mu_, c1)

    def shift_back(sec, cts):
        nb = D // 512
        f = shift_bwd2 if len(cts) == 2 else shift_bwd1
        return rowwise(f, [(proj, 512, sec * nb), (prev_a, 512, sec * nb)] + [(c, 512, 0) for c in cts],
                       [(mu_a, 512, sec * nb)], [(D, 512, F32)] * 2, [(1, D, 512)], tm=t_512, ncb=nb, name=f"d_shift{sec}")

    dp_r, dprev_r, dmu_r = shift_back(0, [dr_w, dr_b])
    dp_k, dprev_k, dmu_k = shift_back(1, [dk_s])
    dp_v, dprev_v, dmu_v = shift_back(2, [dv_w, dv_b])

    def add_cast(p_, q_):
        return p_ + q_

    def dsec(dp_, dprev_, tag):
        C = dp_.shape[1]
        return rowwise(add_cast, [(dp_, 512, 0), (shift_up(dprev_), 512, 0)], [], [(C, 512, BF16)], tm=t_512, ncb=C // 512,
                       name="d_sec_" + tag)[0]

    zpad = jnp.zeros((R, NP - NP0), BF16)
    dproj = jnp.concatenate([dsec(dp_r, dprev_r, "r"), dsec(dp_k, dprev_k, "k"), dsec(dp_v, dprev_v, "v"), dga, dgb,
                             dsec(dpb, dprevb, "l"), dcq, dckv, dkpe, zpad], axis=1)
    d_win_p = mm(u, dproj, "tn", name="d_win")
    du = mm(dproj, win_p, "nt", name="d_u")

    def norm_bwd(h_, dn_, dh_, nw_):
        dh, dnw = vjp_fn(_rms, 2)(h_, nw_, dn_)
        return dh + dh_, dnw

    dh1, gr['mix_norm'] = rowwise(norm_bwd, [(h1, D, 0), (du, D, 0), (dh2, D, 0)], [(w2['mix_norm'], D, 0)],
                                  [(D, D, F32)], [(1, D, D)], tm=t_full, name="d_mix_norm")
    dh0, gr['ffn1_norm'], gr['ffn1_w_gate'], gr['ffn1_w_up'], gr['ffn1_w_down'] = ffn_bwd(
        dh1, sv1, w2['ffn1_norm'], stk['ffn1_w_gate'], stk['ffn1_w_up'], full['ffn1_w_down'], "ffn1")
    dh0_3 = rows3(dh0)
    grad_x = dh0_3[:, N_META:]
    gr['meta_tokens'] = jnp.sum(dh0_3[:, :N_META], axis=0)

    gr['w_in'] = jnp.concatenate([
        d_win_p[:, :3 * D], d_win_p[:, O_L + G_LORA:O_L + G_LORA + W_LORA], d_win_p[:, O_L + G_LORA + LANE:O_L + G_LORA + LANE + A_LORA],
        d_win_p[:, O_L:O_L + G_LORA], d_win_p[:, O_CQ:O_CQ + Q_LORA], d_win_p[:, O_CKV:O_CKV + KV_LORA],
        d_win_p[:, O_KPE:O_KPE + half], d_win_p[:, O_KPE + 2 * half:O_KPE + 3 * half], d_win_p[:, O_GA:O_GA + 2 * D]], axis=1)
    gr['tm_mu'] = jnp.concatenate([dmu_r, dmu_k, dmu_v, dmu_b[:, G_LORA:G_LORA + W_LORA],
                                   dmu_b[:, G_LORA + LANE:G_LORA + LANE + A_LORA], dmu_b[:, :G_LORA]], axis=1)
    gr['w_up'], gr['a_up'] = d_wup[:W_LORA], d_aup[:A_LORA]
    dq3n, dq3p = d_wqn.reshape(Q_LORA, Hm, LANE), d_wqp.reshape(Q_LORA, Hm, LANE)
    gr['w_uq'] = jnp.concatenate([dq3n, dq3p[:, :, :half], dq3p[:, :, 2 * half:3 * half]], axis=2).reshape(Q_LORA, Hm * QK_DIM)
    gr['w_ukv'] = jnp.concatenate([d_wkn.reshape(KV_LORA, Hm, LANE), d_wv.reshape(KV_LORA, Hm, LANE)], axis=2).reshape(
        KV_LORA, Hm * (NOPE_DIM + V_DIM))

    def stacked(g, axis):
        if g.ndim == 3:
            return g
        if axis == 0:
            return g.reshape(4, g.shape[0] // 4, g.shape[1])
        return g.reshape(g.shape[0], 4, g.shape[1] // 4).transpose(1, 0, 2)

    cidx = ci.reshape(1).astype(jnp.int32)
    gs = [stacked(gr[n], SHARD_AXIS[n]) for n in GATHERED]
    got = swap_halves(gs)
    hs = [add_half(g, r, cidx, "add_half_" + n) for n, g, r in zip(GATHERED, gs, got)]
    ts = [sum_chips(p, "sum_chips_" + n) for n, p in zip(GATHERED, rs_chips(hs))]
    g_shard = {n: z.reshape(w2[n].shape) for n, z in zip(GATHERED, share_sibling(ts))}
    small = jnp.concatenate([gr[n].reshape(-1) for n in SMALL] + [gr['meta_tokens'].reshape(-1), lossp.reshape(-1)])
    ns = small.shape[0]
    nsp = -(-ns // (8 * LANE)) * 8 * LANE
    small_sum = allreduce8(jnp.pad(small, (0, nsp - ns)).reshape(-1, LANE), "allreduce_small").reshape(-1)
    g_small = dict(zip(SMALL + ['meta_full'], _unpack(small_sum, [w2[n].shape for n in SMALL] + [(N_META, D)])))
    g_shard['meta_tokens'] = lax.dynamic_slice(
        g_small['meta_full'], (jnp.zeros((), jnp.int32), (chip * mcols).astype(jnp.int32)), (N_META, mcols))
    loss = small_sum[ns - LANE]

    grads, deltas, new_m, new_v = [], [], [], []
    for n in WEIGHTS:
        gw = g_shard[n] if n in g_shard else g_small[n]
        d_, m_, v_ = adamw(w2[n], gw, m2[n], v2[n], "adamw_" + n)
        shp = a[n].shape
        grads.append(gw.reshape(shp))
        deltas.append(d_.reshape(shp))
        new_m.append(m_.reshape(shp))
        new_v.append(v_.reshape(shp))
    return (loss, grad_x, *grads, *deltas, *new_m, *new_v)


def kernel(x, meta_tokens, ffn1_norm, ffn1_w_gate, ffn1_w_up, ffn1_w_down, mix_norm, w_in, tm_mu, w0, w_up, a0, a_up, g_up, k_k, k_a, r_k, gn_w, gn_b, q_norm, w_uq, kv_norm, w_ukv, w_out, ffn2_norm, ffn2_w_gate, ffn2_w_up, ffn2_w_down, final_norm, loss_target, m_meta_tokens, m_ffn1_norm, m_ffn1_w_gate, m_ffn1_w_up, m_ffn1_w_down, m_mix_norm, m_w_in, m_tm_mu, m_w0, m_w_up, m_a0, m_a_up, m_g_up, m_k_k, m_k_a, m_r_k, m_gn_w, m_gn_b, m_q_norm, m_w_uq, m_kv_norm, m_w_ukv, m_w_out, m_ffn2_norm, m_ffn2_w_gate, m_ffn2_w_up, m_ffn2_w_down, m_final_norm, v_meta_tokens, v_ffn1_norm, v_ffn1_w_gate, v_ffn1_w_up, v_ffn1_w_down, v_mix_norm, v_w_in, v_tm_mu, v_w0, v_w_up, v_a0, v_a_up, v_g_up, v_k_k, v_k_a, v_r_k, v_gn_w, v_gn_b, v_q_norm, v_w_uq, v_kv_norm, v_w_ukv, v_w_out, v_ffn2_norm, v_ffn2_w_gate, v_ffn2_w_up, v_ffn2_w_down, v_final_norm):
    return _step(dict(locals()))
```

```python
import functools
import math

import jax
import jax.numpy as jnp
import numpy as np
from jax import lax
from jax.experimental import pallas as pl
from jax.experimental.pallas import tpu as pltpu

F32 = jnp.float32
BF16 = jnp.bfloat16
MESH = pl.DeviceIdType.MESH

N_META = 16
NORM_EPS = 1e-6
RWKV_HEAD = 64
GN_EPS = RWKV_HEAD * 1e-5
W_LORA, A_LORA, G_LORA = 96, 96, 256
Q_LORA, KV_LORA = 512, 512
NOPE_DIM, ROPE_DIM, V_DIM = 128, 64, 128
QK_DIM = NOPE_DIM + ROPE_DIM
ROPE_THETA = 10000.0
Q_BLOCK = 128
ADAM_LR, ADAM_B1, ADAM_B2, ADAM_EPS, ADAM_WD, ADAM_STEP = 0.001, 0.9, 0.999, 1e-08, 0.01, 10

LANE = 128
VMEM_LIMIT = 56 * 1024 * 1024


def _pcall(body, **kw):
    return pl.pallas_call(body, **kw)


def _cparams(sem):
    return pltpu.CompilerParams(dimension_semantics=sem, vmem_limit_bytes=VMEM_LIMIT)


MM_LANE_TILE = 1536
MM_ROW_TILE = 1408


def _div_tile(n, cap, unit):
    best = None
    for t in range(unit, min(n, cap) + 1, unit):
        if n % t == 0:
            best = t
    return best if best is not None else n


def _rtile(n, pref=512):
    best = None
    for t in range(16, min(n, pref * 2) + 1, 16):
        if n % t == 0 and (best is None or abs(t - pref) < abs(best - pref)):
            best = t
    return best if best is not None else n


def mm(a, b, mode="nn", out_dtype=F32, res=None, alpha=1.0, b_stack=False, out_stack=False, name="mm"):
    cs = b.shape[-1] if b_stack else None
    bs = (b.shape[1], 4 * cs) if b_stack else b.shape
    if mode == "nn":
        (M, K), (K2, N) = a.shape, bs
    elif mode == "nt":
        (M, K), (N, K2) = a.shape, bs
    else:
        (K, M), (K2, N) = a.shape, bs
    assert K == K2, (a.shape, b.shape, mode)
    if mode == "tn":
        tm, tk = _div_tile(M, MM_ROW_TILE, LANE), _div_tile(K, 1024, 16)
    else:
        tm = _div_tile(M, MM_ROW_TILE, 16)
        tk = _div_tile(cs if (b_stack and mode == "nt") else K, MM_LANE_TILE, LANE)
    ncol = N // 4 if out_stack else (cs if (b_stack and mode == "nn") else N)
    tn = _div_tile(ncol, MM_LANE_TILE if mode != "nt" else 1024, LANE)
    nk = K // tk
    dims = {"nn": (((1,), (0,)), ((), ())), "nt": (((1,), (1,)), ((), ())), "tn": (((0,), (0,)), ((), ()))}[mode]
    direct = out_dtype == F32

    def body(*refs):
        a_ref, b_ref = refs[:2]
        r_ref = refs[2] if res is not None else None
        o_ref = refs[3] if res is not None else refs[2]
        acc = o_ref if direct else refs[-1]
        k = pl.program_id(2)

        @pl.when(k == 0)
        def _():
            acc[...] = jnp.zeros_like(acc) if res is None else r_ref[...].astype(F32)

        p = lax.dot_general(a_ref[...].astype(BF16), b_ref[...].astype(BF16), dims, preferred_element_type=F32)
        acc[...] += p if alpha == 1.0 else alpha * p

        if not direct:
            @pl.when(k == nk - 1)
            def _():
                o_ref[...] = acc[...].astype(o_ref.dtype)

    if mode == "tn":
        a_spec = pl.BlockSpec((tk, tm), lambda i, j, k: (k, i))
        b_spec = pl.BlockSpec((tk, tn), lambda i, j, k: (k, j))
    else:
        a_spec = pl.BlockSpec((tm, tk), lambda i, j, k: (i, k))
        if mode == "nn":
            if b_stack:
                nps = cs // tn
                b_spec = pl.BlockSpec((None, tk, tn), lambda i, j, k: (j // nps, k, j % nps))
            else:
                b_spec = pl.BlockSpec((tk, tn), lambda i, j, k: (k, j))
        elif b_stack:
            kps = cs // tk
            b_spec = pl.BlockSpec((None, tn, tk), lambda i, j, k: (k // kps, j, k % kps))
        else:
            b_spec = pl.BlockSpec((tn, tk), lambda i, j, k: (j, k))
    r_spec = pl.BlockSpec((tm, tn), lambda i, j, k: (i, j))
    if out_stack:
        ops = (N // 4) // tn
        o_spec = pl.BlockSpec((None, tm, tn), lambda i, j, k: (j // ops, i, j % ops))
        o_shape = jax.ShapeDtypeStruct((4, M, N // 4), out_dtype)
    else:
        o_spec, o_shape = r_spec, jax.ShapeDtypeStruct((M, N), out_dtype)
    in_specs = [a_spec, b_spec] + ([r_spec] if res is not None else [])
    args = (a, b) + ((res,) if res is not None else ())
    return _pcall(
        body, name=name, grid=(M // tm, N // tn, nk), in_specs=in_specs, out_specs=o_spec, out_shape=o_shape,
        scratch_shapes=[] if direct else [pltpu.VMEM((tm, tn), F32)],
        compiler_params=_cparams(("parallel", "parallel", "arbitrary")),
    )(*args)


def rowwise(fn, row_ins, par_ins, outs, accs=(), *, tm, ncb=1, name="rowwise"):
    R = row_ins[0][0].shape[0]
    assert R % tm == 0, (R, tm)
    nrb = R // tm
    in_specs, args = [], []
    for spec in row_ins:
        arr, w, base = spec[:3]
        mod = spec[3] if len(spec) > 3 else None
        cstep = 0 if (len(spec) > 4 and spec[4]) else 1
        if mod is None:
            in_specs.append(pl.BlockSpec((tm, w), lambda j, i, base=base, cstep=cstep: (i, base + cstep * j)))
        else:
            in_specs.append(pl.BlockSpec((tm, w), lambda j, i, base=base, mod=mod, cstep=cstep: (i % mod, base + cstep * j)))
        args.append(arr)
    for arr, w, base in par_ins:
        in_specs.append(pl.BlockSpec((arr.shape[0], w), lambda j, i, base=base: (0, base + j)))
        args.append(arr)
    out_specs, out_shape = [], []
    for cols, w, dt in outs:
        out_specs.append(pl.BlockSpec((tm, w), lambda j, i: (i, j)))
        out_shape.append(jax.ShapeDtypeStruct((R, cols), dt))
    for p, cols, w in accs:
        out_specs.append(pl.BlockSpec((p, w), lambda j, i: (0, j)))
        out_shape.append(jax.ShapeDtypeStruct((p, cols), F32))
    nin, nout, nacc = len(args), len(outs), len(accs)

    def body(*refs):
        vals = [r[...] for r in refs[:nin]]
        res = fn(*vals)
        if not isinstance(res, (tuple, list)):
            res = (res,)
        assert len(res) == nout + nacc, (len(res), nout, nacc)
        for o_ref, v in zip(refs[nin:nin + nout], res[:nout]):
            o_ref[...] = v.astype(o_ref.dtype)
        if nacc:
            i = pl.program_id(1)

            @pl.when(i == 0)
            def _():
                for a_ref in refs[nin + nout:]:
                    a_ref[...] = jnp.zeros_like(a_ref)

            for a_ref, v in zip(refs[nin + nout:], res[nout:]):
                a_ref[...] += v.astype(F32)

    r = _pcall(
        body, name=name, grid=(ncb, nrb), in_specs=in_specs, out_specs=out_specs, out_shape=out_shape,
        compiler_params=_cparams(("parallel", "arbitrary")),
    )(*args)
    return r


def vjp_fn(fwd, nprim):
    def f(*vals):
        prim, cts = vals[:nprim], vals[nprim:]
        out, pull = jax.vjp(fwd, *[p.astype(F32) for p in prim])
        if not isinstance(out, (tuple, list)):
            cts = cts[0].astype(F32)
        else:
            cts = tuple(c.astype(F32) for c in cts)
        return pull(cts)
    return f


WKV_TB = 8


def _dup(x):
    return jnp.concatenate([x, x], axis=-1)


def wkv_fwd(w, kn, b, k, r, v):
    T, N, LH = w.shape
    NI, L = v.shape[1], v.shape[2]
    tb = WKV_TB
    assert T % tb == 0 and L == 2 * LH

    def body(w_ref, kn_ref, b_ref, k_ref, r_ref, v_ref, y_ref, sp_ref, sa_ref, s_ref):
        @pl.when(pl.program_id(0) == 0)
        def _():
            s_ref[...] = jnp.zeros_like(s_ref)

        def step(s, carry):
            W, KN, B, Kk, Rr = _dup(w_ref[s]), _dup(kn_ref[s]), _dup(b_ref[s]), _dup(k_ref[s]), _dup(r_ref[s])
            for i in range(NI):
                S = s_ref[i]
                sp_ref[s, i] = S
                sa = jnp.sum(S * KN, axis=0, keepdims=True)
                sa_ref[s, pl.ds(i, 1), :] = sa
                vi = v_ref[s, pl.ds(i, 1), :]
                Sn = S * W + sa * B + vi * Kk
                s_ref[i] = Sn
                y_ref[s, pl.ds(i, 1), :] = jnp.sum(Sn * Rr, axis=0, keepdims=True)
            return carry

        lax.fori_loop(0, tb, step, 0)

    jspec = pl.BlockSpec((tb, N, LH), lambda t: (t, 0, 0))
    ispec = pl.BlockSpec((tb, NI, L), lambda t: (t, 0, 0))
    ishape = jax.ShapeDtypeStruct((T, NI, L), F32)
    return _pcall(
        body, name="wkv_fwd", grid=(T // tb,), in_specs=[jspec] * 5 + [ispec],
        out_specs=[ispec, pl.BlockSpec((tb, NI, N, L), lambda t: (t, 0, 0, 0)), ispec],
        out_shape=[ishape, jax.ShapeDtypeStruct((T, NI, N, L), F32), ishape],
        scratch_shapes=[pltpu.VMEM((NI, N, L), F32)],
        compiler_params=_cparams(("arbitrary",)),
    )(w, kn, b, k, r, v)


def wkv_bwd(w, kn, b, k, r, v, dy, sp, sa):
    T, N, LH = w.shape
    NI, L = v.shape[1], v.shape[2]
    tb = WKV_TB
    nt = T // tb

    def body(w_ref, kn_ref, b_ref, k_ref, r_ref, v_ref, dy_ref, sp_ref, sa_ref,
             dw_ref, dkn_ref, db_ref, dk_ref, dr_ref, dv_ref, ds_ref):
        @pl.when(pl.program_id(0) == 0)
        def _():
            ds_ref[...] = jnp.zeros_like(ds_ref)

        def step(q, carry):
            s = tb - 1 - q
            W, KN, B, Kk, Rr = _dup(w_ref[s]), _dup(kn_ref[s]), _dup(b_ref[s]), _dup(k_ref[s]), _dup(r_ref[s])
            dW = jnp.zeros((N, L), F32)
            dKN, dB, dK, T1 = dW, dW, dW, dW
            al = jnp.zeros((1, L), F32)
            be = al
            for i in range(NI):
                Sp = sp_ref[s, i]
                vi = v_ref[s, pl.ds(i, 1), :]
                dyi = dy_ref[s, pl.ds(i, 1), :]
                sai = sa_ref[s, pl.ds(i, 1), :]
                dS = ds_ref[i] + dyi * Rr
                T1 = T1 + Sp * dyi
                al = al + sai * dyi
                be = be + vi * dyi
                dv_ref[s, pl.ds(i, 1), :] = jnp.sum(dS * Kk, axis=0, keepdims=True)
                dK = dK + dS * vi
                dsa = jnp.sum(dS * B, axis=0, keepdims=True)
                dB = dB + dS * sai
                dW = dW + dS * Sp
                dKN = dKN + Sp * dsa
                ds_ref[i] = dS * W + dsa * KN
            dR = W * T1 + B * al + Kk * be
            for ref, val in ((dw_ref, dW), (dkn_ref, dKN), (db_ref, dB), (dk_ref, dK), (dr_ref, dR)):
                ref[s] = (val + pltpu.roll(val, LH, 1))[:, :LH]
            return carry

        lax.fori_loop(0, tb, step, 0)

    jspec = pl.BlockSpec((tb, N, LH), lambda t: (nt - 1 - t, 0, 0))
    ispec = pl.BlockSpec((tb, NI, L), lambda t: (nt - 1 - t, 0, 0))
    jshape = jax.ShapeDtypeStruct((T, N, LH), F32)
    return _pcall(
        body, name="wkv_bwd", grid=(nt,),
        in_specs=[jspec] * 5 + [ispec, ispec, pl.BlockSpec((tb, NI, N, L), lambda t: (nt - 1 - t, 0, 0, 0)), ispec],
        out_specs=[jspec] * 5 + [ispec], out_shape=[jshape] * 5 + [jax.ShapeDtypeStruct((T, NI, L), F32)],
        scratch_shapes=[pltpu.VMEM((NI, N, L), F32)],
        compiler_params=_cparams(("arbitrary",)),
    )(w, kn, b, k, r, v, dy, sp, sa)


_NT = (((1,), (1,)), ((), ()))
_TN = (((0,), (0,)), ((), ()))
ATT_SCALE = QK_DIM ** -0.5


def _att_probs(q1, q2, kn_ref, kp_ref, i, L):
    s = lax.dot_general(q1, kn_ref[0, :L, :], _NT, preferred_element_type=F32)
    s = s + lax.dot_general(q2, kp_ref[0, :L, :], _NT, preferred_element_type=F32)
    s = s * ATT_SCALE
    col = lax.broadcasted_iota(jnp.int32, (Q_BLOCK, L), 1)
    row = lax.broadcasted_iota(jnp.int32, (Q_BLOCK, L), 0)
    valid = (col < N_META) | ((col >= Q_BLOCK) & (col - Q_BLOCK <= Q_BLOCK * i + row))
    s = jnp.where(valid, s, -1e30)
    m = jnp.max(s, axis=-1, keepdims=True)
    p = jnp.exp(s - m)
    return p / jnp.sum(p, axis=-1, keepdims=True)


def attn_fwd(qn, qp, kn, kp, v):
    B, S, HD = qn.shape
    SP = kn.shape[1]
    H = HD // LANE
    nq = S // Q_BLOCK

    def body(qn_ref, qp_ref, kn_ref, kp_ref, v_ref, o_ref):
        for i in range(nq):
            L = Q_BLOCK * (i + 2)
            rows = pl.ds(Q_BLOCK * i, Q_BLOCK)
            p = _att_probs(qn_ref[0, rows, :], qp_ref[0, rows, :], kn_ref, kp_ref, i, L)
            o_ref[0, rows, :] = jnp.dot(p.astype(BF16), v_ref[0, :L, :], preferred_element_type=F32)

    qspec = pl.BlockSpec((1, S, LANE), lambda b, h: (b, 0, h))
    kspec = pl.BlockSpec((1, SP, LANE), lambda b, h: (b, 0, h))
    pspec = pl.BlockSpec((1, SP, LANE), lambda b, h: (b, 0, 0))
    return _pcall(
        body, name="attn_fwd", grid=(B, H), in_specs=[qspec, qspec, kspec, pspec, kspec], out_specs=qspec,
        out_shape=jax.ShapeDtypeStruct((B, S, HD), F32), compiler_params=_cparams(("parallel", "parallel")),
    )(qn, qp, kn, kp, v)


def attn_bwd(qn, qp, kn, kp, v, do):
    B, S, HD = qn.shape
    SP = kn.shape[1]
    H = HD // LANE
    nq = S // Q_BLOCK

    def body(qn_ref, qp_ref, kn_ref, kp_ref, v_ref, do_ref, dqn_ref, dqp_ref, dkn_ref, dv_ref, dkp_ref, dkn_acc, dv_acc):
        @pl.when(pl.program_id(1) == 0)
        def _():
            dkp_ref[...] = jnp.zeros_like(dkp_ref)

        dkn_acc[...] = jnp.zeros_like(dkn_acc)
        dv_acc[...] = jnp.zeros_like(dv_acc)
        for i in range(nq):
            L = Q_BLOCK * (i + 2)
            rows = pl.ds(Q_BLOCK * i, Q_BLOCK)
            q1, q2, do_i = qn_ref[0, rows, :], qp_ref[0, rows, :], do_ref[0, rows, :]
            p = _att_probs(q1, q2, kn_ref, kp_ref, i, L)
            dp = lax.dot_general(do_i, v_ref[0, :L, :], _NT, preferred_element_type=F32)
            ds = (p * (dp - jnp.sum(p * dp, axis=-1, keepdims=True)) * ATT_SCALE).astype(BF16)
            dqn_ref[0, rows, :] = jnp.dot(ds, kn_ref[0, :L, :], preferred_element_type=F32).astype(dqn_ref.dtype)
            dqp_ref[0, rows, :] = jnp.dot(ds, kp_ref[0, :L, :], preferred_element_type=F32)
            dkn_acc[:L, :] += lax.dot_general(ds, q1, _TN, preferred_element_type=F32)
            dkp_ref[0, :L, :] += lax.dot_general(ds, q2, _TN, preferred_element_type=F32)
            dv_acc[:L, :] += lax.dot_general(p.astype(BF16), do_i, _TN, preferred_element_type=F32)
        dkn_ref[0] = dkn_acc[...].astype(dkn_ref.dtype)
        dv_ref[0] = dv_acc[...].astype(dv_ref.dtype)

    qspec = pl.BlockSpec((1, S, LANE), lambda b, h: (b, 0, h))
    kspec = pl.BlockSpec((1, SP, LANE), lambda b, h: (b, 0, h))
    pspec = pl.BlockSpec((1, SP, LANE), lambda b, h: (b, 0, 0))
    return _pcall(
        body, name="attn_bwd", grid=(B, H), in_specs=[qspec, qspec, kspec, pspec, kspec, qspec],
        out_specs=[qspec, qspec, kspec, kspec, pspec],
        out_shape=[jax.ShapeDtypeStruct((B, S, HD), BF16), jax.ShapeDtypeStruct((B, S, HD), F32),
                   jax.ShapeDtypeStruct((B, SP, HD), BF16), jax.ShapeDtypeStruct((B, SP, HD), BF16),
                   jax.ShapeDtypeStruct((B, SP, LANE), F32)],
        scratch_shapes=[pltpu.VMEM((SP, LANE), F32), pltpu.VMEM((SP, LANE), F32)],
        compiler_params=_cparams(("parallel", "arbitrary")),
    )(qn, qp, kn, kp, v, do)


_HBM = pl.BlockSpec(memory_space=pltpu.HBM)


def _place():
    x, y, c = lax.axis_index("x"), lax.axis_index("y"), lax.axis_index("c")
    chips = [(1 - x, y), (x, 1 - y), (1 - x, 1 - y)]
    return x, y, c, chips


def _rcopy(src, dst, ssem, rsem, dev):
    return pltpu.make_async_remote_copy(src_ref=src, dst_ref=dst, send_sem=ssem, recv_sem=rsem,
                                        device_id=dev, device_id_type=MESH)


def _half_rows(c, r):
    return pl.ds(pl.multiple_of(c * (r // 2), 16), r // 2)


def ag_weights(bufs):
    n = len(bufs)

    def body(*refs):
        w, out = refs[:n], refs[n:2 * n]
        ssem, rsem = refs[2 * n:]
        x, y, c, chips = _place()
        s = 2 * x + y
        rows = [_half_rows(c, w[i].shape[1]) for i in range(n)]
        orows = [_half_rows(1 - c, w[i].shape[1]) for i in range(n)]
        first = [_rcopy(w[i].at[s, rows[i]], out[i].at[s, rows[i]], ssem.at[6 * i + j], rsem.at[6 * i + j], (px, py, c))
                 for i in range(n) for j, (px, py) in enumerate(chips)]
        for cp in first:
            cp.start()
        passed = []
        for j, (px, py) in enumerate(chips):
            sp = 2 * px + py
            for i in range(n):
                here = out[i].at[sp, rows[i]]
                _rcopy(here, here, ssem.at[6 * i + j], rsem.at[6 * i + j], (px, py, c)).wait_recv()
                fw = _rcopy(here, here, ssem.at[6 * i + 3 + j], rsem.at[6 * i + 3 + j], (x, y, 1 - c))
                fw.start()
                passed.append(fw)
        for j, (px, py) in enumerate(chips):
            sp = 2 * px + py
            for i in range(n):
                there = out[i].at[sp, orows[i]]
                _rcopy(there, there, ssem.at[6 * i + 3 + j], rsem.at[6 * i + 3 + j], (x, y, 1 - c)).wait_recv()
        for cp in first + passed:
            cp.wait_send()

    return _pcall(
        body, name="ag_weights", in_specs=[_HBM] * n, out_specs=[_HBM] * n,
        out_shape=[jax.ShapeDtypeStruct(w.shape, w.dtype) for w in bufs],
        input_output_aliases={i: i for i in range(n)},
        scratch_shapes=[pltpu.SemaphoreType.DMA((6 * n,)), pltpu.SemaphoreType.DMA((6 * n,))],
    )(*bufs)


def swap_halves(gs):
    n = len(gs)

    def body(*refs):
        g, out = refs[:n], refs[n:2 * n]
        ssem, rsem = refs[2 * n:]
        x, y, c, _ = _place()
        cps = [_rcopy(g[i].at[:, _half_rows(1 - c, g[i].shape[1])], out[i], ssem.at[i], rsem.at[i], (x, y, 1 - c))
               for i in range(n)]
        for cp in cps:
            cp.start()
        for cp in cps:
            cp.wait()

    return _pcall(
        body, name="swap_halves", in_specs=[_HBM] * n, out_specs=[_HBM] * n,
        out_shape=[jax.ShapeDtypeStruct((4, g.shape[1] // 2, g.shape[2]), g.dtype) for g in gs],
        scratch_shapes=[pltpu.SemaphoreType.DMA((n,)), pltpu.SemaphoreType.DMA((n,))],
    )(*gs)


def rs_chips(hs):
    n = len(hs)

    def body(*refs):
        h, out = refs[:n], refs[n:2 * n]
        ssem, rsem = refs[2 * n:]
        x, y, c, chips = _place()
        s = 2 * x + y
        cps = [_rcopy(h[i].at[2 * px + py], out[i].at[s], ssem.at[3 * i + j], rsem.at[3 * i + j], (px, py, c))
               for i in range(n) for j, (px, py) in enumerate(chips)]
        for cp in cps:
            cp.start()
        for j, (px, py) in enumerate(chips):
            for i in range(n):
                _rcopy(h[i].at[s], out[i].at[2 * px + py], ssem.at[3 * i + j], rsem.at[3 * i + j], (px, py, c)).wait_recv()
        for cp in cps:
            cp.wait_send()

    return _pcall(
        body, name="rs_chips", in_specs=[_HBM] * n, out_specs=[_HBM] * n,
        out_shape=[jax.ShapeDtypeStruct(h.shape, h.dtype) for h in hs],
        scratch_shapes=[pltpu.SemaphoreType.DMA((3 * n,)), pltpu.SemaphoreType.DMA((3 * n,))],
    )(*hs)


def share_sibling(ts):
    n = len(ts)

    def body(*refs):
        t, out = refs[:n], refs[n:2 * n]
        ssem, rsem = refs[2 * n:]
        x, y, c, _ = _place()
        cps = [_rcopy(t[i].at[c], out[i].at[c], ssem.at[i], rsem.at[i], (x, y, 1 - c)) for i in range(n)]
        for cp in cps:
            cp.start()
        for i in range(n):
            _rcopy(t[i].at[c], out[i].at[1 - c], ssem.at[i], rsem.at[i], (x, y, 1 - c)).wait_recv()
        for cp in cps:
            cp.wait_send()

    return _pcall(
        body, name="share_sibling", in_specs=[_HBM] * n, out_specs=[_HBM] * n,
        out_shape=[jax.ShapeDtypeStruct(t.shape, t.dtype) for t in ts],
        input_output_aliases={i: i for i in range(n)},
        scratch_shapes=[pltpu.SemaphoreType.DMA((n,)), pltpu.SemaphoreType.DMA((n,))],
    )(*ts)


def allreduce8(v, name):
    P, W = v.shape

    def body(v_ref, out_ref, buf, ssem, rsem):
        x, y, c, _ = _place()
        me = 4 * x + 2 * y + c
        buf[me] = v_ref[...]
        cps = []
        for k in range(1, 8):
            px = 1 - x if k & 4 else x
            py = 1 - y if k & 2 else y
            pc = 1 - c if k & 1 else c
            cp = _rcopy(buf.at[me], buf.at[me], ssem.at[k - 1], rsem.at[k - 1], (px, py, pc))
            cp.start()
            cps.append((cp, 4 * px + 2 * py + pc))
        for k, (cp, peer) in enumerate(cps):
            _rcopy(buf.at[me], buf.at[peer], ssem.at[k], rsem.at[k], (x, y, c)).wait_recv()
        for cp, _ in cps:
            cp.wait_send()
        acc = buf[0]
        for d in range(1, 8):
            acc = acc + buf[d]
        out_ref[...] = acc

    return _pcall(
        body, name=name, in_specs=[pl.BlockSpec(memory_space=pltpu.VMEM)],
        out_specs=pl.BlockSpec(memory_space=pltpu.VMEM), out_shape=jax.ShapeDtypeStruct((P, W), F32),
        scratch_shapes=[pltpu.VMEM((8, P, W), F32), pltpu.SemaphoreType.DMA((7,)), pltpu.SemaphoreType.DMA((7,))],
    )(v)


def add_half(g, rcv, cidx, name):
    _, hr, W = rcv.shape
    tr = _rows_tile(hr, W)
    nb = hr // tr

    def body(c_ref, g_ref, r_ref, o_ref):
        o_ref[...] = (g_ref[...] + r_ref[...]).astype(o_ref.dtype)

    return _pcall(
        body, name=name,
        grid_spec=pltpu.PrefetchScalarGridSpec(
            num_scalar_prefetch=1, grid=(4, nb),
            in_specs=[pl.BlockSpec((1, tr, W), lambda s, i, c: (s, c[0] * nb + i, 0)),
                      pl.BlockSpec((1, tr, W), lambda s, i, c: (s, i, 0))],
            out_specs=pl.BlockSpec((1, tr, W), lambda s, i, c: (s, i, 0))),
        out_shape=jax.ShapeDtypeStruct((4, hr, W), BF16), compiler_params=_cparams(("parallel", "parallel")),
    )(cidx, g, rcv)


def sum_chips(p, h, sc, name):
    _, hr, W = p.shape
    tr = _rows_tile(hr, W)

    def body(sc_ref, p_ref, h_ref, o_ref):
        s = sc_ref[0]
        own = h_ref[0]
        f = lambda k: jnp.where(s == k, own, p_ref[k]).astype(F32)
        o_ref[0] = ((f(0) + f(1)) + f(2)) + f(3)

    return _pcall(
        body, name=name,
        grid_spec=pltpu.PrefetchScalarGridSpec(
            num_scalar_prefetch=1, grid=(hr // tr,),
            in_specs=[pl.BlockSpec((4, tr, W), lambda i, sc: (0, i, 0)),
                      pl.BlockSpec((1, tr, W), lambda i, sc: (sc[0], i, 0))],
            out_specs=pl.BlockSpec((1, tr, W), lambda i, sc: (sc[1], i, 0))),
        out_shape=jax.ShapeDtypeStruct((2, hr, W), F32), compiler_params=_cparams(("parallel",)),
    )(sc, p, h)


HI = lax.Precision.HIGHEST
BLOCK_BYTES = 3 * 512 * 1024


def _rows_tile(R, w, T=None):
    cands = [t for t in range(16, R + 1, 16) if R % t == 0 and (T is None or T % t == 0)]
    ok = [t for t in cands if t * w * 4 <= BLOCK_BYTES]
    return max(ok) if ok else min(cands)


def _gsum(x):
    r = jnp.right_shift(lax.broadcasted_iota(jnp.int32, (LANE, LANE), 0), 6)
    c = jnp.right_shift(lax.broadcasted_iota(jnp.int32, (LANE, LANE), 1), 6)
    return jnp.dot(x, (r == c).astype(F32), precision=HI, preferred_element_type=F32)


def _rms(x, g):
    return x * lax.rsqrt(jnp.mean(x * x, axis=-1, keepdims=True) + NORM_EPS) * g


def _silu_mul(gate, up):
    return jax.nn.silu(gate) * up


def _shift(p, prev, mu):
    return p + mu * (prev - p)


def _lora_act(p, prev, mu):
    s = _shift(p, prev, mu)
    return jax.nn.sigmoid(s[:, :G_LORA]), jnp.tanh(s[:, G_LORA:G_LORA + LANE]), s[:, G_LORA + LANE:]


def _prep(k, lw, la, w0, a0, kk_w, ka_w):
    wpre = -jax.nn.softplus(-(w0 + lw)) - 0.5
    decay = jnp.exp(-jnp.exp(wpre))
    a = jax.nn.sigmoid(a0 + la)
    kk = k * kk_w
    kk = kk * lax.rsqrt(jnp.maximum(_gsum(kk * kk), 1e-24))
    k2 = k * (1.0 + (a - 1.0) * ka_w)
    return decay, -kk, kk * a, k2


def _post(y, r, k2, v, g, gnw, gnb, rk):
    mean = _gsum(y) * (1.0 / RWKV_HEAD)
    d = y - mean
    var = _gsum(d * d) * (1.0 / RWKV_HEAD)
    yn = d * lax.rsqrt(var + GN_EPS) * gnw + gnb
    bonus = _gsum(r * k2 * rk) * v
    return (yn + bonus) * g


def _gate_mix(ga, gb, ya, o):
    return jax.nn.sigmoid(ga) * ya + jax.nn.sigmoid(gb) * o


def _rope(x, cos, sin):
    return x * cos + pltpu.roll(x, LANE // 2, 1) * sin


def _rope_t(dy, cos, sin):
    return dy * cos + pltpu.roll(dy * sin, LANE // 2, 1)


def _mla_pre(cq, ckv, kpe, cos, sin, qw, kvw):
    return _rms(cq, qw), _rms(ckv, kvw), _rope(kpe, cos, sin)


def _mla_pre_bwd(cq, ckv, dcqn, dckvn, dkr, cos, sin, qw, kvw):
    _, pull = jax.vjp(lambda a, b, c, d: (_rms(a, c), _rms(b, d)), cq, ckv, qw, kvw)
    dcq, dckv, dqw, dkvw = pull((dcqn, dckvn))
    return dcq, dckv, _rope_t(dkr, cos, sin), dqw, dkvw


WEIGHTS = ['meta_tokens', 'ffn1_norm', 'ffn1_w_gate', 'ffn1_w_up', 'ffn1_w_down', 'mix_norm', 'w_in', 'tm_mu', 'w0',
           'w_up', 'a0', 'a_up', 'g_up', 'k_k', 'k_a', 'r_k', 'gn_w', 'gn_b', 'q_norm', 'w_uq', 'kv_norm', 'w_ukv',
           'w_out', 'ffn2_norm', 'ffn2_w_gate', 'ffn2_w_up', 'ffn2_w_down', 'final_norm']
SHARD_AXIS = {'meta_tokens': 1, 'ffn1_w_gate': 1, 'ffn1_w_up': 1, 'ffn1_w_down': 0, 'w_in': 1, 'w_up': 1, 'a_up': 1,
              'g_up': 1, 'w_uq': 1, 'w_ukv': 1, 'w_out': 0, 'ffn2_w_gate': 1, 'ffn2_w_up': 1, 'ffn2_w_down': 0}
GATHERED = [n for n in WEIGHTS if n in SHARD_AXIS and n != 'meta_tokens']
FFN_IN = ('ffn1_w_gate', 'ffn1_w_up', 'ffn2_w_gate', 'ffn2_w_up')
SMALL = [n for n in WEIGHTS if n not in SHARD_AXIS]


def _to2d(a):
    if a.ndim == 1:
        return a.reshape(1, -1)
    if a.ndim == 3:
        return a.reshape(a.shape[0] * a.shape[1], a.shape[2]) if a.shape[0] == 1 and a.shape[1] > 64 else a.reshape(1, -1)
    return a


def _unpack(flat, shapes):
    out, off = [], 0
    for shp in shapes:
        n = shp[0] * shp[1]
        out.append(flat[off:off + n].reshape(shp))
        off += n
    return out


def _adamw(w, g, m, v):
    m = ADAM_B1 * m + (1.0 - ADAM_B1) * g
    v = ADAM_B2 * v + (1.0 - ADAM_B2) * jnp.square(g)
    m_hat = m / (1.0 - ADAM_B1 ** ADAM_STEP)
    v_hat = v / (1.0 - ADAM_B2 ** ADAM_STEP)
    delta = -ADAM_LR * (m_hat / (jnp.sqrt(v_hat) + ADAM_EPS) + ADAM_WD * w)
    return delta, m, v


def adamw(w, g, m, v, name):
    R, C = w.shape
    if R % 16 == 0 and R > 16:
        tm = _rows_tile(R, C)
    else:
        tm = R
    return rowwise(_adamw, [(w, C, 0), (g, C, 0), (m, C, 0), (v, C, 0)], [], [(C, C, F32)] * 3, tm=tm, name=name)


def _step(a):
    x = a['x']
    Bl, S, D = x.shape
    T = S + N_META
    R, RS = Bl * T, Bl * S
    Hr, Hm = D // RWKV_HEAD, D // LANE
    w2 = {n: _to2d(a[n]) for n in WEIGHTS}
    m2 = {n: _to2d(a['m_' + n]) for n in WEIGHTS}
    v2 = {n: _to2d(a['v_' + n]) for n in WEIGHTS}
    xi, yi, ci = lax.axis_index("x"), lax.axis_index("y"), lax.axis_index("c")
    chip = 2 * xi + yi

    i0 = jnp.zeros((), jnp.int32)
    chip32 = chip.astype(jnp.int32)

    def own_slot(w):
        return lax.dynamic_update_slice(jnp.zeros((4,) + w.shape, BF16), w.astype(BF16)[None], (chip32, i0, i0))

    stk = dict(zip(GATHERED, ag_weights([own_slot(w2[n]) for n in GATHERED])))

    def unstack(z, axis):
        return z.reshape(4 * z.shape[1], z.shape[2]) if axis == 0 else jnp.concatenate([z[s] for s in range(4)], axis=1)

    full = {n: unstack(stk[n], SHARD_AXIS[n]) for n in GATHERED if n not in FFN_IN}
    mt = w2['meta_tokens']
    mcols = mt.shape[1]
    mt_z = lax.dynamic_update_slice(jnp.zeros((N_META, D), F32), 0.5 * mt, (jnp.zeros((), jnp.int32), (chip * mcols).astype(jnp.int32)))
    meta_full = allreduce8(mt_z.reshape(-1, LANE), "gather_meta").reshape(N_META, D)

    F = 4 * stk['ffn1_w_gate'].shape[2]
    win = full['w_in']
    o = 3 * D
    c_xw, c_xa, c_xg = win[:, o:o + W_LORA], win[:, o + W_LORA:o + 2 * W_LORA], win[:, o + 2 * W_LORA:o + 2 * W_LORA + G_LORA]
    o += 2 * W_LORA + G_LORA
    c_cq, c_ckv, c_kpe = win[:, o:o + Q_LORA], win[:, o + Q_LORA:o + Q_LORA + KV_LORA], win[:, o + Q_LORA + KV_LORA:o + Q_LORA + KV_LORA + ROPE_DIM]
    o += Q_LORA + KV_LORA + ROPE_DIM
    c_ga, c_gb = win[:, o:o + D], win[:, o + D:o + 2 * D]
    zc = lambda n: jnp.zeros((D, n), BF16)
    half = ROPE_DIM // 2
    NP0 = 5 * D + 512 + Q_LORA + KV_LORA + LANE
    NP = -(-NP0 // 512) * 512
    win_p = jnp.concatenate([win[:, :3 * D], c_ga, c_gb, c_xg, c_xw, zc(LANE - W_LORA), c_xa, zc(LANE - A_LORA), c_cq, c_ckv,
                             c_kpe[:, :half], zc(half), c_kpe[:, half:], zc(half), zc(NP - NP0)], axis=1)
    O_GA, O_GB, O_L, O_CQ, O_CKV, O_KPE = 3 * D, 4 * D, 5 * D, 5 * D + 512, 5 * D + 512 + Q_LORA, 5 * D + 512 + Q_LORA + KV_LORA
    zr = lambda n: jnp.zeros((n, D), BF16)
    w_up_p = jnp.concatenate([full['w_up'], zr(LANE - W_LORA)], axis=0)
    a_up_p = jnp.concatenate([full['a_up'], zr(LANE - A_LORA)], axis=0)
    g_up = full['g_up']
    wuq = full['w_uq'].reshape(Q_LORA, Hm, QK_DIM)
    zq = jnp.zeros((Q_LORA, Hm, half), BF16)
    wqn = wuq[:, :, :NOPE_DIM].reshape(Q_LORA, Hm * LANE)
    wqp = jnp.concatenate([wuq[:, :, NOPE_DIM:NOPE_DIM + half], zq, wuq[:, :, NOPE_DIM + half:], zq], axis=2).reshape(Q_LORA, Hm * LANE)
    wukv = full['w_ukv'].reshape(KV_LORA, Hm, NOPE_DIM + V_DIM)
    wkn = wukv[:, :, :NOPE_DIM].reshape(KV_LORA, Hm * LANE)
    wv = wukv[:, :, NOPE_DIM:].reshape(KV_LORA, Hm * LANE)
    wout = full['w_out']
    tmu = w2['tm_mu']
    mu_a = tmu[:, :3 * D]
    zm = lambda n: jnp.zeros((1, n), F32)
    mu_b = jnp.concatenate([tmu[:, 3 * D + 2 * W_LORA:], tmu[:, 3 * D:3 * D + W_LORA], zm(LANE - W_LORA),
                            tmu[:, 3 * D + W_LORA:3 * D + 2 * W_LORA], zm(LANE - A_LORA)], axis=1)
    pos = jnp.arange(T, dtype=F32)
    inv_freq = 1.0 / (ROPE_THETA ** (jnp.arange(0, ROPE_DIM, 2, dtype=F32) / ROPE_DIM))
    ang = pos[:, None] * inv_freq[None, :]
    zt = jnp.zeros((T, half), F32)
    cos_t = jnp.concatenate([jnp.cos(ang), zt, jnp.cos(ang), zt], axis=1)
    sin_t = jnp.concatenate([-jnp.sin(ang), zt, jnp.sin(ang), zt], axis=1)
    cos_q, sin_q = cos_t[N_META:], sin_t[N_META:]

    t_full = _rows_tile(R, D)
    t_512 = _rows_tile(R, 512, T)
    t_128 = _rows_tile(R, LANE, T)
    tq_full = _rows_tile(RS, D)
    tq_128 = _rows_tile(RS, LANE, S)

    def rows3(z):
        return z.reshape(Bl, T, z.shape[-1])

    def real_rows(z):
        return rows3(z)[:, N_META:].reshape(RS, z.shape[-1])

    def pad_meta(z):
        z3 = z.reshape(Bl, S, z.shape[-1])
        return jnp.concatenate([jnp.zeros((Bl, N_META, z.shape[-1]), z.dtype), z3], axis=1).reshape(R, z.shape[-1])

    def shift_down(z):
        z3 = rows3(z)
        return jnp.concatenate([jnp.zeros((Bl, 1, z.shape[-1]), z.dtype), z3[:, :-1]], axis=1).reshape(R, z.shape[-1])

    def shift_up(z):
        z3 = rows3(z)
        return jnp.concatenate([z3[:, 1:], jnp.zeros((Bl, 1, z.shape[-1]), z.dtype)], axis=1).reshape(R, z.shape[-1])

    def ffn_fwd(h, nw, wg, wu, wd, tag):
        n = rowwise(_rms, [(h, D, 0)], [(nw, D, 0)], [(D, D, BF16)], tm=t_full, name=tag + "_norm")[0]
        gate = mm(n, wg, b_stack=True, name=tag + "_gate")
        up = mm(n, wu, b_stack=True, name=tag + "_up")
        act = rowwise(_silu_mul, [(gate, 512, 0), (up, 512, 0)], [], [(F, 512, BF16)], tm=t_512, ncb=F // 512,
                      name=tag + "_act")[0]
        return mm(act, wd, res=h, alpha=0.5, name=tag + "_down"), (h, n, gate, up, act)

    def ffn_bwd(dh2, saved, nw, wg, wu, wd, tag):
        h, n, gate, up, act = saved
        dz = rowwise(lambda d: 0.5 * d, [(dh2, 512, 0)], [], [(D, 512, BF16)], tm=t_512, ncb=D // 512, name=tag + "_dz")[0]
        d_wd = mm(act, dz, "tn", name=tag + "_dwd")
        dact = mm(dz, wd, "nt", name=tag + "_dact")
        dgate, dup = rowwise(vjp_fn(_silu_mul, 2), [(gate, 512, 0), (up, 512, 0), (dact, 512, 0)], [],
                             [(F, 512, BF16)] * 2, tm=t_512, ncb=F // 512, name=tag + "_dact2")
        d_wg = mm(n, dgate, "tn", out_stack=True, name=tag + "_dwg")
        d_wu = mm(n, dup, "tn", out_stack=True, name=tag + "_dwu")
        dn = mm(dgate, wg, "nt", b_stack=True, name=tag + "_dn1")
        dn = mm(dup, wu, "nt", res=dn, b_stack=True, name=tag + "_dn2")

        def f(h_, dn_, dh_, nw_):
            dh, dnw = vjp_fn(_rms, 2)(h_, nw_, dn_)
            return dh + dh_, dnw

        dh, d_nw = rowwise(f, [(h, D, 0), (dn, D, 0), (dh2, D, 0)], [(nw, D, 0)], [(D, D, F32)], [(1, D, D)],
                           tm=t_full, name=tag + "_dnorm")
        return dh, d_nw, d_wg, d_wu, d_wd

    h0 = jnp.concatenate([jnp.broadcast_to(meta_full[None], (Bl, N_META, D)), x], axis=1).reshape(R, D)
    h1, sv1 = ffn_fwd(h0, w2['ffn1_norm'], stk['ffn1_w_gate'], stk['ffn1_w_up'], full['ffn1_w_down'], "ffn1")
    u = rowwise(_rms, [(h1, D, 0)], [(w2['mix_norm'], D, 0)], [(D, D, BF16)], tm=t_full, name="mix_norm")[0]
    proj = mm(u, win_p, name="proj")
    prev_a = shift_down(proj[:, :3 * D])
    prev_b = shift_down(proj[:, O_L:O_L + 512])
    ps = rowwise(_shift, [(proj, 512, 0), (prev_a, 512, 0)], [(mu_a, 512, 0)], [(3 * D, 512, F32)], tm=t_512,
                 ncb=3 * D // 512, name="shift_rkv")[0]
    sg, txw, xas = rowwise(_lora_act, [(proj, 512, O_L // 512), (prev_b, 512, 0)], [(mu_b, 512, 0)],
                           [(G_LORA, G_LORA, BF16), (LANE, LANE, BF16), (LANE, LANE, BF16)], tm=t_512, name="lora_act")
    lw = mm(txw, w_up_p, name="lora_w")
    la = mm(xas, a_up_p, name="lora_a")
    g = mm(sg, g_up, name="lora_g")
    hb = D // LANE
    par_d = lambda n: (w2[n], LANE, 0)
    decay, kn, bb, k2 = rowwise(_prep, [(ps, LANE, hb), (lw, LANE, 0), (la, LANE, 0)],
                                [par_d('w0'), par_d('a0'), par_d('k_k'), par_d('k_a')], [(D, LANE, F32)] * 4,
                                tm=t_128, ncb=hb, name="wkv_prep")

    def to_j(z):
        return z.reshape(Bl, T, Hr, RWKV_HEAD).transpose(1, 3, 0, 2).reshape(T, RWKV_HEAD, Bl * Hr)

    def to_i(z):
        return z.reshape(Bl, T, Hr, 2, RWKV_HEAD // 2).transpose(1, 4, 3, 0, 2).reshape(T, RWKV_HEAD // 2, 2 * Bl * Hr)

    def from_i(z):
        return z.reshape(T, RWKV_HEAD // 2, 2, Bl, Hr).transpose(3, 0, 4, 2, 1).reshape(R, D)

    def from_j(z):
        return z.reshape(T, RWKV_HEAD, Bl, Hr).transpose(2, 0, 3, 1).reshape(R, D)

    r_s, v_s = ps[:, :D], ps[:, 2 * D:]
    jw, jkn, jb, jk, jr, iv = to_j(decay), to_j(kn), to_j(bb), to_j(k2), to_j(r_s), to_i(v_s)
    y_i, sp, sa_i = wkv_fwd(jw, jkn, jb, jk, jr, iv)
    y = from_i(y_i)
    post_rows = [(y, LANE, 0), (ps, LANE, 0), (k2, LANE, 0), (ps, LANE, 2 * hb), (g, LANE, 0)]
    post_pars = [par_d('gn_w'), par_d('gn_b'), par_d('r_k')]
    ya = rowwise(_post, post_rows, post_pars, [(D, LANE, F32)], tm=t_128, ncb=hb, name="wkv_post")[0]

    nt512 = T // t_512
    mla_rows = [(proj, Q_LORA, O_CQ // Q_LORA), (proj, KV_LORA, O_CKV // KV_LORA), (proj, LANE, O_KPE // LANE)]
    tabs = [(cos_t, LANE, 0, nt512, True), (sin_t, LANE, 0, nt512, True)]
    mla_pars = [(w2['q_norm'], Q_LORA, 0), (w2['kv_norm'], KV_LORA, 0)]
    cqn, ckvn, kpr = rowwise(_mla_pre, mla_rows + tabs, mla_pars,
                             [(Q_LORA, Q_LORA, BF16), (KV_LORA, KV_LORA, BF16), (LANE, LANE, BF16)], tm=t_512, name="mla_pre")
    cqn_r = real_rows(cqn)
    qn = mm(cqn_r, wqn, out_dtype=BF16, name="q_nope")
    qp_raw = mm(cqn_r, wqp, name="q_pe")
    ntq = S // tq_128
    qtabs = [(cos_q, LANE, 0, ntq, True), (sin_q, LANE, 0, ntq, True)]
    qp = rowwise(_rope, [(qp_raw, LANE, 0)] + qtabs, [], [(D, LANE, BF16)], tm=tq_128, ncb=Hm, name="q_rope")[0]
    knope = mm(ckvn, wkn, out_dtype=BF16, name="k_nope")
    vv = mm(ckvn, wv, out_dtype=BF16, name="v_proj")

    def pad_keys(z):
        z3 = rows3(z)
        return jnp.concatenate([z3[:, :N_META], jnp.zeros((Bl, Q_BLOCK - N_META, z.shape[-1]), z.dtype), z3[:, N_META:]], axis=1)

    def unpad_keys(z):
        return jnp.concatenate([z[:, :N_META], z[:, Q_BLOCK:]], axis=1).reshape(R, z.shape[-1])

    qn3, qp3 = qn.reshape(Bl, S, D), qp.reshape(Bl, S, D)
    knp, kpp, vp = pad_keys(knope), pad_keys(kpr), pad_keys(vv)
    o_att = pad_meta(attn_fwd(qn3, qp3, knp, kpp, vp).reshape(RS, D))
    mix_rows = [(proj, 512, O_GA // 512), (proj, 512, O_GB // 512), (ya, 512, 0), (o_att, 512, 0)]
    mix = rowwise(_gate_mix, mix_rows, [], [(D, 512, BF16)], tm=t_512, ncb=D // 512, name="gate_mix")[0]
    h2 = mm(mix, wout, res=h1, name="w_out")
    h3, sv2 = ffn_fwd(h2, w2['ffn2_norm'], stk['ffn2_w_gate'], stk['ffn2_w_up'], full['ffn2_w_down'], "ffn2")

    def loss_fb(h_, tgt, fw):
        yv, pull = jax.vjp(_rms, h_, fw)
        e = yv - tgt
        dh, dfw = pull(e * (1.0 / D))
        return dh, jnp.full((1, LANE), 0.5 / D * jnp.sum(e * e), F32), dfw

    dh3r, lossp, g_final = rowwise(loss_fb, [(real_rows(h3), D, 0), (a['loss_target'].reshape(RS, D), D, 0)],
                                   [(w2['final_norm'], D, 0)], [(D, D, F32)], [(1, LANE, LANE), (1, D, D)],
                                   tm=tq_full, name="loss")
    dh3 = pad_meta(dh3r)

    gr = {'final_norm': g_final}
    dh2, gr['ffn2_norm'], gr['ffn2_w_gate'], gr['ffn2_w_up'], gr['ffn2_w_down'] = ffn_bwd(
        dh3, sv2, w2['ffn2_norm'], stk['ffn2_w_gate'], stk['ffn2_w_up'], full['ffn2_w_down'], "ffn2")
    dh2b = rowwise(lambda d: d, [(dh2, 512, 0)], [], [(D, 512, BF16)], tm=t_512, ncb=D // 512, name="dh2_cast")[0]
    gr['w_out'] = mm(mix, dh2b, "tn", name="d_wout")
    dmix = mm(dh2b, wout, "nt", name="d_mix")
    dga, dgb, dya, do = rowwise(vjp_fn(_gate_mix, 4), mix_rows + [(dmix, 512, 0)], [],
                                [(D, 512, BF16), (D, 512, BF16), (D, 512, F32), (D, 512, BF16)], tm=t_512, ncb=D // 512,
                                name="d_gate_mix")
    dqn, dqp, dknp, dvp, dkpp = attn_bwd(qn3, qp3, knp, kpp, vp, real_rows(do).reshape(Bl, S, D))
    dqn2 = dqn.reshape(RS, D)
    dqp_raw = rowwise(_rope_t, [(dqp.reshape(RS, D), LANE, 0)] + qtabs, [], [(D, LANE, BF16)], tm=tq_128, ncb=Hm,
                      name="d_q_rope")[0]
    d_wqn = mm(cqn_r, dqn2, "tn", name="d_wqn")
    d_wqp = mm(cqn_r, dqp_raw, "tn", name="d_wqp")
    dcqn = mm(dqn2, wqn, "nt", name="d_cqn1")
    dcqn = pad_meta(mm(dqp_raw, wqp, "nt", res=dcqn, name="d_cqn2"))
    dkn2, dv2, dkp2 = unpad_keys(dknp), unpad_keys(dvp), unpad_keys(dkpp)
    d_wkn = mm(ckvn, dkn2, "tn", name="d_wkn")
    d_wv = mm(ckvn, dv2, "tn", name="d_wv")
    dckvn = mm(dkn2, wkn, "nt", name="d_ckvn1")
    dckvn = mm(dv2, wv, "nt", res=dckvn, name="d_ckvn2")
    dcq, dckv, dkpe, gr['q_norm'], gr['kv_norm'] = rowwise(
        _mla_pre_bwd, mla_rows[:2] + [(dcqn, Q_LORA, 0), (dckvn, KV_LORA, 0), (dkp2, LANE, 0)] + tabs, mla_pars,
        [(Q_LORA, Q_LORA, BF16), (KV_LORA, KV_LORA, BF16), (LANE, LANE, BF16)], [(1, Q_LORA, Q_LORA), (1, KV_LORA, KV_LORA)],
        tm=t_512, name="d_mla_pre")

    def post_bwd(y_, r_, k2_, v_, g_, dya_, gnw, gnb, rk):
        return vjp_fn(_post, 8)(y_, r_, k2_, v_, g_, gnw, gnb, rk, dya_)

    dy, dr_b, dk2_b, dv_b, dg, gr['gn_w'], gr['gn_b'], gr['r_k'] = rowwise(
        post_bwd, post_rows + [(dya, LANE, 0)], post_pars,
        [(D, LANE, F32)] * 4 + [(D, LANE, BF16)], [(1, D, LANE)] * 3, tm=t_128, ncb=hb, name="d_wkv_post")
    jdw, jdkn, jdb, jdk, jdr, idv = wkv_bwd(jw, jkn, jb, jk, jr, iv, to_i(dy), sp, sa_i)
    ddecay, dkn_w, db_w, dk2_w, dr_w, dv_w = from_j(jdw), from_j(jdkn), from_j(jdb), from_j(jdk), from_j(jdr), from_i(idv)

    def prep_bwd(k_, lw_, la_, dd, dkn_, db_, dk2a, dk2b, w0, a0, kkw, kaw):
        return vjp_fn(_prep, 7)(k_, lw_, la_, w0, a0, kkw, kaw, dd, dkn_, db_, dk2a + dk2b)

    dk_s, dlw, dla, gr['w0'], gr['a0'], gr['k_k'], gr['k_a'] = rowwise(
        prep_bwd, [(ps, LANE, hb), (lw, LANE, 0), (la, LANE, 0), (ddecay, LANE, 0), (dkn_w, LANE, 0), (db_w, LANE, 0),
                   (dk2_w, LANE, 0), (dk2_b, LANE, 0)],
        [par_d('w0'), par_d('a0'), par_d('k_k'), par_d('k_a')], [(D, LANE, F32), (D, LANE, BF16), (D, LANE, BF16)],
        [(1, D, LANE)] * 4, tm=t_128, ncb=hb, name="d_wkv_prep")
    d_wup = mm(txw, dlw, "tn", name="d_wup")
    dtxw = mm(dlw, w_up_p, "nt", name="d_txw")
    d_aup = mm(xas, dla, "tn", name="d_aup")
    dxa = mm(dla, a_up_p, "nt", name="d_xa")
    gr['g_up'] = mm(sg, dg, "tn", name="d_gup")
    dsg = mm(dg, g_up, "nt", name="d_sg")

    def lora_bwd(p_, prev_, dsg_, dt_, dxa_, mu_):
        return vjp_fn(_lora_act, 3)(p_, prev_, mu_, dsg_, dt_, dxa_)

    dpb, dprevb, dmu_b = rowwise(lora_bwd, [(proj, 512, O_L // 512), (prev_b, 512, 0), (dsg, G_LORA, 0), (dtxw, LANE, 0),
                                            (dxa, LANE, 0)], [(mu_b, 512, 0)], [(512, 512, F32)] * 2, [(1, 512, 512)],
                                 tm=t_512, name="d_lora_act")

    def shift_bwd2(p_, prev_, c1, c2, mu_):
        return vjp_fn(_shift, 3)(p_, prev_, mu_, c1 + c2)

    def shift_bwd1(p_, prev_, c1, mu_):
        return vjp_fn(_shift, 3)(p_, prev_, mu_, c1)

    def shift_back(sec, cts):
        nb = D // 512
        f = shift_bwd2 if len(cts) == 2 else shift_bwd1
        return rowwise(f, [(proj, 512, sec * nb), (prev_a, 512, sec * nb)] + [(c, 512, 0) for c in cts],
                       [(mu_a, 512, sec * nb)], [(D, 512, F32)] * 2, [(1, D, 512)], tm=t_512, ncb=nb, name=f"d_shift{sec}")

    dp_r, dprev_r, dmu_r = shift_back(0, [dr_w, dr_b])
    dp_k, dprev_k, dmu_k = shift_back(1, [dk_s])
    dp_v, dprev_v, dmu_v = shift_back(2, [dv_w, dv_b])

    def add_cast(p_, q_):
        return p_ + q_

    def dsec(dp_, dprev_, tag):
        C = dp_.shape[1]
        return rowwise(add_cast, [(dp_, 512, 0), (shift_up(dprev_), 512, 0)], [], [(C, 512, BF16)], tm=t_512, ncb=C // 512,
                       name="d_sec_" + tag)[0]

    zpad = jnp.zeros((R, NP - NP0), BF16)
    dproj = jnp.concatenate([dsec(dp_r, dprev_r, "r"), dsec(dp_k, dprev_k, "k"), dsec(dp_v, dprev_v, "v"), dga, dgb,
                             dsec(dpb, dprevb, "l"), dcq, dckv, dkpe, zpad], axis=1)
    d_win_p = mm(u, dproj, "tn", name="d_win")
    du = mm(dproj, win_p, "nt", name="d_u")

    def norm_bwd(h_, dn_, dh_, nw_):
        dh, dnw = vjp_fn(_rms, 2)(h_, nw_, dn_)
        return dh + dh_, dnw

    dh1, gr['mix_norm'] = rowwise(norm_bwd, [(h1, D, 0), (du, D, 0), (dh2, D, 0)], [(w2['mix_norm'], D, 0)],
                                  [(D, D, F32)], [(1, D, D)], tm=t_full, name="d_mix_norm")
    dh0, gr['ffn1_norm'], gr['ffn1_w_gate'], gr['ffn1_w_up'], gr['ffn1_w_down'] = ffn_bwd(
        dh1, sv1, w2['ffn1_norm'], stk['ffn1_w_gate'], stk['ffn1_w_up'], full['ffn1_w_down'], "ffn1")
    dh0_3 = rows3(dh0)
    grad_x = dh0_3[:, N_META:]
    gr['meta_tokens'] = jnp.sum(dh0_3[:, :N_META], axis=0)

    gr['w_in'] = jnp.concatenate([
        d_win_p[:, :3 * D], d_win_p[:, O_L + G_LORA:O_L + G_LORA + W_LORA], d_win_p[:, O_L + G_LORA + LANE:O_L + G_LORA + LANE + A_LORA],
        d_win_p[:, O_L:O_L + G_LORA], d_win_p[:, O_CQ:O_CQ + Q_LORA], d_win_p[:, O_CKV:O_CKV + KV_LORA],
        d_win_p[:, O_KPE:O_KPE + half], d_win_p[:, O_KPE + 2 * half:O_KPE + 3 * half], d_win_p[:, O_GA:O_GA + 2 * D]], axis=1)
    gr['tm_mu'] = jnp.concatenate([dmu_r, dmu_k, dmu_v, dmu_b[:, G_LORA:G_LORA + W_LORA],
                                   dmu_b[:, G_LORA + LANE:G_LORA + LANE + A_LORA], dmu_b[:, :G_LORA]], axis=1)
    gr['w_up'], gr['a_up'] = d_wup[:W_LORA], d_aup[:A_LORA]
    dq3n, dq3p = d_wqn.reshape(Q_LORA, Hm, LANE), d_wqp.reshape(Q_LORA, Hm, LANE)
    gr['w_uq'] = jnp.concatenate([dq3n, dq3p[:, :, :half], dq3p[:, :, 2 * half:3 * half]], axis=2).reshape(Q_LORA, Hm * QK_DIM)
    gr['w_ukv'] = jnp.concatenate([d_wkn.reshape(KV_LORA, Hm, LANE), d_wv.reshape(KV_LORA, Hm, LANE)], axis=2).reshape(
        KV_LORA, Hm * (NOPE_DIM + V_DIM))

    def stacked(g, axis):
        if g.ndim == 3:
            return g
        if axis == 0:
            return g.reshape(4, g.shape[0] // 4, g.shape[1])
        return g.reshape(g.shape[0], 4, g.shape[1] // 4).transpose(1, 0, 2)

    cidx = ci.reshape(1).astype(jnp.int32)
    gs = [stacked(gr[n], SHARD_AXIS[n]) for n in GATHERED]
    got = swap_halves(gs)
    hs = [add_half(g, r, cidx, "add_half_" + n) for n, g, r in zip(GATHERED, gs, got)]
    sc = jnp.stack([chip32, ci.astype(jnp.int32)])
    ts = [sum_chips(p, h, sc, "sum_chips_" + n) for n, p, h in zip(GATHERED, rs_chips(hs), hs)]
    g_shard = {n: z.reshape(w2[n].shape) for n, z in zip(GATHERED, share_sibling(ts))}
    small = jnp.concatenate([gr[n].reshape(-1) for n in SMALL] + [gr['meta_tokens'].reshape(-1), lossp.reshape(-1)])
    ns = small.shape[0]
    nsp = -(-ns // (8 * LANE)) * 8 * LANE
    small_sum = allreduce8(jnp.pad(small, (0, nsp - ns)).reshape(-1, LANE), "allreduce_small").reshape(-1)
    g_small = dict(zip(SMALL + ['meta_full'], _unpack(small_sum, [w2[n].shape for n in SMALL] + [(N_META, D)])))
    g_shard['meta_tokens'] = lax.dynamic_slice(
        g_small['meta_full'], (jnp.zeros((), jnp.int32), (chip * mcols).astype(jnp.int32)), (N_META, mcols))
    loss = small_sum[ns - LANE]

    grads, deltas, new_m, new_v = [], [], [], []
    for n in WEIGHTS:
        gw = g_shard[n] if n in g_shard else g_small[n]
        d_, m_, v_ = adamw(w2[n], gw, m2[n], v2[n], "adamw_" + n)
        shp = a[n].shape
        grads.append(gw.reshape(shp))
        deltas.append(d_.reshape(shp))
        new_m.append(m_.reshape(shp))
        new_v.append(v_.reshape(shp))
    return (loss, grad_x, *grads, *deltas, *new_m, *new_v)


def kernel(x, meta_tokens, ffn1_norm, ffn1_w_gate, ffn1_w_up, ffn1_w_down, mix_norm, w_in, tm_mu, w0, w_up, a0, a_up, g_up, k_k, k_a, r_k, gn_w, gn_b, q_norm, w_uq, kv_norm, w_ukv, w_out, ffn2_norm, ffn2_w_gate, ffn2_w_up, ffn2_w_down, final_norm, loss_target, m_meta_tokens, m_ffn1_norm, m_ffn1_w_gate, m_ffn1_w_up, m_ffn1_w_down, m_mix_norm, m_w_in, m_tm_mu, m_w0, m_w_up, m_a0, m_a_up, m_g_up, m_k_k, m_k_a, m_r_k, m_gn_w, m_gn_b, m_q_norm, m_w_uq, m_kv_norm, m_w_ukv, m_w_out, m_ffn2_norm, m_ffn2_w_gate, m_ffn2_w_up, m_ffn2_w_down, m_final_norm, v_meta_tokens, v_ffn1_norm, v_ffn1_w_gate, v_ffn1_w_up, v_ffn1_w_down, v_mix_norm, v_w_in, v_tm_mu, v_w0, v_w_up, v_a0, v_a_up, v_g_up, v_k_k, v_k_a, v_r_k, v_gn_w, v_gn_b, v_q_norm, v_w_uq, v_kv_norm, v_w_ukv, v_w_out, v_ffn2_norm, v_ffn2_w_gate, v_ffn2_w_up, v_ffn2_w_down, v_final_norm):
    return _step(dict(locals()))
```

```python
import functools
import math

import jax
import jax.numpy as jnp
import numpy as np
from jax import lax
from jax.experimental import pallas as pl
from jax.experimental.pallas import tpu as pltpu

F32 = jnp.float32
BF16 = jnp.bfloat16
MESH = pl.DeviceIdType.MESH

N_META = 16
NORM_EPS = 1e-6
RWKV_HEAD = 64
GN_EPS = RWKV_HEAD * 1e-5
W_LORA, A_LORA, G_LORA = 96, 96, 256
Q_LORA, KV_LORA = 512, 512
NOPE_DIM, ROPE_DIM, V_DIM = 128, 64, 128
QK_DIM = NOPE_DIM + ROPE_DIM
ROPE_THETA = 10000.0
Q_BLOCK = 128
ADAM_LR, ADAM_B1, ADAM_B2, ADAM_EPS, ADAM_WD, ADAM_STEP = 0.001, 0.9, 0.999, 1e-08, 0.01, 10

LANE = 128
VMEM_LIMIT = 56 * 1024 * 1024


def _pcall(body, **kw):
    return pl.pallas_call(body, **kw)


def _cparams(sem):
    return pltpu.CompilerParams(dimension_semantics=sem, vmem_limit_bytes=VMEM_LIMIT)


MM_LANE_TILE = 1536
MM_ROW_TILE = 1408


def _div_tile(n, cap, unit):
    best = None
    for t in range(unit, min(n, cap) + 1, unit):
        if n % t == 0:
            best = t
    return best if best is not None else n


def _rtile(n, pref=512):
    best = None
    for t in range(16, min(n, pref * 2) + 1, 16):
        if n % t == 0 and (best is None or abs(t - pref) < abs(best - pref)):
            best = t
    return best if best is not None else n


def mm(a, b, mode="nn", out_dtype=F32, res=None, alpha=1.0, b_stack=False, out_stack=False, name="mm"):
    cs = b.shape[-1] if b_stack else None
    bs = (b.shape[1], 4 * cs) if b_stack else b.shape
    if mode == "nn":
        (M, K), (K2, N) = a.shape, bs
    elif mode == "nt":
        (M, K), (N, K2) = a.shape, bs
    else:
        (K, M), (K2, N) = a.shape, bs
    assert K == K2, (a.shape, b.shape, mode)
    if mode == "tn":
        tm, tk = _div_tile(M, MM_ROW_TILE, LANE), _div_tile(K, 1024, 16)
    else:
        tm = _div_tile(M, MM_ROW_TILE, 16)
        tk = _div_tile(cs if (b_stack and mode == "nt") else K, MM_LANE_TILE, LANE)
    ncol = N // 4 if out_stack else (cs if (b_stack and mode == "nn") else N)
    tn = _div_tile(ncol, MM_LANE_TILE if mode != "nt" else 1024, LANE)
    nk = K // tk
    dims = {"nn": (((1,), (0,)), ((), ())), "nt": (((1,), (1,)), ((), ())), "tn": (((0,), (0,)), ((), ()))}[mode]
    direct = out_dtype == F32

    def body(*refs):
        a_ref, b_ref = refs[:2]
        r_ref = refs[2] if res is not None else None
        o_ref = refs[3] if res is not None else refs[2]
        acc = o_ref if direct else refs[-1]
        k = pl.program_id(2)

        @pl.when(k == 0)
        def _():
            acc[...] = jnp.zeros_like(acc) if res is None else r_ref[...].astype(F32)

        p = lax.dot_general(a_ref[...].astype(BF16), b_ref[...].astype(BF16), dims, preferred_element_type=F32)
        acc[...] += p if alpha == 1.0 else alpha * p

        if not direct:
            @pl.when(k == nk - 1)
            def _():
                o_ref[...] = acc[...].astype(o_ref.dtype)

    if mode == "tn":
        a_spec = pl.BlockSpec((tk, tm), lambda i, j, k: (k, i))
        b_spec = pl.BlockSpec((tk, tn), lambda i, j, k: (k, j))
    else:
        a_spec = pl.BlockSpec((tm, tk), lambda i, j, k: (i, k))
        if mode == "nn":
            if b_stack:
                nps = cs // tn
                b_spec = pl.BlockSpec((None, tk, tn), lambda i, j, k: (j // nps, k, j % nps))
            else:
                b_spec = pl.BlockSpec((tk, tn), lambda i, j, k: (k, j))
        elif b_stack:
            kps = cs // tk
            b_spec = pl.BlockSpec((None, tn, tk), lambda i, j, k: (k // kps, j, k % kps))
        else:
            b_spec = pl.BlockSpec((tn, tk), lambda i, j, k: (j, k))
    r_spec = pl.BlockSpec((tm, tn), lambda i, j, k: (i, j))
    if out_stack:
        ops = (N // 4) // tn
        o_spec = pl.BlockSpec((None, tm, tn), lambda i, j, k: (j // ops, i, j % ops))
        o_shape = jax.ShapeDtypeStruct((4, M, N // 4), out_dtype)
    else:
        o_spec, o_shape = r_spec, jax.ShapeDtypeStruct((M, N), out_dtype)
    in_specs = [a_spec, b_spec] + ([r_spec] if res is not None else [])
    args = (a, b) + ((res,) if res is not None else ())
    return _pcall(
        body, name=name, grid=(M // tm, N // tn, nk), in_specs=in_specs, out_specs=o_spec, out_shape=o_shape,
        scratch_shapes=[] if direct else [pltpu.VMEM((tm, tn), F32)],
        compiler_params=_cparams(("parallel", "parallel", "arbitrary")),
    )(*args)


MM_EPI_ROW_TILE = 704


def mm_epi(a, bs, mode, epi, extras, out_dtypes, b_stack=False, name="mm_epi"):
    b = bs[0]
    cs = b.shape[-1] if b_stack else None
    bshape = (b.shape[1], 4 * cs) if b_stack else b.shape
    (M, K) = a.shape
    (K2, N) = bshape if mode == "nn" else bshape[::-1]
    assert K == K2 and mode in ("nn", "nt"), (a.shape, b.shape, mode)
    tm = _div_tile(M, MM_EPI_ROW_TILE, 16)
    tk = _div_tile(cs if (b_stack and mode == "nt") else K, MM_LANE_TILE, LANE)
    tn = _div_tile(cs if (b_stack and mode == "nn") else N, MM_LANE_TILE, LANE)
    nk, nb, ne, no = K // tk, len(bs), len(extras), len(out_dtypes)
    dims = (((1,), (0,)), ((), ())) if mode == "nn" else (((1,), (1,)), ((), ()))

    def body(*refs):
        a_ref, b_refs, e_refs = refs[0], refs[1:1 + nb], refs[1 + nb:1 + nb + ne]
        o_refs, accs = refs[1 + nb + ne:1 + nb + ne + no], refs[1 + nb + ne + no:]
        k = pl.program_id(2)

        @pl.when(k == 0)
        def _():
            for acc in accs:
                acc[...] = jnp.zeros_like(acc)

        av = a_ref[...].astype(BF16)
        for b_ref, acc in zip(b_refs, accs):
            acc[...] += lax.dot_general(av, b_ref[...].astype(BF16), dims, preferred_element_type=F32)

        @pl.when(k == nk - 1)
        def _():
            res = epi(*[acc[...] for acc in accs], *[e[...] for e in e_refs])
            for o_ref, v in zip(o_refs, res):
                o_ref[...] = v.astype(o_ref.dtype)

    a_spec = pl.BlockSpec((tm, tk), lambda i, j, k: (i, k))
    if mode == "nn":
        if b_stack:
            nps = cs // tn
            b_spec = pl.BlockSpec((None, tk, tn), lambda i, j, k: (j // nps, k, j % nps))
        else:
            b_spec = pl.BlockSpec((tk, tn), lambda i, j, k: (k, j))
    elif b_stack:
        kps = cs // tk
        b_spec = pl.BlockSpec((None, tn, tk), lambda i, j, k: (k // kps, j, k % kps))
    else:
        b_spec = pl.BlockSpec((tn, tk), lambda i, j, k: (j, k))
    o_spec = pl.BlockSpec((tm, tn), lambda i, j, k: (i, j))
    return _pcall(
        body, name=name, grid=(M // tm, N // tn, nk), in_specs=[a_spec] + [b_spec] * nb + [o_spec] * ne,
        out_specs=[o_spec] * no, out_shape=[jax.ShapeDtypeStruct((M, N), dt) for dt in out_dtypes],
        scratch_shapes=[pltpu.VMEM((tm, tn), F32)] * nb,
        compiler_params=_cparams(("parallel", "parallel", "arbitrary")),
    )(a, *bs, *extras)


def rowwise(fn, row_ins, par_ins, outs, accs=(), *, tm, ncb=1, name="rowwise"):
    R = row_ins[0][0].shape[0]
    assert R % tm == 0, (R, tm)
    nrb = R // tm
    in_specs, args = [], []
    for spec in row_ins:
        arr, w, base = spec[:3]
        mod = spec[3] if len(spec) > 3 else None
        cstep = 0 if (len(spec) > 4 and spec[4]) else 1
        if mod is None:
            in_specs.append(pl.BlockSpec((tm, w), lambda j, i, base=base, cstep=cstep: (i, base + cstep * j)))
        else:
            in_specs.append(pl.BlockSpec((tm, w), lambda j, i, base=base, mod=mod, cstep=cstep: (i % mod, base + cstep * j)))
        args.append(arr)
    for arr, w, base in par_ins:
        in_specs.append(pl.BlockSpec((arr.shape[0], w), lambda j, i, base=base: (0, base + j)))
        args.append(arr)
    out_specs, out_shape = [], []
    for cols, w, dt in outs:
        out_specs.append(pl.BlockSpec((tm, w), lambda j, i: (i, j)))
        out_shape.append(jax.ShapeDtypeStruct((R, cols), dt))
    for p, cols, w in accs:
        out_specs.append(pl.BlockSpec((p, w), lambda j, i: (0, j)))
        out_shape.append(jax.ShapeDtypeStruct((p, cols), F32))
    nin, nout, nacc = len(args), len(outs), len(accs)

    def body(*refs):
        vals = [r[...] for r in refs[:nin]]
        res = fn(*vals)
        if not isinstance(res, (tuple, list)):
            res = (res,)
        assert len(res) == nout + nacc, (len(res), nout, nacc)
        for o_ref, v in zip(refs[nin:nin + nout], res[:nout]):
            o_ref[...] = v.astype(o_ref.dtype)
        if nacc:
            i = pl.program_id(1)

            @pl.when(i == 0)
            def _():
                for a_ref in refs[nin + nout:]:
                    a_ref[...] = jnp.zeros_like(a_ref)

            for a_ref, v in zip(refs[nin + nout:], res[nout:]):
                a_ref[...] += v.astype(F32)

    r = _pcall(
        body, name=name, grid=(ncb, nrb), in_specs=in_specs, out_specs=out_specs, out_shape=out_shape,
        compiler_params=_cparams(("parallel", "arbitrary")),
    )(*args)
    return r


def vjp_fn(fwd, nprim):
    def f(*vals):
        prim, cts = vals[:nprim], vals[nprim:]
        out, pull = jax.vjp(fwd, *[p.astype(F32) for p in prim])
        if not isinstance(out, (tuple, list)):
            cts = cts[0].astype(F32)
        else:
            cts = tuple(c.astype(F32) for c in cts)
        return pull(cts)
    return f


WKV_TB = 8


def wkv_fwd(w, kn, b, k, r, v):
    T, N, L = w.shape
    NI = v.shape[1]
    tb = WKV_TB
    assert T % tb == 0 and L == v.shape[2]

    def body(w_ref, kn_ref, b_ref, k_ref, r_ref, v_ref, y_ref, sp_ref, sa_ref, s_ref):
        @pl.when(pl.program_id(0) == 0)
        def _():
            s_ref[...] = jnp.zeros_like(s_ref)

        def step(s, carry):
            W, KN, B, Kk, Rr = w_ref[s], kn_ref[s], b_ref[s], k_ref[s], r_ref[s]
            for i in range(NI):
                S = s_ref[i]
                sp_ref[s, i] = S
                sa = jnp.sum(S * KN, axis=0, keepdims=True)
                sa_ref[s, pl.ds(i, 1), :] = sa
                vi = v_ref[s, pl.ds(i, 1), :]
                Sn = S * W + sa * B + vi * Kk
                s_ref[i] = Sn
                y_ref[s, pl.ds(i, 1), :] = jnp.sum(Sn * Rr, axis=0, keepdims=True)
            return carry

        lax.fori_loop(0, tb, step, 0)

    jspec = pl.BlockSpec((tb, N, L), lambda t: (t, 0, 0))
    ispec = pl.BlockSpec((tb, NI, L), lambda t: (t, 0, 0))
    ishape = jax.ShapeDtypeStruct((T, NI, L), F32)
    return _pcall(
        body, name="wkv_fwd", grid=(T // tb,), in_specs=[jspec] * 5 + [ispec],
        out_specs=[ispec, pl.BlockSpec((tb, NI, N, L), lambda t: (t, 0, 0, 0)), ispec],
        out_shape=[ishape, jax.ShapeDtypeStruct((T, NI, N, L), F32), ishape],
        scratch_shapes=[pltpu.VMEM((NI, N, L), F32)],
        compiler_params=_cparams(("arbitrary",)),
    )(w, kn, b, k, r, v)


def wkv_bwd(w, kn, b, k, r, v, dy, sp, sa):
    T, N, L = w.shape
    NI, LH = v.shape[1], L // 2
    tb = WKV_TB
    nt = T // tb

    def body(w_ref, kn_ref, b_ref, k_ref, r_ref, v_ref, dy_ref, sp_ref, sa_ref,
             dw_ref, dkn_ref, db_ref, dk_ref, dr_ref, dv_ref, ds_ref):
        @pl.when(pl.program_id(0) == 0)
        def _():
            ds_ref[...] = jnp.zeros_like(ds_ref)

        def step(q, carry):
            s = tb - 1 - q
            W, KN, B, Kk, Rr = w_ref[s], kn_ref[s], b_ref[s], k_ref[s], r_ref[s]
            dW = jnp.zeros((N, L), F32)
            dKN, dB, dK, T1 = dW, dW, dW, dW
            al = jnp.zeros((1, L), F32)
            be = al
            for i in range(NI):
                Sp = sp_ref[s, i]
                vi = v_ref[s, pl.ds(i, 1), :]
                dyi = dy_ref[s, pl.ds(i, 1), :]
                sai = sa_ref[s, pl.ds(i, 1), :]
                dS = ds_ref[i] + dyi * Rr
                T1 = T1 + Sp * dyi
                al = al + sai * dyi
                be = be + vi * dyi
                dv_ref[s, pl.ds(i, 1), :] = jnp.sum(dS * Kk, axis=0, keepdims=True)
                dK = dK + dS * vi
                dsa = jnp.sum(dS * B, axis=0, keepdims=True)
                dB = dB + dS * sai
                dW = dW + dS * Sp
                dKN = dKN + Sp * dsa
                ds_ref[i] = dS * W + dsa * KN
            dR = W * T1 + B * al + Kk * be
            for ref, val in ((dw_ref, dW), (dkn_ref, dKN), (db_ref, dB), (dk_ref, dK), (dr_ref, dR)):
                ref[s] = (val + pltpu.roll(val, LH, 1))[:, :LH]
            return carry

        lax.fori_loop(0, tb, step, 0)

    jspec = pl.BlockSpec((tb, N, L), lambda t: (nt - 1 - t, 0, 0))
    gspec = pl.BlockSpec((tb, N, LH), lambda t: (nt - 1 - t, 0, 0))
    ispec = pl.BlockSpec((tb, NI, L), lambda t: (nt - 1 - t, 0, 0))
    gshape = jax.ShapeDtypeStruct((T, N, LH), F32)
    return _pcall(
        body, name="wkv_bwd", grid=(nt,),
        in_specs=[jspec] * 5 + [ispec, ispec, pl.BlockSpec((tb, NI, N, L), lambda t: (nt - 1 - t, 0, 0, 0)), ispec],
        out_specs=[gspec] * 5 + [ispec], out_shape=[gshape] * 5 + [jax.ShapeDtypeStruct((T, NI, L), F32)],
        scratch_shapes=[pltpu.VMEM((NI, N, L), F32)],
        compiler_params=_cparams(("arbitrary",)),
    )(w, kn, b, k, r, v, dy, sp, sa)


_NT = (((1,), (1,)), ((), ()))
_TN = (((0,), (0,)), ((), ()))
ATT_SCALE = QK_DIM ** -0.5


def _att_probs(q1, q2, kn_ref, kp_ref, i, L):
    s = lax.dot_general(q1, kn_ref[0, :L, :], _NT, preferred_element_type=F32)
    s = s + lax.dot_general(q2, kp_ref[0, :L, :], _NT, preferred_element_type=F32)
    s = s * ATT_SCALE
    col = lax.broadcasted_iota(jnp.int32, (Q_BLOCK, L), 1)
    row = lax.broadcasted_iota(jnp.int32, (Q_BLOCK, L), 0)
    valid = (col < N_META) | ((col >= Q_BLOCK) & (col - Q_BLOCK <= Q_BLOCK * i + row))
    s = jnp.where(valid, s, -1e30)
    m = jnp.max(s, axis=-1, keepdims=True)
    p = jnp.exp(s - m)
    return p / jnp.sum(p, axis=-1, keepdims=True)


def attn_fwd(qn, qp, kn, kp, v):
    B, S, HD = qn.shape
    SP = kn.shape[1]
    H = HD // LANE
    nq = S // Q_BLOCK

    def body(qn_ref, qp_ref, kn_ref, kp_ref, v_ref, o_ref):
        for i in range(nq):
            L = Q_BLOCK * (i + 2)
            rows = pl.ds(Q_BLOCK * i, Q_BLOCK)
            p = _att_probs(qn_ref[0, rows, :], qp_ref[0, rows, :], kn_ref, kp_ref, i, L)
            o_ref[0, rows, :] = jnp.dot(p.astype(BF16), v_ref[0, :L, :], preferred_element_type=F32)

    qspec = pl.BlockSpec((1, S, LANE), lambda b, h: (b, 0, h))
    kspec = pl.BlockSpec((1, SP, LANE), lambda b, h: (b, 0, h))
    pspec = pl.BlockSpec((1, SP, LANE), lambda b, h: (b, 0, 0))
    return _pcall(
        body, name="attn_fwd", grid=(B, H), in_specs=[qspec, qspec, kspec, pspec, kspec], out_specs=qspec,
        out_shape=jax.ShapeDtypeStruct((B, S, HD), F32), compiler_params=_cparams(("parallel", "parallel")),
    )(qn, qp, kn, kp, v)


def attn_bwd(qn, qp, kn, kp, v, do):
    B, S, HD = qn.shape
    SP = kn.shape[1]
    H = HD // LANE
    nq = S // Q_BLOCK

    def body(qn_ref, qp_ref, kn_ref, kp_ref, v_ref, do_ref, dqn_ref, dqp_ref, dkn_ref, dv_ref, dkp_ref, dkn_acc, dv_acc):
        @pl.when(pl.program_id(1) == 0)
        def _():
            dkp_ref[...] = jnp.zeros_like(dkp_ref)

        dkn_acc[...] = jnp.zeros_like(dkn_acc)
        dv_acc[...] = jnp.zeros_like(dv_acc)
        for i in range(nq):
            L = Q_BLOCK * (i + 2)
            rows = pl.ds(Q_BLOCK * i, Q_BLOCK)
            q1, q2, do_i = qn_ref[0, rows, :], qp_ref[0, rows, :], do_ref[0, rows, :]
            p = _att_probs(q1, q2, kn_ref, kp_ref, i, L)
            dp = lax.dot_general(do_i, v_ref[0, :L, :], _NT, preferred_element_type=F32)
            ds = (p * (dp - jnp.sum(p * dp, axis=-1, keepdims=True)) * ATT_SCALE).astype(BF16)
            dqn_ref[0, rows, :] = jnp.dot(ds, kn_ref[0, :L, :], preferred_element_type=F32).astype(dqn_ref.dtype)
            dqp_ref[0, rows, :] = jnp.dot(ds, kp_ref[0, :L, :], preferred_element_type=F32)
            dkn_acc[:L, :] += lax.dot_general(ds, q1, _TN, preferred_element_type=F32)
            dkp_ref[0, :L, :] += lax.dot_general(ds, q2, _TN, preferred_element_type=F32)
            dv_acc[:L, :] += lax.dot_general(p.astype(BF16), do_i, _TN, preferred_element_type=F32)
        dkn_ref[0] = dkn_acc[...].astype(dkn_ref.dtype)
        dv_ref[0] = dv_acc[...].astype(dv_ref.dtype)

    qspec = pl.BlockSpec((1, S, LANE), lambda b, h: (b, 0, h))
    kspec = pl.BlockSpec((1, SP, LANE), lambda b, h: (b, 0, h))
    pspec = pl.BlockSpec((1, SP, LANE), lambda b, h: (b, 0, 0))
    return _pcall(
        body, name="attn_bwd", grid=(B, H), in_specs=[qspec, qspec, kspec, pspec, kspec, qspec],
        out_specs=[qspec, qspec, kspec, kspec, pspec],
        out_shape=[jax.ShapeDtypeStruct((B, S, HD), BF16), jax.ShapeDtypeStruct((B, S, HD), F32),
                   jax.ShapeDtypeStruct((B, SP, HD), BF16), jax.ShapeDtypeStruct((B, SP, HD), BF16),
                   jax.ShapeDtypeStruct((B, SP, LANE), F32)],
        scratch_shapes=[pltpu.VMEM((SP, LANE), F32), pltpu.VMEM((SP, LANE), F32)],
        compiler_params=_cparams(("parallel", "arbitrary")),
    )(qn, qp, kn, kp, v, do)


_HBM = pl.BlockSpec(memory_space=pltpu.HBM)


def _place():
    x, y, c = lax.axis_index("x"), lax.axis_index("y"), lax.axis_index("c")
    chips = [(1 - x, y), (x, 1 - y), (1 - x, 1 - y)]
    return x, y, c, chips


def _rcopy(src, dst, ssem, rsem, dev):
    return pltpu.make_async_remote_copy(src_ref=src, dst_ref=dst, send_sem=ssem, recv_sem=rsem,
                                        device_id=dev, device_id_type=MESH)


def _half_rows(c, r):
    return pl.ds(pl.multiple_of(c * (r // 2), 16), r // 2)


def ag_weights(bufs):
    n = len(bufs)

    def body(*refs):
        w, out = refs[:n], refs[n:2 * n]
        ssem, rsem = refs[2 * n:]
        x, y, c, chips = _place()
        s = 2 * x + y
        rows = [_half_rows(c, w[i].shape[1]) for i in range(n)]
        orows = [_half_rows(1 - c, w[i].shape[1]) for i in range(n)]
        first = [_rcopy(w[i].at[s, rows[i]], out[i].at[s, rows[i]], ssem.at[6 * i + j], rsem.at[6 * i + j], (px, py, c))
                 for i in range(n) for j, (px, py) in enumerate(chips)]
        for cp in first:
            cp.start()
        passed = []
        for j, (px, py) in enumerate(chips):
            sp = 2 * px + py
            for i in range(n):
                here = out[i].at[sp, rows[i]]
                _rcopy(here, here, ssem.at[6 * i + j], rsem.at[6 * i + j], (px, py, c)).wait_recv()
                fw = _rcopy(here, here, ssem.at[6 * i + 3 + j], rsem.at[6 * i + 3 + j], (x, y, 1 - c))
                fw.start()
                passed.append(fw)
        for j, (px, py) in enumerate(chips):
            sp = 2 * px + py
            for i in range(n):
                there = out[i].at[sp, orows[i]]
                _rcopy(there, there, ssem.at[6 * i + 3 + j], rsem.at[6 * i + 3 + j], (x, y, 1 - c)).wait_recv()
        for cp in first + passed:
            cp.wait_send()

    return _pcall(
        body, name="ag_weights", in_specs=[_HBM] * n, out_specs=[_HBM] * n,
        out_shape=[jax.ShapeDtypeStruct(w.shape, w.dtype) for w in bufs],
        input_output_aliases={i: i for i in range(n)},
        scratch_shapes=[pltpu.SemaphoreType.DMA((6 * n,)), pltpu.SemaphoreType.DMA((6 * n,))],
    )(*bufs)


def swap_halves(gs):
    n = len(gs)

    def body(*refs):
        g, out = refs[:n], refs[n:2 * n]
        ssem, rsem = refs[2 * n:]
        x, y, c, _ = _place()
        cps = [_rcopy(g[i].at[:, _half_rows(1 - c, g[i].shape[1])], out[i], ssem.at[i], rsem.at[i], (x, y, 1 - c))
               for i in range(n)]
        for cp in cps:
            cp.start()
        for cp in cps:
            cp.wait()

    return _pcall(
        body, name="swap_halves", in_specs=[_HBM] * n, out_specs=[_HBM] * n,
        out_shape=[jax.ShapeDtypeStruct((4, g.shape[1] // 2, g.shape[2]), g.dtype) for g in gs],
        scratch_shapes=[pltpu.SemaphoreType.DMA((n,)), pltpu.SemaphoreType.DMA((n,))],
    )(*gs)


def rs_chips(hs):
    n = len(hs)

    def body(*refs):
        h, out = refs[:n], refs[n:2 * n]
        ssem, rsem = refs[2 * n:]
        x, y, c, chips = _place()
        s = 2 * x + y
        cps = [_rcopy(h[i].at[2 * px + py], out[i].at[s], ssem.at[3 * i + j], rsem.at[3 * i + j], (px, py, c))
               for i in range(n) for j, (px, py) in enumerate(chips)]
        for cp in cps:
            cp.start()
        for j, (px, py) in enumerate(chips):
            for i in range(n):
                _rcopy(h[i].at[s], out[i].at[2 * px + py], ssem.at[3 * i + j], rsem.at[3 * i + j], (px, py, c)).wait_recv()
        for cp in cps:
            cp.wait_send()

    return _pcall(
        body, name="rs_chips", in_specs=[_HBM] * n, out_specs=[_HBM] * n,
        out_shape=[jax.ShapeDtypeStruct(h.shape, h.dtype) for h in hs],
        scratch_shapes=[pltpu.SemaphoreType.DMA((3 * n,)), pltpu.SemaphoreType.DMA((3 * n,))],
    )(*hs)


def share_sibling(ts):
    n = len(ts)

    def body(*refs):
        t, out = refs[:n], refs[n:2 * n]
        ssem, rsem = refs[2 * n:]
        x, y, c, _ = _place()
        cps = [_rcopy(t[i].at[c], out[i].at[c], ssem.at[i], rsem.at[i], (x, y, 1 - c)) for i in range(n)]
        for cp in cps:
            cp.start()
        for i in range(n):
            _rcopy(t[i].at[c], out[i].at[1 - c], ssem.at[i], rsem.at[i], (x, y, 1 - c)).wait_recv()
        for cp in cps:
            cp.wait_send()

    return _pcall(
        body, name="share_sibling", in_specs=[_HBM] * n, out_specs=[_HBM] * n,
        out_shape=[jax.ShapeDtypeStruct(t.shape, t.dtype) for t in ts],
        input_output_aliases={i: i for i in range(n)},
        scratch_shapes=[pltpu.SemaphoreType.DMA((n,)), pltpu.SemaphoreType.DMA((n,))],
    )(*ts)


def allreduce8(v, name):
    P, W = v.shape

    def body(v_ref, out_ref, buf, ssem, rsem):
        x, y, c, _ = _place()
        me = 4 * x + 2 * y + c
        buf[me] = v_ref[...]
        cps = []
        for k in range(1, 8):
            px = 1 - x if k & 4 else x
            py = 1 - y if k & 2 else y
            pc = 1 - c if k & 1 else c
            cp = _rcopy(buf.at[me], buf.at[me], ssem.at[k - 1], rsem.at[k - 1], (px, py, pc))
            cp.start()
            cps.append((cp, 4 * px + 2 * py + pc))
        for k, (cp, peer) in enumerate(cps):
            _rcopy(buf.at[me], buf.at[peer], ssem.at[k], rsem.at[k], (x, y, c)).wait_recv()
        for cp, _ in cps:
            cp.wait_send()
        acc = buf[0]
        for d in range(1, 8):
            acc = acc + buf[d]
        out_ref[...] = acc

    return _pcall(
        body, name=name, in_specs=[pl.BlockSpec(memory_space=pltpu.VMEM)],
        out_specs=pl.BlockSpec(memory_space=pltpu.VMEM), out_shape=jax.ShapeDtypeStruct((P, W), F32),
        scratch_shapes=[pltpu.VMEM((8, P, W), F32), pltpu.SemaphoreType.DMA((7,)), pltpu.SemaphoreType.DMA((7,))],
    )(v)


def add_half(g, rcv, cidx, name):
    _, hr, W = rcv.shape
    tr = _rows_tile(hr, W)
    nb = hr // tr

    def body(c_ref, g_ref, r_ref, o_ref):
        o_ref[...] = (g_ref[...] + r_ref[...]).astype(o_ref.dtype)

    return _pcall(
        body, name=name,
        grid_spec=pltpu.PrefetchScalarGridSpec(
            num_scalar_prefetch=1, grid=(4, nb),
            in_specs=[pl.BlockSpec((1, tr, W), lambda s, i, c: (s, c[0] * nb + i, 0)),
                      pl.BlockSpec((1, tr, W), lambda s, i, c: (s, i, 0))],
            out_specs=pl.BlockSpec((1, tr, W), lambda s, i, c: (s, i, 0))),
        out_shape=jax.ShapeDtypeStruct((4, hr, W), BF16), compiler_params=_cparams(("parallel", "parallel")),
    )(cidx, g, rcv)


def sum_chips(p, h, sc, name):
    _, hr, W = p.shape
    tr = _rows_tile(hr, W)

    def body(s_ref, c_ref, p_ref, h_ref, o_ref):
        s = s_ref[0]
        own = h_ref[0]
        f = lambda k: jnp.where(s == k, own, p_ref[k]).astype(F32)
        o_ref[0] = ((f(0) + f(1)) + f(2)) + f(3)

    return _pcall(
        body, name=name,
        grid_spec=pltpu.PrefetchScalarGridSpec(
            num_scalar_prefetch=2, grid=(hr // tr,),
            in_specs=[pl.BlockSpec((4, tr, W), lambda i, s, c: (0, i, 0)),
                      pl.BlockSpec((1, tr, W), lambda i, s, c: (s[0], i, 0))],
            out_specs=pl.BlockSpec((1, tr, W), lambda i, s, c: (c[0], i, 0))),
        out_shape=jax.ShapeDtypeStruct((2, hr, W), F32), compiler_params=_cparams(("parallel",)),
    )(sc[0], sc[1], p, h)


HI = lax.Precision.HIGHEST
BLOCK_BYTES = 3 * 512 * 1024


def _rows_tile(R, w, T=None):
    cands = [t for t in range(16, R + 1, 16) if R % t == 0 and (T is None or T % t == 0)]
    ok = [t for t in cands if t * w * 4 <= BLOCK_BYTES]
    return max(ok) if ok else min(cands)


def _gsum(x):
    r = jnp.right_shift(lax.broadcasted_iota(jnp.int32, (LANE, LANE), 0), 6)
    c = jnp.right_shift(lax.broadcasted_iota(jnp.int32, (LANE, LANE), 1), 6)
    return jnp.dot(x, (r == c).astype(F32), precision=HI, preferred_element_type=F32)


def _rms(x, g):
    return x * lax.rsqrt(jnp.mean(x * x, axis=-1, keepdims=True) + NORM_EPS) * g


def _silu_mul(gate, up):
    return jax.nn.silu(gate) * up


def _shift(p, prev, mu):
    return p + mu * (prev - p)


def _lora_act(p, prev, mu):
    s = _shift(p, prev, mu)
    return jax.nn.sigmoid(s[:, :G_LORA]), jnp.tanh(s[:, G_LORA:G_LORA + LANE]), s[:, G_LORA + LANE:]


def _prep(k, lw, la, w0, a0, kk_w, ka_w):
    wpre = -jax.nn.softplus(-(w0 + lw)) - 0.5
    decay = jnp.exp(-jnp.exp(wpre))
    a = jax.nn.sigmoid(a0 + la)
    kk = k * kk_w
    kk = kk * lax.rsqrt(jnp.maximum(_gsum(kk * kk), 1e-24))
    k2 = k * (1.0 + (a - 1.0) * ka_w)
    return decay, -kk, kk * a, k2


def _post(y, r, k2, v, g, gnw, gnb, rk):
    mean = _gsum(y) * (1.0 / RWKV_HEAD)
    d = y - mean
    var = _gsum(d * d) * (1.0 / RWKV_HEAD)
    yn = d * lax.rsqrt(var + GN_EPS) * gnw + gnb
    bonus = _gsum(r * k2 * rk) * v
    return (yn + bonus) * g


def _gate_mix(ga, gb, ya, o):
    return jax.nn.sigmoid(ga) * ya + jax.nn.sigmoid(gb) * o


def _rope(x, cos, sin):
    return x * cos + pltpu.roll(x, LANE // 2, 1) * sin


def _rope_t(dy, cos, sin):
    return dy * cos + pltpu.roll(dy * sin, LANE // 2, 1)


def _mla_pre(cq, ckv, kpe, cos, sin, qw, kvw):
    return _rms(cq, qw), _rms(ckv, kvw), _rope(kpe, cos, sin)


def _mla_pre_bwd(cq, ckv, dcqn, dckvn, dkr, cos, sin, qw, kvw):
    _, pull = jax.vjp(lambda a, b, c, d: (_rms(a, c), _rms(b, d)), cq, ckv, qw, kvw)
    dcq, dckv, dqw, dkvw = pull((dcqn, dckvn))
    return dcq, dckv, _rope_t(dkr, cos, sin), dqw, dkvw


WEIGHTS = ['meta_tokens', 'ffn1_norm', 'ffn1_w_gate', 'ffn1_w_up', 'ffn1_w_down', 'mix_norm', 'w_in', 'tm_mu', 'w0',
           'w_up', 'a0', 'a_up', 'g_up', 'k_k', 'k_a', 'r_k', 'gn_w', 'gn_b', 'q_norm', 'w_uq', 'kv_norm', 'w_ukv',
           'w_out', 'ffn2_norm', 'ffn2_w_gate', 'ffn2_w_up', 'ffn2_w_down', 'final_norm']
SHARD_AXIS = {'meta_tokens': 1, 'ffn1_w_gate': 1, 'ffn1_w_up': 1, 'ffn1_w_down': 0, 'w_in': 1, 'w_up': 1, 'a_up': 1,
              'g_up': 1, 'w_uq': 1, 'w_ukv': 1, 'w_out': 0, 'ffn2_w_gate': 1, 'ffn2_w_up': 1, 'ffn2_w_down': 0}
GATHERED = [n for n in WEIGHTS if n in SHARD_AXIS and n != 'meta_tokens']
FFN_IN = ('ffn1_w_gate', 'ffn1_w_up', 'ffn2_w_gate', 'ffn2_w_up')
SMALL = [n for n in WEIGHTS if n not in SHARD_AXIS]


def _to2d(a):
    if a.ndim == 1:
        return a.reshape(1, -1)
    if a.ndim == 3:
        return a.reshape(a.shape[0] * a.shape[1], a.shape[2]) if a.shape[0] == 1 and a.shape[1] > 64 else a.reshape(1, -1)
    return a


def _unpack(flat, shapes):
    out, off = [], 0
    for shp in shapes:
        n = shp[0] * shp[1]
        out.append(flat[off:off + n].reshape(shp))
        off += n
    return out


def _adamw(w, g, m, v):
    m = ADAM_B1 * m + (1.0 - ADAM_B1) * g
    v = ADAM_B2 * v + (1.0 - ADAM_B2) * jnp.square(g)
    m_hat = m / (1.0 - ADAM_B1 ** ADAM_STEP)
    v_hat = v / (1.0 - ADAM_B2 ** ADAM_STEP)
    delta = -ADAM_LR * (m_hat / (jnp.sqrt(v_hat) + ADAM_EPS) + ADAM_WD * w)
    return delta, m, v


def adamw(w, g, m, v, name):
    R, C = w.shape
    if R % 16 == 0 and R > 16:
        tm = _rows_tile(R, C)
    else:
        tm = R
    return rowwise(_adamw, [(w, C, 0), (g, C, 0), (m, C, 0), (v, C, 0)], [], [(C, C, F32)] * 3, tm=tm, name=name)


def _step(a):
    x = a['x']
    Bl, S, D = x.shape
    T = S + N_META
    R, RS = Bl * T, Bl * S
    Hr, Hm = D // RWKV_HEAD, D // LANE
    w2 = {n: _to2d(a[n]) for n in WEIGHTS}
    m2 = {n: _to2d(a['m_' + n]) for n in WEIGHTS}
    v2 = {n: _to2d(a['v_' + n]) for n in WEIGHTS}
    xi, yi, ci = lax.axis_index("x"), lax.axis_index("y"), lax.axis_index("c")
    chip = 2 * xi + yi

    i0 = jnp.zeros((), jnp.int32)
    chip32 = chip.astype(jnp.int32)

    def own_slot(w):
        return lax.dynamic_update_slice(jnp.zeros((4,) + w.shape, BF16), w.astype(BF16)[None], (chip32, i0, i0))

    stk = dict(zip(GATHERED, ag_weights([own_slot(w2[n]) for n in GATHERED])))

    def unstack(z, axis):
        return z.reshape(4 * z.shape[1], z.shape[2]) if axis == 0 else jnp.concatenate([z[s] for s in range(4)], axis=1)

    full = {n: unstack(stk[n], SHARD_AXIS[n]) for n in GATHERED if n not in FFN_IN}
    mt = w2['meta_tokens']
    mcols = mt.shape[1]
    mt_z = lax.dynamic_update_slice(jnp.zeros((N_META, D), F32), 0.5 * mt, (jnp.zeros((), jnp.int32), (chip * mcols).astype(jnp.int32)))
    meta_full = allreduce8(mt_z.reshape(-1, LANE), "gather_meta").reshape(N_META, D)

    F = 4 * stk['ffn1_w_gate'].shape[2]
    win = full['w_in']
    o = 3 * D
    c_xw, c_xa, c_xg = win[:, o:o + W_LORA], win[:, o + W_LORA:o + 2 * W_LORA], win[:, o + 2 * W_LORA:o + 2 * W_LORA + G_LORA]
    o += 2 * W_LORA + G_LORA
    c_cq, c_ckv, c_kpe = win[:, o:o + Q_LORA], win[:, o + Q_LORA:o + Q_LORA + KV_LORA], win[:, o + Q_LORA + KV_LORA:o + Q_LORA + KV_LORA + ROPE_DIM]
    o += Q_LORA + KV_LORA + ROPE_DIM
    c_ga, c_gb = win[:, o:o + D], win[:, o + D:o + 2 * D]
    zc = lambda n: jnp.zeros((D, n), BF16)
    half = ROPE_DIM // 2
    NP0 = 5 * D + 512 + Q_LORA + KV_LORA + LANE
    NP = -(-NP0 // 512) * 512
    win_p = jnp.concatenate([win[:, :3 * D], c_ga, c_gb, c_xg, c_xw, zc(LANE - W_LORA), c_xa, zc(LANE - A_LORA), c_cq, c_ckv,
                             c_kpe[:, :half], zc(half), c_kpe[:, half:], zc(half), zc(NP - NP0)], axis=1)
    O_GA, O_GB, O_L, O_CQ, O_CKV, O_KPE = 3 * D, 4 * D, 5 * D, 5 * D + 512, 5 * D + 512 + Q_LORA, 5 * D + 512 + Q_LORA + KV_LORA
    zr = lambda n: jnp.zeros((n, D), BF16)
    w_up_p = jnp.concatenate([full['w_up'], zr(LANE - W_LORA)], axis=0)
    a_up_p = jnp.concatenate([full['a_up'], zr(LANE - A_LORA)], axis=0)
    g_up = full['g_up']
    wuq = full['w_uq'].reshape(Q_LORA, Hm, QK_DIM)
    zq = jnp.zeros((Q_LORA, Hm, half), BF16)
    wqn = wuq[:, :, :NOPE_DIM].reshape(Q_LORA, Hm * LANE)
    wqp = jnp.concatenate([wuq[:, :, NOPE_DIM:NOPE_DIM + half], zq, wuq[:, :, NOPE_DIM + half:], zq], axis=2).reshape(Q_LORA, Hm * LANE)
    wukv = full['w_ukv'].reshape(KV_LORA, Hm, NOPE_DIM + V_DIM)
    wkn = wukv[:, :, :NOPE_DIM].reshape(KV_LORA, Hm * LANE)
    wv = wukv[:, :, NOPE_DIM:].reshape(KV_LORA, Hm * LANE)
    wout = full['w_out']
    tmu = w2['tm_mu']
    mu_a = tmu[:, :3 * D]
    zm = lambda n: jnp.zeros((1, n), F32)
    mu_b = jnp.concatenate([tmu[:, 3 * D + 2 * W_LORA:], tmu[:, 3 * D:3 * D + W_LORA], zm(LANE - W_LORA),
                            tmu[:, 3 * D + W_LORA:3 * D + 2 * W_LORA], zm(LANE - A_LORA)], axis=1)
    pos = jnp.arange(T, dtype=F32)
    inv_freq = 1.0 / (ROPE_THETA ** (jnp.arange(0, ROPE_DIM, 2, dtype=F32) / ROPE_DIM))
    ang = pos[:, None] * inv_freq[None, :]
    zt = jnp.zeros((T, half), F32)
    cos_t = jnp.concatenate([jnp.cos(ang), zt, jnp.cos(ang), zt], axis=1)
    sin_t = jnp.concatenate([-jnp.sin(ang), zt, jnp.sin(ang), zt], axis=1)
    cos_q, sin_q = cos_t[N_META:], sin_t[N_META:]

    t_full = _rows_tile(R, D)
    t_512 = _rows_tile(R, 512, T)
    t_128 = _rows_tile(R, LANE, T)
    tq_full = _rows_tile(RS, D)
    tq_128 = _rows_tile(RS, LANE, S)

    def rows3(z):
        return z.reshape(Bl, T, z.shape[-1])

    def real_rows(z):
        return rows3(z)[:, N_META:].reshape(RS, z.shape[-1])

    def pad_meta(z):
        z3 = z.reshape(Bl, S, z.shape[-1])
        return jnp.concatenate([jnp.zeros((Bl, N_META, z.shape[-1]), z.dtype), z3], axis=1).reshape(R, z.shape[-1])

    def shift_down(z):
        z3 = rows3(z)
        return jnp.concatenate([jnp.zeros((Bl, 1, z.shape[-1]), z.dtype), z3[:, :-1]], axis=1).reshape(R, z.shape[-1])

    def shift_up(z):
        z3 = rows3(z)
        return jnp.concatenate([z3[:, 1:], jnp.zeros((Bl, 1, z.shape[-1]), z.dtype)], axis=1).reshape(R, z.shape[-1])

    def ffn_fwd(h, nw, wg, wu, wd, tag):
        n = rowwise(_rms, [(h, D, 0)], [(nw, D, 0)], [(D, D, BF16)], tm=t_full, name=tag + "_norm")[0]
        gate, up, act = mm_epi(n, [wg, wu], "nn", lambda g_, u_: (g_, u_, _silu_mul(g_, u_)), [], [F32, F32, BF16],
                               b_stack=True, name=tag + "_gate_up")
        return mm(act, wd, res=h, alpha=0.5, name=tag + "_down"), (h, n, gate, up, act)

    def ffn_bwd(dh2, saved, nw, wg, wu, wd, tag):
        h, n, gate, up, act = saved
        dz = rowwise(lambda d: 0.5 * d, [(dh2, 512, 0)], [], [(D, 512, BF16)], tm=t_512, ncb=D // 512, name=tag + "_dz")[0]
        d_wd = mm(act, dz, "tn", name=tag + "_dwd")
        dgate, dup = mm_epi(dz, [wd], "nt", lambda da, g_, u_: vjp_fn(_silu_mul, 2)(g_, u_, da), [gate, up],
                            [BF16, BF16], name=tag + "_dact")
        d_wg = mm(n, dgate, "tn", out_stack=True, name=tag + "_dwg")
        d_wu = mm(n, dup, "tn", out_stack=True, name=tag + "_dwu")
        dn = mm(dgate, wg, "nt", b_stack=True, name=tag + "_dn1")
        dn = mm(dup, wu, "nt", res=dn, b_stack=True, name=tag + "_dn2")

        def f(h_, dn_, dh_, nw_):
            dh, dnw = vjp_fn(_rms, 2)(h_, nw_, dn_)
            return dh + dh_, dnw

        dh, d_nw = rowwise(f, [(h, D, 0), (dn, D, 0), (dh2, D, 0)], [(nw, D, 0)], [(D, D, F32)], [(1, D, D)],
                           tm=t_full, name=tag + "_dnorm")
        return dh, d_nw, d_wg, d_wu, d_wd

    h0 = jnp.concatenate([jnp.broadcast_to(meta_full[None], (Bl, N_META, D)), x], axis=1).reshape(R, D)
    h1, sv1 = ffn_fwd(h0, w2['ffn1_norm'], stk['ffn1_w_gate'], stk['ffn1_w_up'], full['ffn1_w_down'], "ffn1")
    u = rowwise(_rms, [(h1, D, 0)], [(w2['mix_norm'], D, 0)], [(D, D, BF16)], tm=t_full, name="mix_norm")[0]
    proj = mm(u, win_p, name="proj")
    prev_a = shift_down(proj[:, :3 * D])
    prev_b = shift_down(proj[:, O_L:O_L + 512])
    ps = rowwise(_shift, [(proj, 512, 0), (prev_a, 512, 0)], [(mu_a, 512, 0)], [(3 * D, 512, F32)], tm=t_512,
                 ncb=3 * D // 512, name="shift_rkv")[0]
    sg, txw, xas = rowwise(_lora_act, [(proj, 512, O_L // 512), (prev_b, 512, 0)], [(mu_b, 512, 0)],
                           [(G_LORA, G_LORA, BF16), (LANE, LANE, BF16), (LANE, LANE, BF16)], tm=t_512, name="lora_act")
    lw = mm(txw, w_up_p, name="lora_w")
    la = mm(xas, a_up_p, name="lora_a")
    g = mm(sg, g_up, name="lora_g")
    hb = D // LANE
    par_d = lambda n: (w2[n], LANE, 0)
    decay, kn, bb, k2 = rowwise(_prep, [(ps, LANE, hb), (lw, LANE, 0), (la, LANE, 0)],
                                [par_d('w0'), par_d('a0'), par_d('k_k'), par_d('k_a')], [(D, LANE, F32)] * 4,
                                tm=t_128, ncb=hb, name="wkv_prep")

    def to_j(z):
        z = z.reshape(Bl, T, Hr, RWKV_HEAD).transpose(1, 3, 0, 2).reshape(T, RWKV_HEAD, Bl * Hr)
        return jnp.concatenate([z, z], axis=-1)

    def to_i(z):
        return z.reshape(Bl, T, Hr, 2, RWKV_HEAD // 2).transpose(1, 4, 3, 0, 2).reshape(T, RWKV_HEAD // 2, 2 * Bl * Hr)

    def from_i(z):
        return z.reshape(T, RWKV_HEAD // 2, 2, Bl, Hr).transpose(3, 0, 4, 2, 1).reshape(R, D)

    def from_j(z):
        return z.reshape(T, RWKV_HEAD, Bl, Hr).transpose(2, 0, 3, 1).reshape(R, D)

    r_s, v_s = ps[:, :D], ps[:, 2 * D:]
    jw, jkn, jb, jk, jr, iv = to_j(decay), to_j(kn), to_j(bb), to_j(k2), to_j(r_s), to_i(v_s)
    y_i, sp, sa_i = wkv_fwd(jw, jkn, jb, jk, jr, iv)
    y = from_i(y_i)
    post_rows = [(y, LANE, 0), (ps, LANE, 0), (k2, LANE, 0), (ps, LANE, 2 * hb), (g, LANE, 0)]
    post_pars = [par_d('gn_w'), par_d('gn_b'), par_d('r_k')]
    ya = rowwise(_post, post_rows, post_pars, [(D, LANE, F32)], tm=t_128, ncb=hb, name="wkv_post")[0]

    nt512 = T // t_512
    mla_rows = [(proj, Q_LORA, O_CQ // Q_LORA), (proj, KV_LORA, O_CKV // KV_LORA), (proj, LANE, O_KPE // LANE)]
    tabs = [(cos_t, LANE, 0, nt512, True), (sin_t, LANE, 0, nt512, True)]
    mla_pars = [(w2['q_norm'], Q_LORA, 0), (w2['kv_norm'], KV_LORA, 0)]
    cqn, ckvn, kpr = rowwise(_mla_pre, mla_rows + tabs, mla_pars,
                             [(Q_LORA, Q_LORA, BF16), (KV_LORA, KV_LORA, BF16), (LANE, LANE, BF16)], tm=t_512, name="mla_pre")
    cqn_r = real_rows(cqn)
    qn = mm(cqn_r, wqn, out_dtype=BF16, name="q_nope")
    qp_raw = mm(cqn_r, wqp, name="q_pe")
    ntq = S // tq_128
    qtabs = [(cos_q, LANE, 0, ntq, True), (sin_q, LANE, 0, ntq, True)]
    qp = rowwise(_rope, [(qp_raw, LANE, 0)] + qtabs, [], [(D, LANE, BF16)], tm=tq_128, ncb=Hm, name="q_rope")[0]
    knope = mm(ckvn, wkn, out_dtype=BF16, name="k_nope")
    vv = mm(ckvn, wv, out_dtype=BF16, name="v_proj")

    def pad_keys(z):
        z3 = rows3(z)
        return jnp.concatenate([z3[:, :N_META], jnp.zeros((Bl, Q_BLOCK - N_META, z.shape[-1]), z.dtype), z3[:, N_META:]], axis=1)

    def unpad_keys(z):
        return jnp.concatenate([z[:, :N_META], z[:, Q_BLOCK:]], axis=1).reshape(R, z.shape[-1])

    qn3, qp3 = qn.reshape(Bl, S, D), qp.reshape(Bl, S, D)
    knp, kpp, vp = pad_keys(knope), pad_keys(kpr), pad_keys(vv)
    o_att = pad_meta(attn_fwd(qn3, qp3, knp, kpp, vp).reshape(RS, D))
    mix_rows = [(proj, 512, O_GA // 512), (proj, 512, O_GB // 512), (ya, 512, 0), (o_att, 512, 0)]
    mix = rowwise(_gate_mix, mix_rows, [], [(D, 512, BF16)], tm=t_512, ncb=D // 512, name="gate_mix")[0]
    h2 = mm(mix, wout, res=h1, name="w_out")
    h3, sv2 = ffn_fwd(h2, w2['ffn2_norm'], stk['ffn2_w_gate'], stk['ffn2_w_up'], full['ffn2_w_down'], "ffn2")

    def loss_fb(h_, tgt, fw):
        yv, pull = jax.vjp(_rms, h_, fw)
        e = yv - tgt
        dh, dfw = pull(e * (1.0 / D))
        return dh, jnp.full((1, LANE), 0.5 / D * jnp.sum(e * e), F32), dfw

    dh3r, lossp, g_final = rowwise(loss_fb, [(real_rows(h3), D, 0), (a['loss_target'].reshape(RS, D), D, 0)],
                                   [(w2['final_norm'], D, 0)], [(D, D, F32)], [(1, LANE, LANE), (1, D, D)],
                                   tm=tq_full, name="loss")
    dh3 = pad_meta(dh3r)

    gr = {'final_norm': g_final}
    dh2, gr['ffn2_norm'], gr['ffn2_w_gate'], gr['ffn2_w_up'], gr['ffn2_w_down'] = ffn_bwd(
        dh3, sv2, w2['ffn2_norm'], stk['ffn2_w_gate'], stk['ffn2_w_up'], full['ffn2_w_down'], "ffn2")
    dh2b = rowwise(lambda d: d, [(dh2, 512, 0)], [], [(D, 512, BF16)], tm=t_512, ncb=D // 512, name="dh2_cast")[0]
    gr['w_out'] = mm(mix, dh2b, "tn", name="d_wout")
    dmix = mm(dh2b, wout, "nt", name="d_mix")
    dga, dgb, dya, do = rowwise(vjp_fn(_gate_mix, 4), mix_rows + [(dmix, 512, 0)], [],
                                [(D, 512, BF16), (D, 512, BF16), (D, 512, F32), (D, 512, BF16)], tm=t_512, ncb=D // 512,
                                name="d_gate_mix")
    dqn, dqp, dknp, dvp, dkpp = attn_bwd(qn3, qp3, knp, kpp, vp, real_rows(do).reshape(Bl, S, D))
    dqn2 = dqn.reshape(RS, D)
    dqp_raw = rowwise(_rope_t, [(dqp.reshape(RS, D), LANE, 0)] + qtabs, [], [(D, LANE, BF16)], tm=tq_128, ncb=Hm,
                      name="d_q_rope")[0]
    d_wqn = mm(cqn_r, dqn2, "tn", name="d_wqn")
    d_wqp = mm(cqn_r, dqp_raw, "tn", name="d_wqp")
    dcqn = mm(dqn2, wqn, "nt", name="d_cqn1")
    dcqn = pad_meta(mm(dqp_raw, wqp, "nt", res=dcqn, name="d_cqn2"))
    dkn2, dv2, dkp2 = unpad_keys(dknp), unpad_keys(dvp), unpad_keys(dkpp)
    d_wkn = mm(ckvn, dkn2, "tn", name="d_wkn")
    d_wv = mm(ckvn, dv2, "tn", name="d_wv")
    dckvn = mm(dkn2, wkn, "nt", name="d_ckvn1")
    dckvn = mm(dv2, wv, "nt", res=dckvn, name="d_ckvn2")
    dcq, dckv, dkpe, gr['q_norm'], gr['kv_norm'] = rowwise(
        _mla_pre_bwd, mla_rows[:2] + [(dcqn, Q_LORA, 0), (dckvn, KV_LORA, 0), (dkp2, LANE, 0)] + tabs, mla_pars,
        [(Q_LORA, Q_LORA, BF16), (KV_LORA, KV_LORA, BF16), (LANE, LANE, BF16)], [(1, Q_LORA, Q_LORA), (1, KV_LORA, KV_LORA)],
        tm=t_512, name="d_mla_pre")

    def post_bwd(y_, r_, k2_, v_, g_, dya_, gnw, gnb, rk):
        return vjp_fn(_post, 8)(y_, r_, k2_, v_, g_, gnw, gnb, rk, dya_)

    dy, dr_b, dk2_b, dv_b, dg, gr['gn_w'], gr['gn_b'], gr['r_k'] = rowwise(
        post_bwd, post_rows + [(dya, LANE, 0)], post_pars,
        [(D, LANE, F32)] * 4 + [(D, LANE, BF16)], [(1, D, LANE)] * 3, tm=t_128, ncb=hb, name="d_wkv_post")
    jdw, jdkn, jdb, jdk, jdr, idv = wkv_bwd(jw, jkn, jb, jk, jr, iv, to_i(dy), sp, sa_i)
    ddecay, dkn_w, db_w, dk2_w, dr_w, dv_w = from_j(jdw), from_j(jdkn), from_j(jdb), from_j(jdk), from_j(jdr), from_i(idv)

    def prep_bwd(k_, lw_, la_, dd, dkn_, db_, dk2a, dk2b, w0, a0, kkw, kaw):
        return vjp_fn(_prep, 7)(k_, lw_, la_, w0, a0, kkw, kaw, dd, dkn_, db_, dk2a + dk2b)

    dk_s, dlw, dla, gr['w0'], gr['a0'], gr['k_k'], gr['k_a'] = rowwise(
        prep_bwd, [(ps, LANE, hb), (lw, LANE, 0), (la, LANE, 0), (ddecay, LANE, 0), (dkn_w, LANE, 0), (db_w, LANE, 0),
                   (dk2_w, LANE, 0), (dk2_b, LANE, 0)],
        [par_d('w0'), par_d('a0'), par_d('k_k'), par_d('k_a')], [(D, LANE, F32), (D, LANE, BF16), (D, LANE, BF16)],
        [(1, D, LANE)] * 4, tm=t_128, ncb=hb, name="d_wkv_prep")
    d_wup = mm(txw, dlw, "tn", name="d_wup")
    dtxw = mm(dlw, w_up_p, "nt", name="d_txw")
    d_aup = mm(xas, dla, "tn", name="d_aup")
    dxa = mm(dla, a_up_p, "nt", name="d_xa")
    gr['g_up'] = mm(sg, dg, "tn", name="d_gup")
    dsg = mm(dg, g_up, "nt", name="d_sg")

    def lora_bwd(p_, prev_, dsg_, dt_, dxa_, mu_):
        return vjp_fn(_lora_act, 3)(p_, prev_, mu_, dsg_, dt_, dxa_)

    dpb, dprevb, dmu_b = rowwise(lora_bwd, [(proj, 512, O_L // 512), (prev_b, 512, 0), (dsg, G_LORA, 0), (dtxw, LANE, 0),
                                            (dxa, LANE, 0)], [(mu_b, 512, 0)], [(512, 512, F32)] * 2, [(1, 512, 512)],
                                 tm=t_512, name="d_lora_act")

    def shift_bwd2(p_, prev_, c1, c2, mu_):
        return vjp_fn(_shift, 3)(p_, prev_, mu_, c1 + c2)

    def shift_bwd1(p_, prev_, c1, mu_):
        return vjp_fn(_shift, 3)(p_, prev_, mu_, c1)

    def shift_back(sec, cts):
        nb = D // 512
        f = shift_bwd2 if len(cts) == 2 else shift_bwd1
        return rowwise(f, [(proj, 512, sec * nb), (prev_a, 512, sec * nb)] + [(c, 512, 0) for c in cts],
                       [(mu_a, 512, sec * nb)], [(D, 512, F32)] * 2, [(1, D, 512)], tm=t_512, ncb=nb, name=f"d_shift{sec}")

    dp_r, dprev_r, dmu_r = shift_back(0, [dr_w, dr_b])
    dp_k, dprev_k, dmu_k = shift_back(1, [dk_s])
    dp_v, dprev_v, dmu_v = shift_back(2, [dv_w, dv_b])

    def add_cast(p_, q_):
        return p_ + q_

    def dsec(dp_, dprev_, tag):
        C = dp_.shape[1]
        return rowwise(add_cast, [(dp_, 512, 0), (shift_up(dprev_), 512, 0)], [], [(C, 512, BF16)], tm=t_512, ncb=C // 512,
                       name="d_sec_" + tag)[0]

    zpad = jnp.zeros((R, NP - NP0), BF16)
    dproj = jnp.concatenate([dsec(dp_r, dprev_r, "r"), dsec(dp_k, dprev_k, "k"), dsec(dp_v, dprev_v, "v"), dga, dgb,
                             dsec(dpb, dprevb, "l"), dcq, dckv, dkpe, zpad], axis=1)
    d_win_p = mm(u, dproj, "tn", name="d_win")
    du = mm(dproj, win_p, "nt", name="d_u")

    def norm_bwd(h_, dn_, dh_, nw_):
        dh, dnw = vjp_fn(_rms, 2)(h_, nw_, dn_)
        return dh + dh_, dnw

    dh1, gr['mix_norm'] = rowwise(norm_bwd, [(h1, D, 0), (du, D, 0), (dh2, D, 0)], [(w2['mix_norm'], D, 0)],
                                  [(D, D, F32)], [(1, D, D)], tm=t_full, name="d_mix_norm")
    dh0, gr['ffn1_norm'], gr['ffn1_w_gate'], gr['ffn1_w_up'], gr['ffn1_w_down'] = ffn_bwd(
        dh1, sv1, w2['ffn1_norm'], stk['ffn1_w_gate'], stk['ffn1_w_up'], full['ffn1_w_down'], "ffn1")
    dh0_3 = rows3(dh0)
    grad_x = dh0_3[:, N_META:]
    gr['meta_tokens'] = jnp.sum(dh0_3[:, :N_META], axis=0)

    gr['w_in'] = jnp.concatenate([
        d_win_p[:, :3 * D], d_win_p[:, O_L + G_LORA:O_L + G_LORA + W_LORA], d_win_p[:, O_L + G_LORA + LANE:O_L + G_LORA + LANE + A_LORA],
        d_win_p[:, O_L:O_L + G_LORA], d_win_p[:, O_CQ:O_CQ + Q_LORA], d_win_p[:, O_CKV:O_CKV + KV_LORA],
        d_win_p[:, O_KPE:O_KPE + half], d_win_p[:, O_KPE + 2 * half:O_KPE + 3 * half], d_win_p[:, O_GA:O_GA + 2 * D]], axis=1)
    gr['tm_mu'] = jnp.concatenate([dmu_r, dmu_k, dmu_v, dmu_b[:, G_LORA:G_LORA + W_LORA],
                                   dmu_b[:, G_LORA + LANE:G_LORA + LANE + A_LORA], dmu_b[:, :G_LORA]], axis=1)
    gr['w_up'], gr['a_up'] = d_wup[:W_LORA], d_aup[:A_LORA]
    dq3n, dq3p = d_wqn.reshape(Q_LORA, Hm, LANE), d_wqp.reshape(Q_LORA, Hm, LANE)
    gr['w_uq'] = jnp.concatenate([dq3n, dq3p[:, :, :half], dq3p[:, :, 2 * half:3 * half]], axis=2).reshape(Q_LORA, Hm * QK_DIM)
    gr['w_ukv'] = jnp.concatenate([d_wkn.reshape(KV_LORA, Hm, LANE), d_wv.reshape(KV_LORA, Hm, LANE)], axis=2).reshape(
        KV_LORA, Hm * (NOPE_DIM + V_DIM))

    def stacked(g, axis):
        if g.ndim == 3:
            return g
        if axis == 0:
            return g.reshape(4, g.shape[0] // 4, g.shape[1])
        return g.reshape(g.shape[0], 4, g.shape[1] // 4).transpose(1, 0, 2)

    cidx = ci.reshape(1).astype(jnp.int32)
    gs = [stacked(gr[n], SHARD_AXIS[n]) for n in GATHERED]
    got = swap_halves(gs)
    hs = [add_half(g, r, cidx, "add_half_" + n) for n, g, r in zip(GATHERED, gs, got)]
    sc = (chip32.reshape(1), cidx)
    ts = [sum_chips(p, h, sc, "sum_chips_" + n) for n, p, h in zip(GATHERED, rs_chips(hs), hs)]
    g_shard = {n: z.reshape(w2[n].shape) for n, z in zip(GATHERED, share_sibling(ts))}
    small = jnp.concatenate([gr[n].reshape(-1) for n in SMALL] + [gr['meta_tokens'].reshape(-1), lossp.reshape(-1)])
    ns = small.shape[0]
    nsp = -(-ns // (8 * LANE)) * 8 * LANE
    small_sum = allreduce8(jnp.pad(small, (0, nsp - ns)).reshape(-1, LANE), "allreduce_small").reshape(-1)
    g_small = dict(zip(SMALL + ['meta_full'], _unpack(small_sum, [w2[n].shape for n in SMALL] + [(N_META, D)])))
    g_shard['meta_tokens'] = lax.dynamic_slice(
        g_small['meta_full'], (jnp.zeros((), jnp.int32), (chip * mcols).astype(jnp.int32)), (N_META, mcols))
    loss = small_sum[ns - LANE]

    grads, deltas, new_m, new_v = [], [], [], []
    for n in WEIGHTS:
        gw = g_shard[n] if n in g_shard else g_small[n]
        d_, m_, v_ = adamw(w2[n], gw, m2[n], v2[n], "adamw_" + n)
        shp = a[n].shape
        grads.append(gw.reshape(shp))
        deltas.append(d_.reshape(shp))
        new_m.append(m_.reshape(shp))
        new_v.append(v_.reshape(shp))
    return (loss, grad_x, *grads, *deltas, *new_m, *new_v)


def kernel(x, meta_tokens, ffn1_norm, ffn1_w_gate, ffn1_w_up, ffn1_w_down, mix_norm, w_in, tm_mu, w0, w_up, a0, a_up, g_up, k_k, k_a, r_k, gn_w, gn_b, q_norm, w_uq, kv_norm, w_ukv, w_out, ffn2_norm, ffn2_w_gate, ffn2_w_up, ffn2_w_down, final_norm, loss_target, m_meta_tokens, m_ffn1_norm, m_ffn1_w_gate, m_ffn1_w_up, m_ffn1_w_down, m_mix_norm, m_w_in, m_tm_mu, m_w0, m_w_up, m_a0, m_a_up, m_g_up, m_k_k, m_k_a, m_r_k, m_gn_w, m_gn_b, m_q_norm, m_w_uq, m_kv_norm, m_w_ukv, m_w_out, m_ffn2_norm, m_ffn2_w_gate, m_ffn2_w_up, m_ffn2_w_down, m_final_norm, v_meta_tokens, v_ffn1_norm, v_ffn1_w_gate, v_ffn1_w_up, v_ffn1_w_down, v_mix_norm, v_w_in, v_tm_mu, v_w0, v_w_up, v_a0, v_a_up, v_g_up, v_k_k, v_k_a, v_r_k, v_gn_w, v_gn_b, v_q_norm, v_w_uq, v_kv_norm, v_w_ukv, v_w_out, v_ffn2_norm, v_ffn2_w_gate, v_ffn2_w_up, v_ffn2_w_down, v_final_norm):
    return _step(dict(locals()))
```

```python
import functools
import math

import jax
import jax.numpy as jnp
import numpy as np
from jax import lax
from jax.experimental import pallas as pl
from jax.experimental.pallas import tpu as pltpu

F32 = jnp.float32
BF16 = jnp.bfloat16
MESH = pl.DeviceIdType.MESH

N_META = 16
NORM_EPS = 1e-6
RWKV_HEAD = 64
GN_EPS = RWKV_HEAD * 1e-5
W_LORA, A_LORA, G_LORA = 96, 96, 256
Q_LORA, KV_LORA = 512, 512
NOPE_DIM, ROPE_DIM, V_DIM = 128, 64, 128
QK_DIM = NOPE_DIM + ROPE_DIM
ROPE_THETA = 10000.0
Q_BLOCK = 128
ADAM_LR, ADAM_B1, ADAM_B2, ADAM_EPS, ADAM_WD, ADAM_STEP = 0.001, 0.9, 0.999, 1e-08, 0.01, 10

LANE = 128
VMEM_LIMIT = 56 * 1024 * 1024


def _pcall(body, **kw):
    return pl.pallas_call(body, **kw)


def _cparams(sem):
    return pltpu.CompilerParams(dimension_semantics=sem, vmem_limit_bytes=VMEM_LIMIT)


MM_LANE_TILE = 1536
MM_ROW_TILE = 1408


def _div_tile(n, cap, unit):
    best = None
    for t in range(unit, min(n, cap) + 1, unit):
        if n % t == 0:
            best = t
    return best if best is not None else n


def _rtile(n, pref=512):
    best = None
    for t in range(16, min(n, pref * 2) + 1, 16):
        if n % t == 0 and (best is None or abs(t - pref) < abs(best - pref)):
            best = t
    return best if best is not None else n


def mm(a, b, mode="nn", out_dtype=F32, res=None, alpha=1.0, b_stack=False, out_stack=False, name="mm"):
    cs = b.shape[-1] if b_stack else None
    bs = (b.shape[1], 4 * cs) if b_stack else b.shape
    if mode == "nn":
        (M, K), (K2, N) = a.shape, bs
    elif mode == "nt":
        (M, K), (N, K2) = a.shape, bs
    else:
        (K, M), (K2, N) = a.shape, bs
    assert K == K2, (a.shape, b.shape, mode)
    if mode == "tn":
        tm, tk = _div_tile(M, MM_ROW_TILE, LANE), _div_tile(K, 1024, 16)
    else:
        tm = _div_tile(M, MM_ROW_TILE, 16)
        tk = _div_tile(cs if (b_stack and mode == "nt") else K, MM_LANE_TILE, LANE)
    ncol = N // 4 if out_stack else (cs if (b_stack and mode == "nn") else N)
    tn = _div_tile(ncol, MM_LANE_TILE if mode != "nt" else 1024, LANE)
    nk = K // tk
    dims = {"nn": (((1,), (0,)), ((), ())), "nt": (((1,), (1,)), ((), ())), "tn": (((0,), (0,)), ((), ()))}[mode]
    direct = out_dtype == F32

    def body(*refs):
        a_ref, b_ref = refs[:2]
        r_ref = refs[2] if res is not None else None
        o_ref = refs[3] if res is not None else refs[2]
        acc = o_ref if direct else refs[-1]
        k = pl.program_id(2)

        @pl.when(k == 0)
        def _():
            acc[...] = jnp.zeros_like(acc) if res is None else r_ref[...].astype(F32)

        p = lax.dot_general(a_ref[...].astype(BF16), b_ref[...].astype(BF16), dims, preferred_element_type=F32)
        acc[...] += p if alpha == 1.0 else alpha * p

        if not direct:
            @pl.when(k == nk - 1)
            def _():
                o_ref[...] = acc[...].astype(o_ref.dtype)

    if mode == "tn":
        a_spec = pl.BlockSpec((tk, tm), lambda i, j, k: (k, i))
        b_spec = pl.BlockSpec((tk, tn), lambda i, j, k: (k, j))
    else:
        a_spec = pl.BlockSpec((tm, tk), lambda i, j, k: (i, k))
        if mode == "nn":
            if b_stack:
                nps = cs // tn
                b_spec = pl.BlockSpec((None, tk, tn), lambda i, j, k: (j // nps, k, j % nps))
            else:
                b_spec = pl.BlockSpec((tk, tn), lambda i, j, k: (k, j))
        elif b_stack:
            kps = cs // tk
            b_spec = pl.BlockSpec((None, tn, tk), lambda i, j, k: (k // kps, j, k % kps))
        else:
            b_spec = pl.BlockSpec((tn, tk), lambda i, j, k: (j, k))
    r_spec = pl.BlockSpec((tm, tn), lambda i, j, k: (i, j))
    if out_stack:
        ops = (N // 4) // tn
        o_spec = pl.BlockSpec((None, tm, tn), lambda i, j, k: (j // ops, i, j % ops))
        o_shape = jax.ShapeDtypeStruct((4, M, N // 4), out_dtype)
    else:
        o_spec, o_shape = r_spec, jax.ShapeDtypeStruct((M, N), out_dtype)
    in_specs = [a_spec, b_spec] + ([r_spec] if res is not None else [])
    args = (a, b) + ((res,) if res is not None else ())
    return _pcall(
        body, name=name, grid=(M // tm, N // tn, nk), in_specs=in_specs, out_specs=o_spec, out_shape=o_shape,
        scratch_shapes=[] if direct else [pltpu.VMEM((tm, tn), F32)],
        compiler_params=_cparams(("parallel", "parallel", "arbitrary")),
    )(*args)


MM_EPI_ROW_TILE = 704


def mm_epi(a, bs, mode, epi, extras, out_dtypes, b_stack=False, name="mm_epi"):
    b = bs[0]
    cs = b.shape[-1] if b_stack else None
    bshape = (b.shape[1], 4 * cs) if b_stack else b.shape
    (M, K) = a.shape
    (K2, N) = bshape if mode == "nn" else bshape[::-1]
    assert K == K2 and mode in ("nn", "nt"), (a.shape, b.shape, mode)
    tm = _div_tile(M, MM_EPI_ROW_TILE, 16)
    tk = _div_tile(cs if (b_stack and mode == "nt") else K, MM_LANE_TILE, LANE)
    tn = _div_tile(cs if (b_stack and mode == "nn") else N, MM_LANE_TILE, LANE)
    nk, nb, ne, no = K // tk, len(bs), len(extras), len(out_dtypes)
    dims = (((1,), (0,)), ((), ())) if mode == "nn" else (((1,), (1,)), ((), ()))

    def body(*refs):
        a_ref, b_refs, e_refs = refs[0], refs[1:1 + nb], refs[1 + nb:1 + nb + ne]
        o_refs, accs = refs[1 + nb + ne:1 + nb + ne + no], refs[1 + nb + ne + no:]
        k = pl.program_id(2)

        @pl.when(k == 0)
        def _():
            for acc in accs:
                acc[...] = jnp.zeros_like(acc)

        av = a_ref[...].astype(BF16)
        for b_ref, acc in zip(b_refs, accs):
            acc[...] += lax.dot_general(av, b_ref[...].astype(BF16), dims, preferred_element_type=F32)

        @pl.when(k == nk - 1)
        def _():
            res = epi(*[acc[...] for acc in accs], *[e[...] for e in e_refs])
            for o_ref, v in zip(o_refs, res):
                o_ref[...] = v.astype(o_ref.dtype)

    a_spec = pl.BlockSpec((tm, tk), lambda i, j, k: (i, k))
    if mode == "nn":
        if b_stack:
            nps = cs // tn
            b_spec = pl.BlockSpec((None, tk, tn), lambda i, j, k: (j // nps, k, j % nps))
        else:
            b_spec = pl.BlockSpec((tk, tn), lambda i, j, k: (k, j))
    elif b_stack:
        kps = cs // tk
        b_spec = pl.BlockSpec((None, tn, tk), lambda i, j, k: (k // kps, j, k % kps))
    else:
        b_spec = pl.BlockSpec((tn, tk), lambda i, j, k: (j, k))
    o_spec = pl.BlockSpec((tm, tn), lambda i, j, k: (i, j))
    return _pcall(
        body, name=name, grid=(M // tm, N // tn, nk), in_specs=[a_spec] + [b_spec] * nb + [o_spec] * ne,
        out_specs=[o_spec] * no, out_shape=[jax.ShapeDtypeStruct((M, N), dt) for dt in out_dtypes],
        scratch_shapes=[pltpu.VMEM((tm, tn), F32)] * nb,
        compiler_params=_cparams(("parallel", "parallel", "arbitrary")),
    )(a, *bs, *extras)


def rowwise(fn, row_ins, par_ins, outs, accs=(), *, tm, ncb=1, name="rowwise"):
    R = row_ins[0][0].shape[0]
    assert R % tm == 0, (R, tm)
    nrb = R // tm
    in_specs, args = [], []
    for spec in row_ins:
        arr, w, base = spec[:3]
        mod = spec[3] if len(spec) > 3 else None
        cstep = 0 if (len(spec) > 4 and spec[4]) else 1
        if mod is None:
            in_specs.append(pl.BlockSpec((tm, w), lambda j, i, base=base, cstep=cstep: (i, base + cstep * j)))
        else:
            in_specs.append(pl.BlockSpec((tm, w), lambda j, i, base=base, mod=mod, cstep=cstep: (i % mod, base + cstep * j)))
        args.append(arr)
    for arr, w, base in par_ins:
        in_specs.append(pl.BlockSpec((arr.shape[0], w), lambda j, i, base=base: (0, base + j)))
        args.append(arr)
    out_specs, out_shape = [], []
    for cols, w, dt in outs:
        out_specs.append(pl.BlockSpec((tm, w), lambda j, i: (i, j)))
        out_shape.append(jax.ShapeDtypeStruct((R, cols), dt))
    for p, cols, w in accs:
        out_specs.append(pl.BlockSpec((p, w), lambda j, i: (0, j)))
        out_shape.append(jax.ShapeDtypeStruct((p, cols), F32))
    nin, nout, nacc = len(args), len(outs), len(accs)

    def body(*refs):
        vals = [r[...] for r in refs[:nin]]
        res = fn(*vals)
        if not isinstance(res, (tuple, list)):
            res = (res,)
        assert len(res) == nout + nacc, (len(res), nout, nacc)
        for o_ref, v in zip(refs[nin:nin + nout], res[:nout]):
            o_ref[...] = v.astype(o_ref.dtype)
        if nacc:
            i = pl.program_id(1)

            @pl.when(i == 0)
            def _():
                for a_ref in refs[nin + nout:]:
                    a_ref[...] = jnp.zeros_like(a_ref)

            for a_ref, v in zip(refs[nin + nout:], res[nout:]):
                a_ref[...] += v.astype(F32)

    r = _pcall(
        body, name=name, grid=(ncb, nrb), in_specs=in_specs, out_specs=out_specs, out_shape=out_shape,
        compiler_params=_cparams(("parallel", "arbitrary")),
    )(*args)
    return r


def vjp_fn(fwd, nprim):
    def f(*vals):
        prim, cts = vals[:nprim], vals[nprim:]
        out, pull = jax.vjp(fwd, *[p.astype(F32) for p in prim])
        if not isinstance(out, (tuple, list)):
            cts = cts[0].astype(F32)
        else:
            cts = tuple(c.astype(F32) for c in cts)
        return pull(cts)
    return f


WKV_TB = 8


def wkv_fwd(w, kn, b, k, r, v):
    T, N, L = w.shape
    NI = v.shape[1]
    tb = WKV_TB
    assert T % tb == 0 and L == v.shape[2]

    def body(w_ref, kn_ref, b_ref, k_ref, r_ref, v_ref, y_ref, sp_ref, sa_ref, s_ref):
        @pl.when(pl.program_id(0) == 0)
        def _():
            s_ref[...] = jnp.zeros_like(s_ref)

        def step(s, carry):
            W, KN, B, Kk, Rr = w_ref[s], kn_ref[s], b_ref[s], k_ref[s], r_ref[s]
            for i in range(NI):
                S = s_ref[i]
                sp_ref[s, i] = S
                sa = jnp.sum(S * KN, axis=0, keepdims=True)
                sa_ref[s, pl.ds(i, 1), :] = sa
                vi = v_ref[s, pl.ds(i, 1), :]
                Sn = S * W + sa * B + vi * Kk
                s_ref[i] = Sn
                y_ref[s, pl.ds(i, 1), :] = jnp.sum(Sn * Rr, axis=0, keepdims=True)
            return carry

        lax.fori_loop(0, tb, step, 0)

    jspec = pl.BlockSpec((tb, N, L), lambda t: (t, 0, 0))
    ispec = pl.BlockSpec((tb, NI, L), lambda t: (t, 0, 0))
    ishape = jax.ShapeDtypeStruct((T, NI, L), F32)
    return _pcall(
        body, name="wkv_fwd", grid=(T // tb,), in_specs=[jspec] * 5 + [ispec],
        out_specs=[ispec, pl.BlockSpec((tb, NI, N, L), lambda t: (t, 0, 0, 0)), ispec],
        out_shape=[ishape, jax.ShapeDtypeStruct((T, NI, N, L), F32), ishape],
        scratch_shapes=[pltpu.VMEM((NI, N, L), F32)],
        compiler_params=_cparams(("arbitrary",)),
    )(w, kn, b, k, r, v)


def wkv_bwd(w, kn, b, k, r, v, dy, sp, sa):
    T, N, L = w.shape
    NI, LH = v.shape[1], L // 2
    tb = WKV_TB
    nt = T // tb

    def body(w_ref, kn_ref, b_ref, k_ref, r_ref, v_ref, dy_ref, sp_ref, sa_ref,
             dw_ref, dkn_ref, db_ref, dk_ref, dr_ref, dv_ref, ds_ref):
        @pl.when(pl.program_id(0) == 0)
        def _():
            ds_ref[...] = jnp.zeros_like(ds_ref)

        def step(q, carry):
            s = tb - 1 - q
            W, KN, B, Kk, Rr = w_ref[s], kn_ref[s], b_ref[s], k_ref[s], r_ref[s]
            dW = jnp.zeros((N, L), F32)
            dKN, dB, dK, T1 = dW, dW, dW, dW
            al = jnp.zeros((1, L), F32)
            be = al
            for i in range(NI):
                Sp = sp_ref[s, i]
                vi = v_ref[s, pl.ds(i, 1), :]
                dyi = dy_ref[s, pl.ds(i, 1), :]
                sai = sa_ref[s, pl.ds(i, 1), :]
                dS = ds_ref[i] + dyi * Rr
                T1 = T1 + Sp * dyi
                al = al + sai * dyi
                be = be + vi * dyi
                dv_ref[s, pl.ds(i, 1), :] = jnp.sum(dS * Kk, axis=0, keepdims=True)
                dK = dK + dS * vi
                dsa = jnp.sum(dS * B, axis=0, keepdims=True)
                dB = dB + dS * sai
                dW = dW + dS * Sp
                dKN = dKN + Sp * dsa
                ds_ref[i] = dS * W + dsa * KN
            dR = W * T1 + B * al + Kk * be
            for ref, val in ((dw_ref, dW), (dkn_ref, dKN), (db_ref, dB), (dk_ref, dK), (dr_ref, dR)):
                ref[s] = (val + pltpu.roll(val, LH, 1))[:, :LH]
            return carry

        lax.fori_loop(0, tb, step, 0)

    jspec = pl.BlockSpec((tb, N, L), lambda t: (nt - 1 - t, 0, 0))
    gspec = pl.BlockSpec((tb, N, LH), lambda t: (nt - 1 - t, 0, 0))
    ispec = pl.BlockSpec((tb, NI, L), lambda t: (nt - 1 - t, 0, 0))
    gshape = jax.ShapeDtypeStruct((T, N, LH), F32)
    return _pcall(
        body, name="wkv_bwd", grid=(nt,),
        in_specs=[jspec] * 5 + [ispec, ispec, pl.BlockSpec((tb, NI, N, L), lambda t: (nt - 1 - t, 0, 0, 0)), ispec],
        out_specs=[gspec] * 5 + [ispec], out_shape=[gshape] * 5 + [jax.ShapeDtypeStruct((T, NI, L), F32)],
        scratch_shapes=[pltpu.VMEM((NI, N, L), F32)],
        compiler_params=_cparams(("arbitrary",)),
    )(w, kn, b, k, r, v, dy, sp, sa)


_NT = (((1,), (1,)), ((), ()))
_TN = (((0,), (0,)), ((), ()))
ATT_SCALE = QK_DIM ** -0.5


def _att_bias():
    col = lax.broadcasted_iota(jnp.int32, (Q_BLOCK, Q_BLOCK), 1)
    row = lax.broadcasted_iota(jnp.int32, (Q_BLOCK, Q_BLOCK), 0)
    return jnp.where(col < N_META, 0.0, -1e30).astype(F32), jnp.where(col <= row, 0.0, -1e30).astype(F32)


def _att_probs(q1, q2, kn_ref, kp_ref, s_ref, bias, L):
    s = lax.dot_general(q1, kn_ref[0, :L, :], _NT, preferred_element_type=F32)
    s = s + lax.dot_general(q2, kp_ref[0, :L, :], _NT, preferred_element_type=F32)
    s_ref[:, :L] = s * ATT_SCALE
    s_ref[:, :Q_BLOCK] += bias[0]
    s_ref[:, L - Q_BLOCK:L] += bias[1]
    s = s_ref[:, :L]
    m = jnp.max(s, axis=-1, keepdims=True)
    p = jnp.exp(s - m)
    return p / jnp.sum(p, axis=-1, keepdims=True)


def attn_fwd(qn, qp, kn, kp, v):
    B, S, HD = qn.shape
    SP = kn.shape[1]
    H = HD // LANE
    nq = S // Q_BLOCK

    def body(qn_ref, qp_ref, kn_ref, kp_ref, v_ref, o_ref, s_ref):
        bias = _att_bias()
        for i in range(nq):
            L = Q_BLOCK * (i + 2)
            rows = pl.ds(Q_BLOCK * i, Q_BLOCK)
            p = _att_probs(qn_ref[0, rows, :], qp_ref[0, rows, :], kn_ref, kp_ref, s_ref, bias, L)
            o_ref[0, rows, :] = jnp.dot(p.astype(BF16), v_ref[0, :L, :], preferred_element_type=F32)

    qspec = pl.BlockSpec((1, S, LANE), lambda b, h: (b, 0, h))
    kspec = pl.BlockSpec((1, SP, LANE), lambda b, h: (b, 0, h))
    pspec = pl.BlockSpec((1, SP, LANE), lambda b, h: (b, 0, 0))
    return _pcall(
        body, name="attn_fwd", grid=(B, H), in_specs=[qspec, qspec, kspec, pspec, kspec], out_specs=qspec,
        out_shape=jax.ShapeDtypeStruct((B, S, HD), F32), scratch_shapes=[pltpu.VMEM((Q_BLOCK, SP), F32)],
        compiler_params=_cparams(("parallel", "parallel")),
    )(qn, qp, kn, kp, v)


def attn_bwd(qn, qp, kn, kp, v, do):
    B, S, HD = qn.shape
    SP = kn.shape[1]
    H = HD // LANE
    nq = S // Q_BLOCK

    def body(qn_ref, qp_ref, kn_ref, kp_ref, v_ref, do_ref, dqn_ref, dqp_ref, dkn_ref, dv_ref, dkp_ref, dkn_acc, dv_acc,
             s_ref):
        @pl.when(pl.program_id(1) == 0)
        def _():
            dkp_ref[...] = jnp.zeros_like(dkp_ref)

        dkn_acc[...] = jnp.zeros_like(dkn_acc)
        dv_acc[...] = jnp.zeros_like(dv_acc)
        bias = _att_bias()
        for i in range(nq):
            L = Q_BLOCK * (i + 2)
            rows = pl.ds(Q_BLOCK * i, Q_BLOCK)
            q1, q2, do_i = qn_ref[0, rows, :], qp_ref[0, rows, :], do_ref[0, rows, :]
            p = _att_probs(q1, q2, kn_ref, kp_ref, s_ref, bias, L)
            dp = lax.dot_general(do_i, v_ref[0, :L, :], _NT, preferred_element_type=F32)
            ds = (p * (dp - jnp.sum(p * dp, axis=-1, keepdims=True)) * ATT_SCALE).astype(BF16)
            dqn_ref[0, rows, :] = jnp.dot(ds, kn_ref[0, :L, :], preferred_element_type=F32).astype(dqn_ref.dtype)
            dqp_ref[0, rows, :] = jnp.dot(ds, kp_ref[0, :L, :], preferred_element_type=F32)
            dkn_acc[:L, :] += lax.dot_general(ds, q1, _TN, preferred_element_type=F32)
            dkp_ref[0, :L, :] += lax.dot_general(ds, q2, _TN, preferred_element_type=F32)
            dv_acc[:L, :] += lax.dot_general(p.astype(BF16), do_i, _TN, preferred_element_type=F32)
        dkn_ref[0] = dkn_acc[...].astype(dkn_ref.dtype)
        dv_ref[0] = dv_acc[...].astype(dv_ref.dtype)

    qspec = pl.BlockSpec((1, S, LANE), lambda b, h: (b, 0, h))
    kspec = pl.BlockSpec((1, SP, LANE), lambda b, h: (b, 0, h))
    pspec = pl.BlockSpec((1, SP, LANE), lambda b, h: (b, 0, 0))
    return _pcall(
        body, name="attn_bwd", grid=(B, H), in_specs=[qspec, qspec, kspec, pspec, kspec, qspec],
        out_specs=[qspec, qspec, kspec, kspec, pspec],
        out_shape=[jax.ShapeDtypeStruct((B, S, HD), BF16), jax.ShapeDtypeStruct((B, S, HD), F32),
                   jax.ShapeDtypeStruct((B, SP, HD), BF16), jax.ShapeDtypeStruct((B, SP, HD), BF16),
                   jax.ShapeDtypeStruct((B, SP, LANE), F32)],
        scratch_shapes=[pltpu.VMEM((SP, LANE), F32), pltpu.VMEM((SP, LANE), F32), pltpu.VMEM((Q_BLOCK, SP), F32)],
        compiler_params=_cparams(("parallel", "arbitrary")),
    )(qn, qp, kn, kp, v, do)


_HBM = pl.BlockSpec(memory_space=pltpu.HBM)


def _place():
    x, y, c = lax.axis_index("x"), lax.axis_index("y"), lax.axis_index("c")
    chips = [(1 - x, y), (x, 1 - y), (1 - x, 1 - y)]
    return x, y, c, chips


def _rcopy(src, dst, ssem, rsem, dev):
    return pltpu.make_async_remote_copy(src_ref=src, dst_ref=dst, send_sem=ssem, recv_sem=rsem,
                                        device_id=dev, device_id_type=MESH)


def _half_rows(c, r):
    return pl.ds(pl.multiple_of(c * (r // 2), 16), r // 2)


def ag_weights(bufs):
    n = len(bufs)

    def body(*refs):
        w, out = refs[:n], refs[n:2 * n]
        ssem, rsem = refs[2 * n:]
        x, y, c, chips = _place()
        s = 2 * x + y
        rows = [_half_rows(c, w[i].shape[1]) for i in range(n)]
        orows = [_half_rows(1 - c, w[i].shape[1]) for i in range(n)]
        first = [_rcopy(w[i].at[s, rows[i]], out[i].at[s, rows[i]], ssem.at[6 * i + j], rsem.at[6 * i + j], (px, py, c))
                 for i in range(n) for j, (px, py) in enumerate(chips)]
        for cp in first:
            cp.start()
        passed = []
        for j, (px, py) in enumerate(chips):
            sp = 2 * px + py
            for i in range(n):
                here = out[i].at[sp, rows[i]]
                _rcopy(here, here, ssem.at[6 * i + j], rsem.at[6 * i + j], (px, py, c)).wait_recv()
                fw = _rcopy(here, here, ssem.at[6 * i + 3 + j], rsem.at[6 * i + 3 + j], (x, y, 1 - c))
                fw.start()
                passed.append(fw)
        for j, (px, py) in enumerate(chips):
            sp = 2 * px + py
            for i in range(n):
                there = out[i].at[sp, orows[i]]
                _rcopy(there, there, ssem.at[6 * i + 3 + j], rsem.at[6 * i + 3 + j], (x, y, 1 - c)).wait_recv()
        for cp in first + passed:
            cp.wait_send()

    return _pcall(
        body, name="ag_weights", in_specs=[_HBM] * n, out_specs=[_HBM] * n,
        out_shape=[jax.ShapeDtypeStruct(w.shape, w.dtype) for w in bufs],
        input_output_aliases={i: i for i in range(n)},
        scratch_shapes=[pltpu.SemaphoreType.DMA((6 * n,)), pltpu.SemaphoreType.DMA((6 * n,))],
    )(*bufs)


def swap_halves(gs):
    n = len(gs)

    def body(*refs):
        g, out = refs[:n], refs[n:2 * n]
        ssem, rsem = refs[2 * n:]
        x, y, c, _ = _place()
        cps = [_rcopy(g[i].at[:, _half_rows(1 - c, g[i].shape[1])], out[i], ssem.at[i], rsem.at[i], (x, y, 1 - c))
               for i in range(n)]
        for cp in cps:
            cp.start()
        for cp in cps:
            cp.wait()

    return _pcall(
        body, name="swap_halves", in_specs=[_HBM] * n, out_specs=[_HBM] * n,
        out_shape=[jax.ShapeDtypeStruct((4, g.shape[1] // 2, g.shape[2]), g.dtype) for g in gs],
        scratch_shapes=[pltpu.SemaphoreType.DMA((n,)), pltpu.SemaphoreType.DMA((n,))],
    )(*gs)


def rs_chips(hs):
    n = len(hs)

    def body(*refs):
        h, out = refs[:n], refs[n:2 * n]
        ssem, rsem = refs[2 * n:]
        x, y, c, chips = _place()
        s = 2 * x + y
        cps = [_rcopy(h[i].at[2 * px + py], out[i].at[s], ssem.at[3 * i + j], rsem.at[3 * i + j], (px, py, c))
               for i in range(n) for j, (px, py) in enumerate(chips)]
        for cp in cps:
            cp.start()
        for j, (px, py) in enumerate(chips):
            for i in range(n):
                _rcopy(h[i].at[s], out[i].at[2 * px + py], ssem.at[3 * i + j], rsem.at[3 * i + j], (px, py, c)).wait_recv()
        for cp in cps:
            cp.wait_send()

    return _pcall(
        body, name="rs_chips", in_specs=[_HBM] * n, out_specs=[_HBM] * n,
        out_shape=[jax.ShapeDtypeStruct(h.shape, h.dtype) for h in hs],
        scratch_shapes=[pltpu.SemaphoreType.DMA((3 * n,)), pltpu.SemaphoreType.DMA((3 * n,))],
    )(*hs)


def share_sibling(ts):
    n = len(ts)

    def body(*refs):
        t, out = refs[:n], refs[n:2 * n]
        ssem, rsem = refs[2 * n:]
        x, y, c, _ = _place()
        cps = [_rcopy(t[i].at[c], out[i].at[c], ssem.at[i], rsem.at[i], (x, y, 1 - c)) for i in range(n)]
        for cp in cps:
            cp.start()
        for i in range(n):
            _rcopy(t[i].at[c], out[i].at[1 - c], ssem.at[i], rsem.at[i], (x, y, 1 - c)).wait_recv()
        for cp in cps:
            cp.wait_send()

    return _pcall(
        body, name="share_sibling", in_specs=[_HBM] * n, out_specs=[_HBM] * n,
        out_shape=[jax.ShapeDtypeStruct(t.shape, t.dtype) for t in ts],
        input_output_aliases={i: i for i in range(n)},
        scratch_shapes=[pltpu.SemaphoreType.DMA((n,)), pltpu.SemaphoreType.DMA((n,))],
    )(*ts)


def allreduce8(v, name):
    P, W = v.shape

    def body(v_ref, out_ref, buf, ssem, rsem):
        x, y, c, _ = _place()
        me = 4 * x + 2 * y + c
        buf[me] = v_ref[...]
        cps = []
        for k in range(1, 8):
            px = 1 - x if k & 4 else x
            py = 1 - y if k & 2 else y
            pc = 1 - c if k & 1 else c
            cp = _rcopy(buf.at[me], buf.at[me], ssem.at[k - 1], rsem.at[k - 1], (px, py, pc))
            cp.start()
            cps.append((cp, 4 * px + 2 * py + pc))
        for k, (cp, peer) in enumerate(cps):
            _rcopy(buf.at[me], buf.at[peer], ssem.at[k], rsem.at[k], (x, y, c)).wait_recv()
        for cp, _ in cps:
            cp.wait_send()
        acc = buf[0]
        for d in range(1, 8):
            acc = acc + buf[d]
        out_ref[...] = acc

    return _pcall(
        body, name=name, in_specs=[pl.BlockSpec(memory_space=pltpu.VMEM)],
        out_specs=pl.BlockSpec(memory_space=pltpu.VMEM), out_shape=jax.ShapeDtypeStruct((P, W), F32),
        scratch_shapes=[pltpu.VMEM((8, P, W), F32), pltpu.SemaphoreType.DMA((7,)), pltpu.SemaphoreType.DMA((7,))],
    )(v)


def add_half(g, rcv, cidx, name):
    _, hr, W = rcv.shape
    tr = _rows_tile(hr, W)
    nb = hr // tr

    def body(c_ref, g_ref, r_ref, o_ref):
        o_ref[...] = (g_ref[...] + r_ref[...]).astype(o_ref.dtype)

    return _pcall(
        body, name=name,
        grid_spec=pltpu.PrefetchScalarGridSpec(
            num_scalar_prefetch=1, grid=(4, nb),
            in_specs=[pl.BlockSpec((1, tr, W), lambda s, i, c: (s, c[0] * nb + i, 0)),
                      pl.BlockSpec((1, tr, W), lambda s, i, c: (s, i, 0))],
            out_specs=pl.BlockSpec((1, tr, W), lambda s, i, c: (s, i, 0))),
        out_shape=jax.ShapeDtypeStruct((4, hr, W), BF16), compiler_params=_cparams(("parallel", "parallel")),
    )(cidx, g, rcv)


def sum_chips(p, h, sc, name):
    _, hr, W = p.shape
    tr = _rows_tile(hr, W)

    def body(s_ref, c_ref, p_ref, h_ref, o_ref):
        s = s_ref[0]
        own = h_ref[0]
        f = lambda k: jnp.where(s == k, own, p_ref[k]).astype(F32)
        o_ref[0] = ((f(0) + f(1)) + f(2)) + f(3)

    return _pcall(
        body, name=name,
        grid_spec=pltpu.PrefetchScalarGridSpec(
            num_scalar_prefetch=2, grid=(hr // tr,),
            in_specs=[pl.BlockSpec((4, tr, W), lambda i, s, c: (0, i, 0)),
                      pl.BlockSpec((1, tr, W), lambda i, s, c: (s[0], i, 0))],
            out_specs=pl.BlockSpec((1, tr, W), lambda i, s, c: (c[0], i, 0))),
        out_shape=jax.ShapeDtypeStruct((2, hr, W), F32), compiler_params=_cparams(("parallel",)),
    )(sc[0], sc[1], p, h)


HI = lax.Precision.HIGHEST
BLOCK_BYTES = 3 * 512 * 1024


def _rows_tile(R, w, T=None):
    cands = [t for t in range(16, R + 1, 16) if R % t == 0 and (T is None or T % t == 0)]
    ok = [t for t in cands if t * w * 4 <= BLOCK_BYTES]
    return max(ok) if ok else min(cands)


def _gsum_exact(x):
    r = jnp.right_shift(lax.broadcasted_iota(jnp.int32, (LANE, LANE), 0), 6)
    c = jnp.right_shift(lax.broadcasted_iota(jnp.int32, (LANE, LANE), 1), 6)
    g = (r == c).astype(BF16)
    x1 = x.astype(BF16)
    r1 = x - x1.astype(F32)
    x2 = r1.astype(BF16)
    x3 = (r1 - x2.astype(F32)).astype(BF16)
    dot = lambda z: jnp.dot(z, g, preferred_element_type=F32)
    return (dot(x3) + dot(x2)) + dot(x1)


@jax.custom_vjp
def _gsum(x):
    return _gsum_exact(x)


_gsum.defvjp(lambda x: (_gsum_exact(x), None), lambda _, ct: (_gsum_exact(ct),))


def _rms(x, g):
    return x * lax.rsqrt(jnp.mean(x * x, axis=-1, keepdims=True) + NORM_EPS) * g


def _silu_mul(gate, up):
    return jax.nn.silu(gate) * up


def _shift(p, prev, mu):
    return p + mu * (prev - p)


def _lora_act(p, prev, mu):
    s = _shift(p, prev, mu)
    return jax.nn.sigmoid(s[:, :G_LORA]), jnp.tanh(s[:, G_LORA:G_LORA + LANE]), s[:, G_LORA + LANE:]


def _prep(k, lw, la, w0, a0, kk_w, ka_w):
    wpre = -jax.nn.softplus(-(w0 + lw)) - 0.5
    decay = jnp.exp(-jnp.exp(wpre))
    a = jax.nn.sigmoid(a0 + la)
    kk = k * kk_w
    kk = kk * lax.rsqrt(jnp.maximum(_gsum(kk * kk), 1e-24))
    k2 = k * (1.0 + (a - 1.0) * ka_w)
    return decay, -kk, kk * a, k2


def _post(y, r, k2, v, g, gnw, gnb, rk):
    mean = _gsum(y) * (1.0 / RWKV_HEAD)
    d = y - mean
    var = _gsum(d * d) * (1.0 / RWKV_HEAD)
    yn = d * lax.rsqrt(var + GN_EPS) * gnw + gnb
    bonus = _gsum(r * k2 * rk) * v
    return (yn + bonus) * g


def _gate_mix(ga, gb, ya, o):
    return jax.nn.sigmoid(ga) * ya + jax.nn.sigmoid(gb) * o


def _rope(x, cos, sin):
    return x * cos + pltpu.roll(x, LANE // 2, 1) * sin


def _rope_t(dy, cos, sin):
    return dy * cos + pltpu.roll(dy * sin, LANE // 2, 1)


def _mla_pre(cq, ckv, kpe, cos, sin, qw, kvw):
    return _rms(cq, qw), _rms(ckv, kvw), _rope(kpe, cos, sin)


def _mla_pre_bwd(cq, ckv, dcqn, dckvn, dkr, cos, sin, qw, kvw):
    _, pull = jax.vjp(lambda a, b, c, d: (_rms(a, c), _rms(b, d)), cq, ckv, qw, kvw)
    dcq, dckv, dqw, dkvw = pull((dcqn, dckvn))
    return dcq, dckv, _rope_t(dkr, cos, sin), dqw, dkvw


WEIGHTS = ['meta_tokens', 'ffn1_norm', 'ffn1_w_gate', 'ffn1_w_up', 'ffn1_w_down', 'mix_norm', 'w_in', 'tm_mu', 'w0',
           'w_up', 'a0', 'a_up', 'g_up', 'k_k', 'k_a', 'r_k', 'gn_w', 'gn_b', 'q_norm', 'w_uq', 'kv_norm', 'w_ukv',
           'w_out', 'ffn2_norm', 'ffn2_w_gate', 'ffn2_w_up', 'ffn2_w_down', 'final_norm']
SHARD_AXIS = {'meta_tokens': 1, 'ffn1_w_gate': 1, 'ffn1_w_up': 1, 'ffn1_w_down': 0, 'w_in': 1, 'w_up': 1, 'a_up': 1,
              'g_up': 1, 'w_uq': 1, 'w_ukv': 1, 'w_out': 0, 'ffn2_w_gate': 1, 'ffn2_w_up': 1, 'ffn2_w_down': 0}
GATHERED = [n for n in WEIGHTS if n in SHARD_AXIS and n != 'meta_tokens']
FFN_IN = ('ffn1_w_gate', 'ffn1_w_up', 'ffn2_w_gate', 'ffn2_w_up')
SMALL = [n for n in WEIGHTS if n not in SHARD_AXIS]


def _to2d(a):
    if a.ndim == 1:
        return a.reshape(1, -1)
    if a.ndim == 3:
        return a.reshape(a.shape[0] * a.shape[1], a.shape[2]) if a.shape[0] == 1 and a.shape[1] > 64 else a.reshape(1, -1)
    return a


def _unpack(flat, shapes):
    out, off = [], 0
    for shp in shapes:
        n = shp[0] * shp[1]
        out.append(flat[off:off + n].reshape(shp))
        off += n
    return out


def _adamw(w, g, m, v):
    m = ADAM_B1 * m + (1.0 - ADAM_B1) * g
    v = ADAM_B2 * v + (1.0 - ADAM_B2) * jnp.square(g)
    m_hat = m / (1.0 - ADAM_B1 ** ADAM_STEP)
    v_hat = v / (1.0 - ADAM_B2 ** ADAM_STEP)
    delta = -ADAM_LR * (m_hat / (jnp.sqrt(v_hat) + ADAM_EPS) + ADAM_WD * w)
    return delta, m, v


def adamw(w, g, m, v, name):
    R, C = w.shape
    if R % 16 == 0 and R > 16:
        tm = _rows_tile(R, C)
    else:
        tm = R
    return rowwise(_adamw, [(w, C, 0), (g, C, 0), (m, C, 0), (v, C, 0)], [], [(C, C, F32)] * 3, tm=tm, name=name)


def _step(a):
    x = a['x']
    Bl, S, D = x.shape
    T = S + N_META
    R, RS = Bl * T, Bl * S
    Hr, Hm = D // RWKV_HEAD, D // LANE
    w2 = {n: _to2d(a[n]) for n in WEIGHTS}
    m2 = {n: _to2d(a['m_' + n]) for n in WEIGHTS}
    v2 = {n: _to2d(a['v_' + n]) for n in WEIGHTS}
    xi, yi, ci = lax.axis_index("x"), lax.axis_index("y"), lax.axis_index("c")
    chip = 2 * xi + yi

    i0 = jnp.zeros((), jnp.int32)
    chip32 = chip.astype(jnp.int32)

    def own_slot(w):
        return lax.dynamic_update_slice(jnp.zeros((4,) + w.shape, BF16), w.astype(BF16)[None], (chip32, i0, i0))

    stk = dict(zip(GATHERED, ag_weights([own_slot(w2[n]) for n in GATHERED])))

    def unstack(z, axis):
        return z.reshape(4 * z.shape[1], z.shape[2]) if axis == 0 else jnp.concatenate([z[s] for s in range(4)], axis=1)

    full = {n: unstack(stk[n], SHARD_AXIS[n]) for n in GATHERED if n not in FFN_IN}
    mt = w2['meta_tokens']
    mcols = mt.shape[1]
    mt_z = lax.dynamic_update_slice(jnp.zeros((N_META, D), F32), 0.5 * mt, (jnp.zeros((), jnp.int32), (chip * mcols).astype(jnp.int32)))
    meta_full = allreduce8(mt_z.reshape(-1, LANE), "gather_meta").reshape(N_META, D)

    F = 4 * stk['ffn1_w_gate'].shape[2]
    win = full['w_in']
    o = 3 * D
    c_xw, c_xa, c_xg = win[:, o:o + W_LORA], win[:, o + W_LORA:o + 2 * W_LORA], win[:, o + 2 * W_LORA:o + 2 * W_LORA + G_LORA]
    o += 2 * W_LORA + G_LORA
    c_cq, c_ckv, c_kpe = win[:, o:o + Q_LORA], win[:, o + Q_LORA:o + Q_LORA + KV_LORA], win[:, o + Q_LORA + KV_LORA:o + Q_LORA + KV_LORA + ROPE_DIM]
    o += Q_LORA + KV_LORA + ROPE_DIM
    c_ga, c_gb = win[:, o:o + D], win[:, o + D:o + 2 * D]
    zc = lambda n: jnp.zeros((D, n), BF16)
    half = ROPE_DIM // 2
    NP0 = 5 * D + 512 + Q_LORA + KV_LORA + LANE
    NP = -(-NP0 // 512) * 512
    win_p = jnp.concatenate([win[:, :3 * D], c_ga, c_gb, c_xg, c_xw, zc(LANE - W_LORA), c_xa, zc(LANE - A_LORA), c_cq, c_ckv,
                             c_kpe[:, :half], zc(half), c_kpe[:, half:], zc(half), zc(NP - NP0)], axis=1)
    O_GA, O_GB, O_L, O_CQ, O_CKV, O_KPE = 3 * D, 4 * D, 5 * D, 5 * D + 512, 5 * D + 512 + Q_LORA, 5 * D + 512 + Q_LORA + KV_LORA
    zr = lambda n: jnp.zeros((n, D), BF16)
    w_up_p = jnp.concatenate([full['w_up'], zr(LANE - W_LORA)], axis=0)
    a_up_p = jnp.concatenate([full['a_up'], zr(LANE - A_LORA)], axis=0)
    g_up = full['g_up']
    wuq = full['w_uq'].reshape(Q_LORA, Hm, QK_DIM)
    zq = jnp.zeros((Q_LORA, Hm, half), BF16)
    wqn = wuq[:, :, :NOPE_DIM].reshape(Q_LORA, Hm * LANE)
    wqp = jnp.concatenate([wuq[:, :, NOPE_DIM:NOPE_DIM + half], zq, wuq[:, :, NOPE_DIM + half:], zq], axis=2).reshape(Q_LORA, Hm * LANE)
    wukv = full['w_ukv'].reshape(KV_LORA, Hm, NOPE_DIM + V_DIM)
    wkn = wukv[:, :, :NOPE_DIM].reshape(KV_LORA, Hm * LANE)
    wv = wukv[:, :, NOPE_DIM:].reshape(KV_LORA, Hm * LANE)
    wout = full['w_out']
    tmu = w2['tm_mu']
    mu_a = tmu[:, :3 * D]
    zm = lambda n: jnp.zeros((1, n), F32)
    mu_b = jnp.concatenate([tmu[:, 3 * D + 2 * W_LORA:], tmu[:, 3 * D:3 * D + W_LORA], zm(LANE - W_LORA),
                            tmu[:, 3 * D + W_LORA:3 * D + 2 * W_LORA], zm(LANE - A_LORA)], axis=1)
    pos = jnp.arange(T, dtype=F32)
    inv_freq = 1.0 / (ROPE_THETA ** (jnp.arange(0, ROPE_DIM, 2, dtype=F32) / ROPE_DIM))
    ang = pos[:, None] * inv_freq[None, :]
    zt = jnp.zeros((T, half), F32)
    cos_t = jnp.concatenate([jnp.cos(ang), zt, jnp.cos(ang), zt], axis=1)
    sin_t = jnp.concatenate([-jnp.sin(ang), zt, jnp.sin(ang), zt], axis=1)
    cos_q, sin_q = cos_t[N_META:], sin_t[N_META:]

    t_full = _rows_tile(R, D)
    t_512 = _rows_tile(R, 512, T)
    t_128 = _rows_tile(R, LANE, T)
    tq_full = _rows_tile(RS, D)
    tq_128 = _rows_tile(RS, LANE, S)

    def rows3(z):
        return z.reshape(Bl, T, z.shape[-1])

    def real_rows(z):
        return rows3(z)[:, N_META:].reshape(RS, z.shape[-1])

    def pad_meta(z):
        z3 = z.reshape(Bl, S, z.shape[-1])
        return jnp.concatenate([jnp.zeros((Bl, N_META, z.shape[-1]), z.dtype), z3], axis=1).reshape(R, z.shape[-1])

    def shift_down(z):
        z3 = rows3(z)
        return jnp.concatenate([jnp.zeros((Bl, 1, z.shape[-1]), z.dtype), z3[:, :-1]], axis=1).reshape(R, z.shape[-1])

    def shift_up(z):
        z3 = rows3(z)
        return jnp.concatenate([z3[:, 1:], jnp.zeros((Bl, 1, z.shape[-1]), z.dtype)], axis=1).reshape(R, z.shape[-1])

    def ffn_fwd(h, nw, wg, wu, wd, tag):
        n = rowwise(_rms, [(h, D, 0)], [(nw, D, 0)], [(D, D, BF16)], tm=t_full, name=tag + "_norm")[0]
        gate, up, act = mm_epi(n, [wg, wu], "nn", lambda g_, u_: (g_, u_, _silu_mul(g_, u_)), [], [F32, F32, BF16],
                               b_stack=True, name=tag + "_gate_up")
        return mm(act, wd, res=h, alpha=0.5, name=tag + "_down"), (h, n, gate, up, act)

    def ffn_bwd(dh2, saved, nw, wg, wu, wd, tag):
        h, n, gate, up, act = saved
        dz = rowwise(lambda d: 0.5 * d, [(dh2, 512, 0)], [], [(D, 512, BF16)], tm=t_512, ncb=D // 512, name=tag + "_dz")[0]
        d_wd = mm(act, dz, "tn", name=tag + "_dwd")
        dgate, dup = mm_epi(dz, [wd], "nt", lambda da, g_, u_: vjp_fn(_silu_mul, 2)(g_, u_, da), [gate, up],
                            [BF16, BF16], name=tag + "_dact")
        d_wg = mm(n, dgate, "tn", out_stack=True, name=tag + "_dwg")
        d_wu = mm(n, dup, "tn", out_stack=True, name=tag + "_dwu")
        dn = mm(dgate, wg, "nt", b_stack=True, name=tag + "_dn1")
        dn = mm(dup, wu, "nt", res=dn, b_stack=True, name=tag + "_dn2")

        def f(h_, dn_, dh_, nw_):
            dh, dnw = vjp_fn(_rms, 2)(h_, nw_, dn_)
            return dh + dh_, dnw

        dh, d_nw = rowwise(f, [(h, D, 0), (dn, D, 0), (dh2, D, 0)], [(nw, D, 0)], [(D, D, F32)], [(1, D, D)],
                           tm=t_full, name=tag + "_dnorm")
        return dh, d_nw, d_wg, d_wu, d_wd

    h0 = jnp.concatenate([jnp.broadcast_to(meta_full[None], (Bl, N_META, D)), x], axis=1).reshape(R, D)
    h1, sv1 = ffn_fwd(h0, w2['ffn1_norm'], stk['ffn1_w_gate'], stk['ffn1_w_up'], full['ffn1_w_down'], "ffn1")
    u = rowwise(_rms, [(h1, D, 0)], [(w2['mix_norm'], D, 0)], [(D, D, BF16)], tm=t_full, name="mix_norm")[0]
    proj = mm(u, win_p, name="proj")
    prev_a = shift_down(proj[:, :3 * D])
    prev_b = shift_down(proj[:, O_L:O_L + 512])
    ps = rowwise(_shift, [(proj, 512, 0), (prev_a, 512, 0)], [(mu_a, 512, 0)], [(3 * D, 512, F32)], tm=t_512,
                 ncb=3 * D // 512, name="shift_rkv")[0]
    sg, txw, xas = rowwise(_lora_act, [(proj, 512, O_L // 512), (prev_b, 512, 0)], [(mu_b, 512, 0)],
                           [(G_LORA, G_LORA, BF16), (LANE, LANE, BF16), (LANE, LANE, BF16)], tm=t_512, name="lora_act")
    lw = mm(txw, w_up_p, name="lora_w")
    la = mm(xas, a_up_p, name="lora_a")
    g = mm(sg, g_up, name="lora_g")
    hb = D // LANE
    par_d = lambda n: (w2[n], LANE, 0)
    decay, kn, bb, k2 = rowwise(_prep, [(ps, LANE, hb), (lw, LANE, 0), (la, LANE, 0)],
                                [par_d('w0'), par_d('a0'), par_d('k_k'), par_d('k_a')], [(D, LANE, F32)] * 4,
                                tm=t_128, ncb=hb, name="wkv_prep")

    def to_j(z):
        z = z.reshape(Bl, T, Hr, RWKV_HEAD).transpose(1, 3, 0, 2).reshape(T, RWKV_HEAD, Bl * Hr)
        return jnp.concatenate([z, z], axis=-1)

    def to_i(z):
        return z.reshape(Bl, T, Hr, 2, RWKV_HEAD // 2).transpose(1, 4, 3, 0, 2).reshape(T, RWKV_HEAD // 2, 2 * Bl * Hr)

    def from_i(z):
        return z.reshape(T, RWKV_HEAD // 2, 2, Bl, Hr).transpose(3, 0, 4, 2, 1).reshape(R, D)

    def from_j(z):
        return z.reshape(T, RWKV_HEAD, Bl, Hr).transpose(2, 0, 3, 1).reshape(R, D)

    r_s, v_s = ps[:, :D], ps[:, 2 * D:]
    jw, jkn, jb, jk, jr, iv = to_j(decay), to_j(kn), to_j(bb), to_j(k2), to_j(r_s), to_i(v_s)
    y_i, sp, sa_i = wkv_fwd(jw, jkn, jb, jk, jr, iv)
    y = from_i(y_i)
    post_rows = [(y, LANE, 0), (ps, LANE, 0), (k2, LANE, 0), (ps, LANE, 2 * hb), (g, LANE, 0)]
    post_pars = [par_d('gn_w'), par_d('gn_b'), par_d('r_k')]
    ya = rowwise(_post, post_rows, post_pars, [(D, LANE, F32)], tm=t_128, ncb=hb, name="wkv_post")[0]

    nt512 = T // t_512
    mla_rows = [(proj, Q_LORA, O_CQ // Q_LORA), (proj, KV_LORA, O_CKV // KV_LORA), (proj, LANE, O_KPE // LANE)]
    tabs = [(cos_t, LANE, 0, nt512, True), (sin_t, LANE, 0, nt512, True)]
    mla_pars = [(w2['q_norm'], Q_LORA, 0), (w2['kv_norm'], KV_LORA, 0)]
    cqn, ckvn, kpr = rowwise(_mla_pre, mla_rows + tabs, mla_pars,
                             [(Q_LORA, Q_LORA, BF16), (KV_LORA, KV_LORA, BF16), (LANE, LANE, BF16)], tm=t_512, name="mla_pre")
    cqn_r = real_rows(cqn)
    qn = mm(cqn_r, wqn, out_dtype=BF16, name="q_nope")
    qp_raw = mm(cqn_r, wqp, name="q_pe")
    ntq = S // tq_128
    qtabs = [(cos_q, LANE, 0, ntq, True), (sin_q, LANE, 0, ntq, True)]
    qp = rowwise(_rope, [(qp_raw, LANE, 0)] + qtabs, [], [(D, LANE, BF16)], tm=tq_128, ncb=Hm, name="q_rope")[0]
    knope = mm(ckvn, wkn, out_dtype=BF16, name="k_nope")
    vv = mm(ckvn, wv, out_dtype=BF16, name="v_proj")

    def pad_keys(z):
        z3 = rows3(z)
        return jnp.concatenate([z3[:, :N_META], jnp.zeros((Bl, Q_BLOCK - N_META, z.shape[-1]), z.dtype), z3[:, N_META:]], axis=1)

    def unpad_keys(z):
        return jnp.concatenate([z[:, :N_META], z[:, Q_BLOCK:]], axis=1).reshape(R, z.shape[-1])

    qn3, qp3 = qn.reshape(Bl, S, D), qp.reshape(Bl, S, D)
    knp, kpp, vp = pad_keys(knope), pad_keys(kpr), pad_keys(vv)
    o_att = pad_meta(attn_fwd(qn3, qp3, knp, kpp, vp).reshape(RS, D))
    mix_rows = [(proj, 512, O_GA // 512), (proj, 512, O_GB // 512), (ya, 512, 0), (o_att, 512, 0)]
    mix = rowwise(_gate_mix, mix_rows, [], [(D, 512, BF16)], tm=t_512, ncb=D // 512, name="gate_mix")[0]
    h2 = mm(mix, wout, res=h1, name="w_out")
    h3, sv2 = ffn_fwd(h2, w2['ffn2_norm'], stk['ffn2_w_gate'], stk['ffn2_w_up'], full['ffn2_w_down'], "ffn2")

    def loss_fb(h_, tgt, fw):
        yv, pull = jax.vjp(_rms, h_, fw)
        e = yv - tgt
        dh, dfw = pull(e * (1.0 / D))
        return dh, jnp.full((1, LANE), 0.5 / D * jnp.sum(e * e), F32), dfw

    dh3r, lossp, g_final = rowwise(loss_fb, [(real_rows(h3), D, 0), (a['loss_target'].reshape(RS, D), D, 0)],
                                   [(w2['final_norm'], D, 0)], [(D, D, F32)], [(1, LANE, LANE), (1, D, D)],
                                   tm=tq_full, name="loss")
    dh3 = pad_meta(dh3r)

    gr = {'final_norm': g_final}
    dh2, gr['ffn2_norm'], gr['ffn2_w_gate'], gr['ffn2_w_up'], gr['ffn2_w_down'] = ffn_bwd(
        dh3, sv2, w2['ffn2_norm'], stk['ffn2_w_gate'], stk['ffn2_w_up'], full['ffn2_w_down'], "ffn2")
    dh2b = rowwise(lambda d: d, [(dh2, 512, 0)], [], [(D, 512, BF16)], tm=t_512, ncb=D // 512, name="dh2_cast")[0]
    gr['w_out'] = mm(mix, dh2b, "tn", name="d_wout")
    dmix = mm(dh2b, wout, "nt", name="d_mix")
    dga, dgb, dya, do = rowwise(vjp_fn(_gate_mix, 4), mix_rows + [(dmix, 512, 0)], [],
                                [(D, 512, BF16), (D, 512, BF16), (D, 512, F32), (D, 512, BF16)], tm=t_512, ncb=D // 512,
                                name="d_gate_mix")
    dqn, dqp, dknp, dvp, dkpp = attn_bwd(qn3, qp3, knp, kpp, vp, real_rows(do).reshape(Bl, S, D))
    dqn2 = dqn.reshape(RS, D)
    dqp_raw = rowwise(_rope_t, [(dqp.reshape(RS, D), LANE, 0)] + qtabs, [], [(D, LANE, BF16)], tm=tq_128, ncb=Hm,
                      name="d_q_rope")[0]
    d_wqn = mm(cqn_r, dqn2, "tn", name="d_wqn")
    d_wqp = mm(cqn_r, dqp_raw, "tn", name="d_wqp")
    dcqn = mm(dqn2, wqn, "nt", name="d_cqn1")
    dcqn = pad_meta(mm(dqp_raw, wqp, "nt", res=dcqn, name="d_cqn2"))
    dkn2, dv2, dkp2 = unpad_keys(dknp), unpad_keys(dvp), unpad_keys(dkpp)
    d_wkn = mm(ckvn, dkn2, "tn", name="d_wkn")
    d_wv = mm(ckvn, dv2, "tn", name="d_wv")
    dckvn = mm(dkn2, wkn, "nt", name="d_ckvn1")
    dckvn = mm(dv2, wv, "nt", res=dckvn, name="d_ckvn2")
    dcq, dckv, dkpe, gr['q_norm'], gr['kv_norm'] = rowwise(
        _mla_pre_bwd, mla_rows[:2] + [(dcqn, Q_LORA, 0), (dckvn, KV_LORA, 0), (dkp2, LANE, 0)] + tabs, mla_pars,
        [(Q_LORA, Q_LORA, BF16), (KV_LORA, KV_LORA, BF16), (LANE, LANE, BF16)], [(1, Q_LORA, Q_LORA), (1, KV_LORA, KV_LORA)],
        tm=t_512, name="d_mla_pre")

    def post_bwd(y_, r_, k2_, v_, g_, dya_, gnw, gnb, rk):
        return vjp_fn(_post, 8)(y_, r_, k2_, v_, g_, gnw, gnb, rk, dya_)

    dy, dr_b, dk2_b, dv_b, dg, gr['gn_w'], gr['gn_b'], gr['r_k'] = rowwise(
        post_bwd, post_rows + [(dya, LANE, 0)], post_pars,
        [(D, LANE, F32)] * 4 + [(D, LANE, BF16)], [(1, D, LANE)] * 3, tm=t_128, ncb=hb, name="d_wkv_post")
    jdw, jdkn, jdb, jdk, jdr, idv = wkv_bwd(jw, jkn, jb, jk, jr, iv, to_i(dy), sp, sa_i)
    ddecay, dkn_w, db_w, dk2_w, dr_w, dv_w = from_j(jdw), from_j(jdkn), from_j(jdb), from_j(jdk), from_j(jdr), from_i(idv)

    def prep_bwd(k_, lw_, la_, dd, dkn_, db_, dk2a, dk2b, w0, a0, kkw, kaw):
        return vjp_fn(_prep, 7)(k_, lw_, la_, w0, a0, kkw, kaw, dd, dkn_, db_, dk2a + dk2b)

    dk_s, dlw, dla, gr['w0'], gr['a0'], gr['k_k'], gr['k_a'] = rowwise(
        prep_bwd, [(ps, LANE, hb), (lw, LANE, 0), (la, LANE, 0), (ddecay, LANE, 0), (dkn_w, LANE, 0), (db_w, LANE, 0),
                   (dk2_w, LANE, 0), (dk2_b, LANE, 0)],
        [par_d('w0'), par_d('a0'), par_d('k_k'), par_d('k_a')], [(D, LANE, F32), (D, LANE, BF16), (D, LANE, BF16)],
        [(1, D, LANE)] * 4, tm=t_128, ncb=hb, name="d_wkv_prep")
    d_wup = mm(txw, dlw, "tn", name="d_wup")
    dtxw = mm(dlw, w_up_p, "nt", name="d_txw")
    d_aup = mm(xas, dla, "tn", name="d_aup")
    dxa = mm(dla, a_up_p, "nt", name="d_xa")
    gr['g_up'] = mm(sg, dg, "tn", name="d_gup")
    dsg = mm(dg, g_up, "nt", name="d_sg")

    def lora_bwd(p_, prev_, dsg_, dt_, dxa_, mu_):
        return vjp_fn(_lora_act, 3)(p_, prev_, mu_, dsg_, dt_, dxa_)

    dpb, dprevb, dmu_b = rowwise(lora_bwd, [(proj, 512, O_L // 512), (prev_b, 512, 0), (dsg, G_LORA, 0), (dtxw, LANE, 0),
                                            (dxa, LANE, 0)], [(mu_b, 512, 0)], [(512, 512, F32)] * 2, [(1, 512, 512)],
                                 tm=t_512, name="d_lora_act")

    def shift_bwd2(p_, prev_, c1, c2, mu_):
        return vjp_fn(_shift, 3)(p_, prev_, mu_, c1 + c2)

    def shift_bwd1(p_, prev_, c1, mu_):
        return vjp_fn(_shift, 3)(p_, prev_, mu_, c1)

    def shift_back(sec, cts):
        nb = D // 512
        f = shift_bwd2 if len(cts) == 2 else shift_bwd1
        return rowwise(f, [(proj, 512, sec * nb), (prev_a, 512, sec * nb)] + [(c, 512, 0) for c in cts],
                       [(mu_a, 512, sec * nb)], [(D, 512, F32)] * 2, [(1, D, 512)], tm=t_512, ncb=nb, name=f"d_shift{sec}")

    dp_r, dprev_r, dmu_r = shift_back(0, [dr_w, dr_b])
    dp_k, dprev_k, dmu_k = shift_back(1, [dk_s])
    dp_v, dprev_v, dmu_v = shift_back(2, [dv_w, dv_b])

    def add_cast(p_, q_):
        return p_ + q_

    def dsec(dp_, dprev_, tag):
        C = dp_.shape[1]
        return rowwise(add_cast, [(dp_, 512, 0), (shift_up(dprev_), 512, 0)], [], [(C, 512, BF16)], tm=t_512, ncb=C // 512,
                       name="d_sec_" + tag)[0]

    zpad = jnp.zeros((R, NP - NP0), BF16)
    dproj = jnp.concatenate([dsec(dp_r, dprev_r, "r"), dsec(dp_k, dprev_k, "k"), dsec(dp_v, dprev_v, "v"), dga, dgb,
                             dsec(dpb, dprevb, "l"), dcq, dckv, dkpe, zpad], axis=1)
    d_win_p = mm(u, dproj, "tn", name="d_win")
    du = mm(dproj, win_p, "nt", name="d_u")

    def norm_bwd(h_, dn_, dh_, nw_):
        dh, dnw = vjp_fn(_rms, 2)(h_, nw_, dn_)
        return dh + dh_, dnw

    dh1, gr['mix_norm'] = rowwise(norm_bwd, [(h1, D, 0), (du, D, 0), (dh2, D, 0)], [(w2['mix_norm'], D, 0)],
                                  [(D, D, F32)], [(1, D, D)], tm=t_full, name="d_mix_norm")
    dh0, gr['ffn1_norm'], gr['ffn1_w_gate'], gr['ffn1_w_up'], gr['ffn1_w_down'] = ffn_bwd(
        dh1, sv1, w2['ffn1_norm'], stk['ffn1_w_gate'], stk['ffn1_w_up'], full['ffn1_w_down'], "ffn1")
    dh0_3 = rows3(dh0)
    grad_x = dh0_3[:, N_META:]
    gr['meta_tokens'] = jnp.sum(dh0_3[:, :N_META], axis=0)

    gr['w_in'] = jnp.concatenate([
        d_win_p[:, :3 * D], d_win_p[:, O_L + G_LORA:O_L + G_LORA + W_LORA], d_win_p[:, O_L + G_LORA + LANE:O_L + G_LORA + LANE + A_LORA],
        d_win_p[:, O_L:O_L + G_LORA], d_win_p[:, O_CQ:O_CQ + Q_LORA], d_win_p[:, O_CKV:O_CKV + KV_LORA],
        d_win_p[:, O_KPE:O_KPE + half], d_win_p[:, O_KPE + 2 * half:O_KPE + 3 * half], d_win_p[:, O_GA:O_GA + 2 * D]], axis=1)
    gr['tm_mu'] = jnp.concatenate([dmu_r, dmu_k, dmu_v, dmu_b[:, G_LORA:G_LORA + W_LORA],
                                   dmu_b[:, G_LORA + LANE:G_LORA + LANE + A_LORA], dmu_b[:, :G_LORA]], axis=1)
    gr['w_up'], gr['a_up'] = d_wup[:W_LORA], d_aup[:A_LORA]
    dq3n, dq3p = d_wqn.reshape(Q_LORA, Hm, LANE), d_wqp.reshape(Q_LORA, Hm, LANE)
    gr['w_uq'] = jnp.concatenate([dq3n, dq3p[:, :, :half], dq3p[:, :, 2 * half:3 * half]], axis=2).reshape(Q_LORA, Hm * QK_DIM)
    gr['w_ukv'] = jnp.concatenate([d_wkn.reshape(KV_LORA, Hm, LANE), d_wv.reshape(KV_LORA, Hm, LANE)], axis=2).reshape(
        KV_LORA, Hm * (NOPE_DIM + V_DIM))

    def stacked(g, axis):
        if g.ndim == 3:
            return g
        if axis == 0:
            return g.reshape(4, g.shape[0] // 4, g.shape[1])
        return g.reshape(g.shape[0], 4, g.shape[1] // 4).transpose(1, 0, 2)

    cidx = ci.reshape(1).astype(jnp.int32)
    gs = [stacked(gr[n], SHARD_AXIS[n]) for n in GATHERED]
    got = swap_halves(gs)
    hs = [add_half(g, r, cidx, "add_half_" + n) for n, g, r in zip(GATHERED, gs, got)]
    sc = (chip32.reshape(1), cidx)
    ts = [sum_chips(p, h, sc, "sum_chips_" + n) for n, p, h in zip(GATHERED, rs_chips(hs), hs)]
    g_shard = {n: z.reshape(w2[n].shape) for n, z in zip(GATHERED, share_sibling(ts))}
    small = jnp.concatenate([gr[n].reshape(-1) for n in SMALL] + [gr['meta_tokens'].reshape(-1), lossp.reshape(-1)])
    ns = small.shape[0]
    nsp = -(-ns // (8 * LANE)) * 8 * LANE
    small_sum = allreduce8(jnp.pad(small, (0, nsp - ns)).reshape(-1, LANE), "allreduce_small").reshape(-1)
    g_small = dict(zip(SMALL + ['meta_full'], _unpack(small_sum, [w2[n].shape for n in SMALL] + [(N_META, D)])))
    g_shard['meta_tokens'] = lax.dynamic_slice(
        g_small['meta_full'], (jnp.zeros((), jnp.int32), (chip * mcols).astype(jnp.int32)), (N_META, mcols))
    loss = small_sum[ns - LANE]

    grads, deltas, new_m, new_v = [], [], [], []
    for n in WEIGHTS:
        gw = g_shard[n] if n in g_shard else g_small[n]
        d_, m_, v_ = adamw(w2[n], gw, m2[n], v2[n], "adamw_" + n)
        shp = a[n].shape
        grads.append(gw.reshape(shp))
        deltas.append(d_.reshape(shp))
        new_m.append(m_.reshape(shp))
        new_v.append(v_.reshape(shp))
    return (loss, grad_x, *grads, *deltas, *new_m, *new_v)


def kernel(x, meta_tokens, ffn1_norm, ffn1_w_gate, ffn1_w_up, ffn1_w_down, mix_norm, w_in, tm_mu, w0, w_up, a0, a_up, g_up, k_k, k_a, r_k, gn_w, gn_b, q_norm, w_uq, kv_norm, w_ukv, w_out, ffn2_norm, ffn2_w_gate, ffn2_w_up, ffn2_w_down, final_norm, loss_target, m_meta_tokens, m_ffn1_norm, m_ffn1_w_gate, m_ffn1_w_up, m_ffn1_w_down, m_mix_norm, m_w_in, m_tm_mu, m_w0, m_w_up, m_a0, m_a_up, m_g_up, m_k_k, m_k_a, m_r_k, m_gn_w, m_gn_b, m_q_norm, m_w_uq, m_kv_norm, m_w_ukv, m_w_out, m_ffn2_norm, m_ffn2_w_gate, m_ffn2_w_up, m_ffn2_w_down, m_final_norm, v_meta_tokens, v_ffn1_norm, v_ffn1_w_gate, v_ffn1_w_up, v_ffn1_w_down, v_mix_norm, v_w_in, v_tm_mu, v_w0, v_w_up, v_a0, v_a_up, v_g_up, v_k_k, v_k_a, v_r_k, v_gn_w, v_gn_b, v_q_norm, v_w_uq, v_kv_norm, v_w_ukv, v_w_out, v_ffn2_norm, v_ffn2_w_gate, v_ffn2_w_up, v_ffn2_w_down, v_final_norm):
    return _step(dict(locals()))
```

```python
import functools
import math

import jax
import jax.numpy as jnp
import numpy as np
from jax import lax
from jax.experimental import pallas as pl
from jax.experimental.pallas import tpu as pltpu

F32 = jnp.float32
BF16 = jnp.bfloat16
MESH = pl.DeviceIdType.MESH

N_META = 16
NORM_EPS = 1e-6
RWKV_HEAD = 64
GN_EPS = RWKV_HEAD * 1e-5
W_LORA, A_LORA, G_LORA = 96, 96, 256
Q_LORA, KV_LORA = 512, 512
NOPE_DIM, ROPE_DIM, V_DIM = 128, 64, 128
QK_DIM = NOPE_DIM + ROPE_DIM
ROPE_THETA = 10000.0
Q_BLOCK = 128
ADAM_LR, ADAM_B1, ADAM_B2, ADAM_EPS, ADAM_WD, ADAM_STEP = 0.001, 0.9, 0.999, 1e-08, 0.01, 10

LANE = 128
VMEM_LIMIT = 56 * 1024 * 1024


def _pcall(body, **kw):
    return pl.pallas_call(body, **kw)


def _cparams(sem):
    return pltpu.CompilerParams(dimension_semantics=sem, vmem_limit_bytes=VMEM_LIMIT)


MM_LANE_TILE = 1536
MM_ROW_TILE = 1408


def _div_tile(n, cap, unit):
    best = None
    for t in range(unit, min(n, cap) + 1, unit):
        if n % t == 0:
            best = t
    return best if best is not None else n


def _rtile(n, pref=512):
    best = None
    for t in range(16, min(n, pref * 2) + 1, 16):
        if n % t == 0 and (best is None or abs(t - pref) < abs(best - pref)):
            best = t
    return best if best is not None else n


def mm(a, b, mode="nn", out_dtype=F32, res=None, alpha=1.0, b_stack=False, out_stack=False, name="mm"):
    cs = b.shape[-1] if b_stack else None
    bs = (b.shape[1], 4 * cs) if b_stack else b.shape
    if mode == "nn":
        (M, K), (K2, N) = a.shape, bs
    elif mode == "nt":
        (M, K), (N, K2) = a.shape, bs
    else:
        (K, M), (K2, N) = a.shape, bs
    assert K == K2, (a.shape, b.shape, mode)
    if mode == "tn":
        tm, tk = _div_tile(M, MM_ROW_TILE, LANE), _div_tile(K, 1024, 16)
    else:
        tm = _div_tile(M, MM_ROW_TILE, 16)
        tk = _div_tile(cs if (b_stack and mode == "nt") else K, MM_LANE_TILE, LANE)
    ncol = N // 4 if out_stack else (cs if (b_stack and mode == "nn") else N)
    tn = _div_tile(ncol, MM_LANE_TILE if mode != "nt" else 1024, LANE)
    nk = K // tk
    dims = {"nn": (((1,), (0,)), ((), ())), "nt": (((1,), (1,)), ((), ())), "tn": (((0,), (0,)), ((), ()))}[mode]
    direct = out_dtype == F32

    def body(*refs):
        a_ref, b_ref = refs[:2]
        r_ref = refs[2] if res is not None else None
        o_ref = refs[3] if res is not None else refs[2]
        acc = o_ref if direct else refs[-1]
        k = pl.program_id(2)

        @pl.when(k == 0)
        def _():
            acc[...] = jnp.zeros_like(acc) if res is None else r_ref[...].astype(F32)

        p = lax.dot_general(a_ref[...].astype(BF16), b_ref[...].astype(BF16), dims, preferred_element_type=F32)
        acc[...] += p if alpha == 1.0 else alpha * p

        if not direct:
            @pl.when(k == nk - 1)
            def _():
                o_ref[...] = acc[...].astype(o_ref.dtype)

    if mode == "tn":
        a_spec = pl.BlockSpec((tk, tm), lambda i, j, k: (k, i))
        b_spec = pl.BlockSpec((tk, tn), lambda i, j, k: (k, j))
    else:
        a_spec = pl.BlockSpec((tm, tk), lambda i, j, k: (i, k))
        if mode == "nn":
            if b_stack:
                nps = cs // tn
                b_spec = pl.BlockSpec((None, tk, tn), lambda i, j, k: (j // nps, k, j % nps))
            else:
                b_spec = pl.BlockSpec((tk, tn), lambda i, j, k: (k, j))
        elif b_stack:
            kps = cs // tk
            b_spec = pl.BlockSpec((None, tn, tk), lambda i, j, k: (k // kps, j, k % kps))
        else:
            b_spec = pl.BlockSpec((tn, tk), lambda i, j, k: (j, k))
    r_spec = pl.BlockSpec((tm, tn), lambda i, j, k: (i, j))
    if out_stack:
        ops = (N // 4) // tn
        o_spec = pl.BlockSpec((None, tm, tn), lambda i, j, k: (j // ops, i, j % ops))
        o_shape = jax.ShapeDtypeStruct((4, M, N // 4), out_dtype)
    else:
        o_spec, o_shape = r_spec, jax.ShapeDtypeStruct((M, N), out_dtype)
    in_specs = [a_spec, b_spec] + ([r_spec] if res is not None else [])
    args = (a, b) + ((res,) if res is not None else ())
    return _pcall(
        body, name=name, grid=(M // tm, N // tn, nk), in_specs=in_specs, out_specs=o_spec, out_shape=o_shape,
        scratch_shapes=[] if direct else [pltpu.VMEM((tm, tn), F32)],
        compiler_params=_cparams(("parallel", "parallel", "arbitrary")),
    )(*args)


MM_EPI_ROW_TILE = 704


def mm_epi(a, bs, mode, epi, extras, out_dtypes, b_stack=False, name="mm_epi"):
    b = bs[0]
    cs = b.shape[-1] if b_stack else None
    bshape = (b.shape[1], 4 * cs) if b_stack else b.shape
    (M, K) = a.shape
    (K2, N) = bshape if mode == "nn" else bshape[::-1]
    assert K == K2 and mode in ("nn", "nt"), (a.shape, b.shape, mode)
    tm = _div_tile(M, MM_EPI_ROW_TILE, 16)
    tk = _div_tile(cs if (b_stack and mode == "nt") else K, MM_LANE_TILE, LANE)
    tn = _div_tile(cs if (b_stack and mode == "nn") else N, MM_LANE_TILE, LANE)
    nk, nb, ne, no = K // tk, len(bs), len(extras), len(out_dtypes)
    dims = (((1,), (0,)), ((), ())) if mode == "nn" else (((1,), (1,)), ((), ()))

    def body(*refs):
        a_ref, b_refs, e_refs = refs[0], refs[1:1 + nb], refs[1 + nb:1 + nb + ne]
        o_refs, accs = refs[1 + nb + ne:1 + nb + ne + no], refs[1 + nb + ne + no:]
        k = pl.program_id(2)

        @pl.when(k == 0)
        def _():
            for acc in accs:
                acc[...] = jnp.zeros_like(acc)

        av = a_ref[...].astype(BF16)
        for b_ref, acc in zip(b_refs, accs):
            acc[...] += lax.dot_general(av, b_ref[...].astype(BF16), dims, preferred_element_type=F32)

        @pl.when(k == nk - 1)
        def _():
            res = epi(*[acc[...] for acc in accs], *[e[...] for e in e_refs])
            for o_ref, v in zip(o_refs, res):
                o_ref[...] = v.astype(o_ref.dtype)

    a_spec = pl.BlockSpec((tm, tk), lambda i, j, k: (i, k))
    if mode == "nn":
        if b_stack:
            nps = cs // tn
            b_spec = pl.BlockSpec((None, tk, tn), lambda i, j, k: (j // nps, k, j % nps))
        else:
            b_spec = pl.BlockSpec((tk, tn), lambda i, j, k: (k, j))
    elif b_stack:
        kps = cs // tk
        b_spec = pl.BlockSpec((None, tn, tk), lambda i, j, k: (k // kps, j, k % kps))
    else:
        b_spec = pl.BlockSpec((tn, tk), lambda i, j, k: (j, k))
    o_spec = pl.BlockSpec((tm, tn), lambda i, j, k: (i, j))
    return _pcall(
        body, name=name, grid=(M // tm, N // tn, nk), in_specs=[a_spec] + [b_spec] * nb + [o_spec] * ne,
        out_specs=[o_spec] * no, out_shape=[jax.ShapeDtypeStruct((M, N), dt) for dt in out_dtypes],
        scratch_shapes=[pltpu.VMEM((tm, tn), F32)] * nb,
        compiler_params=_cparams(("parallel", "parallel", "arbitrary")),
    )(a, *bs, *extras)


def rowwise(fn, row_ins, par_ins, outs, accs=(), *, tm, ncb=1, name="rowwise"):
    R = row_ins[0][0].shape[0]
    assert R % tm == 0, (R, tm)
    nrb = R // tm
    in_specs, args = [], []
    for spec in row_ins:
        arr, w, base = spec[:3]
        mod = spec[3] if len(spec) > 3 else None
        cstep = 0 if (len(spec) > 4 and spec[4]) else 1
        if mod is None:
            in_specs.append(pl.BlockSpec((tm, w), lambda j, i, base=base, cstep=cstep: (i, base + cstep * j)))
        else:
            in_specs.append(pl.BlockSpec((tm, w), lambda j, i, base=base, mod=mod, cstep=cstep: (i % mod, base + cstep * j)))
        args.append(arr)
    for arr, w, base in par_ins:
        in_specs.append(pl.BlockSpec((arr.shape[0], w), lambda j, i, base=base: (0, base + j)))
        args.append(arr)
    out_specs, out_shape = [], []
    for cols, w, dt in outs:
        out_specs.append(pl.BlockSpec((tm, w), lambda j, i: (i, j)))
        out_shape.append(jax.ShapeDtypeStruct((R, cols), dt))
    for p, cols, w in accs:
        out_specs.append(pl.BlockSpec((p, w), lambda j, i: (0, j)))
        out_shape.append(jax.ShapeDtypeStruct((p, cols), F32))
    nin, nout, nacc = len(args), len(outs), len(accs)

    def body(*refs):
        vals = [r[...] for r in refs[:nin]]
        res = fn(*vals)
        if not isinstance(res, (tuple, list)):
            res = (res,)
        assert len(res) == nout + nacc, (len(res), nout, nacc)
        for o_ref, v in zip(refs[nin:nin + nout], res[:nout]):
            o_ref[...] = v.astype(o_ref.dtype)
        if nacc:
            i = pl.program_id(1)

            @pl.when(i == 0)
            def _():
                for a_ref in refs[nin + nout:]:
                    a_ref[...] = jnp.zeros_like(a_ref)

            for a_ref, v in zip(refs[nin + nout:], res[nout:]):
                a_ref[...] += v.astype(F32)

    r = _pcall(
        body, name=name, grid=(ncb, nrb), in_specs=in_specs, out_specs=out_specs, out_shape=out_shape,
        compiler_params=_cparams(("parallel", "arbitrary")),
    )(*args)
    return r


def vjp_fn(fwd, nprim):
    def f(*vals):
        prim, cts = vals[:nprim], vals[nprim:]
        out, pull = jax.vjp(fwd, *[p.astype(F32) for p in prim])
        if not isinstance(out, (tuple, list)):
            cts = cts[0].astype(F32)
        else:
            cts = tuple(c.astype(F32) for c in cts)
        return pull(cts)
    return f


WKV_TB = 8


def wkv_fwd(w, kn, b, k, r, v):
    T, N, L = w.shape
    NI = v.shape[1]
    tb = WKV_TB
    assert T % tb == 0 and L == v.shape[2]

    def body(w_ref, kn_ref, b_ref, k_ref, r_ref, v_ref, y_ref, sp_ref, sa_ref, s_ref):
        @pl.when(pl.program_id(0) == 0)
        def _():
            s_ref[...] = jnp.zeros_like(s_ref)

        def step(s, carry):
            W, KN, B, Kk, Rr = w_ref[s], kn_ref[s], b_ref[s], k_ref[s], r_ref[s]
            for i in range(NI):
                S = s_ref[i]
                sp_ref[s, i] = S
                sa = jnp.sum(S * KN, axis=0, keepdims=True)
                sa_ref[s, pl.ds(i, 1), :] = sa
                vi = v_ref[s, pl.ds(i, 1), :]
                Sn = S * W + sa * B + vi * Kk
                s_ref[i] = Sn
                y_ref[s, pl.ds(i, 1), :] = jnp.sum(Sn * Rr, axis=0, keepdims=True)
            return carry

        lax.fori_loop(0, tb, step, 0)

    jspec = pl.BlockSpec((tb, N, L), lambda t: (t, 0, 0))
    ispec = pl.BlockSpec((tb, NI, L), lambda t: (t, 0, 0))
    ishape = jax.ShapeDtypeStruct((T, NI, L), F32)
    return _pcall(
        body, name="wkv_fwd", grid=(T // tb,), in_specs=[jspec] * 5 + [ispec],
        out_specs=[ispec, pl.BlockSpec((tb, NI, N, L), lambda t: (t, 0, 0, 0)), ispec],
        out_shape=[ishape, jax.ShapeDtypeStruct((T, NI, N, L), F32), ishape],
        scratch_shapes=[pltpu.VMEM((NI, N, L), F32)],
        compiler_params=_cparams(("arbitrary",)),
    )(w, kn, b, k, r, v)


def wkv_bwd(w, kn, b, k, r, v, dy, sp, sa):
    T, N, L = w.shape
    NI, LH = v.shape[1], L // 2
    tb = WKV_TB
    nt = T // tb

    def body(w_ref, kn_ref, b_ref, k_ref, r_ref, v_ref, dy_ref, sp_ref, sa_ref,
             dw_ref, dkn_ref, db_ref, dk_ref, dr_ref, dv_ref, ds_ref):
        @pl.when(pl.program_id(0) == 0)
        def _():
            ds_ref[...] = jnp.zeros_like(ds_ref)

        def step(q, carry):
            s = tb - 1 - q
            W, KN, B, Kk, Rr = w_ref[s], kn_ref[s], b_ref[s], k_ref[s], r_ref[s]
            dW = jnp.zeros((N, L), F32)
            dKN, dB, dK, T1 = dW, dW, dW, dW
            al = jnp.zeros((1, L), F32)
            be = al
            for i in range(NI):
                Sp = sp_ref[s, i]
                vi = v_ref[s, pl.ds(i, 1), :]
                dyi = dy_ref[s, pl.ds(i, 1), :]
                sai = sa_ref[s, pl.ds(i, 1), :]
                dS = ds_ref[i] + dyi * Rr
                T1 = T1 + Sp * dyi
                al = al + sai * dyi
                be = be + vi * dyi
                dv_ref[s, pl.ds(i, 1), :] = jnp.sum(dS * Kk, axis=0, keepdims=True)
                dK = dK + dS * vi
                dsa = jnp.sum(dS * B, axis=0, keepdims=True)
                dB = dB + dS * sai
                dW = dW + dS * Sp
                dKN = dKN + Sp * dsa
                ds_ref[i] = dS * W + dsa * KN
            dR = W * T1 + B * al + Kk * be
            for ref, val in ((dw_ref, dW), (dkn_ref, dKN), (db_ref, dB), (dk_ref, dK), (dr_ref, dR)):
                ref[s] = (val + pltpu.roll(val, LH, 1))[:, :LH]
            return carry

        lax.fori_loop(0, tb, step, 0)

    jspec = pl.BlockSpec((tb, N, L), lambda t: (nt - 1 - t, 0, 0))
    gspec = pl.BlockSpec((tb, N, LH), lambda t: (nt - 1 - t, 0, 0))
    ispec = pl.BlockSpec((tb, NI, L), lambda t: (nt - 1 - t, 0, 0))
    gshape = jax.ShapeDtypeStruct((T, N, LH), F32)
    return _pcall(
        body, name="wkv_bwd", grid=(nt,),
        in_specs=[jspec] * 5 + [ispec, ispec, pl.BlockSpec((tb, NI, N, L), lambda t: (nt - 1 - t, 0, 0, 0)), ispec],
        out_specs=[gspec] * 5 + [ispec], out_shape=[gshape] * 5 + [jax.ShapeDtypeStruct((T, NI, L), F32)],
        scratch_shapes=[pltpu.VMEM((NI, N, L), F32)],
        compiler_params=_cparams(("arbitrary",)),
    )(w, kn, b, k, r, v, dy, sp, sa)


_NT = (((1,), (1,)), ((), ()))
_TN = (((0,), (0,)), ((), ()))
ATT_SCALE = QK_DIM ** -0.5


def _att_bias():
    col = lax.broadcasted_iota(jnp.int32, (Q_BLOCK, Q_BLOCK), 1)
    row = lax.broadcasted_iota(jnp.int32, (Q_BLOCK, Q_BLOCK), 0)
    return jnp.where(col < N_META, 0.0, -1e30).astype(F32), jnp.where(col <= row, 0.0, -1e30).astype(F32)


def _att_probs(q1, q2, kn_ref, kp_ref, s_ref, bias, L):
    s = lax.dot_general(q1, kn_ref[0, :L, :], _NT, preferred_element_type=F32)
    s = s + lax.dot_general(q2, kp_ref[0, :L, :], _NT, preferred_element_type=F32)
    s_ref[:, :L] = s * ATT_SCALE
    s_ref[:, :Q_BLOCK] += bias[0]
    s_ref[:, L - Q_BLOCK:L] += bias[1]
    s = s_ref[:, :L]
    m = jnp.max(s, axis=-1, keepdims=True)
    p = jnp.exp(s - m)
    return p / jnp.sum(p, axis=-1, keepdims=True)


def _edge_steps(nb, nh):
    b, h = pl.program_id(0), pl.program_id(1)
    return (b == 0) & (h == 0), (b == nb - 1) & (h == nh - 1)


def attn_fwd(qn, qp, kn, kp, v, bufs=()):
    B, S, HD = qn.shape
    SP = kn.shape[1]
    H = HD // LANE
    nq = S // Q_BLOCK
    n = len(bufs)

    def body(*refs):
        qn_ref, qp_ref, kn_ref, kp_ref, v_ref = refs[:5]
        o_ref = refs[5 + n]
        s_ref = refs[6 + 2 * n]
        if n:
            start, finish = _ag_steps(refs[5:5 + n], refs[6 + n:6 + 2 * n], *refs[7 + 2 * n:])
            is_first, is_last = _edge_steps(B, H)
            pl.when(is_first)(start)
        bias = _att_bias()
        for i in range(nq):
            L = Q_BLOCK * (i + 2)
            rows = pl.ds(Q_BLOCK * i, Q_BLOCK)
            p = _att_probs(qn_ref[0, rows, :], qp_ref[0, rows, :], kn_ref, kp_ref, s_ref, bias, L)
            o_ref[0, rows, :] = jnp.dot(p.astype(BF16), v_ref[0, :L, :], preferred_element_type=F32)
        if n:
            pl.when(is_last)(finish)

    qspec = pl.BlockSpec((1, S, LANE), lambda b, h: (b, 0, h))
    kspec = pl.BlockSpec((1, SP, LANE), lambda b, h: (b, 0, h))
    pspec = pl.BlockSpec((1, SP, LANE), lambda b, h: (b, 0, 0))
    res = _pcall(
        body, name="attn_fwd", grid=(B, H), in_specs=[qspec, qspec, kspec, pspec, kspec] + [_HBM] * n,
        out_specs=[qspec] + [_HBM] * n,
        out_shape=[jax.ShapeDtypeStruct((B, S, HD), F32)] + [jax.ShapeDtypeStruct(w.shape, w.dtype) for w in bufs],
        input_output_aliases={5 + i: 1 + i for i in range(n)},
        scratch_shapes=[pltpu.VMEM((Q_BLOCK, SP), F32)] + (_ag_sems(n) if n else []),
        compiler_params=_cparams(("arbitrary", "arbitrary")),
    )(qn, qp, kn, kp, v, *bufs)
    return res[0], list(res[1:])


def attn_bwd(qn, qp, kn, kp, v, do, hs=()):
    B, S, HD = qn.shape
    SP = kn.shape[1]
    H = HD // LANE
    nq = S // Q_BLOCK
    n = len(hs)

    def body(*refs):
        qn_ref, qp_ref, kn_ref, kp_ref, v_ref, do_ref = refs[:6]
        dqn_ref, dqp_ref, dkn_ref, dv_ref, dkp_ref = refs[6 + n:11 + n]
        dkn_acc, dv_acc, s_ref = refs[11 + 2 * n:14 + 2 * n]
        if n:
            start, finish = _rs_steps(refs[6:6 + n], refs[11 + n:11 + 2 * n], *refs[14 + 2 * n:])
            is_first, is_last = _edge_steps(B, H)
            pl.when(is_first)(start)

        @pl.when(pl.program_id(1) == 0)
        def _():
            dkp_ref[...] = jnp.zeros_like(dkp_ref)

        dkn_acc[...] = jnp.zeros_like(dkn_acc)
        dv_acc[...] = jnp.zeros_like(dv_acc)
        bias = _att_bias()
        for i in range(nq):
            L = Q_BLOCK * (i + 2)
            rows = pl.ds(Q_BLOCK * i, Q_BLOCK)
            q1, q2, do_i = qn_ref[0, rows, :], qp_ref[0, rows, :], do_ref[0, rows, :]
            p = _att_probs(q1, q2, kn_ref, kp_ref, s_ref, bias, L)
            dp = lax.dot_general(do_i, v_ref[0, :L, :], _NT, preferred_element_type=F32)
            ds = (p * (dp - jnp.sum(p * dp, axis=-1, keepdims=True)) * ATT_SCALE).astype(BF16)
            dqn_ref[0, rows, :] = jnp.dot(ds, kn_ref[0, :L, :], preferred_element_type=F32).astype(dqn_ref.dtype)
            dqp_ref[0, rows, :] = jnp.dot(ds, kp_ref[0, :L, :], preferred_element_type=F32)
            dkn_acc[:L, :] += lax.dot_general(ds, q1, _TN, preferred_element_type=F32)
            dkp_ref[0, :L, :] += lax.dot_general(ds, q2, _TN, preferred_element_type=F32)
            dv_acc[:L, :] += lax.dot_general(p.astype(BF16), do_i, _TN, preferred_element_type=F32)
        dkn_ref[0] = dkn_acc[...].astype(dkn_ref.dtype)
        dv_ref[0] = dv_acc[...].astype(dv_ref.dtype)
        if n:
            pl.when(is_last)(finish)

    qspec = pl.BlockSpec((1, S, LANE), lambda b, h: (b, 0, h))
    kspec = pl.BlockSpec((1, SP, LANE), lambda b, h: (b, 0, h))
    pspec = pl.BlockSpec((1, SP, LANE), lambda b, h: (b, 0, 0))
    res = _pcall(
        body, name="attn_bwd", grid=(B, H), in_specs=[qspec, qspec, kspec, pspec, kspec, qspec] + [_HBM] * n,
        out_specs=[qspec, qspec, kspec, kspec, pspec] + [_HBM] * n,
        out_shape=[jax.ShapeDtypeStruct((B, S, HD), BF16), jax.ShapeDtypeStruct((B, S, HD), F32),
                   jax.ShapeDtypeStruct((B, SP, HD), BF16), jax.ShapeDtypeStruct((B, SP, HD), BF16),
                   jax.ShapeDtypeStruct((B, SP, LANE), F32)] + [jax.ShapeDtypeStruct(h.shape, h.dtype) for h in hs],
        scratch_shapes=[pltpu.VMEM((SP, LANE), F32), pltpu.VMEM((SP, LANE), F32), pltpu.VMEM((Q_BLOCK, SP), F32)]
        + (_rs_sems(n) if n else []),
        compiler_params=_cparams(("arbitrary", "arbitrary")),
    )(qn, qp, kn, kp, v, do, *hs)
    return res[:5], list(res[5:])


_HBM = pl.BlockSpec(memory_space=pltpu.HBM)


def _place():
    x, y, c = lax.axis_index("x"), lax.axis_index("y"), lax.axis_index("c")
    chips = [(1 - x, y), (x, 1 - y), (1 - x, 1 - y)]
    return x, y, c, chips


def _rcopy(src, dst, ssem, rsem, dev):
    return pltpu.make_async_remote_copy(src_ref=src, dst_ref=dst, send_sem=ssem, recv_sem=rsem,
                                        device_id=dev, device_id_type=MESH)


def _half_rows(c, r):
    return pl.ds(pl.multiple_of(c * (r // 2), 16), r // 2)


def _ag_steps(w, out, ssem, rsem):
    n = len(w)
    x, y, c, chips = _place()
    s = 2 * x + y
    rows = [_half_rows(c, w[i].shape[1]) for i in range(n)]
    orows = [_half_rows(1 - c, w[i].shape[1]) for i in range(n)]
    first = [_rcopy(w[i].at[s, rows[i]], out[i].at[s, rows[i]], ssem.at[6 * i + j], rsem.at[6 * i + j], (px, py, c))
             for i in range(n) for j, (px, py) in enumerate(chips)]

    def start():
        for cp in first:
            cp.start()

    def finish():
        passed = []
        for j, (px, py) in enumerate(chips):
            sp = 2 * px + py
            for i in range(n):
                here = out[i].at[sp, rows[i]]
                _rcopy(here, here, ssem.at[6 * i + j], rsem.at[6 * i + j], (px, py, c)).wait_recv()
                fw = _rcopy(here, here, ssem.at[6 * i + 3 + j], rsem.at[6 * i + 3 + j], (x, y, 1 - c))
                fw.start()
                passed.append(fw)
        for j, (px, py) in enumerate(chips):
            sp = 2 * px + py
            for i in range(n):
                there = out[i].at[sp, orows[i]]
                _rcopy(there, there, ssem.at[6 * i + 3 + j], rsem.at[6 * i + 3 + j], (x, y, 1 - c)).wait_recv()
        for cp in first + passed:
            cp.wait_send()

    return start, finish


def _ag_sems(n):
    return [pltpu.SemaphoreType.DMA((6 * n,)), pltpu.SemaphoreType.DMA((6 * n,))]


def ag_weights(bufs):
    n = len(bufs)

    def body(*refs):
        start, finish = _ag_steps(refs[:n], refs[n:2 * n], *refs[2 * n:])
        start()
        finish()

    return _pcall(
        body, name="ag_weights", in_specs=[_HBM] * n, out_specs=[_HBM] * n,
        out_shape=[jax.ShapeDtypeStruct(w.shape, w.dtype) for w in bufs],
        input_output_aliases={i: i for i in range(n)}, scratch_shapes=_ag_sems(n),
    )(*bufs)


def swap_halves(gs, name):
    n = len(gs)

    def body(*refs):
        g, out = refs[:n], refs[n:2 * n]
        ssem, rsem = refs[2 * n:]
        x, y, c, _ = _place()
        cps = [_rcopy(g[i].at[:, _half_rows(1 - c, g[i].shape[1])], out[i], ssem.at[i], rsem.at[i], (x, y, 1 - c))
               for i in range(n)]
        for cp in cps:
            cp.start()
        for cp in cps:
            cp.wait()

    return _pcall(
        body, name=name, in_specs=[_HBM] * n, out_specs=[_HBM] * n,
        out_shape=[jax.ShapeDtypeStruct((4, g.shape[1] // 2, g.shape[2]), g.dtype) for g in gs],
        scratch_shapes=[pltpu.SemaphoreType.DMA((n,)), pltpu.SemaphoreType.DMA((n,))],
    )(*gs)


def _rs_steps(h, out, ssem, rsem):
    n = len(h)
    x, y, c, chips = _place()
    s = 2 * x + y
    cps = [_rcopy(h[i].at[2 * px + py], out[i].at[s], ssem.at[3 * i + j], rsem.at[3 * i + j], (px, py, c))
           for i in range(n) for j, (px, py) in enumerate(chips)]

    def start():
        for cp in cps:
            cp.start()

    def finish():
        for j, (px, py) in enumerate(chips):
            for i in range(n):
                _rcopy(h[i].at[s], out[i].at[2 * px + py], ssem.at[3 * i + j], rsem.at[3 * i + j], (px, py, c)).wait_recv()
        for cp in cps:
            cp.wait_send()

    return start, finish


def _rs_sems(n):
    return [pltpu.SemaphoreType.DMA((3 * n,)), pltpu.SemaphoreType.DMA((3 * n,))]


def rs_chips(hs):
    n = len(hs)

    def body(*refs):
        start, finish = _rs_steps(refs[:n], refs[n:2 * n], *refs[2 * n:])
        start()
        finish()

    return _pcall(
        body, name="rs_chips", in_specs=[_HBM] * n, out_specs=[_HBM] * n,
        out_shape=[jax.ShapeDtypeStruct(h.shape, h.dtype) for h in hs], scratch_shapes=_rs_sems(n),
    )(*hs)


def share_sibling(ts):
    n = len(ts)

    def body(*refs):
        t, out = refs[:n], refs[n:2 * n]
        ssem, rsem = refs[2 * n:]
        x, y, c, _ = _place()
        cps = [_rcopy(t[i].at[c], out[i].at[c], ssem.at[i], rsem.at[i], (x, y, 1 - c)) for i in range(n)]
        for cp in cps:
            cp.start()
        for i in range(n):
            _rcopy(t[i].at[c], out[i].at[1 - c], ssem.at[i], rsem.at[i], (x, y, 1 - c)).wait_recv()
        for cp in cps:
            cp.wait_send()

    return _pcall(
        body, name="share_sibling", in_specs=[_HBM] * n, out_specs=[_HBM] * n,
        out_shape=[jax.ShapeDtypeStruct(t.shape, t.dtype) for t in ts],
        input_output_aliases={i: i for i in range(n)},
        scratch_shapes=[pltpu.SemaphoreType.DMA((n,)), pltpu.SemaphoreType.DMA((n,))],
    )(*ts)


def allreduce8(v, name):
    P, W = v.shape

    def body(v_ref, out_ref, buf, ssem, rsem):
        x, y, c, _ = _place()
        me = 4 * x + 2 * y + c
        buf[me] = v_ref[...]
        cps = []
        for k in range(1, 8):
            px = 1 - x if k & 4 else x
            py = 1 - y if k & 2 else y
            pc = 1 - c if k & 1 else c
            cp = _rcopy(buf.at[me], buf.at[me], ssem.at[k - 1], rsem.at[k - 1], (px, py, pc))
            cp.start()
            cps.append((cp, 4 * px + 2 * py + pc))
        for k, (cp, peer) in enumerate(cps):
            _rcopy(buf.at[me], buf.at[peer], ssem.at[k], rsem.at[k], (x, y, c)).wait_recv()
        for cp, _ in cps:
            cp.wait_send()
        acc = buf[0]
        for d in range(1, 8):
            acc = acc + buf[d]
        out_ref[...] = acc

    return _pcall(
        body, name=name, in_specs=[pl.BlockSpec(memory_space=pltpu.VMEM)],
        out_specs=pl.BlockSpec(memory_space=pltpu.VMEM), out_shape=jax.ShapeDtypeStruct((P, W), F32),
        scratch_shapes=[pltpu.VMEM((8, P, W), F32), pltpu.SemaphoreType.DMA((7,)), pltpu.SemaphoreType.DMA((7,))],
    )(v)


def add_half(g, rcv, cidx, name):
    _, hr, W = rcv.shape
    tr = _rows_tile(hr, W)
    nb = hr // tr

    def body(c_ref, g_ref, r_ref, o_ref):
        o_ref[...] = (g_ref[...] + r_ref[...]).astype(o_ref.dtype)

    return _pcall(
        body, name=name,
        grid_spec=pltpu.PrefetchScalarGridSpec(
            num_scalar_prefetch=1, grid=(4, nb),
            in_specs=[pl.BlockSpec((1, tr, W), lambda s, i, c: (s, c[0] * nb + i, 0)),
                      pl.BlockSpec((1, tr, W), lambda s, i, c: (s, i, 0))],
            out_specs=pl.BlockSpec((1, tr, W), lambda s, i, c: (s, i, 0))),
        out_shape=jax.ShapeDtypeStruct((4, hr, W), BF16), compiler_params=_cparams(("parallel", "parallel")),
    )(cidx, g, rcv)


def sum_chips(p, h, sc, name):
    _, hr, W = p.shape
    tr = _rows_tile(hr, W)

    def body(s_ref, c_ref, p_ref, h_ref, o_ref):
        s = s_ref[0]
        own = h_ref[0]
        f = lambda k: jnp.where(s == k, own, p_ref[k]).astype(F32)
        o_ref[0] = ((f(0) + f(1)) + f(2)) + f(3)

    return _pcall(
        body, name=name,
        grid_spec=pltpu.PrefetchScalarGridSpec(
            num_scalar_prefetch=2, grid=(hr // tr,),
            in_specs=[pl.BlockSpec((4, tr, W), lambda i, s, c: (0, i, 0)),
                      pl.BlockSpec((1, tr, W), lambda i, s, c: (s[0], i, 0))],
            out_specs=pl.BlockSpec((1, tr, W), lambda i, s, c: (c[0], i, 0))),
        out_shape=jax.ShapeDtypeStruct((2, hr, W), F32), compiler_params=_cparams(("parallel",)),
    )(sc[0], sc[1], p, h)


HI = lax.Precision.HIGHEST
BLOCK_BYTES = 3 * 512 * 1024


def _rows_tile(R, w, T=None):
    cands = [t for t in range(16, R + 1, 16) if R % t == 0 and (T is None or T % t == 0)]
    ok = [t for t in cands if t * w * 4 <= BLOCK_BYTES]
    return max(ok) if ok else min(cands)


def _gsum_exact(x):
    r = jnp.right_shift(lax.broadcasted_iota(jnp.int32, (LANE, LANE), 0), 6)
    c = jnp.right_shift(lax.broadcasted_iota(jnp.int32, (LANE, LANE), 1), 6)
    g = (r == c).astype(BF16)
    x1 = x.astype(BF16)
    r1 = x - x1.astype(F32)
    x2 = r1.astype(BF16)
    x3 = (r1 - x2.astype(F32)).astype(BF16)
    dot = lambda z: jnp.dot(z, g, preferred_element_type=F32)
    return (dot(x3) + dot(x2)) + dot(x1)


@jax.custom_vjp
def _gsum(x):
    return _gsum_exact(x)


_gsum.defvjp(lambda x: (_gsum_exact(x), None), lambda _, ct: (_gsum_exact(ct),))


def _rms(x, g):
    return x * lax.rsqrt(jnp.mean(x * x, axis=-1, keepdims=True) + NORM_EPS) * g


def _silu_mul(gate, up):
    return jax.nn.silu(gate) * up


def _shift(p, prev, mu):
    return p + mu * (prev - p)


def _lora_act(p, prev, mu):
    s = _shift(p, prev, mu)
    return jax.nn.sigmoid(s[:, :G_LORA]), jnp.tanh(s[:, G_LORA:G_LORA + LANE]), s[:, G_LORA + LANE:]


def _prep(k, lw, la, w0, a0, kk_w, ka_w):
    wpre = -jax.nn.softplus(-(w0 + lw)) - 0.5
    decay = jnp.exp(-jnp.exp(wpre))
    a = jax.nn.sigmoid(a0 + la)
    kk = k * kk_w
    kk = kk * lax.rsqrt(jnp.maximum(_gsum(kk * kk), 1e-24))
    k2 = k * (1.0 + (a - 1.0) * ka_w)
    return decay, -kk, kk * a, k2


def _post(y, r, k2, v, g, gnw, gnb, rk):
    mean = _gsum(y) * (1.0 / RWKV_HEAD)
    d = y - mean
    var = _gsum(d * d) * (1.0 / RWKV_HEAD)
    yn = d * lax.rsqrt(var + GN_EPS) * gnw + gnb
    bonus = _gsum(r * k2 * rk) * v
    return (yn + bonus) * g


def _gate_mix(ga, gb, ya, o):
    return jax.nn.sigmoid(ga) * ya + jax.nn.sigmoid(gb) * o


def _rope(x, cos, sin):
    return x * cos + pltpu.roll(x, LANE // 2, 1) * sin


def _rope_t(dy, cos, sin):
    return dy * cos + pltpu.roll(dy * sin, LANE // 2, 1)


def _mla_pre(cq, ckv, kpe, cos, sin, qw, kvw):
    return _rms(cq, qw), _rms(ckv, kvw), _rope(kpe, cos, sin)


def _mla_pre_bwd(cq, ckv, dcqn, dckvn, dkr, cos, sin, qw, kvw):
    _, pull = jax.vjp(lambda a, b, c, d: (_rms(a, c), _rms(b, d)), cq, ckv, qw, kvw)
    dcq, dckv, dqw, dkvw = pull((dcqn, dckvn))
    return dcq, dckv, _rope_t(dkr, cos, sin), dqw, dkvw


WEIGHTS = ['meta_tokens', 'ffn1_norm', 'ffn1_w_gate', 'ffn1_w_up', 'ffn1_w_down', 'mix_norm', 'w_in', 'tm_mu', 'w0',
           'w_up', 'a0', 'a_up', 'g_up', 'k_k', 'k_a', 'r_k', 'gn_w', 'gn_b', 'q_norm', 'w_uq', 'kv_norm', 'w_ukv',
           'w_out', 'ffn2_norm', 'ffn2_w_gate', 'ffn2_w_up', 'ffn2_w_down', 'final_norm']
SHARD_AXIS = {'meta_tokens': 1, 'ffn1_w_gate': 1, 'ffn1_w_up': 1, 'ffn1_w_down': 0, 'w_in': 1, 'w_up': 1, 'a_up': 1,
              'g_up': 1, 'w_uq': 1, 'w_ukv': 1, 'w_out': 0, 'ffn2_w_gate': 1, 'ffn2_w_up': 1, 'ffn2_w_down': 0}
GATHERED = [n for n in WEIGHTS if n in SHARD_AXIS and n != 'meta_tokens']
FFN_IN = ('ffn1_w_gate', 'ffn1_w_up', 'ffn2_w_gate', 'ffn2_w_up')
PART_B = ['ffn2_w_gate', 'ffn2_w_up', 'ffn2_w_down', 'w_out']
PART_A = [n for n in GATHERED if n not in PART_B]
SMALL = [n for n in WEIGHTS if n not in SHARD_AXIS]


def _to2d(a):
    if a.ndim == 1:
        return a.reshape(1, -1)
    if a.ndim == 3:
        return a.reshape(a.shape[0] * a.shape[1], a.shape[2]) if a.shape[0] == 1 and a.shape[1] > 64 else a.reshape(1, -1)
    return a


def _unpack(flat, shapes):
    out, off = [], 0
    for shp in shapes:
        n = shp[0] * shp[1]
        out.append(flat[off:off + n].reshape(shp))
        off += n
    return out


def _adamw(w, g, m, v):
    m = ADAM_B1 * m + (1.0 - ADAM_B1) * g
    v = ADAM_B2 * v + (1.0 - ADAM_B2) * jnp.square(g)
    m_hat = m / (1.0 - ADAM_B1 ** ADAM_STEP)
    v_hat = v / (1.0 - ADAM_B2 ** ADAM_STEP)
    delta = -ADAM_LR * (m_hat / (jnp.sqrt(v_hat) + ADAM_EPS) + ADAM_WD * w)
    return delta, m, v


def adamw(w, g, m, v, name):
    R, C = w.shape
    if R % 16 == 0 and R > 16:
        tm = _rows_tile(R, C)
    else:
        tm = R
    return rowwise(_adamw, [(w, C, 0), (g, C, 0), (m, C, 0), (v, C, 0)], [], [(C, C, F32)] * 3, tm=tm, name=name)


def _step(a):
    x = a['x']
    Bl, S, D = x.shape
    T = S + N_META
    R, RS = Bl * T, Bl * S
    Hr, Hm = D // RWKV_HEAD, D // LANE
    w2 = {n: _to2d(a[n]) for n in WEIGHTS}
    m2 = {n: _to2d(a['m_' + n]) for n in WEIGHTS}
    v2 = {n: _to2d(a['v_' + n]) for n in WEIGHTS}
    xi, yi, ci = lax.axis_index("x"), lax.axis_index("y"), lax.axis_index("c")
    chip = 2 * xi + yi

    i0 = jnp.zeros((), jnp.int32)
    chip32 = chip.astype(jnp.int32)

    def own_slot(w):
        return lax.dynamic_update_slice(jnp.zeros((4,) + w.shape, BF16), w.astype(BF16)[None], (chip32, i0, i0))

    stk = dict(zip(PART_A, ag_weights([own_slot(w2[n]) for n in PART_A])))

    def unstack(z, axis):
        return z.reshape(4 * z.shape[1], z.shape[2]) if axis == 0 else jnp.concatenate([z[s] for s in range(4)], axis=1)

    full = {n: unstack(stk[n], SHARD_AXIS[n]) for n in PART_A if n not in FFN_IN}
    mt = w2['meta_tokens']
    mcols = mt.shape[1]
    mt_z = lax.dynamic_update_slice(jnp.zeros((N_META, D), F32), 0.5 * mt, (jnp.zeros((), jnp.int32), (chip * mcols).astype(jnp.int32)))
    meta_full = allreduce8(mt_z.reshape(-1, LANE), "gather_meta").reshape(N_META, D)

    F = 4 * stk['ffn1_w_gate'].shape[2]
    win = full['w_in']
    o = 3 * D
    c_xw, c_xa, c_xg = win[:, o:o + W_LORA], win[:, o + W_LORA:o + 2 * W_LORA], win[:, o + 2 * W_LORA:o + 2 * W_LORA + G_LORA]
    o += 2 * W_LORA + G_LORA
    c_cq, c_ckv, c_kpe = win[:, o:o + Q_LORA], win[:, o + Q_LORA:o + Q_LORA + KV_LORA], win[:, o + Q_LORA + KV_LORA:o + Q_LORA + KV_LORA + ROPE_DIM]
    o += Q_LORA + KV_LORA + ROPE_DIM
    c_ga, c_gb = win[:, o:o + D], win[:, o + D:o + 2 * D]
    zc = lambda n: jnp.zeros((D, n), BF16)
    half = ROPE_DIM // 2
    NP0 = 5 * D + 512 + Q_LORA + KV_LORA + LANE
    NP = -(-NP0 // 512) * 512
    win_p = jnp.concatenate([win[:, :3 * D], c_ga, c_gb, c_xg, c_xw, zc(LANE - W_LORA), c_xa, zc(LANE - A_LORA), c_cq, c_ckv,
                             c_kpe[:, :half], zc(half), c_kpe[:, half:], zc(half), zc(NP - NP0)], axis=1)
    O_GA, O_GB, O_L, O_CQ, O_CKV, O_KPE = 3 * D, 4 * D, 5 * D, 5 * D + 512, 5 * D + 512 + Q_LORA, 5 * D + 512 + Q_LORA + KV_LORA
    zr = lambda n: jnp.zeros((n, D), BF16)
    w_up_p = jnp.concatenate([full['w_up'], zr(LANE - W_LORA)], axis=0)
    a_up_p = jnp.concatenate([full['a_up'], zr(LANE - A_LORA)], axis=0)
    g_up = full['g_up']
    wuq = full['w_uq'].reshape(Q_LORA, Hm, QK_DIM)
    zq = jnp.zeros((Q_LORA, Hm, half), BF16)
    wqn = wuq[:, :, :NOPE_DIM].reshape(Q_LORA, Hm * LANE)
    wqp = jnp.concatenate([wuq[:, :, NOPE_DIM:NOPE_DIM + half], zq, wuq[:, :, NOPE_DIM + half:], zq], axis=2).reshape(Q_LORA, Hm * LANE)
    wukv = full['w_ukv'].reshape(KV_LORA, Hm, NOPE_DIM + V_DIM)
    wkn = wukv[:, :, :NOPE_DIM].reshape(KV_LORA, Hm * LANE)
    wv = wukv[:, :, NOPE_DIM:].reshape(KV_LORA, Hm * LANE)
    tmu = w2['tm_mu']
    mu_a = tmu[:, :3 * D]
    zm = lambda n: jnp.zeros((1, n), F32)
    mu_b = jnp.concatenate([tmu[:, 3 * D + 2 * W_LORA:], tmu[:, 3 * D:3 * D + W_LORA], zm(LANE - W_LORA),
                            tmu[:, 3 * D + W_LORA:3 * D + 2 * W_LORA], zm(LANE - A_LORA)], axis=1)
    pos = jnp.arange(T, dtype=F32)
    inv_freq = 1.0 / (ROPE_THETA ** (jnp.arange(0, ROPE_DIM, 2, dtype=F32) / ROPE_DIM))
    ang = pos[:, None] * inv_freq[None, :]
    zt = jnp.zeros((T, half), F32)
    cos_t = jnp.concatenate([jnp.cos(ang), zt, jnp.cos(ang), zt], axis=1)
    sin_t = jnp.concatenate([-jnp.sin(ang), zt, jnp.sin(ang), zt], axis=1)
    cos_q, sin_q = cos_t[N_META:], sin_t[N_META:]

    t_full = _rows_tile(R, D)
    t_512 = _rows_tile(R, 512, T)
    t_128 = _rows_tile(R, LANE, T)
    tq_full = _rows_tile(RS, D)
    tq_128 = _rows_tile(RS, LANE, S)

    def rows3(z):
        return z.reshape(Bl, T, z.shape[-1])

    def real_rows(z):
        return rows3(z)[:, N_META:].reshape(RS, z.shape[-1])

    def pad_meta(z):
        z3 = z.reshape(Bl, S, z.shape[-1])
        return jnp.concatenate([jnp.zeros((Bl, N_META, z.shape[-1]), z.dtype), z3], axis=1).reshape(R, z.shape[-1])

    def shift_down(z):
        z3 = rows3(z)
        return jnp.concatenate([jnp.zeros((Bl, 1, z.shape[-1]), z.dtype), z3[:, :-1]], axis=1).reshape(R, z.shape[-1])

    def shift_up(z):
        z3 = rows3(z)
        return jnp.concatenate([z3[:, 1:], jnp.zeros((Bl, 1, z.shape[-1]), z.dtype)], axis=1).reshape(R, z.shape[-1])

    def ffn_fwd(h, nw, wg, wu, wd, tag):
        n = rowwise(_rms, [(h, D, 0)], [(nw, D, 0)], [(D, D, BF16)], tm=t_full, name=tag + "_norm")[0]
        gate, up, act = mm_epi(n, [wg, wu], "nn", lambda g_, u_: (g_, u_, _silu_mul(g_, u_)), [], [F32, F32, BF16],
                               b_stack=True, name=tag + "_gate_up")
        return mm(act, wd, res=h, alpha=0.5, name=tag + "_down"), (h, n, gate, up, act)

    def ffn_bwd(dh2, saved, nw, wg, wu, wd, tag):
        h, n, gate, up, act = saved
        dz = rowwise(lambda d: 0.5 * d, [(dh2, 512, 0)], [], [(D, 512, BF16)], tm=t_512, ncb=D // 512, name=tag + "_dz")[0]
        d_wd = mm(act, dz, "tn", name=tag + "_dwd")
        dgate, dup = mm_epi(dz, [wd], "nt", lambda da, g_, u_: vjp_fn(_silu_mul, 2)(g_, u_, da), [gate, up],
                            [BF16, BF16], name=tag + "_dact")
        d_wg = mm(n, dgate, "tn", out_stack=True, name=tag + "_dwg")
        d_wu = mm(n, dup, "tn", out_stack=True, name=tag + "_dwu")
        dn = mm(dgate, wg, "nt", b_stack=True, name=tag + "_dn1")
        dn = mm(dup, wu, "nt", res=dn, b_stack=True, name=tag + "_dn2")

        def f(h_, dn_, dh_, nw_):
            dh, dnw = vjp_fn(_rms, 2)(h_, nw_, dn_)
            return dh + dh_, dnw

        dh, d_nw = rowwise(f, [(h, D, 0), (dn, D, 0), (dh2, D, 0)], [(nw, D, 0)], [(D, D, F32)], [(1, D, D)],
                           tm=t_full, name=tag + "_dnorm")
        return dh, d_nw, d_wg, d_wu, d_wd

    h0 = jnp.concatenate([jnp.broadcast_to(meta_full[None], (Bl, N_META, D)), x], axis=1).reshape(R, D)
    h1, sv1 = ffn_fwd(h0, w2['ffn1_norm'], stk['ffn1_w_gate'], stk['ffn1_w_up'], full['ffn1_w_down'], "ffn1")
    u = rowwise(_rms, [(h1, D, 0)], [(w2['mix_norm'], D, 0)], [(D, D, BF16)], tm=t_full, name="mix_norm")[0]
    proj = mm(u, win_p, name="proj")
    prev_a = shift_down(proj[:, :3 * D])
    prev_b = shift_down(proj[:, O_L:O_L + 512])
    ps = rowwise(_shift, [(proj, 512, 0), (prev_a, 512, 0)], [(mu_a, 512, 0)], [(3 * D, 512, F32)], tm=t_512,
                 ncb=3 * D // 512, name="shift_rkv")[0]
    sg, txw, xas = rowwise(_lora_act, [(proj, 512, O_L // 512), (prev_b, 512, 0)], [(mu_b, 512, 0)],
                           [(G_LORA, G_LORA, BF16), (LANE, LANE, BF16), (LANE, LANE, BF16)], tm=t_512, name="lora_act")
    lw = mm(txw, w_up_p, name="lora_w")
    la = mm(xas, a_up_p, name="lora_a")
    g = mm(sg, g_up, name="lora_g")
    hb = D // LANE
    par_d = lambda n: (w2[n], LANE, 0)
    decay, kn, bb, k2 = rowwise(_prep, [(ps, LANE, hb), (lw, LANE, 0), (la, LANE, 0)],
                                [par_d('w0'), par_d('a0'), par_d('k_k'), par_d('k_a')], [(D, LANE, F32)] * 4,
                                tm=t_128, ncb=hb, name="wkv_prep")

    def to_j(z):
        z = z.reshape(Bl, T, Hr, RWKV_HEAD).transpose(1, 3, 0, 2).reshape(T, RWKV_HEAD, Bl * Hr)
        return jnp.concatenate([z, z], axis=-1)

    def to_i(z):
        return z.reshape(Bl, T, Hr, 2, RWKV_HEAD // 2).transpose(1, 4, 3, 0, 2).reshape(T, RWKV_HEAD // 2, 2 * Bl * Hr)

    def from_i(z):
        return z.reshape(T, RWKV_HEAD // 2, 2, Bl, Hr).transpose(3, 0, 4, 2, 1).reshape(R, D)

    def from_j(z):
        return z.reshape(T, RWKV_HEAD, Bl, Hr).transpose(2, 0, 3, 1).reshape(R, D)

    r_s, v_s = ps[:, :D], ps[:, 2 * D:]
    jw, jkn, jb, jk, jr, iv = to_j(decay), to_j(kn), to_j(bb), to_j(k2), to_j(r_s), to_i(v_s)
    y_i, sp, sa_i = wkv_fwd(jw, jkn, jb, jk, jr, iv)
    y = from_i(y_i)
    post_rows = [(y, LANE, 0), (ps, LANE, 0), (k2, LANE, 0), (ps, LANE, 2 * hb), (g, LANE, 0)]
    post_pars = [par_d('gn_w'), par_d('gn_b'), par_d('r_k')]
    ya = rowwise(_post, post_rows, post_pars, [(D, LANE, F32)], tm=t_128, ncb=hb, name="wkv_post")[0]

    nt512 = T // t_512
    mla_rows = [(proj, Q_LORA, O_CQ // Q_LORA), (proj, KV_LORA, O_CKV // KV_LORA), (proj, LANE, O_KPE // LANE)]
    tabs = [(cos_t, LANE, 0, nt512, True), (sin_t, LANE, 0, nt512, True)]
    mla_pars = [(w2['q_norm'], Q_LORA, 0), (w2['kv_norm'], KV_LORA, 0)]
    cqn, ckvn, kpr = rowwise(_mla_pre, mla_rows + tabs, mla_pars,
                             [(Q_LORA, Q_LORA, BF16), (KV_LORA, KV_LORA, BF16), (LANE, LANE, BF16)], tm=t_512, name="mla_pre")
    cqn_r = real_rows(cqn)
    qn = mm(cqn_r, wqn, out_dtype=BF16, name="q_nope")
    qp_raw = mm(cqn_r, wqp, name="q_pe")
    ntq = S // tq_128
    qtabs = [(cos_q, LANE, 0, ntq, True), (sin_q, LANE, 0, ntq, True)]
    qp = rowwise(_rope, [(qp_raw, LANE, 0)] + qtabs, [], [(D, LANE, BF16)], tm=tq_128, ncb=Hm, name="q_rope")[0]
    knope = mm(ckvn, wkn, out_dtype=BF16, name="k_nope")
    vv = mm(ckvn, wv, out_dtype=BF16, name="v_proj")

    def pad_keys(z):
        z3 = rows3(z)
        return jnp.concatenate([z3[:, :N_META], jnp.zeros((Bl, Q_BLOCK - N_META, z.shape[-1]), z.dtype), z3[:, N_META:]], axis=1)

    def unpad_keys(z):
        return jnp.concatenate([z[:, :N_META], z[:, Q_BLOCK:]], axis=1).reshape(R, z.shape[-1])

    qn3, qp3 = qn.reshape(Bl, S, D), qp.reshape(Bl, S, D)
    knp, kpp, vp = pad_keys(knope), pad_keys(kpr), pad_keys(vv)
    o3, got_b = attn_fwd(qn3, qp3, knp, kpp, vp, [own_slot(w2[n]) for n in PART_B])
    stk.update(zip(PART_B, got_b))
    full.update({n: unstack(stk[n], SHARD_AXIS[n]) for n in PART_B if n not in FFN_IN})
    wout = full['w_out']
    o_att = pad_meta(o3.reshape(RS, D))
    mix_rows = [(proj, 512, O_GA // 512), (proj, 512, O_GB // 512), (ya, 512, 0), (o_att, 512, 0)]
    mix = rowwise(_gate_mix, mix_rows, [], [(D, 512, BF16)], tm=t_512, ncb=D // 512, name="gate_mix")[0]
    h2 = mm(mix, wout, res=h1, name="w_out")
    h3, sv2 = ffn_fwd(h2, w2['ffn2_norm'], stk['ffn2_w_gate'], stk['ffn2_w_up'], full['ffn2_w_down'], "ffn2")

    def loss_fb(h_, tgt, fw):
        yv, pull = jax.vjp(_rms, h_, fw)
        e = yv - tgt
        dh, dfw = pull(e * (1.0 / D))
        return dh, jnp.full((1, LANE), 0.5 / D * jnp.sum(e * e), F32), dfw

    dh3r, lossp, g_final = rowwise(loss_fb, [(real_rows(h3), D, 0), (a['loss_target'].reshape(RS, D), D, 0)],
                                   [(w2['final_norm'], D, 0)], [(D, D, F32)], [(1, LANE, LANE), (1, D, D)],
                                   tm=tq_full, name="loss")
    dh3 = pad_meta(dh3r)

    def stacked(g, axis):
        if g.ndim == 3:
            return g
        if axis == 0:
            return g.reshape(4, g.shape[0] // 4, g.shape[1])
        return g.reshape(g.shape[0], 4, g.shape[1] // 4).transpose(1, 0, 2)

    cidx = ci.reshape(1).astype(jnp.int32)
    gr = {'final_norm': g_final}
    dh2, gr['ffn2_norm'], gr['ffn2_w_gate'], gr['ffn2_w_up'], gr['ffn2_w_down'] = ffn_bwd(
        dh3, sv2, w2['ffn2_norm'], stk['ffn2_w_gate'], stk['ffn2_w_up'], full['ffn2_w_down'], "ffn2")
    dh2b = rowwise(lambda d: d, [(dh2, 512, 0)], [], [(D, 512, BF16)], tm=t_512, ncb=D // 512, name="dh2_cast")[0]
    gr['w_out'] = mm(mix, dh2b, "tn", name="d_wout")
    dmix = mm(dh2b, wout, "nt", name="d_mix")
    dga, dgb, dya, do = rowwise(vjp_fn(_gate_mix, 4), mix_rows + [(dmix, 512, 0)], [],
                                [(D, 512, BF16), (D, 512, BF16), (D, 512, F32), (D, 512, BF16)], tm=t_512, ncb=D // 512,
                                name="d_gate_mix")
    gs_b = [stacked(gr[n], SHARD_AXIS[n]) for n in PART_B]
    hs_b = [add_half(g_, r_, cidx, "add_half_" + n) for n, g_, r_ in zip(PART_B, gs_b, swap_halves(gs_b, "swap_halves_b"))]
    (dqn, dqp, dknp, dvp, dkpp), ps_b = attn_bwd(qn3, qp3, knp, kpp, vp, real_rows(do).reshape(Bl, S, D), hs_b)
    dqn2 = dqn.reshape(RS, D)
    dqp_raw = rowwise(_rope_t, [(dqp.reshape(RS, D), LANE, 0)] + qtabs, [], [(D, LANE, BF16)], tm=tq_128, ncb=Hm,
                      name="d_q_rope")[0]
    d_wqn = mm(cqn_r, dqn2, "tn", name="d_wqn")
    d_wqp = mm(cqn_r, dqp_raw, "tn", name="d_wqp")
    dcqn = mm(dqn2, wqn, "nt", name="d_cqn1")
    dcqn = pad_meta(mm(dqp_raw, wqp, "nt", res=dcqn, name="d_cqn2"))
    dkn2, dv2, dkp2 = unpad_keys(dknp), unpad_keys(dvp), unpad_keys(dkpp)
    d_wkn = mm(ckvn, dkn2, "tn", name="d_wkn")
    d_wv = mm(ckvn, dv2, "tn", name="d_wv")
    dckvn = mm(dkn2, wkn, "nt", name="d_ckvn1")
    dckvn = mm(dv2, wv, "nt", res=dckvn, name="d_ckvn2")
    dcq, dckv, dkpe, gr['q_norm'], gr['kv_norm'] = rowwise(
        _mla_pre_bwd, mla_rows[:2] + [(dcqn, Q_LORA, 0), (dckvn, KV_LORA, 0), (dkp2, LANE, 0)] + tabs, mla_pars,
        [(Q_LORA, Q_LORA, BF16), (KV_LORA, KV_LORA, BF16), (LANE, LANE, BF16)], [(1, Q_LORA, Q_LORA), (1, KV_LORA, KV_LORA)],
        tm=t_512, name="d_mla_pre")

    def post_bwd(y_, r_, k2_, v_, g_, dya_, gnw, gnb, rk):
        return vjp_fn(_post, 8)(y_, r_, k2_, v_, g_, gnw, gnb, rk, dya_)

    dy, dr_b, dk2_b, dv_b, dg, gr['gn_w'], gr['gn_b'], gr['r_k'] = rowwise(
        post_bwd, post_rows + [(dya, LANE, 0)], post_pars,
        [(D, LANE, F32)] * 4 + [(D, LANE, BF16)], [(1, D, LANE)] * 3, tm=t_128, ncb=hb, name="d_wkv_post")
    jdw, jdkn, jdb, jdk, jdr, idv = wkv_bwd(jw, jkn, jb, jk, jr, iv, to_i(dy), sp, sa_i)
    ddecay, dkn_w, db_w, dk2_w, dr_w, dv_w = from_j(jdw), from_j(jdkn), from_j(jdb), from_j(jdk), from_j(jdr), from_i(idv)

    def prep_bwd(k_, lw_, la_, dd, dkn_, db_, dk2a, dk2b, w0, a0, kkw, kaw):
        return vjp_fn(_prep, 7)(k_, lw_, la_, w0, a0, kkw, kaw, dd, dkn_, db_, dk2a + dk2b)

    dk_s, dlw, dla, gr['w0'], gr['a0'], gr['k_k'], gr['k_a'] = rowwise(
        prep_bwd, [(ps, LANE, hb), (lw, LANE, 0), (la, LANE, 0), (ddecay, LANE, 0), (dkn_w, LANE, 0), (db_w, LANE, 0),
                   (dk2_w, LANE, 0), (dk2_b, LANE, 0)],
        [par_d('w0'), par_d('a0'), par_d('k_k'), par_d('k_a')], [(D, LANE, F32), (D, LANE, BF16), (D, LANE, BF16)],
        [(1, D, LANE)] * 4, tm=t_128, ncb=hb, name="d_wkv_prep")
    d_wup = mm(txw, dlw, "tn", name="d_wup")
    dtxw = mm(dlw, w_up_p, "nt", name="d_txw")
    d_aup = mm(xas, dla, "tn", name="d_aup")
    dxa = mm(dla, a_up_p, "nt", name="d_xa")
    gr['g_up'] = mm(sg, dg, "tn", name="d_gup")
    dsg = mm(dg, g_up, "nt", name="d_sg")

    def lora_bwd(p_, prev_, dsg_, dt_, dxa_, mu_):
        return vjp_fn(_lora_act, 3)(p_, prev_, mu_, dsg_, dt_, dxa_)

    dpb, dprevb, dmu_b = rowwise(lora_bwd, [(proj, 512, O_L // 512), (prev_b, 512, 0), (dsg, G_LORA, 0), (dtxw, LANE, 0),
                                            (dxa, LANE, 0)], [(mu_b, 512, 0)], [(512, 512, F32)] * 2, [(1, 512, 512)],
                                 tm=t_512, name="d_lora_act")

    def shift_bwd2(p_, prev_, c1, c2, mu_):
        return vjp_fn(_shift, 3)(p_, prev_, mu_, c1 + c2)

    def shift_bwd1(p_, prev_, c1, mu_):
        return vjp_fn(_shift, 3)(p_, prev_, mu_, c1)

    def shift_back(sec, cts):
        nb = D // 512
        f = shift_bwd2 if len(cts) == 2 else shift_bwd1
        return rowwise(f, [(proj, 512, sec * nb), (prev_a, 512, sec * nb)] + [(c, 512, 0) for c in cts],
                       [(mu_a, 512, sec * nb)], [(D, 512, F32)] * 2, [(1, D, 512)], tm=t_512, ncb=nb, name=f"d_shift{sec}")

    dp_r, dprev_r, dmu_r = shift_back(0, [dr_w, dr_b])
    dp_k, dprev_k, dmu_k = shift_back(1, [dk_s])
    dp_v, dprev_v, dmu_v = shift_back(2, [dv_w, dv_b])

    def add_cast(p_, q_):
        return p_ + q_

    def dsec(dp_, dprev_, tag):
        C = dp_.shape[1]
        return rowwise(add_cast, [(dp_, 512, 0), (shift_up(dprev_), 512, 0)], [], [(C, 512, BF16)], tm=t_512, ncb=C // 512,
                       name="d_sec_" + tag)[0]

    zpad = jnp.zeros((R, NP - NP0), BF16)
    dproj = jnp.concatenate([dsec(dp_r, dprev_r, "r"), dsec(dp_k, dprev_k, "k"), dsec(dp_v, dprev_v, "v"), dga, dgb,
                             dsec(dpb, dprevb, "l"), dcq, dckv, dkpe, zpad], axis=1)
    d_win_p = mm(u, dproj, "tn", name="d_win")
    du = mm(dproj, win_p, "nt", name="d_u")

    def norm_bwd(h_, dn_, dh_, nw_):
        dh, dnw = vjp_fn(_rms, 2)(h_, nw_, dn_)
        return dh + dh_, dnw

    dh1, gr['mix_norm'] = rowwise(norm_bwd, [(h1, D, 0), (du, D, 0), (dh2, D, 0)], [(w2['mix_norm'], D, 0)],
                                  [(D, D, F32)], [(1, D, D)], tm=t_full, name="d_mix_norm")
    dh0, gr['ffn1_norm'], gr['ffn1_w_gate'], gr['ffn1_w_up'], gr['ffn1_w_down'] = ffn_bwd(
        dh1, sv1, w2['ffn1_norm'], stk['ffn1_w_gate'], stk['ffn1_w_up'], full['ffn1_w_down'], "ffn1")
    dh0_3 = rows3(dh0)
    grad_x = dh0_3[:, N_META:]
    gr['meta_tokens'] = jnp.sum(dh0_3[:, :N_META], axis=0)

    gr['w_in'] = jnp.concatenate([
        d_win_p[:, :3 * D], d_win_p[:, O_L + G_LORA:O_L + G_LORA + W_LORA], d_win_p[:, O_L + G_LORA + LANE:O_L + G_LORA + LANE + A_LORA],
        d_win_p[:, O_L:O_L + G_LORA], d_win_p[:, O_CQ:O_CQ + Q_LORA], d_win_p[:, O_CKV:O_CKV + KV_LORA],
        d_win_p[:, O_KPE:O_KPE + half], d_win_p[:, O_KPE + 2 * half:O_KPE + 3 * half], d_win_p[:, O_GA:O_GA + 2 * D]], axis=1)
    gr['tm_mu'] = jnp.concatenate([dmu_r, dmu_k, dmu_v, dmu_b[:, G_LORA:G_LORA + W_LORA],
                                   dmu_b[:, G_LORA + LANE:G_LORA + LANE + A_LORA], dmu_b[:, :G_LORA]], axis=1)
    gr['w_up'], gr['a_up'] = d_wup[:W_LORA], d_aup[:A_LORA]
    dq3n, dq3p = d_wqn.reshape(Q_LORA, Hm, LANE), d_wqp.reshape(Q_LORA, Hm, LANE)
    gr['w_uq'] = jnp.concatenate([dq3n, dq3p[:, :, :half], dq3p[:, :, 2 * half:3 * half]], axis=2).reshape(Q_LORA, Hm * QK_DIM)
    gr['w_ukv'] = jnp.concatenate([d_wkn.reshape(KV_LORA, Hm, LANE), d_wv.reshape(KV_LORA, Hm, LANE)], axis=2).reshape(
        KV_LORA, Hm * (NOPE_DIM + V_DIM))

    gs_a = [stacked(gr[n], SHARD_AXIS[n]) for n in PART_A]
    hs_a = [add_half(g_, r_, cidx, "add_half_" + n) for n, g_, r_ in zip(PART_A, gs_a, swap_halves(gs_a, "swap_halves_a"))]
    sc = (chip32.reshape(1), cidx)
    ts = [sum_chips(p, h, sc, "sum_chips_" + n)
          for n, p, h in zip(PART_A + PART_B, list(rs_chips(hs_a)) + ps_b, hs_a + hs_b)]
    g_shard = {n: z.reshape(w2[n].shape) for n, z in zip(PART_A + PART_B, share_sibling(ts))}
    small = jnp.concatenate([gr[n].reshape(-1) for n in SMALL] + [gr['meta_tokens'].reshape(-1), lossp.reshape(-1)])
    ns = small.shape[0]
    nsp = -(-ns // (8 * LANE)) * 8 * LANE
    small_sum = allreduce8(jnp.pad(small, (0, nsp - ns)).reshape(-1, LANE), "allreduce_small").reshape(-1)
    g_small = dict(zip(SMALL + ['meta_full'], _unpack(small_sum, [w2[n].shape for n in SMALL] + [(N_META, D)])))
    g_shard['meta_tokens'] = lax.dynamic_slice(
        g_small['meta_full'], (jnp.zeros((), jnp.int32), (chip * mcols).astype(jnp.int32)), (N_META, mcols))
    loss = small_sum[ns - LANE]

    grads, deltas, new_m, new_v = [], [], [], []
    for n in WEIGHTS:
        gw = g_shard[n] if n in g_shard else g_small[n]
        d_, m_, v_ = adamw(w2[n], gw, m2[n], v2[n], "adamw_" + n)
        shp = a[n].shape
        grads.append(gw.reshape(shp))
        deltas.append(d_.reshape(shp))
        new_m.append(m_.reshape(shp))
        new_v.append(v_.reshape(shp))
    return (loss, grad_x, *grads, *deltas, *new_m, *new_v)


def kernel(x, meta_tokens, ffn1_norm, ffn1_w_gate, ffn1_w_up, ffn1_w_down, mix_norm, w_in, tm_mu, w0, w_up, a0, a_up, g_up, k_k, k_a, r_k, gn_w, gn_b, q_norm, w_uq, kv_norm, w_ukv, w_out, ffn2_norm, ffn2_w_gate, ffn2_w_up, ffn2_w_down, final_norm, loss_target, m_meta_tokens, m_ffn1_norm, m_ffn1_w_gate, m_ffn1_w_up, m_ffn1_w_down, m_mix_norm, m_w_in, m_tm_mu, m_w0, m_w_up, m_a0, m_a_up, m_g_up, m_k_k, m_k_a, m_r_k, m_gn_w, m_gn_b, m_q_norm, m_w_uq, m_kv_norm, m_w_ukv, m_w_out, m_ffn2_norm, m_ffn2_w_gate, m_ffn2_w_up, m_ffn2_w_down, m_final_norm, v_meta_tokens, v_ffn1_norm, v_ffn1_w_gate, v_ffn1_w_up, v_ffn1_w_down, v_mix_norm, v_w_in, v_tm_mu, v_w0, v_w_up, v_a0, v_a_up, v_g_up, v_k_k, v_k_a, v_r_k, v_gn_w, v_gn_b, v_q_norm, v_w_uq, v_kv_norm, v_w_ukv, v_w_out, v_ffn2_norm, v_ffn2_w_gate, v_ffn2_w_up, v_ffn2_w_down, v_final_norm):
    return _step(dict(locals()))
```

```python
import functools
import math

import jax
import jax.numpy as jnp
import numpy as np
from jax import lax
from jax.experimental import pallas as pl
from jax.experimental.pallas import tpu as pltpu

F32 = jnp.float32
BF16 = jnp.bfloat16
MESH = pl.DeviceIdType.MESH

N_META = 16
NORM_EPS = 1e-6
RWKV_HEAD = 64
GN_EPS = RWKV_HEAD * 1e-5
W_LORA, A_LORA, G_LORA = 96, 96, 256
Q_LORA, KV_LORA = 512, 512
NOPE_DIM, ROPE_DIM, V_DIM = 128, 64, 128
QK_DIM = NOPE_DIM + ROPE_DIM
ROPE_THETA = 10000.0
Q_BLOCK = 128
ADAM_LR, ADAM_B1, ADAM_B2, ADAM_EPS, ADAM_WD, ADAM_STEP = 0.001, 0.9, 0.999, 1e-08, 0.01, 10

LANE = 128
VMEM_LIMIT = 56 * 1024 * 1024


def _pcall(body, **kw):
    return pl.pallas_call(body, **kw)


def _cparams(sem):
    return pltpu.CompilerParams(dimension_semantics=sem, vmem_limit_bytes=VMEM_LIMIT)


MM_LANE_TILE = 1536
MM_ROW_TILE = 1408


def _div_tile(n, cap, unit):
    best = None
    for t in range(unit, min(n, cap) + 1, unit):
        if n % t == 0:
            best = t
    return best if best is not None else n


def _rtile(n, pref=512):
    best = None
    for t in range(16, min(n, pref * 2) + 1, 16):
        if n % t == 0 and (best is None or abs(t - pref) < abs(best - pref)):
            best = t
    return best if best is not None else n


def mm(a, b, mode="nn", out_dtype=F32, res=None, alpha=1.0, b_stack=False, out_stack=False, name="mm"):
    cs = b.shape[-1] if b_stack else None
    bs = (b.shape[1], 4 * cs) if b_stack else b.shape
    if mode == "nn":
        (M, K), (K2, N) = a.shape, bs
    elif mode == "nt":
        (M, K), (N, K2) = a.shape, bs
    else:
        (K, M), (K2, N) = a.shape, bs
    assert K == K2, (a.shape, b.shape, mode)
    if mode == "tn":
        tm, tk = _div_tile(M, MM_ROW_TILE, LANE), _div_tile(K, 1024, 16)
    else:
        tm = _div_tile(M, MM_ROW_TILE, 16)
        tk = _div_tile(cs if (b_stack and mode == "nt") else K, MM_LANE_TILE, LANE)
    ncol = N // 4 if out_stack else (cs if (b_stack and mode == "nn") else N)
    tn = _div_tile(ncol, MM_LANE_TILE if mode != "nt" else 1024, LANE)
    nk = K // tk
    dims = {"nn": (((1,), (0,)), ((), ())), "nt": (((1,), (1,)), ((), ())), "tn": (((0,), (0,)), ((), ()))}[mode]
    direct = out_dtype == F32

    def body(*refs):
        a_ref, b_ref = refs[:2]
        r_ref = refs[2] if res is not None else None
        o_ref = refs[3] if res is not None else refs[2]
        acc = o_ref if direct else refs[-1]
        k = pl.program_id(2)

        @pl.when(k == 0)
        def _():
            acc[...] = jnp.zeros_like(acc) if res is None else r_ref[...].astype(F32)

        p = lax.dot_general(a_ref[...].astype(BF16), b_ref[...].astype(BF16), dims, preferred_element_type=F32)
        acc[...] += p if alpha == 1.0 else alpha * p

        if not direct:
            @pl.when(k == nk - 1)
            def _():
                o_ref[...] = acc[...].astype(o_ref.dtype)

    if mode == "tn":
        a_spec = pl.BlockSpec((tk, tm), lambda i, j, k: (k, i))
        b_spec = pl.BlockSpec((tk, tn), lambda i, j, k: (k, j))
    else:
        a_spec = pl.BlockSpec((tm, tk), lambda i, j, k: (i, k))
        if mode == "nn":
            if b_stack:
                nps = cs // tn
                b_spec = pl.BlockSpec((None, tk, tn), lambda i, j, k: (j // nps, k, j % nps))
            else:
                b_spec = pl.BlockSpec((tk, tn), lambda i, j, k: (k, j))
        elif b_stack:
            kps = cs // tk
            b_spec = pl.BlockSpec((None, tn, tk), lambda i, j, k: (k // kps, j, k % kps))
        else:
            b_spec = pl.BlockSpec((tn, tk), lambda i, j, k: (j, k))
    r_spec = pl.BlockSpec((tm, tn), lambda i, j, k: (i, j))
    if out_stack:
        ops = (N // 4) // tn
        o_spec = pl.BlockSpec((None, tm, tn), lambda i, j, k: (j // ops, i, j % ops))
        o_shape = jax.ShapeDtypeStruct((4, M, N // 4), out_dtype)
    else:
        o_spec, o_shape = r_spec, jax.ShapeDtypeStruct((M, N), out_dtype)
    in_specs = [a_spec, b_spec] + ([r_spec] if res is not None else [])
    args = (a, b) + ((res,) if res is not None else ())
    return _pcall(
        body, name=name, grid=(M // tm, N // tn, nk), in_specs=in_specs, out_specs=o_spec, out_shape=o_shape,
        scratch_shapes=[] if direct else [pltpu.VMEM((tm, tn), F32)],
        compiler_params=_cparams(("parallel", "parallel", "arbitrary")),
    )(*args)


MM_EPI_ROW_TILE = 704


def mm_epi(a, bs, mode, epi, extras, out_dtypes, b_stack=False, carry=None, name="mm_epi"):
    kind, carried = carry if carry else (None, ())
    nc = len(carried)
    b = bs[0]
    cs = b.shape[-1] if b_stack else None
    bshape = (b.shape[1], 4 * cs) if b_stack else b.shape
    (M, K) = a.shape
    (K2, N) = bshape if mode == "nn" else bshape[::-1]
    assert K == K2 and mode in ("nn", "nt"), (a.shape, b.shape, mode)
    tm = _div_tile(M, MM_EPI_ROW_TILE, 16)
    tk = _div_tile(cs if (b_stack and mode == "nt") else K, MM_LANE_TILE, LANE)
    tn = _div_tile(cs if (b_stack and mode == "nn") else N, MM_LANE_TILE, LANE)
    nk, nb, ne, no = K // tk, len(bs), len(extras), len(out_dtypes)
    dims = (((1,), (0,)), ((), ())) if mode == "nn" else (((1,), (1,)), ((), ()))

    def body(*refs):
        a_ref, b_refs, e_refs = refs[0], refs[1:1 + nb], refs[1 + nb:1 + nb + ne]
        base = 1 + nb + ne + nc
        o_refs, accs = refs[base:base + no], refs[base + no + nc:base + no + nc + nb]
        k = pl.program_id(2)
        if nc:
            steps = _ag_steps if kind == "ag" else _rs_steps
            start, finish = steps(refs[base - nc:base], refs[base + no:base + no + nc], *refs[base + no + nc + nb:])
            i, j = pl.program_id(0), pl.program_id(1)
            pl.when((i == 0) & (j == 0) & (k == 0))(start)

        @pl.when(k == 0)
        def _():
            for acc in accs:
                acc[...] = jnp.zeros_like(acc)

        av = a_ref[...].astype(BF16)
        for b_ref, acc in zip(b_refs, accs):
            acc[...] += lax.dot_general(av, b_ref[...].astype(BF16), dims, preferred_element_type=F32)

        @pl.when(k == nk - 1)
        def _():
            res = epi(*[acc[...] for acc in accs], *[e[...] for e in e_refs])
            for o_ref, v in zip(o_refs, res):
                o_ref[...] = v.astype(o_ref.dtype)

        if nc:
            pl.when((i == M // tm - 1) & (j == N // tn - 1) & (k == nk - 1))(finish)

    a_spec = pl.BlockSpec((tm, tk), lambda i, j, k: (i, k))
    if mode == "nn":
        if b_stack:
            nps = cs // tn
            b_spec = pl.BlockSpec((None, tk, tn), lambda i, j, k: (j // nps, k, j % nps))
        else:
            b_spec = pl.BlockSpec((tk, tn), lambda i, j, k: (k, j))
    elif b_stack:
        kps = cs // tk
        b_spec = pl.BlockSpec((None, tn, tk), lambda i, j, k: (k // kps, j, k % kps))
    else:
        b_spec = pl.BlockSpec((tn, tk), lambda i, j, k: (j, k))
    o_spec = pl.BlockSpec((tm, tn), lambda i, j, k: (i, j))
    sems = (_ag_sems(nc) if kind == "ag" else _rs_sems(nc)) if nc else []
    res = _pcall(
        body, name=name, grid=(M // tm, N // tn, nk), in_specs=[a_spec] + [b_spec] * nb + [o_spec] * ne + [_HBM] * nc,
        out_specs=[o_spec] * no + [_HBM] * nc,
        out_shape=[jax.ShapeDtypeStruct((M, N), dt) for dt in out_dtypes]
        + [jax.ShapeDtypeStruct(z.shape, z.dtype) for z in carried],
        input_output_aliases={1 + nb + ne + q: no + q for q in range(nc)} if kind == "ag" else {},
        scratch_shapes=[pltpu.VMEM((tm, tn), F32)] * nb + sems,
        compiler_params=_cparams(("arbitrary",) * 3 if nc else ("parallel", "parallel", "arbitrary")),
    )(a, *bs, *extras, *carried)
    return (list(res[:no]), list(res[no:])) if nc else res


def rowwise(fn, row_ins, par_ins, outs, accs=(), *, tm, ncb=1, name="rowwise"):
    R = row_ins[0][0].shape[0]
    assert R % tm == 0, (R, tm)
    nrb = R // tm
    in_specs, args = [], []
    for spec in row_ins:
        arr, w, base = spec[:3]
        mod = spec[3] if len(spec) > 3 else None
        cstep = 0 if (len(spec) > 4 and spec[4]) else 1
        if mod is None:
            in_specs.append(pl.BlockSpec((tm, w), lambda j, i, base=base, cstep=cstep: (i, base + cstep * j)))
        else:
            in_specs.append(pl.BlockSpec((tm, w), lambda j, i, base=base, mod=mod, cstep=cstep: (i % mod, base + cstep * j)))
        args.append(arr)
    for arr, w, base in par_ins:
        in_specs.append(pl.BlockSpec((arr.shape[0], w), lambda j, i, base=base: (0, base + j)))
        args.append(arr)
    out_specs, out_shape = [], []
    for cols, w, dt in outs:
        out_specs.append(pl.BlockSpec((tm, w), lambda j, i: (i, j)))
        out_shape.append(jax.ShapeDtypeStruct((R, cols), dt))
    for p, cols, w in accs:
        out_specs.append(pl.BlockSpec((p, w), lambda j, i: (0, j)))
        out_shape.append(jax.ShapeDtypeStruct((p, cols), F32))
    nin, nout, nacc = len(args), len(outs), len(accs)

    def body(*refs):
        vals = [r[...] for r in refs[:nin]]
        res = fn(*vals)
        if not isinstance(res, (tuple, list)):
            res = (res,)
        assert len(res) == nout + nacc, (len(res), nout, nacc)
        for o_ref, v in zip(refs[nin:nin + nout], res[:nout]):
            o_ref[...] = v.astype(o_ref.dtype)
        if nacc:
            i = pl.program_id(1)

            @pl.when(i == 0)
            def _():
                for a_ref in refs[nin + nout:]:
                    a_ref[...] = jnp.zeros_like(a_ref)

            for a_ref, v in zip(refs[nin + nout:], res[nout:]):
                a_ref[...] += v.astype(F32)

    r = _pcall(
        body, name=name, grid=(ncb, nrb), in_specs=in_specs, out_specs=out_specs, out_shape=out_shape,
        compiler_params=_cparams(("parallel", "arbitrary")),
    )(*args)
    return r


def vjp_fn(fwd, nprim):
    def f(*vals):
        prim, cts = vals[:nprim], vals[nprim:]
        out, pull = jax.vjp(fwd, *[p.astype(F32) for p in prim])
        if not isinstance(out, (tuple, list)):
            cts = cts[0].astype(F32)
        else:
            cts = tuple(c.astype(F32) for c in cts)
        return pull(cts)
    return f


WKV_TB = 8


def wkv_fwd(w, kn, b, k, r, v):
    T, N, L = w.shape
    NI = v.shape[1]
    tb = WKV_TB
    assert T % tb == 0 and L == v.shape[2]

    def body(w_ref, kn_ref, b_ref, k_ref, r_ref, v_ref, y_ref, sp_ref, sa_ref, s_ref):
        @pl.when(pl.program_id(0) == 0)
        def _():
            s_ref[...] = jnp.zeros_like(s_ref)

        def step(s, carry):
            W, KN, B, Kk, Rr = w_ref[s], kn_ref[s], b_ref[s], k_ref[s], r_ref[s]
            for i in range(NI):
                S = s_ref[i]
                sp_ref[s, i] = S
                sa = jnp.sum(S * KN, axis=0, keepdims=True)
                sa_ref[s, pl.ds(i, 1), :] = sa
                vi = v_ref[s, pl.ds(i, 1), :]
                Sn = S * W + sa * B + vi * Kk
                s_ref[i] = Sn
                y_ref[s, pl.ds(i, 1), :] = jnp.sum(Sn * Rr, axis=0, keepdims=True)
            return carry

        lax.fori_loop(0, tb, step, 0)

    jspec = pl.BlockSpec((tb, N, L), lambda t: (t, 0, 0))
    ispec = pl.BlockSpec((tb, NI, L), lambda t: (t, 0, 0))
    ishape = jax.ShapeDtypeStruct((T, NI, L), F32)
    return _pcall(
        body, name="wkv_fwd", grid=(T // tb,), in_specs=[jspec] * 5 + [ispec],
        out_specs=[ispec, pl.BlockSpec((tb, NI, N, L), lambda t: (t, 0, 0, 0)), ispec],
        out_shape=[ishape, jax.ShapeDtypeStruct((T, NI, N, L), F32), ishape],
        scratch_shapes=[pltpu.VMEM((NI, N, L), F32)],
        compiler_params=_cparams(("arbitrary",)),
    )(w, kn, b, k, r, v)


def wkv_bwd(w, kn, b, k, r, v, dy, sp, sa):
    T, N, L = w.shape
    NI, LH = v.shape[1], L // 2
    tb = WKV_TB
    nt = T // tb

    def body(w_ref, kn_ref, b_ref, k_ref, r_ref, v_ref, dy_ref, sp_ref, sa_ref,
             dw_ref, dkn_ref, db_ref, dk_ref, dr_ref, dv_ref, ds_ref):
        @pl.when(pl.program_id(0) == 0)
        def _():
            ds_ref[...] = jnp.zeros_like(ds_ref)

        def step(q, carry):
            s = tb - 1 - q
            W, KN, B, Kk, Rr = w_ref[s], kn_ref[s], b_ref[s], k_ref[s], r_ref[s]
            dW = jnp.zeros((N, L), F32)
            dKN, dB, dK, T1 = dW, dW, dW, dW
            al = jnp.zeros((1, L), F32)
            be = al
            for i in range(NI):
                Sp = sp_ref[s, i]
                vi = v_ref[s, pl.ds(i, 1), :]
                dyi = dy_ref[s, pl.ds(i, 1), :]
                sai = sa_ref[s, pl.ds(i, 1), :]
                dS = ds_ref[i] + dyi * Rr
                T1 = T1 + Sp * dyi
                al = al + sai * dyi
                be = be + vi * dyi
                dv_ref[s, pl.ds(i, 1), :] = jnp.sum(dS * Kk, axis=0, keepdims=True)
                dK = dK + dS * vi
                dsa = jnp.sum(dS * B, axis=0, keepdims=True)
                dB = dB + dS * sai
                dW = dW + dS * Sp
                dKN = dKN + Sp * dsa
                ds_ref[i] = dS * W + dsa * KN
            dR = W * T1 + B * al + Kk * be
            for ref, val in ((dw_ref, dW), (dkn_ref, dKN), (db_ref, dB), (dk_ref, dK), (dr_ref, dR)):
                ref[s] = (val + pltpu.roll(val, LH, 1))[:, :LH]
            return carry

        lax.fori_loop(0, tb, step, 0)

    jspec = pl.BlockSpec((tb, N, L), lambda t: (nt - 1 - t, 0, 0))
    gspec = pl.BlockSpec((tb, N, LH), lambda t: (nt - 1 - t, 0, 0))
    ispec = pl.BlockSpec((tb, NI, L), lambda t: (nt - 1 - t, 0, 0))
    gshape = jax.ShapeDtypeStruct((T, N, LH), F32)
    return _pcall(
        body, name="wkv_bwd", grid=(nt,),
        in_specs=[jspec] * 5 + [ispec, ispec, pl.BlockSpec((tb, NI, N, L), lambda t: (nt - 1 - t, 0, 0, 0)), ispec],
        out_specs=[gspec] * 5 + [ispec], out_shape=[gshape] * 5 + [jax.ShapeDtypeStruct((T, NI, L), F32)],
        scratch_shapes=[pltpu.VMEM((NI, N, L), F32)],
        compiler_params=_cparams(("arbitrary",)),
    )(w, kn, b, k, r, v, dy, sp, sa)


_NT = (((1,), (1,)), ((), ()))
_TN = (((0,), (0,)), ((), ()))
ATT_SCALE = QK_DIM ** -0.5


def _att_bias():
    col = lax.broadcasted_iota(jnp.int32, (Q_BLOCK, Q_BLOCK), 1)
    row = lax.broadcasted_iota(jnp.int32, (Q_BLOCK, Q_BLOCK), 0)
    return jnp.where(col < N_META, 0.0, -1e30).astype(F32), jnp.where(col <= row, 0.0, -1e30).astype(F32)


def _att_probs(q1, q2, kn_ref, kp_ref, s_ref, bias, L):
    s = lax.dot_general(q1, kn_ref[0, :L, :], _NT, preferred_element_type=F32)
    s = s + lax.dot_general(q2, kp_ref[0, :L, :], _NT, preferred_element_type=F32)
    s_ref[:, :L] = s * ATT_SCALE
    s_ref[:, :Q_BLOCK] += bias[0]
    s_ref[:, L - Q_BLOCK:L] += bias[1]
    s = s_ref[:, :L]
    m = jnp.max(s, axis=-1, keepdims=True)
    p = jnp.exp(s - m)
    return p / jnp.sum(p, axis=-1, keepdims=True)


def _edge_steps(nb, nh):
    b, h = pl.program_id(0), pl.program_id(1)
    return (b == 0) & (h == 0), (b == nb - 1) & (h == nh - 1)


def attn_fwd(qn, qp, kn, kp, v, bufs=()):
    B, S, HD = qn.shape
    SP = kn.shape[1]
    H = HD // LANE
    nq = S // Q_BLOCK
    n = len(bufs)

    def body(*refs):
        qn_ref, qp_ref, kn_ref, kp_ref, v_ref = refs[:5]
        o_ref = refs[5 + n]
        s_ref = refs[6 + 2 * n]
        if n:
            start, finish = _ag_steps(refs[5:5 + n], refs[6 + n:6 + 2 * n], *refs[7 + 2 * n:])
            is_first, is_last = _edge_steps(B, H)
            pl.when(is_first)(start)
        bias = _att_bias()
        for i in range(nq):
            L = Q_BLOCK * (i + 2)
            rows = pl.ds(Q_BLOCK * i, Q_BLOCK)
            p = _att_probs(qn_ref[0, rows, :], qp_ref[0, rows, :], kn_ref, kp_ref, s_ref, bias, L)
            o_ref[0, rows, :] = jnp.dot(p.astype(BF16), v_ref[0, :L, :], preferred_element_type=F32)
        if n:
            pl.when(is_last)(finish)

    qspec = pl.BlockSpec((1, S, LANE), lambda b, h: (b, 0, h))
    kspec = pl.BlockSpec((1, SP, LANE), lambda b, h: (b, 0, h))
    pspec = pl.BlockSpec((1, SP, LANE), lambda b, h: (b, 0, 0))
    res = _pcall(
        body, name="attn_fwd", grid=(B, H), in_specs=[qspec, qspec, kspec, pspec, kspec] + [_HBM] * n,
        out_specs=[qspec] + [_HBM] * n,
        out_shape=[jax.ShapeDtypeStruct((B, S, HD), F32)] + [jax.ShapeDtypeStruct(w.shape, w.dtype) for w in bufs],
        input_output_aliases={5 + i: 1 + i for i in range(n)},
        scratch_shapes=[pltpu.VMEM((Q_BLOCK, SP), F32)] + (_ag_sems(n) if n else []),
        compiler_params=_cparams(("arbitrary", "arbitrary")),
    )(qn, qp, kn, kp, v, *bufs)
    return res[0], list(res[1:])


def attn_bwd(qn, qp, kn, kp, v, do, hs=()):
    B, S, HD = qn.shape
    SP = kn.shape[1]
    H = HD // LANE
    nq = S // Q_BLOCK
    n = len(hs)

    def body(*refs):
        qn_ref, qp_ref, kn_ref, kp_ref, v_ref, do_ref = refs[:6]
        dqn_ref, dqp_ref, dkn_ref, dv_ref, dkp_ref = refs[6 + n:11 + n]
        dkn_acc, dv_acc, s_ref = refs[11 + 2 * n:14 + 2 * n]
        if n:
            start, finish = _rs_steps(refs[6:6 + n], refs[11 + n:11 + 2 * n], *refs[14 + 2 * n:])
            is_first, is_last = _edge_steps(B, H)
            pl.when(is_first)(start)

        @pl.when(pl.program_id(1) == 0)
        def _():
            dkp_ref[...] = jnp.zeros_like(dkp_ref)

        dkn_acc[...] = jnp.zeros_like(dkn_acc)
        dv_acc[...] = jnp.zeros_like(dv_acc)
        bias = _att_bias()
        for i in range(nq):
            L = Q_BLOCK * (i + 2)
            rows = pl.ds(Q_BLOCK * i, Q_BLOCK)
            q1, q2, do_i = qn_ref[0, rows, :], qp_ref[0, rows, :], do_ref[0, rows, :]
            p = _att_probs(q1, q2, kn_ref, kp_ref, s_ref, bias, L)
            dp = lax.dot_general(do_i, v_ref[0, :L, :], _NT, preferred_element_type=F32)
            ds = (p * (dp - jnp.sum(p * dp, axis=-1, keepdims=True)) * ATT_SCALE).astype(BF16)
            dqn_ref[0, rows, :] = jnp.dot(ds, kn_ref[0, :L, :], preferred_element_type=F32).astype(dqn_ref.dtype)
            dqp_ref[0, rows, :] = jnp.dot(ds, kp_ref[0, :L, :], preferred_element_type=F32)
            dkn_acc[:L, :] += lax.dot_general(ds, q1, _TN, preferred_element_type=F32)
            dkp_ref[0, :L, :] += lax.dot_general(ds, q2, _TN, preferred_element_type=F32)
            dv_acc[:L, :] += lax.dot_general(p.astype(BF16), do_i, _TN, preferred_element_type=F32)
        dkn_ref[0] = dkn_acc[...].astype(dkn_ref.dtype)
        dv_ref[0] = dv_acc[...].astype(dv_ref.dtype)
        if n:
            pl.when(is_last)(finish)

    qspec = pl.BlockSpec((1, S, LANE), lambda b, h: (b, 0, h))
    kspec = pl.BlockSpec((1, SP, LANE), lambda b, h: (b, 0, h))
    pspec = pl.BlockSpec((1, SP, LANE), lambda b, h: (b, 0, 0))
    res = _pcall(
        body, name="attn_bwd", grid=(B, H), in_specs=[qspec, qspec, kspec, pspec, kspec, qspec] + [_HBM] * n,
        out_specs=[qspec, qspec, kspec, kspec, pspec] + [_HBM] * n,
        out_shape=[jax.ShapeDtypeStruct((B, S, HD), BF16), jax.ShapeDtypeStruct((B, S, HD), F32),
                   jax.ShapeDtypeStruct((B, SP, HD), BF16), jax.ShapeDtypeStruct((B, SP, HD), BF16),
                   jax.ShapeDtypeStruct((B, SP, LANE), F32)] + [jax.ShapeDtypeStruct(h.shape, h.dtype) for h in hs],
        scratch_shapes=[pltpu.VMEM((SP, LANE), F32), pltpu.VMEM((SP, LANE), F32), pltpu.VMEM((Q_BLOCK, SP), F32)]
        + (_rs_sems(n) if n else []),
        compiler_params=_cparams(("arbitrary", "arbitrary")),
    )(qn, qp, kn, kp, v, do, *hs)
    return res[:5], list(res[5:])


_HBM = pl.BlockSpec(memory_space=pltpu.HBM)


def _place():
    x, y, c = lax.axis_index("x"), lax.axis_index("y"), lax.axis_index("c")
    chips = [(1 - x, y), (x, 1 - y), (1 - x, 1 - y)]
    return x, y, c, chips


def _rcopy(src, dst, ssem, rsem, dev):
    return pltpu.make_async_remote_copy(src_ref=src, dst_ref=dst, send_sem=ssem, recv_sem=rsem,
                                        device_id=dev, device_id_type=MESH)


def _half_rows(c, r):
    return pl.ds(pl.multiple_of(c * (r // 2), 16), r // 2)


def _ag_steps(w, out, ssem, rsem):
    n = len(w)
    x, y, c, chips = _place()
    s = 2 * x + y
    rows = [_half_rows(c, w[i].shape[1]) for i in range(n)]
    orows = [_half_rows(1 - c, w[i].shape[1]) for i in range(n)]
    first = [_rcopy(w[i].at[s, rows[i]], out[i].at[s, rows[i]], ssem.at[6 * i + j], rsem.at[6 * i + j], (px, py, c))
             for i in range(n) for j, (px, py) in enumerate(chips)]

    def start():
        for cp in first:
            cp.start()

    def finish():
        passed = []
        for j, (px, py) in enumerate(chips):
            sp = 2 * px + py
            for i in range(n):
                here = out[i].at[sp, rows[i]]
                _rcopy(here, here, ssem.at[6 * i + j], rsem.at[6 * i + j], (px, py, c)).wait_recv()
                fw = _rcopy(here, here, ssem.at[6 * i + 3 + j], rsem.at[6 * i + 3 + j], (x, y, 1 - c))
                fw.start()
                passed.append(fw)
        for j, (px, py) in enumerate(chips):
            sp = 2 * px + py
            for i in range(n):
                there = out[i].at[sp, orows[i]]
                _rcopy(there, there, ssem.at[6 * i + 3 + j], rsem.at[6 * i + 3 + j], (x, y, 1 - c)).wait_recv()
        for cp in first + passed:
            cp.wait_send()

    return start, finish


def _ag_sems(n):
    return [pltpu.SemaphoreType.DMA((6 * n,)), pltpu.SemaphoreType.DMA((6 * n,))]


def ag_weights(bufs):
    n = len(bufs)

    def body(*refs):
        start, finish = _ag_steps(refs[:n], refs[n:2 * n], *refs[2 * n:])
        start()
        finish()

    return _pcall(
        body, name="ag_weights", in_specs=[_HBM] * n, out_specs=[_HBM] * n,
        out_shape=[jax.ShapeDtypeStruct(w.shape, w.dtype) for w in bufs],
        input_output_aliases={i: i for i in range(n)}, scratch_shapes=_ag_sems(n),
    )(*bufs)


def swap_halves(gs, name):
    n = len(gs)

    def body(*refs):
        g, out = refs[:n], refs[n:2 * n]
        ssem, rsem = refs[2 * n:]
        x, y, c, _ = _place()
        cps = [_rcopy(g[i].at[:, _half_rows(1 - c, g[i].shape[1])], out[i], ssem.at[i], rsem.at[i], (x, y, 1 - c))
               for i in range(n)]
        for cp in cps:
            cp.start()
        for cp in cps:
            cp.wait()

    return _pcall(
        body, name=name, in_specs=[_HBM] * n, out_specs=[_HBM] * n,
        out_shape=[jax.ShapeDtypeStruct((4, g.shape[1] // 2, g.shape[2]), g.dtype) for g in gs],
        scratch_shapes=[pltpu.SemaphoreType.DMA((n,)), pltpu.SemaphoreType.DMA((n,))],
    )(*gs)


def _rs_steps(h, out, ssem, rsem):
    n = len(h)
    x, y, c, chips = _place()
    s = 2 * x + y
    cps = [_rcopy(h[i].at[2 * px + py], out[i].at[s], ssem.at[3 * i + j], rsem.at[3 * i + j], (px, py, c))
           for i in range(n) for j, (px, py) in enumerate(chips)]

    def start():
        for cp in cps:
            cp.start()

    def finish():
        for j, (px, py) in enumerate(chips):
            for i in range(n):
                _rcopy(h[i].at[s], out[i].at[2 * px + py], ssem.at[3 * i + j], rsem.at[3 * i + j], (px, py, c)).wait_recv()
        for cp in cps:
            cp.wait_send()

    return start, finish


def _rs_sems(n):
    return [pltpu.SemaphoreType.DMA((3 * n,)), pltpu.SemaphoreType.DMA((3 * n,))]


def rs_chips(hs):
    n = len(hs)

    def body(*refs):
        start, finish = _rs_steps(refs[:n], refs[n:2 * n], *refs[2 * n:])
        start()
        finish()

    return _pcall(
        body, name="rs_chips", in_specs=[_HBM] * n, out_specs=[_HBM] * n,
        out_shape=[jax.ShapeDtypeStruct(h.shape, h.dtype) for h in hs], scratch_shapes=_rs_sems(n),
    )(*hs)


def share_sibling(ts):
    n = len(ts)

    def body(*refs):
        t, out = refs[:n], refs[n:2 * n]
        ssem, rsem = refs[2 * n:]
        x, y, c, _ = _place()
        cps = [_rcopy(t[i].at[c], out[i].at[c], ssem.at[i], rsem.at[i], (x, y, 1 - c)) for i in range(n)]
        for cp in cps:
            cp.start()
        for i in range(n):
            _rcopy(t[i].at[c], out[i].at[1 - c], ssem.at[i], rsem.at[i], (x, y, 1 - c)).wait_recv()
        for cp in cps:
            cp.wait_send()

    return _pcall(
        body, name="share_sibling", in_specs=[_HBM] * n, out_specs=[_HBM] * n,
        out_shape=[jax.ShapeDtypeStruct(t.shape, t.dtype) for t in ts],
        input_output_aliases={i: i for i in range(n)},
        scratch_shapes=[pltpu.SemaphoreType.DMA((n,)), pltpu.SemaphoreType.DMA((n,))],
    )(*ts)


def allreduce8(v, name):
    P, W = v.shape

    def body(v_ref, out_ref, buf, ssem, rsem):
        x, y, c, _ = _place()
        me = 4 * x + 2 * y + c
        buf[me] = v_ref[...]
        cps = []
        for k in range(1, 8):
            px = 1 - x if k & 4 else x
            py = 1 - y if k & 2 else y
            pc = 1 - c if k & 1 else c
            cp = _rcopy(buf.at[me], buf.at[me], ssem.at[k - 1], rsem.at[k - 1], (px, py, pc))
            cp.start()
            cps.append((cp, 4 * px + 2 * py + pc))
        for k, (cp, peer) in enumerate(cps):
            _rcopy(buf.at[me], buf.at[peer], ssem.at[k], rsem.at[k], (x, y, c)).wait_recv()
        for cp, _ in cps:
            cp.wait_send()
        acc = buf[0]
        for d in range(1, 8):
            acc = acc + buf[d]
        out_ref[...] = acc

    return _pcall(
        body, name=name, in_specs=[pl.BlockSpec(memory_space=pltpu.VMEM)],
        out_specs=pl.BlockSpec(memory_space=pltpu.VMEM), out_shape=jax.ShapeDtypeStruct((P, W), F32),
        scratch_shapes=[pltpu.VMEM((8, P, W), F32), pltpu.SemaphoreType.DMA((7,)), pltpu.SemaphoreType.DMA((7,))],
    )(v)


def add_half(g, rcv, cidx, name):
    _, hr, W = rcv.shape
    tr = _rows_tile(hr, W)
    nb = hr // tr

    def body(c_ref, g_ref, r_ref, o_ref):
        o_ref[...] = (g_ref[...] + r_ref[...]).astype(o_ref.dtype)

    return _pcall(
        body, name=name,
        grid_spec=pltpu.PrefetchScalarGridSpec(
            num_scalar_prefetch=1, grid=(4, nb),
            in_specs=[pl.BlockSpec((1, tr, W), lambda s, i, c: (s, c[0] * nb + i, 0)),
                      pl.BlockSpec((1, tr, W), lambda s, i, c: (s, i, 0))],
            out_specs=pl.BlockSpec((1, tr, W), lambda s, i, c: (s, i, 0))),
        out_shape=jax.ShapeDtypeStruct((4, hr, W), BF16), compiler_params=_cparams(("parallel", "parallel")),
    )(cidx, g, rcv)


def sum_chips(p, h, sc, name):
    _, hr, W = p.shape
    tr = _rows_tile(hr, W)

    def body(s_ref, c_ref, p_ref, h_ref, o_ref):
        s = s_ref[0]
        own = h_ref[0]
        f = lambda k: jnp.where(s == k, own, p_ref[k]).astype(F32)
        o_ref[0] = ((f(0) + f(1)) + f(2)) + f(3)

    return _pcall(
        body, name=name,
        grid_spec=pltpu.PrefetchScalarGridSpec(
            num_scalar_prefetch=2, grid=(hr // tr,),
            in_specs=[pl.BlockSpec((4, tr, W), lambda i, s, c: (0, i, 0)),
                      pl.BlockSpec((1, tr, W), lambda i, s, c: (s[0], i, 0))],
            out_specs=pl.BlockSpec((1, tr, W), lambda i, s, c: (c[0], i, 0))),
        out_shape=jax.ShapeDtypeStruct((2, hr, W), F32), compiler_params=_cparams(("parallel",)),
    )(sc[0], sc[1], p, h)


HI = lax.Precision.HIGHEST
BLOCK_BYTES = 3 * 512 * 1024


def _rows_tile(R, w, T=None):
    cands = [t for t in range(16, R + 1, 16) if R % t == 0 and (T is None or T % t == 0)]
    ok = [t for t in cands if t * w * 4 <= BLOCK_BYTES]
    return max(ok) if ok else min(cands)


def _gsum_exact(x):
    r = jnp.right_shift(lax.broadcasted_iota(jnp.int32, (LANE, LANE), 0), 6)
    c = jnp.right_shift(lax.broadcasted_iota(jnp.int32, (LANE, LANE), 1), 6)
    g = (r == c).astype(BF16)
    x1 = x.astype(BF16)
    r1 = x - x1.astype(F32)
    x2 = r1.astype(BF16)
    x3 = (r1 - x2.astype(F32)).astype(BF16)
    dot = lambda z: jnp.dot(z, g, preferred_element_type=F32)
    return (dot(x3) + dot(x2)) + dot(x1)


@jax.custom_vjp
def _gsum(x):
    return _gsum_exact(x)


_gsum.defvjp(lambda x: (_gsum_exact(x), None), lambda _, ct: (_gsum_exact(ct),))


def _rms(x, g):
    return x * lax.rsqrt(jnp.mean(x * x, axis=-1, keepdims=True) + NORM_EPS) * g


def _silu_mul(gate, up):
    return jax.nn.silu(gate) * up


def _shift(p, prev, mu):
    return p + mu * (prev - p)


def _lora_act(p, prev, mu):
    s = _shift(p, prev, mu)
    return jax.nn.sigmoid(s[:, :G_LORA]), jnp.tanh(s[:, G_LORA:G_LORA + LANE]), s[:, G_LORA + LANE:]


def _prep(k, lw, la, w0, a0, kk_w, ka_w):
    wpre = -jax.nn.softplus(-(w0 + lw)) - 0.5
    decay = jnp.exp(-jnp.exp(wpre))
    a = jax.nn.sigmoid(a0 + la)
    kk = k * kk_w
    kk = kk * lax.rsqrt(jnp.maximum(_gsum(kk * kk), 1e-24))
    k2 = k * (1.0 + (a - 1.0) * ka_w)
    return decay, -kk, kk * a, k2


def _post(y, r, k2, v, g, gnw, gnb, rk):
    mean = _gsum(y) * (1.0 / RWKV_HEAD)
    d = y - mean
    var = _gsum(d * d) * (1.0 / RWKV_HEAD)
    yn = d * lax.rsqrt(var + GN_EPS) * gnw + gnb
    bonus = _gsum(r * k2 * rk) * v
    return (yn + bonus) * g


def _gate_mix(ga, gb, ya, o):
    return jax.nn.sigmoid(ga) * ya + jax.nn.sigmoid(gb) * o


def _rope(x, cos, sin):
    return x * cos + pltpu.roll(x, LANE // 2, 1) * sin


def _rope_t(dy, cos, sin):
    return dy * cos + pltpu.roll(dy * sin, LANE // 2, 1)


def _mla_pre(cq, ckv, kpe, cos, sin, qw, kvw):
    return _rms(cq, qw), _rms(ckv, kvw), _rope(kpe, cos, sin)


def _mla_pre_bwd(cq, ckv, dcqn, dckvn, dkr, cos, sin, qw, kvw):
    _, pull = jax.vjp(lambda a, b, c, d: (_rms(a, c), _rms(b, d)), cq, ckv, qw, kvw)
    dcq, dckv, dqw, dkvw = pull((dcqn, dckvn))
    return dcq, dckv, _rope_t(dkr, cos, sin), dqw, dkvw


WEIGHTS = ['meta_tokens', 'ffn1_norm', 'ffn1_w_gate', 'ffn1_w_up', 'ffn1_w_down', 'mix_norm', 'w_in', 'tm_mu', 'w0',
           'w_up', 'a0', 'a_up', 'g_up', 'k_k', 'k_a', 'r_k', 'gn_w', 'gn_b', 'q_norm', 'w_uq', 'kv_norm', 'w_ukv',
           'w_out', 'ffn2_norm', 'ffn2_w_gate', 'ffn2_w_up', 'ffn2_w_down', 'final_norm']
SHARD_AXIS = {'meta_tokens': 1, 'ffn1_w_gate': 1, 'ffn1_w_up': 1, 'ffn1_w_down': 0, 'w_in': 1, 'w_up': 1, 'a_up': 1,
              'g_up': 1, 'w_uq': 1, 'w_ukv': 1, 'w_out': 0, 'ffn2_w_gate': 1, 'ffn2_w_up': 1, 'ffn2_w_down': 0}
GATHERED = [n for n in WEIGHTS if n in SHARD_AXIS and n != 'meta_tokens']
FFN_IN = ('ffn1_w_gate', 'ffn1_w_up', 'ffn2_w_gate', 'ffn2_w_up')
PART_B = ['ffn2_w_gate', 'ffn2_w_up', 'ffn2_w_down', 'w_out']
PART_A1 = ['ffn1_w_gate', 'ffn1_w_up', 'ffn1_w_down']
PART_A2 = [n for n in GATHERED if n not in PART_B and n not in PART_A1]
SMALL = [n for n in WEIGHTS if n not in SHARD_AXIS]


def _to2d(a):
    if a.ndim == 1:
        return a.reshape(1, -1)
    if a.ndim == 3:
        return a.reshape(a.shape[0] * a.shape[1], a.shape[2]) if a.shape[0] == 1 and a.shape[1] > 64 else a.reshape(1, -1)
    return a


def _unpack(flat, shapes):
    out, off = [], 0
    for shp in shapes:
        n = shp[0] * shp[1]
        out.append(flat[off:off + n].reshape(shp))
        off += n
    return out


def _adamw(w, g, m, v):
    m = ADAM_B1 * m + (1.0 - ADAM_B1) * g
    v = ADAM_B2 * v + (1.0 - ADAM_B2) * jnp.square(g)
    m_hat = m / (1.0 - ADAM_B1 ** ADAM_STEP)
    v_hat = v / (1.0 - ADAM_B2 ** ADAM_STEP)
    delta = -ADAM_LR * (m_hat / (jnp.sqrt(v_hat) + ADAM_EPS) + ADAM_WD * w)
    return delta, m, v


def adamw(w, g, m, v, name):
    R, C = w.shape
    if R % 16 == 0 and R > 16:
        tm = _rows_tile(R, C)
    else:
        tm = R
    return rowwise(_adamw, [(w, C, 0), (g, C, 0), (m, C, 0), (v, C, 0)], [], [(C, C, F32)] * 3, tm=tm, name=name)


def _step(a):
    x = a['x']
    Bl, S, D = x.shape
    T = S + N_META
    R, RS = Bl * T, Bl * S
    Hr, Hm = D // RWKV_HEAD, D // LANE
    w2 = {n: _to2d(a[n]) for n in WEIGHTS}
    m2 = {n: _to2d(a['m_' + n]) for n in WEIGHTS}
    v2 = {n: _to2d(a['v_' + n]) for n in WEIGHTS}
    xi, yi, ci = lax.axis_index("x"), lax.axis_index("y"), lax.axis_index("c")
    chip = 2 * xi + yi

    i0 = jnp.zeros((), jnp.int32)
    chip32 = chip.astype(jnp.int32)

    def own_slot(w):
        return lax.dynamic_update_slice(jnp.zeros((4,) + w.shape, BF16), w.astype(BF16)[None], (chip32, i0, i0))

    stk = dict(zip(PART_A1, ag_weights([own_slot(w2[n]) for n in PART_A1])))

    def unstack(z, axis):
        return z.reshape(4 * z.shape[1], z.shape[2]) if axis == 0 else jnp.concatenate([z[s] for s in range(4)], axis=1)

    full = {n: unstack(stk[n], SHARD_AXIS[n]) for n in PART_A1 if n not in FFN_IN}
    mt = w2['meta_tokens']
    mcols = mt.shape[1]
    mt_z = lax.dynamic_update_slice(jnp.zeros((N_META, D), F32), 0.5 * mt, (jnp.zeros((), jnp.int32), (chip * mcols).astype(jnp.int32)))
    meta_full = allreduce8(mt_z.reshape(-1, LANE), "gather_meta").reshape(N_META, D)

    F = 4 * stk['ffn1_w_gate'].shape[2]
    half = ROPE_DIM // 2
    tmu = w2['tm_mu']
    mu_a = tmu[:, :3 * D]
    zm = lambda n: jnp.zeros((1, n), F32)
    mu_b = jnp.concatenate([tmu[:, 3 * D + 2 * W_LORA:], tmu[:, 3 * D:3 * D + W_LORA], zm(LANE - W_LORA),
                            tmu[:, 3 * D + W_LORA:3 * D + 2 * W_LORA], zm(LANE - A_LORA)], axis=1)
    pos = jnp.arange(T, dtype=F32)
    inv_freq = 1.0 / (ROPE_THETA ** (jnp.arange(0, ROPE_DIM, 2, dtype=F32) / ROPE_DIM))
    ang = pos[:, None] * inv_freq[None, :]
    zt = jnp.zeros((T, half), F32)
    cos_t = jnp.concatenate([jnp.cos(ang), zt, jnp.cos(ang), zt], axis=1)
    sin_t = jnp.concatenate([-jnp.sin(ang), zt, jnp.sin(ang), zt], axis=1)
    cos_q, sin_q = cos_t[N_META:], sin_t[N_META:]

    t_full = _rows_tile(R, D)
    t_512 = _rows_tile(R, 512, T)
    t_128 = _rows_tile(R, LANE, T)
    tq_full = _rows_tile(RS, D)
    tq_128 = _rows_tile(RS, LANE, S)

    def rows3(z):
        return z.reshape(Bl, T, z.shape[-1])

    def real_rows(z):
        return rows3(z)[:, N_META:].reshape(RS, z.shape[-1])

    def pad_meta(z):
        z3 = z.reshape(Bl, S, z.shape[-1])
        return jnp.concatenate([jnp.zeros((Bl, N_META, z.shape[-1]), z.dtype), z3], axis=1).reshape(R, z.shape[-1])

    def shift_down(z):
        z3 = rows3(z)
        return jnp.concatenate([jnp.zeros((Bl, 1, z.shape[-1]), z.dtype), z3[:, :-1]], axis=1).reshape(R, z.shape[-1])

    def shift_up(z):
        z3 = rows3(z)
        return jnp.concatenate([z3[:, 1:], jnp.zeros((Bl, 1, z.shape[-1]), z.dtype)], axis=1).reshape(R, z.shape[-1])

    def ffn_fwd(h, nw, wg, wu, wd, tag, ag_bufs=()):
        n = rowwise(_rms, [(h, D, 0)], [(nw, D, 0)], [(D, D, BF16)], tm=t_full, name=tag + "_norm")[0]
        res = mm_epi(n, [wg, wu], "nn", lambda g_, u_: (g_, u_, _silu_mul(g_, u_)), [], [F32, F32, BF16], b_stack=True,
                     carry=("ag", ag_bufs) if ag_bufs else None, name=tag + "_gate_up")
        (gate, up, act), got = res if ag_bufs else (res, [])
        out = mm(act, wd, res=h, alpha=0.5, name=tag + "_down")
        return (out, (h, n, gate, up, act), got) if ag_bufs else (out, (h, n, gate, up, act))

    def ffn_bwd(dh2, saved, nw, wg, wu, wd, tag, rs_hs=()):
        h, n, gate, up, act = saved
        dz = rowwise(lambda d: 0.5 * d, [(dh2, 512, 0)], [], [(D, 512, BF16)], tm=t_512, ncb=D // 512, name=tag + "_dz")[0]
        d_wd = mm(act, dz, "tn", name=tag + "_dwd")
        res = mm_epi(dz, [wd], "nt", lambda da, g_, u_: vjp_fn(_silu_mul, 2)(g_, u_, da), [gate, up], [BF16, BF16],
                     carry=("rs", rs_hs) if rs_hs else None, name=tag + "_dact")
        (dgate, dup), got = res if rs_hs else (res, [])
        d_wg = mm(n, dgate, "tn", out_stack=True, name=tag + "_dwg")
        d_wu = mm(n, dup, "tn", out_stack=True, name=tag + "_dwu")
        dn = mm(dgate, wg, "nt", b_stack=True, name=tag + "_dn1")
        dn = mm(dup, wu, "nt", res=dn, b_stack=True, name=tag + "_dn2")

        def f(h_, dn_, dh_, nw_):
            dh, dnw = vjp_fn(_rms, 2)(h_, nw_, dn_)
            return dh + dh_, dnw

        dh, d_nw = rowwise(f, [(h, D, 0), (dn, D, 0), (dh2, D, 0)], [(nw, D, 0)], [(D, D, F32)], [(1, D, D)],
                           tm=t_full, name=tag + "_dnorm")
        return (dh, d_nw, d_wg, d_wu, d_wd, got) if rs_hs else (dh, d_nw, d_wg, d_wu, d_wd)

    h0 = jnp.concatenate([jnp.broadcast_to(meta_full[None], (Bl, N_META, D)), x], axis=1).reshape(R, D)
    h1, sv1, got_a2 = ffn_fwd(h0, w2['ffn1_norm'], stk['ffn1_w_gate'], stk['ffn1_w_up'], full['ffn1_w_down'], "ffn1",
                              [own_slot(w2[n]) for n in PART_A2])
    stk.update(zip(PART_A2, got_a2))
    full.update({n: unstack(stk[n], SHARD_AXIS[n]) for n in PART_A2})
    win = full['w_in']
    o = 3 * D
    c_xw, c_xa, c_xg = win[:, o:o + W_LORA], win[:, o + W_LORA:o + 2 * W_LORA], win[:, o + 2 * W_LORA:o + 2 * W_LORA + G_LORA]
    o += 2 * W_LORA + G_LORA
    c_cq, c_ckv, c_kpe = win[:, o:o + Q_LORA], win[:, o + Q_LORA:o + Q_LORA + KV_LORA], win[:, o + Q_LORA + KV_LORA:o + Q_LORA + KV_LORA + ROPE_DIM]
    o += Q_LORA + KV_LORA + ROPE_DIM
    c_ga, c_gb = win[:, o:o + D], win[:, o + D:o + 2 * D]
    zc = lambda n: jnp.zeros((D, n), BF16)
    half = ROPE_DIM // 2
    NP0 = 5 * D + 512 + Q_LORA + KV_LORA + LANE
    NP = -(-NP0 // 512) * 512
    win_p = jnp.concatenate([win[:, :3 * D], c_ga, c_gb, c_xg, c_xw, zc(LANE - W_LORA), c_xa, zc(LANE - A_LORA), c_cq, c_ckv,
                             c_kpe[:, :half], zc(half), c_kpe[:, half:], zc(half), zc(NP - NP0)], axis=1)
    O_GA, O_GB, O_L, O_CQ, O_CKV, O_KPE = 3 * D, 4 * D, 5 * D, 5 * D + 512, 5 * D + 512 + Q_LORA, 5 * D + 512 + Q_LORA + KV_LORA
    zr = lambda n: jnp.zeros((n, D), BF16)
    w_up_p = jnp.concatenate([full['w_up'], zr(LANE - W_LORA)], axis=0)
    a_up_p = jnp.concatenate([full['a_up'], zr(LANE - A_LORA)], axis=0)
    g_up = full['g_up']
    wuq = full['w_uq'].reshape(Q_LORA, Hm, QK_DIM)
    zq = jnp.zeros((Q_LORA, Hm, half), BF16)
    wqn = wuq[:, :, :NOPE_DIM].reshape(Q_LORA, Hm * LANE)
    wqp = jnp.concatenate([wuq[:, :, NOPE_DIM:NOPE_DIM + half], zq, wuq[:, :, NOPE_DIM + half:], zq], axis=2).reshape(Q_LORA, Hm * LANE)
    wukv = full['w_ukv'].reshape(KV_LORA, Hm, NOPE_DIM + V_DIM)
    wkn = wukv[:, :, :NOPE_DIM].reshape(KV_LORA, Hm * LANE)
    wv = wukv[:, :, NOPE_DIM:].reshape(KV_LORA, Hm * LANE)
    u = rowwise(_rms, [(h1, D, 0)], [(w2['mix_norm'], D, 0)], [(D, D, BF16)], tm=t_full, name="mix_norm")[0]
    proj = mm(u, win_p, name="proj")
    prev_a = shift_down(proj[:, :3 * D])
    prev_b = shift_down(proj[:, O_L:O_L + 512])
    ps = rowwise(_shift, [(proj, 512, 0), (prev_a, 512, 0)], [(mu_a, 512, 0)], [(3 * D, 512, F32)], tm=t_512,
                 ncb=3 * D // 512, name="shift_rkv")[0]
    sg, txw, xas = rowwise(_lora_act, [(proj, 512, O_L // 512), (prev_b, 512, 0)], [(mu_b, 512, 0)],
                           [(G_LORA, G_LORA, BF16), (LANE, LANE, BF16), (LANE, LANE, BF16)], tm=t_512, name="lora_act")
    lw = mm(txw, w_up_p, name="lora_w")
    la = mm(xas, a_up_p, name="lora_a")
    g = mm(sg, g_up, name="lora_g")
    hb = D // LANE
    par_d = lambda n: (w2[n], LANE, 0)
    decay, kn, bb, k2 = rowwise(_prep, [(ps, LANE, hb), (lw, LANE, 0), (la, LANE, 0)],
                                [par_d('w0'), par_d('a0'), par_d('k_k'), par_d('k_a')], [(D, LANE, F32)] * 4,
                                tm=t_128, ncb=hb, name="wkv_prep")

    def to_j(z):
        z = z.reshape(Bl, T, Hr, RWKV_HEAD).transpose(1, 3, 0, 2).reshape(T, RWKV_HEAD, Bl * Hr)
        return jnp.concatenate([z, z], axis=-1)

    def to_i(z):
        return z.reshape(Bl, T, Hr, 2, RWKV_HEAD // 2).transpose(1, 4, 3, 0, 2).reshape(T, RWKV_HEAD // 2, 2 * Bl * Hr)

    def from_i(z):
        return z.reshape(T, RWKV_HEAD // 2, 2, Bl, Hr).transpose(3, 0, 4, 2, 1).reshape(R, D)

    def from_j(z):
        return z.reshape(T, RWKV_HEAD, Bl, Hr).transpose(2, 0, 3, 1).reshape(R, D)

    r_s, v_s = ps[:, :D], ps[:, 2 * D:]
    jw, jkn, jb, jk, jr, iv = to_j(decay), to_j(kn), to_j(bb), to_j(k2), to_j(r_s), to_i(v_s)
    y_i, sp, sa_i = wkv_fwd(jw, jkn, jb, jk, jr, iv)
    y = from_i(y_i)
    post_rows = [(y, LANE, 0), (ps, LANE, 0), (k2, LANE, 0), (ps, LANE, 2 * hb), (g, LANE, 0)]
    post_pars = [par_d('gn_w'), par_d('gn_b'), par_d('r_k')]
    ya = rowwise(_post, post_rows, post_pars, [(D, LANE, F32)], tm=t_128, ncb=hb, name="wkv_post")[0]

    nt512 = T // t_512
    mla_rows = [(proj, Q_LORA, O_CQ // Q_LORA), (proj, KV_LORA, O_CKV // KV_LORA), (proj, LANE, O_KPE // LANE)]
    tabs = [(cos_t, LANE, 0, nt512, True), (sin_t, LANE, 0, nt512, True)]
    mla_pars = [(w2['q_norm'], Q_LORA, 0), (w2['kv_norm'], KV_LORA, 0)]
    cqn, ckvn, kpr = rowwise(_mla_pre, mla_rows + tabs, mla_pars,
                             [(Q_LORA, Q_LORA, BF16), (KV_LORA, KV_LORA, BF16), (LANE, LANE, BF16)], tm=t_512, name="mla_pre")
    cqn_r = real_rows(cqn)
    qn = mm(cqn_r, wqn, out_dtype=BF16, name="q_nope")
    qp_raw = mm(cqn_r, wqp, name="q_pe")
    ntq = S // tq_128
    qtabs = [(cos_q, LANE, 0, ntq, True), (sin_q, LANE, 0, ntq, True)]
    qp = rowwise(_rope, [(qp_raw, LANE, 0)] + qtabs, [], [(D, LANE, BF16)], tm=tq_128, ncb=Hm, name="q_rope")[0]
    knope = mm(ckvn, wkn, out_dtype=BF16, name="k_nope")
    vv = mm(ckvn, wv, out_dtype=BF16, name="v_proj")

    def pad_keys(z):
        z3 = rows3(z)
        return jnp.concatenate([z3[:, :N_META], jnp.zeros((Bl, Q_BLOCK - N_META, z.shape[-1]), z.dtype), z3[:, N_META:]], axis=1)

    def unpad_keys(z):
        return jnp.concatenate([z[:, :N_META], z[:, Q_BLOCK:]], axis=1).reshape(R, z.shape[-1])

    qn3, qp3 = qn.reshape(Bl, S, D), qp.reshape(Bl, S, D)
    knp, kpp, vp = pad_keys(knope), pad_keys(kpr), pad_keys(vv)
    o3, got_b = attn_fwd(qn3, qp3, knp, kpp, vp, [own_slot(w2[n]) for n in PART_B])
    stk.update(zip(PART_B, got_b))
    full.update({n: unstack(stk[n], SHARD_AXIS[n]) for n in PART_B if n not in FFN_IN})
    wout = full['w_out']
    o_att = pad_meta(o3.reshape(RS, D))
    mix_rows = [(proj, 512, O_GA // 512), (proj, 512, O_GB // 512), (ya, 512, 0), (o_att, 512, 0)]
    mix = rowwise(_gate_mix, mix_rows, [], [(D, 512, BF16)], tm=t_512, ncb=D // 512, name="gate_mix")[0]
    h2 = mm(mix, wout, res=h1, name="w_out")
    h3, sv2 = ffn_fwd(h2, w2['ffn2_norm'], stk['ffn2_w_gate'], stk['ffn2_w_up'], full['ffn2_w_down'], "ffn2")

    def loss_fb(h_, tgt, fw):
        yv, pull = jax.vjp(_rms, h_, fw)
        e = yv - tgt
        dh, dfw = pull(e * (1.0 / D))
        return dh, jnp.full((1, LANE), 0.5 / D * jnp.sum(e * e), F32), dfw

    dh3r, lossp, g_final = rowwise(loss_fb, [(real_rows(h3), D, 0), (a['loss_target'].reshape(RS, D), D, 0)],
                                   [(w2['final_norm'], D, 0)], [(D, D, F32)], [(1, LANE, LANE), (1, D, D)],
                                   tm=tq_full, name="loss")
    dh3 = pad_meta(dh3r)

    def stacked(g, axis):
        if g.ndim == 3:
            return g
        if axis == 0:
            return g.reshape(4, g.shape[0] // 4, g.shape[1])
        return g.reshape(g.shape[0], 4, g.shape[1] // 4).transpose(1, 0, 2)

    cidx = ci.reshape(1).astype(jnp.int32)
    gr = {'final_norm': g_final}
    dh2, gr['ffn2_norm'], gr['ffn2_w_gate'], gr['ffn2_w_up'], gr['ffn2_w_down'] = ffn_bwd(
        dh3, sv2, w2['ffn2_norm'], stk['ffn2_w_gate'], stk['ffn2_w_up'], full['ffn2_w_down'], "ffn2")
    dh2b = rowwise(lambda d: d, [(dh2, 512, 0)], [], [(D, 512, BF16)], tm=t_512, ncb=D // 512, name="dh2_cast")[0]
    gr['w_out'] = mm(mix, dh2b, "tn", name="d_wout")
    dmix = mm(dh2b, wout, "nt", name="d_mix")
    dga, dgb, dya, do = rowwise(vjp_fn(_gate_mix, 4), mix_rows + [(dmix, 512, 0)], [],
                                [(D, 512, BF16), (D, 512, BF16), (D, 512, F32), (D, 512, BF16)], tm=t_512, ncb=D // 512,
                                name="d_gate_mix")
    gs_b = [stacked(gr[n], SHARD_AXIS[n]) for n in PART_B]
    hs_b = [add_half(g_, r_, cidx, "add_half_" + n) for n, g_, r_ in zip(PART_B, gs_b, swap_halves(gs_b, "swap_halves_b"))]
    (dqn, dqp, dknp, dvp, dkpp), ps_b = attn_bwd(qn3, qp3, knp, kpp, vp, real_rows(do).reshape(Bl, S, D), hs_b)
    dqn2 = dqn.reshape(RS, D)
    dqp_raw = rowwise(_rope_t, [(dqp.reshape(RS, D), LANE, 0)] + qtabs, [], [(D, LANE, BF16)], tm=tq_128, ncb=Hm,
                      name="d_q_rope")[0]
    d_wqn = mm(cqn_r, dqn2, "tn", name="d_wqn")
    d_wqp = mm(cqn_r, dqp_raw, "tn", name="d_wqp")
    dcqn = mm(dqn2, wqn, "nt", name="d_cqn1")
    dcqn = pad_meta(mm(dqp_raw, wqp, "nt", res=dcqn, name="d_cqn2"))
    dkn2, dv2, dkp2 = unpad_keys(dknp), unpad_keys(dvp), unpad_keys(dkpp)
    d_wkn = mm(ckvn, dkn2, "tn", name="d_wkn")
    d_wv = mm(ckvn, dv2, "tn", name="d_wv")
    dckvn = mm(dkn2, wkn, "nt", name="d_ckvn1")
    dckvn = mm(dv2, wv, "nt", res=dckvn, name="d_ckvn2")
    dcq, dckv, dkpe, gr['q_norm'], gr['kv_norm'] = rowwise(
        _mla_pre_bwd, mla_rows[:2] + [(dcqn, Q_LORA, 0), (dckvn, KV_LORA, 0), (dkp2, LANE, 0)] + tabs, mla_pars,
        [(Q_LORA, Q_LORA, BF16), (KV_LORA, KV_LORA, BF16), (LANE, LANE, BF16)], [(1, Q_LORA, Q_LORA), (1, KV_LORA, KV_LORA)],
        tm=t_512, name="d_mla_pre")

    def post_bwd(y_, r_, k2_, v_, g_, dya_, gnw, gnb, rk):
        return vjp_fn(_post, 8)(y_, r_, k2_, v_, g_, gnw, gnb, rk, dya_)

    dy, dr_b, dk2_b, dv_b, dg, gr['gn_w'], gr['gn_b'], gr['r_k'] = rowwise(
        post_bwd, post_rows + [(dya, LANE, 0)], post_pars,
        [(D, LANE, F32)] * 4 + [(D, LANE, BF16)], [(1, D, LANE)] * 3, tm=t_128, ncb=hb, name="d_wkv_post")
    jdw, jdkn, jdb, jdk, jdr, idv = wkv_bwd(jw, jkn, jb, jk, jr, iv, to_i(dy), sp, sa_i)
    ddecay, dkn_w, db_w, dk2_w, dr_w, dv_w = from_j(jdw), from_j(jdkn), from_j(jdb), from_j(jdk), from_j(jdr), from_i(idv)

    def prep_bwd(k_, lw_, la_, dd, dkn_, db_, dk2a, dk2b, w0, a0, kkw, kaw):
        return vjp_fn(_prep, 7)(k_, lw_, la_, w0, a0, kkw, kaw, dd, dkn_, db_, dk2a + dk2b)

    dk_s, dlw, dla, gr['w0'], gr['a0'], gr['k_k'], gr['k_a'] = rowwise(
        prep_bwd, [(ps, LANE, hb), (lw, LANE, 0), (la, LANE, 0), (ddecay, LANE, 0), (dkn_w, LANE, 0), (db_w, LANE, 0),
                   (dk2_w, LANE, 0), (dk2_b, LANE, 0)],
        [par_d('w0'), par_d('a0'), par_d('k_k'), par_d('k_a')], [(D, LANE, F32), (D, LANE, BF16), (D, LANE, BF16)],
        [(1, D, LANE)] * 4, tm=t_128, ncb=hb, name="d_wkv_prep")
    d_wup = mm(txw, dlw, "tn", name="d_wup")
    dtxw = mm(dlw, w_up_p, "nt", name="d_txw")
    d_aup = mm(xas, dla, "tn", name="d_aup")
    dxa = mm(dla, a_up_p, "nt", name="d_xa")
    gr['g_up'] = mm(sg, dg, "tn", name="d_gup")
    dsg = mm(dg, g_up, "nt", name="d_sg")

    def lora_bwd(p_, prev_, dsg_, dt_, dxa_, mu_):
        return vjp_fn(_lora_act, 3)(p_, prev_, mu_, dsg_, dt_, dxa_)

    dpb, dprevb, dmu_b = rowwise(lora_bwd, [(proj, 512, O_L // 512), (prev_b, 512, 0), (dsg, G_LORA, 0), (dtxw, LANE, 0),
                                            (dxa, LANE, 0)], [(mu_b, 512, 0)], [(512, 512, F32)] * 2, [(1, 512, 512)],
                                 tm=t_512, name="d_lora_act")

    def shift_bwd2(p_, prev_, c1, c2, mu_):
        return vjp_fn(_shift, 3)(p_, prev_, mu_, c1 + c2)

    def shift_bwd1(p_, prev_, c1, mu_):
        return vjp_fn(_shift, 3)(p_, prev_, mu_, c1)

    def shift_back(sec, cts):
        nb = D // 512
        f = shift_bwd2 if len(cts) == 2 else shift_bwd1
        return rowwise(f, [(proj, 512, sec * nb), (prev_a, 512, sec * nb)] + [(c, 512, 0) for c in cts],
                       [(mu_a, 512, sec * nb)], [(D, 512, F32)] * 2, [(1, D, 512)], tm=t_512, ncb=nb, name=f"d_shift{sec}")

    dp_r, dprev_r, dmu_r = shift_back(0, [dr_w, dr_b])
    dp_k, dprev_k, dmu_k = shift_back(1, [dk_s])
    dp_v, dprev_v, dmu_v = shift_back(2, [dv_w, dv_b])

    def add_cast(p_, q_):
        return p_ + q_

    def dsec(dp_, dprev_, tag):
        C = dp_.shape[1]
        return rowwise(add_cast, [(dp_, 512, 0), (shift_up(dprev_), 512, 0)], [], [(C, 512, BF16)], tm=t_512, ncb=C // 512,
                       name="d_sec_" + tag)[0]

    zpad = jnp.zeros((R, NP - NP0), BF16)
    dproj = jnp.concatenate([dsec(dp_r, dprev_r, "r"), dsec(dp_k, dprev_k, "k"), dsec(dp_v, dprev_v, "v"), dga, dgb,
                             dsec(dpb, dprevb, "l"), dcq, dckv, dkpe, zpad], axis=1)
    d_win_p = mm(u, dproj, "tn", name="d_win")
    du = mm(dproj, win_p, "nt", name="d_u")

    def norm_bwd(h_, dn_, dh_, nw_):
        dh, dnw = vjp_fn(_rms, 2)(h_, nw_, dn_)
        return dh + dh_, dnw

    dh1, gr['mix_norm'] = rowwise(norm_bwd, [(h1, D, 0), (du, D, 0), (dh2, D, 0)], [(w2['mix_norm'], D, 0)],
                                  [(D, D, F32)], [(1, D, D)], tm=t_full, name="d_mix_norm")
    gr['w_in'] = jnp.concatenate([
        d_win_p[:, :3 * D], d_win_p[:, O_L + G_LORA:O_L + G_LORA + W_LORA], d_win_p[:, O_L + G_LORA + LANE:O_L + G_LORA + LANE + A_LORA],
        d_win_p[:, O_L:O_L + G_LORA], d_win_p[:, O_CQ:O_CQ + Q_LORA], d_win_p[:, O_CKV:O_CKV + KV_LORA],
        d_win_p[:, O_KPE:O_KPE + half], d_win_p[:, O_KPE + 2 * half:O_KPE + 3 * half], d_win_p[:, O_GA:O_GA + 2 * D]], axis=1)
    gr['tm_mu'] = jnp.concatenate([dmu_r, dmu_k, dmu_v, dmu_b[:, G_LORA:G_LORA + W_LORA],
                                   dmu_b[:, G_LORA + LANE:G_LORA + LANE + A_LORA], dmu_b[:, :G_LORA]], axis=1)
    gr['w_up'], gr['a_up'] = d_wup[:W_LORA], d_aup[:A_LORA]
    dq3n, dq3p = d_wqn.reshape(Q_LORA, Hm, LANE), d_wqp.reshape(Q_LORA, Hm, LANE)
    gr['w_uq'] = jnp.concatenate([dq3n, dq3p[:, :, :half], dq3p[:, :, 2 * half:3 * half]], axis=2).reshape(Q_LORA, Hm * QK_DIM)
    gr['w_ukv'] = jnp.concatenate([d_wkn.reshape(KV_LORA, Hm, LANE), d_wv.reshape(KV_LORA, Hm, LANE)], axis=2).reshape(
        KV_LORA, Hm * (NOPE_DIM + V_DIM))
    gs_a2 = [stacked(gr[n], SHARD_AXIS[n]) for n in PART_A2]
    hs_a2 = [add_half(g_, r_, cidx, "add_half_" + n) for n, g_, r_ in zip(PART_A2, gs_a2, swap_halves(gs_a2, "swap_halves_a2"))]
    dh0, gr['ffn1_norm'], gr['ffn1_w_gate'], gr['ffn1_w_up'], gr['ffn1_w_down'], ps_a2 = ffn_bwd(
        dh1, sv1, w2['ffn1_norm'], stk['ffn1_w_gate'], stk['ffn1_w_up'], full['ffn1_w_down'], "ffn1", hs_a2)
    dh0_3 = rows3(dh0)
    grad_x = dh0_3[:, N_META:]
    gr['meta_tokens'] = jnp.sum(dh0_3[:, :N_META], axis=0)

    gs_a1 = [stacked(gr[n], SHARD_AXIS[n]) for n in PART_A1]
    hs_a1 = [add_half(g_, r_, cidx, "add_half_" + n) for n, g_, r_ in zip(PART_A1, gs_a1, swap_halves(gs_a1, "swap_halves_a1"))]
    sc = (chip32.reshape(1), cidx)
    ts = [sum_chips(p, h, sc, "sum_chips_" + n)
          for n, p, h in zip(PART_A1 + PART_A2 + PART_B, list(rs_chips(hs_a1)) + ps_a2 + ps_b, hs_a1 + hs_a2 + hs_b)]
    g_shard = {n: z.reshape(w2[n].shape) for n, z in zip(PART_A1 + PART_A2 + PART_B, share_sibling(ts))}
    small = jnp.concatenate([gr[n].reshape(-1) for n in SMALL] + [gr['meta_tokens'].reshape(-1), lossp.reshape(-1)])
    ns = small.shape[0]
    nsp = -(-ns // (8 * LANE)) * 8 * LANE
    small_sum = allreduce8(jnp.pad(small, (0, nsp - ns)).reshape(-1, LANE), "allreduce_small").reshape(-1)
    g_small = dict(zip(SMALL + ['meta_full'], _unpack(small_sum, [w2[n].shape for n in SMALL] + [(N_META, D)])))
    g_shard['meta_tokens'] = lax.dynamic_slice(
        g_small['meta_full'], (jnp.zeros((), jnp.int32), (chip * mcols).astype(jnp.int32)), (N_META, mcols))
    loss = small_sum[ns - LANE]

    grads, deltas, new_m, new_v = [], [], [], []
    for n in WEIGHTS:
        gw = g_shard[n] if n in g_shard else g_small[n]
        d_, m_, v_ = adamw(w2[n], gw, m2[n], v2[n], "adamw_" + n)
        shp = a[n].shape
        grads.append(gw.reshape(shp))
        deltas.append(d_.reshape(shp))
        new_m.append(m_.reshape(shp))
        new_v.append(v_.reshape(shp))
    return (loss, grad_x, *grads, *deltas, *new_m, *new_v)


def kernel(x, meta_tokens, ffn1_norm, ffn1_w_gate, ffn1_w_up, ffn1_w_down, mix_norm, w_in, tm_mu, w0, w_up, a0, a_up, g_up, k_k, k_a, r_k, gn_w, gn_b, q_norm, w_uq, kv_norm, w_ukv, w_out, ffn2_norm, ffn2_w_gate, ffn2_w_up, ffn2_w_down, final_norm, loss_target, m_meta_tokens, m_ffn1_norm, m_ffn1_w_gate, m_ffn1_w_up, m_ffn1_w_down, m_mix_norm, m_w_in, m_tm_mu, m_w0, m_w_up, m_a0, m_a_up, m_g_up, m_k_k, m_k_a, m_r_k, m_gn_w, m_gn_b, m_q_norm, m_w_uq, m_kv_norm, m_w_ukv, m_w_out, m_ffn2_norm, m_ffn2_w_gate, m_ffn2_w_up, m_ffn2_w_down, m_final_norm, v_meta_tokens, v_ffn1_norm, v_ffn1_w_gate, v_ffn1_w_up, v_ffn1_w_down, v_mix_norm, v_w_in, v_tm_mu, v_w0, v_w_up, v_a0, v_a_up, v_g_up, v_k_k, v_k_a, v_r_k, v_gn_w, v_gn_b, v_q_norm, v_w_uq, v_kv_norm, v_w_ukv, v_w_out, v_ffn2_norm, v_ffn2_w_gate, v_ffn2_w_up, v_ffn2_w_down, v_final_norm):
    return _step(dict(locals()))
```

```python
import functools
import math

import jax
import jax.numpy as jnp
import numpy as np
from jax import lax
from jax.experimental import pallas as pl
from jax.experimental.pallas import tpu as pltpu

F32 = jnp.float32
BF16 = jnp.bfloat16
MESH = pl.DeviceIdType.MESH

N_META = 16
NORM_EPS = 1e-6
RWKV_HEAD = 64
GN_EPS = RWKV_HEAD * 1e-5
W_LORA, A_LORA, G_LORA = 96, 96, 256
Q_LORA, KV_LORA = 512, 512
NOPE_DIM, ROPE_DIM, V_DIM = 128, 64, 128
QK_DIM = NOPE_DIM + ROPE_DIM
ROPE_THETA = 10000.0
Q_BLOCK = 128
ADAM_LR, ADAM_B1, ADAM_B2, ADAM_EPS, ADAM_WD, ADAM_STEP = 0.001, 0.9, 0.999, 1e-08, 0.01, 10

LANE = 128
VMEM_LIMIT = 56 * 1024 * 1024


def _pcall(body, **kw):
    return pl.pallas_call(body, **kw)


def _cparams(sem):
    return pltpu.CompilerParams(dimension_semantics=sem, vmem_limit_bytes=VMEM_LIMIT)


MM_LANE_TILE = 1536
MM_ROW_TILE = 1408


def _div_tile(n, cap, unit):
    best = None
    for t in range(unit, min(n, cap) + 1, unit):
        if n % t == 0:
            best = t
    return best if best is not None else n


def _rtile(n, pref=512):
    best = None
    for t in range(16, min(n, pref * 2) + 1, 16):
        if n % t == 0 and (best is None or abs(t - pref) < abs(best - pref)):
            best = t
    return best if best is not None else n


def mm(a, b, mode="nn", out_dtype=F32, res=None, alpha=1.0, b_stack=False, out_stack=False, name="mm"):
    cs = b.shape[-1] if b_stack else None
    bs = (b.shape[1], 4 * cs) if b_stack else b.shape
    if mode == "nn":
        (M, K), (K2, N) = a.shape, bs
    elif mode == "nt":
        (M, K), (N, K2) = a.shape, bs
    else:
        (K, M), (K2, N) = a.shape, bs
    assert K == K2, (a.shape, b.shape, mode)
    if mode == "tn":
        tm, tk = _div_tile(M, MM_ROW_TILE, LANE), _div_tile(K, 1024, 16)
    else:
        tm = _div_tile(M, MM_ROW_TILE, 16)
        tk = _div_tile(cs if (b_stack and mode == "nt") else K, MM_LANE_TILE, LANE)
    ncol = N // 4 if out_stack else (cs if (b_stack and mode == "nn") else N)
    tn = _div_tile(ncol, MM_LANE_TILE if mode != "nt" else 1024, LANE)
    nk = K // tk
    dims = {"nn": (((1,), (0,)), ((), ())), "nt": (((1,), (1,)), ((), ())), "tn": (((0,), (0,)), ((), ()))}[mode]
    direct = out_dtype == F32

    def body(*refs):
        a_ref, b_ref = refs[:2]
        r_ref = refs[2] if res is not None else None
        o_ref = refs[3] if res is not None else refs[2]
        acc = o_ref if direct else refs[-1]
        k = pl.program_id(2)

        @pl.when(k == 0)
        def _():
            acc[...] = jnp.zeros_like(acc) if res is None else r_ref[...].astype(F32)

        p = lax.dot_general(a_ref[...].astype(BF16), b_ref[...].astype(BF16), dims, preferred_element_type=F32)
        acc[...] += p if alpha == 1.0 else alpha * p

        if not direct:
            @pl.when(k == nk - 1)
            def _():
                o_ref[...] = acc[...].astype(o_ref.dtype)

    if mode == "tn":
        a_spec = pl.BlockSpec((tk, tm), lambda i, j, k: (k, i))
        b_spec = pl.BlockSpec((tk, tn), lambda i, j, k: (k, j))
    else:
        a_spec = pl.BlockSpec((tm, tk), lambda i, j, k: (i, k))
        if mode == "nn":
            if b_stack:
                nps = cs // tn
                b_spec = pl.BlockSpec((None, tk, tn), lambda i, j, k: (j // nps, k, j % nps))
            else:
                b_spec = pl.BlockSpec((tk, tn), lambda i, j, k: (k, j))
        elif b_stack:
            kps = cs // tk
            b_spec = pl.BlockSpec((None, tn, tk), lambda i, j, k: (k // kps, j, k % kps))
        else:
            b_spec = pl.BlockSpec((tn, tk), lambda i, j, k: (j, k))
    r_spec = pl.BlockSpec((tm, tn), lambda i, j, k: (i, j))
    if out_stack:
        ops = (N // 4) // tn
        o_spec = pl.BlockSpec((None, tm, tn), lambda i, j, k: (j // ops, i, j % ops))
        o_shape = jax.ShapeDtypeStruct((4, M, N // 4), out_dtype)
    else:
        o_spec, o_shape = r_spec, jax.ShapeDtypeStruct((M, N), out_dtype)
    in_specs = [a_spec, b_spec] + ([r_spec] if res is not None else [])
    args = (a, b) + ((res,) if res is not None else ())
    return _pcall(
        body, name=name, grid=(M // tm, N // tn, nk), in_specs=in_specs, out_specs=o_spec, out_shape=o_shape,
        scratch_shapes=[] if direct else [pltpu.VMEM((tm, tn), F32)],
        compiler_params=_cparams(("parallel", "parallel", "arbitrary")),
    )(*args)


MM_EPI_ROW_TILE = 704


def mm_epi(a, bs, mode, epi, extras, out_dtypes, b_stack=False, carry=None, name="mm_epi"):
    kind, carried = carry if carry else (None, ())
    nc = len(carried)
    b = bs[0]
    cs = b.shape[-1] if b_stack else None
    bshape = (b.shape[1], 4 * cs) if b_stack else b.shape
    (M, K) = a.shape
    (K2, N) = bshape if mode == "nn" else bshape[::-1]
    assert K == K2 and mode in ("nn", "nt"), (a.shape, b.shape, mode)
    tm = _div_tile(M, MM_EPI_ROW_TILE, 16)
    tk = _div_tile(cs if (b_stack and mode == "nt") else K, MM_LANE_TILE, LANE)
    tn = _div_tile(cs if (b_stack and mode == "nn") else N, MM_LANE_TILE, LANE)
    nk, nb, ne, no = K // tk, len(bs), len(extras), len(out_dtypes)
    dims = (((1,), (0,)), ((), ())) if mode == "nn" else (((1,), (1,)), ((), ()))

    def body(*refs):
        a_ref, b_refs, e_refs = refs[0], refs[1:1 + nb], refs[1 + nb:1 + nb + ne]
        base = 1 + nb + ne + nc
        o_refs, accs = refs[base:base + no], refs[base + no + nc:base + no + nc + nb]
        k = pl.program_id(2)
        if nc:
            steps = _ag_steps if kind == "ag" else _rs_steps
            start, finish = steps(refs[base - nc:base], refs[base + no:base + no + nc], *refs[base + no + nc + nb:])
            i, j = pl.program_id(0), pl.program_id(1)
            pl.when((i == 0) & (j == 0) & (k == 0))(start)

        @pl.when(k == 0)
        def _():
            for acc in accs:
                acc[...] = jnp.zeros_like(acc)

        av = a_ref[...].astype(BF16)
        for b_ref, acc in zip(b_refs, accs):
            acc[...] += lax.dot_general(av, b_ref[...].astype(BF16), dims, preferred_element_type=F32)

        @pl.when(k == nk - 1)
        def _():
            res = epi(*[acc[...] for acc in accs], *[e[...] for e in e_refs])
            for o_ref, v in zip(o_refs, res):
                o_ref[...] = v.astype(o_ref.dtype)

        if nc:
            pl.when((i == M // tm - 1) & (j == N // tn - 1) & (k == nk - 1))(finish)

    a_spec = pl.BlockSpec((tm, tk), lambda i, j, k: (i, k))
    if mode == "nn":
        if b_stack:
            nps = cs // tn
            b_spec = pl.BlockSpec((None, tk, tn), lambda i, j, k: (j // nps, k, j % nps))
        else:
            b_spec = pl.BlockSpec((tk, tn), lambda i, j, k: (k, j))
    elif b_stack:
        kps = cs // tk
        b_spec = pl.BlockSpec((None, tn, tk), lambda i, j, k: (k // kps, j, k % kps))
    else:
        b_spec = pl.BlockSpec((tn, tk), lambda i, j, k: (j, k))
    o_spec = pl.BlockSpec((tm, tn), lambda i, j, k: (i, j))
    sems = (_ag_sems(nc) if kind == "ag" else _rs_sems(nc)) if nc else []
    res = _pcall(
        body, name=name, grid=(M // tm, N // tn, nk), in_specs=[a_spec] + [b_spec] * nb + [o_spec] * ne + [_HBM] * nc,
        out_specs=[o_spec] * no + [_HBM] * nc,
        out_shape=[jax.ShapeDtypeStruct((M, N), dt) for dt in out_dtypes]
        + [jax.ShapeDtypeStruct(z.shape, z.dtype) for z in carried],
        input_output_aliases={1 + nb + ne + q: no + q for q in range(nc)} if kind == "ag" else {},
        scratch_shapes=[pltpu.VMEM((tm, tn), F32)] * nb + sems,
        compiler_params=_cparams(("arbitrary",) * 3 if nc else ("parallel", "parallel", "arbitrary")),
    )(a, *bs, *extras, *carried)
    return (list(res[:no]), list(res[no:])) if nc else res


def rowwise(fn, row_ins, par_ins, outs, accs=(), *, tm, ncb=1, name="rowwise"):
    R = row_ins[0][0].shape[0]
    assert R % tm == 0, (R, tm)
    nrb = R // tm
    in_specs, args = [], []
    for spec in row_ins:
        arr, w, base = spec[:3]
        mod = spec[3] if len(spec) > 3 else None
        cstep = 0 if (len(spec) > 4 and spec[4]) else 1
        if mod is None:
            in_specs.append(pl.BlockSpec((tm, w), lambda j, i, base=base, cstep=cstep: (i, base + cstep * j)))
        else:
            in_specs.append(pl.BlockSpec((tm, w), lambda j, i, base=base, mod=mod, cstep=cstep: (i % mod, base + cstep * j)))
        args.append(arr)
    for arr, w, base in par_ins:
        in_specs.append(pl.BlockSpec((arr.shape[0], w), lambda j, i, base=base: (0, base + j)))
        args.append(arr)
    out_specs, out_shape = [], []
    for cols, w, dt in outs:
        out_specs.append(pl.BlockSpec((tm, w), lambda j, i: (i, j)))
        out_shape.append(jax.ShapeDtypeStruct((R, cols), dt))
    for p, cols, w in accs:
        out_specs.append(pl.BlockSpec((p, w), lambda j, i: (0, j)))
        out_shape.append(jax.ShapeDtypeStruct((p, cols), F32))
    nin, nout, nacc = len(args), len(outs), len(accs)

    def body(*refs):
        vals = [r[...] for r in refs[:nin]]
        res = fn(*vals)
        if not isinstance(res, (tuple, list)):
            res = (res,)
        assert len(res) == nout + nacc, (len(res), nout, nacc)
        for o_ref, v in zip(refs[nin:nin + nout], res[:nout]):
            o_ref[...] = v.astype(o_ref.dtype)
        if nacc:
            i = pl.program_id(1)

            @pl.when(i == 0)
            def _():
                for a_ref in refs[nin + nout:]:
                    a_ref[...] = jnp.zeros_like(a_ref)

            for a_ref, v in zip(refs[nin + nout:], res[nout:]):
                a_ref[...] += v.astype(F32)

    r = _pcall(
        body, name=name, grid=(ncb, nrb), in_specs=in_specs, out_specs=out_specs, out_shape=out_shape,
        compiler_params=_cparams(("parallel", "arbitrary")),
    )(*args)
    return r


def vjp_fn(fwd, nprim):
    def f(*vals):
        prim, cts = vals[:nprim], vals[nprim:]
        out, pull = jax.vjp(fwd, *[p.astype(F32) for p in prim])
        if not isinstance(out, (tuple, list)):
            cts = cts[0].astype(F32)
        else:
            cts = tuple(c.astype(F32) for c in cts)
        return pull(cts)
    return f


WKV_TB = 16


def wkv_fwd(w, kn, b, k, r, v):
    T, N, L = w.shape
    NI = v.shape[1]
    tb = WKV_TB
    assert T % tb == 0 and L == v.shape[2]

    def body(w_ref, kn_ref, b_ref, k_ref, r_ref, v_ref, y_ref, sp_ref, sa_ref, s_ref):
        @pl.when(pl.program_id(0) == 0)
        def _():
            s_ref[...] = jnp.zeros_like(s_ref)

        def step(s, carry):
            W, KN, B, Kk, Rr = w_ref[s], kn_ref[s], b_ref[s], k_ref[s], r_ref[s]
            for i in range(NI):
                S = s_ref[i]
                sp_ref[s, i] = S
                sa = jnp.sum(S * KN, axis=0, keepdims=True)
                sa_ref[s, pl.ds(i, 1), :] = sa
                vi = v_ref[s, pl.ds(i, 1), :]
                Sn = S * W + sa * B + vi * Kk
                s_ref[i] = Sn
                y_ref[s, pl.ds(i, 1), :] = jnp.sum(Sn * Rr, axis=0, keepdims=True)
            return carry

        lax.fori_loop(0, tb, step, 0)

    jspec = pl.BlockSpec((tb, N, L), lambda t: (t, 0, 0))
    ispec = pl.BlockSpec((tb, NI, L), lambda t: (t, 0, 0))
    ishape = jax.ShapeDtypeStruct((T, NI, L), F32)
    return _pcall(
        body, name="wkv_fwd", grid=(T // tb,), in_specs=[jspec] * 5 + [ispec],
        out_specs=[ispec, pl.BlockSpec((tb, NI, N, L), lambda t: (t, 0, 0, 0)), ispec],
        out_shape=[ishape, jax.ShapeDtypeStruct((T, NI, N, L), F32), ishape],
        scratch_shapes=[pltpu.VMEM((NI, N, L), F32)],
        compiler_params=_cparams(("arbitrary",)),
    )(w, kn, b, k, r, v)


def wkv_bwd(w, kn, b, k, r, v, dy, sp, sa):
    T, N, L = w.shape
    NI, LH = v.shape[1], L // 2
    tb = WKV_TB
    nt = T // tb

    def body(w_ref, kn_ref, b_ref, k_ref, r_ref, v_ref, dy_ref, sp_ref, sa_ref,
             dw_ref, dkn_ref, db_ref, dk_ref, dr_ref, dv_ref, ds_ref):
        @pl.when(pl.program_id(0) == 0)
        def _():
            ds_ref[...] = jnp.zeros_like(ds_ref)

        def step(q, carry):
            s = tb - 1 - q
            W, KN, B, Kk, Rr = w_ref[s], kn_ref[s], b_ref[s], k_ref[s], r_ref[s]
            dW = jnp.zeros((N, L), F32)
            dKN, dB, dK, T1 = dW, dW, dW, dW
            al = jnp.zeros((1, L), F32)
            be = al
            for i in range(NI):
                Sp = sp_ref[s, i]
                vi = v_ref[s, pl.ds(i, 1), :]
                dyi = dy_ref[s, pl.ds(i, 1), :]
                sai = sa_ref[s, pl.ds(i, 1), :]
                dS = ds_ref[i] + dyi * Rr
                T1 = T1 + Sp * dyi
                al = al + sai * dyi
                be = be + vi * dyi
                dv_ref[s, pl.ds(i, 1), :] = jnp.sum(dS * Kk, axis=0, keepdims=True)
                dK = dK + dS * vi
                dsa = jnp.sum(dS * B, axis=0, keepdims=True)
                dB = dB + dS * sai
                dW = dW + dS * Sp
                dKN = dKN + Sp * dsa
                ds_ref[i] = dS * W + dsa * KN
            dR = W * T1 + B * al + Kk * be
            for ref, val in ((dw_ref, dW), (dkn_ref, dKN), (db_ref, dB), (dk_ref, dK), (dr_ref, dR)):
                ref[s] = (val + pltpu.roll(val, LH, 1))[:, :LH]
            return carry

        lax.fori_loop(0, tb, step, 0)

    jspec = pl.BlockSpec((tb, N, L), lambda t: (nt - 1 - t, 0, 0))
    gspec = pl.BlockSpec((tb, N, LH), lambda t: (nt - 1 - t, 0, 0))
    ispec = pl.BlockSpec((tb, NI, L), lambda t: (nt - 1 - t, 0, 0))
    gshape = jax.ShapeDtypeStruct((T, N, LH), F32)
    return _pcall(
        body, name="wkv_bwd", grid=(nt,),
        in_specs=[jspec] * 5 + [ispec, ispec, pl.BlockSpec((tb, NI, N, L), lambda t: (nt - 1 - t, 0, 0, 0)), ispec],
        out_specs=[gspec] * 5 + [ispec], out_shape=[gshape] * 5 + [jax.ShapeDtypeStruct((T, NI, L), F32)],
        scratch_shapes=[pltpu.VMEM((NI, N, L), F32)],
        compiler_params=_cparams(("arbitrary",)),
    )(w, kn, b, k, r, v, dy, sp, sa)


_NT = (((1,), (1,)), ((), ()))
_TN = (((0,), (0,)), ((), ()))
ATT_SCALE = QK_DIM ** -0.5


def _att_bias():
    col = lax.broadcasted_iota(jnp.int32, (Q_BLOCK, Q_BLOCK), 1)
    row = lax.broadcasted_iota(jnp.int32, (Q_BLOCK, Q_BLOCK), 0)
    return jnp.where(col < N_META, 0.0, -1e30).astype(F32), jnp.where(col <= row, 0.0, -1e30).astype(F32)


def _att_scores(q1, q2, kn_ref, kp_ref, s_ref, bias, L):
    s = lax.dot_general(q1, kn_ref[0, :L, :], _NT, preferred_element_type=F32)
    s = s + lax.dot_general(q2, kp_ref[0, :L, :], _NT, preferred_element_type=F32)
    s_ref[:, :L] = s * ATT_SCALE
    s_ref[:, :Q_BLOCK] += bias[0]
    s_ref[:, L - Q_BLOCK:L] += bias[1]


def _att_probs(s_ref, L):
    s = s_ref[:, :L]
    m = jnp.max(s, axis=-1, keepdims=True)
    p = jnp.exp(s - m)
    return p / jnp.sum(p, axis=-1, keepdims=True)


def _att_blocks(nq, qn_ref, qp_ref, kn_ref, kp_ref, s_ref):
    bias = _att_bias()
    rows = lambda i: pl.ds(Q_BLOCK * i, Q_BLOCK)
    L = lambda i: Q_BLOCK * (i + 2)
    score = lambda i: _att_scores(qn_ref[0, rows(i), :], qp_ref[0, rows(i), :], kn_ref, kp_ref, s_ref.at[i % 2], bias, L(i))
    score(0)
    for i in range(nq):
        if i + 1 < nq:
            score(i + 1)
        yield i, rows(i), L(i), _att_probs(s_ref.at[i % 2], L(i))


def _edge_steps(nb, nh):
    b, h = pl.program_id(0), pl.program_id(1)
    return (b == 0) & (h == 0), (b == nb - 1) & (h == nh - 1)


def attn_fwd(qn, qp, kn, kp, v, bufs=()):
    B, S, HD = qn.shape
    SP = kn.shape[1]
    H = HD // LANE
    nq = S // Q_BLOCK
    n = len(bufs)

    def body(*refs):
        qn_ref, qp_ref, kn_ref, kp_ref, v_ref = refs[:5]
        o_ref = refs[5 + n]
        s_ref = refs[6 + 2 * n]
        if n:
            start, finish = _ag_steps(refs[5:5 + n], refs[6 + n:6 + 2 * n], *refs[7 + 2 * n:])
            is_first, is_last = _edge_steps(B, H)
            pl.when(is_first)(start)
        for i, rows, L, p in _att_blocks(nq, qn_ref, qp_ref, kn_ref, kp_ref, s_ref):
            o_ref[0, rows, :] = jnp.dot(p.astype(BF16), v_ref[0, :L, :], preferred_element_type=F32)
        if n:
            pl.when(is_last)(finish)

    qspec = pl.BlockSpec((1, S, LANE), lambda b, h: (b, 0, h))
    kspec = pl.BlockSpec((1, SP, LANE), lambda b, h: (b, 0, h))
    pspec = pl.BlockSpec((1, SP, LANE), lambda b, h: (b, 0, 0))
    res = _pcall(
        body, name="attn_fwd", grid=(B, H), in_specs=[qspec, qspec, kspec, pspec, kspec] + [_HBM] * n,
        out_specs=[qspec] + [_HBM] * n,
        out_shape=[jax.ShapeDtypeStruct((B, S, HD), F32)] + [jax.ShapeDtypeStruct(w.shape, w.dtype) for w in bufs],
        input_output_aliases={5 + i: 1 + i for i in range(n)},
        scratch_shapes=[pltpu.VMEM((2, Q_BLOCK, SP), F32)] + (_ag_sems(n) if n else []),
        compiler_params=_cparams(("arbitrary", "arbitrary")),
    )(qn, qp, kn, kp, v, *bufs)
    return res[0], list(res[1:])


def attn_bwd(qn, qp, kn, kp, v, do, hs=()):
    B, S, HD = qn.shape
    SP = kn.shape[1]
    H = HD // LANE
    nq = S // Q_BLOCK
    n = len(hs)

    def body(*refs):
        qn_ref, qp_ref, kn_ref, kp_ref, v_ref, do_ref = refs[:6]
        dqn_ref, dqp_ref, dkn_ref, dv_ref, dkp_ref = refs[6 + n:11 + n]
        dkn_acc, dv_acc, s_ref = refs[11 + 2 * n:14 + 2 * n]
        if n:
            start, finish = _rs_steps(refs[6:6 + n], refs[11 + n:11 + 2 * n], *refs[14 + 2 * n:])
            is_first, is_last = _edge_steps(B, H)
            pl.when(is_first)(start)

        @pl.when(pl.program_id(1) == 0)
        def _():
            dkp_ref[...] = jnp.zeros_like(dkp_ref)

        dkn_acc[...] = jnp.zeros_like(dkn_acc)
        dv_acc[...] = jnp.zeros_like(dv_acc)
        for i, rows, L, p in _att_blocks(nq, qn_ref, qp_ref, kn_ref, kp_ref, s_ref):
            q1, q2, do_i = qn_ref[0, rows, :], qp_ref[0, rows, :], do_ref[0, rows, :]
            dp = lax.dot_general(do_i, v_ref[0, :L, :], _NT, preferred_element_type=F32)
            ds = (p * (dp - jnp.sum(p * dp, axis=-1, keepdims=True)) * ATT_SCALE).astype(BF16)
            dqn_ref[0, rows, :] = jnp.dot(ds, kn_ref[0, :L, :], preferred_element_type=F32).astype(dqn_ref.dtype)
            dqp_ref[0, rows, :] = jnp.dot(ds, kp_ref[0, :L, :], preferred_element_type=F32)
            dkn_acc[:L, :] += lax.dot_general(ds, q1, _TN, preferred_element_type=F32)
            dkp_ref[0, :L, :] += lax.dot_general(ds, q2, _TN, preferred_element_type=F32)
            dv_acc[:L, :] += lax.dot_general(p.astype(BF16), do_i, _TN, preferred_element_type=F32)
        dkn_ref[0] = dkn_acc[...].astype(dkn_ref.dtype)
        dv_ref[0] = dv_acc[...].astype(dv_ref.dtype)
        if n:
            pl.when(is_last)(finish)

    qspec = pl.BlockSpec((1, S, LANE), lambda b, h: (b, 0, h))
    kspec = pl.BlockSpec((1, SP, LANE), lambda b, h: (b, 0, h))
    pspec = pl.BlockSpec((1, SP, LANE), lambda b, h: (b, 0, 0))
    res = _pcall(
        body, name="attn_bwd", grid=(B, H), in_specs=[qspec, qspec, kspec, pspec, kspec, qspec] + [_HBM] * n,
        out_specs=[qspec, qspec, kspec, kspec, pspec] + [_HBM] * n,
        out_shape=[jax.ShapeDtypeStruct((B, S, HD), BF16), jax.ShapeDtypeStruct((B, S, HD), F32),
                   jax.ShapeDtypeStruct((B, SP, HD), BF16), jax.ShapeDtypeStruct((B, SP, HD), BF16),
                   jax.ShapeDtypeStruct((B, SP, LANE), F32)] + [jax.ShapeDtypeStruct(h.shape, h.dtype) for h in hs],
        scratch_shapes=[pltpu.VMEM((SP, LANE), F32), pltpu.VMEM((SP, LANE), F32), pltpu.VMEM((2, Q_BLOCK, SP), F32)]
        + (_rs_sems(n) if n else []),
        compiler_params=_cparams(("arbitrary", "arbitrary")),
    )(qn, qp, kn, kp, v, do, *hs)
    return res[:5], list(res[5:])


_HBM = pl.BlockSpec(memory_space=pltpu.HBM)


def _place():
    x, y, c = lax.axis_index("x"), lax.axis_index("y"), lax.axis_index("c")
    chips = [(1 - x, y), (x, 1 - y), (1 - x, 1 - y)]
    return x, y, c, chips


def _rcopy(src, dst, ssem, rsem, dev):
    return pltpu.make_async_remote_copy(src_ref=src, dst_ref=dst, send_sem=ssem, recv_sem=rsem,
                                        device_id=dev, device_id_type=MESH)


def _half_rows(c, r):
    return pl.ds(pl.multiple_of(c * (r // 2), 16), r // 2)


def _ag_steps(w, out, ssem, rsem):
    n = len(w)
    x, y, c, chips = _place()
    s = 2 * x + y
    rows = [_half_rows(c, w[i].shape[1]) for i in range(n)]
    orows = [_half_rows(1 - c, w[i].shape[1]) for i in range(n)]
    first = [_rcopy(w[i].at[s, rows[i]], out[i].at[s, rows[i]], ssem.at[6 * i + j], rsem.at[6 * i + j], (px, py, c))
             for i in range(n) for j, (px, py) in enumerate(chips)]

    def start():
        for cp in first:
            cp.start()

    def finish():
        passed = []
        for j, (px, py) in enumerate(chips):
            sp = 2 * px + py
            for i in range(n):
                here = out[i].at[sp, rows[i]]
                _rcopy(here, here, ssem.at[6 * i + j], rsem.at[6 * i + j], (px, py, c)).wait_recv()
                fw = _rcopy(here, here, ssem.at[6 * i + 3 + j], rsem.at[6 * i + 3 + j], (x, y, 1 - c))
                fw.start()
                passed.append(fw)
        for j, (px, py) in enumerate(chips):
            sp = 2 * px + py
            for i in range(n):
                there = out[i].at[sp, orows[i]]
                _rcopy(there, there, ssem.at[6 * i + 3 + j], rsem.at[6 * i + 3 + j], (x, y, 1 - c)).wait_recv()
        for cp in first + passed:
            cp.wait_send()

    return start, finish


def _ag_sems(n):
    return [pltpu.SemaphoreType.DMA((6 * n,)), pltpu.SemaphoreType.DMA((6 * n,))]


def ag_weights(bufs):
    n = len(bufs)

    def body(*refs):
        start, finish = _ag_steps(refs[:n], refs[n:2 * n], *refs[2 * n:])
        start()
        finish()

    return _pcall(
        body, name="ag_weights", in_specs=[_HBM] * n, out_specs=[_HBM] * n,
        out_shape=[jax.ShapeDtypeStruct(w.shape, w.dtype) for w in bufs],
        input_output_aliases={i: i for i in range(n)}, scratch_shapes=_ag_sems(n),
    )(*bufs)


def swap_halves(gs, name):
    n = len(gs)

    def body(*refs):
        g, out = refs[:n], refs[n:2 * n]
        ssem, rsem = refs[2 * n:]
        x, y, c, _ = _place()
        cps = [_rcopy(g[i].at[:, _half_rows(1 - c, g[i].shape[1])], out[i], ssem.at[i], rsem.at[i], (x, y, 1 - c))
               for i in range(n)]
        for cp in cps:
            cp.start()
        for cp in cps:
            cp.wait()

    return _pcall(
        body, name=name, in_specs=[_HBM] * n, out_specs=[_HBM] * n,
        out_shape=[jax.ShapeDtypeStruct((4, g.shape[1] // 2, g.shape[2]), g.dtype) for g in gs],
        scratch_shapes=[pltpu.SemaphoreType.DMA((n,)), pltpu.SemaphoreType.DMA((n,))],
    )(*gs)


def _rs_steps(h, out, ssem, rsem):
    n = len(h)
    x, y, c, chips = _place()
    s = 2 * x + y
    cps = [_rcopy(h[i].at[2 * px + py], out[i].at[s], ssem.at[3 * i + j], rsem.at[3 * i + j], (px, py, c))
           for i in range(n) for j, (px, py) in enumerate(chips)]

    def start():
        for cp in cps:
            cp.start()

    def finish():
        for j, (px, py) in enumerate(chips):
            for i in range(n):
                _rcopy(h[i].at[s], out[i].at[2 * px + py], ssem.at[3 * i + j], rsem.at[3 * i + j], (px, py, c)).wait_recv()
        for cp in cps:
            cp.wait_send()

    return start, finish


def _rs_sems(n):
    return [pltpu.SemaphoreType.DMA((3 * n,)), pltpu.SemaphoreType.DMA((3 * n,))]


def rs_chips(hs):
    n = len(hs)

    def body(*refs):
        start, finish = _rs_steps(refs[:n], refs[n:2 * n], *refs[2 * n:])
        start()
        finish()

    return _pcall(
        body, name="rs_chips", in_specs=[_HBM] * n, out_specs=[_HBM] * n,
        out_shape=[jax.ShapeDtypeStruct(h.shape, h.dtype) for h in hs], scratch_shapes=_rs_sems(n),
    )(*hs)


def share_sibling(ts):
    n = len(ts)

    def body(*refs):
        t, out = refs[:n], refs[n:2 * n]
        ssem, rsem = refs[2 * n:]
        x, y, c, _ = _place()
        cps = [_rcopy(t[i].at[c], out[i].at[c], ssem.at[i], rsem.at[i], (x, y, 1 - c)) for i in range(n)]
        for cp in cps:
            cp.start()
        for i in range(n):
            _rcopy(t[i].at[c], out[i].at[1 - c], ssem.at[i], rsem.at[i], (x, y, 1 - c)).wait_recv()
        for cp in cps:
            cp.wait_send()

    return _pcall(
        body, name="share_sibling", in_specs=[_HBM] * n, out_specs=[_HBM] * n,
        out_shape=[jax.ShapeDtypeStruct(t.shape, t.dtype) for t in ts],
        input_output_aliases={i: i for i in range(n)},
        scratch_shapes=[pltpu.SemaphoreType.DMA((n,)), pltpu.SemaphoreType.DMA((n,))],
    )(*ts)


def allreduce8(v, name):
    P, W = v.shape

    def body(v_ref, out_ref, buf, ssem, rsem):
        x, y, c, _ = _place()
        me = 4 * x + 2 * y + c
        buf[me] = v_ref[...]
        cps = []
        for k in range(1, 8):
            px = 1 - x if k & 4 else x
            py = 1 - y if k & 2 else y
            pc = 1 - c if k & 1 else c
            cp = _rcopy(buf.at[me], buf.at[me], ssem.at[k - 1], rsem.at[k - 1], (px, py, pc))
            cp.start()
            cps.append((cp, 4 * px + 2 * py + pc))
        for k, (cp, peer) in enumerate(cps):
            _rcopy(buf.at[me], buf.at[peer], ssem.at[k], rsem.at[k], (x, y, c)).wait_recv()
        for cp, _ in cps:
            cp.wait_send()
        acc = buf[0]
        for d in range(1, 8):
            acc = acc + buf[d]
        out_ref[...] = acc

    return _pcall(
        body, name=name, in_specs=[pl.BlockSpec(memory_space=pltpu.VMEM)],
        out_specs=pl.BlockSpec(memory_space=pltpu.VMEM), out_shape=jax.ShapeDtypeStruct((P, W), F32),
        scratch_shapes=[pltpu.VMEM((8, P, W), F32), pltpu.SemaphoreType.DMA((7,)), pltpu.SemaphoreType.DMA((7,))],
    )(v)


def add_half(g, rcv, cidx, name):
    _, hr, W = rcv.shape
    tr = _rows_tile(hr, W)
    nb = hr // tr

    def body(c_ref, g_ref, r_ref, o_ref):
        o_ref[...] = (g_ref[...] + r_ref[...]).astype(o_ref.dtype)

    return _pcall(
        body, name=name,
        grid_spec=pltpu.PrefetchScalarGridSpec(
            num_scalar_prefetch=1, grid=(4, nb),
            in_specs=[pl.BlockSpec((1, tr, W), lambda s, i, c: (s, c[0] * nb + i, 0)),
                      pl.BlockSpec((1, tr, W), lambda s, i, c: (s, i, 0))],
            out_specs=pl.BlockSpec((1, tr, W), lambda s, i, c: (s, i, 0))),
        out_shape=jax.ShapeDtypeStruct((4, hr, W), BF16), compiler_params=_cparams(("parallel", "parallel")),
    )(cidx, g, rcv)


def sum_chips(p, h, sc, name):
    _, hr, W = p.shape
    tr = _rows_tile(hr, W)

    def body(s_ref, c_ref, p_ref, h_ref, o_ref):
        s = s_ref[0]
        own = h_ref[0]
        f = lambda k: jnp.where(s == k, own, p_ref[k]).astype(F32)
        o_ref[0] = ((f(0) + f(1)) + f(2)) + f(3)

    return _pcall(
        body, name=name,
        grid_spec=pltpu.PrefetchScalarGridSpec(
            num_scalar_prefetch=2, grid=(hr // tr,),
            in_specs=[pl.BlockSpec((4, tr, W), lambda i, s, c: (0, i, 0)),
                      pl.BlockSpec((1, tr, W), lambda i, s, c: (s[0], i, 0))],
            out_specs=pl.BlockSpec((1, tr, W), lambda i, s, c: (c[0], i, 0))),
        out_shape=jax.ShapeDtypeStruct((2, hr, W), F32), compiler_params=_cparams(("parallel",)),
    )(sc[0], sc[1], p, h)


HI = lax.Precision.HIGHEST
BLOCK_BYTES = 3 * 512 * 1024


def _rows_tile(R, w, T=None):
    cands = [t for t in range(16, R + 1, 16) if R % t == 0 and (T is None or T % t == 0)]
    ok = [t for t in cands if t * w * 4 <= BLOCK_BYTES]
    return max(ok) if ok else min(cands)


def _gsum_exact(x):
    r = jnp.right_shift(lax.broadcasted_iota(jnp.int32, (LANE, LANE), 0), 6)
    c = jnp.right_shift(lax.broadcasted_iota(jnp.int32, (LANE, LANE), 1), 6)
    g = (r == c).astype(BF16)
    x1 = x.astype(BF16)
    r1 = x - x1.astype(F32)
    x2 = r1.astype(BF16)
    x3 = (r1 - x2.astype(F32)).astype(BF16)
    dot = lambda z: jnp.dot(z, g, preferred_element_type=F32)
    return (dot(x3) + dot(x2)) + dot(x1)


@jax.custom_vjp
def _gsum(x):
    return _gsum_exact(x)


_gsum.defvjp(lambda x: (_gsum_exact(x), None), lambda _, ct: (_gsum_exact(ct),))


def _rms(x, g):
    return x * lax.rsqrt(jnp.mean(x * x, axis=-1, keepdims=True) + NORM_EPS) * g


def _silu_mul(gate, up):
    return jax.nn.silu(gate) * up


def _shift(p, prev, mu):
    return p + mu * (prev - p)


def _lora_act(p, prev, mu):
    s = _shift(p, prev, mu)
    return jax.nn.sigmoid(s[:, :G_LORA]), jnp.tanh(s[:, G_LORA:G_LORA + LANE]), s[:, G_LORA + LANE:]


def _prep(k, lw, la, w0, a0, kk_w, ka_w):
    wpre = -jax.nn.softplus(-(w0 + lw)) - 0.5
    decay = jnp.exp(-jnp.exp(wpre))
    a = jax.nn.sigmoid(a0 + la)
    kk = k * kk_w
    kk = kk * lax.rsqrt(jnp.maximum(_gsum(kk * kk), 1e-24))
    k2 = k * (1.0 + (a - 1.0) * ka_w)
    return decay, -kk, kk * a, k2


def _post(y, r, k2, v, g, gnw, gnb, rk):
    mean = _gsum(y) * (1.0 / RWKV_HEAD)
    d = y - mean
    var = _gsum(d * d) * (1.0 / RWKV_HEAD)
    yn = d * lax.rsqrt(var + GN_EPS) * gnw + gnb
    bonus = _gsum(r * k2 * rk) * v
    return (yn + bonus) * g


def _gate_mix(ga, gb, ya, o):
    return jax.nn.sigmoid(ga) * ya + jax.nn.sigmoid(gb) * o


def _rope(x, cos, sin):
    return x * cos + pltpu.roll(x, LANE // 2, 1) * sin


def _rope_t(dy, cos, sin):
    return dy * cos + pltpu.roll(dy * sin, LANE // 2, 1)


def _mla_pre(cq, ckv, kpe, cos, sin, qw, kvw):
    return _rms(cq, qw), _rms(ckv, kvw), _rope(kpe, cos, sin)


def _mla_pre_bwd(cq, ckv, dcqn, dckvn, dkr, cos, sin, qw, kvw):
    _, pull = jax.vjp(lambda a, b, c, d: (_rms(a, c), _rms(b, d)), cq, ckv, qw, kvw)
    dcq, dckv, dqw, dkvw = pull((dcqn, dckvn))
    return dcq, dckv, _rope_t(dkr, cos, sin), dqw, dkvw


WEIGHTS = ['meta_tokens', 'ffn1_norm', 'ffn1_w_gate', 'ffn1_w_up', 'ffn1_w_down', 'mix_norm', 'w_in', 'tm_mu', 'w0',
           'w_up', 'a0', 'a_up', 'g_up', 'k_k', 'k_a', 'r_k', 'gn_w', 'gn_b', 'q_norm', 'w_uq', 'kv_norm', 'w_ukv',
           'w_out', 'ffn2_norm', 'ffn2_w_gate', 'ffn2_w_up', 'ffn2_w_down', 'final_norm']
SHARD_AXIS = {'meta_tokens': 1, 'ffn1_w_gate': 1, 'ffn1_w_up': 1, 'ffn1_w_down': 0, 'w_in': 1, 'w_up': 1, 'a_up': 1,
              'g_up': 1, 'w_uq': 1, 'w_ukv': 1, 'w_out': 0, 'ffn2_w_gate': 1, 'ffn2_w_up': 1, 'ffn2_w_down': 0}
GATHERED = [n for n in WEIGHTS if n in SHARD_AXIS and n != 'meta_tokens']
FFN_IN = ('ffn1_w_gate', 'ffn1_w_up', 'ffn2_w_gate', 'ffn2_w_up')
PART_B = ['ffn2_w_gate', 'ffn2_w_up', 'ffn2_w_down', 'w_out']
PART_A1 = ['ffn1_w_gate', 'ffn1_w_up', 'ffn1_w_down']
PART_A2 = [n for n in GATHERED if n not in PART_B and n not in PART_A1]
SMALL = [n for n in WEIGHTS if n not in SHARD_AXIS]


def _to2d(a):
    if a.ndim == 1:
        return a.reshape(1, -1)
    if a.ndim == 3:
        return a.reshape(a.shape[0] * a.shape[1], a.shape[2]) if a.shape[0] == 1 and a.shape[1] > 64 else a.reshape(1, -1)
    return a


def _unpack(flat, shapes):
    out, off = [], 0
    for shp in shapes:
        n = shp[0] * shp[1]
        out.append(flat[off:off + n].reshape(shp))
        off += n
    return out


def _adamw(w, g, m, v):
    m = ADAM_B1 * m + (1.0 - ADAM_B1) * g
    v = ADAM_B2 * v + (1.0 - ADAM_B2) * jnp.square(g)
    m_hat = m / (1.0 - ADAM_B1 ** ADAM_STEP)
    v_hat = v / (1.0 - ADAM_B2 ** ADAM_STEP)
    delta = -ADAM_LR * (m_hat / (jnp.sqrt(v_hat) + ADAM_EPS) + ADAM_WD * w)
    return delta, m, v


def adamw(w, g, m, v, name):
    R, C = w.shape
    if R % 16 == 0 and R > 16:
        tm = _rows_tile(R, C)
    else:
        tm = R
    return rowwise(_adamw, [(w, C, 0), (g, C, 0), (m, C, 0), (v, C, 0)], [], [(C, C, F32)] * 3, tm=tm, name=name)


def _step(a):
    x = a['x']
    Bl, S, D = x.shape
    T = S + N_META
    R, RS = Bl * T, Bl * S
    Hr, Hm = D // RWKV_HEAD, D // LANE
    w2 = {n: _to2d(a[n]) for n in WEIGHTS}
    m2 = {n: _to2d(a['m_' + n]) for n in WEIGHTS}
    v2 = {n: _to2d(a['v_' + n]) for n in WEIGHTS}
    xi, yi, ci = lax.axis_index("x"), lax.axis_index("y"), lax.axis_index("c")
    chip = 2 * xi + yi

    i0 = jnp.zeros((), jnp.int32)
    chip32 = chip.astype(jnp.int32)

    def own_slot(w):
        return lax.dynamic_update_slice(jnp.zeros((4,) + w.shape, BF16), w.astype(BF16)[None], (chip32, i0, i0))

    stk = dict(zip(PART_A1, ag_weights([own_slot(w2[n]) for n in PART_A1])))

    def unstack(z, axis):
        return z.reshape(4 * z.shape[1], z.shape[2]) if axis == 0 else jnp.concatenate([z[s] for s in range(4)], axis=1)

    full = {n: unstack(stk[n], SHARD_AXIS[n]) for n in PART_A1 if n not in FFN_IN}
    mt = w2['meta_tokens']
    mcols = mt.shape[1]
    mt_z = lax.dynamic_update_slice(jnp.zeros((N_META, D), F32), 0.5 * mt, (jnp.zeros((), jnp.int32), (chip * mcols).astype(jnp.int32)))
    meta_full = allreduce8(mt_z.reshape(-1, LANE), "gather_meta").reshape(N_META, D)

    F = 4 * stk['ffn1_w_gate'].shape[2]
    half = ROPE_DIM // 2
    tmu = w2['tm_mu']
    mu_a = tmu[:, :3 * D]
    zm = lambda n: jnp.zeros((1, n), F32)
    mu_b = jnp.concatenate([tmu[:, 3 * D + 2 * W_LORA:], tmu[:, 3 * D:3 * D + W_LORA], zm(LANE - W_LORA),
                            tmu[:, 3 * D + W_LORA:3 * D + 2 * W_LORA], zm(LANE - A_LORA)], axis=1)
    pos = jnp.arange(T, dtype=F32)
    inv_freq = 1.0 / (ROPE_THETA ** (jnp.arange(0, ROPE_DIM, 2, dtype=F32) / ROPE_DIM))
    ang = pos[:, None] * inv_freq[None, :]
    zt = jnp.zeros((T, half), F32)
    cos_t = jnp.concatenate([jnp.cos(ang), zt, jnp.cos(ang), zt], axis=1)
    sin_t = jnp.concatenate([-jnp.sin(ang), zt, jnp.sin(ang), zt], axis=1)
    cos_q, sin_q = cos_t[N_META:], sin_t[N_META:]

    t_full = _rows_tile(R, D)
    t_512 = _rows_tile(R, 512, T)
    t_128 = _rows_tile(R, LANE, T)
    tq_full = _rows_tile(RS, D)
    tq_128 = _rows_tile(RS, LANE, S)

    def rows3(z):
        return z.reshape(Bl, T, z.shape[-1])

    def real_rows(z):
        return rows3(z)[:, N_META:].reshape(RS, z.shape[-1])

    def pad_meta(z):
        z3 = z.reshape(Bl, S, z.shape[-1])
        return jnp.concatenate([jnp.zeros((Bl, N_META, z.shape[-1]), z.dtype), z3], axis=1).reshape(R, z.shape[-1])

    def shift_down(z):
        z3 = rows3(z)
        return jnp.concatenate([jnp.zeros((Bl, 1, z.shape[-1]), z.dtype), z3[:, :-1]], axis=1).reshape(R, z.shape[-1])

    def shift_up(z):
        z3 = rows3(z)
        return jnp.concatenate([z3[:, 1:], jnp.zeros((Bl, 1, z.shape[-1]), z.dtype)], axis=1).reshape(R, z.shape[-1])

    def ffn_fwd(h, nw, wg, wu, wd, tag, ag_bufs=()):
        n = rowwise(_rms, [(h, D, 0)], [(nw, D, 0)], [(D, D, BF16)], tm=t_full, name=tag + "_norm")[0]
        res = mm_epi(n, [wg, wu], "nn", lambda g_, u_: (g_, u_, _silu_mul(g_, u_)), [], [F32, F32, BF16], b_stack=True,
                     carry=("ag", ag_bufs) if ag_bufs else None, name=tag + "_gate_up")
        (gate, up, act), got = res if ag_bufs else (res, [])
        out = mm(act, wd, res=h, alpha=0.5, name=tag + "_down")
        return (out, (h, n, gate, up, act), got) if ag_bufs else (out, (h, n, gate, up, act))

    def ffn_bwd(dh2, saved, nw, wg, wu, wd, tag, rs_hs=()):
        h, n, gate, up, act = saved
        dz = rowwise(lambda d: 0.5 * d, [(dh2, 512, 0)], [], [(D, 512, BF16)], tm=t_512, ncb=D // 512, name=tag + "_dz")[0]
        d_wd = mm(act, dz, "tn", name=tag + "_dwd")
        res = mm_epi(dz, [wd], "nt", lambda da, g_, u_: vjp_fn(_silu_mul, 2)(g_, u_, da), [gate, up], [BF16, BF16],
                     carry=("rs", rs_hs) if rs_hs else None, name=tag + "_dact")
        (dgate, dup), got = res if rs_hs else (res, [])
        d_wg = mm(n, dgate, "tn", out_stack=True, name=tag + "_dwg")
        d_wu = mm(n, dup, "tn", out_stack=True, name=tag + "_dwu")
        dn = mm(dgate, wg, "nt", b_stack=True, name=tag + "_dn1")
        dn = mm(dup, wu, "nt", res=dn, b_stack=True, name=tag + "_dn2")

        def f(h_, dn_, dh_, nw_):
            dh, dnw = vjp_fn(_rms, 2)(h_, nw_, dn_)
            return dh + dh_, dnw

        dh, d_nw = rowwise(f, [(h, D, 0), (dn, D, 0), (dh2, D, 0)], [(nw, D, 0)], [(D, D, F32)], [(1, D, D)],
                           tm=t_full, name=tag + "_dnorm")
        return (dh, d_nw, d_wg, d_wu, d_wd, got) if rs_hs else (dh, d_nw, d_wg, d_wu, d_wd)

    h0 = jnp.concatenate([jnp.broadcast_to(meta_full[None], (Bl, N_META, D)), x], axis=1).reshape(R, D)
    h1, sv1, got_a2 = ffn_fwd(h0, w2['ffn1_norm'], stk['ffn1_w_gate'], stk['ffn1_w_up'], full['ffn1_w_down'], "ffn1",
                              [own_slot(w2[n]) for n in PART_A2])
    stk.update(zip(PART_A2, got_a2))
    full.update({n: unstack(stk[n], SHARD_AXIS[n]) for n in PART_A2})
    win = full['w_in']
    o = 3 * D
    c_xw, c_xa, c_xg = win[:, o:o + W_LORA], win[:, o + W_LORA:o + 2 * W_LORA], win[:, o + 2 * W_LORA:o + 2 * W_LORA + G_LORA]
    o += 2 * W_LORA + G_LORA
    c_cq, c_ckv, c_kpe = win[:, o:o + Q_LORA], win[:, o + Q_LORA:o + Q_LORA + KV_LORA], win[:, o + Q_LORA + KV_LORA:o + Q_LORA + KV_LORA + ROPE_DIM]
    o += Q_LORA + KV_LORA + ROPE_DIM
    c_ga, c_gb = win[:, o:o + D], win[:, o + D:o + 2 * D]
    zc = lambda n: jnp.zeros((D, n), BF16)
    half = ROPE_DIM // 2
    NP0 = 5 * D + 512 + Q_LORA + KV_LORA + LANE
    NP = -(-NP0 // 512) * 512
    win_p = jnp.concatenate([win[:, :3 * D], c_ga, c_gb, c_xg, c_xw, zc(LANE - W_LORA), c_xa, zc(LANE - A_LORA), c_cq, c_ckv,
                             c_kpe[:, :half], zc(half), c_kpe[:, half:], zc(half), zc(NP - NP0)], axis=1)
    O_GA, O_GB, O_L, O_CQ, O_CKV, O_KPE = 3 * D, 4 * D, 5 * D, 5 * D + 512, 5 * D + 512 + Q_LORA, 5 * D + 512 + Q_LORA + KV_LORA
    zr = lambda n: jnp.zeros((n, D), BF16)
    w_up_p = jnp.concatenate([full['w_up'], zr(LANE - W_LORA)], axis=0)
    a_up_p = jnp.concatenate([full['a_up'], zr(LANE - A_LORA)], axis=0)
    g_up = full['g_up']
    wuq = full['w_uq'].reshape(Q_LORA, Hm, QK_DIM)
    zq = jnp.zeros((Q_LORA, Hm, half), BF16)
    wqn = wuq[:, :, :NOPE_DIM].reshape(Q_LORA, Hm * LANE)
    wqp = jnp.concatenate([wuq[:, :, NOPE_DIM:NOPE_DIM + half], zq, wuq[:, :, NOPE_DIM + half:], zq], axis=2).reshape(Q_LORA, Hm * LANE)
    wukv = full['w_ukv'].reshape(KV_LORA, Hm, NOPE_DIM + V_DIM)
    wkn = wukv[:, :, :NOPE_DIM].reshape(KV_LORA, Hm * LANE)
    wv = wukv[:, :, NOPE_DIM:].reshape(KV_LORA, Hm * LANE)
    u = rowwise(_rms, [(h1, D, 0)], [(w2['mix_norm'], D, 0)], [(D, D, BF16)], tm=t_full, name="mix_norm")[0]
    proj = mm(u, win_p, name="proj")
    prev_a = shift_down(proj[:, :3 * D])
    prev_b = shift_down(proj[:, O_L:O_L + 512])
    ps = rowwise(_shift, [(proj, 512, 0), (prev_a, 512, 0)], [(mu_a, 512, 0)], [(3 * D, 512, F32)], tm=t_512,
                 ncb=3 * D // 512, name="shift_rkv")[0]
    sg, txw, xas = rowwise(_lora_act, [(proj, 512, O_L // 512), (prev_b, 512, 0)], [(mu_b, 512, 0)],
                           [(G_LORA, G_LORA, BF16), (LANE, LANE, BF16), (LANE, LANE, BF16)], tm=t_512, name="lora_act")
    lw = mm(txw, w_up_p, name="lora_w")
    la = mm(xas, a_up_p, name="lora_a")
    g = mm(sg, g_up, name="lora_g")
    hb = D // LANE
    par_d = lambda n: (w2[n], LANE, 0)
    decay, kn, bb, k2 = rowwise(_prep, [(ps, LANE, hb), (lw, LANE, 0), (la, LANE, 0)],
                                [par_d('w0'), par_d('a0'), par_d('k_k'), par_d('k_a')], [(D, LANE, F32)] * 4,
                                tm=t_128, ncb=hb, name="wkv_prep")

    def to_j(z):
        z = z.reshape(Bl, T, Hr, RWKV_HEAD).transpose(1, 3, 0, 2).reshape(T, RWKV_HEAD, Bl * Hr)
        return jnp.concatenate([z, z], axis=-1)

    def to_i(z):
        return z.reshape(Bl, T, Hr, 2, RWKV_HEAD // 2).transpose(1, 4, 3, 0, 2).reshape(T, RWKV_HEAD // 2, 2 * Bl * Hr)

    def from_i(z):
        return z.reshape(T, RWKV_HEAD // 2, 2, Bl, Hr).transpose(3, 0, 4, 2, 1).reshape(R, D)

    def from_j(z):
        return z.reshape(T, RWKV_HEAD, Bl, Hr).transpose(2, 0, 3, 1).reshape(R, D)

    r_s, v_s = ps[:, :D], ps[:, 2 * D:]
    jw, jkn, jb, jk, jr, iv = to_j(decay), to_j(kn), to_j(bb), to_j(k2), to_j(r_s), to_i(v_s)
    y_i, sp, sa_i = wkv_fwd(jw, jkn, jb, jk, jr, iv)
    y = from_i(y_i)
    post_rows = [(y, LANE, 0), (ps, LANE, 0), (k2, LANE, 0), (ps, LANE, 2 * hb), (g, LANE, 0)]
    post_pars = [par_d('gn_w'), par_d('gn_b'), par_d('r_k')]
    ya = rowwise(_post, post_rows, post_pars, [(D, LANE, F32)], tm=t_128, ncb=hb, name="wkv_post")[0]

    nt512 = T // t_512
    mla_rows = [(proj, Q_LORA, O_CQ // Q_LORA), (proj, KV_LORA, O_CKV // KV_LORA), (proj, LANE, O_KPE // LANE)]
    tabs = [(cos_t, LANE, 0, nt512, True), (sin_t, LANE, 0, nt512, True)]
    mla_pars = [(w2['q_norm'], Q_LORA, 0), (w2['kv_norm'], KV_LORA, 0)]
    cqn, ckvn, kpr = rowwise(_mla_pre, mla_rows + tabs, mla_pars,
                             [(Q_LORA, Q_LORA, BF16), (KV_LORA, KV_LORA, BF16), (LANE, LANE, BF16)], tm=t_512, name="mla_pre")
    cqn_r = real_rows(cqn)
    qn = mm(cqn_r, wqn, out_dtype=BF16, name="q_nope")
    qp_raw = mm(cqn_r, wqp, name="q_pe")
    ntq = S // tq_128
    qtabs = [(cos_q, LANE, 0, ntq, True), (sin_q, LANE, 0, ntq, True)]
    qp = rowwise(_rope, [(qp_raw, LANE, 0)] + qtabs, [], [(D, LANE, BF16)], tm=tq_128, ncb=Hm, name="q_rope")[0]
    knope = mm(ckvn, wkn, out_dtype=BF16, name="k_nope")
    vv = mm(ckvn, wv, out_dtype=BF16, name="v_proj")

    def pad_keys(z):
        z3 = rows3(z)
        return jnp.concatenate([z3[:, :N_META], jnp.zeros((Bl, Q_BLOCK - N_META, z.shape[-1]), z.dtype), z3[:, N_META:]], axis=1)

    def unpad_keys(z):
        return jnp.concatenate([z[:, :N_META], z[:, Q_BLOCK:]], axis=1).reshape(R, z.shape[-1])

    qn3, qp3 = qn.reshape(Bl, S, D), qp.reshape(Bl, S, D)
    knp, kpp, vp = pad_keys(knope), pad_keys(kpr), pad_keys(vv)
    o3, got_b = attn_fwd(qn3, qp3, knp, kpp, vp, [own_slot(w2[n]) for n in PART_B])
    stk.update(zip(PART_B, got_b))
    full.update({n: unstack(stk[n], SHARD_AXIS[n]) for n in PART_B if n not in FFN_IN})
    wout = full['w_out']
    o_att = pad_meta(o3.reshape(RS, D))
    mix_rows = [(proj, 512, O_GA // 512), (proj, 512, O_GB // 512), (ya, 512, 0), (o_att, 512, 0)]
    mix = rowwise(_gate_mix, mix_rows, [], [(D, 512, BF16)], tm=t_512, ncb=D // 512, name="gate_mix")[0]
    h2 = mm(mix, wout, res=h1, name="w_out")
    h3, sv2 = ffn_fwd(h2, w2['ffn2_norm'], stk['ffn2_w_gate'], stk['ffn2_w_up'], full['ffn2_w_down'], "ffn2")

    def loss_fb(h_, tgt, fw):
        yv, pull = jax.vjp(_rms, h_, fw)
        e = yv - tgt
        dh, dfw = pull(e * (1.0 / D))
        return dh, jnp.full((1, LANE), 0.5 / D * jnp.sum(e * e), F32), dfw

    dh3r, lossp, g_final = rowwise(loss_fb, [(real_rows(h3), D, 0), (a['loss_target'].reshape(RS, D), D, 0)],
                                   [(w2['final_norm'], D, 0)], [(D, D, F32)], [(1, LANE, LANE), (1, D, D)],
                                   tm=tq_full, name="loss")
    dh3 = pad_meta(dh3r)

    def stacked(g, axis):
        if g.ndim == 3:
            return g
        if axis == 0:
            return g.reshape(4, g.shape[0] // 4, g.shape[1])
        return g.reshape(g.shape[0], 4, g.shape[1] // 4).transpose(1, 0, 2)

    cidx = ci.reshape(1).astype(jnp.int32)
    gr = {'final_norm': g_final}
    dh2, gr['ffn2_norm'], gr['ffn2_w_gate'], gr['ffn2_w_up'], gr['ffn2_w_down'] = ffn_bwd(
        dh3, sv2, w2['ffn2_norm'], stk['ffn2_w_gate'], stk['ffn2_w_up'], full['ffn2_w_down'], "ffn2")
    dh2b = rowwise(lambda d: d, [(dh2, 512, 0)], [], [(D, 512, BF16)], tm=t_512, ncb=D // 512, name="dh2_cast")[0]
    gr['w_out'] = mm(mix, dh2b, "tn", name="d_wout")
    dmix = mm(dh2b, wout, "nt", name="d_mix")
    dga, dgb, dya, do = rowwise(vjp_fn(_gate_mix, 4), mix_rows + [(dmix, 512, 0)], [],
                                [(D, 512, BF16), (D, 512, BF16), (D, 512, F32), (D, 512, BF16)], tm=t_512, ncb=D // 512,
                                name="d_gate_mix")
    gs_b = [stacked(gr[n], SHARD_AXIS[n]) for n in PART_B]
    hs_b = [add_half(g_, r_, cidx, "add_half_" + n) for n, g_, r_ in zip(PART_B, gs_b, swap_halves(gs_b, "swap_halves_b"))]
    (dqn, dqp, dknp, dvp, dkpp), ps_b = attn_bwd(qn3, qp3, knp, kpp, vp, real_rows(do).reshape(Bl, S, D), hs_b)
    dqn2 = dqn.reshape(RS, D)
    dqp_raw = rowwise(_rope_t, [(dqp.reshape(RS, D), LANE, 0)] + qtabs, [], [(D, LANE, BF16)], tm=tq_128, ncb=Hm,
                      name="d_q_rope")[0]
    d_wqn = mm(cqn_r, dqn2, "tn", name="d_wqn")
    d_wqp = mm(cqn_r, dqp_raw, "tn", name="d_wqp")
    dcqn = mm(dqn2, wqn, "nt", name="d_cqn1")
    dcqn = pad_meta(mm(dqp_raw, wqp, "nt", res=dcqn, name="d_cqn2"))
    dkn2, dv2, dkp2 = unpad_keys(dknp), unpad_keys(dvp), unpad_keys(dkpp)
    d_wkn = mm(ckvn, dkn2, "tn", name="d_wkn")
    d_wv = mm(ckvn, dv2, "tn", name="d_wv")
    dckvn = mm(dkn2, wkn, "nt", name="d_ckvn1")
    dckvn = mm(dv2, wv, "nt", res=dckvn, name="d_ckvn2")
    dcq, dckv, dkpe, gr['q_norm'], gr['kv_norm'] = rowwise(
        _mla_pre_bwd, mla_rows[:2] + [(dcqn, Q_LORA, 0), (dckvn, KV_LORA, 0), (dkp2, LANE, 0)] + tabs, mla_pars,
        [(Q_LORA, Q_LORA, BF16), (KV_LORA, KV_LORA, BF16), (LANE, LANE, BF16)], [(1, Q_LORA, Q_LORA), (1, KV_LORA, KV_LORA)],
        tm=t_512, name="d_mla_pre")

    def post_bwd(y_, r_, k2_, v_, g_, dya_, gnw, gnb, rk):
        return vjp_fn(_post, 8)(y_, r_, k2_, v_, g_, gnw, gnb, rk, dya_)

    dy, dr_b, dk2_b, dv_b, dg, gr['gn_w'], gr['gn_b'], gr['r_k'] = rowwise(
        post_bwd, post_rows + [(dya, LANE, 0)], post_pars,
        [(D, LANE, F32)] * 4 + [(D, LANE, BF16)], [(1, D, LANE)] * 3, tm=t_128, ncb=hb, name="d_wkv_post")
    jdw, jdkn, jdb, jdk, jdr, idv = wkv_bwd(jw, jkn, jb, jk, jr, iv, to_i(dy), sp, sa_i)
    ddecay, dkn_w, db_w, dk2_w, dr_w, dv_w = from_j(jdw), from_j(jdkn), from_j(jdb), from_j(jdk), from_j(jdr), from_i(idv)

    def prep_bwd(k_, lw_, la_, dd, dkn_, db_, dk2a, dk2b, w0, a0, kkw, kaw):
        return vjp_fn(_prep, 7)(k_, lw_, la_, w0, a0, kkw, kaw, dd, dkn_, db_, dk2a + dk2b)

    dk_s, dlw, dla, gr['w0'], gr['a0'], gr['k_k'], gr['k_a'] = rowwise(
        prep_bwd, [(ps, LANE, hb), (lw, LANE, 0), (la, LANE, 0), (ddecay, LANE, 0), (dkn_w, LANE, 0), (db_w, LANE, 0),
                   (dk2_w, LANE, 0), (dk2_b, LANE, 0)],
        [par_d('w0'), par_d('a0'), par_d('k_k'), par_d('k_a')], [(D, LANE, F32), (D, LANE, BF16), (D, LANE, BF16)],
        [(1, D, LANE)] * 4, tm=t_128, ncb=hb, name="d_wkv_prep")
    d_wup = mm(txw, dlw, "tn", name="d_wup")
    dtxw = mm(dlw, w_up_p, "nt", name="d_txw")
    d_aup = mm(xas, dla, "tn", name="d_aup")
    dxa = mm(dla, a_up_p, "nt", name="d_xa")
    gr['g_up'] = mm(sg, dg, "tn", name="d_gup")
    dsg = mm(dg, g_up, "nt", name="d_sg")

    def lora_bwd(p_, prev_, dsg_, dt_, dxa_, mu_):
        return vjp_fn(_lora_act, 3)(p_, prev_, mu_, dsg_, dt_, dxa_)

    dpb, dprevb, dmu_b = rowwise(lora_bwd, [(proj, 512, O_L // 512), (prev_b, 512, 0), (dsg, G_LORA, 0), (dtxw, LANE, 0),
                                            (dxa, LANE, 0)], [(mu_b, 512, 0)], [(512, 512, F32)] * 2, [(1, 512, 512)],
                                 tm=t_512, name="d_lora_act")

    def shift_bwd2(p_, prev_, c1, c2, mu_):
        return vjp_fn(_shift, 3)(p_, prev_, mu_, c1 + c2)

    def shift_bwd1(p_, prev_, c1, mu_):
        return vjp_fn(_shift, 3)(p_, prev_, mu_, c1)

    def shift_back(sec, cts):
        nb = D // 512
        f = shift_bwd2 if len(cts) == 2 else shift_bwd1
        return rowwise(f, [(proj, 512, sec * nb), (prev_a, 512, sec * nb)] + [(c, 512, 0) for c in cts],
                       [(mu_a, 512, sec * nb)], [(D, 512, F32)] * 2, [(1, D, 512)], tm=t_512, ncb=nb, name=f"d_shift{sec}")

    dp_r, dprev_r, dmu_r = shift_back(0, [dr_w, dr_b])
    dp_k, dprev_k, dmu_k = shift_back(1, [dk_s])
    dp_v, dprev_v, dmu_v = shift_back(2, [dv_w, dv_b])

    def add_cast(p_, q_):
        return p_ + q_

    def dsec(dp_, dprev_, tag):
        C = dp_.shape[1]
        return rowwise(add_cast, [(dp_, 512, 0), (shift_up(dprev_), 512, 0)], [], [(C, 512, BF16)], tm=t_512, ncb=C // 512,
                       name="d_sec_" + tag)[0]

    zpad = jnp.zeros((R, NP - NP0), BF16)
    dproj = jnp.concatenate([dsec(dp_r, dprev_r, "r"), dsec(dp_k, dprev_k, "k"), dsec(dp_v, dprev_v, "v"), dga, dgb,
                             dsec(dpb, dprevb, "l"), dcq, dckv, dkpe, zpad], axis=1)
    d_win_p = mm(u, dproj, "tn", name="d_win")
    du = mm(dproj, win_p, "nt", name="d_u")

    def norm_bwd(h_, dn_, dh_, nw_):
        dh, dnw = vjp_fn(_rms, 2)(h_, nw_, dn_)
        return dh + dh_, dnw

    dh1, gr['mix_norm'] = rowwise(norm_bwd, [(h1, D, 0), (du, D, 0), (dh2, D, 0)], [(w2['mix_norm'], D, 0)],
                                  [(D, D, F32)], [(1, D, D)], tm=t_full, name="d_mix_norm")
    gr['w_in'] = jnp.concatenate([
        d_win_p[:, :3 * D], d_win_p[:, O_L + G_LORA:O_L + G_LORA + W_LORA], d_win_p[:, O_L + G_LORA + LANE:O_L + G_LORA + LANE + A_LORA],
        d_win_p[:, O_L:O_L + G_LORA], d_win_p[:, O_CQ:O_CQ + Q_LORA], d_win_p[:, O_CKV:O_CKV + KV_LORA],
        d_win_p[:, O_KPE:O_KPE + half], d_win_p[:, O_KPE + 2 * half:O_KPE + 3 * half], d_win_p[:, O_GA:O_GA + 2 * D]], axis=1)
    gr['tm_mu'] = jnp.concatenate([dmu_r, dmu_k, dmu_v, dmu_b[:, G_LORA:G_LORA + W_LORA],
                                   dmu_b[:, G_LORA + LANE:G_LORA + LANE + A_LORA], dmu_b[:, :G_LORA]], axis=1)
    gr['w_up'], gr['a_up'] = d_wup[:W_LORA], d_aup[:A_LORA]
    dq3n, dq3p = d_wqn.reshape(Q_LORA, Hm, LANE), d_wqp.reshape(Q_LORA, Hm, LANE)
    gr['w_uq'] = jnp.concatenate([dq3n, dq3p[:, :, :half], dq3p[:, :, 2 * half:3 * half]], axis=2).reshape(Q_LORA, Hm * QK_DIM)
    gr['w_ukv'] = jnp.concatenate([d_wkn.reshape(KV_LORA, Hm, LANE), d_wv.reshape(KV_LORA, Hm, LANE)], axis=2).reshape(
        KV_LORA, Hm * (NOPE_DIM + V_DIM))
    gs_a2 = [stacked(gr[n], SHARD_AXIS[n]) for n in PART_A2]
    hs_a2 = [add_half(g_, r_, cidx, "add_half_" + n) for n, g_, r_ in zip(PART_A2, gs_a2, swap_halves(gs_a2, "swap_halves_a2"))]
    dh0, gr['ffn1_norm'], gr['ffn1_w_gate'], gr['ffn1_w_up'], gr['ffn1_w_down'], ps_a2 = ffn_bwd(
        dh1, sv1, w2['ffn1_norm'], stk['ffn1_w_gate'], stk['ffn1_w_up'], full['ffn1_w_down'], "ffn1", hs_a2)
    dh0_3 = rows3(dh0)
    grad_x = dh0_3[:, N_META:]
    gr['meta_tokens'] = jnp.sum(dh0_3[:, :N_META], axis=0)

    gs_a1 = [stacked(gr[n], SHARD_AXIS[n]) for n in PART_A1]
    hs_a1 = [add_half(g_, r_, cidx, "add_half_" + n) for n, g_, r_ in zip(PART_A1, gs_a1, swap_halves(gs_a1, "swap_halves_a1"))]
    sc = (chip32.reshape(1), cidx)
    ts = [sum_chips(p, h, sc, "sum_chips_" + n)
          for n, p, h in zip(PART_A1 + PART_A2 + PART_B, list(rs_chips(hs_a1)) + ps_a2 + ps_b, hs_a1 + hs_a2 + hs_b)]
    g_shard = {n: z.reshape(w2[n].shape) for n, z in zip(PART_A1 + PART_A2 + PART_B, share_sibling(ts))}
    small = jnp.concatenate([gr[n].reshape(-1) for n in SMALL] + [gr['meta_tokens'].reshape(-1), lossp.reshape(-1)])
    ns = small.shape[0]
    nsp = -(-ns // (8 * LANE)) * 8 * LANE
    small_sum = allreduce8(jnp.pad(small, (0, nsp - ns)).reshape(-1, LANE), "allreduce_small").reshape(-1)
    g_small = dict(zip(SMALL + ['meta_full'], _unpack(small_sum, [w2[n].shape for n in SMALL] + [(N_META, D)])))
    g_shard['meta_tokens'] = lax.dynamic_slice(
        g_small['meta_full'], (jnp.zeros((), jnp.int32), (chip * mcols).astype(jnp.int32)), (N_META, mcols))
    loss = small_sum[ns - LANE]

    grads, deltas, new_m, new_v = [], [], [], []
    for n in WEIGHTS:
        gw = g_shard[n] if n in g_shard else g_small[n]
        d_, m_, v_ = adamw(w2[n], gw, m2[n], v2[n], "adamw_" + n)
        shp = a[n].shape
        grads.append(gw.reshape(shp))
        deltas.append(d_.reshape(shp))
        new_m.append(m_.reshape(shp))
        new_v.append(v_.reshape(shp))
    return (loss, grad_x, *grads, *deltas, *new_m, *new_v)


def kernel(x, meta_tokens, ffn1_norm, ffn1_w_gate, ffn1_w_up, ffn1_w_down, mix_norm, w_in, tm_mu, w0, w_up, a0, a_up, g_up, k_k, k_a, r_k, gn_w, gn_b, q_norm, w_uq, kv_norm, w_ukv, w_out, ffn2_norm, ffn2_w_gate, ffn2_w_up, ffn2_w_down, final_norm, loss_target, m_meta_tokens, m_ffn1_norm, m_ffn1_w_gate, m_ffn1_w_up, m_ffn1_w_down, m_mix_norm, m_w_in, m_tm_mu, m_w0, m_w_up, m_a0, m_a_up, m_g_up, m_k_k, m_k_a, m_r_k, m_gn_w, m_gn_b, m_q_norm, m_w_uq, m_kv_norm, m_w_ukv, m_w_out, m_ffn2_norm, m_ffn2_w_gate, m_ffn2_w_up, m_ffn2_w_down, m_final_norm, v_meta_tokens, v_ffn1_norm, v_ffn1_w_gate, v_ffn1_w_up, v_ffn1_w_down, v_mix_norm, v_w_in, v_tm_mu, v_w0, v_w_up, v_a0, v_a_up, v_g_up, v_k_k, v_k_a, v_r_k, v_gn_w, v_gn_b, v_q_norm, v_w_uq, v_kv_norm, v_w_ukv, v_w_out, v_ffn2_norm, v_ffn2_w_gate, v_ffn2_w_up, v_ffn2_w_down, v_final_norm):
    return _step(dict(locals()))
```

```python
import functools
import math

import jax
import jax.numpy as jnp
import numpy as np
from jax import lax
from jax.experimental import pallas as pl
from jax.experimental.pallas import tpu as pltpu

F32 = jnp.float32
BF16 = jnp.bfloat16
MESH = pl.DeviceIdType.MESH

N_META = 16
NORM_EPS = 1e-6
RWKV_HEAD = 64
GN_EPS = RWKV_HEAD * 1e-5
W_LORA, A_LORA, G_LORA = 96, 96, 256
Q_LORA, KV_LORA = 512, 512
NOPE_DIM, ROPE_DIM, V_DIM = 128, 64, 128
QK_DIM = NOPE_DIM + ROPE_DIM
ROPE_THETA = 10000.0
Q_BLOCK = 128
ADAM_LR, ADAM_B1, ADAM_B2, ADAM_EPS, ADAM_WD, ADAM_STEP = 0.001, 0.9, 0.999, 1e-08, 0.01, 10

LANE = 128
VMEM_LIMIT = 56 * 1024 * 1024


def _pcall(body, **kw):
    return pl.pallas_call(body, **kw)


def _cparams(sem):
    return pltpu.CompilerParams(dimension_semantics=sem, vmem_limit_bytes=VMEM_LIMIT)


MM_LANE_TILE = 1536
MM_ROW_TILE = 1408


def _div_tile(n, cap, unit):
    best = None
    for t in range(unit, min(n, cap) + 1, unit):
        if n % t == 0:
            best = t
    return best if best is not None else n


def _rtile(n, pref=512):
    best = None
    for t in range(16, min(n, pref * 2) + 1, 16):
        if n % t == 0 and (best is None or abs(t - pref) < abs(best - pref)):
            best = t
    return best if best is not None else n


def mm(a, b, mode="nn", out_dtype=F32, res=None, alpha=1.0, b_stack=False, out_stack=False, carry=None, name="mm"):
    kind, carried = carry if carry else (None, ())
    nc = len(carried)
    cs = b.shape[-1] if b_stack else None
    bs = (b.shape[1], 4 * cs) if b_stack else b.shape
    if mode == "nn":
        (M, K), (K2, N) = a.shape, bs
    elif mode == "nt":
        (M, K), (N, K2) = a.shape, bs
    else:
        (K, M), (K2, N) = a.shape, bs
    assert K == K2, (a.shape, b.shape, mode)
    if mode == "tn":
        tm, tk = _div_tile(M, MM_ROW_TILE, LANE), _div_tile(K, 1024, 16)
    else:
        tm = _div_tile(M, MM_ROW_TILE, 16)
        tk = _div_tile(cs if (b_stack and mode == "nt") else K, MM_LANE_TILE, LANE)
    ncol = N // 4 if out_stack else (cs if (b_stack and mode == "nn") else N)
    tn = _div_tile(ncol, MM_LANE_TILE if mode != "nt" else 1024, LANE)
    nk = K // tk
    dims = {"nn": (((1,), (0,)), ((), ())), "nt": (((1,), (1,)), ((), ())), "tn": (((0,), (0,)), ((), ()))}[mode]
    direct = out_dtype == F32

    def body(*refs):
        a_ref, b_ref = refs[:2]
        r_ref = refs[2] if res is not None else None
        nin = 3 if res is not None else 2
        o_ref = refs[nin + nc]
        acc = o_ref if direct else refs[nin + 2 * nc + 1]
        k = pl.program_id(2)
        if nc:
            steps = _ag_steps if kind == "ag" else _rs_steps
            start, finish = steps(refs[nin:nin + nc], refs[nin + nc + 1:nin + 2 * nc + 1],
                                  *refs[nin + 2 * nc + (1 if direct else 2):])
            i, j = pl.program_id(0), pl.program_id(1)
            pl.when((i == 0) & (j == 0) & (k == 0))(start)

        @pl.when(k == 0)
        def _():
            acc[...] = jnp.zeros_like(acc) if res is None else r_ref[...].astype(F32)

        p = lax.dot_general(a_ref[...].astype(BF16), b_ref[...].astype(BF16), dims, preferred_element_type=F32)
        acc[...] += p if alpha == 1.0 else alpha * p

        if not direct:
            @pl.when(k == nk - 1)
            def _():
                o_ref[...] = acc[...].astype(o_ref.dtype)

        if nc:
            pl.when((i == M // tm - 1) & (j == N // tn - 1) & (k == nk - 1))(finish)

    if mode == "tn":
        a_spec = pl.BlockSpec((tk, tm), lambda i, j, k: (k, i))
        b_spec = pl.BlockSpec((tk, tn), lambda i, j, k: (k, j))
    else:
        a_spec = pl.BlockSpec((tm, tk), lambda i, j, k: (i, k))
        if mode == "nn":
            if b_stack:
                nps = cs // tn
                b_spec = pl.BlockSpec((None, tk, tn), lambda i, j, k: (j // nps, k, j % nps))
            else:
                b_spec = pl.BlockSpec((tk, tn), lambda i, j, k: (k, j))
        elif b_stack:
            kps = cs // tk
            b_spec = pl.BlockSpec((None, tn, tk), lambda i, j, k: (k // kps, j, k % kps))
        else:
            b_spec = pl.BlockSpec((tn, tk), lambda i, j, k: (j, k))
    r_spec = pl.BlockSpec((tm, tn), lambda i, j, k: (i, j))
    if out_stack:
        ops = (N // 4) // tn
        o_spec = pl.BlockSpec((None, tm, tn), lambda i, j, k: (j // ops, i, j % ops))
        o_shape = jax.ShapeDtypeStruct((4, M, N // 4), out_dtype)
    else:
        o_spec, o_shape = r_spec, jax.ShapeDtypeStruct((M, N), out_dtype)
    in_specs = [a_spec, b_spec] + ([r_spec] if res is not None else [])
    args = (a, b) + ((res,) if res is not None else ())
    if not nc:
        return _pcall(
            body, name=name, grid=(M // tm, N // tn, nk), in_specs=in_specs, out_specs=o_spec, out_shape=o_shape,
            scratch_shapes=[] if direct else [pltpu.VMEM((tm, tn), F32)],
            compiler_params=_cparams(("parallel", "parallel", "arbitrary")),
        )(*args)
    out = _pcall(
        body, name=name, grid=(M // tm, N // tn, nk), in_specs=in_specs + [_HBM] * nc, out_specs=[o_spec] + [_HBM] * nc,
        out_shape=[o_shape] + [jax.ShapeDtypeStruct(z.shape, z.dtype) for z in carried],
        input_output_aliases={len(args) + q: 1 + q for q in range(nc)} if kind == "ag" else {},
        scratch_shapes=([] if direct else [pltpu.VMEM((tm, tn), F32)]) + (_ag_sems(nc) if kind == "ag" else _rs_sems(nc)),
        compiler_params=_cparams(("arbitrary",) * 3),
    )(*args, *carried)
    return out[0], list(out[1:])


MM_EPI_ROW_TILE = 704


def mm_epi(a, bs, mode, epi, extras, out_dtypes, b_stack=False, carry=None, name="mm_epi"):
    kind, carried = carry if carry else (None, ())
    nc = len(carried)
    b = bs[0]
    cs = b.shape[-1] if b_stack else None
    bshape = (b.shape[1], 4 * cs) if b_stack else b.shape
    (M, K) = a.shape
    (K2, N) = bshape if mode == "nn" else bshape[::-1]
    assert K == K2 and mode in ("nn", "nt"), (a.shape, b.shape, mode)
    tm = _div_tile(M, MM_EPI_ROW_TILE, 16)
    tk = _div_tile(cs if (b_stack and mode == "nt") else K, MM_LANE_TILE, LANE)
    tn = _div_tile(cs if (b_stack and mode == "nn") else N, MM_LANE_TILE, LANE)
    nk, nb, ne, no = K // tk, len(bs), len(extras), len(out_dtypes)
    dims = (((1,), (0,)), ((), ())) if mode == "nn" else (((1,), (1,)), ((), ()))

    def body(*refs):
        a_ref, b_refs, e_refs = refs[0], refs[1:1 + nb], refs[1 + nb:1 + nb + ne]
        base = 1 + nb + ne + nc
        o_refs, accs = refs[base:base + no], refs[base + no + nc:base + no + nc + nb]
        k = pl.program_id(2)
        if nc:
            steps = _ag_steps if kind == "ag" else _rs_steps
            start, finish = steps(refs[base - nc:base], refs[base + no:base + no + nc], *refs[base + no + nc + nb:])
            i, j = pl.program_id(0), pl.program_id(1)
            pl.when((i == 0) & (j == 0) & (k == 0))(start)

        @pl.when(k == 0)
        def _():
            for acc in accs:
                acc[...] = jnp.zeros_like(acc)

        av = a_ref[...].astype(BF16)
        for b_ref, acc in zip(b_refs, accs):
            acc[...] += lax.dot_general(av, b_ref[...].astype(BF16), dims, preferred_element_type=F32)

        @pl.when(k == nk - 1)
        def _():
            res = epi(*[acc[...] for acc in accs], *[e[...] for e in e_refs])
            for o_ref, v in zip(o_refs, res):
                o_ref[...] = v.astype(o_ref.dtype)

        if nc:
            pl.when((i == M // tm - 1) & (j == N // tn - 1) & (k == nk - 1))(finish)

    a_spec = pl.BlockSpec((tm, tk), lambda i, j, k: (i, k))
    if mode == "nn":
        if b_stack:
            nps = cs // tn
            b_spec = pl.BlockSpec((None, tk, tn), lambda i, j, k: (j // nps, k, j % nps))
        else:
            b_spec = pl.BlockSpec((tk, tn), lambda i, j, k: (k, j))
    elif b_stack:
        kps = cs // tk
        b_spec = pl.BlockSpec((None, tn, tk), lambda i, j, k: (k // kps, j, k % kps))
    else:
        b_spec = pl.BlockSpec((tn, tk), lambda i, j, k: (j, k))
    o_spec = pl.BlockSpec((tm, tn), lambda i, j, k: (i, j))
    sems = (_ag_sems(nc) if kind == "ag" else _rs_sems(nc)) if nc else []
    res = _pcall(
        body, name=name, grid=(M // tm, N // tn, nk), in_specs=[a_spec] + [b_spec] * nb + [o_spec] * ne + [_HBM] * nc,
        out_specs=[o_spec] * no + [_HBM] * nc,
        out_shape=[jax.ShapeDtypeStruct((M, N), dt) for dt in out_dtypes]
        + [jax.ShapeDtypeStruct(z.shape, z.dtype) for z in carried],
        input_output_aliases={1 + nb + ne + q: no + q for q in range(nc)} if kind == "ag" else {},
        scratch_shapes=[pltpu.VMEM((tm, tn), F32)] * nb + sems,
        compiler_params=_cparams(("arbitrary",) * 3 if nc else ("parallel", "parallel", "arbitrary")),
    )(a, *bs, *extras, *carried)
    return (list(res[:no]), list(res[no:])) if nc else res


def rowwise(fn, row_ins, par_ins, outs, accs=(), *, tm, ncb=1, name="rowwise"):
    R = row_ins[0][0].shape[0]
    assert R % tm == 0, (R, tm)
    nrb = R // tm
    in_specs, args = [], []
    for spec in row_ins:
        arr, w, base = spec[:3]
        mod = spec[3] if len(spec) > 3 else None
        cstep = 0 if (len(spec) > 4 and spec[4]) else 1
        if mod is None:
            in_specs.append(pl.BlockSpec((tm, w), lambda j, i, base=base, cstep=cstep: (i, base + cstep * j)))
        else:
            in_specs.append(pl.BlockSpec((tm, w), lambda j, i, base=base, mod=mod, cstep=cstep: (i % mod, base + cstep * j)))
        args.append(arr)
    for arr, w, base in par_ins:
        in_specs.append(pl.BlockSpec((arr.shape[0], w), lambda j, i, base=base: (0, base + j)))
        args.append(arr)
    out_specs, out_shape = [], []
    for cols, w, dt in outs:
        out_specs.append(pl.BlockSpec((tm, w), lambda j, i: (i, j)))
        out_shape.append(jax.ShapeDtypeStruct((R, cols), dt))
    for p, cols, w in accs:
        out_specs.append(pl.BlockSpec((p, w), lambda j, i: (0, j)))
        out_shape.append(jax.ShapeDtypeStruct((p, cols), F32))
    nin, nout, nacc = len(args), len(outs), len(accs)

    def body(*refs):
        vals = [r[...] for r in refs[:nin]]
        res = fn(*vals)
        if not isinstance(res, (tuple, list)):
            res = (res,)
        assert len(res) == nout + nacc, (len(res), nout, nacc)
        for o_ref, v in zip(refs[nin:nin + nout], res[:nout]):
            o_ref[...] = v.astype(o_ref.dtype)
        if nacc:
            i = pl.program_id(1)

            @pl.when(i == 0)
            def _():
                for a_ref in refs[nin + nout:]:
                    a_ref[...] = jnp.zeros_like(a_ref)

            for a_ref, v in zip(refs[nin + nout:], res[nout:]):
                a_ref[...] += v.astype(F32)

    r = _pcall(
        body, name=name, grid=(ncb, nrb), in_specs=in_specs, out_specs=out_specs, out_shape=out_shape,
        compiler_params=_cparams(("parallel", "arbitrary")),
    )(*args)
    return r


def vjp_fn(fwd, nprim):
    def f(*vals):
        prim, cts = vals[:nprim], vals[nprim:]
        out, pull = jax.vjp(fwd, *[p.astype(F32) for p in prim])
        if not isinstance(out, (tuple, list)):
            cts = cts[0].astype(F32)
        else:
            cts = tuple(c.astype(F32) for c in cts)
        return pull(cts)
    return f


WKV_TB = 16


def wkv_fwd(w, kn, b, k, r, v, bufs=()):
    T, N, L = w.shape
    NI = v.shape[1]
    tb = WKV_TB
    n = len(bufs)
    assert T % tb == 0 and L == v.shape[2]

    def body(*refs):
        w_ref, kn_ref, b_ref, k_ref, r_ref, v_ref = refs[:6]
        y_ref, sp_ref, sa_ref = refs[6 + n:9 + n]
        s_ref = refs[9 + 2 * n]
        if n:
            start, finish = _ag_steps(refs[6:6 + n], refs[9 + n:9 + 2 * n], *refs[10 + 2 * n:])
            pl.when(pl.program_id(0) == 0)(start)

        @pl.when(pl.program_id(0) == 0)
        def _():
            s_ref[...] = jnp.zeros_like(s_ref)

        def step(s, carry):
            W, KN, B, Kk, Rr = w_ref[s], kn_ref[s], b_ref[s], k_ref[s], r_ref[s]
            for i in range(NI):
                S = s_ref[i]
                sp_ref[s, i] = S
                sa = jnp.sum(S * KN, axis=0, keepdims=True)
                sa_ref[s, pl.ds(i, 1), :] = sa
                vi = v_ref[s, pl.ds(i, 1), :]
                Sn = S * W + sa * B + vi * Kk
                s_ref[i] = Sn
                y_ref[s, pl.ds(i, 1), :] = jnp.sum(Sn * Rr, axis=0, keepdims=True)
            return carry

        lax.fori_loop(0, tb, step, 0)
        if n:
            pl.when(pl.program_id(0) == T // tb - 1)(finish)

    jspec = pl.BlockSpec((tb, N, L), lambda t: (t, 0, 0))
    ispec = pl.BlockSpec((tb, NI, L), lambda t: (t, 0, 0))
    ishape = jax.ShapeDtypeStruct((T, NI, L), F32)
    res = _pcall(
        body, name="wkv_fwd", grid=(T // tb,), in_specs=[jspec] * 5 + [ispec] + [_HBM] * n,
        out_specs=[ispec, pl.BlockSpec((tb, NI, N, L), lambda t: (t, 0, 0, 0)), ispec] + [_HBM] * n,
        out_shape=[ishape, jax.ShapeDtypeStruct((T, NI, N, L), F32), ishape]
        + [jax.ShapeDtypeStruct(z.shape, z.dtype) for z in bufs],
        input_output_aliases={6 + i: 3 + i for i in range(n)},
        scratch_shapes=[pltpu.VMEM((NI, N, L), F32)] + (_ag_sems(n) if n else []),
        compiler_params=_cparams(("arbitrary",)),
    )(w, kn, b, k, r, v, *bufs)
    return res[0], res[1], res[2], list(res[3:])


def wkv_bwd(w, kn, b, k, r, v, dy, sp, sa):
    T, N, L = w.shape
    NI, LH = v.shape[1], L // 2
    tb = WKV_TB
    nt = T // tb

    def body(w_ref, kn_ref, b_ref, k_ref, r_ref, v_ref, dy_ref, sp_ref, sa_ref,
             dw_ref, dkn_ref, db_ref, dk_ref, dr_ref, dv_ref, ds_ref):
        @pl.when(pl.program_id(0) == 0)
        def _():
            ds_ref[...] = jnp.zeros_like(ds_ref)

        def step(q, carry):
            s = tb - 1 - q
            W, KN, B, Kk, Rr = w_ref[s], kn_ref[s], b_ref[s], k_ref[s], r_ref[s]
            dW = jnp.zeros((N, L), F32)
            dKN, dB, dK, T1 = dW, dW, dW, dW
            al = jnp.zeros((1, L), F32)
            be = al
            for i in range(NI):
                Sp = sp_ref[s, i]
                vi = v_ref[s, pl.ds(i, 1), :]
                dyi = dy_ref[s, pl.ds(i, 1), :]
                sai = sa_ref[s, pl.ds(i, 1), :]
                dS = ds_ref[i] + dyi * Rr
                T1 = T1 + Sp * dyi
                al = al + sai * dyi
                be = be + vi * dyi
                dv_ref[s, pl.ds(i, 1), :] = jnp.sum(dS * Kk, axis=0, keepdims=True)
                dK = dK + dS * vi
                dsa = jnp.sum(dS * B, axis=0, keepdims=True)
                dB = dB + dS * sai
                dW = dW + dS * Sp
                dKN = dKN + Sp * dsa
                ds_ref[i] = dS * W + dsa * KN
            dR = W * T1 + B * al + Kk * be
            for ref, val in ((dw_ref, dW), (dkn_ref, dKN), (db_ref, dB), (dk_ref, dK), (dr_ref, dR)):
                ref[s] = (val + pltpu.roll(val, LH, 1))[:, :LH]
            return carry

        lax.fori_loop(0, tb, step, 0)

    jspec = pl.BlockSpec((tb, N, L), lambda t: (nt - 1 - t, 0, 0))
    gspec = pl.BlockSpec((tb, N, LH), lambda t: (nt - 1 - t, 0, 0))
    ispec = pl.BlockSpec((tb, NI, L), lambda t: (nt - 1 - t, 0, 0))
    gshape = jax.ShapeDtypeStruct((T, N, LH), F32)
    return _pcall(
        body, name="wkv_bwd", grid=(nt,),
        in_specs=[jspec] * 5 + [ispec, ispec, pl.BlockSpec((tb, NI, N, L), lambda t: (nt - 1 - t, 0, 0, 0)), ispec],
        out_specs=[gspec] * 5 + [ispec], out_shape=[gshape] * 5 + [jax.ShapeDtypeStruct((T, NI, L), F32)],
        scratch_shapes=[pltpu.VMEM((NI, N, L), F32)],
        compiler_params=_cparams(("arbitrary",)),
    )(w, kn, b, k, r, v, dy, sp, sa)


_NT = (((1,), (1,)), ((), ()))
_TN = (((0,), (0,)), ((), ()))
ATT_SCALE = QK_DIM ** -0.5


def _att_bias():
    col = lax.broadcasted_iota(jnp.int32, (Q_BLOCK, Q_BLOCK), 1)
    row = lax.broadcasted_iota(jnp.int32, (Q_BLOCK, Q_BLOCK), 0)
    return jnp.where(col < N_META, 0.0, -1e30).astype(F32), jnp.where(col <= row, 0.0, -1e30).astype(F32)


def _att_scores(q1, q2, kn_ref, kp_ref, s_ref, bias, L):
    s = lax.dot_general(q1, kn_ref[0, :L, :], _NT, preferred_element_type=F32)
    s = s + lax.dot_general(q2, kp_ref[0, :L, :], _NT, preferred_element_type=F32)
    s_ref[:, :L] = s * ATT_SCALE
    s_ref[:, :Q_BLOCK] += bias[0]
    s_ref[:, L - Q_BLOCK:L] += bias[1]


def _att_probs(s_ref, L):
    s = s_ref[:, :L]
    m = jnp.max(s, axis=-1, keepdims=True)
    p = jnp.exp(s - m)
    return p / jnp.sum(p, axis=-1, keepdims=True)


def _att_blocks(nq, qn_ref, qp_ref, kn_ref, kp_ref, s_ref):
    bias = _att_bias()
    rows = lambda i: pl.ds(Q_BLOCK * i, Q_BLOCK)
    L = lambda i: Q_BLOCK * (i + 2)
    score = lambda i: _att_scores(qn_ref[0, rows(i), :], qp_ref[0, rows(i), :], kn_ref, kp_ref, s_ref.at[i % 2], bias, L(i))
    score(0)
    for i in range(nq):
        if i + 1 < nq:
            score(i + 1)
        yield i, rows(i), L(i), _att_probs(s_ref.at[i % 2], L(i))


def _edge_steps(nb, nh):
    b, h = pl.program_id(0), pl.program_id(1)
    return (b == 0) & (h == 0), (b == nb - 1) & (h == nh - 1)


def attn_fwd(qn, qp, kn, kp, v, bufs=()):
    B, S, HD = qn.shape
    SP = kn.shape[1]
    H = HD // LANE
    nq = S // Q_BLOCK
    n = len(bufs)

    def body(*refs):
        qn_ref, qp_ref, kn_ref, kp_ref, v_ref = refs[:5]
        o_ref = refs[5 + n]
        s_ref = refs[6 + 2 * n]
        if n:
            start, finish = _ag_steps(refs[5:5 + n], refs[6 + n:6 + 2 * n], *refs[7 + 2 * n:])
            is_first, is_last = _edge_steps(B, H)
            pl.when(is_first)(start)
        for i, rows, L, p in _att_blocks(nq, qn_ref, qp_ref, kn_ref, kp_ref, s_ref):
            o_ref[0, rows, :] = jnp.dot(p.astype(BF16), v_ref[0, :L, :], preferred_element_type=F32)
        if n:
            pl.when(is_last)(finish)

    qspec = pl.BlockSpec((1, S, LANE), lambda b, h: (b, 0, h))
    kspec = pl.BlockSpec((1, SP, LANE), lambda b, h: (b, 0, h))
    pspec = pl.BlockSpec((1, SP, LANE), lambda b, h: (b, 0, 0))
    res = _pcall(
        body, name="attn_fwd", grid=(B, H), in_specs=[qspec, qspec, kspec, pspec, kspec] + [_HBM] * n,
        out_specs=[qspec] + [_HBM] * n,
        out_shape=[jax.ShapeDtypeStruct((B, S, HD), F32)] + [jax.ShapeDtypeStruct(w.shape, w.dtype) for w in bufs],
        input_output_aliases={5 + i: 1 + i for i in range(n)},
        scratch_shapes=[pltpu.VMEM((2, Q_BLOCK, SP), F32)] + (_ag_sems(n) if n else []),
        compiler_params=_cparams(("arbitrary", "arbitrary")),
    )(qn, qp, kn, kp, v, *bufs)
    return res[0], list(res[1:])


def attn_bwd(qn, qp, kn, kp, v, do, hs=()):
    B, S, HD = qn.shape
    SP = kn.shape[1]
    H = HD // LANE
    nq = S // Q_BLOCK
    n = len(hs)

    def body(*refs):
        qn_ref, qp_ref, kn_ref, kp_ref, v_ref, do_ref = refs[:6]
        dqn_ref, dqp_ref, dkn_ref, dv_ref, dkp_ref = refs[6 + n:11 + n]
        dkn_acc, dv_acc, s_ref = refs[11 + 2 * n:14 + 2 * n]
        if n:
            start, finish = _rs_steps(refs[6:6 + n], refs[11 + n:11 + 2 * n], *refs[14 + 2 * n:])
            is_first, is_last = _edge_steps(B, H)
            pl.when(is_first)(start)

        @pl.when(pl.program_id(1) == 0)
        def _():
            dkp_ref[...] = jnp.zeros_like(dkp_ref)

        dkn_acc[...] = jnp.zeros_like(dkn_acc)
        dv_acc[...] = jnp.zeros_like(dv_acc)
        for i, rows, L, p in _att_blocks(nq, qn_ref, qp_ref, kn_ref, kp_ref, s_ref):
            q1, q2, do_i = qn_ref[0, rows, :], qp_ref[0, rows, :], do_ref[0, rows, :]
            dp = lax.dot_general(do_i, v_ref[0, :L, :], _NT, preferred_element_type=F32)
            ds = (p * (dp - jnp.sum(p * dp, axis=-1, keepdims=True)) * ATT_SCALE).astype(BF16)
            dqn_ref[0, rows, :] = jnp.dot(ds, kn_ref[0, :L, :], preferred_element_type=F32).astype(dqn_ref.dtype)
            dqp_ref[0, rows, :] = jnp.dot(ds, kp_ref[0, :L, :], preferred_element_type=F32)
            dkn_acc[:L, :] += lax.dot_general(ds, q1, _TN, preferred_element_type=F32)
            dkp_ref[0, :L, :] += lax.dot_general(ds, q2, _TN, preferred_element_type=F32)
            dv_acc[:L, :] += lax.dot_general(p.astype(BF16), do_i, _TN, preferred_element_type=F32)
        dkn_ref[0] = dkn_acc[...].astype(dkn_ref.dtype)
        dv_ref[0] = dv_acc[...].astype(dv_ref.dtype)
        if n:
            pl.when(is_last)(finish)

    qspec = pl.BlockSpec((1, S, LANE), lambda b, h: (b, 0, h))
    kspec = pl.BlockSpec((1, SP, LANE), lambda b, h: (b, 0, h))
    pspec = pl.BlockSpec((1, SP, LANE), lambda b, h: (b, 0, 0))
    res = _pcall(
        body, name="attn_bwd", grid=(B, H), in_specs=[qspec, qspec, kspec, pspec, kspec, qspec] + [_HBM] * n,
        out_specs=[qspec, qspec, kspec, kspec, pspec] + [_HBM] * n,
        out_shape=[jax.ShapeDtypeStruct((B, S, HD), BF16), jax.ShapeDtypeStruct((B, S, HD), F32),
                   jax.ShapeDtypeStruct((B, SP, HD), BF16), jax.ShapeDtypeStruct((B, SP, HD), BF16),
                   jax.ShapeDtypeStruct((B, SP, LANE), F32)] + [jax.ShapeDtypeStruct(h.shape, h.dtype) for h in hs],
        scratch_shapes=[pltpu.VMEM((SP, LANE), F32), pltpu.VMEM((SP, LANE), F32), pltpu.VMEM((2, Q_BLOCK, SP), F32)]
        + (_rs_sems(n) if n else []),
        compiler_params=_cparams(("arbitrary", "arbitrary")),
    )(qn, qp, kn, kp, v, do, *hs)
    return res[:5], list(res[5:])


_HBM = pl.BlockSpec(memory_space=pltpu.HBM)


def _place():
    x, y, c = lax.axis_index("x"), lax.axis_index("y"), lax.axis_index("c")
    chips = [(1 - x, y), (x, 1 - y), (1 - x, 1 - y)]
    return x, y, c, chips


def _rcopy(src, dst, ssem, rsem, dev):
    return pltpu.make_async_remote_copy(src_ref=src, dst_ref=dst, send_sem=ssem, recv_sem=rsem,
                                        device_id=dev, device_id_type=MESH)


def _half_rows(c, r):
    return pl.ds(pl.multiple_of(c * (r // 2), 16), r // 2)


def _ag_steps(w, out, ssem, rsem):
    n = len(w)
    x, y, c, chips = _place()
    s = 2 * x + y
    rows = [_half_rows(c, w[i].shape[1]) for i in range(n)]
    orows = [_half_rows(1 - c, w[i].shape[1]) for i in range(n)]
    first = [_rcopy(w[i].at[s, rows[i]], out[i].at[s, rows[i]], ssem.at[6 * i + j], rsem.at[6 * i + j], (px, py, c))
             for i in range(n) for j, (px, py) in enumerate(chips)]

    def start():
        for cp in first:
            cp.start()

    def finish():
        passed = []
        for j, (px, py) in enumerate(chips):
            sp = 2 * px + py
            for i in range(n):
                here = out[i].at[sp, rows[i]]
                _rcopy(here, here, ssem.at[6 * i + j], rsem.at[6 * i + j], (px, py, c)).wait_recv()
                fw = _rcopy(here, here, ssem.at[6 * i + 3 + j], rsem.at[6 * i + 3 + j], (x, y, 1 - c))
                fw.start()
                passed.append(fw)
        for j, (px, py) in enumerate(chips):
            sp = 2 * px + py
            for i in range(n):
                there = out[i].at[sp, orows[i]]
                _rcopy(there, there, ssem.at[6 * i + 3 + j], rsem.at[6 * i + 3 + j], (x, y, 1 - c)).wait_recv()
        for cp in first + passed:
            cp.wait_send()

    return start, finish


def _ag_sems(n):
    return [pltpu.SemaphoreType.DMA((6 * n,)), pltpu.SemaphoreType.DMA((6 * n,))]


def ag_weights(bufs):
    n = len(bufs)

    def body(*refs):
        start, finish = _ag_steps(refs[:n], refs[n:2 * n], *refs[2 * n:])
        start()
        finish()

    return _pcall(
        body, name="ag_weights", in_specs=[_HBM] * n, out_specs=[_HBM] * n,
        out_shape=[jax.ShapeDtypeStruct(w.shape, w.dtype) for w in bufs],
        input_output_aliases={i: i for i in range(n)}, scratch_shapes=_ag_sems(n),
    )(*bufs)


def swap_halves(gs, name):
    n = len(gs)

    def body(*refs):
        g, out = refs[:n], refs[n:2 * n]
        ssem, rsem = refs[2 * n:]
        x, y, c, _ = _place()
        cps = [_rcopy(g[i].at[:, _half_rows(1 - c, g[i].shape[1])], out[i], ssem.at[i], rsem.at[i], (x, y, 1 - c))
               for i in range(n)]
        for cp in cps:
            cp.start()
        for cp in cps:
            cp.wait()

    return _pcall(
        body, name=name, in_specs=[_HBM] * n, out_specs=[_HBM] * n,
        out_shape=[jax.ShapeDtypeStruct((4, g.shape[1] // 2, g.shape[2]), g.dtype) for g in gs],
        scratch_shapes=[pltpu.SemaphoreType.DMA((n,)), pltpu.SemaphoreType.DMA((n,))],
    )(*gs)


def _rs_steps(h, out, ssem, rsem):
    n = len(h)
    x, y, c, chips = _place()
    s = 2 * x + y
    cps = [_rcopy(h[i].at[2 * px + py], out[i].at[s], ssem.at[3 * i + j], rsem.at[3 * i + j], (px, py, c))
           for i in range(n) for j, (px, py) in enumerate(chips)]

    def start():
        for cp in cps:
            cp.start()

    def finish():
        for j, (px, py) in enumerate(chips):
            for i in range(n):
                _rcopy(h[i].at[s], out[i].at[2 * px + py], ssem.at[3 * i + j], rsem.at[3 * i + j], (px, py, c)).wait_recv()
        for cp in cps:
            cp.wait_send()

    return start, finish


def _rs_sems(n):
    return [pltpu.SemaphoreType.DMA((3 * n,)), pltpu.SemaphoreType.DMA((3 * n,))]


def share_sibling(ts):
    n = len(ts)

    def body(*refs):
        t, out = refs[:n], refs[n:2 * n]
        ssem, rsem = refs[2 * n:]
        x, y, c, _ = _place()
        cps = [_rcopy(t[i].at[c], out[i].at[c], ssem.at[i], rsem.at[i], (x, y, 1 - c)) for i in range(n)]
        for cp in cps:
            cp.start()
        for i in range(n):
            _rcopy(t[i].at[c], out[i].at[1 - c], ssem.at[i], rsem.at[i], (x, y, 1 - c)).wait_recv()
        for cp in cps:
            cp.wait_send()

    return _pcall(
        body, name="share_sibling", in_specs=[_HBM] * n, out_specs=[_HBM] * n,
        out_shape=[jax.ShapeDtypeStruct(t.shape, t.dtype) for t in ts],
        input_output_aliases={i: i for i in range(n)},
        scratch_shapes=[pltpu.SemaphoreType.DMA((n,)), pltpu.SemaphoreType.DMA((n,))],
    )(*ts)


def allreduce8(v, name):
    P, W = v.shape

    def body(v_ref, out_ref, buf, ssem, rsem):
        x, y, c, _ = _place()
        me = 4 * x + 2 * y + c
        buf[me] = v_ref[...]
        cps = []
        for k in range(1, 8):
            px = 1 - x if k & 4 else x
            py = 1 - y if k & 2 else y
            pc = 1 - c if k & 1 else c
            cp = _rcopy(buf.at[me], buf.at[me], ssem.at[k - 1], rsem.at[k - 1], (px, py, pc))
            cp.start()
            cps.append((cp, 4 * px + 2 * py + pc))
        for k, (cp, peer) in enumerate(cps):
            _rcopy(buf.at[me], buf.at[peer], ssem.at[k], rsem.at[k], (x, y, c)).wait_recv()
        for cp, _ in cps:
            cp.wait_send()
        acc = buf[0]
        for d in range(1, 8):
            acc = acc + buf[d]
        out_ref[...] = acc

    return _pcall(
        body, name=name, in_specs=[pl.BlockSpec(memory_space=pltpu.VMEM)],
        out_specs=pl.BlockSpec(memory_space=pltpu.VMEM), out_shape=jax.ShapeDtypeStruct((P, W), F32),
        scratch_shapes=[pltpu.VMEM((8, P, W), F32), pltpu.SemaphoreType.DMA((7,)), pltpu.SemaphoreType.DMA((7,))],
    )(v)


def add_half(g, rcv, cidx, name):
    _, hr, W = rcv.shape
    tr = _rows_tile(hr, W)
    nb = hr // tr

    def body(c_ref, g_ref, r_ref, o_ref):
        o_ref[...] = (g_ref[...] + r_ref[...]).astype(o_ref.dtype)

    return _pcall(
        body, name=name,
        grid_spec=pltpu.PrefetchScalarGridSpec(
            num_scalar_prefetch=1, grid=(4, nb),
            in_specs=[pl.BlockSpec((1, tr, W), lambda s, i, c: (s, c[0] * nb + i, 0)),
                      pl.BlockSpec((1, tr, W), lambda s, i, c: (s, i, 0))],
            out_specs=pl.BlockSpec((1, tr, W), lambda s, i, c: (s, i, 0))),
        out_shape=jax.ShapeDtypeStruct((4, hr, W), BF16), compiler_params=_cparams(("parallel", "parallel")),
    )(cidx, g, rcv)


def sum_chips(p, h, sc, name):
    _, hr, W = p.shape
    tr = _rows_tile(hr, W)

    def body(s_ref, c_ref, p_ref, h_ref, o_ref):
        s = s_ref[0]
        own = h_ref[0]
        f = lambda k: jnp.where(s == k, own, p_ref[k]).astype(F32)
        o_ref[0] = ((f(0) + f(1)) + f(2)) + f(3)

    return _pcall(
        body, name=name,
        grid_spec=pltpu.PrefetchScalarGridSpec(
            num_scalar_prefetch=2, grid=(hr // tr,),
            in_specs=[pl.BlockSpec((4, tr, W), lambda i, s, c: (0, i, 0)),
                      pl.BlockSpec((1, tr, W), lambda i, s, c: (s[0], i, 0))],
            out_specs=pl.BlockSpec((1, tr, W), lambda i, s, c: (c[0], i, 0))),
        out_shape=jax.ShapeDtypeStruct((2, hr, W), F32), compiler_params=_cparams(("parallel",)),
    )(sc[0], sc[1], p, h)


HI = lax.Precision.HIGHEST
BLOCK_BYTES = 3 * 512 * 1024


def _rows_tile(R, w, T=None):
    cands = [t for t in range(16, R + 1, 16) if R % t == 0 and (T is None or T % t == 0)]
    ok = [t for t in cands if t * w * 4 <= BLOCK_BYTES]
    return max(ok) if ok else min(cands)


def _gsum_exact(x):
    r = jnp.right_shift(lax.broadcasted_iota(jnp.int32, (LANE, LANE), 0), 6)
    c = jnp.right_shift(lax.broadcasted_iota(jnp.int32, (LANE, LANE), 1), 6)
    g = (r == c).astype(BF16)
    x1 = x.astype(BF16)
    r1 = x - x1.astype(F32)
    x2 = r1.astype(BF16)
    x3 = (r1 - x2.astype(F32)).astype(BF16)
    dot = lambda z: jnp.dot(z, g, preferred_element_type=F32)
    return (dot(x3) + dot(x2)) + dot(x1)


@jax.custom_vjp
def _gsum(x):
    return _gsum_exact(x)


_gsum.defvjp(lambda x: (_gsum_exact(x), None), lambda _, ct: (_gsum_exact(ct),))


def _rms(x, g):
    return x * lax.rsqrt(jnp.mean(x * x, axis=-1, keepdims=True) + NORM_EPS) * g


def _silu_mul(gate, up):
    return jax.nn.silu(gate) * up


def _shift(p, prev, mu):
    return p + mu * (prev - p)


def _lora_act(p, prev, mu):
    s = _shift(p, prev, mu)
    return jax.nn.sigmoid(s[:, :G_LORA]), jnp.tanh(s[:, G_LORA:G_LORA + LANE]), s[:, G_LORA + LANE:]


def _prep(k, lw, la, w0, a0, kk_w, ka_w):
    wpre = -jax.nn.softplus(-(w0 + lw)) - 0.5
    decay = jnp.exp(-jnp.exp(wpre))
    a = jax.nn.sigmoid(a0 + la)
    kk = k * kk_w
    kk = kk * lax.rsqrt(jnp.maximum(_gsum(kk * kk), 1e-24))
    k2 = k * (1.0 + (a - 1.0) * ka_w)
    return decay, -kk, kk * a, k2


def _post(y, r, k2, v, g, gnw, gnb, rk):
    mean = _gsum(y) * (1.0 / RWKV_HEAD)
    d = y - mean
    var = _gsum(d * d) * (1.0 / RWKV_HEAD)
    yn = d * lax.rsqrt(var + GN_EPS) * gnw + gnb
    bonus = _gsum(r * k2 * rk) * v
    return (yn + bonus) * g


def _gate_mix(ga, gb, ya, o):
    return jax.nn.sigmoid(ga) * ya + jax.nn.sigmoid(gb) * o


def _rope(x, cos, sin):
    return x * cos + pltpu.roll(x, LANE // 2, 1) * sin


def _rope_t(dy, cos, sin):
    return dy * cos + pltpu.roll(dy * sin, LANE // 2, 1)


def _mla_pre(cq, ckv, kpe, cos, sin, qw, kvw):
    return _rms(cq, qw), _rms(ckv, kvw), _rope(kpe, cos, sin)


def _mla_pre_bwd(cq, ckv, dcqn, dckvn, dkr, cos, sin, qw, kvw):
    _, pull = jax.vjp(lambda a, b, c, d: (_rms(a, c), _rms(b, d)), cq, ckv, qw, kvw)
    dcq, dckv, dqw, dkvw = pull((dcqn, dckvn))
    return dcq, dckv, _rope_t(dkr, cos, sin), dqw, dkvw


WEIGHTS = ['meta_tokens', 'ffn1_norm', 'ffn1_w_gate', 'ffn1_w_up', 'ffn1_w_down', 'mix_norm', 'w_in', 'tm_mu', 'w0',
           'w_up', 'a0', 'a_up', 'g_up', 'k_k', 'k_a', 'r_k', 'gn_w', 'gn_b', 'q_norm', 'w_uq', 'kv_norm', 'w_ukv',
           'w_out', 'ffn2_norm', 'ffn2_w_gate', 'ffn2_w_up', 'ffn2_w_down', 'final_norm']
SHARD_AXIS = {'meta_tokens': 1, 'ffn1_w_gate': 1, 'ffn1_w_up': 1, 'ffn1_w_down': 0, 'w_in': 1, 'w_up': 1, 'a_up': 1,
              'g_up': 1, 'w_uq': 1, 'w_ukv': 1, 'w_out': 0, 'ffn2_w_gate': 1, 'ffn2_w_up': 1, 'ffn2_w_down': 0}
GATHERED = [n for n in WEIGHTS if n in SHARD_AXIS and n != 'meta_tokens']
FFN_IN = ('ffn1_w_gate', 'ffn1_w_up', 'ffn2_w_gate', 'ffn2_w_up')
PART_B = ['ffn2_w_gate', 'ffn2_w_up', 'ffn2_w_down', 'w_out']
PART_A1 = ['ffn1_w_gate', 'ffn1_w_up', 'ffn1_w_down']
PART_A2 = [n for n in GATHERED if n not in PART_B and n not in PART_A1]
SMALL = [n for n in WEIGHTS if n not in SHARD_AXIS]


def _to2d(a):
    if a.ndim == 1:
        return a.reshape(1, -1)
    if a.ndim == 3:
        return a.reshape(a.shape[0] * a.shape[1], a.shape[2]) if a.shape[0] == 1 and a.shape[1] > 64 else a.reshape(1, -1)
    return a


def _unpack(flat, shapes):
    out, off = [], 0
    for shp in shapes:
        n = shp[0] * shp[1]
        out.append(flat[off:off + n].reshape(shp))
        off += n
    return out


def _adamw(w, g, m, v):
    m = ADAM_B1 * m + (1.0 - ADAM_B1) * g
    v = ADAM_B2 * v + (1.0 - ADAM_B2) * jnp.square(g)
    m_hat = m / (1.0 - ADAM_B1 ** ADAM_STEP)
    v_hat = v / (1.0 - ADAM_B2 ** ADAM_STEP)
    delta = -ADAM_LR * (m_hat / (jnp.sqrt(v_hat) + ADAM_EPS) + ADAM_WD * w)
    return delta, m, v


def adamw(w, g, m, v, name):
    R, C = w.shape
    if R % 16 == 0 and R > 16:
        tm = _rows_tile(R, C)
    else:
        tm = R
    return rowwise(_adamw, [(w, C, 0), (g, C, 0), (m, C, 0), (v, C, 0)], [], [(C, C, F32)] * 3, tm=tm, name=name)


def _step(a):
    x = a['x']
    Bl, S, D = x.shape
    T = S + N_META
    R, RS = Bl * T, Bl * S
    Hr, Hm = D // RWKV_HEAD, D // LANE
    w2 = {n: _to2d(a[n]) for n in WEIGHTS}
    m2 = {n: _to2d(a['m_' + n]) for n in WEIGHTS}
    v2 = {n: _to2d(a['v_' + n]) for n in WEIGHTS}
    xi, yi, ci = lax.axis_index("x"), lax.axis_index("y"), lax.axis_index("c")
    chip = 2 * xi + yi

    i0 = jnp.zeros((), jnp.int32)
    chip32 = chip.astype(jnp.int32)

    def own_slot(w):
        return lax.dynamic_update_slice(lax.empty((4,) + w.shape, BF16), w.astype(BF16)[None], (chip32, i0, i0))

    stk = dict(zip(PART_A1, ag_weights([own_slot(w2[n]) for n in PART_A1])))

    def unstack(z, axis):
        return z.reshape(4 * z.shape[1], z.shape[2]) if axis == 0 else jnp.concatenate([z[s] for s in range(4)], axis=1)

    full = {n: unstack(stk[n], SHARD_AXIS[n]) for n in PART_A1 if n not in FFN_IN}
    mt = w2['meta_tokens']
    mcols = mt.shape[1]
    mt_z = lax.dynamic_update_slice(jnp.zeros((N_META, D), F32), 0.5 * mt, (jnp.zeros((), jnp.int32), (chip * mcols).astype(jnp.int32)))
    meta_full = allreduce8(mt_z.reshape(-1, LANE), "gather_meta").reshape(N_META, D)

    F = 4 * stk['ffn1_w_gate'].shape[2]
    half = ROPE_DIM // 2
    tmu = w2['tm_mu']
    mu_a = tmu[:, :3 * D]
    zm = lambda n: jnp.zeros((1, n), F32)
    mu_b = jnp.concatenate([tmu[:, 3 * D + 2 * W_LORA:], tmu[:, 3 * D:3 * D + W_LORA], zm(LANE - W_LORA),
                            tmu[:, 3 * D + W_LORA:3 * D + 2 * W_LORA], zm(LANE - A_LORA)], axis=1)
    pos = jnp.arange(T, dtype=F32)
    inv_freq = 1.0 / (ROPE_THETA ** (jnp.arange(0, ROPE_DIM, 2, dtype=F32) / ROPE_DIM))
    ang = pos[:, None] * inv_freq[None, :]
    zt = jnp.zeros((T, half), F32)
    cos_t = jnp.concatenate([jnp.cos(ang), zt, jnp.cos(ang), zt], axis=1)
    sin_t = jnp.concatenate([-jnp.sin(ang), zt, jnp.sin(ang), zt], axis=1)
    cos_q, sin_q = cos_t[N_META:], sin_t[N_META:]

    t_full = _rows_tile(R, D)
    t_512 = _rows_tile(R, 512, T)
    t_128 = _rows_tile(R, LANE, T)
    tq_full = _rows_tile(RS, D)
    tq_128 = _rows_tile(RS, LANE, S)

    def rows3(z):
        return z.reshape(Bl, T, z.shape[-1])

    def real_rows(z):
        return rows3(z)[:, N_META:].reshape(RS, z.shape[-1])

    def pad_meta(z):
        z3 = z.reshape(Bl, S, z.shape[-1])
        return jnp.concatenate([jnp.zeros((Bl, N_META, z.shape[-1]), z.dtype), z3], axis=1).reshape(R, z.shape[-1])

    def shift_down(z):
        z3 = rows3(z)
        return jnp.concatenate([jnp.zeros((Bl, 1, z.shape[-1]), z.dtype), z3[:, :-1]], axis=1).reshape(R, z.shape[-1])

    def shift_up(z):
        z3 = rows3(z)
        return jnp.concatenate([z3[:, 1:], jnp.zeros((Bl, 1, z.shape[-1]), z.dtype)], axis=1).reshape(R, z.shape[-1])

    def ffn_fwd(h, nw, wg, wu, wd, tag, ag_bufs=()):
        n = rowwise(_rms, [(h, D, 0)], [(nw, D, 0)], [(D, D, BF16)], tm=t_full, name=tag + "_norm")[0]
        res = mm_epi(n, [wg, wu], "nn", lambda g_, u_: (g_, u_, _silu_mul(g_, u_)), [], [F32, F32, BF16], b_stack=True,
                     carry=("ag", ag_bufs) if ag_bufs else None, name=tag + "_gate_up")
        (gate, up, act), got = res if ag_bufs else (res, [])
        out = mm(act, wd, res=h, alpha=0.5, name=tag + "_down")
        return (out, (h, n, gate, up, act), got) if ag_bufs else (out, (h, n, gate, up, act))

    def exch(g, name):
        return add_half(g, swap_halves([g], "swap_halves_" + name)[0], cidx, "add_half_" + name)

    def ffn_bwd(dh2, saved, nw, wg, wu, wd, tag, rs_hs=(), rs_own=()):
        h, n, gate, up, act = saved
        dz = rowwise(lambda d: 0.5 * d, [(dh2, 512, 0)], [], [(D, 512, BF16)], tm=t_512, ncb=D // 512, name=tag + "_dz")[0]
        d_wd = mm(act, dz, "tn", name=tag + "_dwd")
        res = mm_epi(dz, [wd], "nt", lambda da, g_, u_: vjp_fn(_silu_mul, 2)(g_, u_, da), [gate, up], [BF16, BF16],
                     carry=("rs", rs_hs) if rs_hs else None, name=tag + "_dact")
        (dgate, dup), got = res if rs_hs else (res, [])
        own = {}
        if rs_own:
            h_wd = exch(stacked(d_wd, 0), rs_own[2])
            d_wg, p_wd = mm(n, dgate, "tn", out_stack=True, carry=("rs", [h_wd]), name=tag + "_dwg")
            d_wu = mm(n, dup, "tn", out_stack=True, name=tag + "_dwu")
            h_wg = exch(d_wg, rs_own[0])
            dn, p_wg = mm(dgate, wg, "nt", b_stack=True, carry=("rs", [h_wg]), name=tag + "_dn1")
            h_wu = exch(d_wu, rs_own[1])
            dn, p_wu = mm(dup, wu, "nt", res=dn, b_stack=True, carry=("rs", [h_wu]), name=tag + "_dn2")
            own = {rs_own[0]: (p_wg[0], h_wg), rs_own[1]: (p_wu[0], h_wu), rs_own[2]: (p_wd[0], h_wd)}
        else:
            d_wg = mm(n, dgate, "tn", out_stack=True, name=tag + "_dwg")
            d_wu = mm(n, dup, "tn", out_stack=True, name=tag + "_dwu")
            dn = mm(dgate, wg, "nt", b_stack=True, name=tag + "_dn1")
            dn = mm(dup, wu, "nt", res=dn, b_stack=True, name=tag + "_dn2")

        def f(h_, dn_, dh_, nw_):
            dh, dnw = vjp_fn(_rms, 2)(h_, nw_, dn_)
            return dh + dh_, dnw

        dh, d_nw = rowwise(f, [(h, D, 0), (dn, D, 0), (dh2, D, 0)], [(nw, D, 0)], [(D, D, F32)], [(1, D, D)],
                           tm=t_full, name=tag + "_dnorm")
        return (dh, d_nw, d_wg, d_wu, d_wd, got, own) if (rs_hs or rs_own) else (dh, d_nw, d_wg, d_wu, d_wd)

    h0 = jnp.concatenate([jnp.broadcast_to(meta_full[None], (Bl, N_META, D)), x], axis=1).reshape(R, D)
    h1, sv1, got_a2 = ffn_fwd(h0, w2['ffn1_norm'], stk['ffn1_w_gate'], stk['ffn1_w_up'], full['ffn1_w_down'], "ffn1",
                              [own_slot(w2[n]) for n in PART_A2])
    stk.update(zip(PART_A2, got_a2))
    full.update({n: unstack(stk[n], SHARD_AXIS[n]) for n in PART_A2})
    win = full['w_in']
    o = 3 * D
    c_xw, c_xa, c_xg = win[:, o:o + W_LORA], win[:, o + W_LORA:o + 2 * W_LORA], win[:, o + 2 * W_LORA:o + 2 * W_LORA + G_LORA]
    o += 2 * W_LORA + G_LORA
    c_cq, c_ckv, c_kpe = win[:, o:o + Q_LORA], win[:, o + Q_LORA:o + Q_LORA + KV_LORA], win[:, o + Q_LORA + KV_LORA:o + Q_LORA + KV_LORA + ROPE_DIM]
    o += Q_LORA + KV_LORA + ROPE_DIM
    c_ga, c_gb = win[:, o:o + D], win[:, o + D:o + 2 * D]
    zc = lambda n: jnp.zeros((D, n), BF16)
    half = ROPE_DIM // 2
    NP0 = 5 * D + 512 + Q_LORA + KV_LORA + LANE
    NP = -(-NP0 // 512) * 512
    win_p = jnp.concatenate([win[:, :3 * D], c_ga, c_gb, c_xg, c_xw, zc(LANE - W_LORA), c_xa, zc(LANE - A_LORA), c_cq, c_ckv,
                             c_kpe[:, :half], zc(half), c_kpe[:, half:], zc(half), zc(NP - NP0)], axis=1)
    O_GA, O_GB, O_L, O_CQ, O_CKV, O_KPE = 3 * D, 4 * D, 5 * D, 5 * D + 512, 5 * D + 512 + Q_LORA, 5 * D + 512 + Q_LORA + KV_LORA
    zr = lambda n: jnp.zeros((n, D), BF16)
    w_up_p = jnp.concatenate([full['w_up'], zr(LANE - W_LORA)], axis=0)
    a_up_p = jnp.concatenate([full['a_up'], zr(LANE - A_LORA)], axis=0)
    g_up = full['g_up']
    wuq = full['w_uq'].reshape(Q_LORA, Hm, QK_DIM)
    zq = jnp.zeros((Q_LORA, Hm, half), BF16)
    wqn = wuq[:, :, :NOPE_DIM].reshape(Q_LORA, Hm * LANE)
    wqp = jnp.concatenate([wuq[:, :, NOPE_DIM:NOPE_DIM + half], zq, wuq[:, :, NOPE_DIM + half:], zq], axis=2).reshape(Q_LORA, Hm * LANE)
    wukv = full['w_ukv'].reshape(KV_LORA, Hm, NOPE_DIM + V_DIM)
    wkn = wukv[:, :, :NOPE_DIM].reshape(KV_LORA, Hm * LANE)
    wv = wukv[:, :, NOPE_DIM:].reshape(KV_LORA, Hm * LANE)
    u = rowwise(_rms, [(h1, D, 0)], [(w2['mix_norm'], D, 0)], [(D, D, BF16)], tm=t_full, name="mix_norm")[0]
    proj = mm(u, win_p, name="proj")
    prev_a = shift_down(proj[:, :3 * D])
    prev_b = shift_down(proj[:, O_L:O_L + 512])
    ps = rowwise(_shift, [(proj, 512, 0), (prev_a, 512, 0)], [(mu_a, 512, 0)], [(3 * D, 512, F32)], tm=t_512,
                 ncb=3 * D // 512, name="shift_rkv")[0]
    sg, txw, xas = rowwise(_lora_act, [(proj, 512, O_L // 512), (prev_b, 512, 0)], [(mu_b, 512, 0)],
                           [(G_LORA, G_LORA, BF16), (LANE, LANE, BF16), (LANE, LANE, BF16)], tm=t_512, name="lora_act")
    lw = mm(txw, w_up_p, name="lora_w")
    la = mm(xas, a_up_p, name="lora_a")
    g = mm(sg, g_up, name="lora_g")
    hb = D // LANE
    par_d = lambda n: (w2[n], LANE, 0)
    decay, kn, bb, k2 = rowwise(_prep, [(ps, LANE, hb), (lw, LANE, 0), (la, LANE, 0)],
                                [par_d('w0'), par_d('a0'), par_d('k_k'), par_d('k_a')], [(D, LANE, F32)] * 4,
                                tm=t_128, ncb=hb, name="wkv_prep")

    def to_j(z):
        z = z.reshape(Bl, T, Hr, RWKV_HEAD).transpose(1, 3, 0, 2).reshape(T, RWKV_HEAD, Bl * Hr)
        return jnp.concatenate([z, z], axis=-1)

    def to_i(z):
        return z.reshape(Bl, T, Hr, 2, RWKV_HEAD // 2).transpose(1, 4, 3, 0, 2).reshape(T, RWKV_HEAD // 2, 2 * Bl * Hr)

    def from_i(z):
        return z.reshape(T, RWKV_HEAD // 2, 2, Bl, Hr).transpose(3, 0, 4, 2, 1).reshape(R, D)

    def from_j(z):
        return z.reshape(T, RWKV_HEAD, Bl, Hr).transpose(2, 0, 3, 1).reshape(R, D)

    r_s, v_s = ps[:, :D], ps[:, 2 * D:]
    jw, jkn, jb, jk, jr, iv = to_j(decay), to_j(kn), to_j(bb), to_j(k2), to_j(r_s), to_i(v_s)
    y_i, sp, sa_i, got_b = wkv_fwd(jw, jkn, jb, jk, jr, iv, [own_slot(w2[n]) for n in PART_B])
    stk.update(zip(PART_B, got_b))
    full.update({n: unstack(stk[n], SHARD_AXIS[n]) for n in PART_B if n not in FFN_IN})
    wout = full['w_out']
    y = from_i(y_i)
    post_rows = [(y, LANE, 0), (ps, LANE, 0), (k2, LANE, 0), (ps, LANE, 2 * hb), (g, LANE, 0)]
    post_pars = [par_d('gn_w'), par_d('gn_b'), par_d('r_k')]
    ya = rowwise(_post, post_rows, post_pars, [(D, LANE, F32)], tm=t_128, ncb=hb, name="wkv_post")[0]

    nt512 = T // t_512
    mla_rows = [(proj, Q_LORA, O_CQ // Q_LORA), (proj, KV_LORA, O_CKV // KV_LORA), (proj, LANE, O_KPE // LANE)]
    tabs = [(cos_t, LANE, 0, nt512, True), (sin_t, LANE, 0, nt512, True)]
    mla_pars = [(w2['q_norm'], Q_LORA, 0), (w2['kv_norm'], KV_LORA, 0)]
    cqn, ckvn, kpr = rowwise(_mla_pre, mla_rows + tabs, mla_pars,
                             [(Q_LORA, Q_LORA, BF16), (KV_LORA, KV_LORA, BF16), (LANE, LANE, BF16)], tm=t_512, name="mla_pre")
    cqn_r = real_rows(cqn)
    qn = mm(cqn_r, wqn, out_dtype=BF16, name="q_nope")
    qp_raw = mm(cqn_r, wqp, name="q_pe")
    ntq = S // tq_128
    qtabs = [(cos_q, LANE, 0, ntq, True), (sin_q, LANE, 0, ntq, True)]
    qp = rowwise(_rope, [(qp_raw, LANE, 0)] + qtabs, [], [(D, LANE, BF16)], tm=tq_128, ncb=Hm, name="q_rope")[0]
    knope = mm(ckvn, wkn, out_dtype=BF16, name="k_nope")
    vv = mm(ckvn, wv, out_dtype=BF16, name="v_proj")

    def pad_keys(z):
        z3 = rows3(z)
        return jnp.concatenate([z3[:, :N_META], jnp.zeros((Bl, Q_BLOCK - N_META, z.shape[-1]), z.dtype), z3[:, N_META:]], axis=1)

    def unpad_keys(z):
        return jnp.concatenate([z[:, :N_META], z[:, Q_BLOCK:]], axis=1).reshape(R, z.shape[-1])

    qn3, qp3 = qn.reshape(Bl, S, D), qp.reshape(Bl, S, D)
    knp, kpp, vp = pad_keys(knope), pad_keys(kpr), pad_keys(vv)
    o3, _ = attn_fwd(qn3, qp3, knp, kpp, vp)
    o_att = pad_meta(o3.reshape(RS, D))
    mix_rows = [(proj, 512, O_GA // 512), (proj, 512, O_GB // 512), (ya, 512, 0), (o_att, 512, 0)]
    mix = rowwise(_gate_mix, mix_rows, [], [(D, 512, BF16)], tm=t_512, ncb=D // 512, name="gate_mix")[0]
    h2 = mm(mix, wout, res=h1, name="w_out")
    h3, sv2 = ffn_fwd(h2, w2['ffn2_norm'], stk['ffn2_w_gate'], stk['ffn2_w_up'], full['ffn2_w_down'], "ffn2")

    def loss_fb(h_, tgt, fw):
        yv, pull = jax.vjp(_rms, h_, fw)
        e = yv - tgt
        dh, dfw = pull(e * (1.0 / D))
        return dh, jnp.full((1, LANE), 0.5 / D * jnp.sum(e * e), F32), dfw

    dh3r, lossp, g_final = rowwise(loss_fb, [(real_rows(h3), D, 0), (a['loss_target'].reshape(RS, D), D, 0)],
                                   [(w2['final_norm'], D, 0)], [(D, D, F32)], [(1, LANE, LANE), (1, D, D)],
                                   tm=tq_full, name="loss")
    dh3 = pad_meta(dh3r)

    def stacked(g, axis):
        if g.ndim == 3:
            return g
        if axis == 0:
            return g.reshape(4, g.shape[0] // 4, g.shape[1])
        return g.reshape(g.shape[0], 4, g.shape[1] // 4).transpose(1, 0, 2)

    cidx = ci.reshape(1).astype(jnp.int32)
    gr = {'final_norm': g_final}
    dh2, gr['ffn2_norm'], gr['ffn2_w_gate'], gr['ffn2_w_up'], gr['ffn2_w_down'] = ffn_bwd(
        dh3, sv2, w2['ffn2_norm'], stk['ffn2_w_gate'], stk['ffn2_w_up'], full['ffn2_w_down'], "ffn2")
    dh2b = rowwise(lambda d: d, [(dh2, 512, 0)], [], [(D, 512, BF16)], tm=t_512, ncb=D // 512, name="dh2_cast")[0]
    gr['w_out'] = mm(mix, dh2b, "tn", name="d_wout")
    dmix = mm(dh2b, wout, "nt", name="d_mix")
    dga, dgb, dya, do = rowwise(vjp_fn(_gate_mix, 4), mix_rows + [(dmix, 512, 0)], [],
                                [(D, 512, BF16), (D, 512, BF16), (D, 512, F32), (D, 512, BF16)], tm=t_512, ncb=D // 512,
                                name="d_gate_mix")
    gs_b = [stacked(gr[n], SHARD_AXIS[n]) for n in PART_B]
    hs_b = [add_half(g_, r_, cidx, "add_half_" + n) for n, g_, r_ in zip(PART_B, gs_b, swap_halves(gs_b, "swap_halves_b"))]
    (dqn, dqp, dknp, dvp, dkpp), ps_b = attn_bwd(qn3, qp3, knp, kpp, vp, real_rows(do).reshape(Bl, S, D), hs_b)
    dqn2 = dqn.reshape(RS, D)
    dqp_raw = rowwise(_rope_t, [(dqp.reshape(RS, D), LANE, 0)] + qtabs, [], [(D, LANE, BF16)], tm=tq_128, ncb=Hm,
                      name="d_q_rope")[0]
    d_wqn = mm(cqn_r, dqn2, "tn", name="d_wqn")
    d_wqp = mm(cqn_r, dqp_raw, "tn", name="d_wqp")
    dcqn = mm(dqn2, wqn, "nt", name="d_cqn1")
    dcqn = pad_meta(mm(dqp_raw, wqp, "nt", res=dcqn, name="d_cqn2"))
    dkn2, dv2, dkp2 = unpad_keys(dknp), unpad_keys(dvp), unpad_keys(dkpp)
    d_wkn = mm(ckvn, dkn2, "tn", name="d_wkn")
    d_wv = mm(ckvn, dv2, "tn", name="d_wv")
    dckvn = mm(dkn2, wkn, "nt", name="d_ckvn1")
    dckvn = mm(dv2, wv, "nt", res=dckvn, name="d_ckvn2")
    dcq, dckv, dkpe, gr['q_norm'], gr['kv_norm'] = rowwise(
        _mla_pre_bwd, mla_rows[:2] + [(dcqn, Q_LORA, 0), (dckvn, KV_LORA, 0), (dkp2, LANE, 0)] + tabs, mla_pars,
        [(Q_LORA, Q_LORA, BF16), (KV_LORA, KV_LORA, BF16), (LANE, LANE, BF16)], [(1, Q_LORA, Q_LORA), (1, KV_LORA, KV_LORA)],
        tm=t_512, name="d_mla_pre")

    def post_bwd(y_, r_, k2_, v_, g_, dya_, gnw, gnb, rk):
        return vjp_fn(_post, 8)(y_, r_, k2_, v_, g_, gnw, gnb, rk, dya_)

    dy, dr_b, dk2_b, dv_b, dg, gr['gn_w'], gr['gn_b'], gr['r_k'] = rowwise(
        post_bwd, post_rows + [(dya, LANE, 0)], post_pars,
        [(D, LANE, F32)] * 4 + [(D, LANE, BF16)], [(1, D, LANE)] * 3, tm=t_128, ncb=hb, name="d_wkv_post")
    jdw, jdkn, jdb, jdk, jdr, idv = wkv_bwd(jw, jkn, jb, jk, jr, iv, to_i(dy), sp, sa_i)
    ddecay, dkn_w, db_w, dk2_w, dr_w, dv_w = from_j(jdw), from_j(jdkn), from_j(jdb), from_j(jdk), from_j(jdr), from_i(idv)

    def prep_bwd(k_, lw_, la_, dd, dkn_, db_, dk2a, dk2b, w0, a0, kkw, kaw):
        return vjp_fn(_prep, 7)(k_, lw_, la_, w0, a0, kkw, kaw, dd, dkn_, db_, dk2a + dk2b)

    dk_s, dlw, dla, gr['w0'], gr['a0'], gr['k_k'], gr['k_a'] = rowwise(
        prep_bwd, [(ps, LANE, hb), (lw, LANE, 0), (la, LANE, 0), (ddecay, LANE, 0), (dkn_w, LANE, 0), (db_w, LANE, 0),
                   (dk2_w, LANE, 0), (dk2_b, LANE, 0)],
        [par_d('w0'), par_d('a0'), par_d('k_k'), par_d('k_a')], [(D, LANE, F32), (D, LANE, BF16), (D, LANE, BF16)],
        [(1, D, LANE)] * 4, tm=t_128, ncb=hb, name="d_wkv_prep")
    d_wup = mm(txw, dlw, "tn", name="d_wup")
    dtxw = mm(dlw, w_up_p, "nt", name="d_txw")
    d_aup = mm(xas, dla, "tn", name="d_aup")
    dxa = mm(dla, a_up_p, "nt", name="d_xa")
    gr['g_up'] = mm(sg, dg, "tn", name="d_gup")
    dsg = mm(dg, g_up, "nt", name="d_sg")

    def lora_bwd(p_, prev_, dsg_, dt_, dxa_, mu_):
        return vjp_fn(_lora_act, 3)(p_, prev_, mu_, dsg_, dt_, dxa_)

    dpb, dprevb, dmu_b = rowwise(lora_bwd, [(proj, 512, O_L // 512), (prev_b, 512, 0), (dsg, G_LORA, 0), (dtxw, LANE, 0),
                                            (dxa, LANE, 0)], [(mu_b, 512, 0)], [(512, 512, F32)] * 2, [(1, 512, 512)],
                                 tm=t_512, name="d_lora_act")

    def shift_bwd2(p_, prev_, c1, c2, mu_):
        return vjp_fn(_shift, 3)(p_, prev_, mu_, c1 + c2)

    def shift_bwd1(p_, prev_, c1, mu_):
        return vjp_fn(_shift, 3)(p_, prev_, mu_, c1)

    def shift_back(sec, cts):
        nb = D // 512
        f = shift_bwd2 if len(cts) == 2 else shift_bwd1
        return rowwise(f, [(proj, 512, sec * nb), (prev_a, 512, sec * nb)] + [(c, 512, 0) for c in cts],
                       [(mu_a, 512, sec * nb)], [(D, 512, F32)] * 2, [(1, D, 512)], tm=t_512, ncb=nb, name=f"d_shift{sec}")

    dp_r, dprev_r, dmu_r = shift_back(0, [dr_w, dr_b])
    dp_k, dprev_k, dmu_k = shift_back(1, [dk_s])
    dp_v, dprev_v, dmu_v = shift_back(2, [dv_w, dv_b])

    def add_cast(p_, q_):
        return p_ + q_

    def dsec(dp_, dprev_, tag):
        C = dp_.shape[1]
        return rowwise(add_cast, [(dp_, 512, 0), (shift_up(dprev_), 512, 0)], [], [(C, 512, BF16)], tm=t_512, ncb=C // 512,
                       name="d_sec_" + tag)[0]

    zpad = jnp.zeros((R, NP - NP0), BF16)
    dproj = jnp.concatenate([dsec(dp_r, dprev_r, "r"), dsec(dp_k, dprev_k, "k"), dsec(dp_v, dprev_v, "v"), dga, dgb,
                             dsec(dpb, dprevb, "l"), dcq, dckv, dkpe, zpad], axis=1)
    d_win_p = mm(u, dproj, "tn", name="d_win")
    du = mm(dproj, win_p, "nt", name="d_u")

    def norm_bwd(h_, dn_, dh_, nw_):
        dh, dnw = vjp_fn(_rms, 2)(h_, nw_, dn_)
        return dh + dh_, dnw

    dh1, gr['mix_norm'] = rowwise(norm_bwd, [(h1, D, 0), (du, D, 0), (dh2, D, 0)], [(w2['mix_norm'], D, 0)],
                                  [(D, D, F32)], [(1, D, D)], tm=t_full, name="d_mix_norm")
    gr['w_in'] = jnp.concatenate([
        d_win_p[:, :3 * D], d_win_p[:, O_L + G_LORA:O_L + G_LORA + W_LORA], d_win_p[:, O_L + G_LORA + LANE:O_L + G_LORA + LANE + A_LORA],
        d_win_p[:, O_L:O_L + G_LORA], d_win_p[:, O_CQ:O_CQ + Q_LORA], d_win_p[:, O_CKV:O_CKV + KV_LORA],
        d_win_p[:, O_KPE:O_KPE + half], d_win_p[:, O_KPE + 2 * half:O_KPE + 3 * half], d_win_p[:, O_GA:O_GA + 2 * D]], axis=1)
    gr['tm_mu'] = jnp.concatenate([dmu_r, dmu_k, dmu_v, dmu_b[:, G_LORA:G_LORA + W_LORA],
                                   dmu_b[:, G_LORA + LANE:G_LORA + LANE + A_LORA], dmu_b[:, :G_LORA]], axis=1)
    gr['w_up'], gr['a_up'] = d_wup[:W_LORA], d_aup[:A_LORA]
    dq3n, dq3p = d_wqn.reshape(Q_LORA, Hm, LANE), d_wqp.reshape(Q_LORA, Hm, LANE)
    gr['w_uq'] = jnp.concatenate([dq3n, dq3p[:, :, :half], dq3p[:, :, 2 * half:3 * half]], axis=2).reshape(Q_LORA, Hm * QK_DIM)
    gr['w_ukv'] = jnp.concatenate([d_wkn.reshape(KV_LORA, Hm, LANE), d_wv.reshape(KV_LORA, Hm, LANE)], axis=2).reshape(
        KV_LORA, Hm * (NOPE_DIM + V_DIM))
    gs_a2 = [stacked(gr[n], SHARD_AXIS[n]) for n in PART_A2]
    hs_a2 = [add_half(g_, r_, cidx, "add_half_" + n) for n, g_, r_ in zip(PART_A2, gs_a2, swap_halves(gs_a2, "swap_halves_a2"))]
    dh0, gr['ffn1_norm'], gr['ffn1_w_gate'], gr['ffn1_w_up'], gr['ffn1_w_down'], ps_a2, own_a1 = ffn_bwd(
        dh1, sv1, w2['ffn1_norm'], stk['ffn1_w_gate'], stk['ffn1_w_up'], full['ffn1_w_down'], "ffn1", hs_a2, PART_A1)
    dh0_3 = rows3(dh0)
    grad_x = dh0_3[:, N_META:]
    gr['meta_tokens'] = jnp.sum(dh0_3[:, :N_META], axis=0)

    ps_a1, hs_a1 = [own_a1[n][0] for n in PART_A1], [own_a1[n][1] for n in PART_A1]
    sc = (chip32.reshape(1), cidx)
    ts = [sum_chips(p, h, sc, "sum_chips_" + n)
          for n, p, h in zip(PART_A1 + PART_A2 + PART_B, ps_a1 + ps_a2 + ps_b, hs_a1 + hs_a2 + hs_b)]
    g_shard = {n: z.reshape(w2[n].shape) for n, z in zip(PART_A1 + PART_A2 + PART_B, share_sibling(ts))}
    small = jnp.concatenate([gr[n].reshape(-1) for n in SMALL] + [gr['meta_tokens'].reshape(-1), lossp.reshape(-1)])
    ns = small.shape[0]
    nsp = -(-ns // (8 * LANE)) * 8 * LANE
    small_sum = allreduce8(jnp.pad(small, (0, nsp - ns)).reshape(-1, LANE), "allreduce_small").reshape(-1)
    g_small = dict(zip(SMALL + ['meta_full'], _unpack(small_sum, [w2[n].shape for n in SMALL] + [(N_META, D)])))
    g_shard['meta_tokens'] = lax.dynamic_slice(
        g_small['meta_full'], (jnp.zeros((), jnp.int32), (chip * mcols).astype(jnp.int32)), (N_META, mcols))
    loss = small_sum[ns - LANE]

    grads, deltas, new_m, new_v = [], [], [], []
    for n in WEIGHTS:
        gw = g_shard[n] if n in g_shard else g_small[n]
        d_, m_, v_ = adamw(w2[n], gw, m2[n], v2[n], "adamw_" + n)
        shp = a[n].shape
        grads.append(gw.reshape(shp))
        deltas.append(d_.reshape(shp))
        new_m.append(m_.reshape(shp))
        new_v.append(v_.reshape(shp))
    return (loss, grad_x, *grads, *deltas, *new_m, *new_v)


def kernel(x, meta_tokens, ffn1_norm, ffn1_w_gate, ffn1_w_up, ffn1_w_down, mix_norm, w_in, tm_mu, w0, w_up, a0, a_up, g_up, k_k, k_a, r_k, gn_w, gn_b, q_norm, w_uq, kv_norm, w_ukv, w_out, ffn2_norm, ffn2_w_gate, ffn2_w_up, ffn2_w_down, final_norm, loss_target, m_meta_tokens, m_ffn1_norm, m_ffn1_w_gate, m_ffn1_w_up, m_ffn1_w_down, m_mix_norm, m_w_in, m_tm_mu, m_w0, m_w_up, m_a0, m_a_up, m_g_up, m_k_k, m_k_a, m_r_k, m_gn_w, m_gn_b, m_q_norm, m_w_uq, m_kv_norm, m_w_ukv, m_w_out, m_ffn2_norm, m_ffn2_w_gate, m_ffn2_w_up, m_ffn2_w_down, m_final_norm, v_meta_tokens, v_ffn1_norm, v_ffn1_w_gate, v_ffn1_w_up, v_ffn1_w_down, v_mix_norm, v_w_in, v_tm_mu, v_w0, v_w_up, v_a0, v_a_up, v_g_up, v_k_k, v_k_a, v_r_k, v_gn_w, v_gn_b, v_q_norm, v_w_uq, v_kv_norm, v_w_ukv, v_w_out, v_ffn2_norm, v_ffn2_w_gate, v_ffn2_w_up, v_ffn2_w_down, v_final_norm):
    return _step(dict(locals()))
```

```python
import functools
import math

import jax
import jax.numpy as jnp
import numpy as np
from jax import lax
from jax.experimental import pallas as pl
from jax.experimental.pallas import tpu as pltpu

F32 = jnp.float32
BF16 = jnp.bfloat16
MESH = pl.DeviceIdType.MESH

N_META = 16
NORM_EPS = 1e-6
RWKV_HEAD = 64
GN_EPS = RWKV_HEAD * 1e-5
W_LORA, A_LORA, G_LORA = 96, 96, 256
Q_LORA, KV_LORA = 512, 512
NOPE_DIM, ROPE_DIM, V_DIM = 128, 64, 128
QK_DIM = NOPE_DIM + ROPE_DIM
ROPE_THETA = 10000.0
Q_BLOCK = 128
ADAM_LR, ADAM_B1, ADAM_B2, ADAM_EPS, ADAM_WD, ADAM_STEP = 0.001, 0.9, 0.999, 1e-08, 0.01, 10

LANE = 128
VMEM_LIMIT = 56 * 1024 * 1024


def _pcall(body, **kw):
    return pl.pallas_call(body, **kw)


def _cparams(sem):
    return pltpu.CompilerParams(dimension_semantics=sem, vmem_limit_bytes=VMEM_LIMIT)


MM_LANE_TILE = 1536
MM_ROW_TILE = 1408


def _div_tile(n, cap, unit):
    best = None
    for t in range(unit, min(n, cap) + 1, unit):
        if n % t == 0:
            best = t
    return best if best is not None else n


def _rtile(n, pref=512):
    best = None
    for t in range(16, min(n, pref * 2) + 1, 16):
        if n % t == 0 and (best is None or abs(t - pref) < abs(best - pref)):
            best = t
    return best if best is not None else n


def mm(a, b, mode="nn", out_dtype=F32, res=None, alpha=1.0, b_stack=False, out_stack=False, carry=None, name="mm"):
    kind, carried = carry if carry else (None, ())
    nc = len(carried)
    cs = b.shape[-1] if b_stack else None
    bs = (b.shape[1], 4 * cs) if b_stack else b.shape
    if mode == "nn":
        (M, K), (K2, N) = a.shape, bs
    elif mode == "nt":
        (M, K), (N, K2) = a.shape, bs
    else:
        (K, M), (K2, N) = a.shape, bs
    assert K == K2, (a.shape, b.shape, mode)
    if mode == "tn":
        tm, tk = _div_tile(M, MM_ROW_TILE, LANE), _div_tile(K, 1024, 16)
    else:
        tm = _div_tile(M, MM_ROW_TILE, 16)
        tk = _div_tile(cs if (b_stack and mode == "nt") else K, MM_LANE_TILE, LANE)
    ncol = N // 4 if out_stack else (cs if (b_stack and mode == "nn") else N)
    tn = _div_tile(ncol, MM_LANE_TILE if mode != "nt" else 1024, LANE)
    nk = K // tk
    dims = {"nn": (((1,), (0,)), ((), ())), "nt": (((1,), (1,)), ((), ())), "tn": (((0,), (0,)), ((), ()))}[mode]
    direct = out_dtype == F32

    def body(*refs):
        a_ref, b_ref = refs[:2]
        r_ref = refs[2] if res is not None else None
        nin = 3 if res is not None else 2
        o_ref = refs[nin + nc]
        acc = o_ref if direct else refs[nin + 2 * nc + 1]
        k = pl.program_id(2)
        if nc:
            steps = _ag_steps if kind == "ag" else _rs_steps
            start, finish = steps(refs[nin:nin + nc], refs[nin + nc + 1:nin + 2 * nc + 1],
                                  *refs[nin + 2 * nc + (1 if direct else 2):])
            i, j = pl.program_id(0), pl.program_id(1)
            pl.when((i == 0) & (j == 0) & (k == 0))(start)

        @pl.when(k == 0)
        def _():
            acc[...] = jnp.zeros_like(acc) if res is None else r_ref[...].astype(F32)

        p = lax.dot_general(a_ref[...].astype(BF16), b_ref[...].astype(BF16), dims, preferred_element_type=F32)
        acc[...] += p if alpha == 1.0 else alpha * p

        if not direct:
            @pl.when(k == nk - 1)
            def _():
                o_ref[...] = acc[...].astype(o_ref.dtype)

        if nc:
            pl.when((i == M // tm - 1) & (j == N // tn - 1) & (k == nk - 1))(finish)

    if mode == "tn":
        a_spec = pl.BlockSpec((tk, tm), lambda i, j, k: (k, i))
        b_spec = pl.BlockSpec((tk, tn), lambda i, j, k: (k, j))
    else:
        a_spec = pl.BlockSpec((tm, tk), lambda i, j, k: (i, k))
        if mode == "nn":
            if b_stack:
                nps = cs // tn
                b_spec = pl.BlockSpec((None, tk, tn), lambda i, j, k: (j // nps, k, j % nps))
            else:
                b_spec = pl.BlockSpec((tk, tn), lambda i, j, k: (k, j))
        elif b_stack:
            kps = cs // tk
            b_spec = pl.BlockSpec((None, tn, tk), lambda i, j, k: (k // kps, j, k % kps))
        else:
            b_spec = pl.BlockSpec((tn, tk), lambda i, j, k: (j, k))
    r_spec = pl.BlockSpec((tm, tn), lambda i, j, k: (i, j))
    if out_stack:
        ops = (N // 4) // tn
        o_spec = pl.BlockSpec((None, tm, tn), lambda i, j, k: (j // ops, i, j % ops))
        o_shape = jax.ShapeDtypeStruct((4, M, N // 4), out_dtype)
    else:
        o_spec, o_shape = r_spec, jax.ShapeDtypeStruct((M, N), out_dtype)
    in_specs = [a_spec, b_spec] + ([r_spec] if res is not None else [])
    args = (a, b) + ((res,) if res is not None else ())
    if not nc:
        return _pcall(
            body, name=name, grid=(M // tm, N // tn, nk), in_specs=in_specs, out_specs=o_spec, out_shape=o_shape,
            scratch_shapes=[] if direct else [pltpu.VMEM((tm, tn), F32)],
            compiler_params=_cparams(("parallel", "parallel", "arbitrary")),
        )(*args)
    out = _pcall(
        body, name=name, grid=(M // tm, N // tn, nk), in_specs=in_specs + [_HBM] * nc, out_specs=[o_spec] + [_HBM] * nc,
        out_shape=[o_shape] + [jax.ShapeDtypeStruct(z.shape, z.dtype) for z in carried],
        input_output_aliases={len(args) + q: 1 + q for q in range(nc)} if kind == "ag" else {},
        scratch_shapes=([] if direct else [pltpu.VMEM((tm, tn), F32)]) + (_ag_sems(nc) if kind == "ag" else _rs_sems(nc)),
        compiler_params=_cparams(("arbitrary",) * 3),
    )(*args, *carried)
    return out[0], list(out[1:])


MM_EPI_ROW_TILE = 704


def mm_epi(a, bs, mode, epi, extras, out_dtypes, b_stack=False, carry=None, name="mm_epi"):
    kind, carried = carry if carry else (None, ())
    nc = len(carried)
    b = bs[0]
    cs = b.shape[-1] if b_stack else None
    bshape = (b.shape[1], 4 * cs) if b_stack else b.shape
    (M, K) = a.shape
    (K2, N) = bshape if mode == "nn" else bshape[::-1]
    assert K == K2 and mode in ("nn", "nt"), (a.shape, b.shape, mode)
    tm = _div_tile(M, MM_EPI_ROW_TILE, 16)
    tk = _div_tile(cs if (b_stack and mode == "nt") else K, MM_LANE_TILE, LANE)
    tn = _div_tile(cs if (b_stack and mode == "nn") else N, MM_LANE_TILE, LANE)
    nk, nb, ne, no = K // tk, len(bs), len(extras), len(out_dtypes)
    dims = (((1,), (0,)), ((), ())) if mode == "nn" else (((1,), (1,)), ((), ()))

    def body(*refs):
        a_ref, b_refs, e_refs = refs[0], refs[1:1 + nb], refs[1 + nb:1 + nb + ne]
        base = 1 + nb + ne + nc
        o_refs, accs = refs[base:base + no], refs[base + no + nc:base + no + nc + nb]
        k = pl.program_id(2)
        if nc:
            steps = _ag_steps if kind == "ag" else _rs_steps
            start, finish = steps(refs[base - nc:base], refs[base + no:base + no + nc], *refs[base + no + nc + nb:])
            i, j = pl.program_id(0), pl.program_id(1)
            pl.when((i == 0) & (j == 0) & (k == 0))(start)

        @pl.when(k == 0)
        def _():
            for acc in accs:
                acc[...] = jnp.zeros_like(acc)

        av = a_ref[...].astype(BF16)
        for b_ref, acc in zip(b_refs, accs):
            acc[...] += lax.dot_general(av, b_ref[...].astype(BF16), dims, preferred_element_type=F32)

        @pl.when(k == nk - 1)
        def _():
            res = epi(*[acc[...] for acc in accs], *[e[...] for e in e_refs])
            for o_ref, v in zip(o_refs, res):
                o_ref[...] = v.astype(o_ref.dtype)

        if nc:
            pl.when((i == M // tm - 1) & (j == N // tn - 1) & (k == nk - 1))(finish)

    a_spec = pl.BlockSpec((tm, tk), lambda i, j, k: (i, k))
    if mode == "nn":
        if b_stack:
            nps = cs // tn
            b_spec = pl.BlockSpec((None, tk, tn), lambda i, j, k: (j // nps, k, j % nps))
        else:
            b_spec = pl.BlockSpec((tk, tn), lambda i, j, k: (k, j))
    elif b_stack:
        kps = cs // tk
        b_spec = pl.BlockSpec((None, tn, tk), lambda i, j, k: (k // kps, j, k % kps))
    else:
        b_spec = pl.BlockSpec((tn, tk), lambda i, j, k: (j, k))
    o_spec = pl.BlockSpec((tm, tn), lambda i, j, k: (i, j))
    sems = (_ag_sems(nc) if kind == "ag" else _rs_sems(nc)) if nc else []
    res = _pcall(
        body, name=name, grid=(M // tm, N // tn, nk), in_specs=[a_spec] + [b_spec] * nb + [o_spec] * ne + [_HBM] * nc,
        out_specs=[o_spec] * no + [_HBM] * nc,
        out_shape=[jax.ShapeDtypeStruct((M, N), dt) for dt in out_dtypes]
        + [jax.ShapeDtypeStruct(z.shape, z.dtype) for z in carried],
        input_output_aliases={1 + nb + ne + q: no + q for q in range(nc)} if kind == "ag" else {},
        scratch_shapes=[pltpu.VMEM((tm, tn), F32)] * nb + sems,
        compiler_params=_cparams(("arbitrary",) * 3 if nc else ("parallel", "parallel", "arbitrary")),
    )(a, *bs, *extras, *carried)
    return (list(res[:no]), list(res[no:])) if nc else res


def rowwise(fn, row_ins, par_ins, outs, accs=(), *, tm, ncb=1, name="rowwise"):
    R = row_ins[0][0].shape[0]
    assert R % tm == 0, (R, tm)
    nrb = R // tm
    in_specs, args = [], []
    for spec in row_ins:
        arr, w, base = spec[:3]
        mod = spec[3] if len(spec) > 3 else None
        cstep = 0 if (len(spec) > 4 and spec[4]) else 1
        halo = spec[5] if len(spec) > 5 else None
        if halo == "prev":
            in_specs.append(pl.BlockSpec((8, w), lambda j, i, base=base: (jnp.maximum(i * (tm // 8) - 1, 0), base + j)))
        elif halo == "next":
            in_specs.append(pl.BlockSpec((8, w), lambda j, i, base=base: (jnp.minimum((i + 1) * (tm // 8), R // 8 - 1), base + j)))
        elif mod is None:
            in_specs.append(pl.BlockSpec((tm, w), lambda j, i, base=base, cstep=cstep: (i, base + cstep * j)))
        else:
            in_specs.append(pl.BlockSpec((tm, w), lambda j, i, base=base, mod=mod, cstep=cstep: (i % mod, base + cstep * j)))
        args.append(arr)
    for arr, w, base in par_ins:
        in_specs.append(pl.BlockSpec((arr.shape[0], w), lambda j, i, base=base: (0, base + j)))
        args.append(arr)
    out_specs, out_shape = [], []
    for cols, w, dt in outs:
        out_specs.append(pl.BlockSpec((tm, w), lambda j, i: (i, j)))
        out_shape.append(jax.ShapeDtypeStruct((R, cols), dt))
    for p, cols, w in accs:
        out_specs.append(pl.BlockSpec((p, w), lambda j, i: (0, j)))
        out_shape.append(jax.ShapeDtypeStruct((p, cols), F32))
    nin, nout, nacc = len(args), len(outs), len(accs)

    def body(*refs):
        vals = [r[...] for r in refs[:nin]]
        res = fn(*vals)
        if not isinstance(res, (tuple, list)):
            res = (res,)
        assert len(res) == nout + nacc, (len(res), nout, nacc)
        for o_ref, v in zip(refs[nin:nin + nout], res[:nout]):
            o_ref[...] = v.astype(o_ref.dtype)
        if nacc:
            i = pl.program_id(1)

            @pl.when(i == 0)
            def _():
                for a_ref in refs[nin + nout:]:
                    a_ref[...] = jnp.zeros_like(a_ref)

            for a_ref, v in zip(refs[nin + nout:], res[nout:]):
                a_ref[...] += v.astype(F32)

    r = _pcall(
        body, name=name, grid=(ncb, nrb), in_specs=in_specs, out_specs=out_specs, out_shape=out_shape,
        compiler_params=_cparams(("parallel", "arbitrary")),
    )(*args)
    return r


def vjp_fn(fwd, nprim):
    def f(*vals):
        prim, cts = vals[:nprim], vals[nprim:]
        out, pull = jax.vjp(fwd, *[p.astype(F32) for p in prim])
        if not isinstance(out, (tuple, list)):
            cts = cts[0].astype(F32)
        else:
            cts = tuple(c.astype(F32) for c in cts)
        return pull(cts)
    return f


WKV_TB = 16


def wkv_fwd(w, kn, b, k, r, v, bufs=()):
    T, N, L = w.shape
    NI = v.shape[1]
    tb = WKV_TB
    n = len(bufs)
    assert T % tb == 0 and L == v.shape[2]

    def body(*refs):
        w_ref, kn_ref, b_ref, k_ref, r_ref, v_ref = refs[:6]
        y_ref, sp_ref, sa_ref = refs[6 + n:9 + n]
        s_ref = refs[9 + 2 * n]
        if n:
            start, finish = _ag_steps(refs[6:6 + n], refs[9 + n:9 + 2 * n], *refs[10 + 2 * n:])
            pl.when(pl.program_id(0) == 0)(start)

        @pl.when(pl.program_id(0) == 0)
        def _():
            s_ref[...] = jnp.zeros_like(s_ref)

        def step(s, carry):
            W, KN, B, Kk, Rr = w_ref[s], kn_ref[s], b_ref[s], k_ref[s], r_ref[s]
            for i in range(NI):
                S = s_ref[i]
                sp_ref[s, i] = S
                sa = jnp.sum(S * KN, axis=0, keepdims=True)
                sa_ref[s, pl.ds(i, 1), :] = sa
                vi = v_ref[s, pl.ds(i, 1), :]
                Sn = S * W + sa * B + vi * Kk
                s_ref[i] = Sn
                y_ref[s, pl.ds(i, 1), :] = jnp.sum(Sn * Rr, axis=0, keepdims=True)
            return carry

        lax.fori_loop(0, tb, step, 0)
        if n:
            pl.when(pl.program_id(0) == T // tb - 1)(finish)

    jspec = pl.BlockSpec((tb, N, L), lambda t: (t, 0, 0))
    ispec = pl.BlockSpec((tb, NI, L), lambda t: (t, 0, 0))
    ishape = jax.ShapeDtypeStruct((T, NI, L), F32)
    res = _pcall(
        body, name="wkv_fwd", grid=(T // tb,), in_specs=[jspec] * 5 + [ispec] + [_HBM] * n,
        out_specs=[ispec, pl.BlockSpec((tb, NI, N, L), lambda t: (t, 0, 0, 0)), ispec] + [_HBM] * n,
        out_shape=[ishape, jax.ShapeDtypeStruct((T, NI, N, L), F32), ishape]
        + [jax.ShapeDtypeStruct(z.shape, z.dtype) for z in bufs],
        input_output_aliases={6 + i: 3 + i for i in range(n)},
        scratch_shapes=[pltpu.VMEM((NI, N, L), F32)] + (_ag_sems(n) if n else []),
        compiler_params=_cparams(("arbitrary",)),
    )(w, kn, b, k, r, v, *bufs)
    return res[0], res[1], res[2], list(res[3:])


def wkv_bwd(w, kn, b, k, r, v, dy, sp, sa):
    T, N, L = w.shape
    NI, LH = v.shape[1], L // 2
    tb = WKV_TB
    nt = T // tb

    def body(w_ref, kn_ref, b_ref, k_ref, r_ref, v_ref, dy_ref, sp_ref, sa_ref,
             dw_ref, dkn_ref, db_ref, dk_ref, dr_ref, dv_ref, ds_ref):
        @pl.when(pl.program_id(0) == 0)
        def _():
            ds_ref[...] = jnp.zeros_like(ds_ref)

        def step(q, carry):
            s = tb - 1 - q
            W, KN, B, Kk, Rr = w_ref[s], kn_ref[s], b_ref[s], k_ref[s], r_ref[s]
            dW = jnp.zeros((N, L), F32)
            dKN, dB, dK, T1 = dW, dW, dW, dW
            al = jnp.zeros((1, L), F32)
            be = al
            for i in range(NI):
                Sp = sp_ref[s, i]
                vi = v_ref[s, pl.ds(i, 1), :]
                dyi = dy_ref[s, pl.ds(i, 1), :]
                sai = sa_ref[s, pl.ds(i, 1), :]
                dS = ds_ref[i] + dyi * Rr
                T1 = T1 + Sp * dyi
                al = al + sai * dyi
                be = be + vi * dyi
                dv_ref[s, pl.ds(i, 1), :] = jnp.sum(dS * Kk, axis=0, keepdims=True)
                dK = dK + dS * vi
                dsa = jnp.sum(dS * B, axis=0, keepdims=True)
                dB = dB + dS * sai
                dW = dW + dS * Sp
                dKN = dKN + Sp * dsa
                ds_ref[i] = dS * W + dsa * KN
            dR = W * T1 + B * al + Kk * be
            for ref, val in ((dw_ref, dW), (dkn_ref, dKN), (db_ref, dB), (dk_ref, dK), (dr_ref, dR)):
                ref[s] = (val + pltpu.roll(val, LH, 1))[:, :LH]
            return carry

        lax.fori_loop(0, tb, step, 0)

    jspec = pl.BlockSpec((tb, N, L), lambda t: (nt - 1 - t, 0, 0))
    gspec = pl.BlockSpec((tb, N, LH), lambda t: (nt - 1 - t, 0, 0))
    ispec = pl.BlockSpec((tb, NI, L), lambda t: (nt - 1 - t, 0, 0))
    gshape = jax.ShapeDtypeStruct((T, N, LH), F32)
    return _pcall(
        body, name="wkv_bwd", grid=(nt,),
        in_specs=[jspec] * 5 + [ispec, ispec, pl.BlockSpec((tb, NI, N, L), lambda t: (nt - 1 - t, 0, 0, 0)), ispec],
        out_specs=[gspec] * 5 + [ispec], out_shape=[gshape] * 5 + [jax.ShapeDtypeStruct((T, NI, L), F32)],
        scratch_shapes=[pltpu.VMEM((NI, N, L), F32)],
        compiler_params=_cparams(("arbitrary",)),
    )(w, kn, b, k, r, v, dy, sp, sa)


_NT = (((1,), (1,)), ((), ()))
_TN = (((0,), (0,)), ((), ()))
ATT_SCALE = QK_DIM ** -0.5


def _att_bias():
    col = lax.broadcasted_iota(jnp.int32, (Q_BLOCK, Q_BLOCK), 1)
    row = lax.broadcasted_iota(jnp.int32, (Q_BLOCK, Q_BLOCK), 0)
    return jnp.where(col < N_META, 0.0, -1e30).astype(F32), jnp.where(col <= row, 0.0, -1e30).astype(F32)


def _att_scores(q1, q2, kn_ref, kp_ref, s_ref, bias, L):
    s = lax.dot_general(q1, kn_ref[0, :L, :], _NT, preferred_element_type=F32)
    s = s + lax.dot_general(q2, kp_ref[0, :L, :], _NT, preferred_element_type=F32)
    s_ref[:, :L] = s * ATT_SCALE
    s_ref[:, :Q_BLOCK] += bias[0]
    s_ref[:, L - Q_BLOCK:L] += bias[1]


def _att_probs(s_ref, L):
    s = s_ref[:, :L]
    m = jnp.max(s, axis=-1, keepdims=True)
    p = jnp.exp(s - m)
    return p / jnp.sum(p, axis=-1, keepdims=True)


def _att_blocks(nq, qn_ref, qp_ref, kn_ref, kp_ref, s_ref):
    bias = _att_bias()
    rows = lambda i: pl.ds(Q_BLOCK * i, Q_BLOCK)
    L = lambda i: Q_BLOCK * (i + 2)
    score = lambda i: _att_scores(qn_ref[0, rows(i), :], qp_ref[0, rows(i), :], kn_ref, kp_ref, s_ref.at[i % 2], bias, L(i))
    score(0)
    for i in range(nq):
        if i + 1 < nq:
            score(i + 1)
        yield i, rows(i), L(i), _att_probs(s_ref.at[i % 2], L(i))


def _edge_steps(nb, nh):
    b, h = pl.program_id(0), pl.program_id(1)
    return (b == 0) & (h == 0), (b == nb - 1) & (h == nh - 1)


def attn_fwd(qn, qp, kn, kp, v, bufs=()):
    B, S, HD = qn.shape
    SP = kn.shape[1]
    H = HD // LANE
    nq = S // Q_BLOCK
    n = len(bufs)

    def body(*refs):
        qn_ref, qp_ref, kn_ref, kp_ref, v_ref = refs[:5]
        o_ref = refs[5 + n]
        s_ref = refs[6 + 2 * n]
        if n:
            start, finish = _ag_steps(refs[5:5 + n], refs[6 + n:6 + 2 * n], *refs[7 + 2 * n:])
            is_first, is_last = _edge_steps(B, H)
            pl.when(is_first)(start)
        for i, rows, L, p in _att_blocks(nq, qn_ref, qp_ref, kn_ref, kp_ref, s_ref):
            o_ref[0, rows, :] = jnp.dot(p.astype(BF16), v_ref[0, :L, :], preferred_element_type=F32)
        if n:
            pl.when(is_last)(finish)

    qspec = pl.BlockSpec((1, S, LANE), lambda b, h: (b, 0, h))
    kspec = pl.BlockSpec((1, SP, LANE), lambda b, h: (b, 0, h))
    pspec = pl.BlockSpec((1, SP, LANE), lambda b, h: (b, 0, 0))
    res = _pcall(
        body, name="attn_fwd", grid=(B, H), in_specs=[qspec, qspec, kspec, pspec, kspec] + [_HBM] * n,
        out_specs=[qspec] + [_HBM] * n,
        out_shape=[jax.ShapeDtypeStruct((B, S, HD), F32)] + [jax.ShapeDtypeStruct(w.shape, w.dtype) for w in bufs],
        input_output_aliases={5 + i: 1 + i for i in range(n)},
        scratch_shapes=[pltpu.VMEM((2, Q_BLOCK, SP), F32)] + (_ag_sems(n) if n else []),
        compiler_params=_cparams(("arbitrary", "arbitrary")),
    )(qn, qp, kn, kp, v, *bufs)
    return res[0], list(res[1:])


def attn_bwd(qn, qp, kn, kp, v, do, hs=()):
    B, S, HD = qn.shape
    SP = kn.shape[1]
    H = HD // LANE
    nq = S // Q_BLOCK
    n = len(hs)

    def body(*refs):
        qn_ref, qp_ref, kn_ref, kp_ref, v_ref, do_ref = refs[:6]
        dqn_ref, dqp_ref, dkn_ref, dv_ref, dkp_ref = refs[6 + n:11 + n]
        dkn_acc, dv_acc, s_ref = refs[11 + 2 * n:14 + 2 * n]
        if n:
            start, finish = _rs_steps(refs[6:6 + n], refs[11 + n:11 + 2 * n], *refs[14 + 2 * n:])
            is_first, is_last = _edge_steps(B, H)
            pl.when(is_first)(start)

        @pl.when(pl.program_id(1) == 0)
        def _():
            dkp_ref[...] = jnp.zeros_like(dkp_ref)

        dkn_acc[...] = jnp.zeros_like(dkn_acc)
        dv_acc[...] = jnp.zeros_like(dv_acc)
        for i, rows, L, p in _att_blocks(nq, qn_ref, qp_ref, kn_ref, kp_ref, s_ref):
            q1, q2, do_i = qn_ref[0, rows, :], qp_ref[0, rows, :], do_ref[0, rows, :]
            dp = lax.dot_general(do_i, v_ref[0, :L, :], _NT, preferred_element_type=F32)
            ds = (p * (dp - jnp.sum(p * dp, axis=-1, keepdims=True)) * ATT_SCALE).astype(BF16)
            dqn_ref[0, rows, :] = jnp.dot(ds, kn_ref[0, :L, :], preferred_element_type=F32).astype(dqn_ref.dtype)
            dqp_ref[0, rows, :] = jnp.dot(ds, kp_ref[0, :L, :], preferred_element_type=F32)
            dkn_acc[:L, :] += lax.dot_general(ds, q1, _TN, preferred_element_type=F32)
            dkp_ref[0, :L, :] += lax.dot_general(ds, q2, _TN, preferred_element_type=F32)
            dv_acc[:L, :] += lax.dot_general(p.astype(BF16), do_i, _TN, preferred_element_type=F32)
        dkn_ref[0] = dkn_acc[...].astype(dkn_ref.dtype)
        dv_ref[0] = dv_acc[...].astype(dv_ref.dtype)
        if n:
            pl.when(is_last)(finish)

    qspec = pl.BlockSpec((1, S, LANE), lambda b, h: (b, 0, h))
    kspec = pl.BlockSpec((1, SP, LANE), lambda b, h: (b, 0, h))
    pspec = pl.BlockSpec((1, SP, LANE), lambda b, h: (b, 0, 0))
    res = _pcall(
        body, name="attn_bwd", grid=(B, H), in_specs=[qspec, qspec, kspec, pspec, kspec, qspec] + [_HBM] * n,
        out_specs=[qspec, qspec, kspec, kspec, pspec] + [_HBM] * n,
        out_shape=[jax.ShapeDtypeStruct((B, S, HD), BF16), jax.ShapeDtypeStruct((B, S, HD), F32),
                   jax.ShapeDtypeStruct((B, SP, HD), BF16), jax.ShapeDtypeStruct((B, SP, HD), BF16),
                   jax.ShapeDtypeStruct((B, SP, LANE), F32)] + [jax.ShapeDtypeStruct(h.shape, h.dtype) for h in hs],
        scratch_shapes=[pltpu.VMEM((SP, LANE), F32), pltpu.VMEM((SP, LANE), F32), pltpu.VMEM((2, Q_BLOCK, SP), F32)]
        + (_rs_sems(n) if n else []),
        compiler_params=_cparams(("arbitrary", "arbitrary")),
    )(qn, qp, kn, kp, v, do, *hs)
    return res[:5], list(res[5:])


_HBM = pl.BlockSpec(memory_space=pltpu.HBM)


def _place():
    x, y, c = lax.axis_index("x"), lax.axis_index("y"), lax.axis_index("c")
    chips = [(1 - x, y), (x, 1 - y), (1 - x, 1 - y)]
    return x, y, c, chips


def _rcopy(src, dst, ssem, rsem, dev):
    return pltpu.make_async_remote_copy(src_ref=src, dst_ref=dst, send_sem=ssem, recv_sem=rsem,
                                        device_id=dev, device_id_type=MESH)


def _half_rows(c, r):
    return pl.ds(pl.multiple_of(c * (r // 2), 16), r // 2)


def _ag_steps(w, out, ssem, rsem):
    n = len(w)
    x, y, c, chips = _place()
    s = 2 * x + y
    rows = [_half_rows(c, w[i].shape[1]) for i in range(n)]
    orows = [_half_rows(1 - c, w[i].shape[1]) for i in range(n)]
    first = [_rcopy(w[i].at[s, rows[i]], out[i].at[s, rows[i]], ssem.at[6 * i + j], rsem.at[6 * i + j], (px, py, c))
             for i in range(n) for j, (px, py) in enumerate(chips)]

    def start():
        for cp in first:
            cp.start()

    def finish():
        passed = []
        for j, (px, py) in enumerate(chips):
            sp = 2 * px + py
            for i in range(n):
                here = out[i].at[sp, rows[i]]
                _rcopy(here, here, ssem.at[6 * i + j], rsem.at[6 * i + j], (px, py, c)).wait_recv()
                fw = _rcopy(here, here, ssem.at[6 * i + 3 + j], rsem.at[6 * i + 3 + j], (x, y, 1 - c))
                fw.start()
                passed.append(fw)
        for j, (px, py) in enumerate(chips):
            sp = 2 * px + py
            for i in range(n):
                there = out[i].at[sp, orows[i]]
                _rcopy(there, there, ssem.at[6 * i + 3 + j], rsem.at[6 * i + 3 + j], (x, y, 1 - c)).wait_recv()
        for cp in first + passed:
            cp.wait_send()

    return start, finish


def _ag_sems(n):
    return [pltpu.SemaphoreType.DMA((6 * n,)), pltpu.SemaphoreType.DMA((6 * n,))]


def ag_weights(bufs):
    n = len(bufs)

    def body(*refs):
        start, finish = _ag_steps(refs[:n], refs[n:2 * n], *refs[2 * n:])
        start()
        finish()

    return _pcall(
        body, name="ag_weights", in_specs=[_HBM] * n, out_specs=[_HBM] * n,
        out_shape=[jax.ShapeDtypeStruct(w.shape, w.dtype) for w in bufs],
        input_output_aliases={i: i for i in range(n)}, scratch_shapes=_ag_sems(n),
    )(*bufs)


def swap_halves(gs, name):
    n = len(gs)

    def body(*refs):
        g, out = refs[:n], refs[n:2 * n]
        ssem, rsem = refs[2 * n:]
        x, y, c, _ = _place()
        cps = [_rcopy(g[i].at[:, _half_rows(1 - c, g[i].shape[1])], out[i], ssem.at[i], rsem.at[i], (x, y, 1 - c))
               for i in range(n)]
        for cp in cps:
            cp.start()
        for cp in cps:
            cp.wait()

    return _pcall(
        body, name=name, in_specs=[_HBM] * n, out_specs=[_HBM] * n,
        out_shape=[jax.ShapeDtypeStruct((4, g.shape[1] // 2, g.shape[2]), g.dtype) for g in gs],
        scratch_shapes=[pltpu.SemaphoreType.DMA((n,)), pltpu.SemaphoreType.DMA((n,))],
    )(*gs)


def _rs_steps(h, out, ssem, rsem):
    n = len(h)
    x, y, c, chips = _place()
    s = 2 * x + y
    cps = [_rcopy(h[i].at[2 * px + py], out[i].at[s], ssem.at[3 * i + j], rsem.at[3 * i + j], (px, py, c))
           for i in range(n) for j, (px, py) in enumerate(chips)]

    def start():
        for cp in cps:
            cp.start()

    def finish():
        for j, (px, py) in enumerate(chips):
            for i in range(n):
                _rcopy(h[i].at[s], out[i].at[2 * px + py], ssem.at[3 * i + j], rsem.at[3 * i + j], (px, py, c)).wait_recv()
        for cp in cps:
            cp.wait_send()

    return start, finish


def _rs_sems(n):
    return [pltpu.SemaphoreType.DMA((3 * n,)), pltpu.SemaphoreType.DMA((3 * n,))]


def share_sibling(ts):
    n = len(ts)

    def body(*refs):
        t, out = refs[:n], refs[n:2 * n]
        ssem, rsem = refs[2 * n:]
        x, y, c, _ = _place()
        cps = [_rcopy(t[i].at[c], out[i].at[c], ssem.at[i], rsem.at[i], (x, y, 1 - c)) for i in range(n)]
        for cp in cps:
            cp.start()
        for i in range(n):
            _rcopy(t[i].at[c], out[i].at[1 - c], ssem.at[i], rsem.at[i], (x, y, 1 - c)).wait_recv()
        for cp in cps:
            cp.wait_send()

    return _pcall(
        body, name="share_sibling", in_specs=[_HBM] * n, out_specs=[_HBM] * n,
        out_shape=[jax.ShapeDtypeStruct(t.shape, t.dtype) for t in ts],
        input_output_aliases={i: i for i in range(n)},
        scratch_shapes=[pltpu.SemaphoreType.DMA((n,)), pltpu.SemaphoreType.DMA((n,))],
    )(*ts)


def allreduce8(v, name):
    P, W = v.shape

    def body(v_ref, out_ref, buf, ssem, rsem):
        x, y, c, _ = _place()
        me = 4 * x + 2 * y + c
        buf[me] = v_ref[...]
        cps = []
        for k in range(1, 8):
            px = 1 - x if k & 4 else x
            py = 1 - y if k & 2 else y
            pc = 1 - c if k & 1 else c
            cp = _rcopy(buf.at[me], buf.at[me], ssem.at[k - 1], rsem.at[k - 1], (px, py, pc))
            cp.start()
            cps.append((cp, 4 * px + 2 * py + pc))
        for k, (cp, peer) in enumerate(cps):
            _rcopy(buf.at[me], buf.at[peer], ssem.at[k], rsem.at[k], (x, y, c)).wait_recv()
        for cp, _ in cps:
            cp.wait_send()
        acc = buf[0]
        for d in range(1, 8):
            acc = acc + buf[d]
        out_ref[...] = acc

    return _pcall(
        body, name=name, in_specs=[pl.BlockSpec(memory_space=pltpu.VMEM)],
        out_specs=pl.BlockSpec(memory_space=pltpu.VMEM), out_shape=jax.ShapeDtypeStruct((P, W), F32),
        scratch_shapes=[pltpu.VMEM((8, P, W), F32), pltpu.SemaphoreType.DMA((7,)), pltpu.SemaphoreType.DMA((7,))],
    )(v)


def add_half(g, rcv, cidx, name):
    _, hr, W = rcv.shape
    tr = _rows_tile(hr, W)
    nb = hr // tr

    def body(c_ref, g_ref, r_ref, o_ref):
        o_ref[...] = (g_ref[...] + r_ref[...]).astype(o_ref.dtype)

    return _pcall(
        body, name=name,
        grid_spec=pltpu.PrefetchScalarGridSpec(
            num_scalar_prefetch=1, grid=(4, nb),
            in_specs=[pl.BlockSpec((1, tr, W), lambda s, i, c: (s, c[0] * nb + i, 0)),
                      pl.BlockSpec((1, tr, W), lambda s, i, c: (s, i, 0))],
            out_specs=pl.BlockSpec((1, tr, W), lambda s, i, c: (s, i, 0))),
        out_shape=jax.ShapeDtypeStruct((4, hr, W), BF16), compiler_params=_cparams(("parallel", "parallel")),
    )(cidx, g, rcv)


def sum_chips(p, h, sc, name):
    _, hr, W = p.shape
    tr = _rows_tile(hr, W)

    def body(s_ref, c_ref, p_ref, h_ref, o_ref):
        s = s_ref[0]
        own = h_ref[0]
        f = lambda k: jnp.where(s == k, own, p_ref[k]).astype(F32)
        o_ref[0] = ((f(0) + f(1)) + f(2)) + f(3)

    return _pcall(
        body, name=name,
        grid_spec=pltpu.PrefetchScalarGridSpec(
            num_scalar_prefetch=2, grid=(hr // tr,),
            in_specs=[pl.BlockSpec((4, tr, W), lambda i, s, c: (0, i, 0)),
                      pl.BlockSpec((1, tr, W), lambda i, s, c: (s[0], i, 0))],
            out_specs=pl.BlockSpec((1, tr, W), lambda i, s, c: (c[0], i, 0))),
        out_shape=jax.ShapeDtypeStruct((2, hr, W), F32), compiler_params=_cparams(("parallel",)),
    )(sc[0], sc[1], p, h)


HI = lax.Precision.HIGHEST
BLOCK_BYTES = 3 * 512 * 1024


def _rows_tile(R, w, T=None):
    cands = [t for t in range(16, R + 1, 16) if R % t == 0 and (T is None or T % t == 0)]
    ok = [t for t in cands if t * w * 4 <= BLOCK_BYTES]
    return max(ok) if ok else min(cands)


def _gsum_exact(x):
    r = jnp.right_shift(lax.broadcasted_iota(jnp.int32, (LANE, LANE), 0), 6)
    c = jnp.right_shift(lax.broadcasted_iota(jnp.int32, (LANE, LANE), 1), 6)
    g = (r == c).astype(BF16)
    x1 = x.astype(BF16)
    r1 = x - x1.astype(F32)
    x2 = r1.astype(BF16)
    x3 = (r1 - x2.astype(F32)).astype(BF16)
    dot = lambda z: jnp.dot(z, g, preferred_element_type=F32)
    return (dot(x3) + dot(x2)) + dot(x1)


@jax.custom_vjp
def _gsum(x):
    return _gsum_exact(x)


_gsum.defvjp(lambda x: (_gsum_exact(x), None), lambda _, ct: (_gsum_exact(ct),))


def _rms(x, g):
    return x * lax.rsqrt(jnp.mean(x * x, axis=-1, keepdims=True) + NORM_EPS) * g


def _silu_mul(gate, up):
    return jax.nn.silu(gate) * up


def _shift(p, prev, mu):
    return p + mu * (prev - p)


def _prev_rows(p, above, tpos):
    first = lax.broadcasted_iota(jnp.int32, p.shape, 0) == 0
    prev = jnp.where(first, above[7:8, :], pltpu.roll(p, 1, 0))
    return jnp.where(tpos[:, :1] == 0.0, 0.0, prev)


def _shift_rows(p, above, tpos, mu):
    return _shift(p, _prev_rows(p, above, tpos), mu)


def _shift_rows_bwd(t_last, p, above, tpos, mu, ct, ct_below):
    dmu = jnp.sum(ct * (_prev_rows(p, above, tpos) - p), axis=0, keepdims=True)
    last = lax.broadcasted_iota(jnp.int32, p.shape, 0) == p.shape[0] - 1
    nxt = jnp.where(last, ct_below[0:1, :], pltpu.roll(ct, p.shape[0] - 1, 0))
    nxt = jnp.where(tpos[:, :1] == t_last, 0.0, nxt)
    return (1.0 - mu) * ct + mu * nxt, dmu


def _lora_act(p, prev, mu):
    s = _shift(p, prev, mu)
    return jax.nn.sigmoid(s[:, :G_LORA]), jnp.tanh(s[:, G_LORA:G_LORA + LANE]), s[:, G_LORA + LANE:]


def _prep(k, lw, la, w0, a0, kk_w, ka_w):
    wpre = -jax.nn.softplus(-(w0 + lw)) - 0.5
    decay = jnp.exp(-jnp.exp(wpre))
    a = jax.nn.sigmoid(a0 + la)
    kk = k * kk_w
    kk = kk * lax.rsqrt(jnp.maximum(_gsum(kk * kk), 1e-24))
    k2 = k * (1.0 + (a - 1.0) * ka_w)
    return decay, -kk, kk * a, k2


def _post(y, r, k2, v, g, gnw, gnb, rk):
    mean = _gsum(y) * (1.0 / RWKV_HEAD)
    d = y - mean
    var = _gsum(d * d) * (1.0 / RWKV_HEAD)
    yn = d * lax.rsqrt(var + GN_EPS) * gnw + gnb
    bonus = _gsum(r * k2 * rk) * v
    return (yn + bonus) * g


def _gate_mix(ga, gb, ya, o):
    return jax.nn.sigmoid(ga) * ya + jax.nn.sigmoid(gb) * o


def _rope(x, cos, sin):
    return x * cos + pltpu.roll(x, LANE // 2, 1) * sin


def _rope_t(dy, cos, sin):
    return dy * cos + pltpu.roll(dy * sin, LANE // 2, 1)


def _mla_pre(cq, ckv, kpe, cos, sin, qw, kvw):
    return _rms(cq, qw), _rms(ckv, kvw), _rope(kpe, cos, sin)


def _mla_pre_bwd(cq, ckv, dcqn, dckvn, dkr, cos, sin, qw, kvw):
    _, pull = jax.vjp(lambda a, b, c, d: (_rms(a, c), _rms(b, d)), cq, ckv, qw, kvw)
    dcq, dckv, dqw, dkvw = pull((dcqn, dckvn))
    return dcq, dckv, _rope_t(dkr, cos, sin), dqw, dkvw


WEIGHTS = ['meta_tokens', 'ffn1_norm', 'ffn1_w_gate', 'ffn1_w_up', 'ffn1_w_down', 'mix_norm', 'w_in', 'tm_mu', 'w0',
           'w_up', 'a0', 'a_up', 'g_up', 'k_k', 'k_a', 'r_k', 'gn_w', 'gn_b', 'q_norm', 'w_uq', 'kv_norm', 'w_ukv',
           'w_out', 'ffn2_norm', 'ffn2_w_gate', 'ffn2_w_up', 'ffn2_w_down', 'final_norm']
SHARD_AXIS = {'meta_tokens': 1, 'ffn1_w_gate': 1, 'ffn1_w_up': 1, 'ffn1_w_down': 0, 'w_in': 1, 'w_up': 1, 'a_up': 1,
              'g_up': 1, 'w_uq': 1, 'w_ukv': 1, 'w_out': 0, 'ffn2_w_gate': 1, 'ffn2_w_up': 1, 'ffn2_w_down': 0}
GATHERED = [n for n in WEIGHTS if n in SHARD_AXIS and n != 'meta_tokens']
FFN_IN = ('ffn1_w_gate', 'ffn1_w_up', 'ffn2_w_gate', 'ffn2_w_up')
PART_B = ['ffn2_w_gate', 'ffn2_w_up', 'ffn2_w_down', 'w_out']
PART_A1 = ['ffn1_w_gate', 'ffn1_w_up', 'ffn1_w_down']
PART_A2 = [n for n in GATHERED if n not in PART_B and n not in PART_A1]
SMALL = [n for n in WEIGHTS if n not in SHARD_AXIS]


def _to2d(a):
    if a.ndim == 1:
        return a.reshape(1, -1)
    if a.ndim == 3:
        return a.reshape(a.shape[0] * a.shape[1], a.shape[2]) if a.shape[0] == 1 and a.shape[1] > 64 else a.reshape(1, -1)
    return a


def _unpack(flat, shapes):
    out, off = [], 0
    for shp in shapes:
        n = shp[0] * shp[1]
        out.append(flat[off:off + n].reshape(shp))
        off += n
    return out


def _adamw(w, g, m, v):
    m = ADAM_B1 * m + (1.0 - ADAM_B1) * g
    v = ADAM_B2 * v + (1.0 - ADAM_B2) * jnp.square(g)
    m_hat = m / (1.0 - ADAM_B1 ** ADAM_STEP)
    v_hat = v / (1.0 - ADAM_B2 ** ADAM_STEP)
    delta = -ADAM_LR * (m_hat / (jnp.sqrt(v_hat) + ADAM_EPS) + ADAM_WD * w)
    return delta, m, v


def adamw(w, g, m, v, name):
    R, C = w.shape
    if R % 16 == 0 and R > 16:
        tm = _rows_tile(R, C)
    else:
        tm = R
    return rowwise(_adamw, [(w, C, 0), (g, C, 0), (m, C, 0), (v, C, 0)], [], [(C, C, F32)] * 3, tm=tm, name=name)


def _step(a):
    x = a['x']
    Bl, S, D = x.shape
    T = S + N_META
    R, RS = Bl * T, Bl * S
    Hr, Hm = D // RWKV_HEAD, D // LANE
    w2 = {n: _to2d(a[n]) for n in WEIGHTS}
    m2 = {n: _to2d(a['m_' + n]) for n in WEIGHTS}
    v2 = {n: _to2d(a['v_' + n]) for n in WEIGHTS}
    xi, yi, ci = lax.axis_index("x"), lax.axis_index("y"), lax.axis_index("c")
    chip = 2 * xi + yi

    i0 = jnp.zeros((), jnp.int32)
    chip32 = chip.astype(jnp.int32)

    def own_slot(w):
        return lax.dynamic_update_slice(lax.empty((4,) + w.shape, BF16), w.astype(BF16)[None], (chip32, i0, i0))

    stk = dict(zip(PART_A1, ag_weights([own_slot(w2[n]) for n in PART_A1])))

    def unstack(z, axis):
        return z.reshape(4 * z.shape[1], z.shape[2]) if axis == 0 else jnp.concatenate([z[s] for s in range(4)], axis=1)

    full = {n: unstack(stk[n], SHARD_AXIS[n]) for n in PART_A1 if n not in FFN_IN}
    mt = w2['meta_tokens']
    mcols = mt.shape[1]
    mt_z = lax.dynamic_update_slice(jnp.zeros((N_META, D), F32), 0.5 * mt, (jnp.zeros((), jnp.int32), (chip * mcols).astype(jnp.int32)))
    meta_full = allreduce8(mt_z.reshape(-1, LANE), "gather_meta").reshape(N_META, D)

    F = 4 * stk['ffn1_w_gate'].shape[2]
    half = ROPE_DIM // 2
    tmu = w2['tm_mu']
    mu_a = tmu[:, :3 * D]
    zm = lambda n: jnp.zeros((1, n), F32)
    mu_b = jnp.concatenate([tmu[:, 3 * D + 2 * W_LORA:], tmu[:, 3 * D:3 * D + W_LORA], zm(LANE - W_LORA),
                            tmu[:, 3 * D + W_LORA:3 * D + 2 * W_LORA], zm(LANE - A_LORA)], axis=1)
    pos = jnp.arange(T, dtype=F32)
    inv_freq = 1.0 / (ROPE_THETA ** (jnp.arange(0, ROPE_DIM, 2, dtype=F32) / ROPE_DIM))
    ang = pos[:, None] * inv_freq[None, :]
    zt = jnp.zeros((T, half), F32)
    cos_t = jnp.concatenate([jnp.cos(ang), zt, jnp.cos(ang), zt], axis=1)
    sin_t = jnp.concatenate([-jnp.sin(ang), zt, jnp.sin(ang), zt], axis=1)
    cos_q, sin_q = cos_t[N_META:], sin_t[N_META:]

    t_full = _rows_tile(R, D)
    t_512 = _rows_tile(R, 512, T)
    t_128 = _rows_tile(R, LANE, T)
    tq_full = _rows_tile(RS, D)
    tq_128 = _rows_tile(RS, LANE, S)

    def rows3(z):
        return z.reshape(Bl, T, z.shape[-1])

    def real_rows(z):
        return rows3(z)[:, N_META:].reshape(RS, z.shape[-1])

    def pad_meta(z):
        z3 = z.reshape(Bl, S, z.shape[-1])
        return jnp.concatenate([jnp.zeros((Bl, N_META, z.shape[-1]), z.dtype), z3], axis=1).reshape(R, z.shape[-1])

    def shift_down(z):
        z3 = rows3(z)
        return jnp.concatenate([jnp.zeros((Bl, 1, z.shape[-1]), z.dtype), z3[:, :-1]], axis=1).reshape(R, z.shape[-1])

    def shift_up(z):
        z3 = rows3(z)
        return jnp.concatenate([z3[:, 1:], jnp.zeros((Bl, 1, z.shape[-1]), z.dtype)], axis=1).reshape(R, z.shape[-1])

    def ffn_fwd(h, nw, wg, wu, wd, tag, ag_bufs=()):
        n = rowwise(_rms, [(h, D, 0)], [(nw, D, 0)], [(D, D, BF16)], tm=t_full, name=tag + "_norm")[0]
        res = mm_epi(n, [wg, wu], "nn", lambda g_, u_: (g_, u_, _silu_mul(g_, u_)), [], [F32, F32, BF16], b_stack=True,
                     carry=("ag", ag_bufs) if ag_bufs else None, name=tag + "_gate_up")
        (gate, up, act), got = res if ag_bufs else (res, [])
        out = mm(act, wd, res=h, alpha=0.5, name=tag + "_down")
        return (out, (h, n, gate, up, act), got) if ag_bufs else (out, (h, n, gate, up, act))

    def exch(g, name):
        return add_half(g, swap_halves([g], "swap_halves_" + name)[0], cidx, "add_half_" + name)

    def ffn_bwd(dh2, saved, nw, wg, wu, wd, tag, rs_hs=(), rs_own=()):
        h, n, gate, up, act = saved
        dz = rowwise(lambda d: 0.5 * d, [(dh2, 512, 0)], [], [(D, 512, BF16)], tm=t_512, ncb=D // 512, name=tag + "_dz")[0]
        d_wd = mm(act, dz, "tn", name=tag + "_dwd")
        res = mm_epi(dz, [wd], "nt", lambda da, g_, u_: vjp_fn(_silu_mul, 2)(g_, u_, da), [gate, up], [BF16, BF16],
                     carry=("rs", rs_hs) if rs_hs else None, name=tag + "_dact")
        (dgate, dup), got = res if rs_hs else (res, [])
        own = {}
        if rs_own:
            h_wd = exch(stacked(d_wd, 0), rs_own[2])
            d_wg, p_wd = mm(n, dgate, "tn", out_stack=True, carry=("rs", [h_wd]), name=tag + "_dwg")
            d_wu = mm(n, dup, "tn", out_stack=True, name=tag + "_dwu")
            h_wg = exch(d_wg, rs_own[0])
            dn, p_wg = mm(dgate, wg, "nt", b_stack=True, carry=("rs", [h_wg]), name=tag + "_dn1")
            h_wu = exch(d_wu, rs_own[1])
            dn, p_wu = mm(dup, wu, "nt", res=dn, b_stack=True, carry=("rs", [h_wu]), name=tag + "_dn2")
            own = {rs_own[0]: (p_wg[0], h_wg), rs_own[1]: (p_wu[0], h_wu), rs_own[2]: (p_wd[0], h_wd)}
        else:
            d_wg = mm(n, dgate, "tn", out_stack=True, name=tag + "_dwg")
            d_wu = mm(n, dup, "tn", out_stack=True, name=tag + "_dwu")
            dn = mm(dgate, wg, "nt", b_stack=True, name=tag + "_dn1")
            dn = mm(dup, wu, "nt", res=dn, b_stack=True, name=tag + "_dn2")

        def f(h_, dn_, dh_, nw_):
            dh, dnw = vjp_fn(_rms, 2)(h_, nw_, dn_)
            return dh + dh_, dnw

        dh, d_nw = rowwise(f, [(h, D, 0), (dn, D, 0), (dh2, D, 0)], [(nw, D, 0)], [(D, D, F32)], [(1, D, D)],
                           tm=t_full, name=tag + "_dnorm")
        return (dh, d_nw, d_wg, d_wu, d_wd, got, own) if (rs_hs or rs_own) else (dh, d_nw, d_wg, d_wu, d_wd)

    h0 = jnp.concatenate([jnp.broadcast_to(meta_full[None], (Bl, N_META, D)), x], axis=1).reshape(R, D)
    h1, sv1, got_a2 = ffn_fwd(h0, w2['ffn1_norm'], stk['ffn1_w_gate'], stk['ffn1_w_up'], full['ffn1_w_down'], "ffn1",
                              [own_slot(w2[n]) for n in PART_A2])
    stk.update(zip(PART_A2, got_a2))
    full.update({n: unstack(stk[n], SHARD_AXIS[n]) for n in PART_A2})
    win = full['w_in']
    o = 3 * D
    c_xw, c_xa, c_xg = win[:, o:o + W_LORA], win[:, o + W_LORA:o + 2 * W_LORA], win[:, o + 2 * W_LORA:o + 2 * W_LORA + G_LORA]
    o += 2 * W_LORA + G_LORA
    c_cq, c_ckv, c_kpe = win[:, o:o + Q_LORA], win[:, o + Q_LORA:o + Q_LORA + KV_LORA], win[:, o + Q_LORA + KV_LORA:o + Q_LORA + KV_LORA + ROPE_DIM]
    o += Q_LORA + KV_LORA + ROPE_DIM
    c_ga, c_gb = win[:, o:o + D], win[:, o + D:o + 2 * D]
    zc = lambda n: jnp.zeros((D, n), BF16)
    half = ROPE_DIM // 2
    NP0 = 5 * D + 512 + Q_LORA + KV_LORA + LANE
    NP = -(-NP0 // 512) * 512
    win_p = jnp.concatenate([win[:, :3 * D], c_ga, c_gb, c_xg, c_xw, zc(LANE - W_LORA), c_xa, zc(LANE - A_LORA), c_cq, c_ckv,
                             c_kpe[:, :half], zc(half), c_kpe[:, half:], zc(half), zc(NP - NP0)], axis=1)
    O_GA, O_GB, O_L, O_CQ, O_CKV, O_KPE = 3 * D, 4 * D, 5 * D, 5 * D + 512, 5 * D + 512 + Q_LORA, 5 * D + 512 + Q_LORA + KV_LORA
    zr = lambda n: jnp.zeros((n, D), BF16)
    w_up_p = jnp.concatenate([full['w_up'], zr(LANE - W_LORA)], axis=0)
    a_up_p = jnp.concatenate([full['a_up'], zr(LANE - A_LORA)], axis=0)
    g_up = full['g_up']
    wuq = full['w_uq'].reshape(Q_LORA, Hm, QK_DIM)
    zq = jnp.zeros((Q_LORA, Hm, half), BF16)
    wqn = wuq[:, :, :NOPE_DIM].reshape(Q_LORA, Hm * LANE)
    wqp = jnp.concatenate([wuq[:, :, NOPE_DIM:NOPE_DIM + half], zq, wuq[:, :, NOPE_DIM + half:], zq], axis=2).reshape(Q_LORA, Hm * LANE)
    wukv = full['w_ukv'].reshape(KV_LORA, Hm, NOPE_DIM + V_DIM)
    wkn = wukv[:, :, :NOPE_DIM].reshape(KV_LORA, Hm * LANE)
    wv = wukv[:, :, NOPE_DIM:].reshape(KV_LORA, Hm * LANE)
    u = rowwise(_rms, [(h1, D, 0)], [(w2['mix_norm'], D, 0)], [(D, D, BF16)], tm=t_full, name="mix_norm")[0]
    proj = mm(u, win_p, name="proj")
    tpos = jnp.broadcast_to(pos[:, None], (T, LANE))
    tpos_in = (tpos, LANE, 0, T // t_512, True)
    prev_b = shift_down(proj[:, O_L:O_L + 512])
    ps = rowwise(_shift_rows, [(proj, 512, 0), (proj, 512, 0, None, False, "prev"), tpos_in], [(mu_a, 512, 0)],
                 [(3 * D, 512, F32)], tm=t_512, ncb=3 * D // 512, name="shift_rkv")[0]
    sg, txw, xas = rowwise(_lora_act, [(proj, 512, O_L // 512), (prev_b, 512, 0)], [(mu_b, 512, 0)],
                           [(G_LORA, G_LORA, BF16), (LANE, LANE, BF16), (LANE, LANE, BF16)], tm=t_512, name="lora_act")
    lw = mm(txw, w_up_p, name="lora_w")
    la = mm(xas, a_up_p, name="lora_a")
    g = mm(sg, g_up, name="lora_g")
    hb = D // LANE
    par_d = lambda n: (w2[n], LANE, 0)
    decay, kn, bb, k2 = rowwise(_prep, [(ps, LANE, hb), (lw, LANE, 0), (la, LANE, 0)],
                                [par_d('w0'), par_d('a0'), par_d('k_k'), par_d('k_a')], [(D, LANE, F32)] * 4,
                                tm=t_128, ncb=hb, name="wkv_prep")

    def to_j(z):
        z = z.reshape(Bl, T, Hr, RWKV_HEAD).transpose(1, 3, 0, 2).reshape(T, RWKV_HEAD, Bl * Hr)
        return jnp.concatenate([z, z], axis=-1)

    def to_i(z):
        return z.reshape(Bl, T, Hr, 2, RWKV_HEAD // 2).transpose(1, 4, 3, 0, 2).reshape(T, RWKV_HEAD // 2, 2 * Bl * Hr)

    def from_i(z):
        return z.reshape(T, RWKV_HEAD // 2, 2, Bl, Hr).transpose(3, 0, 4, 2, 1).reshape(R, D)

    def from_j(z):
        return z.reshape(T, RWKV_HEAD, Bl, Hr).transpose(2, 0, 3, 1).reshape(R, D)

    r_s, v_s = ps[:, :D], ps[:, 2 * D:]
    jw, jkn, jb, jk, jr, iv = to_j(decay), to_j(kn), to_j(bb), to_j(k2), to_j(r_s), to_i(v_s)
    y_i, sp, sa_i, got_b = wkv_fwd(jw, jkn, jb, jk, jr, iv, [own_slot(w2[n]) for n in PART_B])
    stk.update(zip(PART_B, got_b))
    full.update({n: unstack(stk[n], SHARD_AXIS[n]) for n in PART_B if n not in FFN_IN})
    wout = full['w_out']
    y = from_i(y_i)
    post_rows = [(y, LANE, 0), (ps, LANE, 0), (k2, LANE, 0), (ps, LANE, 2 * hb), (g, LANE, 0)]
    post_pars = [par_d('gn_w'), par_d('gn_b'), par_d('r_k')]
    ya = rowwise(_post, post_rows, post_pars, [(D, LANE, F32)], tm=t_128, ncb=hb, name="wkv_post")[0]

    nt512 = T // t_512
    mla_rows = [(proj, Q_LORA, O_CQ // Q_LORA), (proj, KV_LORA, O_CKV // KV_LORA), (proj, LANE, O_KPE // LANE)]
    tabs = [(cos_t, LANE, 0, nt512, True), (sin_t, LANE, 0, nt512, True)]
    mla_pars = [(w2['q_norm'], Q_LORA, 0), (w2['kv_norm'], KV_LORA, 0)]
    cqn, ckvn, kpr = rowwise(_mla_pre, mla_rows + tabs, mla_pars,
                             [(Q_LORA, Q_LORA, BF16), (KV_LORA, KV_LORA, BF16), (LANE, LANE, BF16)], tm=t_512, name="mla_pre")
    cqn_r = real_rows(cqn)
    qn = mm(cqn_r, wqn, out_dtype=BF16, name="q_nope")
    qp_raw = mm(cqn_r, wqp, name="q_pe")
    ntq = S // tq_128
    qtabs = [(cos_q, LANE, 0, ntq, True), (sin_q, LANE, 0, ntq, True)]
    qp = rowwise(_rope, [(qp_raw, LANE, 0)] + qtabs, [], [(D, LANE, BF16)], tm=tq_128, ncb=Hm, name="q_rope")[0]
    knope = mm(ckvn, wkn, out_dtype=BF16, name="k_nope")
    vv = mm(ckvn, wv, out_dtype=BF16, name="v_proj")

    def pad_keys(z):
        z3 = rows3(z)
        return jnp.concatenate([z3[:, :N_META], jnp.zeros((Bl, Q_BLOCK - N_META, z.shape[-1]), z.dtype), z3[:, N_META:]], axis=1)

    def unpad_keys(z):
        return jnp.concatenate([z[:, :N_META], z[:, Q_BLOCK:]], axis=1).reshape(R, z.shape[-1])

    qn3, qp3 = qn.reshape(Bl, S, D), qp.reshape(Bl, S, D)
    knp, kpp, vp = pad_keys(knope), pad_keys(kpr), pad_keys(vv)
    o3, _ = attn_fwd(qn3, qp3, knp, kpp, vp)
    o_att = pad_meta(o3.reshape(RS, D))
    mix_rows = [(proj, 512, O_GA // 512), (proj, 512, O_GB // 512), (ya, 512, 0), (o_att, 512, 0)]
    mix = rowwise(_gate_mix, mix_rows, [], [(D, 512, BF16)], tm=t_512, ncb=D // 512, name="gate_mix")[0]
    h2 = mm(mix, wout, res=h1, name="w_out")
    h3, sv2 = ffn_fwd(h2, w2['ffn2_norm'], stk['ffn2_w_gate'], stk['ffn2_w_up'], full['ffn2_w_down'], "ffn2")

    def loss_fb(h_, tgt, fw):
        yv, pull = jax.vjp(_rms, h_, fw)
        e = yv - tgt
        dh, dfw = pull(e * (1.0 / D))
        return dh, jnp.full((1, LANE), 0.5 / D * jnp.sum(e * e), F32), dfw

    dh3r, lossp, g_final = rowwise(loss_fb, [(real_rows(h3), D, 0), (a['loss_target'].reshape(RS, D), D, 0)],
                                   [(w2['final_norm'], D, 0)], [(D, D, F32)], [(1, LANE, LANE), (1, D, D)],
                                   tm=tq_full, name="loss")
    dh3 = pad_meta(dh3r)

    def stacked(g, axis):
        if g.ndim == 3:
            return g
        if axis == 0:
            return g.reshape(4, g.shape[0] // 4, g.shape[1])
        return g.reshape(g.shape[0], 4, g.shape[1] // 4).transpose(1, 0, 2)

    cidx = ci.reshape(1).astype(jnp.int32)
    gr = {'final_norm': g_final}
    dh2, gr['ffn2_norm'], gr['ffn2_w_gate'], gr['ffn2_w_up'], gr['ffn2_w_down'] = ffn_bwd(
        dh3, sv2, w2['ffn2_norm'], stk['ffn2_w_gate'], stk['ffn2_w_up'], full['ffn2_w_down'], "ffn2")
    dh2b = rowwise(lambda d: d, [(dh2, 512, 0)], [], [(D, 512, BF16)], tm=t_512, ncb=D // 512, name="dh2_cast")[0]
    gr['w_out'] = mm(mix, dh2b, "tn", name="d_wout")
    dmix = mm(dh2b, wout, "nt", name="d_mix")
    dga, dgb, dya, do = rowwise(vjp_fn(_gate_mix, 4), mix_rows + [(dmix, 512, 0)], [],
                                [(D, 512, BF16), (D, 512, BF16), (D, 512, F32), (D, 512, BF16)], tm=t_512, ncb=D // 512,
                                name="d_gate_mix")
    gs_b = [stacked(gr[n], SHARD_AXIS[n]) for n in PART_B]
    hs_b = [add_half(g_, r_, cidx, "add_half_" + n) for n, g_, r_ in zip(PART_B, gs_b, swap_halves(gs_b, "swap_halves_b"))]
    (dqn, dqp, dknp, dvp, dkpp), ps_b = attn_bwd(qn3, qp3, knp, kpp, vp, real_rows(do).reshape(Bl, S, D), hs_b)
    dqn2 = dqn.reshape(RS, D)
    dqp_raw = rowwise(_rope_t, [(dqp.reshape(RS, D), LANE, 0)] + qtabs, [], [(D, LANE, BF16)], tm=tq_128, ncb=Hm,
                      name="d_q_rope")[0]
    d_wqn = mm(cqn_r, dqn2, "tn", name="d_wqn")
    d_wqp = mm(cqn_r, dqp_raw, "tn", name="d_wqp")
    dcqn = mm(dqn2, wqn, "nt", name="d_cqn1")
    dcqn = pad_meta(mm(dqp_raw, wqp, "nt", res=dcqn, name="d_cqn2"))
    dkn2, dv2, dkp2 = unpad_keys(dknp), unpad_keys(dvp), unpad_keys(dkpp)
    d_wkn = mm(ckvn, dkn2, "tn", name="d_wkn")
    d_wv = mm(ckvn, dv2, "tn", name="d_wv")
    dckvn = mm(dkn2, wkn, "nt", name="d_ckvn1")
    dckvn = mm(dv2, wv, "nt", res=dckvn, name="d_ckvn2")
    dcq, dckv, dkpe, gr['q_norm'], gr['kv_norm'] = rowwise(
        _mla_pre_bwd, mla_rows[:2] + [(dcqn, Q_LORA, 0), (dckvn, KV_LORA, 0), (dkp2, LANE, 0)] + tabs, mla_pars,
        [(Q_LORA, Q_LORA, BF16), (KV_LORA, KV_LORA, BF16), (LANE, LANE, BF16)], [(1, Q_LORA, Q_LORA), (1, KV_LORA, KV_LORA)],
        tm=t_512, name="d_mla_pre")

    def post_bwd(y_, r_, k2_, v_, g_, dya_, gnw, gnb, rk):
        return vjp_fn(_post, 8)(y_, r_, k2_, v_, g_, gnw, gnb, rk, dya_)

    dy, dr_b, dk2_b, dv_b, dg, gr['gn_w'], gr['gn_b'], gr['r_k'] = rowwise(
        post_bwd, post_rows + [(dya, LANE, 0)], post_pars,
        [(D, LANE, F32)] * 4 + [(D, LANE, BF16)], [(1, D, LANE)] * 3, tm=t_128, ncb=hb, name="d_wkv_post")
    jdw, jdkn, jdb, jdk, jdr, idv = wkv_bwd(jw, jkn, jb, jk, jr, iv, to_i(dy), sp, sa_i)
    ddecay, dkn_w, db_w, dk2_w, dr_w, dv_w = from_j(jdw), from_j(jdkn), from_j(jdb), from_j(jdk), from_j(jdr), from_i(idv)

    def prep_bwd(k_, lw_, la_, dd, dkn_, db_, dk2a, dk2b, w0, a0, kkw, kaw):
        return vjp_fn(_prep, 7)(k_, lw_, la_, w0, a0, kkw, kaw, dd, dkn_, db_, dk2a + dk2b)

    dk_s, dlw, dla, gr['w0'], gr['a0'], gr['k_k'], gr['k_a'] = rowwise(
        prep_bwd, [(ps, LANE, hb), (lw, LANE, 0), (la, LANE, 0), (ddecay, LANE, 0), (dkn_w, LANE, 0), (db_w, LANE, 0),
                   (dk2_w, LANE, 0), (dk2_b, LANE, 0)],
        [par_d('w0'), par_d('a0'), par_d('k_k'), par_d('k_a')], [(D, LANE, F32), (D, LANE, BF16), (D, LANE, BF16)],
        [(1, D, LANE)] * 4, tm=t_128, ncb=hb, name="d_wkv_prep")
    d_wup = mm(txw, dlw, "tn", name="d_wup")
    dtxw = mm(dlw, w_up_p, "nt", name="d_txw")
    d_aup = mm(xas, dla, "tn", name="d_aup")
    dxa = mm(dla, a_up_p, "nt", name="d_xa")
    gr['g_up'] = mm(sg, dg, "tn", name="d_gup")
    dsg = mm(dg, g_up, "nt", name="d_sg")

    def lora_bwd(p_, prev_, dsg_, dt_, dxa_, mu_):
        return vjp_fn(_lora_act, 3)(p_, prev_, mu_, dsg_, dt_, dxa_)

    dpb, dprevb, dmu_b = rowwise(lora_bwd, [(proj, 512, O_L // 512), (prev_b, 512, 0), (dsg, G_LORA, 0), (dtxw, LANE, 0),
                                            (dxa, LANE, 0)], [(mu_b, 512, 0)], [(512, 512, F32)] * 2, [(1, 512, 512)],
                                 tm=t_512, name="d_lora_act")

    def shift_back(sec, cts, tag):
        nb = D // 512

        def f(p_, above, tp, *rest):
            cs, mu_ = rest[:-1], rest[-1]
            ct = cs[0] if len(cts) == 1 else cs[0] + cs[2]
            below = cs[1] if len(cts) == 1 else cs[1] + cs[3]
            return _shift_rows_bwd(float(T - 1), p_, above, tp, mu_, ct, below)

        ct_ins = [z for c in cts for z in ((c, 512, 0), (c, 512, 0, None, False, "next"))]
        return rowwise(f, [(proj, 512, sec * nb), (proj, 512, sec * nb, None, False, "prev"), tpos_in] + ct_ins,
                       [(mu_a, 512, sec * nb)], [(D, 512, BF16)], [(1, D, 512)], tm=t_512, ncb=nb, name="d_shift_" + tag)

    dsec_r, dmu_r = shift_back(0, [dr_w, dr_b], "r")
    dsec_k, dmu_k = shift_back(1, [dk_s], "k")
    dsec_v, dmu_v = shift_back(2, [dv_w, dv_b], "v")

    def add_cast(p_, q_):
        return p_ + q_

    def dsec(dp_, dprev_, tag):
        C = dp_.shape[1]
        return rowwise(add_cast, [(dp_, 512, 0), (shift_up(dprev_), 512, 0)], [], [(C, 512, BF16)], tm=t_512, ncb=C // 512,
                       name="d_sec_" + tag)[0]

    zpad = jnp.zeros((R, NP - NP0), BF16)
    dproj = jnp.concatenate([dsec_r, dsec_k, dsec_v, dga, dgb,
                             dsec(dpb, dprevb, "l"), dcq, dckv, dkpe, zpad], axis=1)
    d_win_p = mm(u, dproj, "tn", name="d_win")
    du = mm(dproj, win_p, "nt", name="d_u")

    def norm_bwd(h_, dn_, dh_, nw_):
        dh, dnw = vjp_fn(_rms, 2)(h_, nw_, dn_)
        return dh + dh_, dnw

    dh1, gr['mix_norm'] = rowwise(norm_bwd, [(h1, D, 0), (du, D, 0), (dh2, D, 0)], [(w2['mix_norm'], D, 0)],
                                  [(D, D, F32)], [(1, D, D)], tm=t_full, name="d_mix_norm")
    gr['w_in'] = jnp.concatenate([
        d_win_p[:, :3 * D], d_win_p[:, O_L + G_LORA:O_L + G_LORA + W_LORA], d_win_p[:, O_L + G_LORA + LANE:O_L + G_LORA + LANE + A_LORA],
        d_win_p[:, O_L:O_L + G_LORA], d_win_p[:, O_CQ:O_CQ + Q_LORA], d_win_p[:, O_CKV:O_CKV + KV_LORA],
        d_win_p[:, O_KPE:O_KPE + half], d_win_p[:, O_KPE + 2 * half:O_KPE + 3 * half], d_win_p[:, O_GA:O_GA + 2 * D]], axis=1)
    gr['tm_mu'] = jnp.concatenate([dmu_r, dmu_k, dmu_v, dmu_b[:, G_LORA:G_LORA + W_LORA],
                                   dmu_b[:, G_LORA + LANE:G_LORA + LANE + A_LORA], dmu_b[:, :G_LORA]], axis=1)
    gr['w_up'], gr['a_up'] = d_wup[:W_LORA], d_aup[:A_LORA]
    dq3n, dq3p = d_wqn.reshape(Q_LORA, Hm, LANE), d_wqp.reshape(Q_LORA, Hm, LANE)
    gr['w_uq'] = jnp.concatenate([dq3n, dq3p[:, :, :half], dq3p[:, :, 2 * half:3 * half]], axis=2).reshape(Q_LORA, Hm * QK_DIM)
    gr['w_ukv'] = jnp.concatenate([d_wkn.reshape(KV_LORA, Hm, LANE), d_wv.reshape(KV_LORA, Hm, LANE)], axis=2).reshape(
        KV_LORA, Hm * (NOPE_DIM + V_DIM))
    gs_a2 = [stacked(gr[n], SHARD_AXIS[n]) for n in PART_A2]
    hs_a2 = [add_half(g_, r_, cidx, "add_half_" + n) for n, g_, r_ in zip(PART_A2, gs_a2, swap_halves(gs_a2, "swap_halves_a2"))]
    dh0, gr['ffn1_norm'], gr['ffn1_w_gate'], gr['ffn1_w_up'], gr['ffn1_w_down'], ps_a2, own_a1 = ffn_bwd(
        dh1, sv1, w2['ffn1_norm'], stk['ffn1_w_gate'], stk['ffn1_w_up'], full['ffn1_w_down'], "ffn1", hs_a2, PART_A1)
    dh0_3 = rows3(dh0)
    grad_x = dh0_3[:, N_META:]
    gr['meta_tokens'] = jnp.sum(dh0_3[:, :N_META], axis=0)

    ps_a1, hs_a1 = [own_a1[n][0] for n in PART_A1], [own_a1[n][1] for n in PART_A1]
    sc = (chip32.reshape(1), cidx)
    ts = [sum_chips(p, h, sc, "sum_chips_" + n)
          for n, p, h in zip(PART_A1 + PART_A2 + PART_B, ps_a1 + ps_a2 + ps_b, hs_a1 + hs_a2 + hs_b)]
    g_shard = {n: z.reshape(w2[n].shape) for n, z in zip(PART_A1 + PART_A2 + PART_B, share_sibling(ts))}
    small = jnp.concatenate([gr[n].reshape(-1) for n in SMALL] + [gr['meta_tokens'].reshape(-1), lossp.reshape(-1)])
    ns = small.shape[0]
    nsp = -(-ns // (8 * LANE)) * 8 * LANE
    small_sum = allreduce8(jnp.pad(small, (0, nsp - ns)).reshape(-1, LANE), "allreduce_small").reshape(-1)
    g_small = dict(zip(SMALL + ['meta_full'], _unpack(small_sum, [w2[n].shape for n in SMALL] + [(N_META, D)])))
    g_shard['meta_tokens'] = lax.dynamic_slice(
        g_small['meta_full'], (jnp.zeros((), jnp.int32), (chip * mcols).astype(jnp.int32)), (N_META, mcols))
    loss = small_sum[ns - LANE]

    grads, deltas, new_m, new_v = [], [], [], []
    for n in WEIGHTS:
        gw = g_shard[n] if n in g_shard else g_small[n]
        d_, m_, v_ = adamw(w2[n], gw, m2[n], v2[n], "adamw_" + n)
        shp = a[n].shape
        grads.append(gw.reshape(shp))
        deltas.append(d_.reshape(shp))
        new_m.append(m_.reshape(shp))
        new_v.append(v_.reshape(shp))
    return (loss, grad_x, *grads, *deltas, *new_m, *new_v)


def kernel(x, meta_tokens, ffn1_norm, ffn1_w_gate, ffn1_w_up, ffn1_w_down, mix_norm, w_in, tm_mu, w0, w_up, a0, a_up, g_up, k_k, k_a, r_k, gn_w, gn_b, q_norm, w_uq, kv_norm, w_ukv, w_out, ffn2_norm, ffn2_w_gate, ffn2_w_up, ffn2_w_down, final_norm, loss_target, m_meta_tokens, m_ffn1_norm, m_ffn1_w_gate, m_ffn1_w_up, m_ffn1_w_down, m_mix_norm, m_w_in, m_tm_mu, m_w0, m_w_up, m_a0, m_a_up, m_g_up, m_k_k, m_k_a, m_r_k, m_gn_w, m_gn_b, m_q_norm, m_w_uq, m_kv_norm, m_w_ukv, m_w_out, m_ffn2_norm, m_ffn2_w_gate, m_ffn2_w_up, m_ffn2_w_down, m_final_norm, v_meta_tokens, v_ffn1_norm, v_ffn1_w_gate, v_ffn1_w_up, v_ffn1_w_down, v_mix_norm, v_w_in, v_tm_mu, v_w0, v_w_up, v_a0, v_a_up, v_g_up, v_k_k, v_k_a, v_r_k, v_gn_w, v_gn_b, v_q_norm, v_w_uq, v_kv_norm, v_w_ukv, v_w_out, v_ffn2_norm, v_ffn2_w_gate, v_ffn2_w_up, v_ffn2_w_down, v_final_norm):
    return _step(dict(locals()))
```

```python
import functools
import math

import jax
import jax.numpy as jnp
import numpy as np
from jax import lax
from jax.experimental import pallas as pl
from jax.experimental.pallas import tpu as pltpu

F32 = jnp.float32
BF16 = jnp.bfloat16
MESH = pl.DeviceIdType.MESH

N_META = 16
NORM_EPS = 1e-6
RWKV_HEAD = 64
GN_EPS = RWKV_HEAD * 1e-5
W_LORA, A_LORA, G_LORA = 96, 96, 256
Q_LORA, KV_LORA = 512, 512
NOPE_DIM, ROPE_DIM, V_DIM = 128, 64, 128
QK_DIM = NOPE_DIM + ROPE_DIM
ROPE_THETA = 10000.0
Q_BLOCK = 128
ADAM_LR, ADAM_B1, ADAM_B2, ADAM_EPS, ADAM_WD, ADAM_STEP = 0.001, 0.9, 0.999, 1e-08, 0.01, 10

LANE = 128
VMEM_LIMIT = 56 * 1024 * 1024


def _pcall(body, **kw):
    return pl.pallas_call(body, **kw)


def _cparams(sem):
    return pltpu.CompilerParams(dimension_semantics=sem, vmem_limit_bytes=VMEM_LIMIT)


MM_LANE_TILE = 1536
MM_ROW_TILE = 1408


def _div_tile(n, cap, unit):
    best = None
    for t in range(unit, min(n, cap) + 1, unit):
        if n % t == 0:
            best = t
    return best if best is not None else n


def _rtile(n, pref=512):
    best = None
    for t in range(16, min(n, pref * 2) + 1, 16):
        if n % t == 0 and (best is None or abs(t - pref) < abs(best - pref)):
            best = t
    return best if best is not None else n


def mm(a, b, mode="nn", out_dtype=F32, res=None, alpha=1.0, b_stack=False, out_stack=False, carry=None, name="mm"):
    kind, carried = carry if carry else (None, ())
    nc = len(carried)
    cs = b.shape[-1] if b_stack else None
    bs = (b.shape[1], 4 * cs) if b_stack else b.shape
    if mode == "nn":
        (M, K), (K2, N) = a.shape, bs
    elif mode == "nt":
        (M, K), (N, K2) = a.shape, bs
    else:
        (K, M), (K2, N) = a.shape, bs
    assert K == K2, (a.shape, b.shape, mode)
    if mode == "tn":
        tm, tk = _div_tile(M, MM_ROW_TILE, LANE), _div_tile(K, 1024, 16)
    else:
        tm = _div_tile(M, MM_ROW_TILE, 16)
        tk = _div_tile(cs if (b_stack and mode == "nt") else K, MM_LANE_TILE, LANE)
    ncol = N // 4 if out_stack else (cs if (b_stack and mode == "nn") else N)
    tn = _div_tile(ncol, MM_LANE_TILE if mode != "nt" else 1024, LANE)
    nk = K // tk
    dims = {"nn": (((1,), (0,)), ((), ())), "nt": (((1,), (1,)), ((), ())), "tn": (((0,), (0,)), ((), ()))}[mode]
    direct = out_dtype == F32

    def body(*refs):
        a_ref, b_ref = refs[:2]
        r_ref = refs[2] if res is not None else None
        nin = 3 if res is not None else 2
        o_ref = refs[nin + nc]
        acc = o_ref if direct else refs[nin + 2 * nc + 1]
        k = pl.program_id(2)
        if nc:
            steps = _ag_steps if kind == "ag" else _rs_steps
            start, finish = steps(refs[nin:nin + nc], refs[nin + nc + 1:nin + 2 * nc + 1],
                                  *refs[nin + 2 * nc + (1 if direct else 2):])
            i, j = pl.program_id(0), pl.program_id(1)
            pl.when((i == 0) & (j == 0) & (k == 0))(start)

        @pl.when(k == 0)
        def _():
            acc[...] = jnp.zeros_like(acc) if res is None else r_ref[...].astype(F32)

        p = lax.dot_general(a_ref[...].astype(BF16), b_ref[...].astype(BF16), dims, preferred_element_type=F32)
        acc[...] += p if alpha == 1.0 else alpha * p

        if not direct:
            @pl.when(k == nk - 1)
            def _():
                o_ref[...] = acc[...].astype(o_ref.dtype)

        if nc:
            pl.when((i == M // tm - 1) & (j == N // tn - 1) & (k == nk - 1))(finish)

    if mode == "tn":
        a_spec = pl.BlockSpec((tk, tm), lambda i, j, k: (k, i))
        b_spec = pl.BlockSpec((tk, tn), lambda i, j, k: (k, j))
    else:
        a_spec = pl.BlockSpec((tm, tk), lambda i, j, k: (i, k))
        if mode == "nn":
            if b_stack:
                nps = cs // tn
                b_spec = pl.BlockSpec((None, tk, tn), lambda i, j, k: (j // nps, k, j % nps))
            else:
                b_spec = pl.BlockSpec((tk, tn), lambda i, j, k: (k, j))
        elif b_stack:
            kps = cs // tk
            b_spec = pl.BlockSpec((None, tn, tk), lambda i, j, k: (k // kps, j, k % kps))
        else:
            b_spec = pl.BlockSpec((tn, tk), lambda i, j, k: (j, k))
    r_spec = pl.BlockSpec((tm, tn), lambda i, j, k: (i, j))
    if out_stack:
        ops = (N // 4) // tn
        o_spec = pl.BlockSpec((None, tm, tn), lambda i, j, k: (j // ops, i, j % ops))
        o_shape = jax.ShapeDtypeStruct((4, M, N // 4), out_dtype)
    else:
        o_spec, o_shape = r_spec, jax.ShapeDtypeStruct((M, N), out_dtype)
    in_specs = [a_spec, b_spec] + ([r_spec] if res is not None else [])
    args = (a, b) + ((res,) if res is not None else ())
    if not nc:
        return _pcall(
            body, name=name, grid=(M // tm, N // tn, nk), in_specs=in_specs, out_specs=o_spec, out_shape=o_shape,
            scratch_shapes=[] if direct else [pltpu.VMEM((tm, tn), F32)],
            compiler_params=_cparams(("parallel", "parallel", "arbitrary")),
        )(*args)
    out = _pcall(
        body, name=name, grid=(M // tm, N // tn, nk), in_specs=in_specs + [_HBM] * nc, out_specs=[o_spec] + [_HBM] * nc,
        out_shape=[o_shape] + [jax.ShapeDtypeStruct(z.shape, z.dtype) for z in carried],
        input_output_aliases={len(args) + q: 1 + q for q in range(nc)} if kind == "ag" else {},
        scratch_shapes=([] if direct else [pltpu.VMEM((tm, tn), F32)]) + (_ag_sems(nc) if kind == "ag" else _rs_sems(nc)),
        compiler_params=_cparams(("arbitrary",) * 3),
    )(*args, *carried)
    return out[0], list(out[1:])


MM_EPI_ROW_TILE = 704


def mm_epi(a, bs, mode, epi, extras, out_dtypes, b_stack=False, carry=None, name="mm_epi"):
    kind, carried = carry if carry else (None, ())
    nc = len(carried)
    b = bs[0]
    cs = b.shape[-1] if b_stack else None
    bshape = (b.shape[1], 4 * cs) if b_stack else b.shape
    (M, K) = a.shape
    (K2, N) = bshape if mode == "nn" else bshape[::-1]
    assert K == K2 and mode in ("nn", "nt"), (a.shape, b.shape, mode)
    tm = _div_tile(M, MM_EPI_ROW_TILE, 16)
    tk = _div_tile(cs if (b_stack and mode == "nt") else K, MM_LANE_TILE, LANE)
    tn = _div_tile(cs if (b_stack and mode == "nn") else N, MM_LANE_TILE, LANE)
    nk, nb, ne, no = K // tk, len(bs), len(extras), len(out_dtypes)
    dims = (((1,), (0,)), ((), ())) if mode == "nn" else (((1,), (1,)), ((), ()))

    def body(*refs):
        a_ref, b_refs, e_refs = refs[0], refs[1:1 + nb], refs[1 + nb:1 + nb + ne]
        base = 1 + nb + ne + nc
        o_refs, accs = refs[base:base + no], refs[base + no + nc:base + no + nc + nb]
        k = pl.program_id(2)
        if nc:
            steps = _ag_steps if kind == "ag" else _rs_steps
            start, finish = steps(refs[base - nc:base], refs[base + no:base + no + nc], *refs[base + no + nc + nb:])
            i, j = pl.program_id(0), pl.program_id(1)
            pl.when((i == 0) & (j == 0) & (k == 0))(start)

        @pl.when(k == 0)
        def _():
            for acc in accs:
                acc[...] = jnp.zeros_like(acc)

        av = a_ref[...].astype(BF16)
        for b_ref, acc in zip(b_refs, accs):
            acc[...] += lax.dot_general(av, b_ref[...].astype(BF16), dims, preferred_element_type=F32)

        @pl.when(k == nk - 1)
        def _():
            res = epi(*[acc[...] for acc in accs], *[e[...] for e in e_refs])
            for o_ref, v in zip(o_refs, res):
                o_ref[...] = v.astype(o_ref.dtype)

        if nc:
            pl.when((i == M // tm - 1) & (j == N // tn - 1) & (k == nk - 1))(finish)

    a_spec = pl.BlockSpec((tm, tk), lambda i, j, k: (i, k))
    if mode == "nn":
        if b_stack:
            nps = cs // tn
            b_spec = pl.BlockSpec((None, tk, tn), lambda i, j, k: (j // nps, k, j % nps))
        else:
            b_spec = pl.BlockSpec((tk, tn), lambda i, j, k: (k, j))
    elif b_stack:
        kps = cs // tk
        b_spec = pl.BlockSpec((None, tn, tk), lambda i, j, k: (k // kps, j, k % kps))
    else:
        b_spec = pl.BlockSpec((tn, tk), lambda i, j, k: (j, k))
    o_spec = pl.BlockSpec((tm, tn), lambda i, j, k: (i, j))
    sems = (_ag_sems(nc) if kind == "ag" else _rs_sems(nc)) if nc else []
    res = _pcall(
        body, name=name, grid=(M // tm, N // tn, nk), in_specs=[a_spec] + [b_spec] * nb + [o_spec] * ne + [_HBM] * nc,
        out_specs=[o_spec] * no + [_HBM] * nc,
        out_shape=[jax.ShapeDtypeStruct((M, N), dt) for dt in out_dtypes]
        + [jax.ShapeDtypeStruct(z.shape, z.dtype) for z in carried],
        input_output_aliases={1 + nb + ne + q: no + q for q in range(nc)} if kind == "ag" else {},
        scratch_shapes=[pltpu.VMEM((tm, tn), F32)] * nb + sems,
        compiler_params=_cparams(("arbitrary",) * 3 if nc else ("parallel", "parallel", "arbitrary")),
    )(a, *bs, *extras, *carried)
    return (list(res[:no]), list(res[no:])) if nc else res


def rowwise(fn, row_ins, par_ins, outs, accs=(), *, tm, ncb=1, name="rowwise"):
    R = row_ins[0][0].shape[0]
    assert R % tm == 0, (R, tm)
    nrb = R // tm
    in_specs, args = [], []
    for spec in row_ins:
        arr, w, base = spec[:3]
        mod = spec[3] if len(spec) > 3 else None
        cstep = 0 if (len(spec) > 4 and spec[4]) else 1
        halo = spec[5] if len(spec) > 5 else None
        if halo == "prev":
            in_specs.append(pl.BlockSpec((8, w), lambda j, i, base=base: (jnp.maximum(i * (tm // 8) - 1, 0), base + j)))
        elif halo == "next":
            in_specs.append(pl.BlockSpec((8, w), lambda j, i, base=base: (jnp.minimum((i + 1) * (tm // 8), R // 8 - 1), base + j)))
        elif mod is None:
            in_specs.append(pl.BlockSpec((tm, w), lambda j, i, base=base, cstep=cstep: (i, base + cstep * j)))
        else:
            in_specs.append(pl.BlockSpec((tm, w), lambda j, i, base=base, mod=mod, cstep=cstep: (i % mod, base + cstep * j)))
        args.append(arr)
    for arr, w, base in par_ins:
        in_specs.append(pl.BlockSpec((arr.shape[0], w), lambda j, i, base=base: (0, base + j)))
        args.append(arr)
    out_specs, out_shape = [], []
    for cols, w, dt in outs:
        out_specs.append(pl.BlockSpec((tm, w), lambda j, i: (i, j)))
        out_shape.append(jax.ShapeDtypeStruct((R, cols), dt))
    for p, cols, w in accs:
        out_specs.append(pl.BlockSpec((p, w), lambda j, i: (0, j)))
        out_shape.append(jax.ShapeDtypeStruct((p, cols), F32))
    nin, nout, nacc = len(args), len(outs), len(accs)

    def body(*refs):
        vals = [r[...] for r in refs[:nin]]
        res = fn(*vals)
        if not isinstance(res, (tuple, list)):
            res = (res,)
        assert len(res) == nout + nacc, (len(res), nout, nacc)
        for o_ref, v in zip(refs[nin:nin + nout], res[:nout]):
            o_ref[...] = v.astype(o_ref.dtype)
        if nacc:
            i = pl.program_id(1)

            @pl.when(i == 0)
            def _():
                for a_ref in refs[nin + nout:]:
                    a_ref[...] = jnp.zeros_like(a_ref)

            for a_ref, v in zip(refs[nin + nout:], res[nout:]):
                a_ref[...] += v.astype(F32)

    r = _pcall(
        body, name=name, grid=(ncb, nrb), in_specs=in_specs, out_specs=out_specs, out_shape=out_shape,
        compiler_params=_cparams(("parallel", "arbitrary")),
    )(*args)
    return r


def vjp_fn(fwd, nprim):
    def f(*vals):
        prim, cts = vals[:nprim], vals[nprim:]
        out, pull = jax.vjp(fwd, *[p.astype(F32) for p in prim])
        if not isinstance(out, (tuple, list)):
            cts = cts[0].astype(F32)
        else:
            cts = tuple(c.astype(F32) for c in cts)
        return pull(cts)
    return f


WKV_TB = 16


def wkv_fwd(w, kn, b, k, r, v, bufs=()):
    T, N, L = w.shape
    NI = v.shape[1]
    tb = WKV_TB
    n = len(bufs)
    assert T % tb == 0 and L == v.shape[2]

    def body(*refs):
        w_ref, kn_ref, b_ref, k_ref, r_ref, v_ref = refs[:6]
        y_ref, sp_ref, sa_ref = refs[6 + n:9 + n]
        s_ref = refs[9 + 2 * n]
        if n:
            start, finish = _ag_steps(refs[6:6 + n], refs[9 + n:9 + 2 * n], *refs[10 + 2 * n:])
            pl.when(pl.program_id(0) == 0)(start)

        @pl.when(pl.program_id(0) == 0)
        def _():
            s_ref[...] = jnp.zeros_like(s_ref)

        def step(s, carry):
            W, KN, B, Kk, Rr = w_ref[s], kn_ref[s], b_ref[s], k_ref[s], r_ref[s]
            for i in range(NI):
                S = s_ref[i]
                sp_ref[s, i] = S
                sa = jnp.sum(S * KN, axis=0, keepdims=True)
                sa_ref[s, pl.ds(i, 1), :] = sa
                vi = v_ref[s, pl.ds(i, 1), :]
                Sn = S * W + sa * B + vi * Kk
                s_ref[i] = Sn
                y_ref[s, pl.ds(i, 1), :] = jnp.sum(Sn * Rr, axis=0, keepdims=True)
            return carry

        lax.fori_loop(0, tb, step, 0)
        if n:
            pl.when(pl.program_id(0) == T // tb - 1)(finish)

    jspec = pl.BlockSpec((tb, N, L), lambda t: (t, 0, 0))
    ispec = pl.BlockSpec((tb, NI, L), lambda t: (t, 0, 0))
    ishape = jax.ShapeDtypeStruct((T, NI, L), F32)
    res = _pcall(
        body, name="wkv_fwd", grid=(T // tb,), in_specs=[jspec] * 5 + [ispec] + [_HBM] * n,
        out_specs=[ispec, pl.BlockSpec((tb, NI, N, L), lambda t: (t, 0, 0, 0)), ispec] + [_HBM] * n,
        out_shape=[ishape, jax.ShapeDtypeStruct((T, NI, N, L), F32), ishape]
        + [jax.ShapeDtypeStruct(z.shape, z.dtype) for z in bufs],
        input_output_aliases={6 + i: 3 + i for i in range(n)},
        scratch_shapes=[pltpu.VMEM((NI, N, L), F32)] + (_ag_sems(n) if n else []),
        compiler_params=_cparams(("arbitrary",)),
    )(w, kn, b, k, r, v, *bufs)
    return res[0], res[1], res[2], list(res[3:])


def wkv_bwd(w, kn, b, k, r, v, dy, sp, sa):
    T, N, L = w.shape
    NI, LH = v.shape[1], L // 2
    tb = WKV_TB
    nt = T // tb

    def body(w_ref, kn_ref, b_ref, k_ref, r_ref, v_ref, dy_ref, sp_ref, sa_ref,
             dw_ref, dkn_ref, db_ref, dk_ref, dr_ref, dv_ref, ds_ref):
        @pl.when(pl.program_id(0) == 0)
        def _():
            ds_ref[...] = jnp.zeros_like(ds_ref)

        def step(q, carry):
            s = tb - 1 - q
            W, KN, B, Kk, Rr = w_ref[s], kn_ref[s], b_ref[s], k_ref[s], r_ref[s]
            dW = jnp.zeros((N, L), F32)
            dKN, dB, dK, T1 = dW, dW, dW, dW
            al = jnp.zeros((1, L), F32)
            be = al
            for i in range(NI):
                Sp = sp_ref[s, i]
                vi = v_ref[s, pl.ds(i, 1), :]
                dyi = dy_ref[s, pl.ds(i, 1), :]
                sai = sa_ref[s, pl.ds(i, 1), :]
                dS = ds_ref[i] + dyi * Rr
                T1 = T1 + Sp * dyi
                al = al + sai * dyi
                be = be + vi * dyi
                dv_ref[s, pl.ds(i, 1), :] = jnp.sum(dS * Kk, axis=0, keepdims=True)
                dK = dK + dS * vi
                dsa = jnp.sum(dS * B, axis=0, keepdims=True)
                dB = dB + dS * sai
                dW = dW + dS * Sp
                dKN = dKN + Sp * dsa
                ds_ref[i] = dS * W + dsa * KN
            dR = W * T1 + B * al + Kk * be
            for ref, val in ((dw_ref, dW), (dkn_ref, dKN), (db_ref, dB), (dk_ref, dK), (dr_ref, dR)):
                ref[s] = (val + pltpu.roll(val, LH, 1))[:, :LH]
            return carry

        lax.fori_loop(0, tb, step, 0)

    jspec = pl.BlockSpec((tb, N, L), lambda t: (nt - 1 - t, 0, 0))
    gspec = pl.BlockSpec((tb, N, LH), lambda t: (nt - 1 - t, 0, 0))
    ispec = pl.BlockSpec((tb, NI, L), lambda t: (nt - 1 - t, 0, 0))
    gshape = jax.ShapeDtypeStruct((T, N, LH), F32)
    return _pcall(
        body, name="wkv_bwd", grid=(nt,),
        in_specs=[jspec] * 5 + [ispec, ispec, pl.BlockSpec((tb, NI, N, L), lambda t: (nt - 1 - t, 0, 0, 0)), ispec],
        out_specs=[gspec] * 5 + [ispec], out_shape=[gshape] * 5 + [jax.ShapeDtypeStruct((T, NI, L), F32)],
        scratch_shapes=[pltpu.VMEM((NI, N, L), F32)],
        compiler_params=_cparams(("arbitrary",)),
    )(w, kn, b, k, r, v, dy, sp, sa)


_NT = (((1,), (1,)), ((), ()))
_TN = (((0,), (0,)), ((), ()))
ATT_SCALE = QK_DIM ** -0.5


def _att_bias():
    col = lax.broadcasted_iota(jnp.int32, (Q_BLOCK, Q_BLOCK), 1)
    row = lax.broadcasted_iota(jnp.int32, (Q_BLOCK, Q_BLOCK), 0)
    return jnp.where(col < N_META, 0.0, -1e30).astype(F32), jnp.where(col <= row, 0.0, -1e30).astype(F32)


def _att_scores(q1, q2, kn_ref, kp_ref, s_ref, bias, L):
    s = lax.dot_general(q1, kn_ref[0, :L, :], _NT, preferred_element_type=F32)
    s = s + lax.dot_general(q2, kp_ref[0, :L, :], _NT, preferred_element_type=F32)
    s_ref[:, :L] = s * ATT_SCALE
    s_ref[:, :Q_BLOCK] += bias[0]
    s_ref[:, L - Q_BLOCK:L] += bias[1]


def _att_probs(s_ref, L):
    s = s_ref[:, :L]
    m = jnp.max(s, axis=-1, keepdims=True)
    p = jnp.exp(s - m)
    return p / jnp.sum(p, axis=-1, keepdims=True)


def _att_blocks(nq, qn_ref, qp_ref, kn_ref, kp_ref, s_ref):
    bias = _att_bias()
    rows = lambda i: pl.ds(Q_BLOCK * i, Q_BLOCK)
    L = lambda i: Q_BLOCK * (i + 2)
    score = lambda i: _att_scores(qn_ref[0, rows(i), :], qp_ref[0, rows(i), :], kn_ref, kp_ref, s_ref.at[i % 2], bias, L(i))
    score(0)
    for i in range(nq):
        if i + 1 < nq:
            score(i + 1)
        yield i, rows(i), L(i), _att_probs(s_ref.at[i % 2], L(i))


def _edge_steps(nb, nh):
    b, h = pl.program_id(0), pl.program_id(1)
    return (b == 0) & (h == 0), (b == nb - 1) & (h == nh - 1)


def attn_fwd(qn, qp, kn, kp, v, bufs=()):
    B, S, HD = qn.shape
    SP = kn.shape[1]
    H = HD // LANE
    nq = S // Q_BLOCK
    n = len(bufs)

    def body(*refs):
        qn_ref, qp_ref, kn_ref, kp_ref, v_ref = refs[:5]
        o_ref = refs[5 + n]
        s_ref = refs[6 + 2 * n]
        if n:
            start, finish = _ag_steps(refs[5:5 + n], refs[6 + n:6 + 2 * n], *refs[7 + 2 * n:])
            is_first, is_last = _edge_steps(B, H)
            pl.when(is_first)(start)
        for i, rows, L, p in _att_blocks(nq, qn_ref, qp_ref, kn_ref, kp_ref, s_ref):
            o_ref[0, rows, :] = jnp.dot(p.astype(BF16), v_ref[0, :L, :], preferred_element_type=F32)
        if n:
            pl.when(is_last)(finish)

    qspec = pl.BlockSpec((1, S, LANE), lambda b, h: (b, 0, h))
    kspec = pl.BlockSpec((1, SP, LANE), lambda b, h: (b, 0, h))
    pspec = pl.BlockSpec((1, SP, LANE), lambda b, h: (b, 0, 0))
    res = _pcall(
        body, name="attn_fwd", grid=(B, H), in_specs=[qspec, qspec, kspec, pspec, kspec] + [_HBM] * n,
        out_specs=[qspec] + [_HBM] * n,
        out_shape=[jax.ShapeDtypeStruct((B, S, HD), F32)] + [jax.ShapeDtypeStruct(w.shape, w.dtype) for w in bufs],
        input_output_aliases={5 + i: 1 + i for i in range(n)},
        scratch_shapes=[pltpu.VMEM((2, Q_BLOCK, SP), F32)] + (_ag_sems(n) if n else []),
        compiler_params=_cparams(("arbitrary", "arbitrary")),
    )(qn, qp, kn, kp, v, *bufs)
    return res[0], list(res[1:])


def attn_bwd(qn, qp, kn, kp, v, do, hs=()):
    B, S, HD = qn.shape
    SP = kn.shape[1]
    H = HD // LANE
    nq = S // Q_BLOCK
    n = len(hs)

    def body(*refs):
        qn_ref, qp_ref, kn_ref, kp_ref, v_ref, do_ref = refs[:6]
        dqn_ref, dqp_ref, dkn_ref, dv_ref, dkp_ref = refs[6 + n:11 + n]
        dkn_acc, dv_acc, s_ref = refs[11 + 2 * n:14 + 2 * n]
        if n:
            start, finish = _rs_steps(refs[6:6 + n], refs[11 + n:11 + 2 * n], *refs[14 + 2 * n:])
            is_first, is_last = _edge_steps(B, H)
            pl.when(is_first)(start)

        @pl.when(pl.program_id(1) == 0)
        def _():
            dkp_ref[...] = jnp.zeros_like(dkp_ref)

        dkn_acc[...] = jnp.zeros_like(dkn_acc)
        dv_acc[...] = jnp.zeros_like(dv_acc)
        for i, rows, L, p in _att_blocks(nq, qn_ref, qp_ref, kn_ref, kp_ref, s_ref):
            q1, q2, do_i = qn_ref[0, rows, :], qp_ref[0, rows, :], do_ref[0, rows, :]
            dp = lax.dot_general(do_i, v_ref[0, :L, :], _NT, preferred_element_type=F32)
            ds = (p * (dp - jnp.sum(p * dp, axis=-1, keepdims=True)) * ATT_SCALE).astype(BF16)
            dqn_ref[0, rows, :] = jnp.dot(ds, kn_ref[0, :L, :], preferred_element_type=F32).astype(dqn_ref.dtype)
            dqp_ref[0, rows, :] = jnp.dot(ds, kp_ref[0, :L, :], preferred_element_type=F32)
            dkn_acc[:L, :] += lax.dot_general(ds, q1, _TN, preferred_element_type=F32)
            dkp_ref[0, :L, :] += lax.dot_general(ds, q2, _TN, preferred_element_type=F32)
            dv_acc[:L, :] += lax.dot_general(p.astype(BF16), do_i, _TN, preferred_element_type=F32)
        dkn_ref[0] = dkn_acc[...].astype(dkn_ref.dtype)
        dv_ref[0] = dv_acc[...].astype(dv_ref.dtype)
        if n:
            pl.when(is_last)(finish)

    qspec = pl.BlockSpec((1, S, LANE), lambda b, h: (b, 0, h))
    kspec = pl.BlockSpec((1, SP, LANE), lambda b, h: (b, 0, h))
    pspec = pl.BlockSpec((1, SP, LANE), lambda b, h: (b, 0, 0))
    res = _pcall(
        body, name="attn_bwd", grid=(B, H), in_specs=[qspec, qspec, kspec, pspec, kspec, qspec] + [_HBM] * n,
        out_specs=[qspec, qspec, kspec, kspec, pspec] + [_HBM] * n,
        out_shape=[jax.ShapeDtypeStruct((B, S, HD), BF16), jax.ShapeDtypeStruct((B, S, HD), F32),
                   jax.ShapeDtypeStruct((B, SP, HD), BF16), jax.ShapeDtypeStruct((B, SP, HD), BF16),
                   jax.ShapeDtypeStruct((B, SP, LANE), F32)] + [jax.ShapeDtypeStruct(h.shape, h.dtype) for h in hs],
        scratch_shapes=[pltpu.VMEM((SP, LANE), F32), pltpu.VMEM((SP, LANE), F32), pltpu.VMEM((2, Q_BLOCK, SP), F32)]
        + (_rs_sems(n) if n else []),
        compiler_params=_cparams(("arbitrary", "arbitrary")),
    )(qn, qp, kn, kp, v, do, *hs)
    return res[:5], list(res[5:])


_HBM = pl.BlockSpec(memory_space=pltpu.HBM)


def _place():
    x, y, c = lax.axis_index("x"), lax.axis_index("y"), lax.axis_index("c")
    chips = [(1 - x, y), (x, 1 - y), (1 - x, 1 - y)]
    return x, y, c, chips


def _rcopy(src, dst, ssem, rsem, dev):
    return pltpu.make_async_remote_copy(src_ref=src, dst_ref=dst, send_sem=ssem, recv_sem=rsem,
                                        device_id=dev, device_id_type=MESH)


def _half_rows(c, r):
    return pl.ds(pl.multiple_of(c * (r // 2), 16), r // 2)


def _ag_steps(w, out, ssem, rsem):
    n = len(w)
    x, y, c, chips = _place()
    s = 2 * x + y
    rows = [_half_rows(c, w[i].shape[1]) for i in range(n)]
    orows = [_half_rows(1 - c, w[i].shape[1]) for i in range(n)]
    first = [_rcopy(w[i].at[s, rows[i]], out[i].at[s, rows[i]], ssem.at[6 * i + j], rsem.at[6 * i + j], (px, py, c))
             for i in range(n) for j, (px, py) in enumerate(chips)]

    def start():
        for cp in first:
            cp.start()

    def finish():
        passed = []
        for j, (px, py) in enumerate(chips):
            sp = 2 * px + py
            for i in range(n):
                here = out[i].at[sp, rows[i]]
                _rcopy(here, here, ssem.at[6 * i + j], rsem.at[6 * i + j], (px, py, c)).wait_recv()
                fw = _rcopy(here, here, ssem.at[6 * i + 3 + j], rsem.at[6 * i + 3 + j], (x, y, 1 - c))
                fw.start()
                passed.append(fw)
        for j, (px, py) in enumerate(chips):
            sp = 2 * px + py
            for i in range(n):
                there = out[i].at[sp, orows[i]]
                _rcopy(there, there, ssem.at[6 * i + 3 + j], rsem.at[6 * i + 3 + j], (x, y, 1 - c)).wait_recv()
        for cp in first + passed:
            cp.wait_send()

    return start, finish


def _ag_sems(n):
    return [pltpu.SemaphoreType.DMA((6 * n,)), pltpu.SemaphoreType.DMA((6 * n,))]


def ag_weights(bufs):
    n = len(bufs)

    def body(*refs):
        start, finish = _ag_steps(refs[:n], refs[n:2 * n], *refs[2 * n:])
        start()
        finish()

    return _pcall(
        body, name="ag_weights", in_specs=[_HBM] * n, out_specs=[_HBM] * n,
        out_shape=[jax.ShapeDtypeStruct(w.shape, w.dtype) for w in bufs],
        input_output_aliases={i: i for i in range(n)}, scratch_shapes=_ag_sems(n),
    )(*bufs)


def swap_halves(gs, name):
    n = len(gs)

    def body(*refs):
        g, out = refs[:n], refs[n:2 * n]
        ssem, rsem = refs[2 * n:]
        x, y, c, _ = _place()
        cps = [_rcopy(g[i].at[:, _half_rows(1 - c, g[i].shape[1])], out[i], ssem.at[i], rsem.at[i], (x, y, 1 - c))
               for i in range(n)]
        for cp in cps:
            cp.start()
        for cp in cps:
            cp.wait()

    return _pcall(
        body, name=name, in_specs=[_HBM] * n, out_specs=[_HBM] * n,
        out_shape=[jax.ShapeDtypeStruct((4, g.shape[1] // 2, g.shape[2]), g.dtype) for g in gs],
        scratch_shapes=[pltpu.SemaphoreType.DMA((n,)), pltpu.SemaphoreType.DMA((n,))],
    )(*gs)


def _rs_steps(h, out, ssem, rsem):
    n = len(h)
    x, y, c, chips = _place()
    s = 2 * x + y
    cps = [_rcopy(h[i].at[2 * px + py], out[i].at[s], ssem.at[3 * i + j], rsem.at[3 * i + j], (px, py, c))
           for i in range(n) for j, (px, py) in enumerate(chips)]

    def start():
        for cp in cps:
            cp.start()

    def finish():
        for j, (px, py) in enumerate(chips):
            for i in range(n):
                _rcopy(h[i].at[s], out[i].at[2 * px + py], ssem.at[3 * i + j], rsem.at[3 * i + j], (px, py, c)).wait_recv()
        for cp in cps:
            cp.wait_send()

    return start, finish


def _rs_sems(n):
    return [pltpu.SemaphoreType.DMA((3 * n,)), pltpu.SemaphoreType.DMA((3 * n,))]


def share_sibling(ts):
    n = len(ts)

    def body(*refs):
        t, out = refs[:n], refs[n:2 * n]
        ssem, rsem = refs[2 * n:]
        x, y, c, _ = _place()
        cps = [_rcopy(t[i].at[c], out[i].at[c], ssem.at[i], rsem.at[i], (x, y, 1 - c)) for i in range(n)]
        for cp in cps:
            cp.start()
        for i in range(n):
            _rcopy(t[i].at[c], out[i].at[1 - c], ssem.at[i], rsem.at[i], (x, y, 1 - c)).wait_recv()
        for cp in cps:
            cp.wait_send()

    return _pcall(
        body, name="share_sibling", in_specs=[_HBM] * n, out_specs=[_HBM] * n,
        out_shape=[jax.ShapeDtypeStruct(t.shape, t.dtype) for t in ts],
        input_output_aliases={i: i for i in range(n)},
        scratch_shapes=[pltpu.SemaphoreType.DMA((n,)), pltpu.SemaphoreType.DMA((n,))],
    )(*ts)


def allreduce8(v, name):
    P, W = v.shape

    def body(v_ref, out_ref, buf, ssem, rsem):
        x, y, c, _ = _place()
        me = 4 * x + 2 * y + c
        buf[me] = v_ref[...]
        cps = []
        for k in range(1, 8):
            px = 1 - x if k & 4 else x
            py = 1 - y if k & 2 else y
            pc = 1 - c if k & 1 else c
            cp = _rcopy(buf.at[me], buf.at[me], ssem.at[k - 1], rsem.at[k - 1], (px, py, pc))
            cp.start()
            cps.append((cp, 4 * px + 2 * py + pc))
        for k, (cp, peer) in enumerate(cps):
            _rcopy(buf.at[me], buf.at[peer], ssem.at[k], rsem.at[k], (x, y, c)).wait_recv()
        for cp, _ in cps:
            cp.wait_send()
        acc = buf[0]
        for d in range(1, 8):
            acc = acc + buf[d]
        out_ref[...] = acc

    return _pcall(
        body, name=name, in_specs=[pl.BlockSpec(memory_space=pltpu.VMEM)],
        out_specs=pl.BlockSpec(memory_space=pltpu.VMEM), out_shape=jax.ShapeDtypeStruct((P, W), F32),
        scratch_shapes=[pltpu.VMEM((8, P, W), F32), pltpu.SemaphoreType.DMA((7,)), pltpu.SemaphoreType.DMA((7,))],
    )(v)


def add_half(g, rcv, cidx, name):
    _, hr, W = rcv.shape
    tr = _rows_tile(hr, W)
    nb = hr // tr

    def body(c_ref, g_ref, r_ref, o_ref):
        o_ref[...] = (g_ref[...] + r_ref[...]).astype(o_ref.dtype)

    return _pcall(
        body, name=name,
        grid_spec=pltpu.PrefetchScalarGridSpec(
            num_scalar_prefetch=1, grid=(4, nb),
            in_specs=[pl.BlockSpec((1, tr, W), lambda s, i, c: (s, c[0] * nb + i, 0)),
                      pl.BlockSpec((1, tr, W), lambda s, i, c: (s, i, 0))],
            out_specs=pl.BlockSpec((1, tr, W), lambda s, i, c: (s, i, 0))),
        out_shape=jax.ShapeDtypeStruct((4, hr, W), BF16), compiler_params=_cparams(("parallel", "parallel")),
    )(cidx, g, rcv)


def sum_chips(p, h, sc, name):
    _, hr, W = p.shape
    tr = _rows_tile(hr, W)

    def body(s_ref, c_ref, p_ref, h_ref, o_ref):
        s = s_ref[0]
        own = h_ref[0]
        f = lambda k: jnp.where(s == k, own, p_ref[k]).astype(F32)
        o_ref[0] = ((f(0) + f(1)) + f(2)) + f(3)

    return _pcall(
        body, name=name,
        grid_spec=pltpu.PrefetchScalarGridSpec(
            num_scalar_prefetch=2, grid=(hr // tr,),
            in_specs=[pl.BlockSpec((4, tr, W), lambda i, s, c: (0, i, 0)),
                      pl.BlockSpec((1, tr, W), lambda i, s, c: (s[0], i, 0))],
            out_specs=pl.BlockSpec((1, tr, W), lambda i, s, c: (c[0], i, 0))),
        out_shape=jax.ShapeDtypeStruct((2, hr, W), F32), compiler_params=_cparams(("parallel",)),
    )(sc[0], sc[1], p, h)


HI = lax.Precision.HIGHEST
BLOCK_BYTES = 3 * 512 * 1024


def _rows_tile(R, w, T=None):
    cands = [t for t in range(16, R + 1, 16) if R % t == 0 and (T is None or T % t == 0)]
    ok = [t for t in cands if t * w * 4 <= BLOCK_BYTES]
    return max(ok) if ok else min(cands)


def _gsum_exact(x):
    r = jnp.right_shift(lax.broadcasted_iota(jnp.int32, (LANE, LANE), 0), 6)
    c = jnp.right_shift(lax.broadcasted_iota(jnp.int32, (LANE, LANE), 1), 6)
    g = (r == c).astype(BF16)
    x1 = x.astype(BF16)
    r1 = x - x1.astype(F32)
    x2 = r1.astype(BF16)
    x3 = (r1 - x2.astype(F32)).astype(BF16)
    dot = lambda z: jnp.dot(z, g, preferred_element_type=F32)
    return (dot(x3) + dot(x2)) + dot(x1)


@jax.custom_vjp
def _gsum(x):
    return _gsum_exact(x)


_gsum.defvjp(lambda x: (_gsum_exact(x), None), lambda _, ct: (_gsum_exact(ct),))


def _rms(x, g):
    return x * lax.rsqrt(jnp.mean(x * x, axis=-1, keepdims=True) + NORM_EPS) * g


def _silu_mul(gate, up):
    return jax.nn.silu(gate) * up


def _shift(p, prev, mu):
    return p + mu * (prev - p)


def _prev_rows(p, above, tpos):
    first = lax.broadcasted_iota(jnp.int32, p.shape, 0) == 0
    prev = jnp.where(first, above[7:8, :], pltpu.roll(p, 1, 0))
    return jnp.where(tpos[:, :1] == 0.0, 0.0, prev)


def _shift_rows(p, above, tpos, mu):
    return _shift(p, _prev_rows(p, above, tpos), mu)


def _shift_rows_bwd(t_last, p, above, tpos, mu, ct, ct_below):
    dmu = jnp.sum(ct * (_prev_rows(p, above, tpos) - p), axis=0, keepdims=True)
    last = lax.broadcasted_iota(jnp.int32, p.shape, 0) == p.shape[0] - 1
    nxt = jnp.where(last, ct_below[0:1, :], pltpu.roll(ct, p.shape[0] - 1, 0))
    nxt = jnp.where(tpos[:, :1] == t_last, 0.0, nxt)
    return (1.0 - mu) * ct + mu * nxt, dmu


def _lora_act(p, prev, mu):
    s = _shift(p, prev, mu)
    return jax.nn.sigmoid(s[:, :G_LORA]), jnp.tanh(s[:, G_LORA:G_LORA + LANE]), s[:, G_LORA + LANE:]


def _prep(k, lw, la, w0, a0, kk_w, ka_w):
    wpre = -jax.nn.softplus(-(w0 + lw)) - 0.5
    decay = jnp.exp(-jnp.exp(wpre))
    a = jax.nn.sigmoid(a0 + la)
    kk = k * kk_w
    kk = kk * lax.rsqrt(jnp.maximum(_gsum(kk * kk), 1e-24))
    k2 = k * (1.0 + (a - 1.0) * ka_w)
    return decay, -kk, kk * a, k2


def _post(y, r, k2, v, g, gnw, gnb, rk):
    mean = _gsum(y) * (1.0 / RWKV_HEAD)
    d = y - mean
    var = _gsum(d * d) * (1.0 / RWKV_HEAD)
    yn = d * lax.rsqrt(var + GN_EPS) * gnw + gnb
    bonus = _gsum(r * k2 * rk) * v
    return (yn + bonus) * g


def _gate_mix(ga, gb, ya, o):
    return jax.nn.sigmoid(ga) * ya + jax.nn.sigmoid(gb) * o


def _rope(x, cos, sin):
    return x * cos + pltpu.roll(x, LANE // 2, 1) * sin


def _rope_t(dy, cos, sin):
    return dy * cos + pltpu.roll(dy * sin, LANE // 2, 1)


def _mla_pre(cq, ckv, kpe, cos, sin, qw, kvw):
    return _rms(cq, qw), _rms(ckv, kvw), _rope(kpe, cos, sin)


def _mla_pre_bwd(cq, ckv, dcqn, dckvn, dkr, cos, sin, qw, kvw):
    _, pull = jax.vjp(lambda a, b, c, d: (_rms(a, c), _rms(b, d)), cq, ckv, qw, kvw)
    dcq, dckv, dqw, dkvw = pull((dcqn, dckvn))
    return dcq, dckv, _rope_t(dkr, cos, sin), dqw, dkvw


WEIGHTS = ['meta_tokens', 'ffn1_norm', 'ffn1_w_gate', 'ffn1_w_up', 'ffn1_w_down', 'mix_norm', 'w_in', 'tm_mu', 'w0',
           'w_up', 'a0', 'a_up', 'g_up', 'k_k', 'k_a', 'r_k', 'gn_w', 'gn_b', 'q_norm', 'w_uq', 'kv_norm', 'w_ukv',
           'w_out', 'ffn2_norm', 'ffn2_w_gate', 'ffn2_w_up', 'ffn2_w_down', 'final_norm']
SHARD_AXIS = {'meta_tokens': 1, 'ffn1_w_gate': 1, 'ffn1_w_up': 1, 'ffn1_w_down': 0, 'w_in': 1, 'w_up': 1, 'a_up': 1,
              'g_up': 1, 'w_uq': 1, 'w_ukv': 1, 'w_out': 0, 'ffn2_w_gate': 1, 'ffn2_w_up': 1, 'ffn2_w_down': 0}
GATHERED = [n for n in WEIGHTS if n in SHARD_AXIS and n != 'meta_tokens']
FFN_IN = ('ffn1_w_gate', 'ffn1_w_up', 'ffn2_w_gate', 'ffn2_w_up')
PART_B = ['ffn2_w_gate', 'ffn2_w_up', 'ffn2_w_down', 'w_out']
PART_A1 = ['ffn1_w_gate', 'ffn1_w_up', 'ffn1_w_down']
PART_A2 = [n for n in GATHERED if n not in PART_B and n not in PART_A1]
SMALL = [n for n in WEIGHTS if n not in SHARD_AXIS]


def _to2d(a):
    if a.ndim == 1:
        return a.reshape(1, -1)
    if a.ndim == 3:
        return a.reshape(a.shape[0] * a.shape[1], a.shape[2]) if a.shape[0] == 1 and a.shape[1] > 64 else a.reshape(1, -1)
    return a


def _unpack(flat, shapes):
    out, off = [], 0
    for shp in shapes:
        n = shp[0] * shp[1]
        out.append(flat[off:off + n].reshape(shp))
        off += n
    return out


def _adamw(w, g, m, v):
    m = ADAM_B1 * m + (1.0 - ADAM_B1) * g
    v = ADAM_B2 * v + (1.0 - ADAM_B2) * jnp.square(g)
    m_hat = m / (1.0 - ADAM_B1 ** ADAM_STEP)
    v_hat = v / (1.0 - ADAM_B2 ** ADAM_STEP)
    delta = -ADAM_LR * (m_hat / (jnp.sqrt(v_hat) + ADAM_EPS) + ADAM_WD * w)
    return delta, m, v


def adamw(w, g, m, v, name):
    R, C = w.shape
    if R % 16 == 0 and R > 16:
        tm = _rows_tile(R, C)
    else:
        tm = R
    return rowwise(_adamw, [(w, C, 0), (g, C, 0), (m, C, 0), (v, C, 0)], [], [(C, C, F32)] * 3, tm=tm, name=name)


def _step(a):
    x = a['x']
    Bl, S, D = x.shape
    T = S + N_META
    R, RS = Bl * T, Bl * S
    Hr, Hm = D // RWKV_HEAD, D // LANE
    w2 = {n: _to2d(a[n]) for n in WEIGHTS}
    m2 = {n: _to2d(a['m_' + n]) for n in WEIGHTS}
    v2 = {n: _to2d(a['v_' + n]) for n in WEIGHTS}
    xi, yi, ci = lax.axis_index("x"), lax.axis_index("y"), lax.axis_index("c")
    chip = 2 * xi + yi

    i0 = jnp.zeros((), jnp.int32)
    chip32 = chip.astype(jnp.int32)

    def own_slot(w):
        return lax.dynamic_update_slice(lax.empty((4,) + w.shape, BF16), w.astype(BF16)[None], (chip32, i0, i0))

    stk = dict(zip(PART_A1, ag_weights([own_slot(w2[n]) for n in PART_A1])))

    def unstack(z, axis):
        return z.reshape(4 * z.shape[1], z.shape[2]) if axis == 0 else jnp.concatenate([z[s] for s in range(4)], axis=1)

    full = {n: unstack(stk[n], SHARD_AXIS[n]) for n in PART_A1 if n not in FFN_IN}
    mt = w2['meta_tokens']
    mcols = mt.shape[1]
    mt_z = lax.dynamic_update_slice(jnp.zeros((N_META, D), F32), 0.5 * mt, (jnp.zeros((), jnp.int32), (chip * mcols).astype(jnp.int32)))
    meta_full = allreduce8(mt_z.reshape(-1, LANE), "gather_meta").reshape(N_META, D)

    F = 4 * stk['ffn1_w_gate'].shape[2]
    half = ROPE_DIM // 2
    tmu = w2['tm_mu']
    mu_a = tmu[:, :3 * D]
    zm = lambda n: jnp.zeros((1, n), F32)
    mu_b = jnp.concatenate([tmu[:, 3 * D + 2 * W_LORA:], tmu[:, 3 * D:3 * D + W_LORA], zm(LANE - W_LORA),
                            tmu[:, 3 * D + W_LORA:3 * D + 2 * W_LORA], zm(LANE - A_LORA)], axis=1)
    pos = jnp.arange(T, dtype=F32)
    inv_freq = 1.0 / (ROPE_THETA ** (jnp.arange(0, ROPE_DIM, 2, dtype=F32) / ROPE_DIM))
    ang = pos[:, None] * inv_freq[None, :]
    zt = jnp.zeros((T, half), F32)
    cos_t = jnp.concatenate([jnp.cos(ang), zt, jnp.cos(ang), zt], axis=1)
    sin_t = jnp.concatenate([-jnp.sin(ang), zt, jnp.sin(ang), zt], axis=1)
    cos_q, sin_q = cos_t[N_META:], sin_t[N_META:]

    t_full = _rows_tile(R, D)
    t_512 = _rows_tile(R, 512, T)
    t_128 = _rows_tile(R, LANE, T)
    tq_full = _rows_tile(RS, D)
    tq_128 = _rows_tile(RS, LANE, S)

    def rows3(z):
        return z.reshape(Bl, T, z.shape[-1])

    def real_rows(z):
        return rows3(z)[:, N_META:].reshape(RS, z.shape[-1])

    def pad_meta(z):
        z3 = z.reshape(Bl, S, z.shape[-1])
        return jnp.concatenate([jnp.zeros((Bl, N_META, z.shape[-1]), z.dtype), z3], axis=1).reshape(R, z.shape[-1])

    def shift_down(z):
        z3 = rows3(z)
        return jnp.concatenate([jnp.zeros((Bl, 1, z.shape[-1]), z.dtype), z3[:, :-1]], axis=1).reshape(R, z.shape[-1])

    def shift_up(z):
        z3 = rows3(z)
        return jnp.concatenate([z3[:, 1:], jnp.zeros((Bl, 1, z.shape[-1]), z.dtype)], axis=1).reshape(R, z.shape[-1])

    def ffn_fwd(h, nw, wg, wu, wd, tag, ag_bufs=()):
        n = rowwise(_rms, [(h, D, 0)], [(nw, D, 0)], [(D, D, BF16)], tm=t_full, name=tag + "_norm")[0]
        res = mm_epi(n, [wg, wu], "nn", lambda g_, u_: (g_, u_, _silu_mul(g_, u_)), [], [F32, F32, BF16], b_stack=True,
                     carry=("ag", ag_bufs) if ag_bufs else None, name=tag + "_gate_up")
        (gate, up, act), got = res if ag_bufs else (res, [])
        out = mm(act, wd, res=h, alpha=0.5, name=tag + "_down")
        return (out, (h, n, gate, up, act), got) if ag_bufs else (out, (h, n, gate, up, act))

    def exch(g, name):
        return add_half(g, swap_halves([g], "swap_halves_" + name)[0], cidx, "add_half_" + name)

    def ffn_bwd(dh2, saved, nw, wg, wu, wd, tag, rs_hs=(), rs_own=()):
        h, n, gate, up, act = saved
        dz = rowwise(lambda d: 0.5 * d, [(dh2, 512, 0)], [], [(D, 512, BF16)], tm=t_512, ncb=D // 512, name=tag + "_dz")[0]
        d_wd = mm(act, dz, "tn", name=tag + "_dwd")
        res = mm_epi(dz, [wd], "nt", lambda da, g_, u_: vjp_fn(_silu_mul, 2)(g_, u_, da), [gate, up], [BF16, BF16],
                     carry=("rs", rs_hs) if rs_hs else None, name=tag + "_dact")
        (dgate, dup), got = res if rs_hs else (res, [])
        own = {}
        if rs_own:
            h_wd = exch(stacked(d_wd, 0), rs_own[2])
            d_wg, p_wd = mm(n, dgate, "tn", out_stack=True, carry=("rs", [h_wd]), name=tag + "_dwg")
            d_wu = mm(n, dup, "tn", out_stack=True, name=tag + "_dwu")
            h_wg = exch(d_wg, rs_own[0])
            dn, p_wg = mm(dgate, wg, "nt", b_stack=True, carry=("rs", [h_wg]), name=tag + "_dn1")
            h_wu = exch(d_wu, rs_own[1])
            dn, p_wu = mm(dup, wu, "nt", res=dn, b_stack=True, carry=("rs", [h_wu]), name=tag + "_dn2")
            own = {rs_own[0]: (p_wg[0], h_wg), rs_own[1]: (p_wu[0], h_wu), rs_own[2]: (p_wd[0], h_wd)}
        else:
            d_wg = mm(n, dgate, "tn", out_stack=True, name=tag + "_dwg")
            d_wu = mm(n, dup, "tn", out_stack=True, name=tag + "_dwu")
            dn = mm(dgate, wg, "nt", b_stack=True, name=tag + "_dn1")
            dn = mm(dup, wu, "nt", res=dn, b_stack=True, name=tag + "_dn2")

        def f(h_, dn_, dh_, nw_):
            dh, dnw = vjp_fn(_rms, 2)(h_, nw_, dn_)
            return dh + dh_, dnw

        dh, d_nw = rowwise(f, [(h, D, 0), (dn, D, 0), (dh2, D, 0)], [(nw, D, 0)], [(D, D, F32)], [(1, D, D)],
                           tm=t_full, name=tag + "_dnorm")
        return (dh, d_nw, d_wg, d_wu, d_wd, got, own) if (rs_hs or rs_own) else (dh, d_nw, d_wg, d_wu, d_wd)

    h0 = jnp.concatenate([jnp.broadcast_to(meta_full[None], (Bl, N_META, D)), x], axis=1).reshape(R, D)
    h1, sv1, got_a2 = ffn_fwd(h0, w2['ffn1_norm'], stk['ffn1_w_gate'], stk['ffn1_w_up'], full['ffn1_w_down'], "ffn1",
                              [own_slot(w2[n]) for n in PART_A2])
    stk.update(zip(PART_A2, got_a2))
    full.update({n: unstack(stk[n], SHARD_AXIS[n]) for n in PART_A2})
    win = full['w_in']
    o = 3 * D
    c_xw, c_xa, c_xg = win[:, o:o + W_LORA], win[:, o + W_LORA:o + 2 * W_LORA], win[:, o + 2 * W_LORA:o + 2 * W_LORA + G_LORA]
    o += 2 * W_LORA + G_LORA
    c_cq, c_ckv, c_kpe = win[:, o:o + Q_LORA], win[:, o + Q_LORA:o + Q_LORA + KV_LORA], win[:, o + Q_LORA + KV_LORA:o + Q_LORA + KV_LORA + ROPE_DIM]
    o += Q_LORA + KV_LORA + ROPE_DIM
    c_ga, c_gb = win[:, o:o + D], win[:, o + D:o + 2 * D]
    zc = lambda n: jnp.zeros((D, n), BF16)
    half = ROPE_DIM // 2
    NP0 = 5 * D + 512 + Q_LORA + KV_LORA + LANE
    NP = -(-NP0 // 512) * 512
    win_p = jnp.concatenate([win[:, :3 * D], c_ga, c_gb, c_xg, c_xw, zc(LANE - W_LORA), c_xa, zc(LANE - A_LORA), c_cq, c_ckv,
                             c_kpe[:, :half], zc(half), c_kpe[:, half:], zc(half), zc(NP - NP0)], axis=1)
    O_GA, O_GB, O_L, O_CQ, O_CKV, O_KPE = 3 * D, 4 * D, 5 * D, 5 * D + 512, 5 * D + 512 + Q_LORA, 5 * D + 512 + Q_LORA + KV_LORA
    zr = lambda n: jnp.zeros((n, D), BF16)
    w_up_p = jnp.concatenate([full['w_up'], zr(LANE - W_LORA)], axis=0)
    a_up_p = jnp.concatenate([full['a_up'], zr(LANE - A_LORA)], axis=0)
    g_up = full['g_up']
    wuq = full['w_uq'].reshape(Q_LORA, Hm, QK_DIM)
    zq = jnp.zeros((Q_LORA, Hm, half), BF16)
    wqn = wuq[:, :, :NOPE_DIM].reshape(Q_LORA, Hm * LANE)
    wqp = jnp.concatenate([wuq[:, :, NOPE_DIM:NOPE_DIM + half], zq, wuq[:, :, NOPE_DIM + half:], zq], axis=2).reshape(Q_LORA, Hm * LANE)
    wukv = full['w_ukv'].reshape(KV_LORA, Hm, NOPE_DIM + V_DIM)
    wkn = wukv[:, :, :NOPE_DIM].reshape(KV_LORA, Hm * LANE)
    wv = wukv[:, :, NOPE_DIM:].reshape(KV_LORA, Hm * LANE)
    u = rowwise(_rms, [(h1, D, 0)], [(w2['mix_norm'], D, 0)], [(D, D, BF16)], tm=t_full, name="mix_norm")[0]
    proj = mm(u, win_p, name="proj")
    tpos = jnp.broadcast_to(pos[:, None], (T, LANE))
    tpos_in = (tpos, LANE, 0, T // t_512, True)
    prev_b = shift_down(proj[:, O_L:O_L + 512])
    ps = rowwise(_shift_rows, [(proj, 512, 0), (proj, 512, 0, None, False, "prev"), tpos_in], [(mu_a, 512, 0)],
                 [(3 * D, 512, F32)], tm=t_512, ncb=3 * D // 512, name="shift_rkv")[0]
    sg, txw, xas = rowwise(_lora_act, [(proj, 512, O_L // 512), (prev_b, 512, 0)], [(mu_b, 512, 0)],
                           [(G_LORA, G_LORA, BF16), (LANE, LANE, BF16), (LANE, LANE, BF16)], tm=t_512, name="lora_act")
    lw = mm(txw, w_up_p, name="lora_w")
    la = mm(xas, a_up_p, name="lora_a")
    g = mm(sg, g_up, name="lora_g")
    hb = D // LANE
    par_d = lambda n: (w2[n], LANE, 0)
    decay, kn, bb, k2 = rowwise(_prep, [(ps, LANE, hb), (lw, LANE, 0), (la, LANE, 0)],
                                [par_d('w0'), par_d('a0'), par_d('k_k'), par_d('k_a')], [(D, LANE, F32)] * 4,
                                tm=t_128, ncb=hb, name="wkv_prep")

    NS = Bl * Hr

    def t_major(z):
        return lax.optimization_barrier(z.reshape(Bl, T, D).transpose(1, 0, 2))

    def b_major(z):
        return lax.optimization_barrier(z).transpose(1, 0, 2).reshape(R, D)

    def to_j(z):
        z = t_major(z).reshape(T, NS, RWKV_HEAD).transpose(0, 2, 1)
        return jnp.broadcast_to(z[:, :, None, :], (T, RWKV_HEAD, 2, NS)).reshape(T, RWKV_HEAD, 2 * NS)

    def to_i(z):
        return t_major(z).reshape(T, NS, 2, RWKV_HEAD // 2).transpose(0, 3, 2, 1).reshape(T, RWKV_HEAD // 2, 2 * NS)

    def from_i(z):
        return b_major(z.reshape(T, RWKV_HEAD // 2, 2, NS).transpose(0, 3, 2, 1).reshape(T, Bl, D))

    def from_j(z):
        return b_major(z.transpose(0, 2, 1).reshape(T, Bl, D))

    r_s, v_s = ps[:, :D], ps[:, 2 * D:]
    jw, jkn, jb, jk, jr, iv = to_j(decay), to_j(kn), to_j(bb), to_j(k2), to_j(r_s), to_i(v_s)
    y_i, sp, sa_i, got_b = wkv_fwd(jw, jkn, jb, jk, jr, iv, [own_slot(w2[n]) for n in PART_B])
    stk.update(zip(PART_B, got_b))
    full.update({n: unstack(stk[n], SHARD_AXIS[n]) for n in PART_B if n not in FFN_IN})
    wout = full['w_out']
    y = from_i(y_i)
    post_rows = [(y, LANE, 0), (ps, LANE, 0), (k2, LANE, 0), (ps, LANE, 2 * hb), (g, LANE, 0)]
    post_pars = [par_d('gn_w'), par_d('gn_b'), par_d('r_k')]
    ya = rowwise(_post, post_rows, post_pars, [(D, LANE, F32)], tm=t_128, ncb=hb, name="wkv_post")[0]

    nt512 = T // t_512
    mla_rows = [(proj, Q_LORA, O_CQ // Q_LORA), (proj, KV_LORA, O_CKV // KV_LORA), (proj, LANE, O_KPE // LANE)]
    tabs = [(cos_t, LANE, 0, nt512, True), (sin_t, LANE, 0, nt512, True)]
    mla_pars = [(w2['q_norm'], Q_LORA, 0), (w2['kv_norm'], KV_LORA, 0)]
    cqn, ckvn, kpr = rowwise(_mla_pre, mla_rows + tabs, mla_pars,
                             [(Q_LORA, Q_LORA, BF16), (KV_LORA, KV_LORA, BF16), (LANE, LANE, BF16)], tm=t_512, name="mla_pre")
    cqn_r = real_rows(cqn)
    qn = mm(cqn_r, wqn, out_dtype=BF16, name="q_nope")
    qp_raw = mm(cqn_r, wqp, name="q_pe")
    ntq = S // tq_128
    qtabs = [(cos_q, LANE, 0, ntq, True), (sin_q, LANE, 0, ntq, True)]
    qp = rowwise(_rope, [(qp_raw, LANE, 0)] + qtabs, [], [(D, LANE, BF16)], tm=tq_128, ncb=Hm, name="q_rope")[0]
    knope = mm(ckvn, wkn, out_dtype=BF16, name="k_nope")
    vv = mm(ckvn, wv, out_dtype=BF16, name="v_proj")

    def pad_keys(z):
        z3 = rows3(z)
        return jnp.concatenate([z3[:, :N_META], jnp.zeros((Bl, Q_BLOCK - N_META, z.shape[-1]), z.dtype), z3[:, N_META:]], axis=1)

    def unpad_keys(z):
        return jnp.concatenate([z[:, :N_META], z[:, Q_BLOCK:]], axis=1).reshape(R, z.shape[-1])

    qn3, qp3 = qn.reshape(Bl, S, D), qp.reshape(Bl, S, D)
    knp, kpp, vp = pad_keys(knope), pad_keys(kpr), pad_keys(vv)
    o3, _ = attn_fwd(qn3, qp3, knp, kpp, vp)
    o_att = pad_meta(o3.reshape(RS, D))
    mix_rows = [(proj, 512, O_GA // 512), (proj, 512, O_GB // 512), (ya, 512, 0), (o_att, 512, 0)]
    mix = rowwise(_gate_mix, mix_rows, [], [(D, 512, BF16)], tm=t_512, ncb=D // 512, name="gate_mix")[0]
    h2 = mm(mix, wout, res=h1, name="w_out")
    h3, sv2 = ffn_fwd(h2, w2['ffn2_norm'], stk['ffn2_w_gate'], stk['ffn2_w_up'], full['ffn2_w_down'], "ffn2")

    def loss_fb(h_, tgt, fw):
        yv, pull = jax.vjp(_rms, h_, fw)
        e = yv - tgt
        dh, dfw = pull(e * (1.0 / D))
        return dh, jnp.full((1, LANE), 0.5 / D * jnp.sum(e * e), F32), dfw

    dh3r, lossp, g_final = rowwise(loss_fb, [(real_rows(h3), D, 0), (a['loss_target'].reshape(RS, D), D, 0)],
                                   [(w2['final_norm'], D, 0)], [(D, D, F32)], [(1, LANE, LANE), (1, D, D)],
                                   tm=tq_full, name="loss")
    dh3 = pad_meta(dh3r)

    def stacked(g, axis):
        if g.ndim == 3:
            return g
        if axis == 0:
            return g.reshape(4, g.shape[0] // 4, g.shape[1])
        return g.reshape(g.shape[0], 4, g.shape[1] // 4).transpose(1, 0, 2)

    cidx = ci.reshape(1).astype(jnp.int32)
    gr = {'final_norm': g_final}
    dh2, gr['ffn2_norm'], gr['ffn2_w_gate'], gr['ffn2_w_up'], gr['ffn2_w_down'] = ffn_bwd(
        dh3, sv2, w2['ffn2_norm'], stk['ffn2_w_gate'], stk['ffn2_w_up'], full['ffn2_w_down'], "ffn2")
    dh2b = rowwise(lambda d: d, [(dh2, 512, 0)], [], [(D, 512, BF16)], tm=t_512, ncb=D // 512, name="dh2_cast")[0]
    gr['w_out'] = mm(mix, dh2b, "tn", name="d_wout")
    dmix = mm(dh2b, wout, "nt", name="d_mix")
    dga, dgb, dya, do = rowwise(vjp_fn(_gate_mix, 4), mix_rows + [(dmix, 512, 0)], [],
                                [(D, 512, BF16), (D, 512, BF16), (D, 512, F32), (D, 512, BF16)], tm=t_512, ncb=D // 512,
                                name="d_gate_mix")
    gs_b = [stacked(gr[n], SHARD_AXIS[n]) for n in PART_B]
    hs_b = [add_half(g_, r_, cidx, "add_half_" + n) for n, g_, r_ in zip(PART_B, gs_b, swap_halves(gs_b, "swap_halves_b"))]
    (dqn, dqp, dknp, dvp, dkpp), ps_b = attn_bwd(qn3, qp3, knp, kpp, vp, real_rows(do).reshape(Bl, S, D), hs_b)
    dqn2 = dqn.reshape(RS, D)
    dqp_raw = rowwise(_rope_t, [(dqp.reshape(RS, D), LANE, 0)] + qtabs, [], [(D, LANE, BF16)], tm=tq_128, ncb=Hm,
                      name="d_q_rope")[0]
    d_wqn = mm(cqn_r, dqn2, "tn", name="d_wqn")
    d_wqp = mm(cqn_r, dqp_raw, "tn", name="d_wqp")
    dcqn = mm(dqn2, wqn, "nt", name="d_cqn1")
    dcqn = pad_meta(mm(dqp_raw, wqp, "nt", res=dcqn, name="d_cqn2"))
    dkn2, dv2, dkp2 = unpad_keys(dknp), unpad_keys(dvp), unpad_keys(dkpp)
    d_wkn = mm(ckvn, dkn2, "tn", name="d_wkn")
    d_wv = mm(ckvn, dv2, "tn", name="d_wv")
    dckvn = mm(dkn2, wkn, "nt", name="d_ckvn1")
    dckvn = mm(dv2, wv, "nt", res=dckvn, name="d_ckvn2")
    dcq, dckv, dkpe, gr['q_norm'], gr['kv_norm'] = rowwise(
        _mla_pre_bwd, mla_rows[:2] + [(dcqn, Q_LORA, 0), (dckvn, KV_LORA, 0), (dkp2, LANE, 0)] + tabs, mla_pars,
        [(Q_LORA, Q_LORA, BF16), (KV_LORA, KV_LORA, BF16), (LANE, LANE, BF16)], [(1, Q_LORA, Q_LORA), (1, KV_LORA, KV_LORA)],
        tm=t_512, name="d_mla_pre")

    def post_bwd(y_, r_, k2_, v_, g_, dya_, gnw, gnb, rk):
        return vjp_fn(_post, 8)(y_, r_, k2_, v_, g_, gnw, gnb, rk, dya_)

    dy, dr_b, dk2_b, dv_b, dg, gr['gn_w'], gr['gn_b'], gr['r_k'] = rowwise(
        post_bwd, post_rows + [(dya, LANE, 0)], post_pars,
        [(D, LANE, F32)] * 4 + [(D, LANE, BF16)], [(1, D, LANE)] * 3, tm=t_128, ncb=hb, name="d_wkv_post")
    jdw, jdkn, jdb, jdk, jdr, idv = wkv_bwd(jw, jkn, jb, jk, jr, iv, to_i(dy), sp, sa_i)
    ddecay, dkn_w, db_w, dk2_w, dr_w, dv_w = from_j(jdw), from_j(jdkn), from_j(jdb), from_j(jdk), from_j(jdr), from_i(idv)

    def prep_bwd(k_, lw_, la_, dd, dkn_, db_, dk2a, dk2b, w0, a0, kkw, kaw):
        return vjp_fn(_prep, 7)(k_, lw_, la_, w0, a0, kkw, kaw, dd, dkn_, db_, dk2a + dk2b)

    dk_s, dlw, dla, gr['w0'], gr['a0'], gr['k_k'], gr['k_a'] = rowwise(
        prep_bwd, [(ps, LANE, hb), (lw, LANE, 0), (la, LANE, 0), (ddecay, LANE, 0), (dkn_w, LANE, 0), (db_w, LANE, 0),
                   (dk2_w, LANE, 0), (dk2_b, LANE, 0)],
        [par_d('w0'), par_d('a0'), par_d('k_k'), par_d('k_a')], [(D, LANE, F32), (D, LANE, BF16), (D, LANE, BF16)],
        [(1, D, LANE)] * 4, tm=t_128, ncb=hb, name="d_wkv_prep")
    d_wup = mm(txw, dlw, "tn", name="d_wup")
    dtxw = mm(dlw, w_up_p, "nt", name="d_txw")
    d_aup = mm(xas, dla, "tn", name="d_aup")
    dxa = mm(dla, a_up_p, "nt", name="d_xa")
    gr['g_up'] = mm(sg, dg, "tn", name="d_gup")
    dsg = mm(dg, g_up, "nt", name="d_sg")

    def lora_bwd(p_, prev_, dsg_, dt_, dxa_, mu_):
        return vjp_fn(_lora_act, 3)(p_, prev_, mu_, dsg_, dt_, dxa_)

    dpb, dprevb, dmu_b = rowwise(lora_bwd, [(proj, 512, O_L // 512), (prev_b, 512, 0), (dsg, G_LORA, 0), (dtxw, LANE, 0),
                                            (dxa, LANE, 0)], [(mu_b, 512, 0)], [(512, 512, F32)] * 2, [(1, 512, 512)],
                                 tm=t_512, name="d_lora_act")

    def shift_back(sec, cts, tag):
        nb = D // 512

        def f(p_, above, tp, *rest):
            cs, mu_ = rest[:-1], rest[-1]
            ct = cs[0] if len(cts) == 1 else cs[0] + cs[2]
            below = cs[1] if len(cts) == 1 else cs[1] + cs[3]
            return _shift_rows_bwd(float(T - 1), p_, above, tp, mu_, ct, below)

        ct_ins = [z for c in cts for z in ((c, 512, 0), (c, 512, 0, None, False, "next"))]
        return rowwise(f, [(proj, 512, sec * nb), (proj, 512, sec * nb, None, False, "prev"), tpos_in] + ct_ins,
                       [(mu_a, 512, sec * nb)], [(D, 512, BF16)], [(1, D, 512)], tm=t_512, ncb=nb, name="d_shift_" + tag)

    dsec_r, dmu_r = shift_back(0, [dr_w, dr_b], "r")
    dsec_k, dmu_k = shift_back(1, [dk_s], "k")
    dsec_v, dmu_v = shift_back(2, [dv_w, dv_b], "v")

    def add_cast(p_, q_):
        return p_ + q_

    def dsec(dp_, dprev_, tag):
        C = dp_.shape[1]
        return rowwise(add_cast, [(dp_, 512, 0), (shift_up(dprev_), 512, 0)], [], [(C, 512, BF16)], tm=t_512, ncb=C // 512,
                       name="d_sec_" + tag)[0]

    zpad = jnp.zeros((R, NP - NP0), BF16)
    dproj = jnp.concatenate([dsec_r, dsec_k, dsec_v, dga, dgb,
                             dsec(dpb, dprevb, "l"), dcq, dckv, dkpe, zpad], axis=1)
    d_win_p = mm(u, dproj, "tn", name="d_win")
    du = mm(dproj, win_p, "nt", name="d_u")

    def norm_bwd(h_, dn_, dh_, nw_):
        dh, dnw = vjp_fn(_rms, 2)(h_, nw_, dn_)
        return dh + dh_, dnw

    dh1, gr['mix_norm'] = rowwise(norm_bwd, [(h1, D, 0), (du, D, 0), (dh2, D, 0)], [(w2['mix_norm'], D, 0)],
                                  [(D, D, F32)], [(1, D, D)], tm=t_full, name="d_mix_norm")
    gr['w_in'] = jnp.concatenate([
        d_win_p[:, :3 * D], d_win_p[:, O_L + G_LORA:O_L + G_LORA + W_LORA], d_win_p[:, O_L + G_LORA + LANE:O_L + G_LORA + LANE + A_LORA],
        d_win_p[:, O_L:O_L + G_LORA], d_win_p[:, O_CQ:O_CQ + Q_LORA], d_win_p[:, O_CKV:O_CKV + KV_LORA],
        d_win_p[:, O_KPE:O_KPE + half], d_win_p[:, O_KPE + 2 * half:O_KPE + 3 * half], d_win_p[:, O_GA:O_GA + 2 * D]], axis=1)
    gr['tm_mu'] = jnp.concatenate([dmu_r, dmu_k, dmu_v, dmu_b[:, G_LORA:G_LORA + W_LORA],
                                   dmu_b[:, G_LORA + LANE:G_LORA + LANE + A_LORA], dmu_b[:, :G_LORA]], axis=1)
    gr['w_up'], gr['a_up'] = d_wup[:W_LORA], d_aup[:A_LORA]
    dq3n, dq3p = d_wqn.reshape(Q_LORA, Hm, LANE), d_wqp.reshape(Q_LORA, Hm, LANE)
    gr['w_uq'] = jnp.concatenate([dq3n, dq3p[:, :, :half], dq3p[:, :, 2 * half:3 * half]], axis=2).reshape(Q_LORA, Hm * QK_DIM)
    gr['w_ukv'] = jnp.concatenate([d_wkn.reshape(KV_LORA, Hm, LANE), d_wv.reshape(KV_LORA, Hm, LANE)], axis=2).reshape(
        KV_LORA, Hm * (NOPE_DIM + V_DIM))
    gs_a2 = [stacked(gr[n], SHARD_AXIS[n]) for n in PART_A2]
    hs_a2 = [add_half(g_, r_, cidx, "add_half_" + n) for n, g_, r_ in zip(PART_A2, gs_a2, swap_halves(gs_a2, "swap_halves_a2"))]
    dh0, gr['ffn1_norm'], gr['ffn1_w_gate'], gr['ffn1_w_up'], gr['ffn1_w_down'], ps_a2, own_a1 = ffn_bwd(
        dh1, sv1, w2['ffn1_norm'], stk['ffn1_w_gate'], stk['ffn1_w_up'], full['ffn1_w_down'], "ffn1", hs_a2, PART_A1)
    dh0_3 = rows3(dh0)
    grad_x = dh0_3[:, N_META:]
    gr['meta_tokens'] = jnp.sum(dh0_3[:, :N_META], axis=0)

    ps_a1, hs_a1 = [own_a1[n][0] for n in PART_A1], [own_a1[n][1] for n in PART_A1]
    sc = (chip32.reshape(1), cidx)
    ts = [sum_chips(p, h, sc, "sum_chips_" + n)
          for n, p, h in zip(PART_A1 + PART_A2 + PART_B, ps_a1 + ps_a2 + ps_b, hs_a1 + hs_a2 + hs_b)]
    g_shard = {n: z.reshape(w2[n].shape) for n, z in zip(PART_A1 + PART_A2 + PART_B, share_sibling(ts))}
    small = jnp.concatenate([gr[n].reshape(-1) for n in SMALL] + [gr['meta_tokens'].reshape(-1), lossp.reshape(-1)])
    ns = small.shape[0]
    nsp = -(-ns // (8 * LANE)) * 8 * LANE
    small_sum = allreduce8(jnp.pad(small, (0, nsp - ns)).reshape(-1, LANE), "allreduce_small").reshape(-1)
    g_small = dict(zip(SMALL + ['meta_full'], _unpack(small_sum, [w2[n].shape for n in SMALL] + [(N_META, D)])))
    g_shard['meta_tokens'] = lax.dynamic_slice(
        g_small['meta_full'], (jnp.zeros((), jnp.int32), (chip * mcols).astype(jnp.int32)), (N_META, mcols))
    loss = small_sum[ns - LANE]

    grads, deltas, new_m, new_v = [], [], [], []
    for n in WEIGHTS:
        gw = g_shard[n] if n in g_shard else g_small[n]
        d_, m_, v_ = adamw(w2[n], gw, m2[n], v2[n], "adamw_" + n)
        shp = a[n].shape
        grads.append(gw.reshape(shp))
        deltas.append(d_.reshape(shp))
        new_m.append(m_.reshape(shp))
        new_v.append(v_.reshape(shp))
    return (loss, grad_x, *grads, *deltas, *new_m, *new_v)


def kernel(x, meta_tokens, ffn1_norm, ffn1_w_gate, ffn1_w_up, ffn1_w_down, mix_norm, w_in, tm_mu, w0, w_up, a0, a_up, g_up, k_k, k_a, r_k, gn_w, gn_b, q_norm, w_uq, kv_norm, w_ukv, w_out, ffn2_norm, ffn2_w_gate, ffn2_w_up, ffn2_w_down, final_norm, loss_target, m_meta_tokens, m_ffn1_norm, m_ffn1_w_gate, m_ffn1_w_up, m_ffn1_w_down, m_mix_norm, m_w_in, m_tm_mu, m_w0, m_w_up, m_a0, m_a_up, m_g_up, m_k_k, m_k_a, m_r_k, m_gn_w, m_gn_b, m_q_norm, m_w_uq, m_kv_norm, m_w_ukv, m_w_out, m_ffn2_norm, m_ffn2_w_gate, m_ffn2_w_up, m_ffn2_w_down, m_final_norm, v_meta_tokens, v_ffn1_norm, v_ffn1_w_gate, v_ffn1_w_up, v_ffn1_w_down, v_mix_norm, v_w_in, v_tm_mu, v_w0, v_w_up, v_a0, v_a_up, v_g_up, v_k_k, v_k_a, v_r_k, v_gn_w, v_gn_b, v_q_norm, v_w_uq, v_kv_norm, v_w_ukv, v_w_out, v_ffn2_norm, v_ffn2_w_gate, v_ffn2_w_up, v_ffn2_w_down, v_final_norm):
    return _step(dict(locals()))
```

```python
import functools
import math

import jax
import jax.numpy as jnp
import numpy as np
from jax import lax
from jax.experimental import pallas as pl
from jax.experimental.pallas import tpu as pltpu

F32 = jnp.float32
BF16 = jnp.bfloat16
MESH = pl.DeviceIdType.MESH

N_META = 16
NORM_EPS = 1e-6
RWKV_HEAD = 64
GN_EPS = RWKV_HEAD * 1e-5
W_LORA, A_LORA, G_LORA = 96, 96, 256
Q_LORA, KV_LORA = 512, 512
NOPE_DIM, ROPE_DIM, V_DIM = 128, 64, 128
QK_DIM = NOPE_DIM + ROPE_DIM
ROPE_THETA = 10000.0
Q_BLOCK = 128
ADAM_LR, ADAM_B1, ADAM_B2, ADAM_EPS, ADAM_WD, ADAM_STEP = 0.001, 0.9, 0.999, 1e-08, 0.01, 10

LANE = 128
VMEM_LIMIT = 56 * 1024 * 1024


def _pcall(body, **kw):
    return pl.pallas_call(body, **kw)


def _cparams(sem):
    return pltpu.CompilerParams(dimension_semantics=sem, vmem_limit_bytes=VMEM_LIMIT)


MM_LANE_TILE = 1536
MM_ROW_TILE = 1408


def _div_tile(n, cap, unit):
    best = None
    for t in range(unit, min(n, cap) + 1, unit):
        if n % t == 0:
            best = t
    return best if best is not None else n


def _rtile(n, pref=512):
    best = None
    for t in range(16, min(n, pref * 2) + 1, 16):
        if n % t == 0 and (best is None or abs(t - pref) < abs(best - pref)):
            best = t
    return best if best is not None else n


def mm(a, b, mode="nn", out_dtype=F32, res=None, alpha=1.0, b_stack=False, out_stack=False, carry=None, name="mm"):
    kind, carried = carry if carry else (None, ())
    nc = len(carried)
    cs = b.shape[-1] if b_stack else None
    bs = (b.shape[1], 4 * cs) if b_stack else b.shape
    if mode == "nn":
        (M, K), (K2, N) = a.shape, bs
    elif mode == "nt":
        (M, K), (N, K2) = a.shape, bs
    else:
        (K, M), (K2, N) = a.shape, bs
    assert K == K2, (a.shape, b.shape, mode)
    if mode == "tn":
        tm, tk = _div_tile(M, MM_ROW_TILE, LANE), _div_tile(K, 1024, 16)
    else:
        tm = _div_tile(M, MM_ROW_TILE, 16)
        tk = _div_tile(cs if (b_stack and mode == "nt") else K, MM_LANE_TILE, LANE)
    ncol = N // 4 if out_stack else (cs if (b_stack and mode == "nn") else N)
    tn = _div_tile(ncol, MM_LANE_TILE if mode != "nt" else 1024, LANE)
    nk = K // tk
    dims = {"nn": (((1,), (0,)), ((), ())), "nt": (((1,), (1,)), ((), ())), "tn": (((0,), (0,)), ((), ()))}[mode]
    direct = out_dtype == F32

    def body(*refs):
        a_ref, b_ref = refs[:2]
        r_ref = refs[2] if res is not None else None
        nin = 3 if res is not None else 2
        o_ref = refs[nin + nc]
        acc = o_ref if direct else refs[nin + 2 * nc + 1]
        k = pl.program_id(2)
        if nc:
            steps = _ag_steps if kind == "ag" else _rs_steps
            start, finish = steps(refs[nin:nin + nc], refs[nin + nc + 1:nin + 2 * nc + 1],
                                  *refs[nin + 2 * nc + (1 if direct else 2):])
            i, j = pl.program_id(0), pl.program_id(1)
            pl.when((i == 0) & (j == 0) & (k == 0))(start)

        @pl.when(k == 0)
        def _():
            acc[...] = jnp.zeros_like(acc) if res is None else r_ref[...].astype(F32)

        p = lax.dot_general(a_ref[...].astype(BF16), b_ref[...].astype(BF16), dims, preferred_element_type=F32)
        acc[...] += p if alpha == 1.0 else alpha * p

        if not direct:
            @pl.when(k == nk - 1)
            def _():
                o_ref[...] = acc[...].astype(o_ref.dtype)

        if nc:
            pl.when((i == M // tm - 1) & (j == N // tn - 1) & (k == nk - 1))(finish)

    if mode == "tn":
        a_spec = pl.BlockSpec((tk, tm), lambda i, j, k: (k, i))
        b_spec = pl.BlockSpec((tk, tn), lambda i, j, k: (k, j))
    else:
        a_spec = pl.BlockSpec((tm, tk), lambda i, j, k: (i, k))
        if mode == "nn":
            if b_stack:
                nps = cs // tn
                b_spec = pl.BlockSpec((None, tk, tn), lambda i, j, k: (j // nps, k, j % nps))
            else:
                b_spec = pl.BlockSpec((tk, tn), lambda i, j, k: (k, j))
        elif b_stack:
            kps = cs // tk
            b_spec = pl.BlockSpec((None, tn, tk), lambda i, j, k: (k // kps, j, k % kps))
        else:
            b_spec = pl.BlockSpec((tn, tk), lambda i, j, k: (j, k))
    r_spec = pl.BlockSpec((tm, tn), lambda i, j, k: (i, j))
    if out_stack:
        ops = (N // 4) // tn
        o_spec = pl.BlockSpec((None, tm, tn), lambda i, j, k: (j // ops, i, j % ops))
        o_shape = jax.ShapeDtypeStruct((4, M, N // 4), out_dtype)
    else:
        o_spec, o_shape = r_spec, jax.ShapeDtypeStruct((M, N), out_dtype)
    in_specs = [a_spec, b_spec] + ([r_spec] if res is not None else [])
    args = (a, b) + ((res,) if res is not None else ())
    if not nc:
        return _pcall(
            body, name=name, grid=(M // tm, N // tn, nk), in_specs=in_specs, out_specs=o_spec, out_shape=o_shape,
            scratch_shapes=[] if direct else [pltpu.VMEM((tm, tn), F32)],
            compiler_params=_cparams(("parallel", "parallel", "arbitrary")),
        )(*args)
    out = _pcall(
        body, name=name, grid=(M // tm, N // tn, nk), in_specs=in_specs + [_HBM] * nc, out_specs=[o_spec] + [_HBM] * nc,
        out_shape=[o_shape] + [jax.ShapeDtypeStruct(z.shape, z.dtype) for z in carried],
        input_output_aliases={len(args) + q: 1 + q for q in range(nc)} if kind == "ag" else {},
        scratch_shapes=([] if direct else [pltpu.VMEM((tm, tn), F32)]) + (_ag_sems(nc) if kind == "ag" else _rs_sems(nc)),
        compiler_params=_cparams(("arbitrary",) * 3),
    )(*args, *carried)
    return out[0], list(out[1:])


MM_EPI_ROW_TILE = 704


def mm_epi(a, bs, mode, epi, extras, out_dtypes, b_stack=False, carry=None, name="mm_epi"):
    kind, carried = carry if carry else (None, ())
    nc = len(carried)
    b = bs[0]
    cs = b.shape[-1] if b_stack else None
    bshape = (b.shape[1], 4 * cs) if b_stack else b.shape
    (M, K) = a.shape
    (K2, N) = bshape if mode == "nn" else bshape[::-1]
    assert K == K2 and mode in ("nn", "nt"), (a.shape, b.shape, mode)
    tm = _div_tile(M, MM_EPI_ROW_TILE, 16)
    tk = _div_tile(cs if (b_stack and mode == "nt") else K, MM_LANE_TILE, LANE)
    tn = _div_tile(cs if (b_stack and mode == "nn") else N, MM_LANE_TILE, LANE)
    nk, nb, ne, no = K // tk, len(bs), len(extras), len(out_dtypes)
    dims = (((1,), (0,)), ((), ())) if mode == "nn" else (((1,), (1,)), ((), ()))

    def body(*refs):
        a_ref, b_refs, e_refs = refs[0], refs[1:1 + nb], refs[1 + nb:1 + nb + ne]
        base = 1 + nb + ne + nc
        o_refs, accs = refs[base:base + no], refs[base + no + nc:base + no + nc + nb]
        k = pl.program_id(2)
        if nc:
            steps = _ag_steps if kind == "ag" else _rs_steps
            start, finish = steps(refs[base - nc:base], refs[base + no:base + no + nc], *refs[base + no + nc + nb:])
            i, j = pl.program_id(0), pl.program_id(1)
            pl.when((i == 0) & (j == 0) & (k == 0))(start)

        @pl.when(k == 0)
        def _():
            for acc in accs:
                acc[...] = jnp.zeros_like(acc)

        av = a_ref[...].astype(BF16)
        for b_ref, acc in zip(b_refs, accs):
            acc[...] += lax.dot_general(av, b_ref[...].astype(BF16), dims, preferred_element_type=F32)

        @pl.when(k == nk - 1)
        def _():
            res = epi(*[acc[...] for acc in accs], *[e[...] for e in e_refs])
            for o_ref, v in zip(o_refs, res):
                o_ref[...] = v.astype(o_ref.dtype)

        if nc:
            pl.when((i == M // tm - 1) & (j == N // tn - 1) & (k == nk - 1))(finish)

    a_spec = pl.BlockSpec((tm, tk), lambda i, j, k: (i, k))
    if mode == "nn":
        if b_stack:
            nps = cs // tn
            b_spec = pl.BlockSpec((None, tk, tn), lambda i, j, k: (j // nps, k, j % nps))
        else:
            b_spec = pl.BlockSpec((tk, tn), lambda i, j, k: (k, j))
    elif b_stack:
        kps = cs // tk
        b_spec = pl.BlockSpec((None, tn, tk), lambda i, j, k: (k // kps, j, k % kps))
    else:
        b_spec = pl.BlockSpec((tn, tk), lambda i, j, k: (j, k))
    o_spec = pl.BlockSpec((tm, tn), lambda i, j, k: (i, j))
    sems = (_ag_sems(nc) if kind == "ag" else _rs_sems(nc)) if nc else []
    res = _pcall(
        body, name=name, grid=(M // tm, N // tn, nk), in_specs=[a_spec] + [b_spec] * nb + [o_spec] * ne + [_HBM] * nc,
        out_specs=[o_spec] * no + [_HBM] * nc,
        out_shape=[jax.ShapeDtypeStruct((M, N), dt) for dt in out_dtypes]
        + [jax.ShapeDtypeStruct(z.shape, z.dtype) for z in carried],
        input_output_aliases={1 + nb + ne + q: no + q for q in range(nc)} if kind == "ag" else {},
        scratch_shapes=[pltpu.VMEM((tm, tn), F32)] * nb + sems,
        compiler_params=_cparams(("arbitrary",) * 3 if nc else ("parallel", "parallel", "arbitrary")),
    )(a, *bs, *extras, *carried)
    return (list(res[:no]), list(res[no:])) if nc else res


def rowwise(fn, row_ins, par_ins, outs, accs=(), *, tm, ncb=1, name="rowwise"):
    R = row_ins[0][0].shape[0]
    assert R % tm == 0, (R, tm)
    nrb = R // tm
    in_specs, args = [], []
    for spec in row_ins:
        arr, w, base = spec[:3]
        mod = spec[3] if len(spec) > 3 else None
        cstep = 0 if (len(spec) > 4 and spec[4]) else 1
        halo = spec[5] if len(spec) > 5 else None
        if halo == "prev":
            in_specs.append(pl.BlockSpec((8, w), lambda j, i, base=base: (jnp.maximum(i * (tm // 8) - 1, 0), base + j)))
        elif halo == "next":
            in_specs.append(pl.BlockSpec((8, w), lambda j, i, base=base: (jnp.minimum((i + 1) * (tm // 8), R // 8 - 1), base + j)))
        elif mod is None:
            in_specs.append(pl.BlockSpec((tm, w), lambda j, i, base=base, cstep=cstep: (i, base + cstep * j)))
        else:
            in_specs.append(pl.BlockSpec((tm, w), lambda j, i, base=base, mod=mod, cstep=cstep: (i % mod, base + cstep * j)))
        args.append(arr)
    for arr, w, base in par_ins:
        in_specs.append(pl.BlockSpec((arr.shape[0], w), lambda j, i, base=base: (0, base + j)))
        args.append(arr)
    out_specs, out_shape = [], []
    for cols, w, dt in outs:
        out_specs.append(pl.BlockSpec((tm, w), lambda j, i: (i, j)))
        out_shape.append(jax.ShapeDtypeStruct((R, cols), dt))
    for p, cols, w in accs:
        out_specs.append(pl.BlockSpec((p, w), lambda j, i: (0, j)))
        out_shape.append(jax.ShapeDtypeStruct((p, cols), F32))
    nin, nout, nacc = len(args), len(outs), len(accs)

    def body(*refs):
        vals = [r[...] for r in refs[:nin]]
        res = fn(*vals)
        if not isinstance(res, (tuple, list)):
            res = (res,)
        assert len(res) == nout + nacc, (len(res), nout, nacc)
        for o_ref, v in zip(refs[nin:nin + nout], res[:nout]):
            o_ref[...] = v.astype(o_ref.dtype)
        if nacc:
            i = pl.program_id(1)

            @pl.when(i == 0)
            def _():
                for a_ref in refs[nin + nout:]:
                    a_ref[...] = jnp.zeros_like(a_ref)

            for a_ref, v in zip(refs[nin + nout:], res[nout:]):
                a_ref[...] += v.astype(F32)

    r = _pcall(
        body, name=name, grid=(ncb, nrb), in_specs=in_specs, out_specs=out_specs, out_shape=out_shape,
        compiler_params=_cparams(("parallel", "arbitrary")),
    )(*args)
    return r


def vjp_fn(fwd, nprim):
    def f(*vals):
        prim, cts = vals[:nprim], vals[nprim:]
        out, pull = jax.vjp(fwd, *[p.astype(F32) for p in prim])
        if not isinstance(out, (tuple, list)):
            cts = cts[0].astype(F32)
        else:
            cts = tuple(c.astype(F32) for c in cts)
        return pull(cts)
    return f


WKV_TB = 16


def wkv_fwd(w, kn, b, k, r, v, bufs=()):
    T, N, L = w.shape
    NI = v.shape[1]
    tb = WKV_TB
    n = len(bufs)
    assert T % tb == 0 and L == v.shape[2]

    def body(*refs):
        w_ref, kn_ref, b_ref, k_ref, r_ref, v_ref = refs[:6]
        y_ref, sp_ref, sa_ref = refs[6 + n:9 + n]
        s_ref = refs[9 + 2 * n]
        if n:
            start, finish = _ag_steps(refs[6:6 + n], refs[9 + n:9 + 2 * n], *refs[10 + 2 * n:])
            pl.when(pl.program_id(0) == 0)(start)

        @pl.when(pl.program_id(0) == 0)
        def _():
            s_ref[...] = jnp.zeros_like(s_ref)

        def step(s, carry):
            W, KN, B, Kk, Rr = w_ref[s], kn_ref[s], b_ref[s], k_ref[s], r_ref[s]
            for i in range(NI):
                S = s_ref[i]
                sp_ref[s, i] = S
                sa = jnp.sum(S * KN, axis=0, keepdims=True)
                sa_ref[s, pl.ds(i, 1), :] = sa
                vi = v_ref[s, pl.ds(i, 1), :]
                Sn = S * W + sa * B + vi * Kk
                s_ref[i] = Sn
                y_ref[s, pl.ds(i, 1), :] = jnp.sum(Sn * Rr, axis=0, keepdims=True)
            return carry

        lax.fori_loop(0, tb, step, 0)
        if n:
            pl.when(pl.program_id(0) == T // tb - 1)(finish)

    jspec = pl.BlockSpec((tb, N, L), lambda t: (t, 0, 0))
    ispec = pl.BlockSpec((tb, NI, L), lambda t: (t, 0, 0))
    ishape = jax.ShapeDtypeStruct((T, NI, L), F32)
    res = _pcall(
        body, name="wkv_fwd", grid=(T // tb,), in_specs=[jspec] * 5 + [ispec] + [_HBM] * n,
        out_specs=[ispec, pl.BlockSpec((tb, NI, N, L), lambda t: (t, 0, 0, 0)), ispec] + [_HBM] * n,
        out_shape=[ishape, jax.ShapeDtypeStruct((T, NI, N, L), F32), ishape]
        + [jax.ShapeDtypeStruct(z.shape, z.dtype) for z in bufs],
        input_output_aliases={6 + i: 3 + i for i in range(n)},
        scratch_shapes=[pltpu.VMEM((NI, N, L), F32)] + (_ag_sems(n) if n else []),
        compiler_params=_cparams(("arbitrary",)),
    )(w, kn, b, k, r, v, *bufs)
    return res[0], res[1], res[2], list(res[3:])


def wkv_bwd(w, kn, b, k, r, v, dy, sp, sa):
    T, N, L = w.shape
    NI, LH = v.shape[1], L // 2
    tb = WKV_TB
    nt = T // tb

    def body(w_ref, kn_ref, b_ref, k_ref, r_ref, v_ref, dy_ref, sp_ref, sa_ref,
             dw_ref, dkn_ref, db_ref, dk_ref, dr_ref, dv_ref, ds_ref):
        @pl.when(pl.program_id(0) == 0)
        def _():
            ds_ref[...] = jnp.zeros_like(ds_ref)

        def step(q, carry):
            s = tb - 1 - q
            W, KN, B, Kk, Rr = w_ref[s], kn_ref[s], b_ref[s], k_ref[s], r_ref[s]
            dW = jnp.zeros((N, L), F32)
            dKN, dB, dK, T1 = dW, dW, dW, dW
            al = jnp.zeros((1, L), F32)
            be = al
            for i in range(NI):
                Sp = sp_ref[s, i]
                vi = v_ref[s, pl.ds(i, 1), :]
                dyi = dy_ref[s, pl.ds(i, 1), :]
                sai = sa_ref[s, pl.ds(i, 1), :]
                dS = ds_ref[i] + dyi * Rr
                T1 = T1 + Sp * dyi
                al = al + sai * dyi
                be = be + vi * dyi
                dv_ref[s, pl.ds(i, 1), :] = jnp.sum(dS * Kk, axis=0, keepdims=True)
                dK = dK + dS * vi
                dsa = jnp.sum(dS * B, axis=0, keepdims=True)
                dB = dB + dS * sai
                dW = dW + dS * Sp
                dKN = dKN + Sp * dsa
                ds_ref[i] = dS * W + dsa * KN
            dR = W * T1 + B * al + Kk * be
            for ref, val in ((dw_ref, dW), (dkn_ref, dKN), (db_ref, dB), (dk_ref, dK), (dr_ref, dR)):
                ref[s] = (val + pltpu.roll(val, LH, 1))[:, :LH]
            return carry

        lax.fori_loop(0, tb, step, 0)

    jspec = pl.BlockSpec((tb, N, L), lambda t: (nt - 1 - t, 0, 0))
    gspec = pl.BlockSpec((tb, N, LH), lambda t: (nt - 1 - t, 0, 0))
    ispec = pl.BlockSpec((tb, NI, L), lambda t: (nt - 1 - t, 0, 0))
    gshape = jax.ShapeDtypeStruct((T, N, LH), F32)
    return _pcall(
        body, name="wkv_bwd", grid=(nt,),
        in_specs=[jspec] * 5 + [ispec, ispec, pl.BlockSpec((tb, NI, N, L), lambda t: (nt - 1 - t, 0, 0, 0)), ispec],
        out_specs=[gspec] * 5 + [ispec], out_shape=[gshape] * 5 + [jax.ShapeDtypeStruct((T, NI, L), F32)],
        scratch_shapes=[pltpu.VMEM((NI, N, L), F32)],
        compiler_params=_cparams(("arbitrary",)),
    )(w, kn, b, k, r, v, dy, sp, sa)


_NT = (((1,), (1,)), ((), ()))
_TN = (((0,), (0,)), ((), ()))
ATT_SCALE = QK_DIM ** -0.5


def _att_bias():
    col = lax.broadcasted_iota(jnp.int32, (Q_BLOCK, Q_BLOCK), 1)
    row = lax.broadcasted_iota(jnp.int32, (Q_BLOCK, Q_BLOCK), 0)
    return jnp.where(col < N_META, 0.0, -1e30).astype(F32), jnp.where(col <= row, 0.0, -1e30).astype(F32)


def _att_scores(q1, q2, kn_ref, kp_ref, s_ref, bias, L):
    s = lax.dot_general(q1, kn_ref[0, :L, :], _NT, preferred_element_type=F32)
    s = s + lax.dot_general(q2, kp_ref[0, :L, :], _NT, preferred_element_type=F32)
    s_ref[:, :L] = s * ATT_SCALE
    s_ref[:, :Q_BLOCK] += bias[0]
    s_ref[:, L - Q_BLOCK:L] += bias[1]


def _att_probs(s_ref, L):
    s = s_ref[:, :L]
    m = jnp.max(s, axis=-1, keepdims=True)
    p = jnp.exp(s - m)
    return p / jnp.sum(p, axis=-1, keepdims=True)


def _att_blocks(nq, qn_ref, qp_ref, kn_ref, kp_ref, s_ref):
    bias = _att_bias()
    rows = lambda i: pl.ds(Q_BLOCK * i, Q_BLOCK)
    L = lambda i: Q_BLOCK * (i + 2)
    score = lambda i: _att_scores(qn_ref[0, rows(i), :], qp_ref[0, rows(i), :], kn_ref, kp_ref, s_ref.at[i % 2], bias, L(i))
    score(0)
    for i in range(nq):
        if i + 1 < nq:
            score(i + 1)
        yield i, rows(i), L(i), _att_probs(s_ref.at[i % 2], L(i))


def _edge_steps(nb, nh):
    b, h = pl.program_id(0), pl.program_id(1)
    return (b == 0) & (h == 0), (b == nb - 1) & (h == nh - 1)


def attn_fwd(qn, qp, kn, kp, v, bufs=()):
    B, S, HD = qn.shape
    SP = kn.shape[1]
    H = HD // LANE
    nq = S // Q_BLOCK
    n = len(bufs)

    def body(*refs):
        qn_ref, qp_ref, kn_ref, kp_ref, v_ref = refs[:5]
        o_ref = refs[5 + n]
        s_ref = refs[6 + 2 * n]
        if n:
            start, finish = _ag_steps(refs[5:5 + n], refs[6 + n:6 + 2 * n], *refs[7 + 2 * n:])
            is_first, is_last = _edge_steps(B, H)
            pl.when(is_first)(start)
        for i, rows, L, p in _att_blocks(nq, qn_ref, qp_ref, kn_ref, kp_ref, s_ref):
            o_ref[0, rows, :] = jnp.dot(p.astype(BF16), v_ref[0, :L, :], preferred_element_type=F32)
        if n:
            pl.when(is_last)(finish)

    qspec = pl.BlockSpec((1, S, LANE), lambda b, h: (b, 0, h))
    kspec = pl.BlockSpec((1, SP, LANE), lambda b, h: (b, 0, h))
    pspec = pl.BlockSpec((1, SP, LANE), lambda b, h: (b, 0, 0))
    res = _pcall(
        body, name="attn_fwd", grid=(B, H), in_specs=[qspec, qspec, kspec, pspec, kspec] + [_HBM] * n,
        out_specs=[qspec] + [_HBM] * n,
        out_shape=[jax.ShapeDtypeStruct((B, S, HD), F32)] + [jax.ShapeDtypeStruct(w.shape, w.dtype) for w in bufs],
        input_output_aliases={5 + i: 1 + i for i in range(n)},
        scratch_shapes=[pltpu.VMEM((2, Q_BLOCK, SP), F32)] + (_ag_sems(n) if n else []),
        compiler_params=_cparams(("arbitrary", "arbitrary")),
    )(qn, qp, kn, kp, v, *bufs)
    return res[0], list(res[1:])


def attn_bwd(qn, qp, kn, kp, v, do, hs=()):
    B, S, HD = qn.shape
    SP = kn.shape[1]
    H = HD // LANE
    nq = S // Q_BLOCK
    n = len(hs)

    def body(*refs):
        qn_ref, qp_ref, kn_ref, kp_ref, v_ref, do_ref = refs[:6]
        dqn_ref, dqp_ref, dkn_ref, dv_ref, dkp_ref = refs[6 + n:11 + n]
        dkn_acc, dv_acc, s_ref = refs[11 + 2 * n:14 + 2 * n]
        if n:
            start, finish = _rs_steps(refs[6:6 + n], refs[11 + n:11 + 2 * n], *refs[14 + 2 * n:])
            is_first, is_last = _edge_steps(B, H)
            pl.when(is_first)(start)

        @pl.when(pl.program_id(1) == 0)
        def _():
            dkp_ref[...] = jnp.zeros_like(dkp_ref)

        dkn_acc[...] = jnp.zeros_like(dkn_acc)
        dv_acc[...] = jnp.zeros_like(dv_acc)
        for i, rows, L, p in _att_blocks(nq, qn_ref, qp_ref, kn_ref, kp_ref, s_ref):
            q1, q2, do_i = qn_ref[0, rows, :], qp_ref[0, rows, :], do_ref[0, rows, :]
            dp = lax.dot_general(do_i, v_ref[0, :L, :], _NT, preferred_element_type=F32)
            ds = (p * (dp - jnp.sum(p * dp, axis=-1, keepdims=True)) * ATT_SCALE).astype(BF16)
            dqn_ref[0, rows, :] = jnp.dot(ds, kn_ref[0, :L, :], preferred_element_type=F32).astype(dqn_ref.dtype)
            dqp_ref[0, rows, :] = jnp.dot(ds, kp_ref[0, :L, :], preferred_element_type=F32)
            dkn_acc[:L, :] += lax.dot_general(ds, q1, _TN, preferred_element_type=F32)
            dkp_ref[0, :L, :] += lax.dot_general(ds, q2, _TN, preferred_element_type=F32)
            dv_acc[:L, :] += lax.dot_general(p.astype(BF16), do_i, _TN, preferred_element_type=F32)
        dkn_ref[0] = dkn_acc[...].astype(dkn_ref.dtype)
        dv_ref[0] = dv_acc[...].astype(dv_ref.dtype)
        if n:
            pl.when(is_last)(finish)

    qspec = pl.BlockSpec((1, S, LANE), lambda b, h: (b, 0, h))
    kspec = pl.BlockSpec((1, SP, LANE), lambda b, h: (b, 0, h))
    pspec = pl.BlockSpec((1, SP, LANE), lambda b, h: (b, 0, 0))
    res = _pcall(
        body, name="attn_bwd", grid=(B, H), in_specs=[qspec, qspec, kspec, pspec, kspec, qspec] + [_HBM] * n,
        out_specs=[qspec, qspec, kspec, kspec, pspec] + [_HBM] * n,
        out_shape=[jax.ShapeDtypeStruct((B, S, HD), BF16), jax.ShapeDtypeStruct((B, S, HD), F32),
                   jax.ShapeDtypeStruct((B, SP, HD), BF16), jax.ShapeDtypeStruct((B, SP, HD), BF16),
                   jax.ShapeDtypeStruct((B, SP, LANE), F32)] + [jax.ShapeDtypeStruct(h.shape, h.dtype) for h in hs],
        scratch_shapes=[pltpu.VMEM((SP, LANE), F32), pltpu.VMEM((SP, LANE), F32), pltpu.VMEM((2, Q_BLOCK, SP), F32)]
        + (_rs_sems(n) if n else []),
        compiler_params=_cparams(("arbitrary", "arbitrary")),
    )(qn, qp, kn, kp, v, do, *hs)
    return res[:5], list(res[5:])


_HBM = pl.BlockSpec(memory_space=pltpu.HBM)


def _place():
    x, y, c = lax.axis_index("x"), lax.axis_index("y"), lax.axis_index("c")
    chips = [(1 - x, y), (x, 1 - y), (1 - x, 1 - y)]
    return x, y, c, chips


def _rcopy(src, dst, ssem, rsem, dev):
    return pltpu.make_async_remote_copy(src_ref=src, dst_ref=dst, send_sem=ssem, recv_sem=rsem,
                                        device_id=dev, device_id_type=MESH)


def _half_rows(c, r):
    return pl.ds(pl.multiple_of(c * (r // 2), 16), r // 2)


def _ag_steps(w, out, ssem, rsem):
    n = len(w)
    x, y, c, chips = _place()
    s = 2 * x + y
    rows = [_half_rows(c, w[i].shape[1]) for i in range(n)]
    orows = [_half_rows(1 - c, w[i].shape[1]) for i in range(n)]
    first = [_rcopy(w[i].at[s, rows[i]], out[i].at[s, rows[i]], ssem.at[6 * i + j], rsem.at[6 * i + j], (px, py, c))
             for i in range(n) for j, (px, py) in enumerate(chips)]

    def start():
        for cp in first:
            cp.start()

    def finish():
        passed = []
        for j, (px, py) in enumerate(chips):
            sp = 2 * px + py
            for i in range(n):
                here = out[i].at[sp, rows[i]]
                _rcopy(here, here, ssem.at[6 * i + j], rsem.at[6 * i + j], (px, py, c)).wait_recv()
                fw = _rcopy(here, here, ssem.at[6 * i + 3 + j], rsem.at[6 * i + 3 + j], (x, y, 1 - c))
                fw.start()
                passed.append(fw)
        for j, (px, py) in enumerate(chips):
            sp = 2 * px + py
            for i in range(n):
                there = out[i].at[sp, orows[i]]
                _rcopy(there, there, ssem.at[6 * i + 3 + j], rsem.at[6 * i + 3 + j], (x, y, 1 - c)).wait_recv()
        for cp in first + passed:
            cp.wait_send()

    return start, finish


def _ag_sems(n):
    return [pltpu.SemaphoreType.DMA((6 * n,)), pltpu.SemaphoreType.DMA((6 * n,))]


def ag_weights(bufs):
    n = len(bufs)

    def body(*refs):
        start, finish = _ag_steps(refs[:n], refs[n:2 * n], *refs[2 * n:])
        start()
        finish()

    return _pcall(
        body, name="ag_weights", in_specs=[_HBM] * n, out_specs=[_HBM] * n,
        out_shape=[jax.ShapeDtypeStruct(w.shape, w.dtype) for w in bufs],
        input_output_aliases={i: i for i in range(n)}, scratch_shapes=_ag_sems(n),
    )(*bufs)


def swap_halves(gs, name):
    n = len(gs)

    def body(*refs):
        g, out = refs[:n], refs[n:2 * n]
        ssem, rsem = refs[2 * n:]
        x, y, c, _ = _place()
        cps = [_rcopy(g[i].at[:, _half_rows(1 - c, g[i].shape[1])], out[i], ssem.at[i], rsem.at[i], (x, y, 1 - c))
               for i in range(n)]
        for cp in cps:
            cp.start()
        for cp in cps:
            cp.wait()

    return _pcall(
        body, name=name, in_specs=[_HBM] * n, out_specs=[_HBM] * n,
        out_shape=[jax.ShapeDtypeStruct((4, g.shape[1] // 2, g.shape[2]), g.dtype) for g in gs],
        scratch_shapes=[pltpu.SemaphoreType.DMA((n,)), pltpu.SemaphoreType.DMA((n,))],
    )(*gs)


def _rs_steps(h, out, ssem, rsem):
    n = len(h)
    x, y, c, chips = _place()
    s = 2 * x + y
    cps = [_rcopy(h[i].at[2 * px + py], out[i].at[s], ssem.at[3 * i + j], rsem.at[3 * i + j], (px, py, c))
           for i in range(n) for j, (px, py) in enumerate(chips)]

    def start():
        for cp in cps:
            cp.start()

    def finish():
        for j, (px, py) in enumerate(chips):
            for i in range(n):
                _rcopy(h[i].at[s], out[i].at[2 * px + py], ssem.at[3 * i + j], rsem.at[3 * i + j], (px, py, c)).wait_recv()
        for cp in cps:
            cp.wait_send()

    return start, finish


def _rs_sems(n):
    return [pltpu.SemaphoreType.DMA((3 * n,)), pltpu.SemaphoreType.DMA((3 * n,))]


def share_sibling(ts):
    n = len(ts)

    def body(*refs):
        t, out = refs[:n], refs[n:2 * n]
        ssem, rsem = refs[2 * n:]
        x, y, c, _ = _place()
        cps = [_rcopy(t[i].at[c], out[i].at[c], ssem.at[i], rsem.at[i], (x, y, 1 - c)) for i in range(n)]
        for cp in cps:
            cp.start()
        for i in range(n):
            _rcopy(t[i].at[c], out[i].at[1 - c], ssem.at[i], rsem.at[i], (x, y, 1 - c)).wait_recv()
        for cp in cps:
            cp.wait_send()

    return _pcall(
        body, name="share_sibling", in_specs=[_HBM] * n, out_specs=[_HBM] * n,
        out_shape=[jax.ShapeDtypeStruct(t.shape, t.dtype) for t in ts],
        input_output_aliases={i: i for i in range(n)},
        scratch_shapes=[pltpu.SemaphoreType.DMA((n,)), pltpu.SemaphoreType.DMA((n,))],
    )(*ts)


def allreduce8(v, name):
    P, W = v.shape

    def body(v_ref, out_ref, buf, ssem, rsem):
        x, y, c, _ = _place()
        me = 4 * x + 2 * y + c
        buf[me] = v_ref[...]
        cps = []
        for k in range(1, 8):
            px = 1 - x if k & 4 else x
            py = 1 - y if k & 2 else y
            pc = 1 - c if k & 1 else c
            cp = _rcopy(buf.at[me], buf.at[me], ssem.at[k - 1], rsem.at[k - 1], (px, py, pc))
            cp.start()
            cps.append((cp, 4 * px + 2 * py + pc))
        for k, (cp, peer) in enumerate(cps):
            _rcopy(buf.at[me], buf.at[peer], ssem.at[k], rsem.at[k], (x, y, c)).wait_recv()
        for cp, _ in cps:
            cp.wait_send()
        acc = buf[0]
        for d in range(1, 8):
            acc = acc + buf[d]
        out_ref[...] = acc

    return _pcall(
        body, name=name, in_specs=[pl.BlockSpec(memory_space=pltpu.VMEM)],
        out_specs=pl.BlockSpec(memory_space=pltpu.VMEM), out_shape=jax.ShapeDtypeStruct((P, W), F32),
        scratch_shapes=[pltpu.VMEM((8, P, W), F32), pltpu.SemaphoreType.DMA((7,)), pltpu.SemaphoreType.DMA((7,))],
    )(v)


def add_half(g, rcv, cidx, name):
    _, hr, W = rcv.shape
    tr = _rows_tile(hr, W)
    nb = hr // tr

    def body(c_ref, g_ref, r_ref, o_ref):
        o_ref[...] = (g_ref[...] + r_ref[...]).astype(o_ref.dtype)

    return _pcall(
        body, name=name,
        grid_spec=pltpu.PrefetchScalarGridSpec(
            num_scalar_prefetch=1, grid=(4, nb),
            in_specs=[pl.BlockSpec((1, tr, W), lambda s, i, c: (s, c[0] * nb + i, 0)),
                      pl.BlockSpec((1, tr, W), lambda s, i, c: (s, i, 0))],
            out_specs=pl.BlockSpec((1, tr, W), lambda s, i, c: (s, i, 0))),
        out_shape=jax.ShapeDtypeStruct((4, hr, W), BF16), compiler_params=_cparams(("parallel", "parallel")),
    )(cidx, g, rcv)


def sum_chips(p, h, sc, name):
    _, hr, W = p.shape
    tr = _rows_tile(hr, W)

    def body(s_ref, c_ref, p_ref, h_ref, o_ref):
        s = s_ref[0]
        own = h_ref[0]
        f = lambda k: jnp.where(s == k, own, p_ref[k]).astype(F32)
        o_ref[0] = ((f(0) + f(1)) + f(2)) + f(3)

    return _pcall(
        body, name=name,
        grid_spec=pltpu.PrefetchScalarGridSpec(
            num_scalar_prefetch=2, grid=(hr // tr,),
            in_specs=[pl.BlockSpec((4, tr, W), lambda i, s, c: (0, i, 0)),
                      pl.BlockSpec((1, tr, W), lambda i, s, c: (s[0], i, 0))],
            out_specs=pl.BlockSpec((1, tr, W), lambda i, s, c: (c[0], i, 0))),
        out_shape=jax.ShapeDtypeStruct((2, hr, W), F32), compiler_params=_cparams(("parallel",)),
    )(sc[0], sc[1], p, h)


HI = lax.Precision.HIGHEST
BLOCK_BYTES = 3 * 512 * 1024


def _rows_tile(R, w, T=None):
    cands = [t for t in range(16, R + 1, 16) if R % t == 0 and (T is None or T % t == 0)]
    ok = [t for t in cands if t * w * 4 <= BLOCK_BYTES]
    return max(ok) if ok else min(cands)


def _gsum_exact(x):
    r = jnp.right_shift(lax.broadcasted_iota(jnp.int32, (LANE, LANE), 0), 6)
    c = jnp.right_shift(lax.broadcasted_iota(jnp.int32, (LANE, LANE), 1), 6)
    g = (r == c).astype(BF16)
    x1 = x.astype(BF16)
    r1 = x - x1.astype(F32)
    x2 = r1.astype(BF16)
    x3 = (r1 - x2.astype(F32)).astype(BF16)
    dot = lambda z: jnp.dot(z, g, preferred_element_type=F32)
    return (dot(x3) + dot(x2)) + dot(x1)


@jax.custom_vjp
def _gsum(x):
    return _gsum_exact(x)


_gsum.defvjp(lambda x: (_gsum_exact(x), None), lambda _, ct: (_gsum_exact(ct),))


def _rms(x, g):
    return x * lax.rsqrt(jnp.mean(x * x, axis=-1, keepdims=True) + NORM_EPS) * g


def _silu_mul(gate, up):
    return jax.nn.silu(gate) * up


def _shift(p, prev, mu):
    return p + mu * (prev - p)


def _prev_rows(p, above, tpos):
    first = lax.broadcasted_iota(jnp.int32, p.shape, 0) == 0
    prev = jnp.where(first, above[7:8, :], pltpu.roll(p, 1, 0))
    return jnp.where(tpos[:, :1] == 0.0, 0.0, prev)


def _shift_rows(p, above, tpos, mu):
    return _shift(p, _prev_rows(p, above, tpos), mu)


def _shift_rows_bwd(t_last, p, above, tpos, mu, ct, ct_below):
    dmu = jnp.sum(ct * (_prev_rows(p, above, tpos) - p), axis=0, keepdims=True)
    last = lax.broadcasted_iota(jnp.int32, p.shape, 0) == p.shape[0] - 1
    nxt = jnp.where(last, ct_below[0:1, :], pltpu.roll(ct, p.shape[0] - 1, 0))
    nxt = jnp.where(tpos[:, :1] == t_last, 0.0, nxt)
    return (1.0 - mu) * ct + mu * nxt, dmu


def _lora_act(p, prev, mu):
    s = _shift(p, prev, mu)
    return jax.nn.sigmoid(s[:, :G_LORA]), jnp.tanh(s[:, G_LORA:G_LORA + LANE]), s[:, G_LORA + LANE:]


def _prep(k, lw, la, w0, a0, kk_w, ka_w):
    wpre = -jax.nn.softplus(-(w0 + lw)) - 0.5
    decay = jnp.exp(-jnp.exp(wpre))
    a = jax.nn.sigmoid(a0 + la)
    kk = k * kk_w
    kk = kk * lax.rsqrt(jnp.maximum(_gsum(kk * kk), 1e-24))
    k2 = k * (1.0 + (a - 1.0) * ka_w)
    return decay, -kk, kk * a, k2


def _post(y, r, k2, v, g, gnw, gnb, rk):
    mean = _gsum(y) * (1.0 / RWKV_HEAD)
    d = y - mean
    var = _gsum(d * d) * (1.0 / RWKV_HEAD)
    yn = d * lax.rsqrt(var + GN_EPS) * gnw + gnb
    bonus = _gsum(r * k2 * rk) * v
    return (yn + bonus) * g


def _gate_mix(ga, gb, ya, o):
    return jax.nn.sigmoid(ga) * ya + jax.nn.sigmoid(gb) * o


def _rope(x, cos, sin):
    return x * cos + pltpu.roll(x, LANE // 2, 1) * sin


def _rope_t(dy, cos, sin):
    return dy * cos + pltpu.roll(dy * sin, LANE // 2, 1)


def _mla_pre(cq, ckv, kpe, cos, sin, qw, kvw):
    return _rms(cq, qw), _rms(ckv, kvw), _rope(kpe, cos, sin)


def _mla_pre_bwd(cq, ckv, dcqn, dckvn, dkr, cos, sin, qw, kvw):
    _, pull = jax.vjp(lambda a, b, c, d: (_rms(a, c), _rms(b, d)), cq, ckv, qw, kvw)
    dcq, dckv, dqw, dkvw = pull((dcqn, dckvn))
    return dcq, dckv, _rope_t(dkr, cos, sin), dqw, dkvw


WEIGHTS = ['meta_tokens', 'ffn1_norm', 'ffn1_w_gate', 'ffn1_w_up', 'ffn1_w_down', 'mix_norm', 'w_in', 'tm_mu', 'w0',
           'w_up', 'a0', 'a_up', 'g_up', 'k_k', 'k_a', 'r_k', 'gn_w', 'gn_b', 'q_norm', 'w_uq', 'kv_norm', 'w_ukv',
           'w_out', 'ffn2_norm', 'ffn2_w_gate', 'ffn2_w_up', 'ffn2_w_down', 'final_norm']
SHARD_AXIS = {'meta_tokens': 1, 'ffn1_w_gate': 1, 'ffn1_w_up': 1, 'ffn1_w_down': 0, 'w_in': 1, 'w_up': 1, 'a_up': 1,
              'g_up': 1, 'w_uq': 1, 'w_ukv': 1, 'w_out': 0, 'ffn2_w_gate': 1, 'ffn2_w_up': 1, 'ffn2_w_down': 0}
GATHERED = [n for n in WEIGHTS if n in SHARD_AXIS and n != 'meta_tokens']
FFN_IN = ('ffn1_w_gate', 'ffn1_w_up', 'ffn2_w_gate', 'ffn2_w_up')
PART_B = ['ffn2_w_gate', 'ffn2_w_up', 'ffn2_w_down', 'w_out']
PART_A1 = ['ffn1_w_gate', 'ffn1_w_up', 'ffn1_w_down']
PART_A2 = [n for n in GATHERED if n not in PART_B and n not in PART_A1]
SMALL = [n for n in WEIGHTS if n not in SHARD_AXIS]


def _to2d(a):
    if a.ndim == 1:
        return a.reshape(1, -1)
    if a.ndim == 3:
        return a.reshape(a.shape[0] * a.shape[1], a.shape[2]) if a.shape[0] == 1 and a.shape[1] > 64 else a.reshape(1, -1)
    return a


def _unpack(flat, shapes):
    out, off = [], 0
    for shp in shapes:
        n = shp[0] * shp[1]
        out.append(flat[off:off + n].reshape(shp))
        off += n
    return out


def _adamw(w, g, m, v):
    m = ADAM_B1 * m + (1.0 - ADAM_B1) * g
    v = ADAM_B2 * v + (1.0 - ADAM_B2) * jnp.square(g)
    m_hat = m / (1.0 - ADAM_B1 ** ADAM_STEP)
    v_hat = v / (1.0 - ADAM_B2 ** ADAM_STEP)
    delta = -ADAM_LR * (m_hat / (jnp.sqrt(v_hat) + ADAM_EPS) + ADAM_WD * w)
    return delta, m, v


def adamw(w, g, m, v, name):
    R, C = w.shape
    if R % 16 == 0 and R > 16:
        tm = _rows_tile(R, C)
    else:
        tm = R
    return rowwise(_adamw, [(w, C, 0), (g, C, 0), (m, C, 0), (v, C, 0)], [], [(C, C, F32)] * 3, tm=tm, name=name)


def _step(a):
    x = a['x']
    Bl, S, D = x.shape
    T = S + N_META
    R, RS = Bl * T, Bl * S
    Hr, Hm = D // RWKV_HEAD, D // LANE
    w2 = {n: _to2d(a[n]) for n in WEIGHTS}
    m2 = {n: _to2d(a['m_' + n]) for n in WEIGHTS}
    v2 = {n: _to2d(a['v_' + n]) for n in WEIGHTS}
    xi, yi, ci = lax.axis_index("x"), lax.axis_index("y"), lax.axis_index("c")
    chip = 2 * xi + yi

    i0 = jnp.zeros((), jnp.int32)
    chip32 = chip.astype(jnp.int32)

    def own_slot(w):
        return lax.dynamic_update_slice(lax.empty((4,) + w.shape, BF16), w.astype(BF16)[None], (chip32, i0, i0))

    stk = dict(zip(PART_A1, ag_weights([own_slot(w2[n]) for n in PART_A1])))

    def unstack(z, axis):
        return z.reshape(4 * z.shape[1], z.shape[2]) if axis == 0 else jnp.concatenate([z[s] for s in range(4)], axis=1)

    full = {n: unstack(stk[n], SHARD_AXIS[n]) for n in PART_A1 if n not in FFN_IN}
    mt = w2['meta_tokens']
    mcols = mt.shape[1]
    mt_z = lax.dynamic_update_slice(jnp.zeros((N_META, D), F32), 0.5 * mt, (jnp.zeros((), jnp.int32), (chip * mcols).astype(jnp.int32)))
    meta_full = allreduce8(mt_z.reshape(-1, LANE), "gather_meta").reshape(N_META, D)

    F = 4 * stk['ffn1_w_gate'].shape[2]
    half = ROPE_DIM // 2
    tmu = w2['tm_mu']
    mu_a = tmu[:, :3 * D]
    zm = lambda n: jnp.zeros((1, n), F32)
    mu_b = jnp.concatenate([tmu[:, 3 * D + 2 * W_LORA:], tmu[:, 3 * D:3 * D + W_LORA], zm(LANE - W_LORA),
                            tmu[:, 3 * D + W_LORA:3 * D + 2 * W_LORA], zm(LANE - A_LORA)], axis=1)
    pos = jnp.arange(T, dtype=F32)
    inv_freq = 1.0 / (ROPE_THETA ** (jnp.arange(0, ROPE_DIM, 2, dtype=F32) / ROPE_DIM))
    ang = pos[:, None] * inv_freq[None, :]
    zt = jnp.zeros((T, half), F32)
    cos_t = jnp.concatenate([jnp.cos(ang), zt, jnp.cos(ang), zt], axis=1)
    sin_t = jnp.concatenate([-jnp.sin(ang), zt, jnp.sin(ang), zt], axis=1)
    cos_q, sin_q = cos_t[N_META:], sin_t[N_META:]

    t_full = _rows_tile(R, D)
    t_512 = _rows_tile(R, 512, T)
    t_128 = _rows_tile(R, LANE, T)
    tq_full = _rows_tile(RS, D)
    tq_128 = _rows_tile(RS, LANE, S)

    def rows3(z):
        return z.reshape(Bl, T, z.shape[-1])

    def real_rows(z):
        return rows3(z)[:, N_META:].reshape(RS, z.shape[-1])

    def pad_meta(z):
        z3 = z.reshape(Bl, S, z.shape[-1])
        return jnp.concatenate([jnp.zeros((Bl, N_META, z.shape[-1]), z.dtype), z3], axis=1).reshape(R, z.shape[-1])

    def shift_down(z):
        z3 = rows3(z)
        return jnp.concatenate([jnp.zeros((Bl, 1, z.shape[-1]), z.dtype), z3[:, :-1]], axis=1).reshape(R, z.shape[-1])

    def shift_up(z):
        z3 = rows3(z)
        return jnp.concatenate([z3[:, 1:], jnp.zeros((Bl, 1, z.shape[-1]), z.dtype)], axis=1).reshape(R, z.shape[-1])

    def ffn_fwd(h, nw, wg, wu, wd, tag, ag_bufs=()):
        n = rowwise(_rms, [(h, D, 0)], [(nw, D, 0)], [(D, D, BF16)], tm=t_full, name=tag + "_norm")[0]
        res = mm_epi(n, [wg, wu], "nn", lambda g_, u_: (g_, u_, _silu_mul(g_, u_)), [], [F32, F32, BF16], b_stack=True,
                     carry=("ag", ag_bufs) if ag_bufs else None, name=tag + "_gate_up")
        (gate, up, act), got = res if ag_bufs else (res, [])
        out = mm(act, wd, res=h, alpha=0.5, name=tag + "_down")
        return (out, (h, n, gate, up, act), got) if ag_bufs else (out, (h, n, gate, up, act))

    def exch(g, name):
        return add_half(g, swap_halves([g], "swap_halves_" + name)[0], cidx, "add_half_" + name)

    def ffn_bwd(dh2, saved, nw, wg, wu, wd, tag, rs_hs=(), rs_own=()):
        h, n, gate, up, act = saved
        dz = rowwise(lambda d: 0.5 * d, [(dh2, 512, 0)], [], [(D, 512, BF16)], tm=t_512, ncb=D // 512, name=tag + "_dz")[0]
        d_wd = mm(act, dz, "tn", name=tag + "_dwd")
        res = mm_epi(dz, [wd], "nt", lambda da, g_, u_: vjp_fn(_silu_mul, 2)(g_, u_, da), [gate, up], [BF16, BF16],
                     carry=("rs", rs_hs) if rs_hs else None, name=tag + "_dact")
        (dgate, dup), got = res if rs_hs else (res, [])
        own = {}
        if rs_own:
            h_wd = exch(stacked(d_wd, 0), rs_own[2])
            d_wg, p_wd = mm(n, dgate, "tn", out_stack=True, carry=("rs", [h_wd]), name=tag + "_dwg")
            d_wu = mm(n, dup, "tn", out_stack=True, name=tag + "_dwu")
            h_wg = exch(d_wg, rs_own[0])
            dn, p_wg = mm(dgate, wg, "nt", b_stack=True, carry=("rs", [h_wg]), name=tag + "_dn1")
            h_wu = exch(d_wu, rs_own[1])
            dn, p_wu = mm(dup, wu, "nt", res=dn, b_stack=True, carry=("rs", [h_wu]), name=tag + "_dn2")
            own = {rs_own[0]: (p_wg[0], h_wg), rs_own[1]: (p_wu[0], h_wu), rs_own[2]: (p_wd[0], h_wd)}
        else:
            d_wg = mm(n, dgate, "tn", out_stack=True, name=tag + "_dwg")
            d_wu = mm(n, dup, "tn", out_stack=True, name=tag + "_dwu")
            dn = mm(dgate, wg, "nt", b_stack=True, name=tag + "_dn1")
            dn = mm(dup, wu, "nt", res=dn, b_stack=True, name=tag + "_dn2")

        def f(h_, dn_, dh_, nw_):
            dh, dnw = vjp_fn(_rms, 2)(h_, nw_, dn_)
            return dh + dh_, dnw

        dh, d_nw = rowwise(f, [(h, D, 0), (dn, D, 0), (dh2, D, 0)], [(nw, D, 0)], [(D, D, F32)], [(1, D, D)],
                           tm=t_full, name=tag + "_dnorm")
        return (dh, d_nw, d_wg, d_wu, d_wd, got, own) if (rs_hs or rs_own) else (dh, d_nw, d_wg, d_wu, d_wd)

    h0 = jnp.concatenate([jnp.broadcast_to(meta_full[None], (Bl, N_META, D)), x], axis=1).reshape(R, D)
    h1, sv1, got_a2 = ffn_fwd(h0, w2['ffn1_norm'], stk['ffn1_w_gate'], stk['ffn1_w_up'], full['ffn1_w_down'], "ffn1",
                              [own_slot(w2[n]) for n in PART_A2])
    stk.update(zip(PART_A2, got_a2))
    full.update({n: unstack(stk[n], SHARD_AXIS[n]) for n in PART_A2})
    win = full['w_in']
    o = 3 * D
    c_xw, c_xa, c_xg = win[:, o:o + W_LORA], win[:, o + W_LORA:o + 2 * W_LORA], win[:, o + 2 * W_LORA:o + 2 * W_LORA + G_LORA]
    o += 2 * W_LORA + G_LORA
    c_cq, c_ckv, c_kpe = win[:, o:o + Q_LORA], win[:, o + Q_LORA:o + Q_LORA + KV_LORA], win[:, o + Q_LORA + KV_LORA:o + Q_LORA + KV_LORA + ROPE_DIM]
    o += Q_LORA + KV_LORA + ROPE_DIM
    c_ga, c_gb = win[:, o:o + D], win[:, o + D:o + 2 * D]
    zc = lambda n: jnp.zeros((D, n), BF16)
    half = ROPE_DIM // 2
    NP0 = 5 * D + 512 + Q_LORA + KV_LORA + LANE
    NP = -(-NP0 // 512) * 512
    win_p = jnp.concatenate([win[:, :3 * D], c_ga, c_gb, c_xg, c_xw, zc(LANE - W_LORA), c_xa, zc(LANE - A_LORA), c_cq, c_ckv,
                             c_kpe[:, :half], zc(half), c_kpe[:, half:], zc(half), zc(NP - NP0)], axis=1)
    O_GA, O_GB, O_L, O_CQ, O_CKV, O_KPE = 3 * D, 4 * D, 5 * D, 5 * D + 512, 5 * D + 512 + Q_LORA, 5 * D + 512 + Q_LORA + KV_LORA
    zr = lambda n: jnp.zeros((n, D), BF16)
    w_up_p = jnp.concatenate([full['w_up'], zr(LANE - W_LORA)], axis=0)
    a_up_p = jnp.concatenate([full['a_up'], zr(LANE - A_LORA)], axis=0)
    g_up = full['g_up']
    wuq = full['w_uq'].reshape(Q_LORA, Hm, QK_DIM)
    zq = jnp.zeros((Q_LORA, Hm, half), BF16)
    wqn = wuq[:, :, :NOPE_DIM].reshape(Q_LORA, Hm * LANE)
    wqp = jnp.concatenate([wuq[:, :, NOPE_DIM:NOPE_DIM + half], zq, wuq[:, :, NOPE_DIM + half:], zq], axis=2).reshape(Q_LORA, Hm * LANE)
    wukv = full['w_ukv'].reshape(KV_LORA, Hm, NOPE_DIM + V_DIM)
    wkn = wukv[:, :, :NOPE_DIM].reshape(KV_LORA, Hm * LANE)
    wv = wukv[:, :, NOPE_DIM:].reshape(KV_LORA, Hm * LANE)
    u = rowwise(_rms, [(h1, D, 0)], [(w2['mix_norm'], D, 0)], [(D, D, BF16)], tm=t_full, name="mix_norm")[0]
    proj = mm(u, win_p, name="proj")
    tpos = jnp.broadcast_to(pos[:, None], (T, LANE))
    tpos_in = (tpos, LANE, 0, T // t_512, True)
    prev_b = shift_down(proj[:, O_L:O_L + 512])
    nb = D // 512
    r_s, k_s, v_s = [
        rowwise(_shift_rows, [(proj, 512, sec * nb), (proj, 512, sec * nb, None, False, "prev"), tpos_in],
                [(mu_a, 512, sec * nb)], [(D, 512, F32)], tm=t_512, ncb=nb, name="shift_" + tag)[0]
        for sec, tag in enumerate("rkv")]
    sg, txw, xas = rowwise(_lora_act, [(proj, 512, O_L // 512), (prev_b, 512, 0)], [(mu_b, 512, 0)],
                           [(G_LORA, G_LORA, BF16), (LANE, LANE, BF16), (LANE, LANE, BF16)], tm=t_512, name="lora_act")
    lw = mm(txw, w_up_p, name="lora_w")
    la = mm(xas, a_up_p, name="lora_a")
    g = mm(sg, g_up, name="lora_g")
    hb = D // LANE
    par_d = lambda n: (w2[n], LANE, 0)
    decay, kn, bb, k2 = rowwise(_prep, [(k_s, LANE, 0), (lw, LANE, 0), (la, LANE, 0)],
                                [par_d('w0'), par_d('a0'), par_d('k_k'), par_d('k_a')], [(D, LANE, F32)] * 4,
                                tm=t_128, ncb=hb, name="wkv_prep")

    NS = Bl * Hr

    def t_major(z):
        return lax.optimization_barrier(z.reshape(Bl, T, D).transpose(1, 0, 2))

    def b_major(z):
        return lax.optimization_barrier(z).transpose(1, 0, 2).reshape(R, D)

    def to_j(z):
        z = t_major(z).reshape(T, NS, RWKV_HEAD).transpose(0, 2, 1)
        return jnp.broadcast_to(z[:, :, None, :], (T, RWKV_HEAD, 2, NS)).reshape(T, RWKV_HEAD, 2 * NS)

    def to_i(z):
        return t_major(z).reshape(T, NS, 2, RWKV_HEAD // 2).transpose(0, 3, 2, 1).reshape(T, RWKV_HEAD // 2, 2 * NS)

    def from_i(z):
        return b_major(z.reshape(T, RWKV_HEAD // 2, 2, NS).transpose(0, 3, 2, 1).reshape(T, Bl, D))

    def from_j(z):
        return b_major(z.transpose(0, 2, 1).reshape(T, Bl, D))

    jw, jkn, jb, jk, jr, iv = to_j(decay), to_j(kn), to_j(bb), to_j(k2), to_j(r_s), to_i(v_s)
    y_i, sp, sa_i, got_b = wkv_fwd(jw, jkn, jb, jk, jr, iv, [own_slot(w2[n]) for n in PART_B])
    stk.update(zip(PART_B, got_b))
    full.update({n: unstack(stk[n], SHARD_AXIS[n]) for n in PART_B if n not in FFN_IN})
    wout = full['w_out']
    y = from_i(y_i)
    post_rows = [(y, LANE, 0), (r_s, LANE, 0), (k2, LANE, 0), (v_s, LANE, 0), (g, LANE, 0)]
    post_pars = [par_d('gn_w'), par_d('gn_b'), par_d('r_k')]
    ya = rowwise(_post, post_rows, post_pars, [(D, LANE, F32)], tm=t_128, ncb=hb, name="wkv_post")[0]

    nt512 = T // t_512
    mla_rows = [(proj, Q_LORA, O_CQ // Q_LORA), (proj, KV_LORA, O_CKV // KV_LORA), (proj, LANE, O_KPE // LANE)]
    tabs = [(cos_t, LANE, 0, nt512, True), (sin_t, LANE, 0, nt512, True)]
    mla_pars = [(w2['q_norm'], Q_LORA, 0), (w2['kv_norm'], KV_LORA, 0)]
    cqn, ckvn, kpr = rowwise(_mla_pre, mla_rows + tabs, mla_pars,
                             [(Q_LORA, Q_LORA, BF16), (KV_LORA, KV_LORA, BF16), (LANE, LANE, BF16)], tm=t_512, name="mla_pre")
    cqn_r = real_rows(cqn)
    qn = mm(cqn_r, wqn, out_dtype=BF16, name="q_nope")
    qp_raw = mm(cqn_r, wqp, name="q_pe")
    ntq = S // tq_128
    qtabs = [(cos_q, LANE, 0, ntq, True), (sin_q, LANE, 0, ntq, True)]
    qp = rowwise(_rope, [(qp_raw, LANE, 0)] + qtabs, [], [(D, LANE, BF16)], tm=tq_128, ncb=Hm, name="q_rope")[0]
    knope = mm(ckvn, wkn, out_dtype=BF16, name="k_nope")
    vv = mm(ckvn, wv, out_dtype=BF16, name="v_proj")

    def pad_keys(z):
        z3 = rows3(z)
        return jnp.concatenate([z3[:, :N_META], jnp.zeros((Bl, Q_BLOCK - N_META, z.shape[-1]), z.dtype), z3[:, N_META:]], axis=1)

    def unpad_keys(z):
        return jnp.concatenate([z[:, :N_META], z[:, Q_BLOCK:]], axis=1).reshape(R, z.shape[-1])

    qn3, qp3 = qn.reshape(Bl, S, D), qp.reshape(Bl, S, D)
    knp, kpp, vp = pad_keys(knope), pad_keys(kpr), pad_keys(vv)
    o3, _ = attn_fwd(qn3, qp3, knp, kpp, vp)
    o_att = pad_meta(o3.reshape(RS, D))
    mix_rows = [(proj, 512, O_GA // 512), (proj, 512, O_GB // 512), (ya, 512, 0), (o_att, 512, 0)]
    mix = rowwise(_gate_mix, mix_rows, [], [(D, 512, BF16)], tm=t_512, ncb=D // 512, name="gate_mix")[0]
    h2 = mm(mix, wout, res=h1, name="w_out")
    h3, sv2 = ffn_fwd(h2, w2['ffn2_norm'], stk['ffn2_w_gate'], stk['ffn2_w_up'], full['ffn2_w_down'], "ffn2")

    def loss_fb(h_, tgt, fw):
        yv, pull = jax.vjp(_rms, h_, fw)
        e = yv - tgt
        dh, dfw = pull(e * (1.0 / D))
        return dh, jnp.full((1, LANE), 0.5 / D * jnp.sum(e * e), F32), dfw

    dh3r, lossp, g_final = rowwise(loss_fb, [(real_rows(h3), D, 0), (a['loss_target'].reshape(RS, D), D, 0)],
                                   [(w2['final_norm'], D, 0)], [(D, D, F32)], [(1, LANE, LANE), (1, D, D)],
                                   tm=tq_full, name="loss")
    dh3 = pad_meta(dh3r)

    def stacked(g, axis):
        if g.ndim == 3:
            return g
        if axis == 0:
            return g.reshape(4, g.shape[0] // 4, g.shape[1])
        return g.reshape(g.shape[0], 4, g.shape[1] // 4).transpose(1, 0, 2)

    cidx = ci.reshape(1).astype(jnp.int32)
    gr = {'final_norm': g_final}
    dh2, gr['ffn2_norm'], gr['ffn2_w_gate'], gr['ffn2_w_up'], gr['ffn2_w_down'] = ffn_bwd(
        dh3, sv2, w2['ffn2_norm'], stk['ffn2_w_gate'], stk['ffn2_w_up'], full['ffn2_w_down'], "ffn2")
    dh2b = rowwise(lambda d: d, [(dh2, 512, 0)], [], [(D, 512, BF16)], tm=t_512, ncb=D // 512, name="dh2_cast")[0]
    gr['w_out'] = mm(mix, dh2b, "tn", name="d_wout")
    dmix = mm(dh2b, wout, "nt", name="d_mix")
    dga, dgb, dya, do = rowwise(vjp_fn(_gate_mix, 4), mix_rows + [(dmix, 512, 0)], [],
                                [(D, 512, BF16), (D, 512, BF16), (D, 512, F32), (D, 512, BF16)], tm=t_512, ncb=D // 512,
                                name="d_gate_mix")
    gs_b = [stacked(gr[n], SHARD_AXIS[n]) for n in PART_B]
    hs_b = [add_half(g_, r_, cidx, "add_half_" + n) for n, g_, r_ in zip(PART_B, gs_b, swap_halves(gs_b, "swap_halves_b"))]
    (dqn, dqp, dknp, dvp, dkpp), ps_b = attn_bwd(qn3, qp3, knp, kpp, vp, real_rows(do).reshape(Bl, S, D), hs_b)
    dqn2 = dqn.reshape(RS, D)
    dqp_raw = rowwise(_rope_t, [(dqp.reshape(RS, D), LANE, 0)] + qtabs, [], [(D, LANE, BF16)], tm=tq_128, ncb=Hm,
                      name="d_q_rope")[0]
    d_wqn = mm(cqn_r, dqn2, "tn", name="d_wqn")
    d_wqp = mm(cqn_r, dqp_raw, "tn", name="d_wqp")
    dcqn = mm(dqn2, wqn, "nt", name="d_cqn1")
    dcqn = pad_meta(mm(dqp_raw, wqp, "nt", res=dcqn, name="d_cqn2"))
    dkn2, dv2, dkp2 = unpad_keys(dknp), unpad_keys(dvp), unpad_keys(dkpp)
    d_wkn = mm(ckvn, dkn2, "tn", name="d_wkn")
    d_wv = mm(ckvn, dv2, "tn", name="d_wv")
    dckvn = mm(dkn2, wkn, "nt", name="d_ckvn1")
    dckvn = mm(dv2, wv, "nt", res=dckvn, name="d_ckvn2")
    dcq, dckv, dkpe, gr['q_norm'], gr['kv_norm'] = rowwise(
        _mla_pre_bwd, mla_rows[:2] + [(dcqn, Q_LORA, 0), (dckvn, KV_LORA, 0), (dkp2, LANE, 0)] + tabs, mla_pars,
        [(Q_LORA, Q_LORA, BF16), (KV_LORA, KV_LORA, BF16), (LANE, LANE, BF16)], [(1, Q_LORA, Q_LORA), (1, KV_LORA, KV_LORA)],
        tm=t_512, name="d_mla_pre")

    def post_bwd(y_, r_, k2_, v_, g_, dya_, gnw, gnb, rk):
        return vjp_fn(_post, 8)(y_, r_, k2_, v_, g_, gnw, gnb, rk, dya_)

    dy, dr_b, dk2_b, dv_b, dg, gr['gn_w'], gr['gn_b'], gr['r_k'] = rowwise(
        post_bwd, post_rows + [(dya, LANE, 0)], post_pars,
        [(D, LANE, F32)] * 4 + [(D, LANE, BF16)], [(1, D, LANE)] * 3, tm=t_128, ncb=hb, name="d_wkv_post")
    jdw, jdkn, jdb, jdk, jdr, idv = wkv_bwd(jw, jkn, jb, jk, jr, iv, to_i(dy), sp, sa_i)
    ddecay, dkn_w, db_w, dk2_w, dr_w, dv_w = from_j(jdw), from_j(jdkn), from_j(jdb), from_j(jdk), from_j(jdr), from_i(idv)

    def prep_bwd(k_, lw_, la_, dd, dkn_, db_, dk2a, dk2b, w0, a0, kkw, kaw):
        return vjp_fn(_prep, 7)(k_, lw_, la_, w0, a0, kkw, kaw, dd, dkn_, db_, dk2a + dk2b)

    dk_s, dlw, dla, gr['w0'], gr['a0'], gr['k_k'], gr['k_a'] = rowwise(
        prep_bwd, [(k_s, LANE, 0), (lw, LANE, 0), (la, LANE, 0), (ddecay, LANE, 0), (dkn_w, LANE, 0), (db_w, LANE, 0),
                   (dk2_w, LANE, 0), (dk2_b, LANE, 0)],
        [par_d('w0'), par_d('a0'), par_d('k_k'), par_d('k_a')], [(D, LANE, F32), (D, LANE, BF16), (D, LANE, BF16)],
        [(1, D, LANE)] * 4, tm=t_128, ncb=hb, name="d_wkv_prep")
    d_wup = mm(txw, dlw, "tn", name="d_wup")
    dtxw = mm(dlw, w_up_p, "nt", name="d_txw")
    d_aup = mm(xas, dla, "tn", name="d_aup")
    dxa = mm(dla, a_up_p, "nt", name="d_xa")
    gr['g_up'] = mm(sg, dg, "tn", name="d_gup")
    dsg = mm(dg, g_up, "nt", name="d_sg")

    def lora_bwd(p_, prev_, dsg_, dt_, dxa_, mu_):
        return vjp_fn(_lora_act, 3)(p_, prev_, mu_, dsg_, dt_, dxa_)

    dpb, dprevb, dmu_b = rowwise(lora_bwd, [(proj, 512, O_L // 512), (prev_b, 512, 0), (dsg, G_LORA, 0), (dtxw, LANE, 0),
                                            (dxa, LANE, 0)], [(mu_b, 512, 0)], [(512, 512, F32)] * 2, [(1, 512, 512)],
                                 tm=t_512, name="d_lora_act")

    def shift_back(sec, cts, tag):
        nb = D // 512

        def f(p_, above, tp, *rest):
            cs, mu_ = rest[:-1], rest[-1]
            ct = cs[0] if len(cts) == 1 else cs[0] + cs[2]
            below = cs[1] if len(cts) == 1 else cs[1] + cs[3]
            return _shift_rows_bwd(float(T - 1), p_, above, tp, mu_, ct, below)

        ct_ins = [z for c in cts for z in ((c, 512, 0), (c, 512, 0, None, False, "next"))]
        return rowwise(f, [(proj, 512, sec * nb), (proj, 512, sec * nb, None, False, "prev"), tpos_in] + ct_ins,
                       [(mu_a, 512, sec * nb)], [(D, 512, BF16)], [(1, D, 512)], tm=t_512, ncb=nb, name="d_shift_" + tag)

    dsec_r, dmu_r = shift_back(0, [dr_w, dr_b], "r")
    dsec_k, dmu_k = shift_back(1, [dk_s], "k")
    dsec_v, dmu_v = shift_back(2, [dv_w, dv_b], "v")

    def add_cast(p_, q_):
        return p_ + q_

    def dsec(dp_, dprev_, tag):
        C = dp_.shape[1]
        return rowwise(add_cast, [(dp_, 512, 0), (shift_up(dprev_), 512, 0)], [], [(C, 512, BF16)], tm=t_512, ncb=C // 512,
                       name="d_sec_" + tag)[0]

    zpad = jnp.zeros((R, NP - NP0), BF16)
    dproj = jnp.concatenate([dsec_r, dsec_k, dsec_v, dga, dgb,
                             dsec(dpb, dprevb, "l"), dcq, dckv, dkpe, zpad], axis=1)
    d_win_p = mm(u, dproj, "tn", name="d_win")
    du = mm(dproj, win_p, "nt", name="d_u")

    def norm_bwd(h_, dn_, dh_, nw_):
        dh, dnw = vjp_fn(_rms, 2)(h_, nw_, dn_)
        return dh + dh_, dnw

    dh1, gr['mix_norm'] = rowwise(norm_bwd, [(h1, D, 0), (du, D, 0), (dh2, D, 0)], [(w2['mix_norm'], D, 0)],
                                  [(D, D, F32)], [(1, D, D)], tm=t_full, name="d_mix_norm")
    o1 = 3 * D + 2 * W_LORA + G_LORA
    o2 = o1 + Q_LORA + KV_LORA
    segs = [(0, 3 * D, 0), (3 * D, W_LORA, O_L + G_LORA), (3 * D + W_LORA, A_LORA, O_L + G_LORA + LANE),
            (3 * D + 2 * W_LORA, G_LORA, O_L), (o1, Q_LORA, O_CQ), (o1 + Q_LORA, KV_LORA, O_CKV), (o2, half, O_KPE),
            (o2 + half, half, O_KPE + 2 * half), (o2 + ROPE_DIM, 2 * D, O_GA)]
    cs_in = w2['w_in'].shape[1]

    def shard_cols(s):
        pieces = []
        for a0_, n_, p0_ in segs:
            x0, x1 = max(a0_, s * cs_in), min(a0_ + n_, (s + 1) * cs_in)
            if x0 < x1:
                pieces.append(d_win_p[:, p0_ + x0 - a0_:p0_ + x1 - a0_])
        return jnp.concatenate(pieces, axis=1)

    gr['w_in'] = jnp.stack([shard_cols(s) for s in range(4)])
    gr['tm_mu'] = jnp.concatenate([dmu_r, dmu_k, dmu_v, dmu_b[:, G_LORA:G_LORA + W_LORA],
                                   dmu_b[:, G_LORA + LANE:G_LORA + LANE + A_LORA], dmu_b[:, :G_LORA]], axis=1)
    gr['w_up'], gr['a_up'] = d_wup[:W_LORA], d_aup[:A_LORA]
    dq3n, dq3p = d_wqn.reshape(Q_LORA, Hm, LANE), d_wqp.reshape(Q_LORA, Hm, LANE)
    gr['w_uq'] = jnp.concatenate([dq3n, dq3p[:, :, :half], dq3p[:, :, 2 * half:3 * half]], axis=2).reshape(Q_LORA, Hm * QK_DIM)
    gr['w_ukv'] = jnp.concatenate([d_wkn.reshape(KV_LORA, Hm, LANE), d_wv.reshape(KV_LORA, Hm, LANE)], axis=2).reshape(
        KV_LORA, Hm * (NOPE_DIM + V_DIM))
    gs_a2 = [stacked(gr[n], SHARD_AXIS[n]) for n in PART_A2]
    hs_a2 = [add_half(g_, r_, cidx, "add_half_" + n) for n, g_, r_ in zip(PART_A2, gs_a2, swap_halves(gs_a2, "swap_halves_a2"))]
    dh0, gr['ffn1_norm'], gr['ffn1_w_gate'], gr['ffn1_w_up'], gr['ffn1_w_down'], ps_a2, own_a1 = ffn_bwd(
        dh1, sv1, w2['ffn1_norm'], stk['ffn1_w_gate'], stk['ffn1_w_up'], full['ffn1_w_down'], "ffn1", hs_a2, PART_A1)
    dh0_3 = rows3(dh0)
    grad_x = dh0_3[:, N_META:]
    gr['meta_tokens'] = jnp.sum(dh0_3[:, :N_META], axis=0)

    ps_a1, hs_a1 = [own_a1[n][0] for n in PART_A1], [own_a1[n][1] for n in PART_A1]
    sc = (chip32.reshape(1), cidx)
    ts = [sum_chips(p, h, sc, "sum_chips_" + n)
          for n, p, h in zip(PART_A1 + PART_A2 + PART_B, ps_a1 + ps_a2 + ps_b, hs_a1 + hs_a2 + hs_b)]
    g_shard = {n: z.reshape(w2[n].shape) for n, z in zip(PART_A1 + PART_A2 + PART_B, share_sibling(ts))}
    small = jnp.concatenate([gr[n].reshape(-1) for n in SMALL] + [gr['meta_tokens'].reshape(-1), lossp.reshape(-1)])
    ns = small.shape[0]
    nsp = -(-ns // (8 * LANE)) * 8 * LANE
    small_sum = allreduce8(jnp.pad(small, (0, nsp - ns)).reshape(-1, LANE), "allreduce_small").reshape(-1)
    g_small = dict(zip(SMALL + ['meta_full'], _unpack(small_sum, [w2[n].shape for n in SMALL] + [(N_META, D)])))
    g_shard['meta_tokens'] = lax.dynamic_slice(
        g_small['meta_full'], (jnp.zeros((), jnp.int32), (chip * mcols).astype(jnp.int32)), (N_META, mcols))
    loss = small_sum[ns - LANE]

    grads, deltas, new_m, new_v = [], [], [], []
    for n in WEIGHTS:
        gw = g_shard[n] if n in g_shard else g_small[n]
        d_, m_, v_ = adamw(w2[n], gw, m2[n], v2[n], "adamw_" + n)
        shp = a[n].shape
        grads.append(gw.reshape(shp))
        deltas.append(d_.reshape(shp))
        new_m.append(m_.reshape(shp))
        new_v.append(v_.reshape(shp))
    return (loss, grad_x, *grads, *deltas, *new_m, *new_v)


def kernel(x, meta_tokens, ffn1_norm, ffn1_w_gate, ffn1_w_up, ffn1_w_down, mix_norm, w_in, tm_mu, w0, w_up, a0, a_up, g_up, k_k, k_a, r_k, gn_w, gn_b, q_norm, w_uq, kv_norm, w_ukv, w_out, ffn2_norm, ffn2_w_gate, ffn2_w_up, ffn2_w_down, final_norm, loss_target, m_meta_tokens, m_ffn1_norm, m_ffn1_w_gate, m_ffn1_w_up, m_ffn1_w_down, m_mix_norm, m_w_in, m_tm_mu, m_w0, m_w_up, m_a0, m_a_up, m_g_up, m_k_k, m_k_a, m_r_k, m_gn_w, m_gn_b, m_q_norm, m_w_uq, m_kv_norm, m_w_ukv, m_w_out, m_ffn2_norm, m_ffn2_w_gate, m_ffn2_w_up, m_ffn2_w_down, m_final_norm, v_meta_tokens, v_ffn1_norm, v_ffn1_w_gate, v_ffn1_w_up, v_ffn1_w_down, v_mix_norm, v_w_in, v_tm_mu, v_w0, v_w_up, v_a0, v_a_up, v_g_up, v_k_k, v_k_a, v_r_k, v_gn_w, v_gn_b, v_q_norm, v_w_uq, v_kv_norm, v_w_ukv, v_w_out, v_ffn2_norm, v_ffn2_w_gate, v_ffn2_w_up, v_ffn2_w_down, v_final_norm):
    return _step(dict(locals()))
```

```python
import functools
import math

import jax
import jax.numpy as jnp
import numpy as np
from jax import lax
from jax.experimental import pallas as pl
from jax.experimental.pallas import tpu as pltpu

F32 = jnp.float32
BF16 = jnp.bfloat16
MESH = pl.DeviceIdType.MESH

N_META = 16
NORM_EPS = 1e-6
RWKV_HEAD = 64
GN_EPS = RWKV_HEAD * 1e-5
W_LORA, A_LORA, G_LORA = 96, 96, 256
Q_LORA, KV_LORA = 512, 512
NOPE_DIM, ROPE_DIM, V_DIM = 128, 64, 128
QK_DIM = NOPE_DIM + ROPE_DIM
ROPE_THETA = 10000.0
Q_BLOCK = 128
ADAM_LR, ADAM_B1, ADAM_B2, ADAM_EPS, ADAM_WD, ADAM_STEP = 0.001, 0.9, 0.999, 1e-08, 0.01, 10

LANE = 128
VMEM_LIMIT = 56 * 1024 * 1024


def _pcall(body, **kw):
    return pl.pallas_call(body, **kw)


def _cparams(sem):
    return pltpu.CompilerParams(dimension_semantics=sem, vmem_limit_bytes=VMEM_LIMIT)


MM_LANE_TILE = 1536
MM_ROW_TILE = 1408


def _div_tile(n, cap, unit):
    best = None
    for t in range(unit, min(n, cap) + 1, unit):
        if n % t == 0:
            best = t
    return best if best is not None else n


def _rtile(n, pref=512):
    best = None
    for t in range(16, min(n, pref * 2) + 1, 16):
        if n % t == 0 and (best is None or abs(t - pref) < abs(best - pref)):
            best = t
    return best if best is not None else n


def mm(a, b, mode="nn", out_dtype=F32, res=None, alpha=1.0, b_stack=False, out_stack=False, carry=None, name="mm"):
    kind, carried = carry if carry else (None, ())
    nc = len(carried)
    cs = b.shape[-1] if b_stack else None
    bs = (b.shape[1], 4 * cs) if b_stack else b.shape
    if mode == "nn":
        (M, K), (K2, N) = a.shape, bs
    elif mode == "nt":
        (M, K), (N, K2) = a.shape, bs
    else:
        (K, M), (K2, N) = a.shape, bs
    assert K == K2, (a.shape, b.shape, mode)
    if mode == "tn":
        tm, tk = _div_tile(M, MM_ROW_TILE, LANE), _div_tile(K, 1024, 16)
    else:
        tm = _div_tile(M, MM_ROW_TILE, 16)
        tk = _div_tile(cs if (b_stack and mode == "nt") else K, MM_LANE_TILE, LANE)
    ncol = N // 4 if out_stack else (cs if (b_stack and mode == "nn") else N)
    tn = _div_tile(ncol, MM_LANE_TILE if mode != "nt" else 1024, LANE)
    nk = K // tk
    dims = {"nn": (((1,), (0,)), ((), ())), "nt": (((1,), (1,)), ((), ())), "tn": (((0,), (0,)), ((), ()))}[mode]
    direct = out_dtype == F32

    def body(*refs):
        a_ref, b_ref = refs[:2]
        r_ref = refs[2] if res is not None else None
        nin = 3 if res is not None else 2
        o_ref = refs[nin + nc]
        acc = o_ref if direct else refs[nin + 2 * nc + 1]
        k = pl.program_id(2)
        if nc:
            steps = _ag_steps if kind == "ag" else _rs_steps
            start, finish = steps(refs[nin:nin + nc], refs[nin + nc + 1:nin + 2 * nc + 1],
                                  *refs[nin + 2 * nc + (1 if direct else 2):])
            i, j = pl.program_id(0), pl.program_id(1)
            pl.when((i == 0) & (j == 0) & (k == 0))(start)

        @pl.when(k == 0)
        def _():
            acc[...] = jnp.zeros_like(acc) if res is None else r_ref[...].astype(F32)

        p = lax.dot_general(a_ref[...].astype(BF16), b_ref[...].astype(BF16), dims, preferred_element_type=F32)
        acc[...] += p if alpha == 1.0 else alpha * p

        if not direct:
            @pl.when(k == nk - 1)
            def _():
                o_ref[...] = acc[...].astype(o_ref.dtype)

        if nc:
            pl.when((i == M // tm - 1) & (j == N // tn - 1) & (k == nk - 1))(finish)

    if mode == "tn":
        a_spec = pl.BlockSpec((tk, tm), lambda i, j, k: (k, i))
        b_spec = pl.BlockSpec((tk, tn), lambda i, j, k: (k, j))
    else:
        a_spec = pl.BlockSpec((tm, tk), lambda i, j, k: (i, k))
        if mode == "nn":
            if b_stack:
                nps = cs // tn
                b_spec = pl.BlockSpec((None, tk, tn), lambda i, j, k: (j // nps, k, j % nps))
            else:
                b_spec = pl.BlockSpec((tk, tn), lambda i, j, k: (k, j))
        elif b_stack:
            kps = cs // tk
            b_spec = pl.BlockSpec((None, tn, tk), lambda i, j, k: (k // kps, j, k % kps))
        else:
            b_spec = pl.BlockSpec((tn, tk), lambda i, j, k: (j, k))
    r_spec = pl.BlockSpec((tm, tn), lambda i, j, k: (i, j))
    if out_stack:
        ops = (N // 4) // tn
        o_spec = pl.BlockSpec((None, tm, tn), lambda i, j, k: (j // ops, i, j % ops))
        o_shape = jax.ShapeDtypeStruct((4, M, N // 4), out_dtype)
    else:
        o_spec, o_shape = r_spec, jax.ShapeDtypeStruct((M, N), out_dtype)
    in_specs = [a_spec, b_spec] + ([r_spec] if res is not None else [])
    args = (a, b) + ((res,) if res is not None else ())
    if not nc:
        return _pcall(
            body, name=name, grid=(M // tm, N // tn, nk), in_specs=in_specs, out_specs=o_spec, out_shape=o_shape,
            scratch_shapes=[] if direct else [pltpu.VMEM((tm, tn), F32)],
            compiler_params=_cparams(("parallel", "parallel", "arbitrary")),
        )(*args)
    out = _pcall(
        body, name=name, grid=(M // tm, N // tn, nk), in_specs=in_specs + [_HBM] * nc, out_specs=[o_spec] + [_HBM] * nc,
        out_shape=[o_shape] + [jax.ShapeDtypeStruct(z.shape, z.dtype) for z in carried],
        input_output_aliases={len(args) + q: 1 + q for q in range(nc)} if kind == "ag" else {},
        scratch_shapes=([] if direct else [pltpu.VMEM((tm, tn), F32)]) + (_ag_sems(nc) if kind == "ag" else _rs_sems(nc)),
        compiler_params=_cparams(("arbitrary",) * 3),
    )(*args, *carried)
    return out[0], list(out[1:])


MM_EPI_ROW_TILE = 704


def mm_epi(a, bs, mode, epi, extras, out_dtypes, b_stack=False, carry=None, name="mm_epi"):
    kind, carried = carry if carry else (None, ())
    nc = len(carried)
    b = bs[0]
    cs = b.shape[-1] if b_stack else None
    bshape = (b.shape[1], 4 * cs) if b_stack else b.shape
    (M, K) = a.shape
    (K2, N) = bshape if mode == "nn" else bshape[::-1]
    assert K == K2 and mode in ("nn", "nt"), (a.shape, b.shape, mode)
    tm = _div_tile(M, MM_EPI_ROW_TILE, 16)
    tk = _div_tile(cs if (b_stack and mode == "nt") else K, MM_LANE_TILE, LANE)
    tn = _div_tile(cs if (b_stack and mode == "nn") else N, MM_LANE_TILE, LANE)
    nk, nb, ne, no = K // tk, len(bs), len(extras), len(out_dtypes)
    dims = (((1,), (0,)), ((), ())) if mode == "nn" else (((1,), (1,)), ((), ()))

    def body(*refs):
        a_ref, b_refs, e_refs = refs[0], refs[1:1 + nb], refs[1 + nb:1 + nb + ne]
        base = 1 + nb + ne + nc
        o_refs, accs = refs[base:base + no], refs[base + no + nc:base + no + nc + nb]
        k = pl.program_id(2)
        if nc:
            steps = _ag_steps if kind == "ag" else _rs_steps
            start, finish = steps(refs[base - nc:base], refs[base + no:base + no + nc], *refs[base + no + nc + nb:])
            i, j = pl.program_id(0), pl.program_id(1)
            pl.when((i == 0) & (j == 0) & (k == 0))(start)

        @pl.when(k == 0)
        def _():
            for acc in accs:
                acc[...] = jnp.zeros_like(acc)

        av = a_ref[...].astype(BF16)
        for b_ref, acc in zip(b_refs, accs):
            acc[...] += lax.dot_general(av, b_ref[...].astype(BF16), dims, preferred_element_type=F32)

        @pl.when(k == nk - 1)
        def _():
            res = epi(*[acc[...] for acc in accs], *[e[...] for e in e_refs])
            for o_ref, v in zip(o_refs, res):
                o_ref[...] = v.astype(o_ref.dtype)

        if nc:
            pl.when((i == M // tm - 1) & (j == N // tn - 1) & (k == nk - 1))(finish)

    a_spec = pl.BlockSpec((tm, tk), lambda i, j, k: (i, k))
    if mode == "nn":
        if b_stack:
            nps = cs // tn
            b_spec = pl.BlockSpec((None, tk, tn), lambda i, j, k: (j // nps, k, j % nps))
        else:
            b_spec = pl.BlockSpec((tk, tn), lambda i, j, k: (k, j))
    elif b_stack:
        kps = cs // tk
        b_spec = pl.BlockSpec((None, tn, tk), lambda i, j, k: (k // kps, j, k % kps))
    else:
        b_spec = pl.BlockSpec((tn, tk), lambda i, j, k: (j, k))
    o_spec = pl.BlockSpec((tm, tn), lambda i, j, k: (i, j))
    sems = (_ag_sems(nc) if kind == "ag" else _rs_sems(nc)) if nc else []
    res = _pcall(
        body, name=name, grid=(M // tm, N // tn, nk), in_specs=[a_spec] + [b_spec] * nb + [o_spec] * ne + [_HBM] * nc,
        out_specs=[o_spec] * no + [_HBM] * nc,
        out_shape=[jax.ShapeDtypeStruct((M, N), dt) for dt in out_dtypes]
        + [jax.ShapeDtypeStruct(z.shape, z.dtype) for z in carried],
        input_output_aliases={1 + nb + ne + q: no + q for q in range(nc)} if kind == "ag" else {},
        scratch_shapes=[pltpu.VMEM((tm, tn), F32)] * nb + sems,
        compiler_params=_cparams(("arbitrary",) * 3 if nc else ("parallel", "parallel", "arbitrary")),
    )(a, *bs, *extras, *carried)
    return (list(res[:no]), list(res[no:])) if nc else res


def rowwise(fn, row_ins, par_ins, outs, accs=(), *, tm, ncb=1, t_rows=None, name="rowwise"):
    R = row_ins[0][0].shape[0]
    assert R % tm == 0, (R, tm)
    nrb = R // tm
    npe = t_rows // tm if t_rows else None
    in_specs, args = [], []
    for spec in row_ins:
        arr, w, base = spec[:3]
        mod = spec[3] if len(spec) > 3 else None
        cstep = 0 if (len(spec) > 4 and spec[4]) else 1
        halo = spec[5] if len(spec) > 5 else None
        if len(spec) > 6 and spec[6]:
            cb = arr.shape[1] // (R // t_rows) // w
            in_specs.append(pl.BlockSpec((tm, w), lambda j, i, base=base, cb=cb: (i % npe, (i // npe) * cb + base + j)))
        elif halo == "prev":
            in_specs.append(pl.BlockSpec((8, w), lambda j, i, base=base: (jnp.maximum(i * (tm // 8) - 1, 0), base + j)))
        elif halo == "next":
            in_specs.append(pl.BlockSpec((8, w), lambda j, i, base=base: (jnp.minimum((i + 1) * (tm // 8), R // 8 - 1), base + j)))
        elif mod is None:
            in_specs.append(pl.BlockSpec((tm, w), lambda j, i, base=base, cstep=cstep: (i, base + cstep * j)))
        else:
            in_specs.append(pl.BlockSpec((tm, w), lambda j, i, base=base, mod=mod, cstep=cstep: (i % mod, base + cstep * j)))
        args.append(arr)
    for arr, w, base in par_ins:
        in_specs.append(pl.BlockSpec((arr.shape[0], w), lambda j, i, base=base: (0, base + j)))
        args.append(arr)
    out_specs, out_shape = [], []
    for o in outs:
        cols, w, dt = o[:3]
        if len(o) > 3 and o[3]:
            out_specs.append(pl.BlockSpec((tm, w), lambda j, i, cb=cols // w: (i % npe, (i // npe) * cb + j)))
            out_shape.append(jax.ShapeDtypeStruct((t_rows, (R // t_rows) * cols), dt))
        else:
            out_specs.append(pl.BlockSpec((tm, w), lambda j, i: (i, j)))
            out_shape.append(jax.ShapeDtypeStruct((R, cols), dt))
    for p, cols, w in accs:
        out_specs.append(pl.BlockSpec((p, w), lambda j, i: (0, j)))
        out_shape.append(jax.ShapeDtypeStruct((p, cols), F32))
    nin, nout, nacc = len(args), len(outs), len(accs)

    def body(*refs):
        vals = [r[...] for r in refs[:nin]]
        res = fn(*vals)
        if not isinstance(res, (tuple, list)):
            res = (res,)
        assert len(res) == nout + nacc, (len(res), nout, nacc)
        for o_ref, v in zip(refs[nin:nin + nout], res[:nout]):
            o_ref[...] = v.astype(o_ref.dtype)
        if nacc:
            i = pl.program_id(1)

            @pl.when(i == 0)
            def _():
                for a_ref in refs[nin + nout:]:
                    a_ref[...] = jnp.zeros_like(a_ref)

            for a_ref, v in zip(refs[nin + nout:], res[nout:]):
                a_ref[...] += v.astype(F32)

    r = _pcall(
        body, name=name, grid=(ncb, nrb), in_specs=in_specs, out_specs=out_specs, out_shape=out_shape,
        compiler_params=_cparams(("parallel", "arbitrary")),
    )(*args)
    return r


def vjp_fn(fwd, nprim):
    def f(*vals):
        prim, cts = vals[:nprim], vals[nprim:]
        out, pull = jax.vjp(fwd, *[p.astype(F32) for p in prim])
        if not isinstance(out, (tuple, list)):
            cts = cts[0].astype(F32)
        else:
            cts = tuple(c.astype(F32) for c in cts)
        return pull(cts)
    return f


WKV_TB = 16


def wkv_fwd(w, kn, b, k, r, v, bufs=()):
    T, N, L = w.shape
    NI = v.shape[1]
    tb = WKV_TB
    n = len(bufs)
    assert T % tb == 0 and L == v.shape[2]

    def body(*refs):
        w_ref, kn_ref, b_ref, k_ref, r_ref, v_ref = refs[:6]
        y_ref, sp_ref, sa_ref = refs[6 + n:9 + n]
        s_ref = refs[9 + 2 * n]
        if n:
            start, finish = _ag_steps(refs[6:6 + n], refs[9 + n:9 + 2 * n], *refs[10 + 2 * n:])
            pl.when(pl.program_id(0) == 0)(start)

        @pl.when(pl.program_id(0) == 0)
        def _():
            s_ref[...] = jnp.zeros_like(s_ref)

        def step(s, carry):
            W, KN, B, Kk, Rr = w_ref[s], kn_ref[s], b_ref[s], k_ref[s], r_ref[s]
            for i in range(NI):
                S = s_ref[i]
                sp_ref[s, i] = S
                sa = jnp.sum(S * KN, axis=0, keepdims=True)
                sa_ref[s, pl.ds(i, 1), :] = sa
                vi = v_ref[s, pl.ds(i, 1), :]
                Sn = S * W + sa * B + vi * Kk
                s_ref[i] = Sn
                y_ref[s, pl.ds(i, 1), :] = jnp.sum(Sn * Rr, axis=0, keepdims=True)
            return carry

        lax.fori_loop(0, tb, step, 0)
        if n:
            pl.when(pl.program_id(0) == T // tb - 1)(finish)

    jspec = pl.BlockSpec((tb, N, L), lambda t: (t, 0, 0))
    ispec = pl.BlockSpec((tb, NI, L), lambda t: (t, 0, 0))
    ishape = jax.ShapeDtypeStruct((T, NI, L), F32)
    res = _pcall(
        body, name="wkv_fwd", grid=(T // tb,), in_specs=[jspec] * 5 + [ispec] + [_HBM] * n,
        out_specs=[ispec, pl.BlockSpec((tb, NI, N, L), lambda t: (t, 0, 0, 0)), ispec] + [_HBM] * n,
        out_shape=[ishape, jax.ShapeDtypeStruct((T, NI, N, L), F32), ishape]
        + [jax.ShapeDtypeStruct(z.shape, z.dtype) for z in bufs],
        input_output_aliases={6 + i: 3 + i for i in range(n)},
        scratch_shapes=[pltpu.VMEM((NI, N, L), F32)] + (_ag_sems(n) if n else []),
        compiler_params=_cparams(("arbitrary",)),
    )(w, kn, b, k, r, v, *bufs)
    return res[0], res[1], res[2], list(res[3:])


def wkv_bwd(w, kn, b, k, r, v, dy, sp, sa):
    T, N, L = w.shape
    NI, LH = v.shape[1], L // 2
    tb = WKV_TB
    nt = T // tb

    def body(w_ref, kn_ref, b_ref, k_ref, r_ref, v_ref, dy_ref, sp_ref, sa_ref,
             dw_ref, dkn_ref, db_ref, dk_ref, dr_ref, dv_ref, ds_ref):
        @pl.when(pl.program_id(0) == 0)
        def _():
            ds_ref[...] = jnp.zeros_like(ds_ref)

        def step(q, carry):
            s = tb - 1 - q
            W, KN, B, Kk, Rr = w_ref[s], kn_ref[s], b_ref[s], k_ref[s], r_ref[s]
            dW = jnp.zeros((N, L), F32)
            dKN, dB, dK, T1 = dW, dW, dW, dW
            al = jnp.zeros((1, L), F32)
            be = al
            for i in range(NI):
                Sp = sp_ref[s, i]
                vi = v_ref[s, pl.ds(i, 1), :]
                dyi = dy_ref[s, pl.ds(i, 1), :]
                sai = sa_ref[s, pl.ds(i, 1), :]
                dS = ds_ref[i] + dyi * Rr
                T1 = T1 + Sp * dyi
                al = al + sai * dyi
                be = be + vi * dyi
                dv_ref[s, pl.ds(i, 1), :] = jnp.sum(dS * Kk, axis=0, keepdims=True)
                dK = dK + dS * vi
                dsa = jnp.sum(dS * B, axis=0, keepdims=True)
                dB = dB + dS * sai
                dW = dW + dS * Sp
                dKN = dKN + Sp * dsa
                ds_ref[i] = dS * W + dsa * KN
            dR = W * T1 + B * al + Kk * be
            for ref, val in ((dw_ref, dW), (dkn_ref, dKN), (db_ref, dB), (dk_ref, dK), (dr_ref, dR)):
                ref[s] = (val + pltpu.roll(val, LH, 1))[:, :LH]
            return carry

        lax.fori_loop(0, tb, step, 0)

    jspec = pl.BlockSpec((tb, N, L), lambda t: (nt - 1 - t, 0, 0))
    gspec = pl.BlockSpec((tb, N, LH), lambda t: (nt - 1 - t, 0, 0))
    ispec = pl.BlockSpec((tb, NI, L), lambda t: (nt - 1 - t, 0, 0))
    gshape = jax.ShapeDtypeStruct((T, N, LH), F32)
    return _pcall(
        body, name="wkv_bwd", grid=(nt,),
        in_specs=[jspec] * 5 + [ispec, ispec, pl.BlockSpec((tb, NI, N, L), lambda t: (nt - 1 - t, 0, 0, 0)), ispec],
        out_specs=[gspec] * 5 + [ispec], out_shape=[gshape] * 5 + [jax.ShapeDtypeStruct((T, NI, L), F32)],
        scratch_shapes=[pltpu.VMEM((NI, N, L), F32)],
        compiler_params=_cparams(("arbitrary",)),
    )(w, kn, b, k, r, v, dy, sp, sa)


_NT = (((1,), (1,)), ((), ()))
_TN = (((0,), (0,)), ((), ()))
ATT_SCALE = QK_DIM ** -0.5


def _att_bias():
    col = lax.broadcasted_iota(jnp.int32, (Q_BLOCK, Q_BLOCK), 1)
    row = lax.broadcasted_iota(jnp.int32, (Q_BLOCK, Q_BLOCK), 0)
    return jnp.where(col < N_META, 0.0, -1e30).astype(F32), jnp.where(col <= row, 0.0, -1e30).astype(F32)


def _att_scores(q1, q2, kn_ref, kp_ref, s_ref, bias, L):
    s = lax.dot_general(q1, kn_ref[0, :L, :], _NT, preferred_element_type=F32)
    s = s + lax.dot_general(q2, kp_ref[0, :L, :], _NT, preferred_element_type=F32)
    s_ref[:, :L] = s * ATT_SCALE
    s_ref[:, :Q_BLOCK] += bias[0]
    s_ref[:, L - Q_BLOCK:L] += bias[1]


def _att_probs(s_ref, L):
    s = s_ref[:, :L]
    m = jnp.max(s, axis=-1, keepdims=True)
    p = jnp.exp(s - m)
    return p / jnp.sum(p, axis=-1, keepdims=True)


def _att_blocks(nq, qn_ref, qp_ref, kn_ref, kp_ref, s_ref):
    bias = _att_bias()
    rows = lambda i: pl.ds(Q_BLOCK * i, Q_BLOCK)
    L = lambda i: Q_BLOCK * (i + 2)
    score = lambda i: _att_scores(qn_ref[0, rows(i), :], qp_ref[0, rows(i), :], kn_ref, kp_ref, s_ref.at[i % 2], bias, L(i))
    score(0)
    for i in range(nq):
        if i + 1 < nq:
            score(i + 1)
        yield i, rows(i), L(i), _att_probs(s_ref.at[i % 2], L(i))


def _edge_steps(nb, nh):
    b, h = pl.program_id(0), pl.program_id(1)
    return (b == 0) & (h == 0), (b == nb - 1) & (h == nh - 1)


def attn_fwd(qn, qp, kn, kp, v, bufs=()):
    B, S, HD = qn.shape
    SP = kn.shape[1]
    H = HD // LANE
    nq = S // Q_BLOCK
    n = len(bufs)

    def body(*refs):
        qn_ref, qp_ref, kn_ref, kp_ref, v_ref = refs[:5]
        o_ref = refs[5 + n]
        s_ref = refs[6 + 2 * n]
        if n:
            start, finish = _ag_steps(refs[5:5 + n], refs[6 + n:6 + 2 * n], *refs[7 + 2 * n:])
            is_first, is_last = _edge_steps(B, H)
            pl.when(is_first)(start)
        for i, rows, L, p in _att_blocks(nq, qn_ref, qp_ref, kn_ref, kp_ref, s_ref):
            o_ref[0, rows, :] = jnp.dot(p.astype(BF16), v_ref[0, :L, :], preferred_element_type=F32)
        if n:
            pl.when(is_last)(finish)

    qspec = pl.BlockSpec((1, S, LANE), lambda b, h: (b, 0, h))
    kspec = pl.BlockSpec((1, SP, LANE), lambda b, h: (b, 0, h))
    pspec = pl.BlockSpec((1, SP, LANE), lambda b, h: (b, 0, 0))
    res = _pcall(
        body, name="attn_fwd", grid=(B, H), in_specs=[qspec, qspec, kspec, pspec, kspec] + [_HBM] * n,
        out_specs=[qspec] + [_HBM] * n,
        out_shape=[jax.ShapeDtypeStruct((B, S, HD), F32)] + [jax.ShapeDtypeStruct(w.shape, w.dtype) for w in bufs],
        input_output_aliases={5 + i: 1 + i for i in range(n)},
        scratch_shapes=[pltpu.VMEM((2, Q_BLOCK, SP), F32)] + (_ag_sems(n) if n else []),
        compiler_params=_cparams(("arbitrary", "arbitrary")),
    )(qn, qp, kn, kp, v, *bufs)
    return res[0], list(res[1:])


def attn_bwd(qn, qp, kn, kp, v, do, hs=()):
    B, S, HD = qn.shape
    SP = kn.shape[1]
    H = HD // LANE
    nq = S // Q_BLOCK
    n = len(hs)

    def body(*refs):
        qn_ref, qp_ref, kn_ref, kp_ref, v_ref, do_ref = refs[:6]
        dqn_ref, dqp_ref, dkn_ref, dv_ref, dkp_ref = refs[6 + n:11 + n]
        dkn_acc, dv_acc, s_ref = refs[11 + 2 * n:14 + 2 * n]
        if n:
            start, finish = _rs_steps(refs[6:6 + n], refs[11 + n:11 + 2 * n], *refs[14 + 2 * n:])
            is_first, is_last = _edge_steps(B, H)
            pl.when(is_first)(start)

        @pl.when(pl.program_id(1) == 0)
        def _():
            dkp_ref[...] = jnp.zeros_like(dkp_ref)

        dkn_acc[...] = jnp.zeros_like(dkn_acc)
        dv_acc[...] = jnp.zeros_like(dv_acc)
        for i, rows, L, p in _att_blocks(nq, qn_ref, qp_ref, kn_ref, kp_ref, s_ref):
            q1, q2, do_i = qn_ref[0, rows, :], qp_ref[0, rows, :], do_ref[0, rows, :]
            dp = lax.dot_general(do_i, v_ref[0, :L, :], _NT, preferred_element_type=F32)
            ds = (p * (dp - jnp.sum(p * dp, axis=-1, keepdims=True)) * ATT_SCALE).astype(BF16)
            dqn_ref[0, rows, :] = jnp.dot(ds, kn_ref[0, :L, :], preferred_element_type=F32).astype(dqn_ref.dtype)
            dqp_ref[0, rows, :] = jnp.dot(ds, kp_ref[0, :L, :], preferred_element_type=F32)
            dkn_acc[:L, :] += lax.dot_general(ds, q1, _TN, preferred_element_type=F32)
            dkp_ref[0, :L, :] += lax.dot_general(ds, q2, _TN, preferred_element_type=F32)
            dv_acc[:L, :] += lax.dot_general(p.astype(BF16), do_i, _TN, preferred_element_type=F32)
        dkn_ref[0] = dkn_acc[...].astype(dkn_ref.dtype)
        dv_ref[0] = dv_acc[...].astype(dv_ref.dtype)
        if n:
            pl.when(is_last)(finish)

    qspec = pl.BlockSpec((1, S, LANE), lambda b, h: (b, 0, h))
    kspec = pl.BlockSpec((1, SP, LANE), lambda b, h: (b, 0, h))
    pspec = pl.BlockSpec((1, SP, LANE), lambda b, h: (b, 0, 0))
    res = _pcall(
        body, name="attn_bwd", grid=(B, H), in_specs=[qspec, qspec, kspec, pspec, kspec, qspec] + [_HBM] * n,
        out_specs=[qspec, qspec, kspec, kspec, pspec] + [_HBM] * n,
        out_shape=[jax.ShapeDtypeStruct((B, S, HD), BF16), jax.ShapeDtypeStruct((B, S, HD), F32),
                   jax.ShapeDtypeStruct((B, SP, HD), BF16), jax.ShapeDtypeStruct((B, SP, HD), BF16),
                   jax.ShapeDtypeStruct((B, SP, LANE), F32)] + [jax.ShapeDtypeStruct(h.shape, h.dtype) for h in hs],
        scratch_shapes=[pltpu.VMEM((SP, LANE), F32), pltpu.VMEM((SP, LANE), F32), pltpu.VMEM((2, Q_BLOCK, SP), F32)]
        + (_rs_sems(n) if n else []),
        compiler_params=_cparams(("arbitrary", "arbitrary")),
    )(qn, qp, kn, kp, v, do, *hs)
    return res[:5], list(res[5:])


_HBM = pl.BlockSpec(memory_space=pltpu.HBM)


def _place():
    x, y, c = lax.axis_index("x"), lax.axis_index("y"), lax.axis_index("c")
    chips = [(1 - x, y), (x, 1 - y), (1 - x, 1 - y)]
    return x, y, c, chips


def _rcopy(src, dst, ssem, rsem, dev):
    return pltpu.make_async_remote_copy(src_ref=src, dst_ref=dst, send_sem=ssem, recv_sem=rsem,
                                        device_id=dev, device_id_type=MESH)


def _half_rows(c, r):
    return pl.ds(pl.multiple_of(c * (r // 2), 16), r // 2)


def _ag_steps(w, out, ssem, rsem):
    n = len(w)
    x, y, c, chips = _place()
    s = 2 * x + y
    rows = [_half_rows(c, w[i].shape[1]) for i in range(n)]
    orows = [_half_rows(1 - c, w[i].shape[1]) for i in range(n)]
    first = [_rcopy(w[i].at[s, rows[i]], out[i].at[s, rows[i]], ssem.at[6 * i + j], rsem.at[6 * i + j], (px, py, c))
             for i in range(n) for j, (px, py) in enumerate(chips)]

    def start():
        for cp in first:
            cp.start()

    def finish():
        passed = []
        for j, (px, py) in enumerate(chips):
            sp = 2 * px + py
            for i in range(n):
                here = out[i].at[sp, rows[i]]
                _rcopy(here, here, ssem.at[6 * i + j], rsem.at[6 * i + j], (px, py, c)).wait_recv()
                fw = _rcopy(here, here, ssem.at[6 * i + 3 + j], rsem.at[6 * i + 3 + j], (x, y, 1 - c))
                fw.start()
                passed.append(fw)
        for j, (px, py) in enumerate(chips):
            sp = 2 * px + py
            for i in range(n):
                there = out[i].at[sp, orows[i]]
                _rcopy(there, there, ssem.at[6 * i + 3 + j], rsem.at[6 * i + 3 + j], (x, y, 1 - c)).wait_recv()
        for cp in first + passed:
            cp.wait_send()

    return start, finish


def _ag_sems(n):
    return [pltpu.SemaphoreType.DMA((6 * n,)), pltpu.SemaphoreType.DMA((6 * n,))]


def ag_weights(bufs):
    n = len(bufs)

    def body(*refs):
        start, finish = _ag_steps(refs[:n], refs[n:2 * n], *refs[2 * n:])
        start()
        finish()

    return _pcall(
        body, name="ag_weights", in_specs=[_HBM] * n, out_specs=[_HBM] * n,
        out_shape=[jax.ShapeDtypeStruct(w.shape, w.dtype) for w in bufs],
        input_output_aliases={i: i for i in range(n)}, scratch_shapes=_ag_sems(n),
    )(*bufs)


def swap_halves(gs, name):
    n = len(gs)

    def body(*refs):
        g, out = refs[:n], refs[n:2 * n]
        ssem, rsem = refs[2 * n:]
        x, y, c, _ = _place()
        cps = [_rcopy(g[i].at[:, _half_rows(1 - c, g[i].shape[1])], out[i], ssem.at[i], rsem.at[i], (x, y, 1 - c))
               for i in range(n)]
        for cp in cps:
            cp.start()
        for cp in cps:
            cp.wait()

    return _pcall(
        body, name=name, in_specs=[_HBM] * n, out_specs=[_HBM] * n,
        out_shape=[jax.ShapeDtypeStruct((4, g.shape[1] // 2, g.shape[2]), g.dtype) for g in gs],
        scratch_shapes=[pltpu.SemaphoreType.DMA((n,)), pltpu.SemaphoreType.DMA((n,))],
    )(*gs)


def _rs_steps(h, out, ssem, rsem):
    n = len(h)
    x, y, c, chips = _place()
    s = 2 * x + y
    cps = [_rcopy(h[i].at[2 * px + py], out[i].at[s], ssem.at[3 * i + j], rsem.at[3 * i + j], (px, py, c))
           for i in range(n) for j, (px, py) in enumerate(chips)]

    def start():
        for cp in cps:
            cp.start()

    def finish():
        for j, (px, py) in enumerate(chips):
            for i in range(n):
                _rcopy(h[i].at[s], out[i].at[2 * px + py], ssem.at[3 * i + j], rsem.at[3 * i + j], (px, py, c)).wait_recv()
        for cp in cps:
            cp.wait_send()

    return start, finish


def _rs_sems(n):
    return [pltpu.SemaphoreType.DMA((3 * n,)), pltpu.SemaphoreType.DMA((3 * n,))]


def share_sibling(ts):
    n = len(ts)

    def body(*refs):
        t, out = refs[:n], refs[n:2 * n]
        ssem, rsem = refs[2 * n:]
        x, y, c, _ = _place()
        cps = [_rcopy(t[i].at[c], out[i].at[c], ssem.at[i], rsem.at[i], (x, y, 1 - c)) for i in range(n)]
        for cp in cps:
            cp.start()
        for i in range(n):
            _rcopy(t[i].at[c], out[i].at[1 - c], ssem.at[i], rsem.at[i], (x, y, 1 - c)).wait_recv()
        for cp in cps:
            cp.wait_send()

    return _pcall(
        body, name="share_sibling", in_specs=[_HBM] * n, out_specs=[_HBM] * n,
        out_shape=[jax.ShapeDtypeStruct(t.shape, t.dtype) for t in ts],
        input_output_aliases={i: i for i in range(n)},
        scratch_shapes=[pltpu.SemaphoreType.DMA((n,)), pltpu.SemaphoreType.DMA((n,))],
    )(*ts)


def allreduce8(v, name):
    P, W = v.shape

    def body(v_ref, out_ref, buf, ssem, rsem):
        x, y, c, _ = _place()
        me = 4 * x + 2 * y + c
        buf[me] = v_ref[...]
        cps = []
        for k in range(1, 8):
            px = 1 - x if k & 4 else x
            py = 1 - y if k & 2 else y
            pc = 1 - c if k & 1 else c
            cp = _rcopy(buf.at[me], buf.at[me], ssem.at[k - 1], rsem.at[k - 1], (px, py, pc))
            cp.start()
            cps.append((cp, 4 * px + 2 * py + pc))
        for k, (cp, peer) in enumerate(cps):
            _rcopy(buf.at[me], buf.at[peer], ssem.at[k], rsem.at[k], (x, y, c)).wait_recv()
        for cp, _ in cps:
            cp.wait_send()
        acc = buf[0]
        for d in range(1, 8):
            acc = acc + buf[d]
        out_ref[...] = acc

    return _pcall(
        body, name=name, in_specs=[pl.BlockSpec(memory_space=pltpu.VMEM)],
        out_specs=pl.BlockSpec(memory_space=pltpu.VMEM), out_shape=jax.ShapeDtypeStruct((P, W), F32),
        scratch_shapes=[pltpu.VMEM((8, P, W), F32), pltpu.SemaphoreType.DMA((7,)), pltpu.SemaphoreType.DMA((7,))],
    )(v)


def add_half(g, rcv, cidx, name):
    _, hr, W = rcv.shape
    tr = _rows_tile(hr, W)
    nb = hr // tr

    def body(c_ref, g_ref, r_ref, o_ref):
        o_ref[...] = (g_ref[...] + r_ref[...]).astype(o_ref.dtype)

    return _pcall(
        body, name=name,
        grid_spec=pltpu.PrefetchScalarGridSpec(
            num_scalar_prefetch=1, grid=(4, nb),
            in_specs=[pl.BlockSpec((1, tr, W), lambda s, i, c: (s, c[0] * nb + i, 0)),
                      pl.BlockSpec((1, tr, W), lambda s, i, c: (s, i, 0))],
            out_specs=pl.BlockSpec((1, tr, W), lambda s, i, c: (s, i, 0))),
        out_shape=jax.ShapeDtypeStruct((4, hr, W), BF16), compiler_params=_cparams(("parallel", "parallel")),
    )(cidx, g, rcv)


def sum_chips(p, h, sc, name):
    _, hr, W = p.shape
    tr = _rows_tile(hr, W)

    def body(s_ref, c_ref, p_ref, h_ref, o_ref):
        s = s_ref[0]
        own = h_ref[0]
        f = lambda k: jnp.where(s == k, own, p_ref[k]).astype(F32)
        o_ref[0] = ((f(0) + f(1)) + f(2)) + f(3)

    return _pcall(
        body, name=name,
        grid_spec=pltpu.PrefetchScalarGridSpec(
            num_scalar_prefetch=2, grid=(hr // tr,),
            in_specs=[pl.BlockSpec((4, tr, W), lambda i, s, c: (0, i, 0)),
                      pl.BlockSpec((1, tr, W), lambda i, s, c: (s[0], i, 0))],
            out_specs=pl.BlockSpec((1, tr, W), lambda i, s, c: (c[0], i, 0))),
        out_shape=jax.ShapeDtypeStruct((2, hr, W), F32), compiler_params=_cparams(("parallel",)),
    )(sc[0], sc[1], p, h)


HI = lax.Precision.HIGHEST
BLOCK_BYTES = 3 * 512 * 1024


def _rows_tile(R, w, T=None):
    cands = [t for t in range(16, R + 1, 16) if R % t == 0 and (T is None or T % t == 0)]
    ok = [t for t in cands if t * w * 4 <= BLOCK_BYTES]
    return max(ok) if ok else min(cands)


def _gsum_exact(x):
    r = jnp.right_shift(lax.broadcasted_iota(jnp.int32, (LANE, LANE), 0), 6)
    c = jnp.right_shift(lax.broadcasted_iota(jnp.int32, (LANE, LANE), 1), 6)
    g = (r == c).astype(BF16)
    x1 = x.astype(BF16)
    r1 = x - x1.astype(F32)
    x2 = r1.astype(BF16)
    x3 = (r1 - x2.astype(F32)).astype(BF16)
    dot = lambda z: jnp.dot(z, g, preferred_element_type=F32)
    return (dot(x3) + dot(x2)) + dot(x1)


@jax.custom_vjp
def _gsum(x):
    return _gsum_exact(x)


_gsum.defvjp(lambda x: (_gsum_exact(x), None), lambda _, ct: (_gsum_exact(ct),))


def _rms(x, g):
    return x * lax.rsqrt(jnp.mean(x * x, axis=-1, keepdims=True) + NORM_EPS) * g


def _silu_mul(gate, up):
    return jax.nn.silu(gate) * up


def _shift(p, prev, mu):
    return p + mu * (prev - p)


def _prev_rows(p, above, tpos):
    first = lax.broadcasted_iota(jnp.int32, p.shape, 0) == 0
    prev = jnp.where(first, above[7:8, :], pltpu.roll(p, 1, 0))
    return jnp.where(tpos[:, :1] == 0.0, 0.0, prev)


def _shift_rows(p, above, tpos, mu):
    return _shift(p, _prev_rows(p, above, tpos), mu)


def _shift_rows_bwd(t_last, p, above, tpos, mu, ct, ct_below):
    dmu = jnp.sum(ct * (_prev_rows(p, above, tpos) - p), axis=0, keepdims=True)
    last = lax.broadcasted_iota(jnp.int32, p.shape, 0) == p.shape[0] - 1
    nxt = jnp.where(last, ct_below[0:1, :], pltpu.roll(ct, p.shape[0] - 1, 0))
    nxt = jnp.where(tpos[:, :1] == t_last, 0.0, nxt)
    return (1.0 - mu) * ct + mu * nxt, dmu


def _lora_act(p, prev, mu):
    s = _shift(p, prev, mu)
    return jax.nn.sigmoid(s[:, :G_LORA]), jnp.tanh(s[:, G_LORA:G_LORA + LANE]), s[:, G_LORA + LANE:]


def _prep(k, lw, la, w0, a0, kk_w, ka_w):
    wpre = -jax.nn.softplus(-(w0 + lw)) - 0.5
    decay = jnp.exp(-jnp.exp(wpre))
    a = jax.nn.sigmoid(a0 + la)
    kk = k * kk_w
    kk = kk * lax.rsqrt(jnp.maximum(_gsum(kk * kk), 1e-24))
    k2 = k * (1.0 + (a - 1.0) * ka_w)
    return decay, -kk, kk * a, k2


def _post(y, r, k2, v, g, gnw, gnb, rk):
    mean = _gsum(y) * (1.0 / RWKV_HEAD)
    d = y - mean
    var = _gsum(d * d) * (1.0 / RWKV_HEAD)
    yn = d * lax.rsqrt(var + GN_EPS) * gnw + gnb
    bonus = _gsum(r * k2 * rk) * v
    return (yn + bonus) * g


def _gate_mix(ga, gb, ya, o):
    return jax.nn.sigmoid(ga) * ya + jax.nn.sigmoid(gb) * o


def _rope(x, cos, sin):
    return x * cos + pltpu.roll(x, LANE // 2, 1) * sin


def _rope_t(dy, cos, sin):
    return dy * cos + pltpu.roll(dy * sin, LANE // 2, 1)


def _mla_pre(cq, ckv, kpe, cos, sin, qw, kvw):
    return _rms(cq, qw), _rms(ckv, kvw), _rope(kpe, cos, sin)


def _mla_pre_bwd(cq, ckv, dcqn, dckvn, dkr, cos, sin, qw, kvw):
    _, pull = jax.vjp(lambda a, b, c, d: (_rms(a, c), _rms(b, d)), cq, ckv, qw, kvw)
    dcq, dckv, dqw, dkvw = pull((dcqn, dckvn))
    return dcq, dckv, _rope_t(dkr, cos, sin), dqw, dkvw


WEIGHTS = ['meta_tokens', 'ffn1_norm', 'ffn1_w_gate', 'ffn1_w_up', 'ffn1_w_down', 'mix_norm', 'w_in', 'tm_mu', 'w0',
           'w_up', 'a0', 'a_up', 'g_up', 'k_k', 'k_a', 'r_k', 'gn_w', 'gn_b', 'q_norm', 'w_uq', 'kv_norm', 'w_ukv',
           'w_out', 'ffn2_norm', 'ffn2_w_gate', 'ffn2_w_up', 'ffn2_w_down', 'final_norm']
SHARD_AXIS = {'meta_tokens': 1, 'ffn1_w_gate': 1, 'ffn1_w_up': 1, 'ffn1_w_down': 0, 'w_in': 1, 'w_up': 1, 'a_up': 1,
              'g_up': 1, 'w_uq': 1, 'w_ukv': 1, 'w_out': 0, 'ffn2_w_gate': 1, 'ffn2_w_up': 1, 'ffn2_w_down': 0}
GATHERED = [n for n in WEIGHTS if n in SHARD_AXIS and n != 'meta_tokens']
FFN_IN = ('ffn1_w_gate', 'ffn1_w_up', 'ffn2_w_gate', 'ffn2_w_up')
PART_B = ['ffn2_w_gate', 'ffn2_w_up', 'ffn2_w_down', 'w_out']
PART_A1 = ['ffn1_w_gate', 'ffn1_w_up', 'ffn1_w_down']
PART_A2 = [n for n in GATHERED if n not in PART_B and n not in PART_A1]
SMALL = [n for n in WEIGHTS if n not in SHARD_AXIS]


def _to2d(a):
    if a.ndim == 1:
        return a.reshape(1, -1)
    if a.ndim == 3:
        return a.reshape(a.shape[0] * a.shape[1], a.shape[2]) if a.shape[0] == 1 and a.shape[1] > 64 else a.reshape(1, -1)
    return a


def _unpack(flat, shapes):
    out, off = [], 0
    for shp in shapes:
        n = shp[0] * shp[1]
        out.append(flat[off:off + n].reshape(shp))
        off += n
    return out


def _adamw(w, g, m, v):
    m = ADAM_B1 * m + (1.0 - ADAM_B1) * g
    v = ADAM_B2 * v + (1.0 - ADAM_B2) * jnp.square(g)
    m_hat = m / (1.0 - ADAM_B1 ** ADAM_STEP)
    v_hat = v / (1.0 - ADAM_B2 ** ADAM_STEP)
    delta = -ADAM_LR * (m_hat / (jnp.sqrt(v_hat) + ADAM_EPS) + ADAM_WD * w)
    return delta, m, v


def adamw(w, g, m, v, name):
    R, C = w.shape
    if R % 16 == 0 and R > 16:
        tm = _rows_tile(R, C)
    else:
        tm = R
    return rowwise(_adamw, [(w, C, 0), (g, C, 0), (m, C, 0), (v, C, 0)], [], [(C, C, F32)] * 3, tm=tm, name=name)


def _step(a):
    x = a['x']
    Bl, S, D = x.shape
    T = S + N_META
    R, RS = Bl * T, Bl * S
    Hr, Hm = D // RWKV_HEAD, D // LANE
    w2 = {n: _to2d(a[n]) for n in WEIGHTS}
    m2 = {n: _to2d(a['m_' + n]) for n in WEIGHTS}
    v2 = {n: _to2d(a['v_' + n]) for n in WEIGHTS}
    xi, yi, ci = lax.axis_index("x"), lax.axis_index("y"), lax.axis_index("c")
    chip = 2 * xi + yi

    i0 = jnp.zeros((), jnp.int32)
    chip32 = chip.astype(jnp.int32)

    def own_slot(w):
        return lax.dynamic_update_slice(lax.empty((4,) + w.shape, BF16), w.astype(BF16)[None], (chip32, i0, i0))

    stk = dict(zip(PART_A1, ag_weights([own_slot(w2[n]) for n in PART_A1])))

    def unstack(z, axis):
        return z.reshape(4 * z.shape[1], z.shape[2]) if axis == 0 else jnp.concatenate([z[s] for s in range(4)], axis=1)

    full = {n: unstack(stk[n], SHARD_AXIS[n]) for n in PART_A1 if n not in FFN_IN}
    mt = w2['meta_tokens']
    mcols = mt.shape[1]
    mt_z = lax.dynamic_update_slice(jnp.zeros((N_META, D), F32), 0.5 * mt, (jnp.zeros((), jnp.int32), (chip * mcols).astype(jnp.int32)))
    meta_full = allreduce8(mt_z.reshape(-1, LANE), "gather_meta").reshape(N_META, D)

    F = 4 * stk['ffn1_w_gate'].shape[2]
    half = ROPE_DIM // 2
    tmu = w2['tm_mu']
    mu_a = tmu[:, :3 * D]
    zm = lambda n: jnp.zeros((1, n), F32)
    mu_b = jnp.concatenate([tmu[:, 3 * D + 2 * W_LORA:], tmu[:, 3 * D:3 * D + W_LORA], zm(LANE - W_LORA),
                            tmu[:, 3 * D + W_LORA:3 * D + 2 * W_LORA], zm(LANE - A_LORA)], axis=1)
    pos = jnp.arange(T, dtype=F32)
    inv_freq = 1.0 / (ROPE_THETA ** (jnp.arange(0, ROPE_DIM, 2, dtype=F32) / ROPE_DIM))
    ang = pos[:, None] * inv_freq[None, :]
    zt = jnp.zeros((T, half), F32)
    cos_t = jnp.concatenate([jnp.cos(ang), zt, jnp.cos(ang), zt], axis=1)
    sin_t = jnp.concatenate([-jnp.sin(ang), zt, jnp.sin(ang), zt], axis=1)
    cos_q, sin_q = cos_t[N_META:], sin_t[N_META:]

    t_full = _rows_tile(R, D)
    t_512 = _rows_tile(R, 512, T)
    t_128 = _rows_tile(R, LANE, T)
    tq_full = _rows_tile(RS, D)
    tq_128 = _rows_tile(RS, LANE, S)

    def rows3(z):
        return z.reshape(Bl, T, z.shape[-1])

    def real_rows(z):
        return rows3(z)[:, N_META:].reshape(RS, z.shape[-1])

    def pad_meta(z):
        z3 = z.reshape(Bl, S, z.shape[-1])
        return jnp.concatenate([jnp.zeros((Bl, N_META, z.shape[-1]), z.dtype), z3], axis=1).reshape(R, z.shape[-1])

    def shift_down(z):
        z3 = rows3(z)
        return jnp.concatenate([jnp.zeros((Bl, 1, z.shape[-1]), z.dtype), z3[:, :-1]], axis=1).reshape(R, z.shape[-1])

    def shift_up(z):
        z3 = rows3(z)
        return jnp.concatenate([z3[:, 1:], jnp.zeros((Bl, 1, z.shape[-1]), z.dtype)], axis=1).reshape(R, z.shape[-1])

    def ffn_fwd(h, nw, wg, wu, wd, tag, ag_bufs=()):
        n = rowwise(_rms, [(h, D, 0)], [(nw, D, 0)], [(D, D, BF16)], tm=t_full, name=tag + "_norm")[0]
        res = mm_epi(n, [wg, wu], "nn", lambda g_, u_: (g_, u_, _silu_mul(g_, u_)), [], [F32, F32, BF16], b_stack=True,
                     carry=("ag", ag_bufs) if ag_bufs else None, name=tag + "_gate_up")
        (gate, up, act), got = res if ag_bufs else (res, [])
        out = mm(act, wd, res=h, alpha=0.5, name=tag + "_down")
        return (out, (h, n, gate, up, act), got) if ag_bufs else (out, (h, n, gate, up, act))

    def exch(g, name):
        return add_half(g, swap_halves([g], "swap_halves_" + name)[0], cidx, "add_half_" + name)

    def ffn_bwd(dh2, saved, nw, wg, wu, wd, tag, rs_hs=(), rs_own=()):
        h, n, gate, up, act = saved
        dz = rowwise(lambda d: 0.5 * d, [(dh2, 512, 0)], [], [(D, 512, BF16)], tm=t_512, ncb=D // 512, name=tag + "_dz")[0]
        d_wd = mm(act, dz, "tn", name=tag + "_dwd")
        res = mm_epi(dz, [wd], "nt", lambda da, g_, u_: vjp_fn(_silu_mul, 2)(g_, u_, da), [gate, up], [BF16, BF16],
                     carry=("rs", rs_hs) if rs_hs else None, name=tag + "_dact")
        (dgate, dup), got = res if rs_hs else (res, [])
        own = {}
        if rs_own:
            h_wd = exch(stacked(d_wd, 0), rs_own[2])
            d_wg, p_wd = mm(n, dgate, "tn", out_stack=True, carry=("rs", [h_wd]), name=tag + "_dwg")
            d_wu = mm(n, dup, "tn", out_stack=True, name=tag + "_dwu")
            h_wg = exch(d_wg, rs_own[0])
            dn, p_wg = mm(dgate, wg, "nt", b_stack=True, carry=("rs", [h_wg]), name=tag + "_dn1")
            h_wu = exch(d_wu, rs_own[1])
            dn, p_wu = mm(dup, wu, "nt", res=dn, b_stack=True, carry=("rs", [h_wu]), name=tag + "_dn2")
            own = {rs_own[0]: (p_wg[0], h_wg), rs_own[1]: (p_wu[0], h_wu), rs_own[2]: (p_wd[0], h_wd)}
        else:
            d_wg = mm(n, dgate, "tn", out_stack=True, name=tag + "_dwg")
            d_wu = mm(n, dup, "tn", out_stack=True, name=tag + "_dwu")
            dn = mm(dgate, wg, "nt", b_stack=True, name=tag + "_dn1")
            dn = mm(dup, wu, "nt", res=dn, b_stack=True, name=tag + "_dn2")

        def f(h_, dn_, dh_, nw_):
            dh, dnw = vjp_fn(_rms, 2)(h_, nw_, dn_)
            return dh + dh_, dnw

        dh, d_nw = rowwise(f, [(h, D, 0), (dn, D, 0), (dh2, D, 0)], [(nw, D, 0)], [(D, D, F32)], [(1, D, D)],
                           tm=t_full, name=tag + "_dnorm")
        return (dh, d_nw, d_wg, d_wu, d_wd, got, own) if (rs_hs or rs_own) else (dh, d_nw, d_wg, d_wu, d_wd)

    h0 = jnp.concatenate([jnp.broadcast_to(meta_full[None], (Bl, N_META, D)), x], axis=1).reshape(R, D)
    h1, sv1, got_a2 = ffn_fwd(h0, w2['ffn1_norm'], stk['ffn1_w_gate'], stk['ffn1_w_up'], full['ffn1_w_down'], "ffn1",
                              [own_slot(w2[n]) for n in PART_A2])
    stk.update(zip(PART_A2, got_a2))
    full.update({n: unstack(stk[n], SHARD_AXIS[n]) for n in PART_A2})
    win = full['w_in']
    o = 3 * D
    c_xw, c_xa, c_xg = win[:, o:o + W_LORA], win[:, o + W_LORA:o + 2 * W_LORA], win[:, o + 2 * W_LORA:o + 2 * W_LORA + G_LORA]
    o += 2 * W_LORA + G_LORA
    c_cq, c_ckv, c_kpe = win[:, o:o + Q_LORA], win[:, o + Q_LORA:o + Q_LORA + KV_LORA], win[:, o + Q_LORA + KV_LORA:o + Q_LORA + KV_LORA + ROPE_DIM]
    o += Q_LORA + KV_LORA + ROPE_DIM
    c_ga, c_gb = win[:, o:o + D], win[:, o + D:o + 2 * D]
    zc = lambda n: jnp.zeros((D, n), BF16)
    half = ROPE_DIM // 2
    NP0 = 5 * D + 512 + Q_LORA + KV_LORA + LANE
    NP = -(-NP0 // 512) * 512
    win_p = jnp.concatenate([win[:, :3 * D], c_ga, c_gb, c_xg, c_xw, zc(LANE - W_LORA), c_xa, zc(LANE - A_LORA), c_cq, c_ckv,
                             c_kpe[:, :half], zc(half), c_kpe[:, half:], zc(half), zc(NP - NP0)], axis=1)
    O_GA, O_GB, O_L, O_CQ, O_CKV, O_KPE = 3 * D, 4 * D, 5 * D, 5 * D + 512, 5 * D + 512 + Q_LORA, 5 * D + 512 + Q_LORA + KV_LORA
    zr = lambda n: jnp.zeros((n, D), BF16)
    w_up_p = jnp.concatenate([full['w_up'], zr(LANE - W_LORA)], axis=0)
    a_up_p = jnp.concatenate([full['a_up'], zr(LANE - A_LORA)], axis=0)
    g_up = full['g_up']
    wuq = full['w_uq'].reshape(Q_LORA, Hm, QK_DIM)
    zq = jnp.zeros((Q_LORA, Hm, half), BF16)
    wqn = wuq[:, :, :NOPE_DIM].reshape(Q_LORA, Hm * LANE)
    wqp = jnp.concatenate([wuq[:, :, NOPE_DIM:NOPE_DIM + half], zq, wuq[:, :, NOPE_DIM + half:], zq], axis=2).reshape(Q_LORA, Hm * LANE)
    wukv = full['w_ukv'].reshape(KV_LORA, Hm, NOPE_DIM + V_DIM)
    wkn = wukv[:, :, :NOPE_DIM].reshape(KV_LORA, Hm * LANE)
    wv = wukv[:, :, NOPE_DIM:].reshape(KV_LORA, Hm * LANE)
    u = rowwise(_rms, [(h1, D, 0)], [(w2['mix_norm'], D, 0)], [(D, D, BF16)], tm=t_full, name="mix_norm")[0]
    proj = mm(u, win_p, name="proj")
    tpos = jnp.broadcast_to(pos[:, None], (T, LANE))
    tpos_in = (tpos, LANE, 0, T // t_512, True)
    prev_b = shift_down(proj[:, O_L:O_L + 512])
    nb = D // 512
    r_s, k_s, v_s = [
        rowwise(_shift_rows, [(proj, 512, sec * nb), (proj, 512, sec * nb, None, False, "prev"), tpos_in],
                [(mu_a, 512, sec * nb)], [(D, 512, F32)], tm=t_512, ncb=nb, name="shift_" + tag)[0]
        for sec, tag in enumerate("rkv")]
    sg, txw, xas = rowwise(_lora_act, [(proj, 512, O_L // 512), (prev_b, 512, 0)], [(mu_b, 512, 0)],
                           [(G_LORA, G_LORA, BF16), (LANE, LANE, BF16), (LANE, LANE, BF16)], tm=t_512, name="lora_act")
    lw = mm(txw, w_up_p, name="lora_w")
    la = mm(xas, a_up_p, name="lora_a")
    g = mm(sg, g_up, name="lora_g")
    hb = D // LANE
    par_d = lambda n: (w2[n], LANE, 0)
    decay, kn, bb, k2 = rowwise(_prep, [(k_s, LANE, 0), (lw, LANE, 0), (la, LANE, 0)],
                                [par_d('w0'), par_d('a0'), par_d('k_k'), par_d('k_a')],
                                [(D, LANE, F32, True)] * 3 + [(D, LANE, F32)], tm=t_128, ncb=hb, t_rows=T, name="wkv_prep")

    NS = Bl * Hr

    def t_major(z):
        return lax.optimization_barrier(z.reshape(Bl, T, D).transpose(1, 0, 2))

    def b_major(z):
        return lax.optimization_barrier(z).transpose(1, 0, 2).reshape(R, D)

    def tm_to_j(z):
        z = z.reshape(T, NS, RWKV_HEAD).transpose(0, 2, 1)
        return jnp.broadcast_to(z[:, :, None, :], (T, RWKV_HEAD, 2, NS)).reshape(T, RWKV_HEAD, 2 * NS)

    def to_j(z):
        return tm_to_j(t_major(z))

    def j_to_tm(z):
        return z.transpose(0, 2, 1).reshape(T, Bl * D)

    def to_i(z):
        return t_major(z).reshape(T, NS, 2, RWKV_HEAD // 2).transpose(0, 3, 2, 1).reshape(T, RWKV_HEAD // 2, 2 * NS)

    def from_i(z):
        return b_major(z.reshape(T, RWKV_HEAD // 2, 2, NS).transpose(0, 3, 2, 1).reshape(T, Bl, D))

    def from_j(z):
        return b_major(z.transpose(0, 2, 1).reshape(T, Bl, D))

    jw, jkn, jb, jk, jr, iv = tm_to_j(decay), tm_to_j(kn), tm_to_j(bb), to_j(k2), to_j(r_s), to_i(v_s)
    y_i, sp, sa_i, got_b = wkv_fwd(jw, jkn, jb, jk, jr, iv, [own_slot(w2[n]) for n in PART_B])
    stk.update(zip(PART_B, got_b))
    full.update({n: unstack(stk[n], SHARD_AXIS[n]) for n in PART_B if n not in FFN_IN})
    wout = full['w_out']
    y = from_i(y_i)
    post_rows = [(y, LANE, 0), (r_s, LANE, 0), (k2, LANE, 0), (v_s, LANE, 0), (g, LANE, 0)]
    post_pars = [par_d('gn_w'), par_d('gn_b'), par_d('r_k')]
    ya = rowwise(_post, post_rows, post_pars, [(D, LANE, F32)], tm=t_128, ncb=hb, name="wkv_post")[0]

    nt512 = T // t_512
    mla_rows = [(proj, Q_LORA, O_CQ // Q_LORA), (proj, KV_LORA, O_CKV // KV_LORA), (proj, LANE, O_KPE // LANE)]
    tabs = [(cos_t, LANE, 0, nt512, True), (sin_t, LANE, 0, nt512, True)]
    mla_pars = [(w2['q_norm'], Q_LORA, 0), (w2['kv_norm'], KV_LORA, 0)]
    cqn, ckvn, kpr = rowwise(_mla_pre, mla_rows + tabs, mla_pars,
                             [(Q_LORA, Q_LORA, BF16), (KV_LORA, KV_LORA, BF16), (LANE, LANE, BF16)], tm=t_512, name="mla_pre")
    cqn_r = real_rows(cqn)
    qn = mm(cqn_r, wqn, out_dtype=BF16, name="q_nope")
    qp_raw = mm(cqn_r, wqp, name="q_pe")
    ntq = S // tq_128
    qtabs = [(cos_q, LANE, 0, ntq, True), (sin_q, LANE, 0, ntq, True)]
    qp = rowwise(_rope, [(qp_raw, LANE, 0)] + qtabs, [], [(D, LANE, BF16)], tm=tq_128, ncb=Hm, name="q_rope")[0]
    knope = mm(ckvn, wkn, out_dtype=BF16, name="k_nope")
    vv = mm(ckvn, wv, out_dtype=BF16, name="v_proj")

    def pad_keys(z):
        z3 = rows3(z)
        return jnp.concatenate([z3[:, :N_META], jnp.zeros((Bl, Q_BLOCK - N_META, z.shape[-1]), z.dtype), z3[:, N_META:]], axis=1)

    def unpad_keys(z):
        return jnp.concatenate([z[:, :N_META], z[:, Q_BLOCK:]], axis=1).reshape(R, z.shape[-1])

    qn3, qp3 = qn.reshape(Bl, S, D), qp.reshape(Bl, S, D)
    knp, kpp, vp = pad_keys(knope), pad_keys(kpr), pad_keys(vv)
    o3, _ = attn_fwd(qn3, qp3, knp, kpp, vp)
    o_att = pad_meta(o3.reshape(RS, D))
    mix_rows = [(proj, 512, O_GA // 512), (proj, 512, O_GB // 512), (ya, 512, 0), (o_att, 512, 0)]
    mix = rowwise(_gate_mix, mix_rows, [], [(D, 512, BF16)], tm=t_512, ncb=D // 512, name="gate_mix")[0]
    h2 = mm(mix, wout, res=h1, name="w_out")
    h3, sv2 = ffn_fwd(h2, w2['ffn2_norm'], stk['ffn2_w_gate'], stk['ffn2_w_up'], full['ffn2_w_down'], "ffn2")

    def loss_fb(h_, tgt, fw):
        yv, pull = jax.vjp(_rms, h_, fw)
        e = yv - tgt
        dh, dfw = pull(e * (1.0 / D))
        return dh, jnp.full((1, LANE), 0.5 / D * jnp.sum(e * e), F32), dfw

    dh3r, lossp, g_final = rowwise(loss_fb, [(real_rows(h3), D, 0), (a['loss_target'].reshape(RS, D), D, 0)],
                                   [(w2['final_norm'], D, 0)], [(D, D, F32)], [(1, LANE, LANE), (1, D, D)],
                                   tm=tq_full, name="loss")
    dh3 = pad_meta(dh3r)

    def stacked(g, axis):
        if g.ndim == 3:
            return g
        if axis == 0:
            return g.reshape(4, g.shape[0] // 4, g.shape[1])
        return g.reshape(g.shape[0], 4, g.shape[1] // 4).transpose(1, 0, 2)

    cidx = ci.reshape(1).astype(jnp.int32)
    gr = {'final_norm': g_final}
    dh2, gr['ffn2_norm'], gr['ffn2_w_gate'], gr['ffn2_w_up'], gr['ffn2_w_down'] = ffn_bwd(
        dh3, sv2, w2['ffn2_norm'], stk['ffn2_w_gate'], stk['ffn2_w_up'], full['ffn2_w_down'], "ffn2")
    dh2b = rowwise(lambda d: d, [(dh2, 512, 0)], [], [(D, 512, BF16)], tm=t_512, ncb=D // 512, name="dh2_cast")[0]
    gr['w_out'] = mm(mix, dh2b, "tn", name="d_wout")
    dmix = mm(dh2b, wout, "nt", name="d_mix")
    dga, dgb, dya, do = rowwise(vjp_fn(_gate_mix, 4), mix_rows + [(dmix, 512, 0)], [],
                                [(D, 512, BF16), (D, 512, BF16), (D, 512, F32), (D, 512, BF16)], tm=t_512, ncb=D // 512,
                                name="d_gate_mix")
    gs_b = [stacked(gr[n], SHARD_AXIS[n]) for n in PART_B]
    hs_b = [add_half(g_, r_, cidx, "add_half_" + n) for n, g_, r_ in zip(PART_B, gs_b, swap_halves(gs_b, "swap_halves_b"))]
    (dqn, dqp, dknp, dvp, dkpp), ps_b = attn_bwd(qn3, qp3, knp, kpp, vp, real_rows(do).reshape(Bl, S, D), hs_b)
    dqn2 = dqn.reshape(RS, D)
    dqp_raw = rowwise(_rope_t, [(dqp.reshape(RS, D), LANE, 0)] + qtabs, [], [(D, LANE, BF16)], tm=tq_128, ncb=Hm,
                      name="d_q_rope")[0]
    d_wqn = mm(cqn_r, dqn2, "tn", name="d_wqn")
    d_wqp = mm(cqn_r, dqp_raw, "tn", name="d_wqp")
    dcqn = mm(dqn2, wqn, "nt", name="d_cqn1")
    dcqn = pad_meta(mm(dqp_raw, wqp, "nt", res=dcqn, name="d_cqn2"))
    dkn2, dv2, dkp2 = unpad_keys(dknp), unpad_keys(dvp), unpad_keys(dkpp)
    d_wkn = mm(ckvn, dkn2, "tn", name="d_wkn")
    d_wv = mm(ckvn, dv2, "tn", name="d_wv")
    dckvn = mm(dkn2, wkn, "nt", name="d_ckvn1")
    dckvn = mm(dv2, wv, "nt", res=dckvn, name="d_ckvn2")
    dcq, dckv, dkpe, gr['q_norm'], gr['kv_norm'] = rowwise(
        _mla_pre_bwd, mla_rows[:2] + [(dcqn, Q_LORA, 0), (dckvn, KV_LORA, 0), (dkp2, LANE, 0)] + tabs, mla_pars,
        [(Q_LORA, Q_LORA, BF16), (KV_LORA, KV_LORA, BF16), (LANE, LANE, BF16)], [(1, Q_LORA, Q_LORA), (1, KV_LORA, KV_LORA)],
        tm=t_512, name="d_mla_pre")

    def post_bwd(y_, r_, k2_, v_, g_, dya_, gnw, gnb, rk):
        return vjp_fn(_post, 8)(y_, r_, k2_, v_, g_, gnw, gnb, rk, dya_)

    dy, dr_b, dk2_b, dv_b, dg, gr['gn_w'], gr['gn_b'], gr['r_k'] = rowwise(
        post_bwd, post_rows + [(dya, LANE, 0)], post_pars,
        [(D, LANE, F32)] * 4 + [(D, LANE, BF16)], [(1, D, LANE)] * 3, tm=t_128, ncb=hb, name="d_wkv_post")
    jdw, jdkn, jdb, jdk, jdr, idv = wkv_bwd(jw, jkn, jb, jk, jr, iv, to_i(dy), sp, sa_i)
    ddecay, dkn_w, db_w = j_to_tm(jdw), j_to_tm(jdkn), j_to_tm(jdb)
    dk2_w, dr_w, dv_w = from_j(jdk), from_j(jdr), from_i(idv)

    def prep_bwd(k_, lw_, la_, dd, dkn_, db_, dk2a, dk2b, w0, a0, kkw, kaw):
        return vjp_fn(_prep, 7)(k_, lw_, la_, w0, a0, kkw, kaw, dd, dkn_, db_, dk2a + dk2b)

    dk_s, dlw, dla, gr['w0'], gr['a0'], gr['k_k'], gr['k_a'] = rowwise(
        prep_bwd, [(k_s, LANE, 0), (lw, LANE, 0), (la, LANE, 0)]
        + [(z, LANE, 0, None, False, None, True) for z in (ddecay, dkn_w, db_w)] + [(dk2_w, LANE, 0), (dk2_b, LANE, 0)],
        [par_d('w0'), par_d('a0'), par_d('k_k'), par_d('k_a')], [(D, LANE, F32), (D, LANE, BF16), (D, LANE, BF16)],
        [(1, D, LANE)] * 4, tm=t_128, ncb=hb, t_rows=T, name="d_wkv_prep")
    d_wup = mm(txw, dlw, "tn", name="d_wup")
    dtxw = mm(dlw, w_up_p, "nt", name="d_txw")
    d_aup = mm(xas, dla, "tn", name="d_aup")
    dxa = mm(dla, a_up_p, "nt", name="d_xa")
    gr['g_up'] = mm(sg, dg, "tn", name="d_gup")
    dsg = mm(dg, g_up, "nt", name="d_sg")

    def lora_bwd(p_, prev_, dsg_, dt_, dxa_, mu_):
        return vjp_fn(_lora_act, 3)(p_, prev_, mu_, dsg_, dt_, dxa_)

    dpb, dprevb, dmu_b = rowwise(lora_bwd, [(proj, 512, O_L // 512), (prev_b, 512, 0), (dsg, G_LORA, 0), (dtxw, LANE, 0),
                                            (dxa, LANE, 0)], [(mu_b, 512, 0)], [(512, 512, F32)] * 2, [(1, 512, 512)],
                                 tm=t_512, name="d_lora_act")

    def shift_back(sec, cts, tag):
        nb = D // 512

        def f(p_, above, tp, *rest):
            cs, mu_ = rest[:-1], rest[-1]
            ct = cs[0] if len(cts) == 1 else cs[0] + cs[2]
            below = cs[1] if len(cts) == 1 else cs[1] + cs[3]
            return _shift_rows_bwd(float(T - 1), p_, above, tp, mu_, ct, below)

        ct_ins = [z for c in cts for z in ((c, 512, 0), (c, 512, 0, None, False, "next"))]
        return rowwise(f, [(proj, 512, sec * nb), (proj, 512, sec * nb, None, False, "prev"), tpos_in] + ct_ins,
                       [(mu_a, 512, sec * nb)], [(D, 512, BF16)], [(1, D, 512)], tm=t_512, ncb=nb, name="d_shift_" + tag)

    dsec_r, dmu_r = shift_back(0, [dr_w, dr_b], "r")
    dsec_k, dmu_k = shift_back(1, [dk_s], "k")
    dsec_v, dmu_v = shift_back(2, [dv_w, dv_b], "v")

    def add_cast(p_, q_):
        return p_ + q_

    def dsec(dp_, dprev_, tag):
        C = dp_.shape[1]
        return rowwise(add_cast, [(dp_, 512, 0), (shift_up(dprev_), 512, 0)], [], [(C, 512, BF16)], tm=t_512, ncb=C // 512,
                       name="d_sec_" + tag)[0]

    zpad = jnp.zeros((R, NP - NP0), BF16)
    dproj = jnp.concatenate([dsec_r, dsec_k, dsec_v, dga, dgb,
                             dsec(dpb, dprevb, "l"), dcq, dckv, dkpe, zpad], axis=1)
    d_win_p = mm(u, dproj, "tn", name="d_win")
    du = mm(dproj, win_p, "nt", name="d_u")

    def norm_bwd(h_, dn_, dh_, nw_):
        dh, dnw = vjp_fn(_rms, 2)(h_, nw_, dn_)
        return dh + dh_, dnw

    dh1, gr['mix_norm'] = rowwise(norm_bwd, [(h1, D, 0), (du, D, 0), (dh2, D, 0)], [(w2['mix_norm'], D, 0)],
                                  [(D, D, F32)], [(1, D, D)], tm=t_full, name="d_mix_norm")
    o1 = 3 * D + 2 * W_LORA + G_LORA
    o2 = o1 + Q_LORA + KV_LORA
    segs = [(0, 3 * D, 0), (3 * D, W_LORA, O_L + G_LORA), (3 * D + W_LORA, A_LORA, O_L + G_LORA + LANE),
            (3 * D + 2 * W_LORA, G_LORA, O_L), (o1, Q_LORA, O_CQ), (o1 + Q_LORA, KV_LORA, O_CKV), (o2, half, O_KPE),
            (o2 + half, half, O_KPE + 2 * half), (o2 + ROPE_DIM, 2 * D, O_GA)]
    cs_in = w2['w_in'].shape[1]

    def shard_cols(s):
        pieces = []
        for a0_, n_, p0_ in segs:
            x0, x1 = max(a0_, s * cs_in), min(a0_ + n_, (s + 1) * cs_in)
            if x0 < x1:
                pieces.append(d_win_p[:, p0_ + x0 - a0_:p0_ + x1 - a0_])
        return jnp.concatenate(pieces, axis=1)

    gr['w_in'] = jnp.stack([shard_cols(s) for s in range(4)])
    gr['tm_mu'] = jnp.concatenate([dmu_r, dmu_k, dmu_v, dmu_b[:, G_LORA:G_LORA + W_LORA],
                                   dmu_b[:, G_LORA + LANE:G_LORA + LANE + A_LORA], dmu_b[:, :G_LORA]], axis=1)
    gr['w_up'], gr['a_up'] = d_wup[:W_LORA], d_aup[:A_LORA]
    dq3n, dq3p = d_wqn.reshape(Q_LORA, Hm, LANE), d_wqp.reshape(Q_LORA, Hm, LANE)
    gr['w_uq'] = jnp.concatenate([dq3n, dq3p[:, :, :half], dq3p[:, :, 2 * half:3 * half]], axis=2).reshape(Q_LORA, Hm * QK_DIM)
    gr['w_ukv'] = jnp.concatenate([d_wkn.reshape(KV_LORA, Hm, LANE), d_wv.reshape(KV_LORA, Hm, LANE)], axis=2).reshape(
        KV_LORA, Hm * (NOPE_DIM + V_DIM))
    gs_a2 = [stacked(gr[n], SHARD_AXIS[n]) for n in PART_A2]
    hs_a2 = [add_half(g_, r_, cidx, "add_half_" + n) for n, g_, r_ in zip(PART_A2, gs_a2, swap_halves(gs_a2, "swap_halves_a2"))]
    dh0, gr['ffn1_norm'], gr['ffn1_w_gate'], gr['ffn1_w_up'], gr['ffn1_w_down'], ps_a2, own_a1 = ffn_bwd(
        dh1, sv1, w2['ffn1_norm'], stk['ffn1_w_gate'], stk['ffn1_w_up'], full['ffn1_w_down'], "ffn1", hs_a2, PART_A1)
    dh0_3 = rows3(dh0)
    grad_x = dh0_3[:, N_META:]
    gr['meta_tokens'] = jnp.sum(dh0_3[:, :N_META], axis=0)

    ps_a1, hs_a1 = [own_a1[n][0] for n in PART_A1], [own_a1[n][1] for n in PART_A1]
    sc = (chip32.reshape(1), cidx)
    ts = [sum_chips(p, h, sc, "sum_chips_" + n)
          for n, p, h in zip(PART_A1 + PART_A2 + PART_B, ps_a1 + ps_a2 + ps_b, hs_a1 + hs_a2 + hs_b)]
    g_shard = {n: z.reshape(w2[n].shape) for n, z in zip(PART_A1 + PART_A2 + PART_B, share_sibling(ts))}
    small = jnp.concatenate([gr[n].reshape(-1) for n in SMALL] + [gr['meta_tokens'].reshape(-1), lossp.reshape(-1)])
    ns = small.shape[0]
    nsp = -(-ns // (8 * LANE)) * 8 * LANE
    small_sum = allreduce8(jnp.pad(small, (0, nsp - ns)).reshape(-1, LANE), "allreduce_small").reshape(-1)
    g_small = dict(zip(SMALL + ['meta_full'], _unpack(small_sum, [w2[n].shape for n in SMALL] + [(N_META, D)])))
    g_shard['meta_tokens'] = lax.dynamic_slice(
        g_small['meta_full'], (jnp.zeros((), jnp.int32), (chip * mcols).astype(jnp.int32)), (N_META, mcols))
    loss = small_sum[ns - LANE]

    grads, deltas, new_m, new_v = [], [], [], []
    for n in WEIGHTS:
        gw = g_shard[n] if n in g_shard else g_small[n]
        d_, m_, v_ = adamw(w2[n], gw, m2[n], v2[n], "adamw_" + n)
        shp = a[n].shape
        grads.append(gw.reshape(shp))
        deltas.append(d_.reshape(shp))
        new_m.append(m_.reshape(shp))
        new_v.append(v_.reshape(shp))
    return (loss, grad_x, *grads, *deltas, *new_m, *new_v)


def kernel(x, meta_tokens, ffn1_norm, ffn1_w_gate, ffn1_w_up, ffn1_w_down, mix_norm, w_in, tm_mu, w0, w_up, a0, a_up, g_up, k_k, k_a, r_k, gn_w, gn_b, q_norm, w_uq, kv_norm, w_ukv, w_out, ffn2_norm, ffn2_w_gate, ffn2_w_up, ffn2_w_down, final_norm, loss_target, m_meta_tokens, m_ffn1_norm, m_ffn1_w_gate, m_ffn1_w_up, m_ffn1_w_down, m_mix_norm, m_w_in, m_tm_mu, m_w0, m_w_up, m_a0, m_a_up, m_g_up, m_k_k, m_k_a, m_r_k, m_gn_w, m_gn_b, m_q_norm, m_w_uq, m_kv_norm, m_w_ukv, m_w_out, m_ffn2_norm, m_ffn2_w_gate, m_ffn2_w_up, m_ffn2_w_down, m_final_norm, v_meta_tokens, v_ffn1_norm, v_ffn1_w_gate, v_ffn1_w_up, v_ffn1_w_down, v_mix_norm, v_w_in, v_tm_mu, v_w0, v_w_up, v_a0, v_a_up, v_g_up, v_k_k, v_k_a, v_r_k, v_gn_w, v_gn_b, v_q_norm, v_w_uq, v_kv_norm, v_w_ukv, v_w_out, v_ffn2_norm, v_ffn2_w_gate, v_ffn2_w_up, v_ffn2_w_down, v_final_norm):
    return _step(dict(locals()))
```

```python
import functools
import math

import jax
import jax.numpy as jnp
import numpy as np
from jax import lax
from jax.experimental import pallas as pl
from jax.experimental.pallas import tpu as pltpu

F32 = jnp.float32
BF16 = jnp.bfloat16
MESH = pl.DeviceIdType.MESH

N_META = 16
NORM_EPS = 1e-6
RWKV_HEAD = 64
GN_EPS = RWKV_HEAD * 1e-5
W_LORA, A_LORA, G_LORA = 96, 96, 256
Q_LORA, KV_LORA = 512, 512
NOPE_DIM, ROPE_DIM, V_DIM = 128, 64, 128
QK_DIM = NOPE_DIM + ROPE_DIM
ROPE_THETA = 10000.0
Q_BLOCK = 128
ADAM_LR, ADAM_B1, ADAM_B2, ADAM_EPS, ADAM_WD, ADAM_STEP = 0.001, 0.9, 0.999, 1e-08, 0.01, 10

LANE = 128
VMEM_LIMIT = 56 * 1024 * 1024


def _pcall(body, **kw):
    return pl.pallas_call(body, **kw)


def _cparams(sem):
    return pltpu.CompilerParams(dimension_semantics=sem, vmem_limit_bytes=VMEM_LIMIT)


MM_LANE_TILE = 1536
MM_ROW_TILE = 1408


def _div_tile(n, cap, unit):
    best = None
    for t in range(unit, min(n, cap) + 1, unit):
        if n % t == 0:
            best = t
    return best if best is not None else n


def _rtile(n, pref=512):
    best = None
    for t in range(16, min(n, pref * 2) + 1, 16):
        if n % t == 0 and (best is None or abs(t - pref) < abs(best - pref)):
            best = t
    return best if best is not None else n


def mm(a, b, mode="nn", out_dtype=F32, res=None, alpha=1.0, b_stack=False, out_stack=False, carry=None, name="mm"):
    kind, carried = carry if carry else (None, ())
    nc = len(carried)
    cs = b.shape[-1] if b_stack else None
    bs = (b.shape[1], 4 * cs) if b_stack else b.shape
    if mode == "nn":
        (M, K), (K2, N) = a.shape, bs
    elif mode == "nt":
        (M, K), (N, K2) = a.shape, bs
    else:
        (K, M), (K2, N) = a.shape, bs
    assert K == K2, (a.shape, b.shape, mode)
    if mode == "tn":
        tm, tk = _div_tile(M, MM_ROW_TILE, LANE), _div_tile(K, 1024, 16)
    else:
        tm = _div_tile(M, MM_ROW_TILE, 16)
        tk = _div_tile(cs if (b_stack and mode == "nt") else K, MM_LANE_TILE, LANE)
    ncol = N // 4 if out_stack else (cs if (b_stack and mode == "nn") else N)
    tn = _div_tile(ncol, MM_LANE_TILE if mode != "nt" else 1024, LANE)
    nk = K // tk
    dims = {"nn": (((1,), (0,)), ((), ())), "nt": (((1,), (1,)), ((), ())), "tn": (((0,), (0,)), ((), ()))}[mode]
    direct = out_dtype == F32

    def body(*refs):
        a_ref, b_ref = refs[:2]
        r_ref = refs[2] if res is not None else None
        nin = 3 if res is not None else 2
        o_ref = refs[nin + nc]
        acc = o_ref if direct else refs[nin + 2 * nc + 1]
        k = pl.program_id(2)
        if nc:
            steps = _ag_steps if kind == "ag" else _rs_steps
            start, finish = steps(refs[nin:nin + nc], refs[nin + nc + 1:nin + 2 * nc + 1],
                                  *refs[nin + 2 * nc + (1 if direct else 2):])
            i, j = pl.program_id(0), pl.program_id(1)
            pl.when((i == 0) & (j == 0) & (k == 0))(start)

        @pl.when(k == 0)
        def _():
            acc[...] = jnp.zeros_like(acc) if res is None else r_ref[...].astype(F32)

        p = lax.dot_general(a_ref[...].astype(BF16), b_ref[...].astype(BF16), dims, preferred_element_type=F32)
        acc[...] += p if alpha == 1.0 else alpha * p

        if not direct:
            @pl.when(k == nk - 1)
            def _():
                o_ref[...] = acc[...].astype(o_ref.dtype)

        if nc:
            pl.when((i == M // tm - 1) & (j == N // tn - 1) & (k == nk - 1))(finish)

    if mode == "tn":
        a_spec = pl.BlockSpec((tk, tm), lambda i, j, k: (k, i))
        b_spec = pl.BlockSpec((tk, tn), lambda i, j, k: (k, j))
    else:
        a_spec = pl.BlockSpec((tm, tk), lambda i, j, k: (i, k))
        if mode == "nn":
            if b_stack:
                nps = cs // tn
                b_spec = pl.BlockSpec((None, tk, tn), lambda i, j, k: (j // nps, k, j % nps))
            else:
                b_spec = pl.BlockSpec((tk, tn), lambda i, j, k: (k, j))
        elif b_stack:
            kps = cs // tk
            b_spec = pl.BlockSpec((None, tn, tk), lambda i, j, k: (k // kps, j, k % kps))
        else:
            b_spec = pl.BlockSpec((tn, tk), lambda i, j, k: (j, k))
    r_spec = pl.BlockSpec((tm, tn), lambda i, j, k: (i, j))
    if out_stack:
        ops = (N // 4) // tn
        o_spec = pl.BlockSpec((None, tm, tn), lambda i, j, k: (j // ops, i, j % ops))
        o_shape = jax.ShapeDtypeStruct((4, M, N // 4), out_dtype)
    else:
        o_spec, o_shape = r_spec, jax.ShapeDtypeStruct((M, N), out_dtype)
    in_specs = [a_spec, b_spec] + ([r_spec] if res is not None else [])
    args = (a, b) + ((res,) if res is not None else ())
    if not nc:
        return _pcall(
            body, name=name, grid=(M // tm, N // tn, nk), in_specs=in_specs, out_specs=o_spec, out_shape=o_shape,
            scratch_shapes=[] if direct else [pltpu.VMEM((tm, tn), F32)],
            compiler_params=_cparams(("parallel", "parallel", "arbitrary")),
        )(*args)
    out = _pcall(
        body, name=name, grid=(M // tm, N // tn, nk), in_specs=in_specs + [_HBM] * nc, out_specs=[o_spec] + [_HBM] * nc,
        out_shape=[o_shape] + [jax.ShapeDtypeStruct(z.shape, z.dtype) for z in carried],
        input_output_aliases={len(args) + q: 1 + q for q in range(nc)} if kind == "ag" else {},
        scratch_shapes=([] if direct else [pltpu.VMEM((tm, tn), F32)]) + (_ag_sems(nc) if kind == "ag" else _rs_sems(nc)),
        compiler_params=_cparams(("arbitrary",) * 3),
    )(*args, *carried)
    return out[0], list(out[1:])


MM_EPI_ROW_TILE = 704


def mm_epi(a, bs, mode, epi, extras, out_dtypes, b_stack=False, carry=None, name="mm_epi"):
    kind, carried = carry if carry else (None, ())
    nc = len(carried)
    b = bs[0]
    cs = b.shape[-1] if b_stack else None
    bshape = (b.shape[1], 4 * cs) if b_stack else b.shape
    (M, K) = a.shape
    (K2, N) = bshape if mode == "nn" else bshape[::-1]
    assert K == K2 and mode in ("nn", "nt"), (a.shape, b.shape, mode)
    tm = _div_tile(M, MM_EPI_ROW_TILE, 16)
    tk = _div_tile(cs if (b_stack and mode == "nt") else K, MM_LANE_TILE, LANE)
    tn = _div_tile(cs if (b_stack and mode == "nn") else N, MM_LANE_TILE, LANE)
    nk, nb, ne, no = K // tk, len(bs), len(extras), len(out_dtypes)
    dims = (((1,), (0,)), ((), ())) if mode == "nn" else (((1,), (1,)), ((), ()))

    def body(*refs):
        a_ref, b_refs, e_refs = refs[0], refs[1:1 + nb], refs[1 + nb:1 + nb + ne]
        base = 1 + nb + ne + nc
        o_refs, accs = refs[base:base + no], refs[base + no + nc:base + no + nc + nb]
        k = pl.program_id(2)
        if nc:
            steps = _ag_steps if kind == "ag" else _rs_steps
            start, finish = steps(refs[base - nc:base], refs[base + no:base + no + nc], *refs[base + no + nc + nb:])
            i, j = pl.program_id(0), pl.program_id(1)
            pl.when((i == 0) & (j == 0) & (k == 0))(start)

        @pl.when(k == 0)
        def _():
            for acc in accs:
                acc[...] = jnp.zeros_like(acc)

        av = a_ref[...].astype(BF16)
        for b_ref, acc in zip(b_refs, accs):
            acc[...] += lax.dot_general(av, b_ref[...].astype(BF16), dims, preferred_element_type=F32)

        @pl.when(k == nk - 1)
        def _():
            res = epi(*[acc[...] for acc in accs], *[e[...] for e in e_refs])
            for o_ref, v in zip(o_refs, res):
                o_ref[...] = v.astype(o_ref.dtype)

        if nc:
            pl.when((i == M // tm - 1) & (j == N // tn - 1) & (k == nk - 1))(finish)

    a_spec = pl.BlockSpec((tm, tk), lambda i, j, k: (i, k))
    if mode == "nn":
        if b_stack:
            nps = cs // tn
            b_spec = pl.BlockSpec((None, tk, tn), lambda i, j, k: (j // nps, k, j % nps))
        else:
            b_spec = pl.BlockSpec((tk, tn), lambda i, j, k: (k, j))
    elif b_stack:
        kps = cs // tk
        b_spec = pl.BlockSpec((None, tn, tk), lambda i, j, k: (k // kps, j, k % kps))
    else:
        b_spec = pl.BlockSpec((tn, tk), lambda i, j, k: (j, k))
    o_spec = pl.BlockSpec((tm, tn), lambda i, j, k: (i, j))
    sems = (_ag_sems(nc) if kind == "ag" else _rs_sems(nc)) if nc else []
    res = _pcall(
        body, name=name, grid=(M // tm, N // tn, nk), in_specs=[a_spec] + [b_spec] * nb + [o_spec] * ne + [_HBM] * nc,
        out_specs=[o_spec] * no + [_HBM] * nc,
        out_shape=[jax.ShapeDtypeStruct((M, N), dt) for dt in out_dtypes]
        + [jax.ShapeDtypeStruct(z.shape, z.dtype) for z in carried],
        input_output_aliases={1 + nb + ne + q: no + q for q in range(nc)} if kind == "ag" else {},
        scratch_shapes=[pltpu.VMEM((tm, tn), F32)] * nb + sems,
        compiler_params=_cparams(("arbitrary",) * 3 if nc else ("parallel", "parallel", "arbitrary")),
    )(a, *bs, *extras, *carried)
    return (list(res[:no]), list(res[no:])) if nc else res


def rowwise(fn, row_ins, par_ins, outs, accs=(), *, tm, ncb=1, name="rowwise"):
    R = row_ins[0][0].shape[0]
    assert R % tm == 0, (R, tm)
    nrb = R // tm
    in_specs, args = [], []
    for spec in row_ins:
        arr, w, base = spec[:3]
        mod = spec[3] if len(spec) > 3 else None
        cstep = 0 if (len(spec) > 4 and spec[4]) else 1
        halo = spec[5] if len(spec) > 5 else None
        if halo == "prev":
            in_specs.append(pl.BlockSpec((8, w), lambda j, i, base=base: (jnp.maximum(i * (tm // 8) - 1, 0), base + j)))
        elif halo == "next":
            in_specs.append(pl.BlockSpec((8, w), lambda j, i, base=base: (jnp.minimum((i + 1) * (tm // 8), R // 8 - 1), base + j)))
        elif mod is None:
            in_specs.append(pl.BlockSpec((tm, w), lambda j, i, base=base, cstep=cstep: (i, base + cstep * j)))
        else:
            in_specs.append(pl.BlockSpec((tm, w), lambda j, i, base=base, mod=mod, cstep=cstep: (i % mod, base + cstep * j)))
        args.append(arr)
    for arr, w, base in par_ins:
        in_specs.append(pl.BlockSpec((arr.shape[0], w), lambda j, i, base=base: (0, base + j)))
        args.append(arr)
    out_specs, out_shape = [], []
    for cols, w, dt in outs:
        out_specs.append(pl.BlockSpec((tm, w), lambda j, i: (i, j)))
        out_shape.append(jax.ShapeDtypeStruct((R, cols), dt))
    for p, cols, w in accs:
        out_specs.append(pl.BlockSpec((p, w), lambda j, i: (0, j)))
        out_shape.append(jax.ShapeDtypeStruct((p, cols), F32))
    nin, nout, nacc = len(args), len(outs), len(accs)

    def body(*refs):
        vals = [r[...] for r in refs[:nin]]
        res = fn(*vals)
        if not isinstance(res, (tuple, list)):
            res = (res,)
        assert len(res) == nout + nacc, (len(res), nout, nacc)
        for o_ref, v in zip(refs[nin:nin + nout], res[:nout]):
            o_ref[...] = v.astype(o_ref.dtype)
        if nacc:
            i = pl.program_id(1)

            @pl.when(i == 0)
            def _():
                for a_ref in refs[nin + nout:]:
                    a_ref[...] = jnp.zeros_like(a_ref)

            for a_ref, v in zip(refs[nin + nout:], res[nout:]):
                a_ref[...] += v.astype(F32)

    r = _pcall(
        body, name=name, grid=(ncb, nrb), in_specs=in_specs, out_specs=out_specs, out_shape=out_shape,
        compiler_params=_cparams(("parallel", "arbitrary")),
    )(*args)
    return r


def vjp_fn(fwd, nprim):
    def f(*vals):
        prim, cts = vals[:nprim], vals[nprim:]
        out, pull = jax.vjp(fwd, *[p.astype(F32) for p in prim])
        if not isinstance(out, (tuple, list)):
            cts = cts[0].astype(F32)
        else:
            cts = tuple(c.astype(F32) for c in cts)
        return pull(cts)
    return f


WKV_TB = 16


def wkv_fwd(w, kn, b, k, r, v, bufs=()):
    T, N, L = w.shape
    NI = v.shape[1]
    tb = WKV_TB
    n = len(bufs)
    assert T % tb == 0 and L == v.shape[2]

    def body(*refs):
        w_ref, kn_ref, b_ref, k_ref, r_ref, v_ref = refs[:6]
        y_ref, sp_ref, sa_ref = refs[6 + n:9 + n]
        s_ref = refs[9 + 2 * n]
        if n:
            start, finish = _ag_steps(refs[6:6 + n], refs[9 + n:9 + 2 * n], *refs[10 + 2 * n:])
            pl.when(pl.program_id(0) == 0)(start)

        @pl.when(pl.program_id(0) == 0)
        def _():
            s_ref[...] = jnp.zeros_like(s_ref)

        def step(s, carry):
            W, KN, B, Kk, Rr = w_ref[s], kn_ref[s], b_ref[s], k_ref[s], r_ref[s]
            for i in range(NI):
                S = s_ref[i]
                sp_ref[s, i] = S
                sa = jnp.sum(S * KN, axis=0, keepdims=True)
                sa_ref[s, pl.ds(i, 1), :] = sa
                vi = v_ref[s, pl.ds(i, 1), :]
                Sn = S * W + sa * B + vi * Kk
                s_ref[i] = Sn
                y_ref[s, pl.ds(i, 1), :] = jnp.sum(Sn * Rr, axis=0, keepdims=True)
            return carry

        lax.fori_loop(0, tb, step, 0)
        if n:
            pl.when(pl.program_id(0) == T // tb - 1)(finish)

    jspec = pl.BlockSpec((tb, N, L), lambda t: (t, 0, 0))
    ispec = pl.BlockSpec((tb, NI, L), lambda t: (t, 0, 0))
    ishape = jax.ShapeDtypeStruct((T, NI, L), F32)
    res = _pcall(
        body, name="wkv_fwd", grid=(T // tb,), in_specs=[jspec] * 5 + [ispec] + [_HBM] * n,
        out_specs=[ispec, pl.BlockSpec((tb, NI, N, L), lambda t: (t, 0, 0, 0)), ispec] + [_HBM] * n,
        out_shape=[ishape, jax.ShapeDtypeStruct((T, NI, N, L), F32), ishape]
        + [jax.ShapeDtypeStruct(z.shape, z.dtype) for z in bufs],
        input_output_aliases={6 + i: 3 + i for i in range(n)},
        scratch_shapes=[pltpu.VMEM((NI, N, L), F32)] + (_ag_sems(n) if n else []),
        compiler_params=_cparams(("arbitrary",)),
    )(w, kn, b, k, r, v, *bufs)
    return res[0], res[1], res[2], list(res[3:])


def wkv_bwd(w, kn, b, k, r, v, dy, sp, sa):
    T, N, L = w.shape
    NI, LH = v.shape[1], L // 2
    tb = WKV_TB
    nt = T // tb

    def body(w_ref, kn_ref, b_ref, k_ref, r_ref, v_ref, dy_ref, sp_ref, sa_ref,
             dw_ref, dkn_ref, db_ref, dk_ref, dr_ref, dv_ref, ds_ref):
        @pl.when(pl.program_id(0) == 0)
        def _():
            ds_ref[...] = jnp.zeros_like(ds_ref)

        def step(q, carry):
            s = tb - 1 - q
            W, KN, B, Kk, Rr = w_ref[s], kn_ref[s], b_ref[s], k_ref[s], r_ref[s]
            dW = jnp.zeros((N, L), F32)
            dKN, dB, dK, T1 = dW, dW, dW, dW
            al = jnp.zeros((1, L), F32)
            be = al
            for i in range(NI):
                Sp = sp_ref[s, i]
                vi = v_ref[s, pl.ds(i, 1), :]
                dyi = dy_ref[s, pl.ds(i, 1), :]
                sai = sa_ref[s, pl.ds(i, 1), :]
                dS = ds_ref[i] + dyi * Rr
                T1 = T1 + Sp * dyi
                al = al + sai * dyi
                be = be + vi * dyi
                dv_ref[s, pl.ds(i, 1), :] = jnp.sum(dS * Kk, axis=0, keepdims=True)
                dK = dK + dS * vi
                dsa = jnp.sum(dS * B, axis=0, keepdims=True)
                dB = dB + dS * sai
                dW = dW + dS * Sp
                dKN = dKN + Sp * dsa
                ds_ref[i] = dS * W + dsa * KN
            dR = W * T1 + B * al + Kk * be
            for ref, val in ((dw_ref, dW), (dkn_ref, dKN), (db_ref, dB), (dk_ref, dK), (dr_ref, dR)):
                ref[s] = (val + pltpu.roll(val, LH, 1))[:, :LH]
            return carry

        lax.fori_loop(0, tb, step, 0)

    jspec = pl.BlockSpec((tb, N, L), lambda t: (nt - 1 - t, 0, 0))
    gspec = pl.BlockSpec((tb, N, LH), lambda t: (nt - 1 - t, 0, 0))
    ispec = pl.BlockSpec((tb, NI, L), lambda t: (nt - 1 - t, 0, 0))
    gshape = jax.ShapeDtypeStruct((T, N, LH), F32)
    return _pcall(
        body, name="wkv_bwd", grid=(nt,),
        in_specs=[jspec] * 5 + [ispec, ispec, pl.BlockSpec((tb, NI, N, L), lambda t: (nt - 1 - t, 0, 0, 0)), ispec],
        out_specs=[gspec] * 5 + [ispec], out_shape=[gshape] * 5 + [jax.ShapeDtypeStruct((T, NI, L), F32)],
        scratch_shapes=[pltpu.VMEM((NI, N, L), F32)],
        compiler_params=_cparams(("arbitrary",)),
    )(w, kn, b, k, r, v, dy, sp, sa)


_NT = (((1,), (1,)), ((), ()))
_TN = (((0,), (0,)), ((), ()))
ATT_SCALE = QK_DIM ** -0.5


def _att_bias():
    col = lax.broadcasted_iota(jnp.int32, (Q_BLOCK, Q_BLOCK), 1)
    row = lax.broadcasted_iota(jnp.int32, (Q_BLOCK, Q_BLOCK), 0)
    return jnp.where(col < N_META, 0.0, -1e30).astype(F32), jnp.where(col <= row, 0.0, -1e30).astype(F32)


def _att_scores(q1, q2, kn_ref, kp_ref, s_ref, bias, L):
    s = lax.dot_general(q1, kn_ref[0, :L, :], _NT, preferred_element_type=F32)
    s = s + lax.dot_general(q2, kp_ref[0, :L, :], _NT, preferred_element_type=F32)
    s_ref[:, :L] = s * ATT_SCALE
    s_ref[:, :Q_BLOCK] += bias[0]
    s_ref[:, L - Q_BLOCK:L] += bias[1]


def _att_probs(s_ref, L):
    s = s_ref[:, :L]
    m = jnp.max(s, axis=-1, keepdims=True)
    p = jnp.exp(s - m)
    return p / jnp.sum(p, axis=-1, keepdims=True)


def _att_blocks(nq, qn_ref, qp_ref, kn_ref, kp_ref, s_ref):
    bias = _att_bias()
    rows = lambda i: pl.ds(Q_BLOCK * i, Q_BLOCK)
    L = lambda i: Q_BLOCK * (i + 2)
    score = lambda i: _att_scores(qn_ref[0, rows(i), :], qp_ref[0, rows(i), :], kn_ref, kp_ref, s_ref.at[i % 2], bias, L(i))
    score(0)
    for i in range(nq):
        if i + 1 < nq:
            score(i + 1)
        yield i, rows(i), L(i), _att_probs(s_ref.at[i % 2], L(i))


def _edge_steps(nb, nh):
    b, h = pl.program_id(0), pl.program_id(1)
    return (b == 0) & (h == 0), (b == nb - 1) & (h == nh - 1)


def attn_fwd(qn, qp, kn, kp, v, bufs=()):
    B, S, HD = qn.shape
    SP = kn.shape[1]
    H = HD // LANE
    nq = S // Q_BLOCK
    n = len(bufs)

    def body(*refs):
        qn_ref, qp_ref, kn_ref, kp_ref, v_ref = refs[:5]
        o_ref = refs[5 + n]
        s_ref = refs[6 + 2 * n]
        if n:
            start, finish = _ag_steps(refs[5:5 + n], refs[6 + n:6 + 2 * n], *refs[7 + 2 * n:])
            is_first, is_last = _edge_steps(B, H)
            pl.when(is_first)(start)
        for i, rows, L, p in _att_blocks(nq, qn_ref, qp_ref, kn_ref, kp_ref, s_ref):
            o_ref[0, rows, :] = jnp.dot(p.astype(BF16), v_ref[0, :L, :], preferred_element_type=F32)
        if n:
            pl.when(is_last)(finish)

    qspec = pl.BlockSpec((1, S, LANE), lambda b, h: (b, 0, h))
    kspec = pl.BlockSpec((1, SP, LANE), lambda b, h: (b, 0, h))
    pspec = pl.BlockSpec((1, SP, LANE), lambda b, h: (b, 0, 0))
    res = _pcall(
        body, name="attn_fwd", grid=(B, H), in_specs=[qspec, qspec, kspec, pspec, kspec] + [_HBM] * n,
        out_specs=[qspec] + [_HBM] * n,
        out_shape=[jax.ShapeDtypeStruct((B, S, HD), F32)] + [jax.ShapeDtypeStruct(w.shape, w.dtype) for w in bufs],
        input_output_aliases={5 + i: 1 + i for i in range(n)},
        scratch_shapes=[pltpu.VMEM((2, Q_BLOCK, SP), F32)] + (_ag_sems(n) if n else []),
        compiler_params=_cparams(("arbitrary", "arbitrary")),
    )(qn, qp, kn, kp, v, *bufs)
    return res[0], list(res[1:])


def attn_bwd(qn, qp, kn, kp, v, do, hs=()):
    B, S, HD = qn.shape
    SP = kn.shape[1]
    H = HD // LANE
    nq = S // Q_BLOCK
    n = len(hs)

    def body(*refs):
        qn_ref, qp_ref, kn_ref, kp_ref, v_ref, do_ref = refs[:6]
        dqn_ref, dqp_ref, dkn_ref, dv_ref, dkp_ref = refs[6 + n:11 + n]
        dkn_acc, dv_acc, s_ref = refs[11 + 2 * n:14 + 2 * n]
        if n:
            start, finish = _rs_steps(refs[6:6 + n], refs[11 + n:11 + 2 * n], *refs[14 + 2 * n:])
            is_first, is_last = _edge_steps(B, H)
            pl.when(is_first)(start)

        @pl.when(pl.program_id(1) == 0)
        def _():
            dkp_ref[...] = jnp.zeros_like(dkp_ref)

        dkn_acc[...] = jnp.zeros_like(dkn_acc)
        dv_acc[...] = jnp.zeros_like(dv_acc)
        for i, rows, L, p in _att_blocks(nq, qn_ref, qp_ref, kn_ref, kp_ref, s_ref):
            q1, q2, do_i = qn_ref[0, rows, :], qp_ref[0, rows, :], do_ref[0, rows, :]
            dp = lax.dot_general(do_i, v_ref[0, :L, :], _NT, preferred_element_type=F32)
            ds = (p * (dp - jnp.sum(p * dp, axis=-1, keepdims=True)) * ATT_SCALE).astype(BF16)
            dqn_ref[0, rows, :] = jnp.dot(ds, kn_ref[0, :L, :], preferred_element_type=F32).astype(dqn_ref.dtype)
            dqp_ref[0, rows, :] = jnp.dot(ds, kp_ref[0, :L, :], preferred_element_type=F32)
            dkn_acc[:L, :] += lax.dot_general(ds, q1, _TN, preferred_element_type=F32)
            dkp_ref[0, :L, :] += lax.dot_general(ds, q2, _TN, preferred_element_type=F32)
            dv_acc[:L, :] += lax.dot_general(p.astype(BF16), do_i, _TN, preferred_element_type=F32)
        dkn_ref[0] = dkn_acc[...].astype(dkn_ref.dtype)
        dv_ref[0] = dv_acc[...].astype(dv_ref.dtype)
        if n:
            pl.when(is_last)(finish)

    qspec = pl.BlockSpec((1, S, LANE), lambda b, h: (b, 0, h))
    kspec = pl.BlockSpec((1, SP, LANE), lambda b, h: (b, 0, h))
    pspec = pl.BlockSpec((1, SP, LANE), lambda b, h: (b, 0, 0))
    res = _pcall(
        body, name="attn_bwd", grid=(B, H), in_specs=[qspec, qspec, kspec, pspec, kspec, qspec] + [_HBM] * n,
        out_specs=[qspec, qspec, kspec, kspec, pspec] + [_HBM] * n,
        out_shape=[jax.ShapeDtypeStruct((B, S, HD), BF16), jax.ShapeDtypeStruct((B, S, HD), F32),
                   jax.ShapeDtypeStruct((B, SP, HD), BF16), jax.ShapeDtypeStruct((B, SP, HD), BF16),
                   jax.ShapeDtypeStruct((B, SP, LANE), F32)] + [jax.ShapeDtypeStruct(h.shape, h.dtype) for h in hs],
        scratch_shapes=[pltpu.VMEM((SP, LANE), F32), pltpu.VMEM((SP, LANE), F32), pltpu.VMEM((2, Q_BLOCK, SP), F32)]
        + (_rs_sems(n) if n else []),
        compiler_params=_cparams(("arbitrary", "arbitrary")),
    )(qn, qp, kn, kp, v, do, *hs)
    return res[:5], list(res[5:])


_HBM = pl.BlockSpec(memory_space=pltpu.HBM)


def _place():
    x, y, c = lax.axis_index("x"), lax.axis_index("y"), lax.axis_index("c")
    chips = [(1 - x, y), (x, 1 - y), (1 - x, 1 - y)]
    return x, y, c, chips


def _rcopy(src, dst, ssem, rsem, dev):
    return pltpu.make_async_remote_copy(src_ref=src, dst_ref=dst, send_sem=ssem, recv_sem=rsem,
                                        device_id=dev, device_id_type=MESH)


def _half_rows(c, r):
    return pl.ds(pl.multiple_of(c * (r // 2), 16), r // 2)


def _ag_steps(w, out, ssem, rsem):
    n = len(w)
    x, y, c, chips = _place()
    s = 2 * x + y
    rows = [_half_rows(c, w[i].shape[1]) for i in range(n)]
    orows = [_half_rows(1 - c, w[i].shape[1]) for i in range(n)]
    first = [_rcopy(w[i].at[s, rows[i]], out[i].at[s, rows[i]], ssem.at[6 * i + j], rsem.at[6 * i + j], (px, py, c))
             for i in range(n) for j, (px, py) in enumerate(chips)]

    def start():
        for cp in first:
            cp.start()

    def finish():
        passed = []
        for j, (px, py) in enumerate(chips):
            sp = 2 * px + py
            for i in range(n):
                here = out[i].at[sp, rows[i]]
                _rcopy(here, here, ssem.at[6 * i + j], rsem.at[6 * i + j], (px, py, c)).wait_recv()
                fw = _rcopy(here, here, ssem.at[6 * i + 3 + j], rsem.at[6 * i + 3 + j], (x, y, 1 - c))
                fw.start()
                passed.append(fw)
        for j, (px, py) in enumerate(chips):
            sp = 2 * px + py
            for i in range(n):
                there = out[i].at[sp, orows[i]]
                _rcopy(there, there, ssem.at[6 * i + 3 + j], rsem.at[6 * i + 3 + j], (x, y, 1 - c)).wait_recv()
        for cp in first + passed:
            cp.wait_send()

    return start, finish


def _ag_sems(n):
    return [pltpu.SemaphoreType.DMA((6 * n,)), pltpu.SemaphoreType.DMA((6 * n,))]


def ag_weights(bufs):
    n = len(bufs)

    def body(*refs):
        start, finish = _ag_steps(refs[:n], refs[n:2 * n], *refs[2 * n:])
        start()
        finish()

    return _pcall(
        body, name="ag_weights", in_specs=[_HBM] * n, out_specs=[_HBM] * n,
        out_shape=[jax.ShapeDtypeStruct(w.shape, w.dtype) for w in bufs],
        input_output_aliases={i: i for i in range(n)}, scratch_shapes=_ag_sems(n),
    )(*bufs)


def swap_halves(gs, name):
    n = len(gs)

    def body(*refs):
        g, out = refs[:n], refs[n:2 * n]
        ssem, rsem = refs[2 * n:]
        x, y, c, _ = _place()
        cps = [_rcopy(g[i].at[:, _half_rows(1 - c, g[i].shape[1])], out[i], ssem.at[i], rsem.at[i], (x, y, 1 - c))
               for i in range(n)]
        for cp in cps:
            cp.start()
        for cp in cps:
            cp.wait()

    return _pcall(
        body, name=name, in_specs=[_HBM] * n, out_specs=[_HBM] * n,
        out_shape=[jax.ShapeDtypeStruct((4, g.shape[1] // 2, g.shape[2]), g.dtype) for g in gs],
        scratch_shapes=[pltpu.SemaphoreType.DMA((n,)), pltpu.SemaphoreType.DMA((n,))],
    )(*gs)


def _rs_steps(h, out, ssem, rsem):
    n = len(h)
    x, y, c, chips = _place()
    s = 2 * x + y
    cps = [_rcopy(h[i].at[2 * px + py], out[i].at[s], ssem.at[3 * i + j], rsem.at[3 * i + j], (px, py, c))
           for i in range(n) for j, (px, py) in enumerate(chips)]

    def start():
        for cp in cps:
            cp.start()

    def finish():
        for j, (px, py) in enumerate(chips):
            for i in range(n):
                _rcopy(h[i].at[s], out[i].at[2 * px + py], ssem.at[3 * i + j], rsem.at[3 * i + j], (px, py, c)).wait_recv()
        for cp in cps:
            cp.wait_send()

    return start, finish


def _rs_sems(n):
    return [pltpu.SemaphoreType.DMA((3 * n,)), pltpu.SemaphoreType.DMA((3 * n,))]


def share_sibling(ts):
    n = len(ts)

    def body(*refs):
        t, out = refs[:n], refs[n:2 * n]
        ssem, rsem = refs[2 * n:]
        x, y, c, _ = _place()
        cps = [_rcopy(t[i].at[c], out[i].at[c], ssem.at[i], rsem.at[i], (x, y, 1 - c)) for i in range(n)]
        for cp in cps:
            cp.start()
        for i in range(n):
            _rcopy(t[i].at[c], out[i].at[1 - c], ssem.at[i], rsem.at[i], (x, y, 1 - c)).wait_recv()
        for cp in cps:
            cp.wait_send()

    return _pcall(
        body, name="share_sibling", in_specs=[_HBM] * n, out_specs=[_HBM] * n,
        out_shape=[jax.ShapeDtypeStruct(t.shape, t.dtype) for t in ts],
        input_output_aliases={i: i for i in range(n)},
        scratch_shapes=[pltpu.SemaphoreType.DMA((n,)), pltpu.SemaphoreType.DMA((n,))],
    )(*ts)


def allreduce8(v, name):
    P, W = v.shape

    def body(v_ref, out_ref, buf, ssem, rsem):
        x, y, c, _ = _place()
        me = 4 * x + 2 * y + c
        buf[me] = v_ref[...]
        cps = []
        for k in range(1, 8):
            px = 1 - x if k & 4 else x
            py = 1 - y if k & 2 else y
            pc = 1 - c if k & 1 else c
            cp = _rcopy(buf.at[me], buf.at[me], ssem.at[k - 1], rsem.at[k - 1], (px, py, pc))
            cp.start()
            cps.append((cp, 4 * px + 2 * py + pc))
        for k, (cp, peer) in enumerate(cps):
            _rcopy(buf.at[me], buf.at[peer], ssem.at[k], rsem.at[k], (x, y, c)).wait_recv()
        for cp, _ in cps:
            cp.wait_send()
        acc = buf[0]
        for d in range(1, 8):
            acc = acc + buf[d]
        out_ref[...] = acc

    return _pcall(
        body, name=name, in_specs=[pl.BlockSpec(memory_space=pltpu.VMEM)],
        out_specs=pl.BlockSpec(memory_space=pltpu.VMEM), out_shape=jax.ShapeDtypeStruct((P, W), F32),
        scratch_shapes=[pltpu.VMEM((8, P, W), F32), pltpu.SemaphoreType.DMA((7,)), pltpu.SemaphoreType.DMA((7,))],
    )(v)


def add_half(g, rcv, cidx, name):
    _, hr, W = rcv.shape
    tr = _rows_tile(hr, W)
    nb = hr // tr

    def body(c_ref, g_ref, r_ref, o_ref):
        o_ref[...] = (g_ref[...] + r_ref[...]).astype(o_ref.dtype)

    return _pcall(
        body, name=name,
        grid_spec=pltpu.PrefetchScalarGridSpec(
            num_scalar_prefetch=1, grid=(4, nb),
            in_specs=[pl.BlockSpec((1, tr, W), lambda s, i, c: (s, c[0] * nb + i, 0)),
                      pl.BlockSpec((1, tr, W), lambda s, i, c: (s, i, 0))],
            out_specs=pl.BlockSpec((1, tr, W), lambda s, i, c: (s, i, 0))),
        out_shape=jax.ShapeDtypeStruct((4, hr, W), BF16), compiler_params=_cparams(("parallel", "parallel")),
    )(cidx, g, rcv)


def sum_chips(p, h, sc, name):
    _, hr, W = p.shape
    tr = _rows_tile(hr, W)

    def body(s_ref, c_ref, p_ref, h_ref, o_ref):
        s = s_ref[0]
        own = h_ref[0]
        f = lambda k: jnp.where(s == k, own, p_ref[k]).astype(F32)
        o_ref[0] = ((f(0) + f(1)) + f(2)) + f(3)

    return _pcall(
        body, name=name,
        grid_spec=pltpu.PrefetchScalarGridSpec(
            num_scalar_prefetch=2, grid=(hr // tr,),
            in_specs=[pl.BlockSpec((4, tr, W), lambda i, s, c: (0, i, 0)),
                      pl.BlockSpec((1, tr, W), lambda i, s, c: (s[0], i, 0))],
            out_specs=pl.BlockSpec((1, tr, W), lambda i, s, c: (c[0], i, 0))),
        out_shape=jax.ShapeDtypeStruct((2, hr, W), F32), compiler_params=_cparams(("parallel",)),
    )(sc[0], sc[1], p, h)


HI = lax.Precision.HIGHEST
BLOCK_BYTES = 3 * 512 * 1024


def _rows_tile(R, w, T=None):
    cands = [t for t in range(16, R + 1, 16) if R % t == 0 and (T is None or T % t == 0)]
    ok = [t for t in cands if t * w * 4 <= BLOCK_BYTES]
    return max(ok) if ok else min(cands)


def _gsum_exact(x):
    r = jnp.right_shift(lax.broadcasted_iota(jnp.int32, (LANE, LANE), 0), 6)
    c = jnp.right_shift(lax.broadcasted_iota(jnp.int32, (LANE, LANE), 1), 6)
    g = (r == c).astype(BF16)
    x1 = x.astype(BF16)
    r1 = x - x1.astype(F32)
    x2 = r1.astype(BF16)
    x3 = (r1 - x2.astype(F32)).astype(BF16)
    dot = lambda z: jnp.dot(z, g, preferred_element_type=F32)
    return (dot(x3) + dot(x2)) + dot(x1)


@jax.custom_vjp
def _gsum(x):
    return _gsum_exact(x)


_gsum.defvjp(lambda x: (_gsum_exact(x), None), lambda _, ct: (_gsum_exact(ct),))


def _rms(x, g):
    return x * lax.rsqrt(jnp.mean(x * x, axis=-1, keepdims=True) + NORM_EPS) * g


def _silu_mul(gate, up):
    return jax.nn.silu(gate) * up


def _shift(p, prev, mu):
    return p + mu * (prev - p)


def _prev_rows(p, above, tpos):
    first = lax.broadcasted_iota(jnp.int32, p.shape, 0) == 0
    prev = jnp.where(first, above[7:8, :], pltpu.roll(p, 1, 0))
    return jnp.where(tpos[:, :1] == 0.0, 0.0, prev)


def _shift_rows(p, above, tpos, mu):
    return _shift(p, _prev_rows(p, above, tpos), mu)


def _shift_rows_bwd(t_last, p, above, tpos, mu, ct, ct_below):
    dmu = jnp.sum(ct * (_prev_rows(p, above, tpos) - p), axis=0, keepdims=True)
    last = lax.broadcasted_iota(jnp.int32, p.shape, 0) == p.shape[0] - 1
    nxt = jnp.where(last, ct_below[0:1, :], pltpu.roll(ct, p.shape[0] - 1, 0))
    nxt = jnp.where(tpos[:, :1] == t_last, 0.0, nxt)
    return (1.0 - mu) * ct + mu * nxt, dmu


def _lora_act(p, prev, mu):
    s = _shift(p, prev, mu)
    return jax.nn.sigmoid(s[:, :G_LORA]), jnp.tanh(s[:, G_LORA:G_LORA + LANE]), s[:, G_LORA + LANE:]


def _prep(k, lw, la, w0, a0, kk_w, ka_w):
    wpre = -jax.nn.softplus(-(w0 + lw)) - 0.5
    decay = jnp.exp(-jnp.exp(wpre))
    a = jax.nn.sigmoid(a0 + la)
    kk = k * kk_w
    kk = kk * lax.rsqrt(jnp.maximum(_gsum(kk * kk), 1e-24))
    k2 = k * (1.0 + (a - 1.0) * ka_w)
    return decay, -kk, kk * a, k2


def _post(y, r, k2, v, g, gnw, gnb, rk):
    mean = _gsum(y) * (1.0 / RWKV_HEAD)
    d = y - mean
    var = _gsum(d * d) * (1.0 / RWKV_HEAD)
    yn = d * lax.rsqrt(var + GN_EPS) * gnw + gnb
    bonus = _gsum(r * k2 * rk) * v
    return (yn + bonus) * g


def _gate_mix(ga, gb, ya, o):
    return jax.nn.sigmoid(ga) * ya + jax.nn.sigmoid(gb) * o


def _rope(x, cos, sin):
    return x * cos + pltpu.roll(x, LANE // 2, 1) * sin


def _rope_t(dy, cos, sin):
    return dy * cos + pltpu.roll(dy * sin, LANE // 2, 1)


def _mla_pre(cq, ckv, kpe, cos, sin, qw, kvw):
    return _rms(cq, qw), _rms(ckv, kvw), _rope(kpe, cos, sin)


def _mla_pre_bwd(cq, ckv, dcqn, dckvn, dkr, cos, sin, qw, kvw):
    _, pull = jax.vjp(lambda a, b, c, d: (_rms(a, c), _rms(b, d)), cq, ckv, qw, kvw)
    dcq, dckv, dqw, dkvw = pull((dcqn, dckvn))
    return dcq, dckv, _rope_t(dkr, cos, sin), dqw, dkvw


WEIGHTS = ['meta_tokens', 'ffn1_norm', 'ffn1_w_gate', 'ffn1_w_up', 'ffn1_w_down', 'mix_norm', 'w_in', 'tm_mu', 'w0',
           'w_up', 'a0', 'a_up', 'g_up', 'k_k', 'k_a', 'r_k', 'gn_w', 'gn_b', 'q_norm', 'w_uq', 'kv_norm', 'w_ukv',
           'w_out', 'ffn2_norm', 'ffn2_w_gate', 'ffn2_w_up', 'ffn2_w_down', 'final_norm']
SHARD_AXIS = {'meta_tokens': 1, 'ffn1_w_gate': 1, 'ffn1_w_up': 1, 'ffn1_w_down': 0, 'w_in': 1, 'w_up': 1, 'a_up': 1,
              'g_up': 1, 'w_uq': 1, 'w_ukv': 1, 'w_out': 0, 'ffn2_w_gate': 1, 'ffn2_w_up': 1, 'ffn2_w_down': 0}
GATHERED = [n for n in WEIGHTS if n in SHARD_AXIS and n != 'meta_tokens']
FFN_IN = ('ffn1_w_gate', 'ffn1_w_up', 'ffn2_w_gate', 'ffn2_w_up')
PART_B = ['ffn2_w_gate', 'ffn2_w_up', 'ffn2_w_down', 'w_out']
PART_A1 = ['ffn1_w_gate', 'ffn1_w_up', 'ffn1_w_down']
PART_A2 = [n for n in GATHERED if n not in PART_B and n not in PART_A1]
SMALL = [n for n in WEIGHTS if n not in SHARD_AXIS]


def _to2d(a):
    if a.ndim == 1:
        return a.reshape(1, -1)
    if a.ndim == 3:
        return a.reshape(a.shape[0] * a.shape[1], a.shape[2]) if a.shape[0] == 1 and a.shape[1] > 64 else a.reshape(1, -1)
    return a


def _unpack(flat, shapes):
    out, off = [], 0
    for shp in shapes:
        n = shp[0] * shp[1]
        out.append(flat[off:off + n].reshape(shp))
        off += n
    return out


def _adamw(w, g, m, v):
    m = ADAM_B1 * m + (1.0 - ADAM_B1) * g
    v = ADAM_B2 * v + (1.0 - ADAM_B2) * jnp.square(g)
    m_hat = m / (1.0 - ADAM_B1 ** ADAM_STEP)
    v_hat = v / (1.0 - ADAM_B2 ** ADAM_STEP)
    delta = -ADAM_LR * (m_hat / (jnp.sqrt(v_hat) + ADAM_EPS) + ADAM_WD * w)
    return delta, m, v


def adamw(w, g, m, v, name):
    R, C = w.shape
    if R % 16 == 0 and R > 16:
        tm = _rows_tile(R, C)
    else:
        tm = R
    return rowwise(_adamw, [(w, C, 0), (g, C, 0), (m, C, 0), (v, C, 0)], [], [(C, C, F32)] * 3, tm=tm, name=name)


def _step(a):
    x = a['x']
    Bl, S, D = x.shape
    T = S + N_META
    R, RS = Bl * T, Bl * S
    Hr, Hm = D // RWKV_HEAD, D // LANE
    w2 = {n: _to2d(a[n]) for n in WEIGHTS}
    m2 = {n: _to2d(a['m_' + n]) for n in WEIGHTS}
    v2 = {n: _to2d(a['v_' + n]) for n in WEIGHTS}
    xi, yi, ci = lax.axis_index("x"), lax.axis_index("y"), lax.axis_index("c")
    chip = 2 * xi + yi

    i0 = jnp.zeros((), jnp.int32)
    chip32 = chip.astype(jnp.int32)

    def own_slot(w):
        return lax.dynamic_update_slice(lax.empty((4,) + w.shape, BF16), w.astype(BF16)[None], (chip32, i0, i0))

    stk = dict(zip(PART_A1, ag_weights([own_slot(w2[n]) for n in PART_A1])))

    def unstack(z, axis):
        return z.reshape(4 * z.shape[1], z.shape[2]) if axis == 0 else jnp.concatenate([z[s] for s in range(4)], axis=1)

    full = {n: unstack(stk[n], SHARD_AXIS[n]) for n in PART_A1 if n not in FFN_IN}
    mt = w2['meta_tokens']
    mcols = mt.shape[1]
    mt_z = lax.dynamic_update_slice(jnp.zeros((N_META, D), F32), 0.5 * mt, (jnp.zeros((), jnp.int32), (chip * mcols).astype(jnp.int32)))
    meta_full = allreduce8(mt_z.reshape(-1, LANE), "gather_meta").reshape(N_META, D)

    F = 4 * stk['ffn1_w_gate'].shape[2]
    half = ROPE_DIM // 2
    tmu = w2['tm_mu']
    mu_a = tmu[:, :3 * D]
    zm = lambda n: jnp.zeros((1, n), F32)
    mu_b = jnp.concatenate([tmu[:, 3 * D + 2 * W_LORA:], tmu[:, 3 * D:3 * D + W_LORA], zm(LANE - W_LORA),
                            tmu[:, 3 * D + W_LORA:3 * D + 2 * W_LORA], zm(LANE - A_LORA)], axis=1)
    pos = jnp.arange(T, dtype=F32)
    inv_freq = 1.0 / (ROPE_THETA ** (jnp.arange(0, ROPE_DIM, 2, dtype=F32) / ROPE_DIM))
    ang = pos[:, None] * inv_freq[None, :]
    zt = jnp.zeros((T, half), F32)
    cos_t = jnp.concatenate([jnp.cos(ang), zt, jnp.cos(ang), zt], axis=1)
    sin_t = jnp.concatenate([-jnp.sin(ang), zt, jnp.sin(ang), zt], axis=1)
    cos_q, sin_q = cos_t[N_META:], sin_t[N_META:]

    t_full = _rows_tile(R, D)
    t_512 = _rows_tile(R, 512, T)
    t_128 = _rows_tile(R, LANE, T)
    tq_full = _rows_tile(RS, D)
    tq_128 = _rows_tile(RS, LANE, S)

    def rows3(z):
        return z.reshape(Bl, T, z.shape[-1])

    def real_rows(z):
        return rows3(z)[:, N_META:].reshape(RS, z.shape[-1])

    def pad_meta(z):
        z3 = z.reshape(Bl, S, z.shape[-1])
        return jnp.concatenate([jnp.zeros((Bl, N_META, z.shape[-1]), z.dtype), z3], axis=1).reshape(R, z.shape[-1])

    def shift_down(z):
        z3 = rows3(z)
        return jnp.concatenate([jnp.zeros((Bl, 1, z.shape[-1]), z.dtype), z3[:, :-1]], axis=1).reshape(R, z.shape[-1])

    def shift_up(z):
        z3 = rows3(z)
        return jnp.concatenate([z3[:, 1:], jnp.zeros((Bl, 1, z.shape[-1]), z.dtype)], axis=1).reshape(R, z.shape[-1])

    def ffn_fwd(h, nw, wg, wu, wd, tag, ag_bufs=()):
        n = rowwise(_rms, [(h, D, 0)], [(nw, D, 0)], [(D, D, BF16)], tm=t_full, name=tag + "_norm")[0]
        res = mm_epi(n, [wg, wu], "nn", lambda g_, u_: (g_, u_, _silu_mul(g_, u_)), [], [F32, F32, BF16], b_stack=True,
                     carry=("ag", ag_bufs) if ag_bufs else None, name=tag + "_gate_up")
        (gate, up, act), got = res if ag_bufs else (res, [])
        out = mm(act, wd, res=h, alpha=0.5, name=tag + "_down")
        return (out, (h, n, gate, up, act), got) if ag_bufs else (out, (h, n, gate, up, act))

    def exch(g, name):
        return add_half(g, swap_halves([g], "swap_halves_" + name)[0], cidx, "add_half_" + name)

    def ffn_bwd(dh2, saved, nw, wg, wu, wd, tag, rs_hs=(), rs_own=()):
        h, n, gate, up, act = saved
        dz = rowwise(lambda d: 0.5 * d, [(dh2, 512, 0)], [], [(D, 512, BF16)], tm=t_512, ncb=D // 512, name=tag + "_dz")[0]
        d_wd = mm(act, dz, "tn", name=tag + "_dwd")
        res = mm_epi(dz, [wd], "nt", lambda da, g_, u_: vjp_fn(_silu_mul, 2)(g_, u_, da), [gate, up], [BF16, BF16],
                     carry=("rs", rs_hs) if rs_hs else None, name=tag + "_dact")
        (dgate, dup), got = res if rs_hs else (res, [])
        own = {}
        if rs_own:
            h_wd = exch(stacked(d_wd, 0), rs_own[2])
            d_wg, p_wd = mm(n, dgate, "tn", out_stack=True, carry=("rs", [h_wd]), name=tag + "_dwg")
            d_wu = mm(n, dup, "tn", out_stack=True, name=tag + "_dwu")
            h_wg = exch(d_wg, rs_own[0])
            dn, p_wg = mm(dgate, wg, "nt", b_stack=True, carry=("rs", [h_wg]), name=tag + "_dn1")
            h_wu = exch(d_wu, rs_own[1])
            dn, p_wu = mm(dup, wu, "nt", res=dn, b_stack=True, carry=("rs", [h_wu]), name=tag + "_dn2")
            own = {rs_own[0]: (p_wg[0], h_wg), rs_own[1]: (p_wu[0], h_wu), rs_own[2]: (p_wd[0], h_wd)}
        else:
            d_wg = mm(n, dgate, "tn", out_stack=True, name=tag + "_dwg")
            d_wu = mm(n, dup, "tn", out_stack=True, name=tag + "_dwu")
            dn = mm(dgate, wg, "nt", b_stack=True, name=tag + "_dn1")
            dn = mm(dup, wu, "nt", res=dn, b_stack=True, name=tag + "_dn2")

        def f(h_, dn_, dh_, nw_):
            dh, dnw = vjp_fn(_rms, 2)(h_, nw_, dn_)
            return dh + dh_, dnw

        dh, d_nw = rowwise(f, [(h, D, 0), (dn, D, 0), (dh2, D, 0)], [(nw, D, 0)], [(D, D, F32)], [(1, D, D)],
                           tm=t_full, name=tag + "_dnorm")
        return (dh, d_nw, d_wg, d_wu, d_wd, got, own) if (rs_hs or rs_own) else (dh, d_nw, d_wg, d_wu, d_wd)

    h0 = jnp.concatenate([jnp.broadcast_to(meta_full[None], (Bl, N_META, D)), x], axis=1).reshape(R, D)
    h1, sv1, got_a2 = ffn_fwd(h0, w2['ffn1_norm'], stk['ffn1_w_gate'], stk['ffn1_w_up'], full['ffn1_w_down'], "ffn1",
                              [own_slot(w2[n]) for n in PART_A2])
    stk.update(zip(PART_A2, got_a2))
    full.update({n: unstack(stk[n], SHARD_AXIS[n]) for n in PART_A2})
    win = full['w_in']
    o = 3 * D
    c_xw, c_xa, c_xg = win[:, o:o + W_LORA], win[:, o + W_LORA:o + 2 * W_LORA], win[:, o + 2 * W_LORA:o + 2 * W_LORA + G_LORA]
    o += 2 * W_LORA + G_LORA
    c_cq, c_ckv, c_kpe = win[:, o:o + Q_LORA], win[:, o + Q_LORA:o + Q_LORA + KV_LORA], win[:, o + Q_LORA + KV_LORA:o + Q_LORA + KV_LORA + ROPE_DIM]
    o += Q_LORA + KV_LORA + ROPE_DIM
    c_ga, c_gb = win[:, o:o + D], win[:, o + D:o + 2 * D]
    zc = lambda n: jnp.zeros((D, n), BF16)
    half = ROPE_DIM // 2
    NP0 = 5 * D + 512 + Q_LORA + KV_LORA + LANE
    NP = -(-NP0 // 512) * 512
    win_p = jnp.concatenate([win[:, :3 * D], c_ga, c_gb, c_xg, c_xw, zc(LANE - W_LORA), c_xa, zc(LANE - A_LORA), c_cq, c_ckv,
                             c_kpe[:, :half], zc(half), c_kpe[:, half:], zc(half), zc(NP - NP0)], axis=1)
    O_GA, O_GB, O_L, O_CQ, O_CKV, O_KPE = 3 * D, 4 * D, 5 * D, 5 * D + 512, 5 * D + 512 + Q_LORA, 5 * D + 512 + Q_LORA + KV_LORA
    zr = lambda n: jnp.zeros((n, D), BF16)
    w_up_p = jnp.concatenate([full['w_up'], zr(LANE - W_LORA)], axis=0)
    a_up_p = jnp.concatenate([full['a_up'], zr(LANE - A_LORA)], axis=0)
    g_up = full['g_up']
    wuq = full['w_uq'].reshape(Q_LORA, Hm, QK_DIM)
    zq = jnp.zeros((Q_LORA, Hm, half), BF16)
    wqn = wuq[:, :, :NOPE_DIM].reshape(Q_LORA, Hm * LANE)
    wqp = jnp.concatenate([wuq[:, :, NOPE_DIM:NOPE_DIM + half], zq, wuq[:, :, NOPE_DIM + half:], zq], axis=2).reshape(Q_LORA, Hm * LANE)
    wukv = full['w_ukv'].reshape(KV_LORA, Hm, NOPE_DIM + V_DIM)
    wkn = wukv[:, :, :NOPE_DIM].reshape(KV_LORA, Hm * LANE)
    wv = wukv[:, :, NOPE_DIM:].reshape(KV_LORA, Hm * LANE)
    u = rowwise(_rms, [(h1, D, 0)], [(w2['mix_norm'], D, 0)], [(D, D, BF16)], tm=t_full, name="mix_norm")[0]
    proj = mm(u, win_p, name="proj")
    tpos = jnp.broadcast_to(pos[:, None], (T, LANE))
    tpos_in = (tpos, LANE, 0, T // t_512, True)
    prev_b = shift_down(proj[:, O_L:O_L + 512])
    nb = D // 512
    r_s, k_s, v_s = [
        rowwise(_shift_rows, [(proj, 512, sec * nb), (proj, 512, sec * nb, None, False, "prev"), tpos_in],
                [(mu_a, 512, sec * nb)], [(D, 512, F32)], tm=t_512, ncb=nb, name="shift_" + tag)[0]
        for sec, tag in enumerate("rkv")]
    sg, txw, xas = rowwise(_lora_act, [(proj, 512, O_L // 512), (prev_b, 512, 0)], [(mu_b, 512, 0)],
                           [(G_LORA, G_LORA, BF16), (LANE, LANE, BF16), (LANE, LANE, BF16)], tm=t_512, name="lora_act")
    lw = mm(txw, w_up_p, name="lora_w")
    la = mm(xas, a_up_p, name="lora_a")
    g = mm(sg, g_up, name="lora_g")
    hb = D // LANE
    par_d = lambda n: (w2[n], LANE, 0)
    decay, kn, bb, k2 = rowwise(_prep, [(k_s, LANE, 0), (lw, LANE, 0), (la, LANE, 0)],
                                [par_d('w0'), par_d('a0'), par_d('k_k'), par_d('k_a')], [(D, LANE, F32)] * 4,
                                tm=t_128, ncb=hb, name="wkv_prep")

    NS = Bl * Hr

    def t_major(z):
        return lax.optimization_barrier(z.reshape(Bl, T, D).transpose(1, 0, 2))

    def b_major(z):
        return lax.optimization_barrier(z).transpose(1, 0, 2).reshape(R, D)

    def to_j(z):
        z = t_major(z).reshape(T, NS, RWKV_HEAD).transpose(0, 2, 1)
        return jnp.broadcast_to(z[:, :, None, :], (T, RWKV_HEAD, 2, NS)).reshape(T, RWKV_HEAD, 2 * NS)

    def to_i(z):
        return t_major(z).reshape(T, NS, 2, RWKV_HEAD // 2).transpose(0, 3, 2, 1).reshape(T, RWKV_HEAD // 2, 2 * NS)

    def from_i(z):
        return b_major(z.reshape(T, RWKV_HEAD // 2, 2, NS).transpose(0, 3, 2, 1).reshape(T, Bl, D))

    def from_j(z):
        return b_major(z.transpose(0, 2, 1).reshape(T, Bl, D))

    jw, jkn, jb, jk, jr, iv = to_j(decay), to_j(kn), to_j(bb), to_j(k2), to_j(r_s), to_i(v_s)
    y_i, sp, sa_i, got_b = wkv_fwd(jw, jkn, jb, jk, jr, iv, [own_slot(w2[n]) for n in PART_B])
    stk.update(zip(PART_B, got_b))
    full.update({n: unstack(stk[n], SHARD_AXIS[n]) for n in PART_B if n not in FFN_IN})
    wout = full['w_out']
    y = from_i(y_i)
    post_rows = [(y, LANE, 0), (r_s, LANE, 0), (k2, LANE, 0), (v_s, LANE, 0), (g, LANE, 0)]
    post_pars = [par_d('gn_w'), par_d('gn_b'), par_d('r_k')]
    ya = rowwise(_post, post_rows, post_pars, [(D, LANE, F32)], tm=t_128, ncb=hb, name="wkv_post")[0]

    nt512 = T // t_512
    mla_rows = [(proj, Q_LORA, O_CQ // Q_LORA), (proj, KV_LORA, O_CKV // KV_LORA), (proj, LANE, O_KPE // LANE)]
    tabs = [(cos_t, LANE, 0, nt512, True), (sin_t, LANE, 0, nt512, True)]
    mla_pars = [(w2['q_norm'], Q_LORA, 0), (w2['kv_norm'], KV_LORA, 0)]
    cqn, ckvn, kpr = rowwise(_mla_pre, mla_rows + tabs, mla_pars,
                             [(Q_LORA, Q_LORA, BF16), (KV_LORA, KV_LORA, BF16), (LANE, LANE, BF16)], tm=t_512, name="mla_pre")
    cqn_r = real_rows(cqn)
    qn = mm(cqn_r, wqn, out_dtype=BF16, name="q_nope")
    qp_raw = mm(cqn_r, wqp, name="q_pe")
    ntq = S // tq_128
    qtabs = [(cos_q, LANE, 0, ntq, True), (sin_q, LANE, 0, ntq, True)]
    qp = rowwise(_rope, [(qp_raw, LANE, 0)] + qtabs, [], [(D, LANE, BF16)], tm=tq_128, ncb=Hm, name="q_rope")[0]
    knope = mm(ckvn, wkn, out_dtype=BF16, name="k_nope")
    vv = mm(ckvn, wv, out_dtype=BF16, name="v_proj")

    def pad_keys(z):
        z3 = rows3(z)
        return jnp.concatenate([z3[:, :N_META], jnp.zeros((Bl, Q_BLOCK - N_META, z.shape[-1]), z.dtype), z3[:, N_META:]], axis=1)

    def unpad_keys(z):
        return jnp.concatenate([z[:, :N_META], z[:, Q_BLOCK:]], axis=1).reshape(R, z.shape[-1])

    qn3, qp3 = qn.reshape(Bl, S, D), qp.reshape(Bl, S, D)
    knp, kpp, vp = pad_keys(knope), pad_keys(kpr), pad_keys(vv)
    o3, _ = attn_fwd(qn3, qp3, knp, kpp, vp)
    o_att = pad_meta(o3.reshape(RS, D))
    mix_rows = [(proj, 512, O_GA // 512), (proj, 512, O_GB // 512), (ya, 512, 0), (o_att, 512, 0)]
    mix = rowwise(_gate_mix, mix_rows, [], [(D, 512, BF16)], tm=t_512, ncb=D // 512, name="gate_mix")[0]
    h2 = mm(mix, wout, res=h1, name="w_out")
    h3, sv2 = ffn_fwd(h2, w2['ffn2_norm'], stk['ffn2_w_gate'], stk['ffn2_w_up'], full['ffn2_w_down'], "ffn2")

    def loss_fb(h_, tgt, fw):
        yv, pull = jax.vjp(_rms, h_, fw)
        e = yv - tgt
        dh, dfw = pull(e * (1.0 / D))
        return dh, jnp.full((1, LANE), 0.5 / D * jnp.sum(e * e), F32), dfw

    dh3r, lossp, g_final = rowwise(loss_fb, [(real_rows(h3), D, 0), (a['loss_target'].reshape(RS, D), D, 0)],
                                   [(w2['final_norm'], D, 0)], [(D, D, F32)], [(1, LANE, LANE), (1, D, D)],
                                   tm=tq_full, name="loss")
    dh3 = pad_meta(dh3r)

    def stacked(g, axis):
        if g.ndim == 3:
            return g
        if axis == 0:
            return g.reshape(4, g.shape[0] // 4, g.shape[1])
        return g.reshape(g.shape[0], 4, g.shape[1] // 4).transpose(1, 0, 2)

    cidx = ci.reshape(1).astype(jnp.int32)
    gr = {'final_norm': g_final}
    dh2, gr['ffn2_norm'], gr['ffn2_w_gate'], gr['ffn2_w_up'], gr['ffn2_w_down'] = ffn_bwd(
        dh3, sv2, w2['ffn2_norm'], stk['ffn2_w_gate'], stk['ffn2_w_up'], full['ffn2_w_down'], "ffn2")
    dh2b = rowwise(lambda d: d, [(dh2, 512, 0)], [], [(D, 512, BF16)], tm=t_512, ncb=D // 512, name="dh2_cast")[0]
    gr['w_out'] = mm(mix, dh2b, "tn", name="d_wout")
    dmix = mm(dh2b, wout, "nt", name="d_mix")
    dga, dgb, dya, do = rowwise(vjp_fn(_gate_mix, 4), mix_rows + [(dmix, 512, 0)], [],
                                [(D, 512, BF16), (D, 512, BF16), (D, 512, F32), (D, 512, BF16)], tm=t_512, ncb=D // 512,
                                name="d_gate_mix")
    gs_b = [stacked(gr[n], SHARD_AXIS[n]) for n in PART_B]
    hs_b = [add_half(g_, r_, cidx, "add_half_" + n) for n, g_, r_ in zip(PART_B, gs_b, swap_halves(gs_b, "swap_halves_b"))]
    (dqn, dqp, dknp, dvp, dkpp), ps_b = attn_bwd(qn3, qp3, knp, kpp, vp, real_rows(do).reshape(Bl, S, D), hs_b)
    dqn2 = dqn.reshape(RS, D)
    dqp_raw = rowwise(_rope_t, [(dqp.reshape(RS, D), LANE, 0)] + qtabs, [], [(D, LANE, BF16)], tm=tq_128, ncb=Hm,
                      name="d_q_rope")[0]
    d_wqn = mm(cqn_r, dqn2, "tn", name="d_wqn")
    d_wqp = mm(cqn_r, dqp_raw, "tn", name="d_wqp")
    dcqn = mm(dqn2, wqn, "nt", name="d_cqn1")
    dcqn = pad_meta(mm(dqp_raw, wqp, "nt", res=dcqn, name="d_cqn2"))
    dkn2, dv2, dkp2 = unpad_keys(dknp), unpad_keys(dvp), unpad_keys(dkpp)
    d_wkn = mm(ckvn, dkn2, "tn", name="d_wkn")
    d_wv = mm(ckvn, dv2, "tn", name="d_wv")
    dckvn = mm(dkn2, wkn, "nt", name="d_ckvn1")
    dckvn = mm(dv2, wv, "nt", res=dckvn, name="d_ckvn2")
    dcq, dckv, dkpe, gr['q_norm'], gr['kv_norm'] = rowwise(
        _mla_pre_bwd, mla_rows[:2] + [(dcqn, Q_LORA, 0), (dckvn, KV_LORA, 0), (dkp2, LANE, 0)] + tabs, mla_pars,
        [(Q_LORA, Q_LORA, BF16), (KV_LORA, KV_LORA, BF16), (LANE, LANE, BF16)], [(1, Q_LORA, Q_LORA), (1, KV_LORA, KV_LORA)],
        tm=t_512, name="d_mla_pre")

    def post_bwd(y_, r_, k2_, v_, g_, dya_, gnw, gnb, rk):
        return vjp_fn(_post, 8)(y_, r_, k2_, v_, g_, gnw, gnb, rk, dya_)

    dy, dr_b, dk2_b, dv_b, dg, gr['gn_w'], gr['gn_b'], gr['r_k'] = rowwise(
        post_bwd, post_rows + [(dya, LANE, 0)], post_pars,
        [(D, LANE, F32)] * 4 + [(D, LANE, BF16)], [(1, D, LANE)] * 3, tm=t_128, ncb=hb, name="d_wkv_post")
    jdw, jdkn, jdb, jdk, jdr, idv = wkv_bwd(jw, jkn, jb, jk, jr, iv, to_i(dy), sp, sa_i)
    ddecay, dkn_w, db_w, dk2_w, dr_w, dv_w = from_j(jdw), from_j(jdkn), from_j(jdb), from_j(jdk), from_j(jdr), from_i(idv)

    def prep_bwd(k_, lw_, la_, dd, dkn_, db_, dk2a, dk2b, w0, a0, kkw, kaw):
        return vjp_fn(_prep, 7)(k_, lw_, la_, w0, a0, kkw, kaw, dd, dkn_, db_, dk2a + dk2b)

    dk_s, dlw, dla, gr['w0'], gr['a0'], gr['k_k'], gr['k_a'] = rowwise(
        prep_bwd, [(k_s, LANE, 0), (lw, LANE, 0), (la, LANE, 0), (ddecay, LANE, 0), (dkn_w, LANE, 0), (db_w, LANE, 0),
                   (dk2_w, LANE, 0), (dk2_b, LANE, 0)],
        [par_d('w0'), par_d('a0'), par_d('k_k'), par_d('k_a')], [(D, LANE, F32), (D, LANE, BF16), (D, LANE, BF16)],
        [(1, D, LANE)] * 4, tm=t_128, ncb=hb, name="d_wkv_prep")
    d_wup = mm(txw, dlw, "tn", name="d_wup")
    dtxw = mm(dlw, w_up_p, "nt", name="d_txw")
    d_aup = mm(xas, dla, "tn", name="d_aup")
    dxa = mm(dla, a_up_p, "nt", name="d_xa")
    gr['g_up'] = mm(sg, dg, "tn", name="d_gup")
    dsg = mm(dg, g_up, "nt", name="d_sg")

    def lora_bwd(p_, prev_, dsg_, dt_, dxa_, mu_):
        return vjp_fn(_lora_act, 3)(p_, prev_, mu_, dsg_, dt_, dxa_)

    dpb, dprevb, dmu_b = rowwise(lora_bwd, [(proj, 512, O_L // 512), (prev_b, 512, 0), (dsg, G_LORA, 0), (dtxw, LANE, 0),
                                            (dxa, LANE, 0)], [(mu_b, 512, 0)], [(512, 512, F32)] * 2, [(1, 512, 512)],
                                 tm=t_512, name="d_lora_act")

    def shift_back(sec, cts, tag):
        nb = D // 512

        def f(p_, above, tp, *rest):
            cs, mu_ = rest[:-1], rest[-1]
            ct = cs[0] if len(cts) == 1 else cs[0] + cs[2]
            below = cs[1] if len(cts) == 1 else cs[1] + cs[3]
            return _shift_rows_bwd(float(T - 1), p_, above, tp, mu_, ct, below)

        ct_ins = [z for c in cts for z in ((c, 512, 0), (c, 512, 0, None, False, "next"))]
        return rowwise(f, [(proj, 512, sec * nb), (proj, 512, sec * nb, None, False, "prev"), tpos_in] + ct_ins,
                       [(mu_a, 512, sec * nb)], [(D, 512, BF16)], [(1, D, 512)], tm=t_512, ncb=nb, name="d_shift_" + tag)

    dsec_r, dmu_r = shift_back(0, [dr_w, dr_b], "r")
    dsec_k, dmu_k = shift_back(1, [dk_s], "k")
    dsec_v, dmu_v = shift_back(2, [dv_w, dv_b], "v")

    def add_cast(p_, q_):
        return p_ + q_

    def dsec(dp_, dprev_, tag):
        C = dp_.shape[1]
        return rowwise(add_cast, [(dp_, 512, 0), (shift_up(dprev_), 512, 0)], [], [(C, 512, BF16)], tm=t_512, ncb=C // 512,
                       name="d_sec_" + tag)[0]

    zpad = jnp.zeros((R, NP - NP0), BF16)
    dproj = jnp.concatenate([dsec_r, dsec_k, dsec_v, dga, dgb,
                             dsec(dpb, dprevb, "l"), dcq, dckv, dkpe, zpad], axis=1)
    d_win_p = mm(u, dproj, "tn", name="d_win")
    o1 = 3 * D + 2 * W_LORA + G_LORA
    o2 = o1 + Q_LORA + KV_LORA
    segs = [(0, 3 * D, 0), (3 * D, W_LORA, O_L + G_LORA), (3 * D + W_LORA, A_LORA, O_L + G_LORA + LANE),
            (3 * D + 2 * W_LORA, G_LORA, O_L), (o1, Q_LORA, O_CQ), (o1 + Q_LORA, KV_LORA, O_CKV), (o2, half, O_KPE),
            (o2 + half, half, O_KPE + 2 * half), (o2 + ROPE_DIM, 2 * D, O_GA)]
    cs_in = w2['w_in'].shape[1]

    def shard_cols(s):
        pieces = []
        for a0_, n_, p0_ in segs:
            x0, x1 = max(a0_, s * cs_in), min(a0_ + n_, (s + 1) * cs_in)
            if x0 < x1:
                pieces.append(d_win_p[:, p0_ + x0 - a0_:p0_ + x1 - a0_])
        return jnp.concatenate(pieces, axis=1)

    gr['w_in'] = jnp.stack([shard_cols(s) for s in range(4)])
    hs_win = exch(gr['w_in'], "w_in")
    du, ps_win = mm(dproj, win_p, "nt", carry=("rs", [hs_win]), name="d_u")

    def norm_bwd(h_, dn_, dh_, nw_):
        dh, dnw = vjp_fn(_rms, 2)(h_, nw_, dn_)
        return dh + dh_, dnw

    dh1, gr['mix_norm'] = rowwise(norm_bwd, [(h1, D, 0), (du, D, 0), (dh2, D, 0)], [(w2['mix_norm'], D, 0)],
                                  [(D, D, F32)], [(1, D, D)], tm=t_full, name="d_mix_norm")
    gr['tm_mu'] = jnp.concatenate([dmu_r, dmu_k, dmu_v, dmu_b[:, G_LORA:G_LORA + W_LORA],
                                   dmu_b[:, G_LORA + LANE:G_LORA + LANE + A_LORA], dmu_b[:, :G_LORA]], axis=1)
    gr['w_up'], gr['a_up'] = d_wup[:W_LORA], d_aup[:A_LORA]
    dq3n, dq3p = d_wqn.reshape(Q_LORA, Hm, LANE), d_wqp.reshape(Q_LORA, Hm, LANE)
    gr['w_uq'] = jnp.concatenate([dq3n, dq3p[:, :, :half], dq3p[:, :, 2 * half:3 * half]], axis=2).reshape(Q_LORA, Hm * QK_DIM)
    gr['w_ukv'] = jnp.concatenate([d_wkn.reshape(KV_LORA, Hm, LANE), d_wv.reshape(KV_LORA, Hm, LANE)], axis=2).reshape(
        KV_LORA, Hm * (NOPE_DIM + V_DIM))
    rest_a2 = PART_A2[1:]
    gs_a2 = [stacked(gr[n], SHARD_AXIS[n]) for n in rest_a2]
    hs_a2 = [add_half(g_, r_, cidx, "add_half_" + n) for n, g_, r_ in zip(rest_a2, gs_a2, swap_halves(gs_a2, "swap_halves_a2"))]
    dh0, gr['ffn1_norm'], gr['ffn1_w_gate'], gr['ffn1_w_up'], gr['ffn1_w_down'], ps_a2, own_a1 = ffn_bwd(
        dh1, sv1, w2['ffn1_norm'], stk['ffn1_w_gate'], stk['ffn1_w_up'], full['ffn1_w_down'], "ffn1", hs_a2, PART_A1)
    ps_a2, hs_a2 = ps_win + ps_a2, [hs_win] + hs_a2
    dh0_3 = rows3(dh0)
    grad_x = dh0_3[:, N_META:]
    gr['meta_tokens'] = jnp.sum(dh0_3[:, :N_META], axis=0)

    ps_a1, hs_a1 = [own_a1[n][0] for n in PART_A1], [own_a1[n][1] for n in PART_A1]
    sc = (chip32.reshape(1), cidx)
    ts = [sum_chips(p, h, sc, "sum_chips_" + n)
          for n, p, h in zip(PART_A1 + PART_A2 + PART_B, ps_a1 + ps_a2 + ps_b, hs_a1 + hs_a2 + hs_b)]
    g_shard = {n: z.reshape(w2[n].shape) for n, z in zip(PART_A1 + PART_A2 + PART_B, share_sibling(ts))}
    small = jnp.concatenate([gr[n].reshape(-1) for n in SMALL] + [gr['meta_tokens'].reshape(-1), lossp.reshape(-1)])
    ns = small.shape[0]
    nsp = -(-ns // (8 * LANE)) * 8 * LANE
    small_sum = allreduce8(jnp.pad(small, (0, nsp - ns)).reshape(-1, LANE), "allreduce_small").reshape(-1)
    g_small = dict(zip(SMALL + ['meta_full'], _unpack(small_sum, [w2[n].shape for n in SMALL] + [(N_META, D)])))
    g_shard['meta_tokens'] = lax.dynamic_slice(
        g_small['meta_full'], (jnp.zeros((), jnp.int32), (chip * mcols).astype(jnp.int32)), (N_META, mcols))
    loss = small_sum[ns - LANE]

    grads, deltas, new_m, new_v = [], [], [], []
    for n in WEIGHTS:
        gw = g_shard[n] if n in g_shard else g_small[n]
        d_, m_, v_ = adamw(w2[n], gw, m2[n], v2[n], "adamw_" + n)
        shp = a[n].shape
        grads.append(gw.reshape(shp))
        deltas.append(d_.reshape(shp))
        new_m.append(m_.reshape(shp))
        new_v.append(v_.reshape(shp))
    return (loss, grad_x, *grads, *deltas, *new_m, *new_v)


def kernel(x, meta_tokens, ffn1_norm, ffn1_w_gate, ffn1_w_up, ffn1_w_down, mix_norm, w_in, tm_mu, w0, w_up, a0, a_up, g_up, k_k, k_a, r_k, gn_w, gn_b, q_norm, w_uq, kv_norm, w_ukv, w_out, ffn2_norm, ffn2_w_gate, ffn2_w_up, ffn2_w_down, final_norm, loss_target, m_meta_tokens, m_ffn1_norm, m_ffn1_w_gate, m_ffn1_w_up, m_ffn1_w_down, m_mix_norm, m_w_in, m_tm_mu, m_w0, m_w_up, m_a0, m_a_up, m_g_up, m_k_k, m_k_a, m_r_k, m_gn_w, m_gn_b, m_q_norm, m_w_uq, m_kv_norm, m_w_ukv, m_w_out, m_ffn2_norm, m_ffn2_w_gate, m_ffn2_w_up, m_ffn2_w_down, m_final_norm, v_meta_tokens, v_ffn1_norm, v_ffn1_w_gate, v_ffn1_w_up, v_ffn1_w_down, v_mix_norm, v_w_in, v_tm_mu, v_w0, v_w_up, v_a0, v_a_up, v_g_up, v_k_k, v_k_a, v_r_k, v_gn_w, v_gn_b, v_q_norm, v_w_uq, v_kv_norm, v_w_ukv, v_w_out, v_ffn2_norm, v_ffn2_w_gate, v_ffn2_w_up, v_ffn2_w_down, v_final_norm):
    return _step(dict(locals()))
```

```python
import functools
import math

import jax
import jax.numpy as jnp
import numpy as np
from jax import lax
from jax.experimental import pallas as pl
from jax.experimental.pallas import tpu as pltpu

F32 = jnp.float32
BF16 = jnp.bfloat16
MESH = pl.DeviceIdType.MESH

N_META = 16
NORM_EPS = 1e-6
RWKV_HEAD = 64
GN_EPS = RWKV_HEAD * 1e-5
W_LORA, A_LORA, G_LORA = 96, 96, 256
Q_LORA, KV_LORA = 512, 512
NOPE_DIM, ROPE_DIM, V_DIM = 128, 64, 128
QK_DIM = NOPE_DIM + ROPE_DIM
ROPE_THETA = 10000.0
Q_BLOCK = 128
ADAM_LR, ADAM_B1, ADAM_B2, ADAM_EPS, ADAM_WD, ADAM_STEP = 0.001, 0.9, 0.999, 1e-08, 0.01, 10

LANE = 128
VMEM_LIMIT = 56 * 1024 * 1024


def _pcall(body, **kw):
    return pl.pallas_call(body, **kw)


def _cparams(sem):
    return pltpu.CompilerParams(dimension_semantics=sem, vmem_limit_bytes=VMEM_LIMIT)


MM_LANE_TILE = 1536
MM_ROW_TILE = 1408


def _div_tile(n, cap, unit):
    best = None
    for t in range(unit, min(n, cap) + 1, unit):
        if n % t == 0:
            best = t
    return best if best is not None else n


def _rtile(n, pref=512):
    best = None
    for t in range(16, min(n, pref * 2) + 1, 16):
        if n % t == 0 and (best is None or abs(t - pref) < abs(best - pref)):
            best = t
    return best if best is not None else n


def mm(a, b, mode="nn", out_dtype=F32, res=None, alpha=1.0, b_stack=False, out_stack=False, carry=None, name="mm"):
    kind, carried = carry if carry else (None, ())
    nc = len(carried)
    cs = b.shape[-1] if b_stack else None
    bs = (b.shape[1], 4 * cs) if b_stack else b.shape
    if mode == "nn":
        (M, K), (K2, N) = a.shape, bs
    elif mode == "nt":
        (M, K), (N, K2) = a.shape, bs
    else:
        (K, M), (K2, N) = a.shape, bs
    assert K == K2, (a.shape, b.shape, mode)
    if mode == "tn":
        tm, tk = _div_tile(M, MM_ROW_TILE, LANE), _div_tile(K, 1024, 16)
    else:
        tm = _div_tile(M, MM_ROW_TILE, 16)
        tk = _div_tile(cs if (b_stack and mode == "nt") else K, MM_LANE_TILE, LANE)
    ncol = N // 4 if out_stack else (cs if (b_stack and mode == "nn") else N)
    tn = _div_tile(ncol, MM_LANE_TILE if mode != "nt" else 1024, LANE)
    nk = K // tk
    dims = {"nn": (((1,), (0,)), ((), ())), "nt": (((1,), (1,)), ((), ())), "tn": (((0,), (0,)), ((), ()))}[mode]
    direct = out_dtype == F32

    def body(*refs):
        a_ref, b_ref = refs[:2]
        r_ref = refs[2] if res is not None else None
        nin = 3 if res is not None else 2
        o_ref = refs[nin + nc]
        acc = o_ref if direct else refs[nin + 2 * nc + 1]
        k = pl.program_id(2)
        if nc:
            steps = _ag_steps if kind == "ag" else _rs_steps
            start, finish = steps(refs[nin:nin + nc], refs[nin + nc + 1:nin + 2 * nc + 1],
                                  *refs[nin + 2 * nc + (1 if direct else 2):])
            i, j = pl.program_id(0), pl.program_id(1)
            pl.when((i == 0) & (j == 0) & (k == 0))(start)

        @pl.when(k == 0)
        def _():
            acc[...] = jnp.zeros_like(acc) if res is None else r_ref[...].astype(F32)

        p = lax.dot_general(a_ref[...].astype(BF16), b_ref[...].astype(BF16), dims, preferred_element_type=F32)
        acc[...] += p if alpha == 1.0 else alpha * p

        if not direct:
            @pl.when(k == nk - 1)
            def _():
                o_ref[...] = acc[...].astype(o_ref.dtype)

        if nc:
            pl.when((i == M // tm - 1) & (j == N // tn - 1) & (k == nk - 1))(finish)

    if mode == "tn":
        a_spec = pl.BlockSpec((tk, tm), lambda i, j, k: (k, i))
        b_spec = pl.BlockSpec((tk, tn), lambda i, j, k: (k, j))
    else:
        a_spec = pl.BlockSpec((tm, tk), lambda i, j, k: (i, k))
        if mode == "nn":
            if b_stack:
                nps = cs // tn
                b_spec = pl.BlockSpec((None, tk, tn), lambda i, j, k: (j // nps, k, j % nps))
            else:
                b_spec = pl.BlockSpec((tk, tn), lambda i, j, k: (k, j))
        elif b_stack:
            kps = cs // tk
            b_spec = pl.BlockSpec((None, tn, tk), lambda i, j, k: (k // kps, j, k % kps))
        else:
            b_spec = pl.BlockSpec((tn, tk), lambda i, j, k: (j, k))
    r_spec = pl.BlockSpec((tm, tn), lambda i, j, k: (i, j))
    if out_stack:
        ops = (N // 4) // tn
        o_spec = pl.BlockSpec((None, tm, tn), lambda i, j, k: (j // ops, i, j % ops))
        o_shape = jax.ShapeDtypeStruct((4, M, N // 4), out_dtype)
    else:
        o_spec, o_shape = r_spec, jax.ShapeDtypeStruct((M, N), out_dtype)
    in_specs = [a_spec, b_spec] + ([r_spec] if res is not None else [])
    args = (a, b) + ((res,) if res is not None else ())
    if not nc:
        return _pcall(
            body, name=name, grid=(M // tm, N // tn, nk), in_specs=in_specs, out_specs=o_spec, out_shape=o_shape,
            scratch_shapes=[] if direct else [pltpu.VMEM((tm, tn), F32)],
            compiler_params=_cparams(("parallel", "parallel", "arbitrary")),
        )(*args)
    out = _pcall(
        body, name=name, grid=(M // tm, N // tn, nk), in_specs=in_specs + [_HBM] * nc, out_specs=[o_spec] + [_HBM] * nc,
        out_shape=[o_shape] + [jax.ShapeDtypeStruct(z.shape, z.dtype) for z in carried],
        input_output_aliases={len(args) + q: 1 + q for q in range(nc)} if kind == "ag" else {},
        scratch_shapes=([] if direct else [pltpu.VMEM((tm, tn), F32)]) + (_ag_sems(nc) if kind == "ag" else _rs_sems(nc)),
        compiler_params=_cparams(("arbitrary",) * 3),
    )(*args, *carried)
    return out[0], list(out[1:])


MM_EPI_ROW_TILE = 704


def mm_epi(a, bs, mode, epi, extras, out_dtypes, b_stack=False, carry=None, name="mm_epi"):
    kind, carried = carry if carry else (None, ())
    nc = len(carried)
    b = bs[0]
    cs = b.shape[-1] if b_stack else None
    bshape = (b.shape[1], 4 * cs) if b_stack else b.shape
    (M, K) = a.shape
    (K2, N) = bshape if mode == "nn" else bshape[::-1]
    assert K == K2 and mode in ("nn", "nt"), (a.shape, b.shape, mode)
    tm = _div_tile(M, MM_EPI_ROW_TILE, 16)
    tk = _div_tile(cs if (b_stack and mode == "nt") else K, MM_LANE_TILE, LANE)
    tn = _div_tile(cs if (b_stack and mode == "nn") else N, MM_LANE_TILE, LANE)
    nk, nb, ne, no = K // tk, len(bs), len(extras), len(out_dtypes)
    dims = (((1,), (0,)), ((), ())) if mode == "nn" else (((1,), (1,)), ((), ()))

    def body(*refs):
        a_ref, b_refs, e_refs = refs[0], refs[1:1 + nb], refs[1 + nb:1 + nb + ne]
        base = 1 + nb + ne + nc
        o_refs, accs = refs[base:base + no], refs[base + no + nc:base + no + nc + nb]
        k = pl.program_id(2)
        if nc:
            steps = _ag_steps if kind == "ag" else _rs_steps
            start, finish = steps(refs[base - nc:base], refs[base + no:base + no + nc], *refs[base + no + nc + nb:])
            i, j = pl.program_id(0), pl.program_id(1)
            pl.when((i == 0) & (j == 0) & (k == 0))(start)

        @pl.when(k == 0)
        def _():
            for acc in accs:
                acc[...] = jnp.zeros_like(acc)

        av = a_ref[...].astype(BF16)
        for b_ref, acc in zip(b_refs, accs):
            acc[...] += lax.dot_general(av, b_ref[...].astype(BF16), dims, preferred_element_type=F32)

        @pl.when(k == nk - 1)
        def _():
            res = epi(*[acc[...] for acc in accs], *[e[...] for e in e_refs])
            for o_ref, v in zip(o_refs, res):
                o_ref[...] = v.astype(o_ref.dtype)

        if nc:
            pl.when((i == M // tm - 1) & (j == N // tn - 1) & (k == nk - 1))(finish)

    a_spec = pl.BlockSpec((tm, tk), lambda i, j, k: (i, k))
    if mode == "nn":
        if b_stack:
            nps = cs // tn
            b_spec = pl.BlockSpec((None, tk, tn), lambda i, j, k: (j // nps, k, j % nps))
        else:
            b_spec = pl.BlockSpec((tk, tn), lambda i, j, k: (k, j))
    elif b_stack:
        kps = cs // tk
        b_spec = pl.BlockSpec((None, tn, tk), lambda i, j, k: (k // kps, j, k % kps))
    else:
        b_spec = pl.BlockSpec((tn, tk), lambda i, j, k: (j, k))
    o_spec = pl.BlockSpec((tm, tn), lambda i, j, k: (i, j))
    sems = (_ag_sems(nc) if kind == "ag" else _rs_sems(nc)) if nc else []
    res = _pcall(
        body, name=name, grid=(M // tm, N // tn, nk), in_specs=[a_spec] + [b_spec] * nb + [o_spec] * ne + [_HBM] * nc,
        out_specs=[o_spec] * no + [_HBM] * nc,
        out_shape=[jax.ShapeDtypeStruct((M, N), dt) for dt in out_dtypes]
        + [jax.ShapeDtypeStruct(z.shape, z.dtype) for z in carried],
        input_output_aliases={1 + nb + ne + q: no + q for q in range(nc)} if kind == "ag" else {},
        scratch_shapes=[pltpu.VMEM((tm, tn), F32)] * nb + sems,
        compiler_params=_cparams(("arbitrary",) * 3 if nc else ("parallel", "parallel", "arbitrary")),
    )(a, *bs, *extras, *carried)
    return (list(res[:no]), list(res[no:])) if nc else res


def rowwise(fn, row_ins, par_ins, outs, accs=(), *, tm, ncb=1, name="rowwise"):
    R = row_ins[0][0].shape[0]
    assert R % tm == 0, (R, tm)
    nrb = R // tm
    in_specs, args = [], []
    for spec in row_ins:
        arr, w, base = spec[:3]
        mod = spec[3] if len(spec) > 3 else None
        cstep = 0 if (len(spec) > 4 and spec[4]) else 1
        halo = spec[5] if len(spec) > 5 else None
        if halo == "prev":
            in_specs.append(pl.BlockSpec((8, w), lambda j, i, base=base: (jnp.maximum(i * (tm // 8) - 1, 0), base + j)))
        elif halo == "next":
            in_specs.append(pl.BlockSpec((8, w), lambda j, i, base=base: (jnp.minimum((i + 1) * (tm // 8), R // 8 - 1), base + j)))
        elif mod is None:
            in_specs.append(pl.BlockSpec((tm, w), lambda j, i, base=base, cstep=cstep: (i, base + cstep * j)))
        else:
            in_specs.append(pl.BlockSpec((tm, w), lambda j, i, base=base, mod=mod, cstep=cstep: (i % mod, base + cstep * j)))
        args.append(arr)
    for arr, w, base in par_ins:
        in_specs.append(pl.BlockSpec((arr.shape[0], w), lambda j, i, base=base: (0, base + j)))
        args.append(arr)
    out_specs, out_shape = [], []
    for cols, w, dt in outs:
        out_specs.append(pl.BlockSpec((tm, w), lambda j, i: (i, j)))
        out_shape.append(jax.ShapeDtypeStruct((R, cols), dt))
    for p, cols, w in accs:
        out_specs.append(pl.BlockSpec((p, w), lambda j, i: (0, j)))
        out_shape.append(jax.ShapeDtypeStruct((p, cols), F32))
    nin, nout, nacc = len(args), len(outs), len(accs)

    def body(*refs):
        vals = [r[...] for r in refs[:nin]]
        res = fn(*vals)
        if not isinstance(res, (tuple, list)):
            res = (res,)
        assert len(res) == nout + nacc, (len(res), nout, nacc)
        for o_ref, v in zip(refs[nin:nin + nout], res[:nout]):
            o_ref[...] = v.astype(o_ref.dtype)
        if nacc:
            i = pl.program_id(1)

            @pl.when(i == 0)
            def _():
                for a_ref in refs[nin + nout:]:
                    a_ref[...] = jnp.zeros_like(a_ref)

            for a_ref, v in zip(refs[nin + nout:], res[nout:]):
                a_ref[...] += v.astype(F32)

    r = _pcall(
        body, name=name, grid=(ncb, nrb), in_specs=in_specs, out_specs=out_specs, out_shape=out_shape,
        compiler_params=_cparams(("parallel", "arbitrary")),
    )(*args)
    return r


def vjp_fn(fwd, nprim):
    def f(*vals):
        prim, cts = vals[:nprim], vals[nprim:]
        out, pull = jax.vjp(fwd, *[p.astype(F32) for p in prim])
        if not isinstance(out, (tuple, list)):
            cts = cts[0].astype(F32)
        else:
            cts = tuple(c.astype(F32) for c in cts)
        return pull(cts)
    return f


WKV_TB = 16
WKV_RING = 8


def wkv_fwd(w, kn, b, k, r, v, bufs=()):
    T, N, L = w.shape
    NI = v.shape[1]
    tb = WKV_TB
    n = len(bufs)
    assert T % tb == 0 and L == v.shape[2]

    def body(*refs):
        w_ref, kn_ref, b_ref, k_ref, r_ref, v_ref = refs[:6]
        y_ref, sp_ref, sa_ref = refs[6 + n:9 + n]
        s_ref, ring, wsem = refs[9 + 2 * n:12 + 2 * n]
        if n:
            start, finish = _ag_steps(refs[6:6 + n], refs[9 + n:9 + 2 * n], *refs[12 + 2 * n:])
            pl.when(pl.program_id(0) == 0)(start)

        @pl.when(pl.program_id(0) == 0)
        def _():
            s_ref[...] = jnp.zeros_like(s_ref)

        def step(s, carry):
            t = pl.program_id(0) * tb + s
            slot = t % WKV_RING
            out = pltpu.make_async_copy(ring.at[slot], sp_ref.at[t], wsem.at[slot])

            @pl.when(t >= WKV_RING)
            def _():
                out.wait()

            W, KN, B, Kk, Rr = w_ref[s], kn_ref[s], b_ref[s], k_ref[s], r_ref[s]
            for i in range(NI):
                S = s_ref[i]
                ring[slot, i] = S
                sa = jnp.sum(S * KN, axis=0, keepdims=True)
                sa_ref[s, pl.ds(i, 1), :] = sa
                vi = v_ref[s, pl.ds(i, 1), :]
                Sn = S * W + sa * B + vi * Kk
                s_ref[i] = Sn
                y_ref[s, pl.ds(i, 1), :] = jnp.sum(Sn * Rr, axis=0, keepdims=True)
            out.start()
            return carry

        lax.fori_loop(0, tb, step, 0)

        @pl.when(pl.program_id(0) == T // tb - 1)
        def _():
            for q in range(WKV_RING):
                pltpu.make_async_copy(ring.at[q], sp_ref.at[q], wsem.at[q]).wait()

        if n:
            pl.when(pl.program_id(0) == T // tb - 1)(finish)

    jspec = pl.BlockSpec((tb, N, L), lambda t: (t, 0, 0))
    ispec = pl.BlockSpec((tb, NI, L), lambda t: (t, 0, 0))
    ishape = jax.ShapeDtypeStruct((T, NI, L), F32)
    res = _pcall(
        body, name="wkv_fwd", grid=(T // tb,), in_specs=[jspec] * 5 + [ispec] + [_HBM] * n,
        out_specs=[ispec, _HBM, ispec] + [_HBM] * n,
        out_shape=[ishape, jax.ShapeDtypeStruct((T, NI, N, L), F32), ishape]
        + [jax.ShapeDtypeStruct(z.shape, z.dtype) for z in bufs],
        input_output_aliases={6 + i: 3 + i for i in range(n)},
        scratch_shapes=[pltpu.VMEM((NI, N, L), F32), pltpu.VMEM((WKV_RING, NI, N, L), F32), pltpu.SemaphoreType.DMA((WKV_RING,))]
        + (_ag_sems(n) if n else []),
        compiler_params=_cparams(("arbitrary",)),
    )(w, kn, b, k, r, v, *bufs)
    return res[0], res[1], res[2], list(res[3:])


def wkv_bwd(w, kn, b, k, r, v, dy, sp, sa):
    T, N, L = w.shape
    NI, LH = v.shape[1], L // 2
    tb = WKV_TB
    nt = T // tb

    def body(w_ref, kn_ref, b_ref, k_ref, r_ref, v_ref, dy_ref, sp_ref, sa_ref,
             dw_ref, dkn_ref, db_ref, dk_ref, dr_ref, dv_ref, ds_ref):
        @pl.when(pl.program_id(0) == 0)
        def _():
            ds_ref[...] = jnp.zeros_like(ds_ref)

        def step(q, carry):
            s = tb - 1 - q
            W, KN, B, Kk, Rr = w_ref[s], kn_ref[s], b_ref[s], k_ref[s], r_ref[s]
            dW = jnp.zeros((N, L), F32)
            dKN, dB, dK, T1 = dW, dW, dW, dW
            al = jnp.zeros((1, L), F32)
            be = al
            for i in range(NI):
                Sp = sp_ref[s, i]
                vi = v_ref[s, pl.ds(i, 1), :]
                dyi = dy_ref[s, pl.ds(i, 1), :]
                sai = sa_ref[s, pl.ds(i, 1), :]
                dS = ds_ref[i] + dyi * Rr
                T1 = T1 + Sp * dyi
                al = al + sai * dyi
                be = be + vi * dyi
                dv_ref[s, pl.ds(i, 1), :] = jnp.sum(dS * Kk, axis=0, keepdims=True)
                dK = dK + dS * vi
                dsa = jnp.sum(dS * B, axis=0, keepdims=True)
                dB = dB + dS * sai
                dW = dW + dS * Sp
                dKN = dKN + Sp * dsa
                ds_ref[i] = dS * W + dsa * KN
            dR = W * T1 + B * al + Kk * be
            for ref, val in ((dw_ref, dW), (dkn_ref, dKN), (db_ref, dB), (dk_ref, dK), (dr_ref, dR)):
                ref[s] = (val + pltpu.roll(val, LH, 1))[:, :LH]
            return carry

        lax.fori_loop(0, tb, step, 0)

    jspec = pl.BlockSpec((tb, N, L), lambda t: (nt - 1 - t, 0, 0))
    gspec = pl.BlockSpec((tb, N, LH), lambda t: (nt - 1 - t, 0, 0))
    ispec = pl.BlockSpec((tb, NI, L), lambda t: (nt - 1 - t, 0, 0))
    gshape = jax.ShapeDtypeStruct((T, N, LH), F32)
    return _pcall(
        body, name="wkv_bwd", grid=(nt,),
        in_specs=[jspec] * 5 + [ispec, ispec, pl.BlockSpec((tb, NI, N, L), lambda t: (nt - 1 - t, 0, 0, 0)), ispec],
        out_specs=[gspec] * 5 + [ispec], out_shape=[gshape] * 5 + [jax.ShapeDtypeStruct((T, NI, L), F32)],
        scratch_shapes=[pltpu.VMEM((NI, N, L), F32)],
        compiler_params=_cparams(("arbitrary",)),
    )(w, kn, b, k, r, v, dy, sp, sa)


_NT = (((1,), (1,)), ((), ()))
_TN = (((0,), (0,)), ((), ()))
ATT_SCALE = QK_DIM ** -0.5


def _att_bias():
    col = lax.broadcasted_iota(jnp.int32, (Q_BLOCK, Q_BLOCK), 1)
    row = lax.broadcasted_iota(jnp.int32, (Q_BLOCK, Q_BLOCK), 0)
    return jnp.where(col < N_META, 0.0, -1e30).astype(F32), jnp.where(col <= row, 0.0, -1e30).astype(F32)


def _att_scores(q1, q2, kn_ref, kp_ref, s_ref, bias, L):
    s = lax.dot_general(q1, kn_ref[0, :L, :], _NT, preferred_element_type=F32)
    s = s + lax.dot_general(q2, kp_ref[0, :L, :], _NT, preferred_element_type=F32)
    s_ref[:, :L] = s * ATT_SCALE
    s_ref[:, :Q_BLOCK] += bias[0]
    s_ref[:, L - Q_BLOCK:L] += bias[1]


def _att_probs(s_ref, L):
    s = s_ref[:, :L]
    m = jnp.max(s, axis=-1, keepdims=True)
    p = jnp.exp(s - m)
    return p / jnp.sum(p, axis=-1, keepdims=True)


def _att_blocks(nq, qn_ref, qp_ref, kn_ref, kp_ref, s_ref):
    bias = _att_bias()
    rows = lambda i: pl.ds(Q_BLOCK * i, Q_BLOCK)
    L = lambda i: Q_BLOCK * (i + 2)
    score = lambda i: _att_scores(qn_ref[0, rows(i), :], qp_ref[0, rows(i), :], kn_ref, kp_ref, s_ref.at[i % 2], bias, L(i))
    score(0)
    for i in range(nq):
        if i + 1 < nq:
            score(i + 1)
        yield i, rows(i), L(i), _att_probs(s_ref.at[i % 2], L(i))


def _edge_steps(nb, nh):
    b, h = pl.program_id(0), pl.program_id(1)
    return (b == 0) & (h == 0), (b == nb - 1) & (h == nh - 1)


def attn_fwd(qn, qp, kn, kp, v, bufs=()):
    B, S, HD = qn.shape
    SP = kn.shape[1]
    H = HD // LANE
    nq = S // Q_BLOCK
    n = len(bufs)

    def body(*refs):
        qn_ref, qp_ref, kn_ref, kp_ref, v_ref = refs[:5]
        o_ref = refs[5 + n]
        s_ref = refs[6 + 2 * n]
        if n:
            start, finish = _ag_steps(refs[5:5 + n], refs[6 + n:6 + 2 * n], *refs[7 + 2 * n:])
            is_first, is_last = _edge_steps(B, H)
            pl.when(is_first)(start)
        for i, rows, L, p in _att_blocks(nq, qn_ref, qp_ref, kn_ref, kp_ref, s_ref):
            o_ref[0, rows, :] = jnp.dot(p.astype(BF16), v_ref[0, :L, :], preferred_element_type=F32)
        if n:
            pl.when(is_last)(finish)

    qspec = pl.BlockSpec((1, S, LANE), lambda b, h: (b, 0, h))
    kspec = pl.BlockSpec((1, SP, LANE), lambda b, h: (b, 0, h))
    pspec = pl.BlockSpec((1, SP, LANE), lambda b, h: (b, 0, 0))
    res = _pcall(
        body, name="attn_fwd", grid=(B, H), in_specs=[qspec, qspec, kspec, pspec, kspec] + [_HBM] * n,
        out_specs=[qspec] + [_HBM] * n,
        out_shape=[jax.ShapeDtypeStruct((B, S, HD), F32)] + [jax.ShapeDtypeStruct(w.shape, w.dtype) for w in bufs],
        input_output_aliases={5 + i: 1 + i for i in range(n)},
        scratch_shapes=[pltpu.VMEM((2, Q_BLOCK, SP), F32)] + (_ag_sems(n) if n else []),
        compiler_params=_cparams(("arbitrary", "arbitrary")),
    )(qn, qp, kn, kp, v, *bufs)
    return res[0], list(res[1:])


def attn_bwd(qn, qp, kn, kp, v, do, hs=()):
    B, S, HD = qn.shape
    SP = kn.shape[1]
    H = HD // LANE
    nq = S // Q_BLOCK
    n = len(hs)

    def body(*refs):
        qn_ref, qp_ref, kn_ref, kp_ref, v_ref, do_ref = refs[:6]
        dqn_ref, dqp_ref, dkn_ref, dv_ref, dkp_ref = refs[6 + n:11 + n]
        dkn_acc, dv_acc, s_ref = refs[11 + 2 * n:14 + 2 * n]
        if n:
            start, finish = _rs_steps(refs[6:6 + n], refs[11 + n:11 + 2 * n], *refs[14 + 2 * n:])
            is_first, is_last = _edge_steps(B, H)
            pl.when(is_first)(start)

        @pl.when(pl.program_id(1) == 0)
        def _():
            dkp_ref[...] = jnp.zeros_like(dkp_ref)

        dkn_acc[...] = jnp.zeros_like(dkn_acc)
        dv_acc[...] = jnp.zeros_like(dv_acc)
        for i, rows, L, p in _att_blocks(nq, qn_ref, qp_ref, kn_ref, kp_ref, s_ref):
            q1, q2, do_i = qn_ref[0, rows, :], qp_ref[0, rows, :], do_ref[0, rows, :]
            dp = lax.dot_general(do_i, v_ref[0, :L, :], _NT, preferred_element_type=F32)
            ds = (p * (dp - jnp.sum(p * dp, axis=-1, keepdims=True)) * ATT_SCALE).astype(BF16)
            dqn_ref[0, rows, :] = jnp.dot(ds, kn_ref[0, :L, :], preferred_element_type=F32).astype(dqn_ref.dtype)
            dqp_ref[0, rows, :] = jnp.dot(ds, kp_ref[0, :L, :], preferred_element_type=F32)
            dkn_acc[:L, :] += lax.dot_general(ds, q1, _TN, preferred_element_type=F32)
            dkp_ref[0, :L, :] += lax.dot_general(ds, q2, _TN, preferred_element_type=F32)
            dv_acc[:L, :] += lax.dot_general(p.astype(BF16), do_i, _TN, preferred_element_type=F32)
        dkn_ref[0] = dkn_acc[...].astype(dkn_ref.dtype)
        dv_ref[0] = dv_acc[...].astype(dv_ref.dtype)
        if n:
            pl.when(is_last)(finish)

    qspec = pl.BlockSpec((1, S, LANE), lambda b, h: (b, 0, h))
    kspec = pl.BlockSpec((1, SP, LANE), lambda b, h: (b, 0, h))
    pspec = pl.BlockSpec((1, SP, LANE), lambda b, h: (b, 0, 0))
    res = _pcall(
        body, name="attn_bwd", grid=(B, H), in_specs=[qspec, qspec, kspec, pspec, kspec, qspec] + [_HBM] * n,
        out_specs=[qspec, qspec, kspec, kspec, pspec] + [_HBM] * n,
        out_shape=[jax.ShapeDtypeStruct((B, S, HD), BF16), jax.ShapeDtypeStruct((B, S, HD), F32),
                   jax.ShapeDtypeStruct((B, SP, HD), BF16), jax.ShapeDtypeStruct((B, SP, HD), BF16),
                   jax.ShapeDtypeStruct((B, SP, LANE), F32)] + [jax.ShapeDtypeStruct(h.shape, h.dtype) for h in hs],
        scratch_shapes=[pltpu.VMEM((SP, LANE), F32), pltpu.VMEM((SP, LANE), F32), pltpu.VMEM((2, Q_BLOCK, SP), F32)]
        + (_rs_sems(n) if n else []),
        compiler_params=_cparams(("arbitrary", "arbitrary")),
    )(qn, qp, kn, kp, v, do, *hs)
    return res[:5], list(res[5:])


_HBM = pl.BlockSpec(memory_space=pltpu.HBM)


def _place():
    x, y, c = lax.axis_index("x"), lax.axis_index("y"), lax.axis_index("c")
    chips = [(1 - x, y), (x, 1 - y), (1 - x, 1 - y)]
    return x, y, c, chips


def _rcopy(src, dst, ssem, rsem, dev):
    return pltpu.make_async_remote_copy(src_ref=src, dst_ref=dst, send_sem=ssem, recv_sem=rsem,
                                        device_id=dev, device_id_type=MESH)


def _half_rows(c, r):
    return pl.ds(pl.multiple_of(c * (r // 2), 16), r // 2)


def _ag_steps(w, out, ssem, rsem):
    n = len(w)
    x, y, c, chips = _place()
    s = 2 * x + y
    rows = [_half_rows(c, w[i].shape[1]) for i in range(n)]
    orows = [_half_rows(1 - c, w[i].shape[1]) for i in range(n)]
    first = [_rcopy(w[i].at[s, rows[i]], out[i].at[s, rows[i]], ssem.at[6 * i + j], rsem.at[6 * i + j], (px, py, c))
             for i in range(n) for j, (px, py) in enumerate(chips)]

    def start():
        for cp in first:
            cp.start()

    def finish():
        passed = []
        for j, (px, py) in enumerate(chips):
            sp = 2 * px + py
            for i in range(n):
                here = out[i].at[sp, rows[i]]
                _rcopy(here, here, ssem.at[6 * i + j], rsem.at[6 * i + j], (px, py, c)).wait_recv()
                fw = _rcopy(here, here, ssem.at[6 * i + 3 + j], rsem.at[6 * i + 3 + j], (x, y, 1 - c))
                fw.start()
                passed.append(fw)
        for j, (px, py) in enumerate(chips):
            sp = 2 * px + py
            for i in range(n):
                there = out[i].at[sp, orows[i]]
                _rcopy(there, there, ssem.at[6 * i + 3 + j], rsem.at[6 * i + 3 + j], (x, y, 1 - c)).wait_recv()
        for cp in first + passed:
            cp.wait_send()

    return start, finish


def _ag_sems(n):
    return [pltpu.SemaphoreType.DMA((6 * n,)), pltpu.SemaphoreType.DMA((6 * n,))]


def ag_weights(bufs):
    n = len(bufs)

    def body(*refs):
        start, finish = _ag_steps(refs[:n], refs[n:2 * n], *refs[2 * n:])
        start()
        finish()

    return _pcall(
        body, name="ag_weights", in_specs=[_HBM] * n, out_specs=[_HBM] * n,
        out_shape=[jax.ShapeDtypeStruct(w.shape, w.dtype) for w in bufs],
        input_output_aliases={i: i for i in range(n)}, scratch_shapes=_ag_sems(n),
    )(*bufs)


def swap_halves(gs, name):
    n = len(gs)

    def body(*refs):
        g, out = refs[:n], refs[n:2 * n]
        ssem, rsem = refs[2 * n:]
        x, y, c, _ = _place()
        cps = [_rcopy(g[i].at[:, _half_rows(1 - c, g[i].shape[1])], out[i], ssem.at[i], rsem.at[i], (x, y, 1 - c))
               for i in range(n)]
        for cp in cps:
            cp.start()
        for cp in cps:
            cp.wait()

    return _pcall(
        body, name=name, in_specs=[_HBM] * n, out_specs=[_HBM] * n,
        out_shape=[jax.ShapeDtypeStruct((4, g.shape[1] // 2, g.shape[2]), g.dtype) for g in gs],
        scratch_shapes=[pltpu.SemaphoreType.DMA((n,)), pltpu.SemaphoreType.DMA((n,))],
    )(*gs)


def _rs_steps(h, out, ssem, rsem):
    n = len(h)
    x, y, c, chips = _place()
    s = 2 * x + y
    cps = [_rcopy(h[i].at[2 * px + py], out[i].at[s], ssem.at[3 * i + j], rsem.at[3 * i + j], (px, py, c))
           for i in range(n) for j, (px, py) in enumerate(chips)]

    def start():
        for cp in cps:
            cp.start()

    def finish():
        for j, (px, py) in enumerate(chips):
            for i in range(n):
                _rcopy(h[i].at[s], out[i].at[2 * px + py], ssem.at[3 * i + j], rsem.at[3 * i + j], (px, py, c)).wait_recv()
        for cp in cps:
            cp.wait_send()

    return start, finish


def _rs_sems(n):
    return [pltpu.SemaphoreType.DMA((3 * n,)), pltpu.SemaphoreType.DMA((3 * n,))]


def share_sibling(ts):
    n = len(ts)

    def body(*refs):
        t, out = refs[:n], refs[n:2 * n]
        ssem, rsem = refs[2 * n:]
        x, y, c, _ = _place()
        cps = [_rcopy(t[i].at[c], out[i].at[c], ssem.at[i], rsem.at[i], (x, y, 1 - c)) for i in range(n)]
        for cp in cps:
            cp.start()
        for i in range(n):
            _rcopy(t[i].at[c], out[i].at[1 - c], ssem.at[i], rsem.at[i], (x, y, 1 - c)).wait_recv()
        for cp in cps:
            cp.wait_send()

    return _pcall(
        body, name="share_sibling", in_specs=[_HBM] * n, out_specs=[_HBM] * n,
        out_shape=[jax.ShapeDtypeStruct(t.shape, t.dtype) for t in ts],
        input_output_aliases={i: i for i in range(n)},
        scratch_shapes=[pltpu.SemaphoreType.DMA((n,)), pltpu.SemaphoreType.DMA((n,))],
    )(*ts)


def allreduce8(v, name):
    P, W = v.shape

    def body(v_ref, out_ref, buf, ssem, rsem):
        x, y, c, _ = _place()
        me = 4 * x + 2 * y + c
        buf[me] = v_ref[...]
        cps = []
        for k in range(1, 8):
            px = 1 - x if k & 4 else x
            py = 1 - y if k & 2 else y
            pc = 1 - c if k & 1 else c
            cp = _rcopy(buf.at[me], buf.at[me], ssem.at[k - 1], rsem.at[k - 1], (px, py, pc))
            cp.start()
            cps.append((cp, 4 * px + 2 * py + pc))
        for k, (cp, peer) in enumerate(cps):
            _rcopy(buf.at[me], buf.at[peer], ssem.at[k], rsem.at[k], (x, y, c)).wait_recv()
        for cp, _ in cps:
            cp.wait_send()
        acc = buf[0]
        for d in range(1, 8):
            acc = acc + buf[d]
        out_ref[...] = acc

    return _pcall(
        body, name=name, in_specs=[pl.BlockSpec(memory_space=pltpu.VMEM)],
        out_specs=pl.BlockSpec(memory_space=pltpu.VMEM), out_shape=jax.ShapeDtypeStruct((P, W), F32),
        scratch_shapes=[pltpu.VMEM((8, P, W), F32), pltpu.SemaphoreType.DMA((7,)), pltpu.SemaphoreType.DMA((7,))],
    )(v)


def add_half(g, rcv, cidx, name):
    _, hr, W = rcv.shape
    tr = _rows_tile(hr, W)
    nb = hr // tr

    def body(c_ref, g_ref, r_ref, o_ref):
        o_ref[...] = (g_ref[...] + r_ref[...]).astype(o_ref.dtype)

    return _pcall(
        body, name=name,
        grid_spec=pltpu.PrefetchScalarGridSpec(
            num_scalar_prefetch=1, grid=(4, nb),
            in_specs=[pl.BlockSpec((1, tr, W), lambda s, i, c: (s, c[0] * nb + i, 0)),
                      pl.BlockSpec((1, tr, W), lambda s, i, c: (s, i, 0))],
            out_specs=pl.BlockSpec((1, tr, W), lambda s, i, c: (s, i, 0))),
        out_shape=jax.ShapeDtypeStruct((4, hr, W), BF16), compiler_params=_cparams(("parallel", "parallel")),
    )(cidx, g, rcv)


def sum_chips(p, h, sc, name):
    _, hr, W = p.shape
    tr = _rows_tile(hr, W)

    def body(s_ref, c_ref, p_ref, h_ref, o_ref):
        s = s_ref[0]
        own = h_ref[0]
        f = lambda k: jnp.where(s == k, own, p_ref[k]).astype(F32)
        o_ref[0] = ((f(0) + f(1)) + f(2)) + f(3)

    return _pcall(
        body, name=name,
        grid_spec=pltpu.PrefetchScalarGridSpec(
            num_scalar_prefetch=2, grid=(hr // tr,),
            in_specs=[pl.BlockSpec((4, tr, W), lambda i, s, c: (0, i, 0)),
                      pl.BlockSpec((1, tr, W), lambda i, s, c: (s[0], i, 0))],
            out_specs=pl.BlockSpec((1, tr, W), lambda i, s, c: (c[0], i, 0))),
        out_shape=jax.ShapeDtypeStruct((2, hr, W), F32), compiler_params=_cparams(("parallel",)),
    )(sc[0], sc[1], p, h)


HI = lax.Precision.HIGHEST
BLOCK_BYTES = 3 * 512 * 1024


def _rows_tile(R, w, T=None):
    cands = [t for t in range(16, R + 1, 16) if R % t == 0 and (T is None or T % t == 0)]
    ok = [t for t in cands if t * w * 4 <= BLOCK_BYTES]
    return max(ok) if ok else min(cands)


def _gsum_exact(x):
    r = jnp.right_shift(lax.broadcasted_iota(jnp.int32, (LANE, LANE), 0), 6)
    c = jnp.right_shift(lax.broadcasted_iota(jnp.int32, (LANE, LANE), 1), 6)
    g = (r == c).astype(BF16)
    x1 = x.astype(BF16)
    r1 = x - x1.astype(F32)
    x2 = r1.astype(BF16)
    x3 = (r1 - x2.astype(F32)).astype(BF16)
    dot = lambda z: jnp.dot(z, g, preferred_element_type=F32)
    return (dot(x3) + dot(x2)) + dot(x1)


@jax.custom_vjp
def _gsum(x):
    return _gsum_exact(x)


_gsum.defvjp(lambda x: (_gsum_exact(x), None), lambda _, ct: (_gsum_exact(ct),))


def _rms(x, g):
    return x * lax.rsqrt(jnp.mean(x * x, axis=-1, keepdims=True) + NORM_EPS) * g


def _silu_mul(gate, up):
    return jax.nn.silu(gate) * up


def _shift(p, prev, mu):
    return p + mu * (prev - p)


def _prev_rows(p, above, tpos):
    first = lax.broadcasted_iota(jnp.int32, p.shape, 0) == 0
    prev = jnp.where(first, above[7:8, :], pltpu.roll(p, 1, 0))
    return jnp.where(tpos[:, :1] == 0.0, 0.0, prev)


def _shift_rows(p, above, tpos, mu):
    return _shift(p, _prev_rows(p, above, tpos), mu)


def _shift_rows_bwd(t_last, p, above, tpos, mu, ct, ct_below):
    dmu = jnp.sum(ct * (_prev_rows(p, above, tpos) - p), axis=0, keepdims=True)
    last = lax.broadcasted_iota(jnp.int32, p.shape, 0) == p.shape[0] - 1
    nxt = jnp.where(last, ct_below[0:1, :], pltpu.roll(ct, p.shape[0] - 1, 0))
    nxt = jnp.where(tpos[:, :1] == t_last, 0.0, nxt)
    return (1.0 - mu) * ct + mu * nxt, dmu


def _lora_act(p, prev, mu):
    s = _shift(p, prev, mu)
    return jax.nn.sigmoid(s[:, :G_LORA]), jnp.tanh(s[:, G_LORA:G_LORA + LANE]), s[:, G_LORA + LANE:]


def _prep(k, lw, la, w0, a0, kk_w, ka_w):
    wpre = -jax.nn.softplus(-(w0 + lw)) - 0.5
    decay = jnp.exp(-jnp.exp(wpre))
    a = jax.nn.sigmoid(a0 + la)
    kk = k * kk_w
    kk = kk * lax.rsqrt(jnp.maximum(_gsum(kk * kk), 1e-24))
    k2 = k * (1.0 + (a - 1.0) * ka_w)
    return decay, -kk, kk * a, k2


def _post(y, r, k2, v, g, gnw, gnb, rk):
    mean = _gsum(y) * (1.0 / RWKV_HEAD)
    d = y - mean
    var = _gsum(d * d) * (1.0 / RWKV_HEAD)
    yn = d * lax.rsqrt(var + GN_EPS) * gnw + gnb
    bonus = _gsum(r * k2 * rk) * v
    return (yn + bonus) * g


def _gate_mix(ga, gb, ya, o):
    return jax.nn.sigmoid(ga) * ya + jax.nn.sigmoid(gb) * o


def _rope(x, cos, sin):
    return x * cos + pltpu.roll(x, LANE // 2, 1) * sin


def _rope_t(dy, cos, sin):
    return dy * cos + pltpu.roll(dy * sin, LANE // 2, 1)


def _mla_pre(cq, ckv, kpe, cos, sin, qw, kvw):
    return _rms(cq, qw), _rms(ckv, kvw), _rope(kpe, cos, sin)


def _mla_pre_bwd(cq, ckv, dcqn, dckvn, dkr, cos, sin, qw, kvw):
    _, pull = jax.vjp(lambda a, b, c, d: (_rms(a, c), _rms(b, d)), cq, ckv, qw, kvw)
    dcq, dckv, dqw, dkvw = pull((dcqn, dckvn))
    return dcq, dckv, _rope_t(dkr, cos, sin), dqw, dkvw


WEIGHTS = ['meta_tokens', 'ffn1_norm', 'ffn1_w_gate', 'ffn1_w_up', 'ffn1_w_down', 'mix_norm', 'w_in', 'tm_mu', 'w0',
           'w_up', 'a0', 'a_up', 'g_up', 'k_k', 'k_a', 'r_k', 'gn_w', 'gn_b', 'q_norm', 'w_uq', 'kv_norm', 'w_ukv',
           'w_out', 'ffn2_norm', 'ffn2_w_gate', 'ffn2_w_up', 'ffn2_w_down', 'final_norm']
SHARD_AXIS = {'meta_tokens': 1, 'ffn1_w_gate': 1, 'ffn1_w_up': 1, 'ffn1_w_down': 0, 'w_in': 1, 'w_up': 1, 'a_up': 1,
              'g_up': 1, 'w_uq': 1, 'w_ukv': 1, 'w_out': 0, 'ffn2_w_gate': 1, 'ffn2_w_up': 1, 'ffn2_w_down': 0}
GATHERED = [n for n in WEIGHTS if n in SHARD_AXIS and n != 'meta_tokens']
FFN_IN = ('ffn1_w_gate', 'ffn1_w_up', 'ffn2_w_gate', 'ffn2_w_up')
PART_B = ['ffn2_w_gate', 'ffn2_w_up', 'ffn2_w_down', 'w_out']
PART_A1 = ['ffn1_w_gate', 'ffn1_w_up', 'ffn1_w_down']
PART_A2 = [n for n in GATHERED if n not in PART_B and n not in PART_A1]
SMALL = [n for n in WEIGHTS if n not in SHARD_AXIS]


def _to2d(a):
    if a.ndim == 1:
        return a.reshape(1, -1)
    if a.ndim == 3:
        return a.reshape(a.shape[0] * a.shape[1], a.shape[2]) if a.shape[0] == 1 and a.shape[1] > 64 else a.reshape(1, -1)
    return a


def _unpack(flat, shapes):
    out, off = [], 0
    for shp in shapes:
        n = shp[0] * shp[1]
        out.append(flat[off:off + n].reshape(shp))
        off += n
    return out


def _adamw(w, g, m, v):
    m = ADAM_B1 * m + (1.0 - ADAM_B1) * g
    v = ADAM_B2 * v + (1.0 - ADAM_B2) * jnp.square(g)
    m_hat = m / (1.0 - ADAM_B1 ** ADAM_STEP)
    v_hat = v / (1.0 - ADAM_B2 ** ADAM_STEP)
    delta = -ADAM_LR * (m_hat / (jnp.sqrt(v_hat) + ADAM_EPS) + ADAM_WD * w)
    return delta, m, v


def adamw(w, g, m, v, name):
    R, C = w.shape
    if R % 16 == 0 and R > 16:
        tm = _rows_tile(R, C)
    else:
        tm = R
    return rowwise(_adamw, [(w, C, 0), (g, C, 0), (m, C, 0), (v, C, 0)], [], [(C, C, F32)] * 3, tm=tm, name=name)


def _step(a):
    x = a['x']
    Bl, S, D = x.shape
    T = S + N_META
    R, RS = Bl * T, Bl * S
    Hr, Hm = D // RWKV_HEAD, D // LANE
    w2 = {n: _to2d(a[n]) for n in WEIGHTS}
    m2 = {n: _to2d(a['m_' + n]) for n in WEIGHTS}
    v2 = {n: _to2d(a['v_' + n]) for n in WEIGHTS}
    xi, yi, ci = lax.axis_index("x"), lax.axis_index("y"), lax.axis_index("c")
    chip = 2 * xi + yi

    i0 = jnp.zeros((), jnp.int32)
    chip32 = chip.astype(jnp.int32)

    def own_slot(w):
        return lax.dynamic_update_slice(lax.empty((4,) + w.shape, BF16), w.astype(BF16)[None], (chip32, i0, i0))

    stk = dict(zip(PART_A1, ag_weights([own_slot(w2[n]) for n in PART_A1])))

    def unstack(z, axis):
        return z.reshape(4 * z.shape[1], z.shape[2]) if axis == 0 else jnp.concatenate([z[s] for s in range(4)], axis=1)

    full = {n: unstack(stk[n], SHARD_AXIS[n]) for n in PART_A1 if n not in FFN_IN}
    mt = w2['meta_tokens']
    mcols = mt.shape[1]
    mt_z = lax.dynamic_update_slice(jnp.zeros((N_META, D), F32), 0.5 * mt, (jnp.zeros((), jnp.int32), (chip * mcols).astype(jnp.int32)))
    meta_full = allreduce8(mt_z.reshape(-1, LANE), "gather_meta").reshape(N_META, D)

    F = 4 * stk['ffn1_w_gate'].shape[2]
    half = ROPE_DIM // 2
    tmu = w2['tm_mu']
    mu_a = tmu[:, :3 * D]
    zm = lambda n: jnp.zeros((1, n), F32)
    mu_b = jnp.concatenate([tmu[:, 3 * D + 2 * W_LORA:], tmu[:, 3 * D:3 * D + W_LORA], zm(LANE - W_LORA),
                            tmu[:, 3 * D + W_LORA:3 * D + 2 * W_LORA], zm(LANE - A_LORA)], axis=1)
    pos = jnp.arange(T, dtype=F32)
    inv_freq = 1.0 / (ROPE_THETA ** (jnp.arange(0, ROPE_DIM, 2, dtype=F32) / ROPE_DIM))
    ang = pos[:, None] * inv_freq[None, :]
    zt = jnp.zeros((T, half), F32)
    cos_t = jnp.concatenate([jnp.cos(ang), zt, jnp.cos(ang), zt], axis=1)
    sin_t = jnp.concatenate([-jnp.sin(ang), zt, jnp.sin(ang), zt], axis=1)
    cos_q, sin_q = cos_t[N_META:], sin_t[N_META:]

    t_full = _rows_tile(R, D)
    t_512 = _rows_tile(R, 512, T)
    t_128 = _rows_tile(R, LANE, T)
    tq_full = _rows_tile(RS, D)
    tq_128 = _rows_tile(RS, LANE, S)

    def rows3(z):
        return z.reshape(Bl, T, z.shape[-1])

    def real_rows(z):
        return rows3(z)[:, N_META:].reshape(RS, z.shape[-1])

    def pad_meta(z):
        z3 = z.reshape(Bl, S, z.shape[-1])
        return jnp.concatenate([jnp.zeros((Bl, N_META, z.shape[-1]), z.dtype), z3], axis=1).reshape(R, z.shape[-1])

    def shift_down(z):
        z3 = rows3(z)
        return jnp.concatenate([jnp.zeros((Bl, 1, z.shape[-1]), z.dtype), z3[:, :-1]], axis=1).reshape(R, z.shape[-1])

    def shift_up(z):
        z3 = rows3(z)
        return jnp.concatenate([z3[:, 1:], jnp.zeros((Bl, 1, z.shape[-1]), z.dtype)], axis=1).reshape(R, z.shape[-1])

    def ffn_fwd(h, nw, wg, wu, wd, tag, ag_bufs=()):
        n = rowwise(_rms, [(h, D, 0)], [(nw, D, 0)], [(D, D, BF16)], tm=t_full, name=tag + "_norm")[0]
        res = mm_epi(n, [wg, wu], "nn", lambda g_, u_: (g_, u_, _silu_mul(g_, u_)), [], [F32, F32, BF16], b_stack=True,
                     carry=("ag", ag_bufs) if ag_bufs else None, name=tag + "_gate_up")
        (gate, up, act), got = res if ag_bufs else (res, [])
        out = mm(act, wd, res=h, alpha=0.5, name=tag + "_down")
        return (out, (h, n, gate, up, act), got) if ag_bufs else (out, (h, n, gate, up, act))

    def exch(g, name):
        return add_half(g, swap_halves([g], "swap_halves_" + name)[0], cidx, "add_half_" + name)

    def ffn_bwd(dh2, saved, nw, wg, wu, wd, tag, rs_hs=(), rs_own=()):
        h, n, gate, up, act = saved
        dz = rowwise(lambda d: 0.5 * d, [(dh2, 512, 0)], [], [(D, 512, BF16)], tm=t_512, ncb=D // 512, name=tag + "_dz")[0]
        d_wd = mm(act, dz, "tn", name=tag + "_dwd")
        res = mm_epi(dz, [wd], "nt", lambda da, g_, u_: vjp_fn(_silu_mul, 2)(g_, u_, da), [gate, up], [BF16, BF16],
                     carry=("rs", rs_hs) if rs_hs else None, name=tag + "_dact")
        (dgate, dup), got = res if rs_hs else (res, [])
        own = {}
        if rs_own:
            h_wd = exch(stacked(d_wd, 0), rs_own[2])
            d_wg, p_wd = mm(n, dgate, "tn", out_stack=True, carry=("rs", [h_wd]), name=tag + "_dwg")
            d_wu = mm(n, dup, "tn", out_stack=True, name=tag + "_dwu")
            h_wg = exch(d_wg, rs_own[0])
            dn, p_wg = mm(dgate, wg, "nt", b_stack=True, carry=("rs", [h_wg]), name=tag + "_dn1")
            h_wu = exch(d_wu, rs_own[1])
            dn, p_wu = mm(dup, wu, "nt", res=dn, b_stack=True, carry=("rs", [h_wu]), name=tag + "_dn2")
            own = {rs_own[0]: (p_wg[0], h_wg), rs_own[1]: (p_wu[0], h_wu), rs_own[2]: (p_wd[0], h_wd)}
        else:
            d_wg = mm(n, dgate, "tn", out_stack=True, name=tag + "_dwg")
            d_wu = mm(n, dup, "tn", out_stack=True, name=tag + "_dwu")
            dn = mm(dgate, wg, "nt", b_stack=True, name=tag + "_dn1")
            dn = mm(dup, wu, "nt", res=dn, b_stack=True, name=tag + "_dn2")

        def f(h_, dn_, dh_, nw_):
            dh, dnw = vjp_fn(_rms, 2)(h_, nw_, dn_)
            return dh + dh_, dnw

        dh, d_nw = rowwise(f, [(h, D, 0), (dn, D, 0), (dh2, D, 0)], [(nw, D, 0)], [(D, D, F32)], [(1, D, D)],
                           tm=t_full, name=tag + "_dnorm")
        return (dh, d_nw, d_wg, d_wu, d_wd, got, own) if (rs_hs or rs_own) else (dh, d_nw, d_wg, d_wu, d_wd)

    h0 = jnp.concatenate([jnp.broadcast_to(meta_full[None], (Bl, N_META, D)), x], axis=1).reshape(R, D)
    h1, sv1, got_a2 = ffn_fwd(h0, w2['ffn1_norm'], stk['ffn1_w_gate'], stk['ffn1_w_up'], full['ffn1_w_down'], "ffn1",
                              [own_slot(w2[n]) for n in PART_A2])
    stk.update(zip(PART_A2, got_a2))
    full.update({n: unstack(stk[n], SHARD_AXIS[n]) for n in PART_A2})
    win = full['w_in']
    o = 3 * D
    c_xw, c_xa, c_xg = win[:, o:o + W_LORA], win[:, o + W_LORA:o + 2 * W_LORA], win[:, o + 2 * W_LORA:o + 2 * W_LORA + G_LORA]
    o += 2 * W_LORA + G_LORA
    c_cq, c_ckv, c_kpe = win[:, o:o + Q_LORA], win[:, o + Q_LORA:o + Q_LORA + KV_LORA], win[:, o + Q_LORA + KV_LORA:o + Q_LORA + KV_LORA + ROPE_DIM]
    o += Q_LORA + KV_LORA + ROPE_DIM
    c_ga, c_gb = win[:, o:o + D], win[:, o + D:o + 2 * D]
    zc = lambda n: jnp.zeros((D, n), BF16)
    half = ROPE_DIM // 2
    NP0 = 5 * D + 512 + Q_LORA + KV_LORA + LANE
    NP = -(-NP0 // 512) * 512
    win_p = jnp.concatenate([win[:, :3 * D], c_ga, c_gb, c_xg, c_xw, zc(LANE - W_LORA), c_xa, zc(LANE - A_LORA), c_cq, c_ckv,
                             c_kpe[:, :half], zc(half), c_kpe[:, half:], zc(half), zc(NP - NP0)], axis=1)
    O_GA, O_GB, O_L, O_CQ, O_CKV, O_KPE = 3 * D, 4 * D, 5 * D, 5 * D + 512, 5 * D + 512 + Q_LORA, 5 * D + 512 + Q_LORA + KV_LORA
    zr = lambda n: jnp.zeros((n, D), BF16)
    w_up_p = jnp.concatenate([full['w_up'], zr(LANE - W_LORA)], axis=0)
    a_up_p = jnp.concatenate([full['a_up'], zr(LANE - A_LORA)], axis=0)
    g_up = full['g_up']
    wuq = full['w_uq'].reshape(Q_LORA, Hm, QK_DIM)
    zq = jnp.zeros((Q_LORA, Hm, half), BF16)
    wqn = wuq[:, :, :NOPE_DIM].reshape(Q_LORA, Hm * LANE)
    wqp = jnp.concatenate([wuq[:, :, NOPE_DIM:NOPE_DIM + half], zq, wuq[:, :, NOPE_DIM + half:], zq], axis=2).reshape(Q_LORA, Hm * LANE)
    wukv = full['w_ukv'].reshape(KV_LORA, Hm, NOPE_DIM + V_DIM)
    wkn = wukv[:, :, :NOPE_DIM].reshape(KV_LORA, Hm * LANE)
    wv = wukv[:, :, NOPE_DIM:].reshape(KV_LORA, Hm * LANE)
    u = rowwise(_rms, [(h1, D, 0)], [(w2['mix_norm'], D, 0)], [(D, D, BF16)], tm=t_full, name="mix_norm")[0]
    proj = mm(u, win_p, name="proj")
    tpos = jnp.broadcast_to(pos[:, None], (T, LANE))
    tpos_in = (tpos, LANE, 0, T // t_512, True)
    prev_b = shift_down(proj[:, O_L:O_L + 512])
    nb = D // 512
    r_s, k_s, v_s = [
        rowwise(_shift_rows, [(proj, 512, sec * nb), (proj, 512, sec * nb, None, False, "prev"), tpos_in],
                [(mu_a, 512, sec * nb)], [(D, 512, F32)], tm=t_512, ncb=nb, name="shift_" + tag)[0]
        for sec, tag in enumerate("rkv")]
    sg, txw, xas = rowwise(_lora_act, [(proj, 512, O_L // 512), (prev_b, 512, 0)], [(mu_b, 512, 0)],
                           [(G_LORA, G_LORA, BF16), (LANE, LANE, BF16), (LANE, LANE, BF16)], tm=t_512, name="lora_act")
    lw = mm(txw, w_up_p, name="lora_w")
    la = mm(xas, a_up_p, name="lora_a")
    g = mm(sg, g_up, name="lora_g")
    hb = D // LANE
    par_d = lambda n: (w2[n], LANE, 0)
    decay, kn, bb, k2 = rowwise(_prep, [(k_s, LANE, 0), (lw, LANE, 0), (la, LANE, 0)],
                                [par_d('w0'), par_d('a0'), par_d('k_k'), par_d('k_a')], [(D, LANE, F32)] * 4,
                                tm=t_128, ncb=hb, name="wkv_prep")

    NS = Bl * Hr

    def t_major(z):
        return lax.optimization_barrier(z.reshape(Bl, T, D).transpose(1, 0, 2))

    def b_major(z):
        return lax.optimization_barrier(z).transpose(1, 0, 2).reshape(R, D)

    def to_j(z):
        z = t_major(z).reshape(T, NS, RWKV_HEAD).transpose(0, 2, 1)
        return jnp.broadcast_to(z[:, :, None, :], (T, RWKV_HEAD, 2, NS)).reshape(T, RWKV_HEAD, 2 * NS)

    def to_i(z):
        return t_major(z).reshape(T, NS, 2, RWKV_HEAD // 2).transpose(0, 3, 2, 1).reshape(T, RWKV_HEAD // 2, 2 * NS)

    def from_i(z):
        return b_major(z.reshape(T, RWKV_HEAD // 2, 2, NS).transpose(0, 3, 2, 1).reshape(T, Bl, D))

    def from_j(z):
        return b_major(z.transpose(0, 2, 1).reshape(T, Bl, D))

    jw, jkn, jb, jk, jr, iv = to_j(decay), to_j(kn), to_j(bb), to_j(k2), to_j(r_s), to_i(v_s)
    y_i, sp, sa_i, got_b = wkv_fwd(jw, jkn, jb, jk, jr, iv, [own_slot(w2[n]) for n in PART_B])
    stk.update(zip(PART_B, got_b))
    full.update({n: unstack(stk[n], SHARD_AXIS[n]) for n in PART_B if n not in FFN_IN})
    wout = full['w_out']
    y = from_i(y_i)
    post_rows = [(y, LANE, 0), (r_s, LANE, 0), (k2, LANE, 0), (v_s, LANE, 0), (g, LANE, 0)]
    post_pars = [par_d('gn_w'), par_d('gn_b'), par_d('r_k')]
    ya = rowwise(_post, post_rows, post_pars, [(D, LANE, F32)], tm=t_128, ncb=hb, name="wkv_post")[0]

    nt512 = T // t_512
    mla_rows = [(proj, Q_LORA, O_CQ // Q_LORA), (proj, KV_LORA, O_CKV // KV_LORA), (proj, LANE, O_KPE // LANE)]
    tabs = [(cos_t, LANE, 0, nt512, True), (sin_t, LANE, 0, nt512, True)]
    mla_pars = [(w2['q_norm'], Q_LORA, 0), (w2['kv_norm'], KV_LORA, 0)]
    cqn, ckvn, kpr = rowwise(_mla_pre, mla_rows + tabs, mla_pars,
                             [(Q_LORA, Q_LORA, BF16), (KV_LORA, KV_LORA, BF16), (LANE, LANE, BF16)], tm=t_512, name="mla_pre")
    cqn_r = real_rows(cqn)
    qn = mm(cqn_r, wqn, out_dtype=BF16, name="q_nope")
    qp_raw = mm(cqn_r, wqp, name="q_pe")
    ntq = S // tq_128
    qtabs = [(cos_q, LANE, 0, ntq, True), (sin_q, LANE, 0, ntq, True)]
    qp = rowwise(_rope, [(qp_raw, LANE, 0)] + qtabs, [], [(D, LANE, BF16)], tm=tq_128, ncb=Hm, name="q_rope")[0]
    knope = mm(ckvn, wkn, out_dtype=BF16, name="k_nope")
    vv = mm(ckvn, wv, out_dtype=BF16, name="v_proj")

    def pad_keys(z):
        z3 = rows3(z)
        return jnp.concatenate([z3[:, :N_META], jnp.zeros((Bl, Q_BLOCK - N_META, z.shape[-1]), z.dtype), z3[:, N_META:]], axis=1)

    def unpad_keys(z):
        return jnp.concatenate([z[:, :N_META], z[:, Q_BLOCK:]], axis=1).reshape(R, z.shape[-1])

    qn3, qp3 = qn.reshape(Bl, S, D), qp.reshape(Bl, S, D)
    knp, kpp, vp = pad_keys(knope), pad_keys(kpr), pad_keys(vv)
    o3, _ = attn_fwd(qn3, qp3, knp, kpp, vp)
    o_att = pad_meta(o3.reshape(RS, D))
    mix_rows = [(proj, 512, O_GA // 512), (proj, 512, O_GB // 512), (ya, 512, 0), (o_att, 512, 0)]
    mix = rowwise(_gate_mix, mix_rows, [], [(D, 512, BF16)], tm=t_512, ncb=D // 512, name="gate_mix")[0]
    h2 = mm(mix, wout, res=h1, name="w_out")
    h3, sv2 = ffn_fwd(h2, w2['ffn2_norm'], stk['ffn2_w_gate'], stk['ffn2_w_up'], full['ffn2_w_down'], "ffn2")

    def loss_fb(h_, tgt, fw):
        yv, pull = jax.vjp(_rms, h_, fw)
        e = yv - tgt
        dh, dfw = pull(e * (1.0 / D))
        return dh, jnp.full((1, LANE), 0.5 / D * jnp.sum(e * e), F32), dfw

    dh3r, lossp, g_final = rowwise(loss_fb, [(real_rows(h3), D, 0), (a['loss_target'].reshape(RS, D), D, 0)],
                                   [(w2['final_norm'], D, 0)], [(D, D, F32)], [(1, LANE, LANE), (1, D, D)],
                                   tm=tq_full, name="loss")
    dh3 = pad_meta(dh3r)

    def stacked(g, axis):
        if g.ndim == 3:
            return g
        if axis == 0:
            return g.reshape(4, g.shape[0] // 4, g.shape[1])
        return g.reshape(g.shape[0], 4, g.shape[1] // 4).transpose(1, 0, 2)

    cidx = ci.reshape(1).astype(jnp.int32)
    gr = {'final_norm': g_final}
    dh2, gr['ffn2_norm'], gr['ffn2_w_gate'], gr['ffn2_w_up'], gr['ffn2_w_down'] = ffn_bwd(
        dh3, sv2, w2['ffn2_norm'], stk['ffn2_w_gate'], stk['ffn2_w_up'], full['ffn2_w_down'], "ffn2")
    dh2b = rowwise(lambda d: d, [(dh2, 512, 0)], [], [(D, 512, BF16)], tm=t_512, ncb=D // 512, name="dh2_cast")[0]
    gr['w_out'] = mm(mix, dh2b, "tn", name="d_wout")
    dmix = mm(dh2b, wout, "nt", name="d_mix")
    dga, dgb, dya, do = rowwise(vjp_fn(_gate_mix, 4), mix_rows + [(dmix, 512, 0)], [],
                                [(D, 512, BF16), (D, 512, BF16), (D, 512, F32), (D, 512, BF16)], tm=t_512, ncb=D // 512,
                                name="d_gate_mix")
    gs_b = [stacked(gr[n], SHARD_AXIS[n]) for n in PART_B]
    hs_b = [add_half(g_, r_, cidx, "add_half_" + n) for n, g_, r_ in zip(PART_B, gs_b, swap_halves(gs_b, "swap_halves_b"))]
    (dqn, dqp, dknp, dvp, dkpp), ps_b = attn_bwd(qn3, qp3, knp, kpp, vp, real_rows(do).reshape(Bl, S, D), hs_b)
    dqn2 = dqn.reshape(RS, D)
    dqp_raw = rowwise(_rope_t, [(dqp.reshape(RS, D), LANE, 0)] + qtabs, [], [(D, LANE, BF16)], tm=tq_128, ncb=Hm,
                      name="d_q_rope")[0]
    d_wqn = mm(cqn_r, dqn2, "tn", name="d_wqn")
    d_wqp = mm(cqn_r, dqp_raw, "tn", name="d_wqp")
    dcqn = mm(dqn2, wqn, "nt", name="d_cqn1")
    dcqn = pad_meta(mm(dqp_raw, wqp, "nt", res=dcqn, name="d_cqn2"))
    dkn2, dv2, dkp2 = unpad_keys(dknp), unpad_keys(dvp), unpad_keys(dkpp)
    d_wkn = mm(ckvn, dkn2, "tn", name="d_wkn")
    d_wv = mm(ckvn, dv2, "tn", name="d_wv")
    dckvn = mm(dkn2, wkn, "nt", name="d_ckvn1")
    dckvn = mm(dv2, wv, "nt", res=dckvn, name="d_ckvn2")
    dcq, dckv, dkpe, gr['q_norm'], gr['kv_norm'] = rowwise(
        _mla_pre_bwd, mla_rows[:2] + [(dcqn, Q_LORA, 0), (dckvn, KV_LORA, 0), (dkp2, LANE, 0)] + tabs, mla_pars,
        [(Q_LORA, Q_LORA, BF16), (KV_LORA, KV_LORA, BF16), (LANE, LANE, BF16)], [(1, Q_LORA, Q_LORA), (1, KV_LORA, KV_LORA)],
        tm=t_512, name="d_mla_pre")

    def post_bwd(y_, r_, k2_, v_, g_, dya_, gnw, gnb, rk):
        return vjp_fn(_post, 8)(y_, r_, k2_, v_, g_, gnw, gnb, rk, dya_)

    dy, dr_b, dk2_b, dv_b, dg, gr['gn_w'], gr['gn_b'], gr['r_k'] = rowwise(
        post_bwd, post_rows + [(dya, LANE, 0)], post_pars,
        [(D, LANE, F32)] * 4 + [(D, LANE, BF16)], [(1, D, LANE)] * 3, tm=t_128, ncb=hb, name="d_wkv_post")
    jdw, jdkn, jdb, jdk, jdr, idv = wkv_bwd(jw, jkn, jb, jk, jr, iv, to_i(dy), sp, sa_i)
    ddecay, dkn_w, db_w, dk2_w, dr_w, dv_w = from_j(jdw), from_j(jdkn), from_j(jdb), from_j(jdk), from_j(jdr), from_i(idv)

    def prep_bwd(k_, lw_, la_, dd, dkn_, db_, dk2a, dk2b, w0, a0, kkw, kaw):
        return vjp_fn(_prep, 7)(k_, lw_, la_, w0, a0, kkw, kaw, dd, dkn_, db_, dk2a + dk2b)

    dk_s, dlw, dla, gr['w0'], gr['a0'], gr['k_k'], gr['k_a'] = rowwise(
        prep_bwd, [(k_s, LANE, 0), (lw, LANE, 0), (la, LANE, 0), (ddecay, LANE, 0), (dkn_w, LANE, 0), (db_w, LANE, 0),
                   (dk2_w, LANE, 0), (dk2_b, LANE, 0)],
        [par_d('w0'), par_d('a0'), par_d('k_k'), par_d('k_a')], [(D, LANE, F32), (D, LANE, BF16), (D, LANE, BF16)],
        [(1, D, LANE)] * 4, tm=t_128, ncb=hb, name="d_wkv_prep")
    d_wup = mm(txw, dlw, "tn", name="d_wup")
    dtxw = mm(dlw, w_up_p, "nt", name="d_txw")
    d_aup = mm(xas, dla, "tn", name="d_aup")
    dxa = mm(dla, a_up_p, "nt", name="d_xa")
    gr['g_up'] = mm(sg, dg, "tn", name="d_gup")
    dsg = mm(dg, g_up, "nt", name="d_sg")

    def lora_bwd(p_, prev_, dsg_, dt_, dxa_, mu_):
        return vjp_fn(_lora_act, 3)(p_, prev_, mu_, dsg_, dt_, dxa_)

    dpb, dprevb, dmu_b = rowwise(lora_bwd, [(proj, 512, O_L // 512), (prev_b, 512, 0), (dsg, G_LORA, 0), (dtxw, LANE, 0),
                                            (dxa, LANE, 0)], [(mu_b, 512, 0)], [(512, 512, F32)] * 2, [(1, 512, 512)],
                                 tm=t_512, name="d_lora_act")

    def shift_back(sec, cts, tag):
        nb = D // 512

        def f(p_, above, tp, *rest):
            cs, mu_ = rest[:-1], rest[-1]
            ct = cs[0] if len(cts) == 1 else cs[0] + cs[2]
            below = cs[1] if len(cts) == 1 else cs[1] + cs[3]
            return _shift_rows_bwd(float(T - 1), p_, above, tp, mu_, ct, below)

        ct_ins = [z for c in cts for z in ((c, 512, 0), (c, 512, 0, None, False, "next"))]
        return rowwise(f, [(proj, 512, sec * nb), (proj, 512, sec * nb, None, False, "prev"), tpos_in] + ct_ins,
                       [(mu_a, 512, sec * nb)], [(D, 512, BF16)], [(1, D, 512)], tm=t_512, ncb=nb, name="d_shift_" + tag)

    dsec_r, dmu_r = shift_back(0, [dr_w, dr_b], "r")
    dsec_k, dmu_k = shift_back(1, [dk_s], "k")
    dsec_v, dmu_v = shift_back(2, [dv_w, dv_b], "v")

    def add_cast(p_, q_):
        return p_ + q_

    def dsec(dp_, dprev_, tag):
        C = dp_.shape[1]
        return rowwise(add_cast, [(dp_, 512, 0), (shift_up(dprev_), 512, 0)], [], [(C, 512, BF16)], tm=t_512, ncb=C // 512,
                       name="d_sec_" + tag)[0]

    zpad = jnp.zeros((R, NP - NP0), BF16)
    dproj = jnp.concatenate([dsec_r, dsec_k, dsec_v, dga, dgb,
                             dsec(dpb, dprevb, "l"), dcq, dckv, dkpe, zpad], axis=1)
    d_win_p = mm(u, dproj, "tn", name="d_win")
    o1 = 3 * D + 2 * W_LORA + G_LORA
    o2 = o1 + Q_LORA + KV_LORA
    segs = [(0, 3 * D, 0), (3 * D, W_LORA, O_L + G_LORA), (3 * D + W_LORA, A_LORA, O_L + G_LORA + LANE),
            (3 * D + 2 * W_LORA, G_LORA, O_L), (o1, Q_LORA, O_CQ), (o1 + Q_LORA, KV_LORA, O_CKV), (o2, half, O_KPE),
            (o2 + half, half, O_KPE + 2 * half), (o2 + ROPE_DIM, 2 * D, O_GA)]
    cs_in = w2['w_in'].shape[1]

    def shard_cols(s):
        pieces = []
        for a0_, n_, p0_ in segs:
            x0, x1 = max(a0_, s * cs_in), min(a0_ + n_, (s + 1) * cs_in)
            if x0 < x1:
                pieces.append(d_win_p[:, p0_ + x0 - a0_:p0_ + x1 - a0_])
        return jnp.concatenate(pieces, axis=1)

    gr['w_in'] = jnp.stack([shard_cols(s) for s in range(4)])
    hs_win = exch(gr['w_in'], "w_in")
    du, ps_win = mm(dproj, win_p, "nt", carry=("rs", [hs_win]), name="d_u")

    def norm_bwd(h_, dn_, dh_, nw_):
        dh, dnw = vjp_fn(_rms, 2)(h_, nw_, dn_)
        return dh + dh_, dnw

    dh1, gr['mix_norm'] = rowwise(norm_bwd, [(h1, D, 0), (du, D, 0), (dh2, D, 0)], [(w2['mix_norm'], D, 0)],
                                  [(D, D, F32)], [(1, D, D)], tm=t_full, name="d_mix_norm")
    gr['tm_mu'] = jnp.concatenate([dmu_r, dmu_k, dmu_v, dmu_b[:, G_LORA:G_LORA + W_LORA],
                                   dmu_b[:, G_LORA + LANE:G_LORA + LANE + A_LORA], dmu_b[:, :G_LORA]], axis=1)
    gr['w_up'], gr['a_up'] = d_wup[:W_LORA], d_aup[:A_LORA]
    dq3n, dq3p = d_wqn.reshape(Q_LORA, Hm, LANE), d_wqp.reshape(Q_LORA, Hm, LANE)
    gr['w_uq'] = jnp.concatenate([dq3n, dq3p[:, :, :half], dq3p[:, :, 2 * half:3 * half]], axis=2).reshape(Q_LORA, Hm * QK_DIM)
    gr['w_ukv'] = jnp.concatenate([d_wkn.reshape(KV_LORA, Hm, LANE), d_wv.reshape(KV_LORA, Hm, LANE)], axis=2).reshape(
        KV_LORA, Hm * (NOPE_DIM + V_DIM))
    rest_a2 = PART_A2[1:]
    gs_a2 = [stacked(gr[n], SHARD_AXIS[n]) for n in rest_a2]
    hs_a2 = [add_half(g_, r_, cidx, "add_half_" + n) for n, g_, r_ in zip(rest_a2, gs_a2, swap_halves(gs_a2, "swap_halves_a2"))]
    dh0, gr['ffn1_norm'], gr['ffn1_w_gate'], gr['ffn1_w_up'], gr['ffn1_w_down'], ps_a2, own_a1 = ffn_bwd(
        dh1, sv1, w2['ffn1_norm'], stk['ffn1_w_gate'], stk['ffn1_w_up'], full['ffn1_w_down'], "ffn1", hs_a2, PART_A1)
    ps_a2, hs_a2 = ps_win + ps_a2, [hs_win] + hs_a2
    dh0_3 = rows3(dh0)
    grad_x = dh0_3[:, N_META:]
    gr['meta_tokens'] = jnp.sum(dh0_3[:, :N_META], axis=0)

    ps_a1, hs_a1 = [own_a1[n][0] for n in PART_A1], [own_a1[n][1] for n in PART_A1]
    sc = (chip32.reshape(1), cidx)
    ts = [sum_chips(p, h, sc, "sum_chips_" + n)
          for n, p, h in zip(PART_A1 + PART_A2 + PART_B, ps_a1 + ps_a2 + ps_b, hs_a1 + hs_a2 + hs_b)]
    g_shard = {n: z.reshape(w2[n].shape) for n, z in zip(PART_A1 + PART_A2 + PART_B, share_sibling(ts))}
    small = jnp.concatenate([gr[n].reshape(-1) for n in SMALL] + [gr['meta_tokens'].reshape(-1), lossp.reshape(-1)])
    ns = small.shape[0]
    nsp = -(-ns // (8 * LANE)) * 8 * LANE
    small_sum = allreduce8(jnp.pad(small, (0, nsp - ns)).reshape(-1, LANE), "allreduce_small").reshape(-1)
    g_small = dict(zip(SMALL + ['meta_full'], _unpack(small_sum, [w2[n].shape for n in SMALL] + [(N_META, D)])))
    g_shard['meta_tokens'] = lax.dynamic_slice(
        g_small['meta_full'], (jnp.zeros((), jnp.int32), (chip * mcols).astype(jnp.int32)), (N_META, mcols))
    loss = small_sum[ns - LANE]

    grads, deltas, new_m, new_v = [], [], [], []
    for n in WEIGHTS:
        gw = g_shard[n] if n in g_shard else g_small[n]
        d_, m_, v_ = adamw(w2[n], gw, m2[n], v2[n], "adamw_" + n)
        shp = a[n].shape
        grads.append(gw.reshape(shp))
        deltas.append(d_.reshape(shp))
        new_m.append(m_.reshape(shp))
        new_v.append(v_.reshape(shp))
    return (loss, grad_x, *grads, *deltas, *new_m, *new_v)


def kernel(x, meta_tokens, ffn1_norm, ffn1_w_gate, ffn1_w_up, ffn1_w_down, mix_norm, w_in, tm_mu, w0, w_up, a0, a_up, g_up, k_k, k_a, r_k, gn_w, gn_b, q_norm, w_uq, kv_norm, w_ukv, w_out, ffn2_norm, ffn2_w_gate, ffn2_w_up, ffn2_w_down, final_norm, loss_target, m_meta_tokens, m_ffn1_norm, m_ffn1_w_gate, m_ffn1_w_up, m_ffn1_w_down, m_mix_norm, m_w_in, m_tm_mu, m_w0, m_w_up, m_a0, m_a_up, m_g_up, m_k_k, m_k_a, m_r_k, m_gn_w, m_gn_b, m_q_norm, m_w_uq, m_kv_norm, m_w_ukv, m_w_out, m_ffn2_norm, m_ffn2_w_gate, m_ffn2_w_up, m_ffn2_w_down, m_final_norm, v_meta_tokens, v_ffn1_norm, v_ffn1_w_gate, v_ffn1_w_up, v_ffn1_w_down, v_mix_norm, v_w_in, v_tm_mu, v_w0, v_w_up, v_a0, v_a_up, v_g_up, v_k_k, v_k_a, v_r_k, v_gn_w, v_gn_b, v_q_norm, v_w_uq, v_kv_norm, v_w_ukv, v_w_out, v_ffn2_norm, v_ffn2_w_gate, v_ffn2_w_up, v_ffn2_w_down, v_final_norm):
    return _step(dict(locals()))
```
